```python
import jax, jax.numpy as jnp
from jax import lax
import numpy as np

D_MODEL = 1024
BATCH = 8
SEQ = 2048
DEPTH = 1
DEC_BATCH = 128
DEC_SEQ = 8
PAST_LEN = 16384
PAGE_SIZE = 128

RW_HEADS = 8
RW_HEAD = 64
RW_WIDTH = RW_HEADS * RW_HEAD
RW_W_RANK = 64
RW_A_RANK = 64
RW_G_RANK = 128
RW_GN_EPS = 64e-5
GLA_HEADS = 4
GLA_DK = 64
GLA_DV = 128
GLA_KW = GLA_HEADS * GLA_DK
GLA_VW = GLA_HEADS * GLA_DV
GLA_GATE_RANK = 16
GLA_GATE_TAU = 16.0
GLA_CHUNK = 16
RW_SHIFT_COLS = 3 * RW_WIDTH + RW_W_RANK + RW_A_RANK + RW_G_RANK
GLA_COLS = 2 * GLA_KW + GLA_VW + GLA_GATE_RANK + GLA_VW
MERGE_COLS = 2 * D_MODEL
IN_COLS = RW_SHIFT_COLS + GLA_COLS + MERGE_COLS
N_EXPERTS = 256
TOP_K = 8
N_GROUPS = 8
TOPK_GROUPS = 4
EXPERT_FF = 256
SHARED_FF = 256
ROUTED_SCALE = 2.5
MOE_BLOCK = 64
NORM_EPS = 1e-6

kernel_name = 'hybrid_rwkv7_gla_moe_adaln_step'


def _offsets(sizes):
    out, acc = [], 0
    for s in sizes[:-1]:
        acc += s
        out.append(acc)
    return out


def _rmsnorm(x, g):
    xf = x.astype(jnp.float32)
    y = xf * lax.rsqrt(jnp.mean(xf * xf, axis=-1, keepdims=True) + NORM_EPS)
    return (y * g.astype(jnp.float32)).astype(x.dtype)


def _rwkv7_scan(r, w, k, v, a, b, s0):
    def step(s, inp):
        r_t, w_t, k_t, v_t, a_t, b_t = inp
        sa = jnp.einsum('bhij,bhj->bhi', s, a_t)
        s = s * w_t[:, :, None, :] + sa[..., None] * b_t[:, :, None, :] + v_t[..., None] * k_t[:, :, None, :]
        y = jnp.einsum('bhij,bhj->bhi', s, r_t)
        return s, y
    xs = tuple(jnp.moveaxis(t, 1, 0) for t in (r, w, k, v, a, b))
    s, ys = lax.scan(step, s0, xs)
    return jnp.moveaxis(ys, 0, 1), s


def _gla_chunked(q, k, v, la, s0):
    bn, L, H, dk = q.shape
    dv = v.shape[-1]
    C = min(GLA_CHUNK, L)
    n = -(-L // C)
    pad = n * C - L
    if pad:
        pw = ((0, 0), (0, pad), (0, 0), (0, 0))
        q, k, v, la = (jnp.pad(t, pw) for t in (q, k, v, la))
    ch = lambda t: t.reshape(bn, n, C, H, t.shape[-1])
    q, k, v, la = ch(q), ch(k), ch(v), ch(la)
    bcum = jnp.cumsum(la, axis=2)
    b_last = bcum[:, :, -1:]
    qe = q * jnp.exp(bcum)
    ke = k * jnp.exp(-bcum)
    kd = k * jnp.exp(b_last - bcum)
    causal = jnp.tril(jnp.ones((C, C), dtype=bool))
    att = jnp.where(causal, jnp.einsum('bnthd,bnshd->bnhts', qe, ke), 0.0)
    o_intra = jnp.einsum('bnhts,bnshv->bnthv', att, v)
    dec = jnp.exp(b_last[:, :, 0])

    def step(s, inp):
        qe_c, kd_c, v_c, dec_c = inp
        o = jnp.einsum('bthd,bhdv->bthv', qe_c, s)
        s = s * dec_c[..., None] + jnp.einsum('bshd,bshv->bhdv', kd_c, v_c)
        return s, o
    s, o_inter = lax.scan(step, s0, (jnp.moveaxis(qe, 1, 0), jnp.moveaxis(kd, 1, 0),
                                     jnp.moveaxis(v, 1, 0), jnp.moveaxis(dec, 1, 0)))
    o = o_intra + jnp.moveaxis(o_inter, 0, 1)
    return o.reshape(bn, n * C, H, dv)[:, :L], s


def _mixer(h, s_rw, s_sh, s_gla, w_in, mu_shift, rw_w0, rw_w_up, rw_a0, rw_a_up, rw_g_up, rw_k_k, rw_k_a,
           rw_r_k, rw_gn_g, rw_gn_b, gla_a_up, gla_a_bias, gla_norm_g, w_pa, w_pb, w_out):
    f32 = jnp.float32
    bn, L, _ = h.shape
    proj = h @ w_in
    pa = proj[..., :RW_SHIFT_COLS]
    pb = proj[..., RW_SHIFT_COLS:RW_SHIFT_COLS + GLA_COLS]
    pg = proj[..., RW_SHIFT_COLS + GLA_COLS:]

    prev = jnp.concatenate([s_sh[:, None, :].astype(pa.dtype), pa[:, :-1]], axis=1)
    xs = pa + (prev - pa) * mu_shift
    new_sh = pa[:, -1]
    r, k, v, xw, xa, xg = jnp.split(xs, _offsets([RW_WIDTH] * 3 + [RW_W_RANK, RW_A_RANK, RW_G_RANK]), axis=-1)
    rh = lambda t: t.reshape(bn, L, RW_HEADS, RW_HEAD)
    w_log = -jax.nn.softplus(-(rw_w0 + jnp.tanh(xw) @ rw_w_up).astype(f32)) - 0.5
    decay = jnp.exp(-jnp.exp(w_log))
    a = jax.nn.sigmoid((rw_a0 + xa @ rw_a_up).astype(f32))
    g = jax.nn.sigmoid(xg) @ rw_g_up
    kf = k.astype(f32)
    kk = rh(kf * rw_k_k.astype(f32))
    kk = kk * lax.rsqrt(jnp.maximum(jnp.sum(kk * kk, axis=-1, keepdims=True), 1e-24))
    k2 = rh(kf * (1.0 + (a - 1.0) * rw_k_a.astype(f32)))
    rf, vf, ah = rh(r.astype(f32)), rh(v.astype(f32)), rh(a)
    y, s_rw_new = _rwkv7_scan(rf, rh(decay), k2, vf, -kk, kk * ah, s_rw.astype(f32))
    mu = jnp.mean(y, axis=-1, keepdims=True)
    var = jnp.mean(jnp.square(y - mu), axis=-1, keepdims=True)
    y = ((y - mu) * lax.rsqrt(var + RW_GN_EPS)).reshape(bn, L, RW_WIDTH) * rw_gn_g.astype(f32) + rw_gn_b.astype(f32)
    bonus = jnp.sum(rf * k2 * rw_r_k.astype(f32), axis=-1, keepdims=True) * vf
    o_a = (y + bonus.reshape(bn, L, RW_WIDTH)).astype(h.dtype) * g

    q, kb, vb, xal, gate = jnp.split(pb, _offsets([GLA_KW, GLA_KW, GLA_VW, GLA_GATE_RANK, GLA_VW]), axis=-1)
    gh = lambda t, d: t.reshape(bn, L, GLA_HEADS, d)
    la = jax.nn.log_sigmoid((xal @ gla_a_up + gla_a_bias).astype(f32)) / GLA_GATE_TAU
    o, s_gla_new = _gla_chunked(gh(q.astype(f32), GLA_DK) * (GLA_DK ** -0.5), gh(kb.astype(f32), GLA_DK),
                                gh(vb.astype(f32), GLA_DV), gh(la, GLA_DK), s_gla.astype(f32))
    o = o * lax.rsqrt(jnp.mean(o * o, axis=-1, keepdims=True) + NORM_EPS) * gla_norm_g.astype(f32)
    o_b = o.reshape(bn, L, GLA_VW).astype(h.dtype) * jax.nn.silu(gate)

    gate_a, gate_b = jnp.split(pg, 2, axis=-1)
    merged = jax.nn.sigmoid(gate_a) * (o_a @ w_pa) + jax.nn.sigmoid(gate_b) * (o_b @ w_pb)
    return merged @ w_out, s_rw_new.astype(h.dtype), new_sh, s_gla_new.astype(h.dtype)


def _moe(h, router_w, router_b, exp_gate, exp_up, exp_down, sh_gate, sh_up, sh_down):
    f32 = jnp.float32
    T, D = h.shape
    scores = jax.nn.sigmoid(h.astype(f32) @ router_w.astype(f32))
    sel = scores + router_b.astype(f32)
    grp = sel.reshape(T, N_GROUPS, N_EXPERTS // N_GROUPS)
    grp_score = jnp.sum(lax.top_k(grp, 2)[0], axis=-1)
    _, gidx = lax.top_k(grp_score, TOPK_GROUPS)
    gmask = jnp.sum(jax.nn.one_hot(gidx, N_GROUPS, dtype=f32), axis=1) > 0
    emask = jnp.repeat(gmask, N_EXPERTS // N_GROUPS, axis=1)
    _, eidx = lax.top_k(jnp.where(emask, sel, -jnp.inf), TOP_K)
    wts = jnp.take_along_axis(scores, eidx, axis=1)
    wts = wts / jnp.sum(wts, axis=-1, keepdims=True) * ROUTED_SCALE

    TK = T * TOP_K
    flat_e = eidx.reshape(-1)
    flat_t = jnp.arange(TK, dtype=jnp.int32) // TOP_K
    flat_w = wts.reshape(-1)
    order = jnp.argsort(flat_e)
    se = flat_e[order]
    counts = jnp.bincount(flat_e, length=N_EXPERTS)
    padded = (counts + MOE_BLOCK - 1) // MOE_BLOCK * MOE_BLOCK
    pad_end = jnp.cumsum(padded)
    pad_start = pad_end - padded
    start = jnp.cumsum(counts) - counts
    dest = pad_start[se] + jnp.arange(TK, dtype=jnp.int32) - start[se]
    n_blocks = (TK + N_EXPERTS * (MOE_BLOCK - 1)) // MOE_BLOCK + 1
    R = n_blocks * MOE_BLOCK
    row_tok = jnp.full((R,), T, dtype=jnp.int32).at[dest].set(flat_t[order])
    row_w = jnp.zeros((R,), f32).at[dest].set(flat_w[order])
    block_e = jnp.minimum(jnp.searchsorted(pad_end, jnp.arange(n_blocks) * MOE_BLOCK, side='right'),
                          N_EXPERTS - 1).astype(jnp.int32)
    h_pad = jnp.concatenate([h, jnp.zeros((1, D), h.dtype)], axis=0)

    def step(acc, inp):
        toks, wr, e = inp
        xb = h_pad[toks]
        ob = (jax.nn.silu(xb @ exp_gate[e]) * (xb @ exp_up[e])) @ exp_down[e]
        return acc.at[toks].add(ob.astype(f32) * wr[:, None]), None
    acc, _ = lax.scan(step, jnp.zeros((T + 1, D), f32),
                      (row_tok.reshape(n_blocks, MOE_BLOCK), row_w.reshape(n_blocks, MOE_BLOCK), block_e))
    shared = (jax.nn.silu(h @ sh_gate) * (h @ sh_up)) @ sh_down
    return (acc[:T] + shared.astype(f32)).astype(h.dtype)


def _layer(x, c, s_rw, s_sh, s_gla, ada_w, ada_b, norm1_g, norm2_g, w_in, mu_shift, rw_w0, rw_w_up, rw_a0,
           rw_a_up, rw_g_up, rw_k_k, rw_k_a, rw_r_k, rw_gn_g, rw_gn_b, gla_a_up, gla_a_bias, gla_norm_g, w_pa,
           w_pb, w_out, router_w, router_b, exp_gate, exp_up, exp_down, sh_gate, sh_up, sh_down):
    bn, L, _ = x.shape
    mod = (jax.nn.silu(c) @ ada_w + ada_b).reshape(bn, 6, D_MODEL)[:, :, None, :]
    h = _rmsnorm(x, norm1_g) * (1.0 + mod[:, 1]) + mod[:, 0]
    mix, s_rw, s_sh, s_gla = _mixer(h, s_rw, s_sh, s_gla, w_in, mu_shift, rw_w0, rw_w_up, rw_a0, rw_a_up,
                                    rw_g_up, rw_k_k, rw_k_a, rw_r_k, rw_gn_g, rw_gn_b, gla_a_up, gla_a_bias,
                                    gla_norm_g, w_pa, w_pb, w_out)
    x = x + mod[:, 2] * mix
    h = _rmsnorm(x, norm2_g) * (1.0 + mod[:, 4]) + mod[:, 3]
    ff = _moe(h.reshape(bn * L, D_MODEL), router_w, router_b, exp_gate, exp_up, exp_down,
              sh_gate, sh_up, sh_down).reshape(bn, L, D_MODEL)
    x = x + mod[:, 5] * ff
    return x, s_rw, s_sh, s_gla


def setup_inputs(seed: int = 0) -> dict:
    key = jax.random.key(seed)
    ks = iter(jax.random.split(key, 64))
    f32 = jnp.float32
    nrm = lambda shape, scale: jax.random.normal(next(ks), shape, f32) * scale
    uni = lambda shape, lo, hi: jax.random.uniform(next(ks), shape, f32, lo, hi)
    L_ = DEPTH
    D = D_MODEL
    return {
        'x_prompt': nrm((BATCH, SEQ, D), 1.0),
        'x_sample': nrm((DEC_BATCH, DEC_SEQ, D), 1.0),
        'c_prompt': nrm((BATCH, D), 1.0),
        'c_sample': nrm((DEC_BATCH, D), 1.0),
        'state_rwkv': nrm((L_, DEC_BATCH, RW_HEADS, RW_HEAD, RW_HEAD), 0.3),
        'state_shift': nrm((L_, DEC_BATCH, RW_SHIFT_COLS), 1.0),
        'state_gla': nrm((L_, DEC_BATCH, GLA_HEADS, GLA_DK, GLA_DV), 0.3),
        'ada_w': nrm((L_, D, 6 * D), 0.5 * D ** -0.5),
        'ada_b': nrm((L_, 6 * D), 0.02),
        'norm1_g': 1.0 + nrm((L_, D), 0.02),
        'norm2_g': 1.0 + nrm((L_, D), 0.02),
        'w_in': nrm((L_, D, IN_COLS), D ** -0.5),
        'mu_shift': uni((L_, RW_SHIFT_COLS), 0.0, 1.0),
        'rw_w0': uni((L_, RW_WIDTH), -5.0, -1.0),
        'rw_w_up': nrm((L_, RW_W_RANK, RW_WIDTH), 0.1 * RW_W_RANK ** -0.5),
        'rw_a0': nrm((L_, RW_WIDTH), 0.1),
        'rw_a_up': nrm((L_, RW_A_RANK, RW_WIDTH), 0.5 * RW_A_RANK ** -0.5),
        'rw_g_up': nrm((L_, RW_G_RANK, RW_WIDTH), RW_G_RANK ** -0.5),
        'rw_k_k': 0.85 + nrm((L_, RW_WIDTH), 0.02),
        'rw_k_a': 1.0 + nrm((L_, RW_WIDTH), 0.02),
        'rw_r_k': nrm((L_, RW_HEADS, RW_HEAD), 0.1),
        'rw_gn_g': 1.0 + nrm((L_, RW_WIDTH), 0.02),
        'rw_gn_b': nrm((L_, RW_WIDTH), 0.02),
        'gla_a_up': nrm((L_, GLA_GATE_RANK, GLA_KW), GLA_GATE_RANK ** -0.5),
        'gla_a_bias': 1.0 + nrm((L_, GLA_KW), 0.5),
        'gla_norm_g': 1.0 + nrm((L_, GLA_DV), 0.02),
        'w_pa': nrm((L_, RW_WIDTH, D), RW_WIDTH ** -0.5),
        'w_pb': nrm((L_, GLA_VW, D), GLA_VW ** -0.5),
        'w_out': nrm((L_, D, D), D ** -0.5),
        'router_w': nrm((L_, D, N_EXPERTS), D ** -0.5),
        'router_b': nrm((L_, N_EXPERTS), 0.01),
        'exp_gate': nrm((L_, N_EXPERTS, D, EXPERT_FF), D ** -0.5),
        'exp_up': nrm((L_, N_EXPERTS, D, EXPERT_FF), D ** -0.5),
        'exp_down': nrm((L_, N_EXPERTS, EXPERT_FF, D), EXPERT_FF ** -0.5),
        'sh_gate': nrm((L_, D, SHARED_FF), D ** -0.5),
        'sh_up': nrm((L_, D, SHARED_FF), D ** -0.5),
        'sh_down': nrm((L_, SHARED_FF, D), SHARED_FF ** -0.5),
        'final_g': 1.0 + nrm((D,), 0.02),
    }


def reference(x_prompt, x_sample, c_prompt, c_sample, state_rwkv, state_shift, state_gla, ada_w, ada_b,
              norm1_g, norm2_g, w_in, mu_shift, rw_w0, rw_w_up, rw_a0, rw_a_up, rw_g_up, rw_k_k, rw_k_a, rw_r_k,
              rw_gn_g, rw_gn_b, gla_a_up, gla_a_bias, gla_norm_g, w_pa, w_pb, w_out, router_w, router_b,
              exp_gate, exp_up, exp_down, sh_gate, sh_up, sh_down, final_g):
    def run(x, c, s_rw_all, s_sh_all, s_gla_all):
        new_rw, new_sh, new_gla = [], [], []
        for l in range(DEPTH):
            x, s_rw, s_sh, s_gla = _layer(
                x, c, s_rw_all[l], s_sh_all[l], s_gla_all[l], ada_w[l], ada_b[l], norm1_g[l], norm2_g[l],
                w_in[l], mu_shift[l], rw_w0[l], rw_w_up[l], rw_a0[l], rw_a_up[l], rw_g_up[l], rw_k_k[l],
                rw_k_a[l], rw_r_k[l], rw_gn_g[l], rw_gn_b[l], gla_a_up[l], gla_a_bias[l], gla_norm_g[l],
                w_pa[l], w_pb[l], w_out[l], router_w[l], router_b[l], exp_gate[l], exp_up[l], exp_down[l],
                sh_gate[l], sh_up[l], sh_down[l])
            new_rw.append(s_rw)
            new_sh.append(s_sh)
            new_gla.append(s_gla)
        return _rmsnorm(x, final_g), jnp.stack(new_rw), jnp.stack(new_sh), jnp.stack(new_gla)

    bp = x_prompt.shape[0]
    z_rw = jnp.zeros((DEPTH, bp, RW_HEADS, RW_HEAD, RW_HEAD), x_prompt.dtype)
    z_sh = jnp.zeros((DEPTH, bp, RW_SHIFT_COLS), x_prompt.dtype)
    z_gla = jnp.zeros((DEPTH, bp, GLA_HEADS, GLA_DK, GLA_DV), x_prompt.dtype)
    y_prompt, rw_p, sh_p, gla_p = run(x_prompt, c_prompt, z_rw, z_sh, z_gla)
    y_sample, rw_s, sh_s, gla_s = run(x_sample, c_sample, state_rwkv, state_shift, state_gla)
    return (y_prompt, y_sample, rw_p, sh_p, gla_p, rw_s, sh_s, gla_s)
```

```python
import functools

import jax
import jax.numpy as jnp
from jax import lax
from jax.experimental import pallas as pl
from jax.experimental.pallas import tpu as pltpu

F32, BF16, I32 = jnp.float32, jnp.bfloat16, jnp.int32

D_MODEL = 1024
RW_HEADS, RW_HEAD = 8, 64
RW_WIDTH = RW_HEADS * RW_HEAD
RW_W_RANK, RW_A_RANK, RW_G_RANK = 64, 64, 128
RW_GN_EPS = 64e-5
GLA_HEADS, GLA_DK, GLA_DV = 4, 64, 128
GLA_KW, GLA_VW = GLA_HEADS * GLA_DK, GLA_HEADS * GLA_DV
GLA_GATE_RANK = 16
GLA_GATE_TAU = 16.0
GLA_CHUNK = 16
RW_SHIFT_COLS = 3 * RW_WIDTH + RW_W_RANK + RW_A_RANK + RW_G_RANK
N_EXPERTS, TOP_K, N_GROUPS, TOPK_GROUPS = 256, 8, 8, 4
GROUP_SIZE = N_EXPERTS // N_GROUPS
EXPERT_FF = 256
ROUTED_SCALE = 2.5
NORM_EPS = 1e-6

LANES = 128
SUBLANES = 8
CHUNKS = D_MODEL // LANES
UNIT = 64
VMEM_LIMIT = 56 * 1024 * 1024

PA_W, QKV_W, XAL_W, GG_W, MG_W = RW_SHIFT_COLS, 2 * GLA_KW + GLA_VW, LANES, GLA_VW, 2 * D_MODEL
PACK_OFFS = (0, PA_W, PA_W + QKV_W, PA_W + QKV_W + XAL_W, PA_W + QKV_W + XAL_W + GG_W)
PACK_W = PA_W + QKV_W + XAL_W + GG_W + MG_W

TOK_TILE = 256
MOE_BLK = 256
CMB_TILE = 128

_DN = {
    "nn": (((1,), (0,)), ((), ())),
    "nt": (((1,), (1,)), ((), ())),
    "tn": (((0,), (0,)), ((), ())),
}


def _split(x, pieces):
    out, rem = [], x
    for i in range(pieces):
        p = rem.astype(BF16)
        out.append(p)
        if i + 1 < pieces:
            rem = rem - p.astype(F32)
    return out


def _mm(a, b, form="nn", passes=1):
    dn = _DN[form]
    if passes == 6:
        return lax.dot_general(a.astype(F32), b.astype(F32), dn, precision=lax.Precision.HIGHEST,
                               preferred_element_type=F32)
    if passes == 1:
        return lax.dot_general(a.astype(BF16), b.astype(BF16), dn, preferred_element_type=F32)
    ah, al = _split(a, 2)
    bh, bl = _split(b, 2)
    out = lax.dot_general(ah, bh, dn, preferred_element_type=F32)
    out = out + lax.dot_general(ah, bl, dn, preferred_element_type=F32)
    return out + lax.dot_general(al, bh, dn, preferred_element_type=F32)


def _mm01(m01, x, pieces=3):
    m = m01.astype(BF16)
    out = None
    for p in _split(x, pieces):
        t = lax.dot_general(m, p, _DN["nn"], preferred_element_type=F32)
        out = t if out is None else out + t
    return out


def _xmm01(x, m01, pieces=2):
    m = m01.astype(BF16)
    out = None
    for p in _split(x, pieces):
        t = lax.dot_general(p, m, _DN["nn"], preferred_element_type=F32)
        out = t if out is None else out + t
    return out


def _sigmoid(x):
    return 1.0 / (1.0 + jnp.exp(-x))


def _softplus(x):
    return jnp.maximum(x, 0.0) + jnp.log(1.0 + jnp.exp(-jnp.abs(x)))


def _log2(n):
    assert n > 0 and n & (n - 1) == 0, n
    return n.bit_length() - 1


def _cparams(sem, vmem=None):
    return pltpu.CompilerParams(dimension_semantics=sem, vmem_limit_bytes=vmem)


def _mod_body(c_ref, w_ref, b_ref, o_ref):
    c = c_ref[...]
    o_ref[0] = _mm(c * _sigmoid(c), w_ref[...], passes=6) + b_ref[...]


def _mod_call(c_all, ada_w, ada_b):
    bt, d = c_all.shape
    out = pl.pallas_call(
        _mod_body,
        grid=(6,),
        in_specs=[pl.BlockSpec((bt, d), lambda k: (0, 0)),
                  pl.BlockSpec((d, d), lambda k: (0, k)),
                  pl.BlockSpec((1, d), lambda k: (0, k))],
        out_specs=pl.BlockSpec((1, bt, d), lambda k: (k, 0, 0)),
        out_shape=jax.ShapeDtypeStruct((6, bt, d), F32),
        compiler_params=_cparams(("arbitrary",)),
        name="adaln_mod",
    )(c_all, ada_w, ada_b.reshape(1, 6 * d))
    return jnp.transpose(out, (1, 0, 2))


def _inproj_body(x_ref, mod_ref, g_ref, w_ref, pa_ref, qkv_ref, xal_ref, gg_ref, mg_ref):
    bb, ll, d = x_ref.shape
    x = x_ref[...]
    y = x * lax.rsqrt(jnp.mean(x * x, axis=-1, keepdims=True) + NORM_EPS) * g_ref[...]
    h = y * (1.0 + mod_ref[:, 1:2, :]) + mod_ref[:, 0:1, :]
    hb = h.reshape(bb * ll, d).astype(BF16)
    for ref, off in zip((pa_ref, qkv_ref, xal_ref, gg_ref, mg_ref), PACK_OFFS):
        w = ref.shape[-1]
        ref[...] = jnp.dot(hb, w_ref[:, off:off + w], preferred_element_type=F32).reshape(bb, ll, w)


def _tile(bn, seq, tile):
    if seq >= tile:
        assert seq % tile == 0
        return 1, tile
    assert tile % seq == 0 and bn % (tile // seq) == 0
    return tile // seq, seq


def _inproj_call(x, mod, norm_g, w_pack):
    bn, seq, d = x.shape
    bb, ll = _tile(bn, seq, TOK_TILE)
    tok = lambda w: pl.BlockSpec((bb, ll, w), lambda b, l: (b, l, 0))
    widths = (PA_W, QKV_W, XAL_W, GG_W, MG_W)
    return pl.pallas_call(
        _inproj_body,
        grid=(bn // bb, seq // ll),
        in_specs=[tok(d),
                  pl.BlockSpec((bb, 6, d), lambda b, l: (b, 0, 0)),
                  pl.BlockSpec((1, 1, d), lambda b, l: (0, 0, 0)),
                  pl.BlockSpec((d, PACK_W), lambda b, l: (0, 0))],
        out_specs=[tok(w) for w in widths],
        out_shape=[jax.ShapeDtypeStruct((bn, seq, w), F32) for w in widths],
        compiler_params=_cparams(("arbitrary", "arbitrary"), VMEM_LIMIT),
        name="norm_inproj",
    )(x, mod, norm_g.reshape(1, 1, d), w_pack)


def _rwprep_body(pa_ref, sh_ref, mu_ref, w0_ref, wup_ref, a0_ref, aup_ref, gup_ref, kk_ref, ka_ref, rk_ref,
                 bd_ref, r_o, lw_o, k_o, v_o, a_o, b_o, g_o, bon_o, nsh_o, carry):
    bb, ll, wd = pa_ref.shape
    n = bb * ll
    hw = RW_WIDTH

    @pl.when(pl.program_id(1) == 0)
    def _():
        carry[...] = sh_ref[...]

    pa = pa_ref[...]
    rolled = pltpu.roll(pa.reshape(n, wd), 1, 0).reshape(bb, ll, wd)
    tok = lax.broadcasted_iota(I32, (bb, ll, wd), 1)
    prev = jnp.where(tok == 0, carry[...], rolled)
    last = pa_ref[:, ll - 1:ll, :]
    carry[...] = last
    nsh_o[...] = last
    xs = (pa + (prev - pa) * mu_ref[...]).reshape(n, wd)

    r, k, v = xs[:, 0:hw], xs[:, hw:2 * hw], xs[:, 2 * hw:3 * hw]
    xwa = xs[:, 3 * hw:3 * hw + LANES]
    xg = xs[:, 3 * hw + LANES:]
    w_log = -_softplus(-(w0_ref[...] + _mm(jnp.tanh(xwa), wup_ref[...], passes=3))) - 0.5
    lw = -jnp.exp(w_log)
    a = _sigmoid(a0_ref[...] + _mm(xwa, aup_ref[...], passes=3))
    g = _mm(_sigmoid(xg), gup_ref[...])
    bd = bd_ref[...]
    kkv = k * kk_ref[...]
    kkn = kkv * lax.rsqrt(jnp.maximum(_xmm01(kkv * kkv, bd), 1e-24))
    k2 = k * (1.0 + (a - 1.0) * ka_ref[...])
    bonus = _xmm01(r * k2 * rk_ref[...], bd) * v
    for ref, val in ((r_o, r), (lw_o, lw), (k_o, k2), (v_o, v), (a_o, -kkn), (b_o, kkn * a), (g_o, g),
                     (bon_o, bonus)):
        ref[...] = val.reshape(bb, ll, hw)


def _rwprep_call(pa, s_sh, p):
    bn, seq, wd = pa.shape
    bb, ll = _tile(bn, seq, TOK_TILE)
    hw = RW_WIDTH
    tok = lambda w: pl.BlockSpec((bb, ll, w), lambda b, l: (b, l, 0))
    row = lambda w: pl.BlockSpec((bb, 1, w), lambda b, l: (b, 0, 0))
    full = lambda a: pl.BlockSpec(a.shape, lambda b, l: (0,) * a.ndim)
    consts = (p["mu"], p["w0"], p["wup"], p["a0"], p["aup"], p["gup"], p["kk"], p["ka"], p["rk"], p["bd64"])
    outs = pl.pallas_call(
        _rwprep_body,
        grid=(bn // bb, seq // ll),
        in_specs=[tok(wd), row(wd)] + [full(c) for c in consts],
        out_specs=[tok(hw)] * 8 + [row(wd)],
        out_shape=[jax.ShapeDtypeStruct((bn, seq, hw), F32)] * 8 + [jax.ShapeDtypeStruct((bn, 1, wd), F32)],
        scratch_shapes=[pltpu.VMEM((bb, 1, wd), F32)],
        compiler_params=_cparams(("arbitrary", "arbitrary"), VMEM_LIMIT),
        name="rwkv_prep",
    )(pa, s_sh.reshape(bn, 1, wd), *consts)
    return outs


def _unit_masks(n, tl):
    ri = lax.broadcasted_iota(I32, (n, n), 0)
    ci = lax.broadcasted_iota(I32, (n, n), 1)
    same = (ri >> _log2(tl)) == (ci >> _log2(tl))
    return same, same & (ri > ci), same & (ri >= ci)


def _rwscan_body(r_ref, lw_ref, k_ref, v_ref, a_ref, b_ref, s0_ref, y_ref, sn_ref, st, *, nseq, tl, passes):
    n = nseq * tl
    n2 = 2 * n
    mm = functools.partial(_mm, passes=passes)

    @pl.when(pl.program_id(1) == 0)
    def _():
        st[...] = s0_ref[...]

    same, _, incl = _unit_masks(n, tl)
    m_cum = jnp.where(incl, 1.0, 0.0)
    m_seq = jnp.where(same, 1.0, 0.0)
    ri = lax.broadcasted_iota(I32, (n2, n2), 0)
    ci = lax.broadcasted_iota(I32, (n2, n2), 1)
    rt, ct = ri & (n - 1), ci & (n - 1)
    dsame = ((rt >> _log2(tl)) == (ct >> _log2(tl))) & ((ri >> _log2(n)) == (ci >> _log2(n)))
    strict_d = dsame & (rt > ct)
    incl_d = dsame & (rt >= ct)
    eye_d = jnp.where(ri == ci, 1.0, 0.0)
    lane = lax.broadcasted_iota(I32, (1, LANES), 1)
    m0 = jnp.where(lane < RW_HEAD, 1.0, 0.0)
    m1 = 1.0 - m0

    def dup(x):
        return jnp.concatenate([x * m0, x * m1], axis=0)

    def seq_rows(x, q):
        if nseq == 1:
            return x
        return jnp.concatenate([x[q * tl:(q + 1) * tl], x[n + q * tl:n + (q + 1) * tl]], axis=0)

    def unit_rows(parts):
        if nseq == 1:
            return parts[0]
        return jnp.concatenate([p[0:tl] for p in parts] + [p[tl:2 * tl] for p in parts], axis=0)

    for p in range(RW_HEADS // 2):
        sl = slice(p * LANES, (p + 1) * LANES)
        ld = lambda ref: ref[:, :, sl].reshape(n, LANES)
        r, lw, k, v, a, b = ld(r_ref), ld(lw_ref), ld(k_ref), ld(v_ref), ld(a_ref), ld(b_ref)
        cum = _mm01(m_cum, lw)
        tot = _mm01(m_seq, lw)
        e_c, e_n, e_l = jnp.exp(cum), jnp.exp(-cum), jnp.exp(tot - cum)
        at_d = dup(a * jnp.exp(cum - lw))
        rt_d = dup(r * e_c)
        bt_d, kt_d = dup(b * e_n), dup(k * e_n)
        bh_d, kh_d = dup(b * e_l), dup(k * e_l)
        v_d = dup(v)
        aa = mm(jnp.concatenate([at_d, rt_d], axis=0), jnp.concatenate([bt_d, kt_d], axis=0), "nt")
        a_ab = jnp.where(strict_d, aa[0:n2, 0:n2], 0.0)
        a_ak = jnp.where(strict_d, aa[0:n2, n2:], 0.0)
        a_rb = jnp.where(incl_d, aa[n2:, 0:n2], 0.0)
        a_rk = jnp.where(incl_d, aa[n2:, n2:], 0.0)
        tinv, nk = eye_d + a_ab, a_ab
        for _ in range(_log2(tl) - 1):
            nk = mm(nk, nk)
            tinv = tinv + mm(tinv, nk)
        zy = mm(jnp.concatenate([a_ak, a_rk], axis=0), v_d)
        wu = mm(tinv, jnp.concatenate([at_d, zy[0:n2]], axis=1))
        w_d, u0_d, y0_d = wu[:, 0:LANES], wu[:, LANES:], zy[n2:]
        u_parts, ys_parts = [], []
        for q in range(nseq):
            s = st[q, p]
            xs = mm(jnp.concatenate([seq_rows(w_d, q), seq_rows(rt_d, q)], axis=0), s, "nt")
            u_q = xs[0:2 * tl] + seq_rows(u0_d, q)
            u_parts.append(u_q)
            ys_parts.append(xs[2 * tl:])
            g_c = jnp.exp(tot[q * tl:q * tl + 1, :])
            st[q, p] = s * g_c + mm(jnp.concatenate([u_q, seq_rows(v_d, q)], axis=0),
                                    jnp.concatenate([seq_rows(bh_d, q), seq_rows(kh_d, q)], axis=0), "tn")
        y_d = unit_rows(ys_parts) + mm(a_rb, unit_rows(u_parts)) + y0_d
        y_ref[:, :, sl] = (y_d[0:n] + y_d[n:]).reshape(nseq, tl, LANES)

    @pl.when(pl.program_id(1) == pl.num_programs(1) - 1)
    def _():
        sn_ref[...] = st[...]


def _unit_shape(bn, seq):
    if seq >= UNIT:
        assert seq % UNIT == 0
        return 1, UNIT
    assert UNIT % seq == 0 and bn % (UNIT // seq) == 0
    return UNIT // seq, seq


def _rwscan_call(r, lw, k2, v, a_s, b_s, s0_bd, passes=3):
    bn, seq, hw = r.shape
    nseq, tl = _unit_shape(bn, seq)
    tok = pl.BlockSpec((nseq, tl, hw), lambda b, c: (b, c, 0))
    stt = pl.BlockSpec((nseq, RW_HEADS // 2, LANES, LANES), lambda b, c: (b, 0, 0, 0))
    return pl.pallas_call(
        functools.partial(_rwscan_body, nseq=nseq, tl=tl, passes=passes),
        grid=(bn // nseq, seq // tl),
        in_specs=[tok] * 6 + [stt],
        out_specs=[tok, stt],
        out_shape=[jax.ShapeDtypeStruct((bn, seq, hw), F32), jax.ShapeDtypeStruct(s0_bd.shape, F32)],
        scratch_shapes=[pltpu.VMEM((nseq, RW_HEADS // 2, LANES, LANES), F32)],
        compiler_params=_cparams(("arbitrary", "arbitrary"), VMEM_LIMIT),
        name="rwkv_scan",
    )(r, lw, k2, v, a_s, b_s, s0_bd)


def _gla_body(qkv_ref, xal_ref, gate_ref, aup_ref, ab_ref, ng_ref, s0_ref, o_ref, sn_ref, st, *, nseq, tl, cs):
    n = nseq * tl
    n2 = 2 * n
    nsub = tl // cs

    @pl.when(pl.program_id(1) == 0)
    def _():
        st[...] = s0_ref[...]

    same, _, incl = _unit_masks(n, cs)
    m_cum = jnp.where(incl, 1.0, 0.0)
    m_sub = jnp.where(same, 1.0, 0.0)
    ri = lax.broadcasted_iota(I32, (n2, n2), 0)
    ci = lax.broadcasted_iota(I32, (n2, n2), 1)
    rt, ct = ri & (n - 1), ci & (n - 1)
    causal_d = ((rt >> _log2(cs)) == (ct >> _log2(cs))) & ((ri >> _log2(n)) == (ci >> _log2(n))) & (rt >= ct)
    lane = lax.broadcasted_iota(I32, (1, LANES), 1)
    m0 = jnp.where(lane < GLA_DK, 1.0, 0.0)
    m1 = 1.0 - m0
    sr = lax.broadcasted_iota(I32, (2 * GLA_DV, LANES), 0)
    sc = lax.broadcasted_iota(I32, (2 * GLA_DV, LANES), 1)
    st_mask = jnp.where((sr >> _log2(GLA_DV)) == (sc >> _log2(GLA_DK)), 1.0, 0.0)

    def dup(x):
        return jnp.concatenate([x * m0, x * m1], axis=0)

    xal = xal_ref[...].reshape(n, LANES)
    la_all = -_softplus(-(_mm(xal, aup_ref[...], passes=3) + ab_ref[...])) * (1.0 / GLA_GATE_TAU)
    ng = ng_ref[...]
    for p in range(GLA_HEADS // 2):
        ksl = slice(p * LANES, (p + 1) * LANES)
        q = qkv_ref[:, :, ksl].reshape(n, LANES) * (GLA_DK ** -0.5)
        k = qkv_ref[:, :, GLA_KW + p * LANES:GLA_KW + (p + 1) * LANES].reshape(n, LANES)
        voff = 2 * GLA_KW + p * 2 * GLA_DV
        vp = qkv_ref[:, :, voff:voff + 2 * GLA_DV].reshape(n, 2 * GLA_DV)
        la = la_all[:, ksl]
        bc = _mm01(m_cum, la)
        bl = _mm01(m_sub, la)
        qe, ke, kd = q * jnp.exp(bc), k * jnp.exp(-bc), k * jnp.exp(bl - bc)
        att = jnp.where(causal_d, _mm(dup(qe), dup(ke), "nt", passes=1), 0.0)
        v_st = jnp.concatenate([vp[:, 0:GLA_DV], vp[:, GLA_DV:]], axis=0)
        o_st = _mm(att, v_st, passes=1)
        inter0, inter1 = [], []
        for sq in range(nseq):
            s = st[sq, p]
            for j in range(nsub):
                r0 = sq * tl + j * cs
                o_j = _mm(qe[r0:r0 + cs], s, "nt", passes=1)
                inter0.append(o_j[:, 0:GLA_DV])
                inter1.append(o_j[:, GLA_DV:])
                dec = jnp.exp(bl[r0:r0 + 1, :])
                s = s * dec + st_mask * _mm(vp[r0:r0 + cs], kd[r0:r0 + cs], "tn", passes=1)
            st[sq, p] = s
        o_st = o_st + jnp.concatenate(inter0 + inter1, axis=0)
        o_st = o_st * lax.rsqrt(jnp.mean(o_st * o_st, axis=-1, keepdims=True) + NORM_EPS) * ng
        goff = p * 2 * GLA_DV
        gp = gate_ref[:, :, goff:goff + 2 * GLA_DV].reshape(n, 2 * GLA_DV)
        g_st = jnp.concatenate([gp[:, 0:GLA_DV], gp[:, GLA_DV:]], axis=0)
        ob = o_st * (g_st * _sigmoid(g_st))
        o_ref[:, :, goff:goff + GLA_DV] = ob[0:n].reshape(nseq, tl, GLA_DV)
        o_ref[:, :, goff + GLA_DV:goff + 2 * GLA_DV] = ob[n:].reshape(nseq, tl, GLA_DV)

    @pl.when(pl.program_id(1) == pl.num_programs(1) - 1)
    def _():
        sn_ref[...] = st[...]


def _gla_call(qkv, xal, gate, s0_t, p):
    bn, seq, _ = qkv.shape
    nseq, tl = _unit_shape(bn, seq)
    cs = min(GLA_CHUNK, seq)
    assert tl % cs == 0
    tok = lambda w: pl.BlockSpec((nseq, tl, w), lambda b, c: (b, c, 0))
    full = lambda a: pl.BlockSpec(a.shape, lambda b, c: (0,) * a.ndim)
    stt = pl.BlockSpec((nseq, GLA_HEADS // 2, 2 * GLA_DV, LANES), lambda b, c: (b, 0, 0, 0))
    consts = (p["gla_aup"], p["gla_ab"], p["gla_ng"])
    return pl.pallas_call(
        functools.partial(_gla_body, nseq=nseq, tl=tl, cs=cs),
        grid=(bn // nseq, seq // tl),
        in_specs=[tok(QKV_W), tok(XAL_W), tok(GG_W)] + [full(c) for c in consts] + [stt],
        out_specs=[tok(GLA_VW), stt],
        out_shape=[jax.ShapeDtypeStruct((bn, seq, GLA_VW), F32), jax.ShapeDtypeStruct(s0_t.shape, F32)],
        scratch_shapes=[pltpu.VMEM((nseq, GLA_HEADS // 2, 2 * GLA_DV, LANES), F32)],
        compiler_params=_cparams(("arbitrary", "arbitrary"), VMEM_LIMIT),
        name="gla_chunked",
    )(qkv, xal, gate, *consts, s0_t)


def _merge_body(y_ref, g_ref, bon_ref, ob_ref, mg_ref, x_ref, mod_ref, gng_ref, gnb_ref, bd_ref, wpa_ref,
                wpb_ref, wout_ref, n2_ref, rwh_ref, rwl_ref, x1_o, h2_o, lg_o):
    bb, ll, d = x_ref.shape
    n = bb * ll
    hw = RW_WIDTH
    bd = bd_ref[...]
    y = y_ref[...].reshape(n, hw)
    mu = _xmm01(y, bd, pieces=3) * (1.0 / RW_HEAD)
    dv = y - mu
    var = _xmm01(dv * dv, bd) * (1.0 / RW_HEAD)
    yn = dv * lax.rsqrt(var + RW_GN_EPS) * gng_ref[...] + gnb_ref[...]
    o_a = (yn + bon_ref[...].reshape(n, hw)) * g_ref[...].reshape(n, hw)
    o_b = ob_ref[...].reshape(n, GLA_VW)
    mg = mg_ref[...].reshape(n, 2 * d)
    merged = _sigmoid(mg[:, 0:d]) * _mm(o_a, wpa_ref[...]) + _sigmoid(mg[:, d:]) * _mm(o_b, wpb_ref[...])
    mix = _mm(merged, wout_ref[...]).reshape(bb, ll, d)
    x1 = x_ref[...] + mod_ref[:, 2:3, :] * mix
    x1_o[...] = x1
    yn2 = x1 * lax.rsqrt(jnp.mean(x1 * x1, axis=-1, keepdims=True) + NORM_EPS) * n2_ref[...]
    h2 = (yn2 * (1.0 + mod_ref[:, 4:5, :]) + mod_ref[:, 3:4, :]).reshape(n, d)
    hh, hl = _split(h2, 2)
    rwh, rwl = rwh_ref[...], rwl_ref[...]
    lg_o[...] = (jnp.dot(hh, rwh, preferred_element_type=F32) + jnp.dot(hh, rwl, preferred_element_type=F32)
                 + jnp.dot(hl, rwh, preferred_element_type=F32))
    for c in range(CHUNKS):
        h2_o[:, c, :] = h2[:, c * LANES:(c + 1) * LANES]


def _merge_call(y, g, bonus, o_b, mg, x, mod, p):
    bn, seq, d = x.shape
    bb, ll = _tile(bn, seq, TOK_TILE)
    nl = seq // ll
    tn = bn * seq
    tok = lambda w: pl.BlockSpec((bb, ll, w), lambda b, l: (b, l, 0))
    full = lambda a: pl.BlockSpec(a.shape, lambda b, l: (0,) * a.ndim)
    consts = (p["gn_g"], p["gn_b"], p["bd64"], p["w_pa"], p["w_pb"], p["w_out"], p["norm2_g"], p["rw_hi"],
              p["rw_lo"])
    return pl.pallas_call(
        _merge_body,
        grid=(bn // bb, nl),
        in_specs=[tok(RW_WIDTH)] * 3 + [tok(GLA_VW), tok(MG_W), tok(d),
                                        pl.BlockSpec((bb, 6, d), lambda b, l: (b, 0, 0))] + [full(c) for c in consts],
        out_specs=[tok(d),
                   pl.BlockSpec((bb * ll, CHUNKS, LANES), lambda b, l: (b * nl + l, 0, 0)),
                   pl.BlockSpec((bb * ll, N_EXPERTS), lambda b, l: (b * nl + l, 0))],
        out_shape=[jax.ShapeDtypeStruct((bn, seq, d), F32),
                   jax.ShapeDtypeStruct((tn, CHUNKS, LANES), F32),
                   jax.ShapeDtypeStruct((tn, N_EXPERTS), F32)],
        compiler_params=_cparams(("arbitrary", "arbitrary"), VMEM_LIMIT),
        name="merge_outproj_router",
    )(y, g, bonus, o_b, mg, x, mod, *consts)


def _route_body(lg_ref, rb_ref, e_o, rk_o, w_o, cnt_o, carry):
    tm = lg_ref.shape[0]
    ne = N_EXPERTS

    @pl.when(pl.program_id(0) == 0)
    def _():
        carry[...] = jnp.zeros_like(carry)

    neg = -jnp.inf
    scores = _sigmoid(lg_ref[...])
    sel = scores + rb_ref[...]
    lane_i = lax.broadcasted_iota(I32, (tm, ne), 1)
    lane = lane_i.astype(F32)
    grp = (lane_i >> _log2(GROUP_SIZE)).astype(F32)

    def first_max(x):
        m = jnp.max(x, axis=-1, keepdims=True)
        idx = jnp.min(jnp.where(x == m, lane, float(ne)), axis=-1, keepdims=True)
        return m, idx

    gs = jnp.full((tm, ne), neg, F32)
    for gidx in range(N_GROUPS):
        sg = jnp.where(grp == float(gidx), sel, neg)
        m1, i1 = first_max(sg)
        m2 = jnp.max(jnp.where(lane == i1, neg, sg), axis=-1, keepdims=True)
        gs = jnp.where(lane == float(gidx), m1 + m2, gs)
    cur = jnp.full((tm, ne), neg, F32)
    for _ in range(TOPK_GROUPS):
        _, gi = first_max(gs)
        cur = jnp.where(grp == gi, sel, cur)
        gs = jnp.where(lane == gi, neg, gs)

    pm = jnp.zeros((tm, ne), F32)
    eidx, wts = [], []
    for _ in range(TOP_K):
        _, ei = first_max(cur)
        hit = lane == ei
        pm = jnp.where(hit, 1.0, pm)
        eidx.append(ei)
        wts.append(jnp.sum(jnp.where(hit, scores, 0.0), axis=-1, keepdims=True))
        cur = jnp.where(hit, neg, cur)
    wsum = wts[0]
    for w in wts[1:]:
        wsum = wsum + w

    ri = lax.broadcasted_iota(I32, (tm, tm), 0)
    ci = lax.broadcasted_iota(I32, (tm, tm), 1)
    below = jnp.where(ri > ci, 1.0, 0.0)
    rank = _mm(below, pm, passes=1) + carry[...]
    carry[...] = carry[...] + jnp.sum(pm, axis=0, keepdims=True)
    cnt_o[...] = carry[...]

    ol = lax.broadcasted_iota(I32, (tm, LANES), 1)
    e_out = jnp.zeros((tm, LANES), I32)
    r_out = jnp.zeros((tm, LANES), I32)
    w_out = jnp.zeros((tm, LANES), F32)
    for kk in range(TOP_K):
        rk = jnp.sum(jnp.where(lane == eidx[kk], rank, 0.0), axis=-1, keepdims=True)
        e_out = jnp.where(ol == kk, eidx[kk].astype(I32), e_out)
        r_out = jnp.where(ol == kk, rk.astype(I32), r_out)
        w_out = jnp.where(ol == kk, wts[kk] / wsum * ROUTED_SCALE, w_out)
    e_o[...] = e_out
    rk_o[...] = r_out
    w_o[...] = w_out


def _route_call(logits, router_b):
    tn, ne = logits.shape
    tm = TOK_TILE
    assert tn % tm == 0
    tok = lambda w: pl.BlockSpec((tm, w), lambda i: (i, 0))
    one = pl.BlockSpec((1, ne), lambda i: (0, 0))
    return pl.pallas_call(
        _route_body,
        grid=(tn // tm,),
        in_specs=[tok(ne), one],
        out_specs=[tok(LANES), tok(LANES), tok(LANES), one],
        out_shape=[jax.ShapeDtypeStruct((tn, LANES), I32), jax.ShapeDtypeStruct((tn, LANES), I32),
                   jax.ShapeDtypeStruct((tn, LANES), F32), jax.ShapeDtypeStruct((1, ne), F32)],
        scratch_shapes=[pltpu.VMEM((1, ne), F32)],
        compiler_params=_cparams(("arbitrary",)),
        name="moe_route",
    )(logits, router_b.reshape(1, ne))


def _dispatch_body(dest_ref, h2_hbm, xs_in, xs_hbm, sem, *, tm):
    del xs_in
    base = pl.program_id(0) * tm

    def row_copy(m, kk):
        return pltpu.make_async_copy(h2_hbm.at[base + m], xs_hbm.at[dest_ref[0, 0, m * TOP_K + kk]], sem)

    def issue(m, carry):
        for kk in range(TOP_K):
            row_copy(m, kk).start()
        return carry

    lax.fori_loop(0, tm, issue, 0)

    def drain(m, carry):
        for kk in range(TOP_K):
            row_copy(m, kk).wait()
        return carry

    lax.fori_loop(0, tm, drain, 0)


def _dispatch_call(dest, h2s, n_rows):
    tn = h2s.shape[0]
    tm = TOK_TILE
    xs0 = jnp.zeros((n_rows, CHUNKS, LANES), F32)
    return pl.pallas_call(
        functools.partial(_dispatch_body, tm=tm),
        grid=(tn // tm,),
        in_specs=[pl.BlockSpec((1, 1, tm * TOP_K), lambda i: (i, 0, 0), memory_space=pltpu.SMEM),
                  pl.BlockSpec(memory_space=pl.ANY),
                  pl.BlockSpec(memory_space=pl.ANY)],
        out_specs=pl.BlockSpec(memory_space=pl.ANY),
        out_shape=jax.ShapeDtypeStruct(xs0.shape, F32),
        scratch_shapes=[pltpu.SemaphoreType.DMA],
        input_output_aliases={2: 0},
        compiler_params=_cparams(("arbitrary",)),
        name="moe_dispatch",
    )(dest.reshape(tn // tm, 1, tm * TOP_K), h2s, xs0)


def _expert_body(be_ref, nv_ref, xs_ref, wg_ref, wu_ref, wd_ref, ob_ref):
    del be_ref
    i = pl.program_id(0)

    @pl.when(i < nv_ref[0])
    def _():
        x = jnp.concatenate([xs_ref[:, c, :] for c in range(CHUNKS)], axis=1).astype(BF16)
        hg = jnp.dot(x, wg_ref[0].astype(BF16), preferred_element_type=F32)
        hu = jnp.dot(x, wu_ref[0].astype(BF16), preferred_element_type=F32)
        hh = (hg * _sigmoid(hg) * hu).astype(BF16)
        out = jnp.dot(hh, wd_ref[0].astype(BF16), preferred_element_type=F32)
        for c in range(CHUNKS):
            ob_ref[:, c, :] = out[:, c * LANES:(c + 1) * LANES]

    @pl.when(i >= nv_ref[0])
    def _():
        ob_ref[...] = jnp.zeros_like(ob_ref)


def _expert_call(block_e, n_valid, xs, wg, wu, wd):
    n_rows = xs.shape[0]
    nb = n_rows // MOE_BLK
    d, ff = wg.shape[1], wg.shape[2]
    rows = pl.BlockSpec((MOE_BLK, CHUNKS, LANES), lambda i, be, nv: (i, 0, 0))
    grid_spec = pltpu.PrefetchScalarGridSpec(
        num_scalar_prefetch=2,
        grid=(nb,),
        in_specs=[rows,
                  pl.BlockSpec((1, d, ff), lambda i, be, nv: (be[i], 0, 0)),
                  pl.BlockSpec((1, d, ff), lambda i, be, nv: (be[i], 0, 0)),
                  pl.BlockSpec((1, ff, d), lambda i, be, nv: (be[i], 0, 0))],
        out_specs=rows,
    )
    return pl.pallas_call(
        _expert_body,
        grid_spec=grid_spec,
        out_shape=jax.ShapeDtypeStruct(xs.shape, F32),
        compiler_params=_cparams(("arbitrary",), VMEM_LIMIT),
        name="moe_experts",
    )(block_e, n_valid, xs, wg, wu, wd)


def _combine_body(dest_ref, wt_ref, ob_hbm, h2_ref, x1_ref, mod_ref, sg_ref, su_ref, sd_ref, fg_ref, out_ref,
                  gbuf, rbuf, sem, *, tm):
    bb, ll, d = x1_ref.shape

    def row_copy(j):
        return pltpu.make_async_copy(ob_hbm.at[dest_ref[0, 0, j]], gbuf.at[j], sem)

    def issue(m, carry):
        for kk in range(TOP_K):
            row_copy(m * TOP_K + kk).start()
        return carry

    lax.fori_loop(0, tm, issue, 0)

    def drain(m, carry):
        for kk in range(TOP_K):
            row_copy(m * TOP_K + kk).wait()
        return carry

    lax.fori_loop(0, tm, drain, 0)

    def mix(m, carry):
        acc = wt_ref[0, 0, m * TOP_K] * gbuf[m * TOP_K]
        for kk in range(1, TOP_K):
            acc = acc + wt_ref[0, 0, m * TOP_K + kk] * gbuf[m * TOP_K + kk]
        rbuf[m] = acc
        return carry

    lax.fori_loop(0, tm, mix, 0)

    routed = jnp.concatenate([rbuf[:, c, :] for c in range(CHUNKS)], axis=1)
    h2 = jnp.concatenate([h2_ref[:, c, :] for c in range(CHUNKS)], axis=1).astype(BF16)
    hg = jnp.dot(h2, sg_ref[...], preferred_element_type=F32)
    hu = jnp.dot(h2, su_ref[...], preferred_element_type=F32)
    shared = jnp.dot((hg * _sigmoid(hg) * hu).astype(BF16), sd_ref[...], preferred_element_type=F32)
    ff = (routed + shared).reshape(bb, ll, d)
    x2 = x1_ref[...] + mod_ref[:, 5:6, :] * ff
    out_ref[...] = x2 * lax.rsqrt(jnp.mean(x2 * x2, axis=-1, keepdims=True) + NORM_EPS) * fg_ref[...]


def _combine_call(dest, wts, ob, h2s, x1, mod, p):
    bn, seq, d = x1.shape
    tm = CMB_TILE
    bb, ll = _tile(bn, seq, tm)
    nl = seq // ll
    tn = bn * seq
    smem = pl.BlockSpec((1, 1, tm * TOP_K), lambda b, l: (b * nl + l, 0, 0), memory_space=pltpu.SMEM)
    tok = pl.BlockSpec((bb, ll, d), lambda b, l: (b, l, 0))
    full = lambda a: pl.BlockSpec(a.shape, lambda b, l: (0,) * a.ndim)
    consts = (p["sh_gate"], p["sh_up"], p["sh_down"], p["final_g"])
    return pl.pallas_call(
        functools.partial(_combine_body, tm=tm),
        grid=(bn // bb, nl),
        in_specs=[smem, smem, pl.BlockSpec(memory_space=pl.ANY),
                  pl.BlockSpec((tm, CHUNKS, LANES), lambda b, l: (b * nl + l, 0, 0)),
                  tok, pl.BlockSpec((bb, 6, d), lambda b, l: (b, 0, 0))] + [full(c) for c in consts],
        out_specs=tok,
        out_shape=jax.ShapeDtypeStruct((bn, seq, d), F32),
        scratch_shapes=[pltpu.VMEM((tm * TOP_K, CHUNKS, LANES), F32), pltpu.VMEM((tm, CHUNKS, LANES), F32),
                        pltpu.SemaphoreType.DMA],
        compiler_params=_cparams(("arbitrary", "arbitrary"), VMEM_LIMIT),
        name="moe_combine_final",
    )(dest.reshape(tn // tm, 1, tm * TOP_K), wts.reshape(tn // tm, 1, tm * TOP_K), ob, h2s, x1, mod, *consts)


def _rw_state_to_pairs(s):
    bn = s.shape[0]
    s = s.reshape(bn, RW_HEADS // 2, 2, RW_HEAD, RW_HEAD)
    z = jnp.zeros_like(s[:, :, 0])
    return jnp.concatenate([jnp.concatenate([s[:, :, 0], z], axis=-1),
                            jnp.concatenate([z, s[:, :, 1]], axis=-1)], axis=-2)


def _rw_state_from_pairs(sp):
    bn = sp.shape[0]
    s = jnp.stack([sp[:, :, :RW_HEAD, :RW_HEAD], sp[:, :, RW_HEAD:, RW_HEAD:]], axis=2)
    return s.reshape(bn, RW_HEADS, RW_HEAD, RW_HEAD)


def _gla_state_to_pairs(s):
    bn = s.shape[0]
    t = jnp.swapaxes(s, -1, -2).reshape(bn, GLA_HEADS // 2, 2, GLA_DV, GLA_DK)
    z = jnp.zeros_like(t[:, :, 0])
    return jnp.concatenate([jnp.concatenate([t[:, :, 0], z], axis=-1),
                            jnp.concatenate([z, t[:, :, 1]], axis=-1)], axis=-2)


def _gla_state_from_pairs(sp):
    bn = sp.shape[0]
    t = jnp.stack([sp[:, :, :GLA_DV, :GLA_DK], sp[:, :, GLA_DV:, GLA_DK:]], axis=2)
    return jnp.swapaxes(t.reshape(bn, GLA_HEADS, GLA_DV, GLA_DK), -1, -2)


def _layer_params(l, ada_w, ada_b, norm1_g, norm2_g, w_in, mu_shift, rw_w0, rw_w_up, rw_a0, rw_a_up, rw_g_up,
                  rw_k_k, rw_k_a, rw_r_k, rw_gn_g, rw_gn_b, gla_a_up, gla_a_bias, gla_norm_g, w_pa, w_pb, w_out,
                  router_w, router_b, exp_gate, exp_up, exp_down, sh_gate, sh_up, sh_down):
    d = D_MODEL
    wi = w_in[l]
    gla0 = RW_SHIFT_COLS
    xal0 = gla0 + QKV_W
    pad = jnp.zeros((d, XAL_W - GLA_GATE_RANK), F32)
    w_pack = jnp.concatenate([wi[:, :xal0], wi[:, xal0:xal0 + GLA_GATE_RANK], pad,
                              wi[:, xal0 + GLA_GATE_RANK:]], axis=1).astype(BF16)
    zr = jnp.zeros((RW_W_RANK, RW_WIDTH), F32)
    hid = jnp.arange(RW_WIDTH) // RW_HEAD
    row = lambda a: a.reshape(1, -1)
    rw_hi = router_w[l].astype(BF16)
    return dict(
        ada_w=ada_w[l], ada_b=ada_b[l], norm1_g=norm1_g[l], norm2_g=norm2_g[l].reshape(1, 1, d), w_pack=w_pack,
        mu=mu_shift[l].reshape(1, 1, -1), w0=row(rw_w0[l]), wup=jnp.concatenate([rw_w_up[l], zr], axis=0),
        a0=row(rw_a0[l]), aup=jnp.concatenate([zr, rw_a_up[l]], axis=0), gup=rw_g_up[l].astype(BF16),
        kk=row(rw_k_k[l]), ka=row(rw_k_a[l]), rk=row(rw_r_k[l]),
        bd64=(hid[:, None] == hid[None, :]).astype(BF16),
        gn_g=row(rw_gn_g[l]), gn_b=row(rw_gn_b[l]),
        gla_aup=jnp.concatenate([gla_a_up[l], jnp.zeros((XAL_W - GLA_GATE_RANK, GLA_KW), F32)], axis=0),
        gla_ab=row(gla_a_bias[l]), gla_ng=row(gla_norm_g[l]),
        w_pa=w_pa[l].astype(BF16), w_pb=w_pb[l].astype(BF16), w_out=w_out[l].astype(BF16),
        rw_hi=rw_hi, rw_lo=(router_w[l] - rw_hi.astype(F32)).astype(BF16), router_b=router_b[l],
        exp_gate=exp_gate[l], exp_up=exp_up[l], exp_down=exp_down[l],
        sh_gate=sh_gate[l].astype(BF16), sh_up=sh_up[l].astype(BF16), sh_down=sh_down[l].astype(BF16),
    )


def _mixer_group(x, mod, s_rw, s_sh, s_gla, p):
    pa, qkv, xal, gg, mg = _inproj_call(x, mod, p["norm1_g"], p["w_pack"])
    r, lw, k2, v, a_s, b_s, g, bonus, new_sh = _rwprep_call(pa, s_sh, p)
    y, rw_new = _rwscan_call(r, lw, k2, v, a_s, b_s, _rw_state_to_pairs(s_rw))
    o_b, gla_new = _gla_call(qkv, xal, gg, _gla_state_to_pairs(s_gla), p)
    x1, h2s, logits = _merge_call(y, g, bonus, o_b, mg, x, mod, p)
    states = (_rw_state_from_pairs(rw_new), new_sh[:, 0, :], _gla_state_from_pairs(gla_new))
    return x1, h2s, logits, states


def _moe(h2s, logits, p):
    tn = h2s.shape[0]
    eidx, rank, wts, counts = _route_call(logits, p["router_b"])
    eidx, rank, wts = eidx[:, :TOP_K], rank[:, :TOP_K], wts[:, :TOP_K]
    counts = counts[0].astype(I32)
    padded = (counts + MOE_BLK - 1) // MOE_BLK * MOE_BLK
    pad_end = jnp.cumsum(padded)
    dest = (pad_end - padded)[eidx] + rank
    nb = (tn * TOP_K + N_EXPERTS * (MOE_BLK - 1)) // MOE_BLK + 1
    block_e = jnp.minimum(jnp.searchsorted(pad_end, jnp.arange(nb, dtype=I32) * MOE_BLK, side="right"),
                          N_EXPERTS - 1).astype(I32)
    n_valid = (pad_end[-1:] // MOE_BLK).astype(I32)
    xs = _dispatch_call(dest, h2s, nb * MOE_BLK)
    ob = _expert_call(block_e, n_valid, xs, p["exp_gate"], p["exp_up"], p["exp_down"])
    return ob, dest, wts


def kernel(x_prompt, x_sample, c_prompt, c_sample, state_rwkv, state_shift, state_gla, ada_w, ada_b, norm1_g,
           norm2_g, w_in, mu_shift, rw_w0, rw_w_up, rw_a0, rw_a_up, rw_g_up, rw_k_k, rw_k_a, rw_r_k, rw_gn_g,
           rw_gn_b, gla_a_up, gla_a_bias, gla_norm_g, w_pa, w_pb, w_out, router_w, router_b, exp_gate, exp_up,
           exp_down, sh_gate, sh_up, sh_down, final_g):
    depth = ada_w.shape[0]
    bp, bs = x_prompt.shape[0], x_sample.shape[0]
    tp = bp * x_prompt.shape[1]
    xs_g = [x_prompt, x_sample]
    c_all = jnp.concatenate([c_prompt, c_sample], axis=0)
    zeros = lambda shape: jnp.zeros(shape, x_prompt.dtype)
    new_states = [[], []]
    fg = final_g.reshape(1, 1, D_MODEL)
    for l in range(depth):
        p = _layer_params(l, ada_w, ada_b, norm1_g, norm2_g, w_in, mu_shift, rw_w0, rw_w_up, rw_a0, rw_a_up,
                          rw_g_up, rw_k_k, rw_k_a, rw_r_k, rw_gn_g, rw_gn_b, gla_a_up, gla_a_bias, gla_norm_g,
                          w_pa, w_pb, w_out, router_w, router_b, exp_gate, exp_up, exp_down, sh_gate, sh_up,
                          sh_down)
        p["final_g"] = fg
        mod_all = _mod_call(c_all, p["ada_w"], p["ada_b"])
        mods = [mod_all[:bp], mod_all[bp:]]
        states_in = [
            (zeros((bp, RW_HEADS, RW_HEAD, RW_HEAD)), zeros((bp, RW_SHIFT_COLS)),
             zeros((bp, GLA_HEADS, GLA_DK, GLA_DV))),
            (state_rwkv[l], state_shift[l], state_gla[l]),
        ]
        x1s, h2ss, lgs = [], [], []
        for gi in range(2):
            x1, h2s, logits, st = _mixer_group(xs_g[gi], mods[gi], *states_in[gi], p)
            x1s.append(x1)
            h2ss.append(h2s)
            lgs.append(logits)
            new_states[gi].append(st)
        h2_all = jnp.concatenate(h2ss, axis=0)
        ob, dest, wts = _moe(h2_all, jnp.concatenate(lgs, axis=0), p)
        assert depth == 1, "the fused final norm assumes a single layer"
        xs_g = [
            _combine_call(dest[:tp], wts[:tp], ob, h2ss[0], x1s[0], mods[0], p),
            _combine_call(dest[tp:], wts[tp:], ob, h2ss[1], x1s[1], mods[1], p),
        ]
    stack = lambda gi, j: jnp.stack([s[j] for s in new_states[gi]])
    return (xs_g[0], xs_g[1], stack(0, 0), stack(0, 1), stack(0, 2), stack(1, 0), stack(1, 1), stack(1, 2))
```

```python
import functools

import jax
import jax.numpy as jnp
from jax import lax
from jax.experimental import pallas as pl
from jax.experimental.pallas import tpu as pltpu

F32, BF16, I32 = jnp.float32, jnp.bfloat16, jnp.int32

D_MODEL = 1024
RW_HEADS, RW_HEAD = 8, 64
RW_WIDTH = RW_HEADS * RW_HEAD
RW_W_RANK, RW_A_RANK, RW_G_RANK = 64, 64, 128
RW_GN_EPS = 64e-5
GLA_HEADS, GLA_DK, GLA_DV = 4, 64, 128
GLA_KW, GLA_VW = GLA_HEADS * GLA_DK, GLA_HEADS * GLA_DV
GLA_GATE_RANK = 16
GLA_GATE_TAU = 16.0
GLA_CHUNK = 16
RW_SHIFT_COLS = 3 * RW_WIDTH + RW_W_RANK + RW_A_RANK + RW_G_RANK
N_EXPERTS, TOP_K, N_GROUPS, TOPK_GROUPS = 256, 8, 8, 4
GROUP_SIZE = N_EXPERTS // N_GROUPS
EXPERT_FF = 256
ROUTED_SCALE = 2.5
NORM_EPS = 1e-6

LANES = 128
SUBLANES = 8
CHUNKS = D_MODEL // LANES
UNIT = 64
RW_SCAN_PASSES = (1, 1, 1, 1, 1)
VMEM_LIMIT = 56 * 1024 * 1024

PA_W, QKV_W, XAL_W, GG_W, MG_W = RW_SHIFT_COLS, 2 * GLA_KW + GLA_VW, LANES, GLA_VW, 2 * D_MODEL
PACK_OFFS = (0, PA_W, PA_W + QKV_W, PA_W + QKV_W + XAL_W, PA_W + QKV_W + XAL_W + GG_W)
PACK_W = PA_W + QKV_W + XAL_W + GG_W + MG_W

TOK_TILE = 256
MOE_BLK = 256
CMB_TILE = 128

_DN = {
    "nn": (((1,), (0,)), ((), ())),
    "nt": (((1,), (1,)), ((), ())),
    "tn": (((0,), (0,)), ((), ())),
}


def _split(x, pieces):
    out, rem = [], x
    for i in range(pieces):
        p = rem.astype(BF16)
        out.append(p)
        if i + 1 < pieces:
            rem = rem - p.astype(F32)
    return out


def _mm(a, b, form="nn", passes=1):
    dn = _DN[form]
    if passes == 6:
        return lax.dot_general(a.astype(F32), b.astype(F32), dn, precision=lax.Precision.HIGHEST,
                               preferred_element_type=F32)
    if passes == 1:
        return lax.dot_general(a.astype(BF16), b.astype(BF16), dn, preferred_element_type=F32)
    ah, al = _split(a, 2)
    bh, bl = _split(b, 2)
    out = lax.dot_general(ah, bh, dn, preferred_element_type=F32)
    out = out + lax.dot_general(ah, bl, dn, preferred_element_type=F32)
    return out + lax.dot_general(al, bh, dn, preferred_element_type=F32)


def _mm01(m01, x, pieces=3):
    m = m01.astype(BF16)
    out = None
    for p in _split(x, pieces):
        t = lax.dot_general(m, p, _DN["nn"], preferred_element_type=F32)
        out = t if out is None else out + t
    return out


def _xmm01(x, m01, pieces=2):
    m = m01.astype(BF16)
    out = None
    for p in _split(x, pieces):
        t = lax.dot_general(p, m, _DN["nn"], preferred_element_type=F32)
        out = t if out is None else out + t
    return out


def _sigmoid(x):
    return 1.0 / (1.0 + jnp.exp(-x))


def _softplus(x):
    return jnp.maximum(x, 0.0) + jnp.log(1.0 + jnp.exp(-jnp.abs(x)))


def _log2(n):
    assert n > 0 and n & (n - 1) == 0, n
    return n.bit_length() - 1


def _cparams(sem, vmem=None):
    return pltpu.CompilerParams(dimension_semantics=sem, vmem_limit_bytes=vmem)


def _mod_body(c_ref, w_ref, b_ref, o_ref):
    c = c_ref[...]
    o_ref[0] = _mm(c * _sigmoid(c), w_ref[...], passes=6) + b_ref[...]


def _mod_call(c_all, ada_w, ada_b):
    bt, d = c_all.shape
    out = pl.pallas_call(
        _mod_body,
        grid=(6,),
        in_specs=[pl.BlockSpec((bt, d), lambda k: (0, 0)),
                  pl.BlockSpec((d, d), lambda k: (0, k)),
                  pl.BlockSpec((1, d), lambda k: (0, k))],
        out_specs=pl.BlockSpec((1, bt, d), lambda k: (k, 0, 0)),
        out_shape=jax.ShapeDtypeStruct((6, bt, d), F32),
        compiler_params=_cparams(("arbitrary",)),
        name="adaln_mod",
    )(c_all, ada_w, ada_b.reshape(1, 6 * d))
    return jnp.transpose(out, (1, 0, 2))


def _inproj_body(x_ref, mod_ref, g_ref, w_ref, pa_ref, qkv_ref, xal_ref, gg_ref, mg_ref):
    bb, ll, d = x_ref.shape
    x = x_ref[...]
    y = x * lax.rsqrt(jnp.mean(x * x, axis=-1, keepdims=True) + NORM_EPS) * g_ref[...]
    h = y * (1.0 + mod_ref[:, 1:2, :]) + mod_ref[:, 0:1, :]
    hb = h.reshape(bb * ll, d).astype(BF16)
    for ref, off in zip((pa_ref, qkv_ref, xal_ref, gg_ref, mg_ref), PACK_OFFS):
        w = ref.shape[-1]
        ref[...] = jnp.dot(hb, w_ref[:, off:off + w], preferred_element_type=F32).reshape(bb, ll, w)


def _tile(bn, seq, tile):
    if seq >= tile:
        assert seq % tile == 0
        return 1, tile
    assert tile % seq == 0 and bn % (tile // seq) == 0
    return tile // seq, seq


def _inproj_call(x, mod, norm_g, w_pack):
    bn, seq, d = x.shape
    bb, ll = _tile(bn, seq, TOK_TILE)
    tok = lambda w: pl.BlockSpec((bb, ll, w), lambda b, l: (b, l, 0))
    widths = (PA_W, QKV_W, XAL_W, GG_W, MG_W)
    return pl.pallas_call(
        _inproj_body,
        grid=(bn // bb, seq // ll),
        in_specs=[tok(d),
                  pl.BlockSpec((bb, 6, d), lambda b, l: (b, 0, 0)),
                  pl.BlockSpec((1, 1, d), lambda b, l: (0, 0, 0)),
                  pl.BlockSpec((d, PACK_W), lambda b, l: (0, 0))],
        out_specs=[tok(w) for w in widths],
        out_shape=[jax.ShapeDtypeStruct((bn, seq, w), F32) for w in widths],
        compiler_params=_cparams(("arbitrary", "arbitrary"), VMEM_LIMIT),
        name="norm_inproj",
    )(x, mod, norm_g.reshape(1, 1, d), w_pack)


def _rwprep_body(pa_ref, sh_ref, mu_ref, w0_ref, wup_ref, a0_ref, aup_ref, gup_ref, kk_ref, ka_ref, rk_ref,
                 bd_ref, r_o, lw_o, k_o, v_o, a_o, b_o, g_o, bon_o, nsh_o, carry):
    bb, ll, wd = pa_ref.shape
    n = bb * ll
    hw = RW_WIDTH

    @pl.when(pl.program_id(1) == 0)
    def _():
        carry[...] = sh_ref[...]

    pa = pa_ref[...]
    rolled = pltpu.roll(pa.reshape(n, wd), 1, 0).reshape(bb, ll, wd)
    tok = lax.broadcasted_iota(I32, (bb, ll, wd), 1)
    prev = jnp.where(tok == 0, carry[...], rolled)
    last = pa_ref[:, ll - 1:ll, :]
    carry[...] = last
    nsh_o[...] = last
    xs = (pa + (prev - pa) * mu_ref[...]).reshape(n, wd)

    r, k, v = xs[:, 0:hw], xs[:, hw:2 * hw], xs[:, 2 * hw:3 * hw]
    xwa = xs[:, 3 * hw:3 * hw + LANES]
    xg = xs[:, 3 * hw + LANES:]
    w_log = -_softplus(-(w0_ref[...] + _mm(jnp.tanh(xwa), wup_ref[...], passes=3))) - 0.5
    lw = -jnp.exp(w_log)
    a = _sigmoid(a0_ref[...] + _mm(xwa, aup_ref[...], passes=3))
    g = _mm(_sigmoid(xg), gup_ref[...])
    bd = bd_ref[...]
    kkv = k * kk_ref[...]
    kkn = kkv * lax.rsqrt(jnp.maximum(_xmm01(kkv * kkv, bd), 1e-24))
    k2 = k * (1.0 + (a - 1.0) * ka_ref[...])
    bonus = _xmm01(r * k2 * rk_ref[...], bd) * v
    for ref, val in ((r_o, r), (lw_o, lw), (k_o, k2), (v_o, v), (a_o, -kkn), (b_o, kkn * a), (g_o, g),
                     (bon_o, bonus)):
        ref[...] = val.reshape(bb, ll, hw)


def _rwprep_call(pa, s_sh, p):
    bn, seq, wd = pa.shape
    bb, ll = _tile(bn, seq, TOK_TILE)
    hw = RW_WIDTH
    tok = lambda w: pl.BlockSpec((bb, ll, w), lambda b, l: (b, l, 0))
    row = lambda w: pl.BlockSpec((bb, 1, w), lambda b, l: (b, 0, 0))
    full = lambda a: pl.BlockSpec(a.shape, lambda b, l: (0,) * a.ndim)
    consts = (p["mu"], p["w0"], p["wup"], p["a0"], p["aup"], p["gup"], p["kk"], p["ka"], p["rk"], p["bd64"])
    outs = pl.pallas_call(
        _rwprep_body,
        grid=(bn // bb, seq // ll),
        in_specs=[tok(wd), row(wd)] + [full(c) for c in consts],
        out_specs=[tok(hw)] * 8 + [row(wd)],
        out_shape=[jax.ShapeDtypeStruct((bn, seq, hw), F32)] * 8 + [jax.ShapeDtypeStruct((bn, 1, wd), F32)],
        scratch_shapes=[pltpu.VMEM((bb, 1, wd), F32)],
        compiler_params=_cparams(("arbitrary", "arbitrary"), VMEM_LIMIT),
        name="rwkv_prep",
    )(pa, s_sh.reshape(bn, 1, wd), *consts)
    return outs


def _unit_masks(n, tl):
    ri = lax.broadcasted_iota(I32, (n, n), 0)
    ci = lax.broadcasted_iota(I32, (n, n), 1)
    same = (ri >> _log2(tl)) == (ci >> _log2(tl))
    return same, same & (ri > ci), same & (ri >= ci)


def _rwscan_body(r_ref, lw_ref, k_ref, v_ref, a_ref, b_ref, s0_ref, y_ref, sn_ref, st, *, nseq, tl, passes):
    n = nseq * tl
    n2 = 2 * n
    p_aa, p_inv, p_apply, p_state, p_y = passes

    @pl.when(pl.program_id(1) == 0)
    def _():
        st[...] = s0_ref[...]

    same, _, incl = _unit_masks(n, tl)
    m_cum = jnp.where(incl, 1.0, 0.0)
    m_seq = jnp.where(same, 1.0, 0.0)
    ri = lax.broadcasted_iota(I32, (n2, n2), 0)
    ci = lax.broadcasted_iota(I32, (n2, n2), 1)
    rt, ct = ri & (n - 1), ci & (n - 1)
    dsame = ((rt >> _log2(tl)) == (ct >> _log2(tl))) & ((ri >> _log2(n)) == (ci >> _log2(n)))
    strict_d = dsame & (rt > ct)
    incl_d = dsame & (rt >= ct)
    eye_d = jnp.where(ri == ci, 1.0, 0.0)
    lane = lax.broadcasted_iota(I32, (1, LANES), 1)
    m0 = jnp.where(lane < RW_HEAD, 1.0, 0.0)
    m1 = 1.0 - m0

    def dup(x):
        return jnp.concatenate([x * m0, x * m1], axis=0)

    def seq_rows(x, q):
        if nseq == 1:
            return x
        return jnp.concatenate([x[q * tl:(q + 1) * tl], x[n + q * tl:n + (q + 1) * tl]], axis=0)

    def unit_rows(parts):
        if nseq == 1:
            return parts[0]
        return jnp.concatenate([p[0:tl] for p in parts] + [p[tl:2 * tl] for p in parts], axis=0)

    for p in range(RW_HEADS // 2):
        sl = slice(p * LANES, (p + 1) * LANES)
        ld = lambda ref: ref[:, :, sl].reshape(n, LANES)
        r, lw, k, v, a, b = ld(r_ref), ld(lw_ref), ld(k_ref), ld(v_ref), ld(a_ref), ld(b_ref)
        cum = _mm01(m_cum, lw)
        tot = _mm01(m_seq, lw)
        e_c, e_n, e_l = jnp.exp(cum), jnp.exp(-cum), jnp.exp(tot - cum)
        at_d = dup(a * jnp.exp(cum - lw))
        rt_d = dup(r * e_c)
        bt_d, kt_d = dup(b * e_n), dup(k * e_n)
        bh_d, kh_d = dup(b * e_l), dup(k * e_l)
        v_d = dup(v)
        aa = _mm(jnp.concatenate([at_d, rt_d], axis=0), jnp.concatenate([bt_d, kt_d], axis=0), "nt", p_aa)
        a_ab = jnp.where(strict_d, aa[0:n2, 0:n2], 0.0)
        a_ak = jnp.where(strict_d, aa[0:n2, n2:], 0.0)
        a_rb = jnp.where(incl_d, aa[n2:, 0:n2], 0.0)
        a_rk = jnp.where(incl_d, aa[n2:, n2:], 0.0)
        tinv, nk = eye_d + a_ab, a_ab
        for _ in range(_log2(tl) - 1):
            nk = _mm(nk, nk, passes=p_inv)
            tinv = tinv + _mm(tinv, nk, passes=p_inv)
        zy = _mm(jnp.concatenate([a_ak, a_rk], axis=0), v_d, passes=p_apply)
        wu = _mm(tinv, jnp.concatenate([at_d, zy[0:n2]], axis=1), passes=p_apply)
        w_d, u0_d, y0_d = wu[:, 0:LANES], wu[:, LANES:], zy[n2:]
        u_parts, ys_parts = [], []
        for q in range(nseq):
            s = st[q, p]
            xs = _mm(jnp.concatenate([seq_rows(w_d, q), seq_rows(rt_d, q)], axis=0), s, "nt", p_state)
            u_q = xs[0:2 * tl] + seq_rows(u0_d, q)
            u_parts.append(u_q)
            ys_parts.append(xs[2 * tl:])
            g_c = jnp.exp(tot[q * tl:q * tl + 1, :])
            st[q, p] = s * g_c + _mm(jnp.concatenate([u_q, seq_rows(v_d, q)], axis=0),
                                     jnp.concatenate([seq_rows(bh_d, q), seq_rows(kh_d, q)], axis=0), "tn", p_state)
        y_d = unit_rows(ys_parts) + _mm(a_rb, unit_rows(u_parts), passes=p_y) + y0_d
        y_ref[:, :, sl] = (y_d[0:n] + y_d[n:]).reshape(nseq, tl, LANES)

    @pl.when(pl.program_id(1) == pl.num_programs(1) - 1)
    def _():
        sn_ref[...] = st[...]


def _unit_shape(bn, seq):
    if seq >= UNIT:
        assert seq % UNIT == 0
        return 1, UNIT
    assert UNIT % seq == 0 and bn % (UNIT // seq) == 0
    return UNIT // seq, seq


def _rwscan_call(r, lw, k2, v, a_s, b_s, s0_bd, passes=RW_SCAN_PASSES):
    bn, seq, hw = r.shape
    nseq, tl = _unit_shape(bn, seq)
    tok = pl.BlockSpec((nseq, tl, hw), lambda b, c: (b, c, 0))
    stt = pl.BlockSpec((nseq, RW_HEADS // 2, LANES, LANES), lambda b, c: (b, 0, 0, 0))
    return pl.pallas_call(
        functools.partial(_rwscan_body, nseq=nseq, tl=tl, passes=passes),
        grid=(bn // nseq, seq // tl),
        in_specs=[tok] * 6 + [stt],
        out_specs=[tok, stt],
        out_shape=[jax.ShapeDtypeStruct((bn, seq, hw), F32), jax.ShapeDtypeStruct(s0_bd.shape, F32)],
        scratch_shapes=[pltpu.VMEM((nseq, RW_HEADS // 2, LANES, LANES), F32)],
        compiler_params=_cparams(("arbitrary", "arbitrary"), VMEM_LIMIT),
        name="rwkv_scan",
    )(r, lw, k2, v, a_s, b_s, s0_bd)


def _gla_body(qkv_ref, xal_ref, gate_ref, aup_ref, ab_ref, ng_ref, s0_ref, o_ref, sn_ref, st, *, nseq, tl, cs):
    n = nseq * tl
    n2 = 2 * n
    nsub = tl // cs

    @pl.when(pl.program_id(1) == 0)
    def _():
        st[...] = s0_ref[...]

    same, _, incl = _unit_masks(n, cs)
    m_cum = jnp.where(incl, 1.0, 0.0)
    m_sub = jnp.where(same, 1.0, 0.0)
    ri = lax.broadcasted_iota(I32, (n2, n2), 0)
    ci = lax.broadcasted_iota(I32, (n2, n2), 1)
    rt, ct = ri & (n - 1), ci & (n - 1)
    causal_d = ((rt >> _log2(cs)) == (ct >> _log2(cs))) & ((ri >> _log2(n)) == (ci >> _log2(n))) & (rt >= ct)
    lane = lax.broadcasted_iota(I32, (1, LANES), 1)
    m0 = jnp.where(lane < GLA_DK, 1.0, 0.0)
    m1 = 1.0 - m0
    sr = lax.broadcasted_iota(I32, (2 * GLA_DV, LANES), 0)
    sc = lax.broadcasted_iota(I32, (2 * GLA_DV, LANES), 1)
    st_mask = jnp.where((sr >> _log2(GLA_DV)) == (sc >> _log2(GLA_DK)), 1.0, 0.0)

    def dup(x):
        return jnp.concatenate([x * m0, x * m1], axis=0)

    xal = xal_ref[...].reshape(n, LANES)
    la_all = -_softplus(-(_mm(xal, aup_ref[...], passes=3) + ab_ref[...])) * (1.0 / GLA_GATE_TAU)
    ng = ng_ref[...]
    for p in range(GLA_HEADS // 2):
        ksl = slice(p * LANES, (p + 1) * LANES)
        q = qkv_ref[:, :, ksl].reshape(n, LANES) * (GLA_DK ** -0.5)
        k = qkv_ref[:, :, GLA_KW + p * LANES:GLA_KW + (p + 1) * LANES].reshape(n, LANES)
        voff = 2 * GLA_KW + p * 2 * GLA_DV
        vp = qkv_ref[:, :, voff:voff + 2 * GLA_DV].reshape(n, 2 * GLA_DV)
        la = la_all[:, ksl]
        bc = _mm01(m_cum, la)
        bl = _mm01(m_sub, la)
        qe, ke, kd = q * jnp.exp(bc), k * jnp.exp(-bc), k * jnp.exp(bl - bc)
        att = jnp.where(causal_d, _mm(dup(qe), dup(ke), "nt", passes=1), 0.0)
        v_st = jnp.concatenate([vp[:, 0:GLA_DV], vp[:, GLA_DV:]], axis=0)
        o_st = _mm(att, v_st, passes=1)
        inter0, inter1 = [], []
        for sq in range(nseq):
            s = st[sq, p]
            for j in range(nsub):
                r0 = sq * tl + j * cs
                o_j = _mm(qe[r0:r0 + cs], s, "nt", passes=1)
                inter0.append(o_j[:, 0:GLA_DV])
                inter1.append(o_j[:, GLA_DV:])
                dec = jnp.exp(bl[r0:r0 + 1, :])
                s = s * dec + st_mask * _mm(vp[r0:r0 + cs], kd[r0:r0 + cs], "tn", passes=1)
            st[sq, p] = s
        o_st = o_st + jnp.concatenate(inter0 + inter1, axis=0)
        o_st = o_st * lax.rsqrt(jnp.mean(o_st * o_st, axis=-1, keepdims=True) + NORM_EPS) * ng
        goff = p * 2 * GLA_DV
        gp = gate_ref[:, :, goff:goff + 2 * GLA_DV].reshape(n, 2 * GLA_DV)
        g_st = jnp.concatenate([gp[:, 0:GLA_DV], gp[:, GLA_DV:]], axis=0)
        ob = o_st * (g_st * _sigmoid(g_st))
        o_ref[:, :, goff:goff + GLA_DV] = ob[0:n].reshape(nseq, tl, GLA_DV)
        o_ref[:, :, goff + GLA_DV:goff + 2 * GLA_DV] = ob[n:].reshape(nseq, tl, GLA_DV)

    @pl.when(pl.program_id(1) == pl.num_programs(1) - 1)
    def _():
        sn_ref[...] = st[...]


def _gla_call(qkv, xal, gate, s0_t, p):
    bn, seq, _ = qkv.shape
    nseq, tl = _unit_shape(bn, seq)
    cs = min(GLA_CHUNK, seq)
    assert tl % cs == 0
    tok = lambda w: pl.BlockSpec((nseq, tl, w), lambda b, c: (b, c, 0))
    full = lambda a: pl.BlockSpec(a.shape, lambda b, c: (0,) * a.ndim)
    stt = pl.BlockSpec((nseq, GLA_HEADS // 2, 2 * GLA_DV, LANES), lambda b, c: (b, 0, 0, 0))
    consts = (p["gla_aup"], p["gla_ab"], p["gla_ng"])
    return pl.pallas_call(
        functools.partial(_gla_body, nseq=nseq, tl=tl, cs=cs),
        grid=(bn // nseq, seq // tl),
        in_specs=[tok(QKV_W), tok(XAL_W), tok(GG_W)] + [full(c) for c in consts] + [stt],
        out_specs=[tok(GLA_VW), stt],
        out_shape=[jax.ShapeDtypeStruct((bn, seq, GLA_VW), F32), jax.ShapeDtypeStruct(s0_t.shape, F32)],
        scratch_shapes=[pltpu.VMEM((nseq, GLA_HEADS // 2, 2 * GLA_DV, LANES), F32)],
        compiler_params=_cparams(("arbitrary", "arbitrary"), VMEM_LIMIT),
        name="gla_chunked",
    )(qkv, xal, gate, *consts, s0_t)


def _merge_body(y_ref, g_ref, bon_ref, ob_ref, mg_ref, x_ref, mod_ref, gng_ref, gnb_ref, bd_ref, wpa_ref,
                wpb_ref, wout_ref, n2_ref, rwh_ref, rwl_ref, x1_o, h2_o, lg_o):
    bb, ll, d = x_ref.shape
    n = bb * ll
    hw = RW_WIDTH
    bd = bd_ref[...]
    y = y_ref[...].reshape(n, hw)
    mu = _xmm01(y, bd, pieces=3) * (1.0 / RW_HEAD)
    dv = y - mu
    var = _xmm01(dv * dv, bd) * (1.0 / RW_HEAD)
    yn = dv * lax.rsqrt(var + RW_GN_EPS) * gng_ref[...] + gnb_ref[...]
    o_a = (yn + bon_ref[...].reshape(n, hw)) * g_ref[...].reshape(n, hw)
    o_b = ob_ref[...].reshape(n, GLA_VW)
    mg = mg_ref[...].reshape(n, 2 * d)
    merged = _sigmoid(mg[:, 0:d]) * _mm(o_a, wpa_ref[...]) + _sigmoid(mg[:, d:]) * _mm(o_b, wpb_ref[...])
    mix = _mm(merged, wout_ref[...]).reshape(bb, ll, d)
    x1 = x_ref[...] + mod_ref[:, 2:3, :] * mix
    x1_o[...] = x1
    yn2 = x1 * lax.rsqrt(jnp.mean(x1 * x1, axis=-1, keepdims=True) + NORM_EPS) * n2_ref[...]
    h2 = (yn2 * (1.0 + mod_ref[:, 4:5, :]) + mod_ref[:, 3:4, :]).reshape(n, d)
    hh, hl = _split(h2, 2)
    rwh, rwl = rwh_ref[...], rwl_ref[...]
    lg_o[...] = (jnp.dot(hh, rwh, preferred_element_type=F32) + jnp.dot(hh, rwl, preferred_element_type=F32)
                 + jnp.dot(hl, rwh, preferred_element_type=F32))
    for c in range(CHUNKS):
        h2_o[:, c, :] = h2[:, c * LANES:(c + 1) * LANES]


def _merge_call(y, g, bonus, o_b, mg, x, mod, p):
    bn, seq, d = x.shape
    bb, ll = _tile(bn, seq, TOK_TILE)
    nl = seq // ll
    tn = bn * seq
    tok = lambda w: pl.BlockSpec((bb, ll, w), lambda b, l: (b, l, 0))
    full = lambda a: pl.BlockSpec(a.shape, lambda b, l: (0,) * a.ndim)
    consts = (p["gn_g"], p["gn_b"], p["bd64"], p["w_pa"], p["w_pb"], p["w_out"], p["norm2_g"], p["rw_hi"],
              p["rw_lo"])
    return pl.pallas_call(
        _merge_body,
        grid=(bn // bb, nl),
        in_specs=[tok(RW_WIDTH)] * 3 + [tok(GLA_VW), tok(MG_W), tok(d),
                                        pl.BlockSpec((bb, 6, d), lambda b, l: (b, 0, 0))] + [full(c) for c in consts],
        out_specs=[tok(d),
                   pl.BlockSpec((bb * ll, CHUNKS, LANES), lambda b, l: (b * nl + l, 0, 0)),
                   pl.BlockSpec((bb * ll, N_EXPERTS), lambda b, l: (b * nl + l, 0))],
        out_shape=[jax.ShapeDtypeStruct((bn, seq, d), F32),
                   jax.ShapeDtypeStruct((tn, CHUNKS, LANES), F32),
                   jax.ShapeDtypeStruct((tn, N_EXPERTS), F32)],
        compiler_params=_cparams(("arbitrary", "arbitrary"), VMEM_LIMIT),
        name="merge_outproj_router",
    )(y, g, bonus, o_b, mg, x, mod, *consts)


def _route_body(lg_ref, rb_ref, e_o, rk_o, w_o, cnt_o, carry):
    tm = lg_ref.shape[0]
    ne = N_EXPERTS

    @pl.when(pl.program_id(0) == 0)
    def _():
        carry[...] = jnp.zeros_like(carry)

    neg = -jnp.inf
    scores = _sigmoid(lg_ref[...])
    sel = scores + rb_ref[...]
    lane_i = lax.broadcasted_iota(I32, (tm, ne), 1)
    lane = lane_i.astype(F32)
    grp = (lane_i >> _log2(GROUP_SIZE)).astype(F32)

    def first_max(x):
        m = jnp.max(x, axis=-1, keepdims=True)
        idx = jnp.min(jnp.where(x == m, lane, float(ne)), axis=-1, keepdims=True)
        return m, idx

    gs = jnp.full((tm, ne), neg, F32)
    for gidx in range(N_GROUPS):
        sg = jnp.where(grp == float(gidx), sel, neg)
        m1, i1 = first_max(sg)
        m2 = jnp.max(jnp.where(lane == i1, neg, sg), axis=-1, keepdims=True)
        gs = jnp.where(lane == float(gidx), m1 + m2, gs)
    cur = jnp.full((tm, ne), neg, F32)
    for _ in range(TOPK_GROUPS):
        _, gi = first_max(gs)
        cur = jnp.where(grp == gi, sel, cur)
        gs = jnp.where(lane == gi, neg, gs)

    pm = jnp.zeros((tm, ne), F32)
    eidx, wts = [], []
    for _ in range(TOP_K):
        _, ei = first_max(cur)
        hit = lane == ei
        pm = jnp.where(hit, 1.0, pm)
        eidx.append(ei)
        wts.append(jnp.sum(jnp.where(hit, scores, 0.0), axis=-1, keepdims=True))
        cur = jnp.where(hit, neg, cur)
    wsum = wts[0]
    for w in wts[1:]:
        wsum = wsum + w

    ri = lax.broadcasted_iota(I32, (tm, tm), 0)
    ci = lax.broadcasted_iota(I32, (tm, tm), 1)
    below = jnp.where(ri > ci, 1.0, 0.0)
    rank = _mm(below, pm, passes=1) + carry[...]
    carry[...] = carry[...] + jnp.sum(pm, axis=0, keepdims=True)
    cnt_o[...] = carry[...]

    ol = lax.broadcasted_iota(I32, (tm, LANES), 1)
    e_out = jnp.zeros((tm, LANES), I32)
    r_out = jnp.zeros((tm, LANES), I32)
    w_out = jnp.zeros((tm, LANES), F32)
    for kk in range(TOP_K):
        rk = jnp.sum(jnp.where(lane == eidx[kk], rank, 0.0), axis=-1, keepdims=True)
        e_out = jnp.where(ol == kk, eidx[kk].astype(I32), e_out)
        r_out = jnp.where(ol == kk, rk.astype(I32), r_out)
        w_out = jnp.where(ol == kk, wts[kk] / wsum * ROUTED_SCALE, w_out)
    e_o[...] = e_out
    rk_o[...] = r_out
    w_o[...] = w_out


def _route_call(logits, router_b):
    tn, ne = logits.shape
    tm = TOK_TILE
    assert tn % tm == 0
    tok = lambda w: pl.BlockSpec((tm, w), lambda i: (i, 0))
    one = pl.BlockSpec((1, ne), lambda i: (0, 0))
    return pl.pallas_call(
        _route_body,
        grid=(tn // tm,),
        in_specs=[tok(ne), one],
        out_specs=[tok(LANES), tok(LANES), tok(LANES), one],
        out_shape=[jax.ShapeDtypeStruct((tn, LANES), I32), jax.ShapeDtypeStruct((tn, LANES), I32),
                   jax.ShapeDtypeStruct((tn, LANES), F32), jax.ShapeDtypeStruct((1, ne), F32)],
        scratch_shapes=[pltpu.VMEM((1, ne), F32)],
        compiler_params=_cparams(("arbitrary",)),
        name="moe_route",
    )(logits, router_b.reshape(1, ne))


def _dispatch_body(e_ref, rk_ref, ps_ref, h2_ref, xs_in, xs_hbm, sem, *, tm):
    del xs_in

    def issue(m, carry):
        for kk in range(TOP_K):
            j = m * TOP_K + kk
            row = ps_ref[e_ref[0, 0, j]] + rk_ref[0, 0, j]
            pltpu.make_async_copy(h2_ref.at[m], xs_hbm.at[row], sem).start()
        return carry

    lax.fori_loop(0, tm, issue, 0)
    all_rows = xs_hbm.at[pl.ds(0, tm * TOP_K)]
    pltpu.make_async_copy(all_rows, all_rows, sem).wait()


def _assign_specs(tm, index_map):
    blk = pl.BlockSpec((1, 1, tm * TOP_K), index_map, memory_space=pltpu.SMEM)
    return blk, pl.BlockSpec(memory_space=pltpu.SMEM)


def _dispatch_call(eidx, rank, pad_start, h2s, n_rows):
    tn = h2s.shape[0]
    tm = TOK_TILE
    xs0 = jnp.zeros((n_rows, CHUNKS, LANES), F32)
    blk, whole = _assign_specs(tm, lambda i: (i, 0, 0))
    shp = (tn // tm, 1, tm * TOP_K)
    return pl.pallas_call(
        functools.partial(_dispatch_body, tm=tm),
        grid=(tn // tm,),
        in_specs=[blk, blk, whole,
                  pl.BlockSpec((tm, CHUNKS, LANES), lambda i: (i, 0, 0)),
                  pl.BlockSpec(memory_space=pl.ANY)],
        out_specs=pl.BlockSpec(memory_space=pl.ANY),
        out_shape=jax.ShapeDtypeStruct(xs0.shape, F32),
        scratch_shapes=[pltpu.SemaphoreType.DMA],
        input_output_aliases={4: 0},
        compiler_params=_cparams(("arbitrary",)),
        name="moe_dispatch",
    )(eidx.reshape(shp), rank.reshape(shp), pad_start, h2s, xs0)


def _expert_body(be_ref, nv_ref, xs_ref, wg_ref, wu_ref, wd_ref, ob_ref):
    del be_ref
    i = pl.program_id(0)

    @pl.when(i < nv_ref[0])
    def _():
        x = jnp.concatenate([xs_ref[:, c, :] for c in range(CHUNKS)], axis=1).astype(BF16)
        hg = jnp.dot(x, wg_ref[0].astype(BF16), preferred_element_type=F32)
        hu = jnp.dot(x, wu_ref[0].astype(BF16), preferred_element_type=F32)
        hh = (hg * _sigmoid(hg) * hu).astype(BF16)
        out = jnp.dot(hh, wd_ref[0].astype(BF16), preferred_element_type=F32)
        for c in range(CHUNKS):
            ob_ref[:, c, :] = out[:, c * LANES:(c + 1) * LANES]

    @pl.when(i >= nv_ref[0])
    def _():
        ob_ref[...] = jnp.zeros_like(ob_ref)


def _expert_call(block_e, n_valid, xs, wg, wu, wd):
    n_rows = xs.shape[0]
    nb = n_rows // MOE_BLK
    d, ff = wg.shape[1], wg.shape[2]
    rows = pl.BlockSpec((MOE_BLK, CHUNKS, LANES), lambda i, be, nv: (i, 0, 0))
    grid_spec = pltpu.PrefetchScalarGridSpec(
        num_scalar_prefetch=2,
        grid=(nb,),
        in_specs=[rows,
                  pl.BlockSpec((1, d, ff), lambda i, be, nv: (be[i], 0, 0)),
                  pl.BlockSpec((1, d, ff), lambda i, be, nv: (be[i], 0, 0)),
                  pl.BlockSpec((1, ff, d), lambda i, be, nv: (be[i], 0, 0))],
        out_specs=rows,
    )
    return pl.pallas_call(
        _expert_body,
        grid_spec=grid_spec,
        out_shape=jax.ShapeDtypeStruct(xs.shape, F32),
        compiler_params=_cparams(("arbitrary",), VMEM_LIMIT),
        name="moe_experts",
    )(block_e, n_valid, xs, wg, wu, wd)


def _combine_body(e_ref, rk_ref, wt_ref, ps_ref, ob_hbm, h2_ref, x1_ref, mod_ref, sg_ref, su_ref, sd_ref, fg_ref,
                  out_ref, gbuf, rbuf, sem, *, tm):
    bb, ll, d = x1_ref.shape

    def issue(m, carry):
        for kk in range(TOP_K):
            j = m * TOP_K + kk
            row = ps_ref[e_ref[0, 0, j]] + rk_ref[0, 0, j]
            pltpu.make_async_copy(ob_hbm.at[row], gbuf.at[j], sem).start()
        return carry

    lax.fori_loop(0, tm, issue, 0)
    pltpu.make_async_copy(ob_hbm.at[pl.ds(0, tm * TOP_K)], gbuf, sem).wait()

    def mix(m, carry):
        acc = wt_ref[0, 0, m * TOP_K] * gbuf[m * TOP_K]
        for kk in range(1, TOP_K):
            acc = acc + wt_ref[0, 0, m * TOP_K + kk] * gbuf[m * TOP_K + kk]
        rbuf[m] = acc
        return carry

    lax.fori_loop(0, tm, mix, 0)

    routed = jnp.concatenate([rbuf[:, c, :] for c in range(CHUNKS)], axis=1)
    h2 = jnp.concatenate([h2_ref[:, c, :] for c in range(CHUNKS)], axis=1).astype(BF16)
    hg = jnp.dot(h2, sg_ref[...], preferred_element_type=F32)
    hu = jnp.dot(h2, su_ref[...], preferred_element_type=F32)
    shared = jnp.dot((hg * _sigmoid(hg) * hu).astype(BF16), sd_ref[...], preferred_element_type=F32)
    ff = (routed + shared).reshape(bb, ll, d)
    x2 = x1_ref[...] + mod_ref[:, 5:6, :] * ff
    out_ref[...] = x2 * lax.rsqrt(jnp.mean(x2 * x2, axis=-1, keepdims=True) + NORM_EPS) * fg_ref[...]


def _combine_call(eidx, rank, wts, pad_start, ob, h2s, x1, mod, p):
    bn, seq, d = x1.shape
    tm = CMB_TILE
    bb, ll = _tile(bn, seq, tm)
    nl = seq // ll
    tn = bn * seq
    smem, whole = _assign_specs(tm, lambda b, l: (b * nl + l, 0, 0))
    tok = pl.BlockSpec((bb, ll, d), lambda b, l: (b, l, 0))
    full = lambda a: pl.BlockSpec(a.shape, lambda b, l: (0,) * a.ndim)
    consts = (p["sh_gate"], p["sh_up"], p["sh_down"], p["final_g"])
    shp = (tn // tm, 1, tm * TOP_K)
    return pl.pallas_call(
        functools.partial(_combine_body, tm=tm),
        grid=(bn // bb, nl),
        in_specs=[smem, smem, smem, whole, pl.BlockSpec(memory_space=pl.ANY),
                  pl.BlockSpec((tm, CHUNKS, LANES), lambda b, l: (b * nl + l, 0, 0)),
                  tok, pl.BlockSpec((bb, 6, d), lambda b, l: (b, 0, 0))] + [full(c) for c in consts],
        out_specs=tok,
        out_shape=jax.ShapeDtypeStruct((bn, seq, d), F32),
        scratch_shapes=[pltpu.VMEM((tm * TOP_K, CHUNKS, LANES), F32), pltpu.VMEM((tm, CHUNKS, LANES), F32),
                        pltpu.SemaphoreType.DMA],
        compiler_params=_cparams(("arbitrary", "arbitrary"), VMEM_LIMIT),
        name="moe_combine_final",
    )(eidx.reshape(shp), rank.reshape(shp), wts.reshape(shp), pad_start, ob, h2s, x1, mod, *consts)


def _rw_state_to_pairs(s):
    bn = s.shape[0]
    s = s.reshape(bn, RW_HEADS // 2, 2, RW_HEAD, RW_HEAD)
    z = jnp.zeros_like(s[:, :, 0])
    return jnp.concatenate([jnp.concatenate([s[:, :, 0], z], axis=-1),
                            jnp.concatenate([z, s[:, :, 1]], axis=-1)], axis=-2)


def _rw_state_from_pairs(sp):
    bn = sp.shape[0]
    s = jnp.stack([sp[:, :, :RW_HEAD, :RW_HEAD], sp[:, :, RW_HEAD:, RW_HEAD:]], axis=2)
    return s.reshape(bn, RW_HEADS, RW_HEAD, RW_HEAD)


def _gla_state_to_pairs(s):
    bn = s.shape[0]
    t = jnp.swapaxes(s, -1, -2).reshape(bn, GLA_HEADS // 2, 2, GLA_DV, GLA_DK)
    z = jnp.zeros_like(t[:, :, 0])
    return jnp.concatenate([jnp.concatenate([t[:, :, 0], z], axis=-1),
                            jnp.concatenate([z, t[:, :, 1]], axis=-1)], axis=-2)


def _gla_state_from_pairs(sp):
    bn = sp.shape[0]
    t = jnp.stack([sp[:, :, :GLA_DV, :GLA_DK], sp[:, :, GLA_DV:, GLA_DK:]], axis=2)
    return jnp.swapaxes(t.reshape(bn, GLA_HEADS, GLA_DV, GLA_DK), -1, -2)


def _layer_params(l, ada_w, ada_b, norm1_g, norm2_g, w_in, mu_shift, rw_w0, rw_w_up, rw_a0, rw_a_up, rw_g_up,
                  rw_k_k, rw_k_a, rw_r_k, rw_gn_g, rw_gn_b, gla_a_up, gla_a_bias, gla_norm_g, w_pa, w_pb, w_out,
                  router_w, router_b, exp_gate, exp_up, exp_down, sh_gate, sh_up, sh_down):
    d = D_MODEL
    wi = w_in[l]
    gla0 = RW_SHIFT_COLS
    xal0 = gla0 + QKV_W
    pad = jnp.zeros((d, XAL_W - GLA_GATE_RANK), F32)
    w_pack = jnp.concatenate([wi[:, :xal0], wi[:, xal0:xal0 + GLA_GATE_RANK], pad,
                              wi[:, xal0 + GLA_GATE_RANK:]], axis=1).astype(BF16)
    zr = jnp.zeros((RW_W_RANK, RW_WIDTH), F32)
    hid = jnp.arange(RW_WIDTH) // RW_HEAD
    row = lambda a: a.reshape(1, -1)
    rw_hi = router_w[l].astype(BF16)
    return dict(
        ada_w=ada_w[l], ada_b=ada_b[l], norm1_g=norm1_g[l], norm2_g=norm2_g[l].reshape(1, 1, d), w_pack=w_pack,
        mu=mu_shift[l].reshape(1, 1, -1), w0=row(rw_w0[l]), wup=jnp.concatenate([rw_w_up[l], zr], axis=0),
        a0=row(rw_a0[l]), aup=jnp.concatenate([zr, rw_a_up[l]], axis=0), gup=rw_g_up[l].astype(BF16),
        kk=row(rw_k_k[l]), ka=row(rw_k_a[l]), rk=row(rw_r_k[l]),
        bd64=(hid[:, None] == hid[None, :]).astype(BF16),
        gn_g=row(rw_gn_g[l]), gn_b=row(rw_gn_b[l]),
        gla_aup=jnp.concatenate([gla_a_up[l], jnp.zeros((XAL_W - GLA_GATE_RANK, GLA_KW), F32)], axis=0),
        gla_ab=row(gla_a_bias[l]), gla_ng=row(gla_norm_g[l]),
        w_pa=w_pa[l].astype(BF16), w_pb=w_pb[l].astype(BF16), w_out=w_out[l].astype(BF16),
        rw_hi=rw_hi, rw_lo=(router_w[l] - rw_hi.astype(F32)).astype(BF16), router_b=router_b[l],
        exp_gate=exp_gate[l], exp_up=exp_up[l], exp_down=exp_down[l],
        sh_gate=sh_gate[l].astype(BF16), sh_up=sh_up[l].astype(BF16), sh_down=sh_down[l].astype(BF16),
    )


def _mixer_group(x, mod, s_rw, s_sh, s_gla, p):
    pa, qkv, xal, gg, mg = _inproj_call(x, mod, p["norm1_g"], p["w_pack"])
    r, lw, k2, v, a_s, b_s, g, bonus, new_sh = _rwprep_call(pa, s_sh, p)
    y, rw_new = _rwscan_call(r, lw, k2, v, a_s, b_s, _rw_state_to_pairs(s_rw))
    o_b, gla_new = _gla_call(qkv, xal, gg, _gla_state_to_pairs(s_gla), p)
    x1, h2s, logits = _merge_call(y, g, bonus, o_b, mg, x, mod, p)
    states = (_rw_state_from_pairs(rw_new), new_sh[:, 0, :], _gla_state_from_pairs(gla_new))
    return x1, h2s, logits, states


def _moe(h2s, logits, p):
    tn = h2s.shape[0]
    eidx, rank, wts, counts = _route_call(logits, p["router_b"])
    eidx, rank, wts = eidx[:, :TOP_K], rank[:, :TOP_K], wts[:, :TOP_K]
    counts = counts[0].astype(I32)
    padded = (counts + MOE_BLK - 1) // MOE_BLK * MOE_BLK
    pad_end = jnp.cumsum(padded)
    pad_start = (pad_end - padded).astype(I32)
    nb = (tn * TOP_K + N_EXPERTS * (MOE_BLK - 1)) // MOE_BLK + 1
    block_e = jnp.minimum(jnp.searchsorted(pad_end, jnp.arange(nb, dtype=I32) * MOE_BLK, side="right"),
                          N_EXPERTS - 1).astype(I32)
    n_valid = (pad_end[-1:] // MOE_BLK).astype(I32)
    xs = _dispatch_call(eidx, rank, pad_start, h2s, nb * MOE_BLK)
    ob = _expert_call(block_e, n_valid, xs, p["exp_gate"], p["exp_up"], p["exp_down"])
    return ob, (eidx, rank, wts, pad_start)


def kernel(x_prompt, x_sample, c_prompt, c_sample, state_rwkv, state_shift, state_gla, ada_w, ada_b, norm1_g,
           norm2_g, w_in, mu_shift, rw_w0, rw_w_up, rw_a0, rw_a_up, rw_g_up, rw_k_k, rw_k_a, rw_r_k, rw_gn_g,
           rw_gn_b, gla_a_up, gla_a_bias, gla_norm_g, w_pa, w_pb, w_out, router_w, router_b, exp_gate, exp_up,
           exp_down, sh_gate, sh_up, sh_down, final_g):
    depth = ada_w.shape[0]
    bp, bs = x_prompt.shape[0], x_sample.shape[0]
    tp = bp * x_prompt.shape[1]
    xs_g = [x_prompt, x_sample]
    c_all = jnp.concatenate([c_prompt, c_sample], axis=0)
    zeros = lambda shape: jnp.zeros(shape, x_prompt.dtype)
    new_states = [[], []]
    fg = final_g.reshape(1, 1, D_MODEL)
    for l in range(depth):
        p = _layer_params(l, ada_w, ada_b, norm1_g, norm2_g, w_in, mu_shift, rw_w0, rw_w_up, rw_a0, rw_a_up,
                          rw_g_up, rw_k_k, rw_k_a, rw_r_k, rw_gn_g, rw_gn_b, gla_a_up, gla_a_bias, gla_norm_g,
                          w_pa, w_pb, w_out, router_w, router_b, exp_gate, exp_up, exp_down, sh_gate, sh_up,
                          sh_down)
        p["final_g"] = fg
        mod_all = _mod_call(c_all, p["ada_w"], p["ada_b"])
        mods = [mod_all[:bp], mod_all[bp:]]
        states_in = [
            (zeros((bp, RW_HEADS, RW_HEAD, RW_HEAD)), zeros((bp, RW_SHIFT_COLS)),
             zeros((bp, GLA_HEADS, GLA_DK, GLA_DV))),
            (state_rwkv[l], state_shift[l], state_gla[l]),
        ]
        x1s, h2ss, lgs = [], [], []
        for gi in range(2):
            x1, h2s, logits, st = _mixer_group(xs_g[gi], mods[gi], *states_in[gi], p)
            x1s.append(x1)
            h2ss.append(h2s)
            lgs.append(logits)
            new_states[gi].append(st)
        h2_all = jnp.concatenate(h2ss, axis=0)
        ob, (eidx, rank, wts, pad_start) = _moe(h2_all, jnp.concatenate(lgs, axis=0), p)
        assert depth == 1, "the fused final norm assumes a single layer"
        xs_g = [
            _combine_call(eidx[:tp], rank[:tp], wts[:tp], pad_start, ob, h2ss[0], x1s[0], mods[0], p),
            _combine_call(eidx[tp:], rank[tp:], wts[tp:], pad_start, ob, h2ss[1], x1s[1], mods[1], p),
        ]
    stack = lambda gi, j: jnp.stack([s[j] for s in new_states[gi]])
    return (xs_g[0], xs_g[1], stack(0, 0), stack(0, 1), stack(0, 2), stack(1, 0), stack(1, 1), stack(1, 2))
```

```python
import functools

import jax
import jax.numpy as jnp
from jax import lax
from jax.experimental import pallas as pl
from jax.experimental.pallas import tpu as pltpu

F32, BF16, I32 = jnp.float32, jnp.bfloat16, jnp.int32

D_MODEL = 1024
RW_HEADS, RW_HEAD = 8, 64
RW_WIDTH = RW_HEADS * RW_HEAD
RW_W_RANK, RW_A_RANK, RW_G_RANK = 64, 64, 128
RW_GN_EPS = 64e-5
GLA_HEADS, GLA_DK, GLA_DV = 4, 64, 128
GLA_KW, GLA_VW = GLA_HEADS * GLA_DK, GLA_HEADS * GLA_DV
GLA_GATE_RANK = 16
GLA_GATE_TAU = 16.0
GLA_CHUNK = 16
RW_SHIFT_COLS = 3 * RW_WIDTH + RW_W_RANK + RW_A_RANK + RW_G_RANK
N_EXPERTS, TOP_K, N_GROUPS, TOPK_GROUPS = 256, 8, 8, 4
GROUP_SIZE = N_EXPERTS // N_GROUPS
EXPERT_FF = 256
ROUTED_SCALE = 2.5
NORM_EPS = 1e-6

LANES = 128
SUBLANES = 8
CHUNKS = D_MODEL // LANES
UNIT = 64
RW_SCAN_PASSES = (1, 1, 1, 1, 1)
GLA_UNITS_PER_STEP = 4
RW_UNITS_PER_STEP = 4
VMEM_LIMIT = 56 * 1024 * 1024

PA_W, QKV_W, XAL_W, GG_W, MG_W = RW_SHIFT_COLS, 2 * GLA_KW + GLA_VW, LANES, GLA_VW, 2 * D_MODEL
PACK_OFFS = (0, PA_W, PA_W + QKV_W, PA_W + QKV_W + XAL_W, PA_W + QKV_W + XAL_W + GG_W)
PACK_W = PA_W + QKV_W + XAL_W + GG_W + MG_W

TOK_TILE = 256
MOE_BLK = 256
CMB_TILE = 128

_DN = {
    "nn": (((1,), (0,)), ((), ())),
    "nt": (((1,), (1,)), ((), ())),
    "tn": (((0,), (0,)), ((), ())),
}


def _split(x, pieces):
    out, rem = [], x
    for i in range(pieces):
        p = rem.astype(BF16)
        out.append(p)
        if i + 1 < pieces:
            rem = rem - p.astype(F32)
    return out


def _mm(a, b, form="nn", passes=1):
    dn = _DN[form]
    if passes == 6:
        return lax.dot_general(a.astype(F32), b.astype(F32), dn, precision=lax.Precision.HIGHEST,
                               preferred_element_type=F32)
    if passes == 1:
        return lax.dot_general(a.astype(BF16), b.astype(BF16), dn, preferred_element_type=F32)
    ah, al = _split(a, 2)
    bh, bl = _split(b, 2)
    out = lax.dot_general(ah, bh, dn, preferred_element_type=F32)
    out = out + lax.dot_general(ah, bl, dn, preferred_element_type=F32)
    return out + lax.dot_general(al, bh, dn, preferred_element_type=F32)


def _mm01(m01, x, pieces=3):
    m = m01.astype(BF16)
    out = None
    for p in _split(x, pieces):
        t = lax.dot_general(m, p, _DN["nn"], preferred_element_type=F32)
        out = t if out is None else out + t
    return out


def _xmm01(x, m01, pieces=2):
    m = m01.astype(BF16)
    out = None
    for p in _split(x, pieces):
        t = lax.dot_general(p, m, _DN["nn"], preferred_element_type=F32)
        out = t if out is None else out + t
    return out


def _rows_from_slabs(ref, n):
    return jnp.concatenate([ref[pl.ds(c, n, stride=CHUNKS), :] for c in range(CHUNKS)], axis=1)


def _rows_to_slabs(ref, x):
    for c in range(CHUNKS):
        ref[pl.ds(c, x.shape[0], stride=CHUNKS), :] = x[:, c * LANES:(c + 1) * LANES]


def _slab(ref, row):
    return ref.at[pl.ds(pl.multiple_of(row * CHUNKS, CHUNKS), CHUNKS)]


def _sigmoid(x):
    return 1.0 / (1.0 + jnp.exp(-x))


def _softplus(x):
    return jnp.maximum(x, 0.0) + jnp.log(1.0 + jnp.exp(-jnp.abs(x)))


def _log2(n):
    assert n > 0 and n & (n - 1) == 0, n
    return n.bit_length() - 1


def _cparams(sem, vmem=None):
    return pltpu.CompilerParams(dimension_semantics=sem, vmem_limit_bytes=vmem)


def _mod_body(c_ref, w_ref, b_ref, o_ref):
    c = c_ref[...]
    o_ref[0] = _mm(c * _sigmoid(c), w_ref[...], passes=6) + b_ref[...]


def _mod_call(c_all, ada_w, ada_b):
    bt, d = c_all.shape
    out = pl.pallas_call(
        _mod_body,
        grid=(6,),
        in_specs=[pl.BlockSpec((bt, d), lambda k: (0, 0)),
                  pl.BlockSpec((d, d), lambda k: (0, k)),
                  pl.BlockSpec((1, d), lambda k: (0, k))],
        out_specs=pl.BlockSpec((1, bt, d), lambda k: (k, 0, 0)),
        out_shape=jax.ShapeDtypeStruct((6, bt, d), F32),
        compiler_params=_cparams(("arbitrary",)),
        name="adaln_mod",
    )(c_all, ada_w, ada_b.reshape(1, 6 * d))
    return jnp.transpose(out, (1, 0, 2))


def _inproj_body(x_ref, mod_ref, g_ref, w_ref, pa_ref, qkv_ref, xal_ref, gg_ref, mg_ref):
    bb, ll, d = x_ref.shape
    x = x_ref[...]
    y = x * lax.rsqrt(jnp.mean(x * x, axis=-1, keepdims=True) + NORM_EPS) * g_ref[...]
    h = y * (1.0 + mod_ref[:, 1:2, :]) + mod_ref[:, 0:1, :]
    hb = h.reshape(bb * ll, d).astype(BF16)
    for ref, off in zip((pa_ref, qkv_ref, xal_ref, gg_ref, mg_ref), PACK_OFFS):
        w = ref.shape[-1]
        ref[...] = jnp.dot(hb, w_ref[:, off:off + w], preferred_element_type=F32).reshape(bb, ll, w)


def _tile(bn, seq, tile):
    if seq >= tile:
        assert seq % tile == 0
        return 1, tile
    assert tile % seq == 0 and bn % (tile // seq) == 0
    return tile // seq, seq


def _inproj_call(x, mod, norm_g, w_pack):
    bn, seq, d = x.shape
    bb, ll = _tile(bn, seq, TOK_TILE)
    tok = lambda w: pl.BlockSpec((bb, ll, w), lambda b, l: (b, l, 0))
    widths = (PA_W, QKV_W, XAL_W, GG_W, MG_W)
    return pl.pallas_call(
        _inproj_body,
        grid=(bn // bb, seq // ll),
        in_specs=[tok(d),
                  pl.BlockSpec((bb, 6, d), lambda b, l: (b, 0, 0)),
                  pl.BlockSpec((1, 1, d), lambda b, l: (0, 0, 0)),
                  pl.BlockSpec((d, PACK_W), lambda b, l: (0, 0))],
        out_specs=[tok(w) for w in widths],
        out_shape=[jax.ShapeDtypeStruct((bn, seq, w), F32) for w in widths],
        compiler_params=_cparams(("arbitrary", "arbitrary"), VMEM_LIMIT),
        name="norm_inproj",
    )(x, mod, norm_g.reshape(1, 1, d), w_pack)


def _rwprep_body(pa_ref, sh_ref, mu_ref, w0_ref, wup_ref, a0_ref, aup_ref, gup_ref, kk_ref, ka_ref, rk_ref,
                 bd_ref, r_o, lw_o, k_o, v_o, a_o, b_o, g_o, bon_o, nsh_o, carry):
    bb, ll, wd = pa_ref.shape
    n = bb * ll
    hw = RW_WIDTH

    @pl.when(pl.program_id(1) == 0)
    def _():
        carry[...] = sh_ref[...]

    pa = pa_ref[...]
    rolled = pltpu.roll(pa.reshape(n, wd), 1, 0).reshape(bb, ll, wd)
    tok = lax.broadcasted_iota(I32, (bb, ll, wd), 1)
    prev = jnp.where(tok == 0, carry[...], rolled)
    last = pa_ref[:, ll - 1:ll, :]
    carry[...] = last
    nsh_o[...] = last
    xs = (pa + (prev - pa) * mu_ref[...]).reshape(n, wd)

    r, k, v = xs[:, 0:hw], xs[:, hw:2 * hw], xs[:, 2 * hw:3 * hw]
    xwa = xs[:, 3 * hw:3 * hw + LANES]
    xg = xs[:, 3 * hw + LANES:]
    w_log = -_softplus(-(w0_ref[...] + _mm(jnp.tanh(xwa), wup_ref[...], passes=3))) - 0.5
    lw = -jnp.exp(w_log)
    a = _sigmoid(a0_ref[...] + _mm(xwa, aup_ref[...], passes=3))
    g = _mm(_sigmoid(xg), gup_ref[...])
    bd = bd_ref[...]
    kkv = k * kk_ref[...]
    kkn = kkv * lax.rsqrt(jnp.maximum(_xmm01(kkv * kkv, bd), 1e-24))
    k2 = k * (1.0 + (a - 1.0) * ka_ref[...])
    bonus = _xmm01(r * k2 * rk_ref[...], bd) * v
    for ref, val in ((r_o, r), (lw_o, lw), (k_o, k2), (v_o, v), (a_o, -kkn), (b_o, kkn * a), (g_o, g),
                     (bon_o, bonus)):
        ref[...] = val.reshape(bb, ll, hw)


def _rwprep_call(pa, s_sh, p):
    bn, seq, wd = pa.shape
    bb, ll = _tile(bn, seq, TOK_TILE)
    hw = RW_WIDTH
    tok = lambda w: pl.BlockSpec((bb, ll, w), lambda b, l: (b, l, 0))
    row = lambda w: pl.BlockSpec((bb, 1, w), lambda b, l: (b, 0, 0))
    full = lambda a: pl.BlockSpec(a.shape, lambda b, l: (0,) * a.ndim)
    consts = (p["mu"], p["w0"], p["wup"], p["a0"], p["aup"], p["gup"], p["kk"], p["ka"], p["rk"], p["bd64"])
    outs = pl.pallas_call(
        _rwprep_body,
        grid=(bn // bb, seq // ll),
        in_specs=[tok(wd), row(wd)] + [full(c) for c in consts],
        out_specs=[tok(hw)] * 8 + [row(wd)],
        out_shape=[jax.ShapeDtypeStruct((bn, seq, hw), F32)] * 8 + [jax.ShapeDtypeStruct((bn, 1, wd), F32)],
        scratch_shapes=[pltpu.VMEM((bb, 1, wd), F32)],
        compiler_params=_cparams(("arbitrary", "arbitrary"), VMEM_LIMIT),
        name="rwkv_prep",
    )(pa, s_sh.reshape(bn, 1, wd), *consts)
    return outs


def _unit_masks(n, tl):
    ri = lax.broadcasted_iota(I32, (n, n), 0)
    ci = lax.broadcasted_iota(I32, (n, n), 1)
    same = (ri >> _log2(tl)) == (ci >> _log2(tl))
    return same, same & (ri > ci), same & (ri >= ci)


def _rwscan_body(r_ref, lw_ref, k_ref, v_ref, a_ref, b_ref, s0_ref, y_ref, sn_ref, st, *, nu, nseq, tl, passes):
    n = nseq * tl
    n2 = 2 * n
    p_aa, p_inv, p_apply, p_state, p_y = passes

    @pl.when(pl.program_id(1) == 0)
    def _():
        st[...] = s0_ref[...]

    same, _, incl = _unit_masks(n, tl)
    m_cum = jnp.where(incl, 1.0, 0.0)
    m_seq = jnp.where(same, 1.0, 0.0)
    ri = lax.broadcasted_iota(I32, (n2, n2), 0)
    ci = lax.broadcasted_iota(I32, (n2, n2), 1)
    rt, ct = ri & (n - 1), ci & (n - 1)
    dsame = ((rt >> _log2(tl)) == (ct >> _log2(tl))) & ((ri >> _log2(n)) == (ci >> _log2(n)))
    strict_d = dsame & (rt > ct)
    incl_d = dsame & (rt >= ct)
    eye_d = jnp.where(ri == ci, 1.0, 0.0)
    lane = lax.broadcasted_iota(I32, (1, LANES), 1)
    m0 = jnp.where(lane < RW_HEAD, 1.0, 0.0)
    m1 = 1.0 - m0

    def dup(x):
        return jnp.concatenate([x * m0, x * m1], axis=0)

    def seq_rows(x, q):
        if nseq == 1:
            return x
        return jnp.concatenate([x[q * tl:(q + 1) * tl], x[n + q * tl:n + (q + 1) * tl]], axis=0)

    def unit_rows(parts):
        if nseq == 1:
            return parts[0]
        return jnp.concatenate([p[0:tl] for p in parts] + [p[tl:2 * tl] for p in parts], axis=0)

    chains = [(u, p) for u in range(nu) for p in range(RW_HEADS // 2)]
    ids = range(len(chains))
    cat0 = lambda *xs: jnp.concatenate(xs, axis=0)

    def ld(ref, c):
        u, p = chains[c]
        return ref[u * nseq:(u + 1) * nseq, :, p * LANES:(p + 1) * LANES].reshape(n, LANES)

    lw = [ld(lw_ref, c) for c in ids]
    cum = [_mm01(m_cum, x) for x in lw]
    tot = [_mm01(m_seq, x) for x in lw]
    e_c = [jnp.exp(x) for x in cum]
    e_n = [jnp.exp(-x) for x in cum]
    e_l = [jnp.exp(t - x) for t, x in zip(tot, cum)]
    at_d = [dup(ld(a_ref, c) * jnp.exp(cum[c] - lw[c])) for c in ids]
    rt_d = [dup(ld(r_ref, c) * e_c[c]) for c in ids]
    bt_d = [dup(ld(b_ref, c) * e_n[c]) for c in ids]
    kt_d = [dup(ld(k_ref, c) * e_n[c]) for c in ids]
    bh_d = [dup(ld(b_ref, c) * e_l[c]) for c in ids]
    kh_d = [dup(ld(k_ref, c) * e_l[c]) for c in ids]
    v_d = [dup(ld(v_ref, c)) for c in ids]
    aa = [_mm(cat0(at_d[c], rt_d[c]), cat0(bt_d[c], kt_d[c]), "nt", p_aa) for c in ids]
    a_ab = [jnp.where(strict_d, x[0:n2, 0:n2], 0.0) for x in aa]
    a_ak = [jnp.where(strict_d, x[0:n2, n2:], 0.0) for x in aa]
    a_rb = [jnp.where(incl_d, x[n2:, 0:n2], 0.0) for x in aa]
    a_rk = [jnp.where(incl_d, x[n2:, n2:], 0.0) for x in aa]
    zy = [_mm(cat0(a_ak[c], a_rk[c]), v_d[c], passes=p_apply) for c in ids]
    tinv = [eye_d + x for x in a_ab]
    nk = a_ab
    for _ in range(_log2(tl) - 1):
        nk = [_mm(x, x, passes=p_inv) for x in nk]
        tinv = [t + _mm(t, x, passes=p_inv) for t, x in zip(tinv, nk)]
    wu = [_mm(tinv[c], jnp.concatenate([at_d[c], zy[c][0:n2]], axis=1), passes=p_apply) for c in ids]
    seqs = range(nseq)
    srow = lambda c, q: (chains[c][0] * nseq + q, chains[c][1])
    s_old = [[st[srow(c, q)] for q in seqs] for c in ids]
    xs = [[_mm(cat0(seq_rows(wu[c][:, 0:LANES], q), seq_rows(rt_d[c], q)), s_old[c][q], "nt", p_state)
           for q in seqs] for c in ids]
    u_q = [[xs[c][q][0:2 * tl] + seq_rows(wu[c][:, LANES:], q) for q in seqs] for c in ids]
    for c in ids:
        for q in seqs:
            g_c = jnp.exp(tot[c][q * tl:q * tl + 1, :])
            st[srow(c, q)] = s_old[c][q] * g_c + _mm(cat0(u_q[c][q], seq_rows(v_d[c], q)),
                                                     cat0(seq_rows(bh_d[c], q), seq_rows(kh_d[c], q)), "tn", p_state)
    for c in ids:
        u, p = chains[c]
        y_d = (unit_rows([xs[c][q][2 * tl:] for q in seqs]) + _mm(a_rb[c], unit_rows(u_q[c]), passes=p_y)
               + zy[c][n2:])
        y_ref[u * nseq:(u + 1) * nseq, :, p * LANES:(p + 1) * LANES] = (y_d[0:n] + y_d[n:]).reshape(nseq, tl, LANES)

    @pl.when(pl.program_id(1) == pl.num_programs(1) - 1)
    def _():
        sn_ref[...] = st[...]


def _unit_shape(bn, seq):
    if seq >= UNIT:
        assert seq % UNIT == 0
        return 1, UNIT
    assert UNIT % seq == 0 and bn % (UNIT // seq) == 0
    return UNIT // seq, seq


def _rwscan_call(r, lw, k2, v, a_s, b_s, s0_bd, passes=RW_SCAN_PASSES):
    bn, seq, hw = r.shape
    nseq, tl = _unit_shape(bn, seq)
    nu = RW_UNITS_PER_STEP if bn % (RW_UNITS_PER_STEP * nseq) == 0 else 1
    rows = nu * nseq
    tok = pl.BlockSpec((rows, tl, hw), lambda b, c: (b, c, 0))
    stt = pl.BlockSpec((rows, RW_HEADS // 2, LANES, LANES), lambda b, c: (b, 0, 0, 0))
    return pl.pallas_call(
        functools.partial(_rwscan_body, nu=nu, nseq=nseq, tl=tl, passes=passes),
        grid=(bn // rows, seq // tl),
        in_specs=[tok] * 6 + [stt],
        out_specs=[tok, stt],
        out_shape=[jax.ShapeDtypeStruct((bn, seq, hw), F32), jax.ShapeDtypeStruct(s0_bd.shape, F32)],
        scratch_shapes=[pltpu.VMEM((rows, RW_HEADS // 2, LANES, LANES), F32)],
        compiler_params=_cparams(("arbitrary", "arbitrary"), VMEM_LIMIT),
        name="rwkv_scan",
    )(r, lw, k2, v, a_s, b_s, s0_bd)


def _gla_body(qkv_ref, xal_ref, gate_ref, aup_ref, ab_ref, ng_ref, s0_ref, o_ref, sn_ref, st, *, nu, nseq, tl, cs):
    n = nseq * tl
    n2 = 2 * n
    nsub = tl // cs

    @pl.when(pl.program_id(1) == 0)
    def _():
        st[...] = s0_ref[...]

    same, _, incl = _unit_masks(n, cs)
    m_cum = jnp.where(incl, 1.0, 0.0)
    m_sub = jnp.where(same, 1.0, 0.0)
    ri = lax.broadcasted_iota(I32, (n2, n2), 0)
    ci = lax.broadcasted_iota(I32, (n2, n2), 1)
    rt, ct = ri & (n - 1), ci & (n - 1)
    causal_d = ((rt >> _log2(cs)) == (ct >> _log2(cs))) & ((ri >> _log2(n)) == (ci >> _log2(n))) & (rt >= ct)
    lane = lax.broadcasted_iota(I32, (1, LANES), 1)
    m0 = jnp.where(lane < GLA_DK, 1.0, 0.0)
    m1 = 1.0 - m0
    sr = lax.broadcasted_iota(I32, (2 * GLA_DV, LANES), 0)
    sc = lax.broadcasted_iota(I32, (2 * GLA_DV, LANES), 1)
    st_mask = jnp.where((sr >> _log2(GLA_DV)) == (sc >> _log2(GLA_DK)), 1.0, 0.0)

    def dup(x):
        return jnp.concatenate([x * m0, x * m1], axis=0)

    chains = [(u, p) for u in range(nu) for p in range(GLA_HEADS // 2)]
    ids = range(len(chains))
    urows = lambda u: slice(u * nseq, (u + 1) * nseq)
    ng = ng_ref[...]
    la_all = [-_softplus(-(_mm(xal_ref[urows(u), :, :].reshape(n, LANES), aup_ref[...], passes=3) + ab_ref[...]))
              * (1.0 / GLA_GATE_TAU) for u in range(nu)]

    def ld(ref, c, off, width):
        return ref[urows(chains[c][0]), :, off:off + width].reshape(n, width)

    q = [ld(qkv_ref, c, chains[c][1] * LANES, LANES) * (GLA_DK ** -0.5) for c in ids]
    k = [ld(qkv_ref, c, GLA_KW + chains[c][1] * LANES, LANES) for c in ids]
    vp = [ld(qkv_ref, c, 2 * GLA_KW + chains[c][1] * 2 * GLA_DV, 2 * GLA_DV) for c in ids]
    la = [la_all[u][:, p * LANES:(p + 1) * LANES] for u, p in chains]
    bc = [_mm01(m_cum, x) for x in la]
    bl = [_mm01(m_sub, x) for x in la]
    qe = [q[c] * jnp.exp(bc[c]) for c in ids]
    ke = [k[c] * jnp.exp(-bc[c]) for c in ids]
    kd = [k[c] * jnp.exp(bl[c] - bc[c]) for c in ids]
    att = [jnp.where(causal_d, _mm(dup(qe[c]), dup(ke[c]), "nt", passes=1), 0.0) for c in ids]
    v_st = [jnp.concatenate([x[:, 0:GLA_DV], x[:, GLA_DV:]], axis=0) for x in vp]
    o_st = [_mm(att[c], v_st[c], passes=1) for c in ids]
    upd = [[_mm(vp[c][r0:r0 + cs], kd[c][r0:r0 + cs], "tn", passes=1) for r0 in range(0, n, cs)] for c in ids]
    inter = [[None] * (n // cs) for _ in ids]
    for sq in range(nseq):
        s = [st[chains[c][0] * nseq + sq, chains[c][1]] for c in ids]
        for j in range(nsub):
            i = sq * nsub + j
            r0 = i * cs
            for c in ids:
                inter[c][i] = _mm(qe[c][r0:r0 + cs], s[c], "nt", passes=1)
                s[c] = s[c] * jnp.exp(bl[c][r0:r0 + 1, :]) + st_mask * upd[c][i]
        for c in ids:
            st[chains[c][0] * nseq + sq, chains[c][1]] = s[c]
    for c in ids:
        u, p = chains[c]
        o = o_st[c] + jnp.concatenate([x[:, 0:GLA_DV] for x in inter[c]] + [x[:, GLA_DV:] for x in inter[c]], axis=0)
        o = o * lax.rsqrt(jnp.mean(o * o, axis=-1, keepdims=True) + NORM_EPS) * ng
        goff = p * 2 * GLA_DV
        gp = ld(gate_ref, c, goff, 2 * GLA_DV)
        g_st = jnp.concatenate([gp[:, 0:GLA_DV], gp[:, GLA_DV:]], axis=0)
        ob = o * (g_st * _sigmoid(g_st))
        o_ref[urows(u), :, goff:goff + GLA_DV] = ob[0:n].reshape(nseq, tl, GLA_DV)
        o_ref[urows(u), :, goff + GLA_DV:goff + 2 * GLA_DV] = ob[n:].reshape(nseq, tl, GLA_DV)

    @pl.when(pl.program_id(1) == pl.num_programs(1) - 1)
    def _():
        sn_ref[...] = st[...]


def _gla_call(qkv, xal, gate, s0_t, p):
    bn, seq, _ = qkv.shape
    nseq, tl = _unit_shape(bn, seq)
    cs = min(GLA_CHUNK, seq)
    assert tl % cs == 0
    nu = GLA_UNITS_PER_STEP if bn % (GLA_UNITS_PER_STEP * nseq) == 0 else 1
    rows = nu * nseq
    tok = lambda w: pl.BlockSpec((rows, tl, w), lambda b, c: (b, c, 0))
    full = lambda a: pl.BlockSpec(a.shape, lambda b, c: (0,) * a.ndim)
    stt = pl.BlockSpec((rows, GLA_HEADS // 2, 2 * GLA_DV, LANES), lambda b, c: (b, 0, 0, 0))
    consts = (p["gla_aup"], p["gla_ab"], p["gla_ng"])
    return pl.pallas_call(
        functools.partial(_gla_body, nu=nu, nseq=nseq, tl=tl, cs=cs),
        grid=(bn // rows, seq // tl),
        in_specs=[tok(QKV_W), tok(XAL_W), tok(GG_W)] + [full(c) for c in consts] + [stt],
        out_specs=[tok(GLA_VW), stt],
        out_shape=[jax.ShapeDtypeStruct((bn, seq, GLA_VW), F32), jax.ShapeDtypeStruct(s0_t.shape, F32)],
        scratch_shapes=[pltpu.VMEM((rows, GLA_HEADS // 2, 2 * GLA_DV, LANES), F32)],
        compiler_params=_cparams(("arbitrary", "arbitrary"), VMEM_LIMIT),
        name="gla_chunked",
    )(qkv, xal, gate, *consts, s0_t)


def _merge_body(y_ref, g_ref, bon_ref, ob_ref, mg_ref, x_ref, mod_ref, gng_ref, gnb_ref, bd_ref, wpa_ref,
                wpb_ref, wout_ref, n2_ref, rwh_ref, rwl_ref, x1_o, h2_o, lg_o):
    bb, ll, d = x_ref.shape
    n = bb * ll
    hw = RW_WIDTH
    bd = bd_ref[...]
    y = y_ref[...].reshape(n, hw)
    mu = _xmm01(y, bd, pieces=3) * (1.0 / RW_HEAD)
    dv = y - mu
    var = _xmm01(dv * dv, bd) * (1.0 / RW_HEAD)
    yn = dv * lax.rsqrt(var + RW_GN_EPS) * gng_ref[...] + gnb_ref[...]
    o_a = (yn + bon_ref[...].reshape(n, hw)) * g_ref[...].reshape(n, hw)
    o_b = ob_ref[...].reshape(n, GLA_VW)
    mg = mg_ref[...].reshape(n, 2 * d)
    merged = _sigmoid(mg[:, 0:d]) * _mm(o_a, wpa_ref[...]) + _sigmoid(mg[:, d:]) * _mm(o_b, wpb_ref[...])
    mix = _mm(merged, wout_ref[...]).reshape(bb, ll, d)
    x1 = x_ref[...] + mod_ref[:, 2:3, :] * mix
    x1_o[...] = x1
    yn2 = x1 * lax.rsqrt(jnp.mean(x1 * x1, axis=-1, keepdims=True) + NORM_EPS) * n2_ref[...]
    h2 = (yn2 * (1.0 + mod_ref[:, 4:5, :]) + mod_ref[:, 3:4, :]).reshape(n, d)
    hh, hl = _split(h2, 2)
    rwh, rwl = rwh_ref[...], rwl_ref[...]
    lg_o[...] = (jnp.dot(hh, rwh, preferred_element_type=F32) + jnp.dot(hh, rwl, preferred_element_type=F32)
                 + jnp.dot(hl, rwh, preferred_element_type=F32))
    _rows_to_slabs(h2_o, h2)


def _merge_call(y, g, bonus, o_b, mg, x, mod, p):
    bn, seq, d = x.shape
    bb, ll = _tile(bn, seq, TOK_TILE)
    nl = seq // ll
    tn = bn * seq
    tok = lambda w: pl.BlockSpec((bb, ll, w), lambda b, l: (b, l, 0))
    full = lambda a: pl.BlockSpec(a.shape, lambda b, l: (0,) * a.ndim)
    consts = (p["gn_g"], p["gn_b"], p["bd64"], p["w_pa"], p["w_pb"], p["w_out"], p["norm2_g"], p["rw_hi"],
              p["rw_lo"])
    return pl.pallas_call(
        _merge_body,
        grid=(bn // bb, nl),
        in_specs=[tok(RW_WIDTH)] * 3 + [tok(GLA_VW), tok(MG_W), tok(d),
                                        pl.BlockSpec((bb, 6, d), lambda b, l: (b, 0, 0))] + [full(c) for c in consts],
        out_specs=[tok(d),
                   pl.BlockSpec((bb * ll * CHUNKS, LANES), lambda b, l: (b * nl + l, 0)),
                   pl.BlockSpec((bb * ll, N_EXPERTS), lambda b, l: (b * nl + l, 0))],
        out_shape=[jax.ShapeDtypeStruct((bn, seq, d), F32),
                   jax.ShapeDtypeStruct((tn * CHUNKS, LANES), F32),
                   jax.ShapeDtypeStruct((tn, N_EXPERTS), F32)],
        compiler_params=_cparams(("arbitrary", "arbitrary"), VMEM_LIMIT),
        name="merge_outproj_router",
    )(y, g, bonus, o_b, mg, x, mod, *consts)


def _route_body(lg_ref, rb_ref, e_o, rk_o, w_o, cnt_o, carry):
    tm = lg_ref.shape[0]
    ne = N_EXPERTS

    @pl.when(pl.program_id(0) == 0)
    def _():
        carry[...] = jnp.zeros_like(carry)

    neg = -jnp.inf
    scores = _sigmoid(lg_ref[...])
    sel = scores + rb_ref[...]
    lane_i = lax.broadcasted_iota(I32, (tm, ne), 1)
    lane = lane_i.astype(F32)
    grp = (lane_i >> _log2(GROUP_SIZE)).astype(F32)

    def first_max(x):
        m = jnp.max(x, axis=-1, keepdims=True)
        idx = jnp.min(jnp.where(x == m, lane, float(ne)), axis=-1, keepdims=True)
        return m, idx

    gs = jnp.full((tm, ne), neg, F32)
    for gidx in range(N_GROUPS):
        sg = jnp.where(grp == float(gidx), sel, neg)
        m1, i1 = first_max(sg)
        m2 = jnp.max(jnp.where(lane == i1, neg, sg), axis=-1, keepdims=True)
        gs = jnp.where(lane == float(gidx), m1 + m2, gs)
    cur = jnp.full((tm, ne), neg, F32)
    for _ in range(TOPK_GROUPS):
        _, gi = first_max(gs)
        cur = jnp.where(grp == gi, sel, cur)
        gs = jnp.where(lane == gi, neg, gs)

    pm = jnp.zeros((tm, ne), F32)
    eidx, wts = [], []
    for _ in range(TOP_K):
        _, ei = first_max(cur)
        hit = lane == ei
        pm = jnp.where(hit, 1.0, pm)
        eidx.append(ei)
        wts.append(jnp.sum(jnp.where(hit, scores, 0.0), axis=-1, keepdims=True))
        cur = jnp.where(hit, neg, cur)
    wsum = wts[0]
    for w in wts[1:]:
        wsum = wsum + w

    ri = lax.broadcasted_iota(I32, (tm, tm), 0)
    ci = lax.broadcasted_iota(I32, (tm, tm), 1)
    below = jnp.where(ri > ci, 1.0, 0.0)
    rank = _mm(below, pm, passes=1) + carry[...]
    carry[...] = carry[...] + jnp.sum(pm, axis=0, keepdims=True)
    cnt_o[...] = carry[...]

    ol = lax.broadcasted_iota(I32, (tm, LANES), 1)
    e_out = jnp.zeros((tm, LANES), I32)
    r_out = jnp.zeros((tm, LANES), I32)
    w_out = jnp.zeros((tm, LANES), F32)
    for kk in range(TOP_K):
        rk = jnp.sum(jnp.where(lane == eidx[kk], rank, 0.0), axis=-1, keepdims=True)
        e_out = jnp.where(ol == kk, eidx[kk].astype(I32), e_out)
        r_out = jnp.where(ol == kk, rk.astype(I32), r_out)
        w_out = jnp.where(ol == kk, wts[kk] / wsum * ROUTED_SCALE, w_out)
    e_o[...] = e_out
    rk_o[...] = r_out
    w_o[...] = w_out


def _route_call(logits, router_b):
    tn, ne = logits.shape
    tm = TOK_TILE
    assert tn % tm == 0
    tok = lambda w: pl.BlockSpec((tm, w), lambda i: (i, 0))
    one = pl.BlockSpec((1, ne), lambda i: (0, 0))
    return pl.pallas_call(
        _route_body,
        grid=(tn // tm,),
        in_specs=[tok(ne), one],
        out_specs=[tok(LANES), tok(LANES), tok(LANES), one],
        out_shape=[jax.ShapeDtypeStruct((tn, LANES), I32), jax.ShapeDtypeStruct((tn, LANES), I32),
                   jax.ShapeDtypeStruct((tn, LANES), F32), jax.ShapeDtypeStruct((1, ne), F32)],
        scratch_shapes=[pltpu.VMEM((1, ne), F32)],
        compiler_params=_cparams(("arbitrary",)),
        name="moe_route",
    )(logits, router_b.reshape(1, ne))


def _dispatch_body(e_ref, rk_ref, ps_ref, h2_ref, xs_hbm, sem, *, tm):
    def issue(m, carry):
        for kk in range(TOP_K):
            j = m * TOP_K + kk
            row = ps_ref[e_ref[0, 0, j]] + rk_ref[0, 0, j]
            pltpu.make_async_copy(_slab(h2_ref, m), _slab(xs_hbm, row), sem).start()
        return carry

    lax.fori_loop(0, tm, issue, 0)
    all_rows = xs_hbm.at[pl.ds(0, tm * TOP_K * CHUNKS)]
    pltpu.make_async_copy(all_rows, all_rows, sem).wait()


def _assign_specs(tm, index_map):
    blk = pl.BlockSpec((1, 1, tm * TOP_K), index_map, memory_space=pltpu.SMEM)
    return blk, pl.BlockSpec(memory_space=pltpu.SMEM)


def _dispatch_call(eidx, rank, pad_start, h2s, n_rows):
    tn = h2s.shape[0] // CHUNKS
    tm = TOK_TILE
    blk, whole = _assign_specs(tm, lambda i: (i, 0, 0))
    shp = (tn // tm, 1, tm * TOP_K)
    return pl.pallas_call(
        functools.partial(_dispatch_body, tm=tm),
        grid=(tn // tm,),
        in_specs=[blk, blk, whole, pl.BlockSpec((tm * CHUNKS, LANES), lambda i: (i, 0))],
        out_specs=pl.BlockSpec(memory_space=pl.ANY),
        out_shape=jax.ShapeDtypeStruct((n_rows * CHUNKS, LANES), F32),
        scratch_shapes=[pltpu.SemaphoreType.DMA],
        compiler_params=_cparams(("arbitrary",)),
        name="moe_dispatch",
    )(eidx.reshape(shp), rank.reshape(shp), pad_start, h2s)


def _expert_body(be_ref, nr_ref, xs_ref, wg_ref, wu_ref, wd_ref, ob_ref):
    del be_ref
    nr = nr_ref[pl.program_id(0)]

    @pl.when(nr > 0)
    def _():
        live = lax.broadcasted_iota(I32, (MOE_BLK, LANES), 0) < nr
        x = jnp.concatenate([jnp.where(live, xs_ref[pl.ds(c, MOE_BLK, stride=CHUNKS), :], 0.0)
                             for c in range(CHUNKS)], axis=1).astype(BF16)
        hg = jnp.dot(x, wg_ref[0].astype(BF16), preferred_element_type=F32)
        hu = jnp.dot(x, wu_ref[0].astype(BF16), preferred_element_type=F32)
        hh = (hg * _sigmoid(hg) * hu).astype(BF16)
        _rows_to_slabs(ob_ref, jnp.dot(hh, wd_ref[0].astype(BF16), preferred_element_type=F32))

    @pl.when(nr == 0)
    def _():
        ob_ref[...] = jnp.zeros_like(ob_ref)


def _expert_call(block_e, block_rows, xs, wg, wu, wd):
    nb = xs.shape[0] // (MOE_BLK * CHUNKS)
    d, ff = wg.shape[1], wg.shape[2]
    rows = pl.BlockSpec((MOE_BLK * CHUNKS, LANES), lambda i, be, nr: (i, 0))
    grid_spec = pltpu.PrefetchScalarGridSpec(
        num_scalar_prefetch=2,
        grid=(nb,),
        in_specs=[rows,
                  pl.BlockSpec((1, d, ff), lambda i, be, nr: (be[i], 0, 0)),
                  pl.BlockSpec((1, d, ff), lambda i, be, nr: (be[i], 0, 0)),
                  pl.BlockSpec((1, ff, d), lambda i, be, nr: (be[i], 0, 0))],
        out_specs=rows,
    )
    return pl.pallas_call(
        _expert_body,
        grid_spec=grid_spec,
        out_shape=jax.ShapeDtypeStruct(xs.shape, F32),
        compiler_params=_cparams(("arbitrary",), VMEM_LIMIT),
        name="moe_experts",
    )(block_e, block_rows, xs, wg, wu, wd)


def _combine_body(e_ref, rk_ref, wt_ref, ps_ref, ob_hbm, h2_ref, x1_ref, mod_ref, sg_ref, su_ref, sd_ref, fg_ref,
                  out_ref, gbuf, rbuf, sem, *, tm):
    bb, ll, d = x1_ref.shape

    def issue(m, carry):
        for kk in range(TOP_K):
            j = m * TOP_K + kk
            row = ps_ref[e_ref[0, 0, j]] + rk_ref[0, 0, j]
            pltpu.make_async_copy(_slab(ob_hbm, row), _slab(gbuf, j), sem).start()
        return carry

    lax.fori_loop(0, tm, issue, 0)
    pltpu.make_async_copy(ob_hbm.at[pl.ds(0, tm * TOP_K * CHUNKS)], gbuf, sem).wait()

    def mix(m, carry):
        acc = wt_ref[0, 0, m * TOP_K] * _slab(gbuf, m * TOP_K)[...]
        for kk in range(1, TOP_K):
            acc = acc + wt_ref[0, 0, m * TOP_K + kk] * _slab(gbuf, m * TOP_K + kk)[...]
        _slab(rbuf, m)[...] = acc
        return carry

    lax.fori_loop(0, tm, mix, 0)

    routed = _rows_from_slabs(rbuf, tm)
    h2 = _rows_from_slabs(h2_ref, tm).astype(BF16)
    hg = jnp.dot(h2, sg_ref[...], preferred_element_type=F32)
    hu = jnp.dot(h2, su_ref[...], preferred_element_type=F32)
    shared = jnp.dot((hg * _sigmoid(hg) * hu).astype(BF16), sd_ref[...], preferred_element_type=F32)
    ff = (routed + shared).reshape(bb, ll, d)
    x2 = x1_ref[...] + mod_ref[:, 5:6, :] * ff
    out_ref[...] = x2 * lax.rsqrt(jnp.mean(x2 * x2, axis=-1, keepdims=True) + NORM_EPS) * fg_ref[...]


def _combine_call(eidx, rank, wts, pad_start, ob, h2s, x1, mod, p):
    bn, seq, d = x1.shape
    tm = CMB_TILE
    bb, ll = _tile(bn, seq, tm)
    nl = seq // ll
    tn = bn * seq
    smem, whole = _assign_specs(tm, lambda b, l: (b * nl + l, 0, 0))
    tok = pl.BlockSpec((bb, ll, d), lambda b, l: (b, l, 0))
    full = lambda a: pl.BlockSpec(a.shape, lambda b, l: (0,) * a.ndim)
    consts = (p["sh_gate"], p["sh_up"], p["sh_down"], p["final_g"])
    shp = (tn // tm, 1, tm * TOP_K)
    return pl.pallas_call(
        functools.partial(_combine_body, tm=tm),
        grid=(bn // bb, nl),
        in_specs=[smem, smem, smem, whole, pl.BlockSpec(memory_space=pl.ANY),
                  pl.BlockSpec((tm * CHUNKS, LANES), lambda b, l: (b * nl + l, 0)),
                  tok, pl.BlockSpec((bb, 6, d), lambda b, l: (b, 0, 0))] + [full(c) for c in consts],
        out_specs=tok,
        out_shape=jax.ShapeDtypeStruct((bn, seq, d), F32),
        scratch_shapes=[pltpu.VMEM((tm * TOP_K * CHUNKS, LANES), F32), pltpu.VMEM((tm * CHUNKS, LANES), F32),
                        pltpu.SemaphoreType.DMA],
        compiler_params=_cparams(("arbitrary", "arbitrary"), VMEM_LIMIT),
        name="moe_combine_final",
    )(eidx.reshape(shp), rank.reshape(shp), wts.reshape(shp), pad_start, ob, h2s, x1, mod, *consts)


def _rw_state_to_pairs(s):
    bn = s.shape[0]
    s = s.reshape(bn, RW_HEADS // 2, 2, RW_HEAD, RW_HEAD)
    z = jnp.zeros_like(s[:, :, 0])
    return jnp.concatenate([jnp.concatenate([s[:, :, 0], z], axis=-1),
                            jnp.concatenate([z, s[:, :, 1]], axis=-1)], axis=-2)


def _rw_state_from_pairs(sp):
    bn = sp.shape[0]
    s = jnp.stack([sp[:, :, :RW_HEAD, :RW_HEAD], sp[:, :, RW_HEAD:, RW_HEAD:]], axis=2)
    return s.reshape(bn, RW_HEADS, RW_HEAD, RW_HEAD)


def _gla_state_to_pairs(s):
    bn = s.shape[0]
    t = jnp.swapaxes(s, -1, -2).reshape(bn, GLA_HEADS // 2, 2, GLA_DV, GLA_DK)
    z = jnp.zeros_like(t[:, :, 0])
    return jnp.concatenate([jnp.concatenate([t[:, :, 0], z], axis=-1),
                            jnp.concatenate([z, t[:, :, 1]], axis=-1)], axis=-2)


def _gla_state_from_pairs(sp):
    bn = sp.shape[0]
    t = jnp.stack([sp[:, :, :GLA_DV, :GLA_DK], sp[:, :, GLA_DV:, GLA_DK:]], axis=2)
    return jnp.swapaxes(t.reshape(bn, GLA_HEADS, GLA_DV, GLA_DK), -1, -2)


def _layer_params(l, ada_w, ada_b, norm1_g, norm2_g, w_in, mu_shift, rw_w0, rw_w_up, rw_a0, rw_a_up, rw_g_up,
                  rw_k_k, rw_k_a, rw_r_k, rw_gn_g, rw_gn_b, gla_a_up, gla_a_bias, gla_norm_g, w_pa, w_pb, w_out,
                  router_w, router_b, exp_gate, exp_up, exp_down, sh_gate, sh_up, sh_down):
    d = D_MODEL
    wi = w_in[l]
    gla0 = RW_SHIFT_COLS
    xal0 = gla0 + QKV_W
    pad = jnp.zeros((d, XAL_W - GLA_GATE_RANK), F32)
    w_pack = jnp.concatenate([wi[:, :xal0], wi[:, xal0:xal0 + GLA_GATE_RANK], pad,
                              wi[:, xal0 + GLA_GATE_RANK:]], axis=1).astype(BF16)
    zr = jnp.zeros((RW_W_RANK, RW_WIDTH), F32)
    hid = jnp.arange(RW_WIDTH) // RW_HEAD
    row = lambda a: a.reshape(1, -1)
    rw_hi = router_w[l].astype(BF16)
    return dict(
        ada_w=ada_w[l], ada_b=ada_b[l], norm1_g=norm1_g[l], norm2_g=norm2_g[l].reshape(1, 1, d), w_pack=w_pack,
        mu=mu_shift[l].reshape(1, 1, -1), w0=row(rw_w0[l]), wup=jnp.concatenate([rw_w_up[l], zr], axis=0),
        a0=row(rw_a0[l]), aup=jnp.concatenate([zr, rw_a_up[l]], axis=0), gup=rw_g_up[l].astype(BF16),
        kk=row(rw_k_k[l]), ka=row(rw_k_a[l]), rk=row(rw_r_k[l]),
        bd64=(hid[:, None] == hid[None, :]).astype(BF16),
        gn_g=row(rw_gn_g[l]), gn_b=row(rw_gn_b[l]),
        gla_aup=jnp.concatenate([gla_a_up[l], jnp.zeros((XAL_W - GLA_GATE_RANK, GLA_KW), F32)], axis=0),
        gla_ab=row(gla_a_bias[l]), gla_ng=row(gla_norm_g[l]),
        w_pa=w_pa[l].astype(BF16), w_pb=w_pb[l].astype(BF16), w_out=w_out[l].astype(BF16),
        rw_hi=rw_hi, rw_lo=(router_w[l] - rw_hi.astype(F32)).astype(BF16), router_b=router_b[l],
        exp_gate=exp_gate[l], exp_up=exp_up[l], exp_down=exp_down[l],
        sh_gate=sh_gate[l].astype(BF16), sh_up=sh_up[l].astype(BF16), sh_down=sh_down[l].astype(BF16),
    )


def _mixer_group(x, mod, s_rw, s_sh, s_gla, p):
    pa, qkv, xal, gg, mg = _inproj_call(x, mod, p["norm1_g"], p["w_pack"])
    r, lw, k2, v, a_s, b_s, g, bonus, new_sh = _rwprep_call(pa, s_sh, p)
    y, rw_new = _rwscan_call(r, lw, k2, v, a_s, b_s, _rw_state_to_pairs(s_rw))
    o_b, gla_new = _gla_call(qkv, xal, gg, _gla_state_to_pairs(s_gla), p)
    x1, h2s, logits = _merge_call(y, g, bonus, o_b, mg, x, mod, p)
    states = (_rw_state_from_pairs(rw_new), new_sh[:, 0, :], _gla_state_from_pairs(gla_new))
    return x1, h2s, logits, states


def _moe(h2s, logits, p):
    tn = h2s.shape[0] // CHUNKS
    eidx, rank, wts, counts = _route_call(logits, p["router_b"])
    eidx, rank, wts = eidx[:, :TOP_K], rank[:, :TOP_K], wts[:, :TOP_K]
    counts = counts[0].astype(I32)
    padded = (counts + MOE_BLK - 1) // MOE_BLK * MOE_BLK
    pad_end = jnp.cumsum(padded)
    pad_start = (pad_end - padded).astype(I32)
    nb = (tn * TOP_K + N_EXPERTS * (MOE_BLK - 1)) // MOE_BLK + 1
    first_row = jnp.arange(nb, dtype=I32) * MOE_BLK
    block_e = jnp.minimum(jnp.searchsorted(pad_end, first_row, side="right"), N_EXPERTS - 1).astype(I32)
    block_rows = jnp.clip(pad_start[block_e] + counts[block_e] - first_row, 0, MOE_BLK).astype(I32)
    xs = _dispatch_call(eidx, rank, pad_start, h2s, nb * MOE_BLK)
    ob = _expert_call(block_e, block_rows, xs, p["exp_gate"], p["exp_up"], p["exp_down"])
    return ob, (eidx, rank, wts, pad_start)


def kernel(x_prompt, x_sample, c_prompt, c_sample, state_rwkv, state_shift, state_gla, ada_w, ada_b, norm1_g,
           norm2_g, w_in, mu_shift, rw_w0, rw_w_up, rw_a0, rw_a_up, rw_g_up, rw_k_k, rw_k_a, rw_r_k, rw_gn_g,
           rw_gn_b, gla_a_up, gla_a_bias, gla_norm_g, w_pa, w_pb, w_out, router_w, router_b, exp_gate, exp_up,
           exp_down, sh_gate, sh_up, sh_down, final_g):
    depth = ada_w.shape[0]
    bp, bs = x_prompt.shape[0], x_sample.shape[0]
    tp = bp * x_prompt.shape[1]
    xs_g = [x_prompt, x_sample]
    c_all = jnp.concatenate([c_prompt, c_sample], axis=0)
    zeros = lambda shape: jnp.zeros(shape, x_prompt.dtype)
    new_states = [[], []]
    fg = final_g.reshape(1, 1, D_MODEL)
    for l in range(depth):
        p = _layer_params(l, ada_w, ada_b, norm1_g, norm2_g, w_in, mu_shift, rw_w0, rw_w_up, rw_a0, rw_a_up,
                          rw_g_up, rw_k_k, rw_k_a, rw_r_k, rw_gn_g, rw_gn_b, gla_a_up, gla_a_bias, gla_norm_g,
                          w_pa, w_pb, w_out, router_w, router_b, exp_gate, exp_up, exp_down, sh_gate, sh_up,
                          sh_down)
        p["final_g"] = fg
        mod_all = _mod_call(c_all, p["ada_w"], p["ada_b"])
        mods = [mod_all[:bp], mod_all[bp:]]
        states_in = [
            (zeros((bp, RW_HEADS, RW_HEAD, RW_HEAD)), zeros((bp, RW_SHIFT_COLS)),
             zeros((bp, GLA_HEADS, GLA_DK, GLA_DV))),
            (state_rwkv[l], state_shift[l], state_gla[l]),
        ]
        x1s, h2ss, lgs = [], [], []
        for gi in range(2):
            x1, h2s, logits, st = _mixer_group(xs_g[gi], mods[gi], *states_in[gi], p)
            x1s.append(x1)
            h2ss.append(h2s)
            lgs.append(logits)
            new_states[gi].append(st)
        h2_all = jnp.concatenate(h2ss, axis=0)
        ob, (eidx, rank, wts, pad_start) = _moe(h2_all, jnp.concatenate(lgs, axis=0), p)
        assert depth == 1, "the fused final norm assumes a single layer"
        xs_g = [
            _combine_call(eidx[:tp], rank[:tp], wts[:tp], pad_start, ob, h2ss[0], x1s[0], mods[0], p),
            _combine_call(eidx[tp:], rank[tp:], wts[tp:], pad_start, ob, h2ss[1], x1s[1], mods[1], p),
        ]
    stack = lambda gi, j: jnp.stack([s[j] for s in new_states[gi]])
    return (xs_g[0], xs_g[1], stack(0, 0), stack(0, 1), stack(0, 2), stack(1, 0), stack(1, 1), stack(1, 2))
```

```python
import functools

import jax
import jax.numpy as jnp
from jax import lax
from jax.experimental import pallas as pl
from jax.experimental.pallas import tpu as pltpu

F32, BF16, I32 = jnp.float32, jnp.bfloat16, jnp.int32

D_MODEL = 1024
RW_HEADS, RW_HEAD = 8, 64
RW_WIDTH = RW_HEADS * RW_HEAD
RW_W_RANK, RW_A_RANK, RW_G_RANK = 64, 64, 128
RW_GN_EPS = 64e-5
GLA_HEADS, GLA_DK, GLA_DV = 4, 64, 128
GLA_KW, GLA_VW = GLA_HEADS * GLA_DK, GLA_HEADS * GLA_DV
GLA_GATE_RANK = 16
GLA_GATE_TAU = 16.0
GLA_CHUNK = 16
RW_SHIFT_COLS = 3 * RW_WIDTH + RW_W_RANK + RW_A_RANK + RW_G_RANK
N_EXPERTS, TOP_K, N_GROUPS, TOPK_GROUPS = 256, 8, 8, 4
GROUP_SIZE = N_EXPERTS // N_GROUPS
EXPERT_FF = 256
ROUTED_SCALE = 2.5
NORM_EPS = 1e-6

LANES = 128
SUBLANES = 8
CHUNKS = D_MODEL // LANES
PCH = CHUNKS // 2
UNIT = 64
RW_SCAN_PASSES = (1, 1, 1, 1, 1)
GLA_UNITS_PER_STEP = 4
RW_UNITS_PER_STEP = 4
VMEM_LIMIT = 56 * 1024 * 1024

PA_W, QKV_W, XAL_W, GG_W, MG_W = RW_SHIFT_COLS, 2 * GLA_KW + GLA_VW, LANES, GLA_VW, 2 * D_MODEL
PACK_OFFS = (0, PA_W, PA_W + QKV_W, PA_W + QKV_W + XAL_W, PA_W + QKV_W + XAL_W + GG_W)
PACK_W = PA_W + QKV_W + XAL_W + GG_W + MG_W

TOK_TILE = 256
MOE_BLK = 256
CMB_TILE = 128

_DN = {
    "nn": (((1,), (0,)), ((), ())),
    "nt": (((1,), (1,)), ((), ())),
    "tn": (((0,), (0,)), ((), ())),
}


def _split(x, pieces):
    out, rem = [], x
    for i in range(pieces):
        p = rem.astype(BF16)
        out.append(p)
        if i + 1 < pieces:
            rem = rem - p.astype(F32)
    return out


def _mm(a, b, form="nn", passes=1):
    dn = _DN[form]
    if passes == 6:
        return lax.dot_general(a.astype(F32), b.astype(F32), dn, precision=lax.Precision.HIGHEST,
                               preferred_element_type=F32)
    if passes == 1:
        return lax.dot_general(a.astype(BF16), b.astype(BF16), dn, preferred_element_type=F32)
    ah, al = _split(a, 2)
    bh, bl = _split(b, 2)
    out = lax.dot_general(ah, bh, dn, preferred_element_type=F32)
    out = out + lax.dot_general(ah, bl, dn, preferred_element_type=F32)
    return out + lax.dot_general(al, bh, dn, preferred_element_type=F32)


def _mm01(m01, x, pieces=3):
    m = m01.astype(BF16)
    out = None
    for p in _split(x, pieces):
        t = lax.dot_general(m, p, _DN["nn"], preferred_element_type=F32)
        out = t if out is None else out + t
    return out


def _xmm01(x, m01, pieces=2):
    m = m01.astype(BF16)
    out = None
    for p in _split(x, pieces):
        t = lax.dot_general(p, m, _DN["nn"], preferred_element_type=F32)
        out = t if out is None else out + t
    return out


HI16 = -65536


def _bf16_bits(x):
    return lax.bitcast_convert_type(x.astype(BF16).astype(F32), I32)


def _unpack_pair(w):
    return lax.bitcast_convert_type(w << 16, F32), lax.bitcast_convert_type(w & HI16, F32)


def _rows_to_packed(ref, x):
    for c in range(PCH):
        lo = _bf16_bits(x[:, c * LANES:(c + 1) * LANES])
        hi = _bf16_bits(x[:, (c + PCH) * LANES:(c + PCH + 1) * LANES])
        ref[pl.ds(c, x.shape[0], stride=PCH), :] = ((lo >> 16) & 0xFFFF) | (hi & HI16)


def _rows_from_packed(ref, n, live=None):
    lows, highs = [], []
    for c in range(PCH):
        w = ref[pl.ds(c, n, stride=PCH), :]
        if live is not None:
            w = jnp.where(live, w, 0)
        lo, hi = _unpack_pair(w)
        lows.append(lo.astype(BF16))
        highs.append(hi.astype(BF16))
    return jnp.concatenate(lows + highs, axis=1)


def _slab(ref, row):
    return ref.at[pl.ds(pl.multiple_of(row * PCH, PCH), PCH)]


def _fslab(ref, row):
    return ref.at[pl.ds(pl.multiple_of(row * CHUNKS, CHUNKS), CHUNKS)]


def _sigmoid(x):
    return 1.0 / (1.0 + jnp.exp(-x))


def _softplus(x):
    return jnp.maximum(x, 0.0) + jnp.log(1.0 + jnp.exp(-jnp.abs(x)))


def _log2(n):
    assert n > 0 and n & (n - 1) == 0, n
    return n.bit_length() - 1


def _cparams(sem, vmem=None):
    return pltpu.CompilerParams(dimension_semantics=sem, vmem_limit_bytes=vmem)


def _mod_body(c_ref, w_ref, b_ref, o_ref):
    c = c_ref[...]
    o_ref[0] = _mm(c * _sigmoid(c), w_ref[...], passes=6) + b_ref[...]


def _mod_call(c_all, ada_w, ada_b):
    bt, d = c_all.shape
    out = pl.pallas_call(
        _mod_body,
        grid=(6,),
        in_specs=[pl.BlockSpec((bt, d), lambda k: (0, 0)),
                  pl.BlockSpec((d, d), lambda k: (0, k)),
                  pl.BlockSpec((1, d), lambda k: (0, k))],
        out_specs=pl.BlockSpec((1, bt, d), lambda k: (k, 0, 0)),
        out_shape=jax.ShapeDtypeStruct((6, bt, d), F32),
        compiler_params=_cparams(("arbitrary",)),
        name="adaln_mod",
    )(c_all, ada_w, ada_b.reshape(1, 6 * d))
    return jnp.transpose(out, (1, 0, 2))


def _inproj_body(x_ref, mod_ref, g_ref, w_ref, pa_ref, qkv_ref, xal_ref, gg_ref, mg_ref):
    bb, ll, d = x_ref.shape
    x = x_ref[...]
    y = x * lax.rsqrt(jnp.mean(x * x, axis=-1, keepdims=True) + NORM_EPS) * g_ref[...]
    h = y * (1.0 + mod_ref[:, 1:2, :]) + mod_ref[:, 0:1, :]
    hb = h.reshape(bb * ll, d).astype(BF16)
    for ref, off in zip((pa_ref, qkv_ref, xal_ref, gg_ref, mg_ref), PACK_OFFS):
        w = ref.shape[-1]
        ref[...] = jnp.dot(hb, w_ref[:, off:off + w], preferred_element_type=F32).reshape(bb, ll, w)


def _tile(bn, seq, tile):
    if seq >= tile:
        assert seq % tile == 0
        return 1, tile
    assert tile % seq == 0 and bn % (tile // seq) == 0
    return tile // seq, seq


def _inproj_call(x, mod, norm_g, w_pack):
    bn, seq, d = x.shape
    bb, ll = _tile(bn, seq, TOK_TILE)
    tok = lambda w: pl.BlockSpec((bb, ll, w), lambda b, l: (b, l, 0))
    widths = (PA_W, QKV_W, XAL_W, GG_W, MG_W)
    return pl.pallas_call(
        _inproj_body,
        grid=(bn // bb, seq // ll),
        in_specs=[tok(d),
                  pl.BlockSpec((bb, 6, d), lambda b, l: (b, 0, 0)),
                  pl.BlockSpec((1, 1, d), lambda b, l: (0, 0, 0)),
                  pl.BlockSpec((d, PACK_W), lambda b, l: (0, 0))],
        out_specs=[tok(w) for w in widths],
        out_shape=[jax.ShapeDtypeStruct((bn, seq, w), F32) for w in widths],
        compiler_params=_cparams(("arbitrary", "arbitrary"), VMEM_LIMIT),
        name="norm_inproj",
    )(x, mod, norm_g.reshape(1, 1, d), w_pack)


def _rwprep_body(pa_ref, sh_ref, mu_ref, w0_ref, wup_ref, a0_ref, aup_ref, gup_ref, kk_ref, ka_ref, rk_ref,
                 bd_ref, r_o, lw_o, k_o, v_o, a_o, b_o, g_o, bon_o, nsh_o, carry):
    bb, ll, wd = pa_ref.shape
    n = bb * ll
    hw = RW_WIDTH

    @pl.when(pl.program_id(1) == 0)
    def _():
        carry[...] = sh_ref[...]

    pa = pa_ref[...]
    rolled = pltpu.roll(pa.reshape(n, wd), 1, 0).reshape(bb, ll, wd)
    tok = lax.broadcasted_iota(I32, (bb, ll, wd), 1)
    prev = jnp.where(tok == 0, carry[...], rolled)
    last = pa_ref[:, ll - 1:ll, :]
    carry[...] = last
    nsh_o[...] = last
    xs = (pa + (prev - pa) * mu_ref[...]).reshape(n, wd)

    r, k, v = xs[:, 0:hw], xs[:, hw:2 * hw], xs[:, 2 * hw:3 * hw]
    xwa = xs[:, 3 * hw:3 * hw + LANES]
    xg = xs[:, 3 * hw + LANES:]
    w_log = -_softplus(-(w0_ref[...] + _mm(jnp.tanh(xwa), wup_ref[...], passes=3))) - 0.5
    lw = -jnp.exp(w_log)
    a = _sigmoid(a0_ref[...] + _mm(xwa, aup_ref[...], passes=3))
    g = _mm(_sigmoid(xg), gup_ref[...])
    bd = bd_ref[...]
    kkv = k * kk_ref[...]
    kkn = kkv * lax.rsqrt(jnp.maximum(_xmm01(kkv * kkv, bd), 1e-24))
    k2 = k * (1.0 + (a - 1.0) * ka_ref[...])
    bonus = _xmm01(r * k2 * rk_ref[...], bd) * v
    for ref, val in ((r_o, r), (lw_o, lw), (k_o, k2), (v_o, v), (a_o, -kkn), (b_o, kkn * a), (g_o, g),
                     (bon_o, bonus)):
        ref[...] = val.reshape(bb, ll, hw)


def _rwprep_call(pa, s_sh, p):
    bn, seq, wd = pa.shape
    bb, ll = _tile(bn, seq, TOK_TILE)
    hw = RW_WIDTH
    tok = lambda w: pl.BlockSpec((bb, ll, w), lambda b, l: (b, l, 0))
    row = lambda w: pl.BlockSpec((bb, 1, w), lambda b, l: (b, 0, 0))
    full = lambda a: pl.BlockSpec(a.shape, lambda b, l: (0,) * a.ndim)
    consts = (p["mu"], p["w0"], p["wup"], p["a0"], p["aup"], p["gup"], p["kk"], p["ka"], p["rk"], p["bd64"])
    outs = pl.pallas_call(
        _rwprep_body,
        grid=(bn // bb, seq // ll),
        in_specs=[tok(wd), row(wd)] + [full(c) for c in consts],
        out_specs=[tok(hw)] * 8 + [row(wd)],
        out_shape=[jax.ShapeDtypeStruct((bn, seq, hw), F32)] * 8 + [jax.ShapeDtypeStruct((bn, 1, wd), F32)],
        scratch_shapes=[pltpu.VMEM((bb, 1, wd), F32)],
        compiler_params=_cparams(("arbitrary", "arbitrary"), VMEM_LIMIT),
        name="rwkv_prep",
    )(pa, s_sh.reshape(bn, 1, wd), *consts)
    return outs


def _unit_masks(n, tl):
    ri = lax.broadcasted_iota(I32, (n, n), 0)
    ci = lax.broadcasted_iota(I32, (n, n), 1)
    same = (ri >> _log2(tl)) == (ci >> _log2(tl))
    return same, same & (ri > ci), same & (ri >= ci)


def _rwscan_body(r_ref, lw_ref, k_ref, v_ref, a_ref, b_ref, s0_ref, y_ref, sn_ref, st, *, nu, nseq, tl, passes):
    n = nseq * tl
    n2 = 2 * n
    p_aa, p_inv, p_apply, p_state, p_y = passes

    @pl.when(pl.program_id(1) == 0)
    def _():
        st[...] = s0_ref[...]

    same, _, incl = _unit_masks(n, tl)
    m_cum = jnp.where(incl, 1.0, 0.0)
    m_seq = jnp.where(same, 1.0, 0.0)
    ri = lax.broadcasted_iota(I32, (n2, n2), 0)
    ci = lax.broadcasted_iota(I32, (n2, n2), 1)
    rt, ct = ri & (n - 1), ci & (n - 1)
    dsame = ((rt >> _log2(tl)) == (ct >> _log2(tl))) & ((ri >> _log2(n)) == (ci >> _log2(n)))
    strict_d = dsame & (rt > ct)
    incl_d = dsame & (rt >= ct)
    eye_d = jnp.where(ri == ci, 1.0, 0.0)
    lane = lax.broadcasted_iota(I32, (1, LANES), 1)
    m0 = jnp.where(lane < RW_HEAD, 1.0, 0.0)
    m1 = 1.0 - m0

    def dup(x):
        return jnp.concatenate([x * m0, x * m1], axis=0)

    def seq_rows(x, q):
        if nseq == 1:
            return x
        return jnp.concatenate([x[q * tl:(q + 1) * tl], x[n + q * tl:n + (q + 1) * tl]], axis=0)

    def unit_rows(parts):
        if nseq == 1:
            return parts[0]
        return jnp.concatenate([p[0:tl] for p in parts] + [p[tl:2 * tl] for p in parts], axis=0)

    chains = [(u, p) for u in range(nu) for p in range(RW_HEADS // 2)]
    ids = range(len(chains))
    cat0 = lambda *xs: jnp.concatenate(xs, axis=0)

    def ld(ref, c):
        u, p = chains[c]
        return ref[u * nseq:(u + 1) * nseq, :, p * LANES:(p + 1) * LANES].reshape(n, LANES)

    lw = [ld(lw_ref, c) for c in ids]
    cum = [_mm01(m_cum, x) for x in lw]
    tot = [_mm01(m_seq, x) for x in lw]
    e_c = [jnp.exp(x) for x in cum]
    e_n = [jnp.exp(-x) for x in cum]
    e_l = [jnp.exp(t - x) for t, x in zip(tot, cum)]
    at_d = [dup(ld(a_ref, c) * jnp.exp(cum[c] - lw[c])) for c in ids]
    rt_d = [dup(ld(r_ref, c) * e_c[c]) for c in ids]
    bt_d = [dup(ld(b_ref, c) * e_n[c]) for c in ids]
    kt_d = [dup(ld(k_ref, c) * e_n[c]) for c in ids]
    bh_d = [dup(ld(b_ref, c) * e_l[c]) for c in ids]
    kh_d = [dup(ld(k_ref, c) * e_l[c]) for c in ids]
    v_d = [dup(ld(v_ref, c)) for c in ids]
    aa = [_mm(cat0(at_d[c], rt_d[c]), cat0(bt_d[c], kt_d[c]), "nt", p_aa) for c in ids]
    a_ab = [jnp.where(strict_d, x[0:n2, 0:n2], 0.0) for x in aa]
    a_ak = [jnp.where(strict_d, x[0:n2, n2:], 0.0) for x in aa]
    a_rb = [jnp.where(incl_d, x[n2:, 0:n2], 0.0) for x in aa]
    a_rk = [jnp.where(incl_d, x[n2:, n2:], 0.0) for x in aa]
    zy = [_mm(cat0(a_ak[c], a_rk[c]), v_d[c], passes=p_apply) for c in ids]
    tinv = [eye_d + x for x in a_ab]
    nk = a_ab
    for _ in range(_log2(tl) - 1):
        nk = [_mm(x, x, passes=p_inv) for x in nk]
        tinv = [t + _mm(t, x, passes=p_inv) for t, x in zip(tinv, nk)]
    wu = [_mm(tinv[c], jnp.concatenate([at_d[c], zy[c][0:n2]], axis=1), passes=p_apply) for c in ids]
    seqs = range(nseq)
    srow = lambda c, q: (chains[c][0] * nseq + q, chains[c][1])
    s_old = [[st[srow(c, q)] for q in seqs] for c in ids]
    xs = [[_mm(cat0(seq_rows(wu[c][:, 0:LANES], q), seq_rows(rt_d[c], q)), s_old[c][q], "nt", p_state)
           for q in seqs] for c in ids]
    u_q = [[xs[c][q][0:2 * tl] + seq_rows(wu[c][:, LANES:], q) for q in seqs] for c in ids]
    for c in ids:
        for q in seqs:
            g_c = jnp.exp(tot[c][q * tl:q * tl + 1, :])
            st[srow(c, q)] = s_old[c][q] * g_c + _mm(cat0(u_q[c][q], seq_rows(v_d[c], q)),
                                                     cat0(seq_rows(bh_d[c], q), seq_rows(kh_d[c], q)), "tn", p_state)
    for c in ids:
        u, p = chains[c]
        y_d = (unit_rows([xs[c][q][2 * tl:] for q in seqs]) + _mm(a_rb[c], unit_rows(u_q[c]), passes=p_y)
               + zy[c][n2:])
        y_ref[u * nseq:(u + 1) * nseq, :, p * LANES:(p + 1) * LANES] = (y_d[0:n] + y_d[n:]).reshape(nseq, tl, LANES)

    @pl.when(pl.program_id(1) == pl.num_programs(1) - 1)
    def _():
        sn_ref[...] = st[...]


def _unit_shape(bn, seq):
    if seq >= UNIT:
        assert seq % UNIT == 0
        return 1, UNIT
    assert UNIT % seq == 0 and bn % (UNIT // seq) == 0
    return UNIT // seq, seq


def _rwscan_call(r, lw, k2, v, a_s, b_s, s0_bd, passes=RW_SCAN_PASSES):
    bn, seq, hw = r.shape
    nseq, tl = _unit_shape(bn, seq)
    nu = RW_UNITS_PER_STEP if bn % (RW_UNITS_PER_STEP * nseq) == 0 else 1
    rows = nu * nseq
    tok = pl.BlockSpec((rows, tl, hw), lambda b, c: (b, c, 0))
    stt = pl.BlockSpec((rows, RW_HEADS // 2, LANES, LANES), lambda b, c: (b, 0, 0, 0))
    return pl.pallas_call(
        functools.partial(_rwscan_body, nu=nu, nseq=nseq, tl=tl, passes=passes),
        grid=(bn // rows, seq // tl),
        in_specs=[tok] * 6 + [stt],
        out_specs=[tok, stt],
        out_shape=[jax.ShapeDtypeStruct((bn, seq, hw), F32), jax.ShapeDtypeStruct(s0_bd.shape, F32)],
        scratch_shapes=[pltpu.VMEM((rows, RW_HEADS // 2, LANES, LANES), F32)],
        compiler_params=_cparams(("arbitrary", "arbitrary"), VMEM_LIMIT),
        name="rwkv_scan",
    )(r, lw, k2, v, a_s, b_s, s0_bd)


def _gla_body(qkv_ref, xal_ref, gate_ref, aup_ref, ab_ref, ng_ref, s0_ref, o_ref, sn_ref, st, *, nu, nseq, tl, cs):
    n = nseq * tl
    n2 = 2 * n
    nsub = tl // cs

    @pl.when(pl.program_id(1) == 0)
    def _():
        st[...] = s0_ref[...]

    same, _, incl = _unit_masks(n, cs)
    m_cum = jnp.where(incl, 1.0, 0.0)
    m_sub = jnp.where(same, 1.0, 0.0)
    ri = lax.broadcasted_iota(I32, (n2, n2), 0)
    ci = lax.broadcasted_iota(I32, (n2, n2), 1)
    rt, ct = ri & (n - 1), ci & (n - 1)
    causal_d = ((rt >> _log2(cs)) == (ct >> _log2(cs))) & ((ri >> _log2(n)) == (ci >> _log2(n))) & (rt >= ct)
    lane = lax.broadcasted_iota(I32, (1, LANES), 1)
    m0 = jnp.where(lane < GLA_DK, 1.0, 0.0)
    m1 = 1.0 - m0
    sr = lax.broadcasted_iota(I32, (2 * GLA_DV, LANES), 0)
    sc = lax.broadcasted_iota(I32, (2 * GLA_DV, LANES), 1)
    st_mask = jnp.where((sr >> _log2(GLA_DV)) == (sc >> _log2(GLA_DK)), 1.0, 0.0)

    def dup(x):
        return jnp.concatenate([x * m0, x * m1], axis=0)

    chains = [(u, p) for u in range(nu) for p in range(GLA_HEADS // 2)]
    ids = range(len(chains))
    urows = lambda u: slice(u * nseq, (u + 1) * nseq)
    ng = ng_ref[...]
    la_all = [-_softplus(-(_mm(xal_ref[urows(u), :, :].reshape(n, LANES), aup_ref[...], passes=3) + ab_ref[...]))
              * (1.0 / GLA_GATE_TAU) for u in range(nu)]

    def ld(ref, c, off, width):
        return ref[urows(chains[c][0]), :, off:off + width].reshape(n, width)

    q = [ld(qkv_ref, c, chains[c][1] * LANES, LANES) * (GLA_DK ** -0.5) for c in ids]
    k = [ld(qkv_ref, c, GLA_KW + chains[c][1] * LANES, LANES) for c in ids]
    vp = [ld(qkv_ref, c, 2 * GLA_KW + chains[c][1] * 2 * GLA_DV, 2 * GLA_DV) for c in ids]
    la = [la_all[u][:, p * LANES:(p + 1) * LANES] for u, p in chains]
    bc = [_mm01(m_cum, x) for x in la]
    bl = [_mm01(m_sub, x) for x in la]
    qe = [q[c] * jnp.exp(bc[c]) for c in ids]
    ke = [k[c] * jnp.exp(-bc[c]) for c in ids]
    kd = [k[c] * jnp.exp(bl[c] - bc[c]) for c in ids]
    att = [jnp.where(causal_d, _mm(dup(qe[c]), dup(ke[c]), "nt", passes=1), 0.0) for c in ids]
    v_st = [jnp.concatenate([x[:, 0:GLA_DV], x[:, GLA_DV:]], axis=0) for x in vp]
    o_st = [_mm(att[c], v_st[c], passes=1) for c in ids]
    upd = [[_mm(vp[c][r0:r0 + cs], kd[c][r0:r0 + cs], "tn", passes=1) for r0 in range(0, n, cs)] for c in ids]
    inter = [[None] * (n // cs) for _ in ids]
    for sq in range(nseq):
        s = [st[chains[c][0] * nseq + sq, chains[c][1]] for c in ids]
        for j in range(nsub):
            i = sq * nsub + j
            r0 = i * cs
            for c in ids:
                inter[c][i] = _mm(qe[c][r0:r0 + cs], s[c], "nt", passes=1)
                s[c] = s[c] * jnp.exp(bl[c][r0:r0 + 1, :]) + st_mask * upd[c][i]
        for c in ids:
            st[chains[c][0] * nseq + sq, chains[c][1]] = s[c]
    for c in ids:
        u, p = chains[c]
        o = o_st[c] + jnp.concatenate([x[:, 0:GLA_DV] for x in inter[c]] + [x[:, GLA_DV:] for x in inter[c]], axis=0)
        o = o * lax.rsqrt(jnp.mean(o * o, axis=-1, keepdims=True) + NORM_EPS) * ng
        goff = p * 2 * GLA_DV
        gp = ld(gate_ref, c, goff, 2 * GLA_DV)
        g_st = jnp.concatenate([gp[:, 0:GLA_DV], gp[:, GLA_DV:]], axis=0)
        ob = o * (g_st * _sigmoid(g_st))
        o_ref[urows(u), :, goff:goff + GLA_DV] = ob[0:n].reshape(nseq, tl, GLA_DV)
        o_ref[urows(u), :, goff + GLA_DV:goff + 2 * GLA_DV] = ob[n:].reshape(nseq, tl, GLA_DV)

    @pl.when(pl.program_id(1) == pl.num_programs(1) - 1)
    def _():
        sn_ref[...] = st[...]


def _gla_call(qkv, xal, gate, s0_t, p):
    bn, seq, _ = qkv.shape
    nseq, tl = _unit_shape(bn, seq)
    cs = min(GLA_CHUNK, seq)
    assert tl % cs == 0
    nu = GLA_UNITS_PER_STEP if bn % (GLA_UNITS_PER_STEP * nseq) == 0 else 1
    rows = nu * nseq
    tok = lambda w: pl.BlockSpec((rows, tl, w), lambda b, c: (b, c, 0))
    full = lambda a: pl.BlockSpec(a.shape, lambda b, c: (0,) * a.ndim)
    stt = pl.BlockSpec((rows, GLA_HEADS // 2, 2 * GLA_DV, LANES), lambda b, c: (b, 0, 0, 0))
    consts = (p["gla_aup"], p["gla_ab"], p["gla_ng"])
    return pl.pallas_call(
        functools.partial(_gla_body, nu=nu, nseq=nseq, tl=tl, cs=cs),
        grid=(bn // rows, seq // tl),
        in_specs=[tok(QKV_W), tok(XAL_W), tok(GG_W)] + [full(c) for c in consts] + [stt],
        out_specs=[tok(GLA_VW), stt],
        out_shape=[jax.ShapeDtypeStruct((bn, seq, GLA_VW), F32), jax.ShapeDtypeStruct(s0_t.shape, F32)],
        scratch_shapes=[pltpu.VMEM((rows, GLA_HEADS // 2, 2 * GLA_DV, LANES), F32)],
        compiler_params=_cparams(("arbitrary", "arbitrary"), VMEM_LIMIT),
        name="gla_chunked",
    )(qkv, xal, gate, *consts, s0_t)


def _merge_body(y_ref, g_ref, bon_ref, ob_ref, mg_ref, x_ref, mod_ref, gng_ref, gnb_ref, bd_ref, wpa_ref,
                wpb_ref, wout_ref, n2_ref, rwh_ref, rwl_ref, x1_o, h2_o, lg_o):
    bb, ll, d = x_ref.shape
    n = bb * ll
    hw = RW_WIDTH
    bd = bd_ref[...]
    y = y_ref[...].reshape(n, hw)
    mu = _xmm01(y, bd, pieces=3) * (1.0 / RW_HEAD)
    dv = y - mu
    var = _xmm01(dv * dv, bd) * (1.0 / RW_HEAD)
    yn = dv * lax.rsqrt(var + RW_GN_EPS) * gng_ref[...] + gnb_ref[...]
    o_a = (yn + bon_ref[...].reshape(n, hw)) * g_ref[...].reshape(n, hw)
    o_b = ob_ref[...].reshape(n, GLA_VW)
    mg = mg_ref[...].reshape(n, 2 * d)
    merged = _sigmoid(mg[:, 0:d]) * _mm(o_a, wpa_ref[...]) + _sigmoid(mg[:, d:]) * _mm(o_b, wpb_ref[...])
    mix = _mm(merged, wout_ref[...]).reshape(bb, ll, d)
    x1 = x_ref[...] + mod_ref[:, 2:3, :] * mix
    x1_o[...] = x1
    yn2 = x1 * lax.rsqrt(jnp.mean(x1 * x1, axis=-1, keepdims=True) + NORM_EPS) * n2_ref[...]
    h2 = (yn2 * (1.0 + mod_ref[:, 4:5, :]) + mod_ref[:, 3:4, :]).reshape(n, d)
    hh, hl = _split(h2, 2)
    rwh, rwl = rwh_ref[...], rwl_ref[...]
    lg_o[...] = (jnp.dot(hh, rwh, preferred_element_type=F32) + jnp.dot(hh, rwl, preferred_element_type=F32)
                 + jnp.dot(hl, rwh, preferred_element_type=F32))
    _rows_to_packed(h2_o, h2)


def _merge_call(y, g, bonus, o_b, mg, x, mod, p):
    bn, seq, d = x.shape
    bb, ll = _tile(bn, seq, TOK_TILE)
    nl = seq // ll
    tn = bn * seq
    tok = lambda w: pl.BlockSpec((bb, ll, w), lambda b, l: (b, l, 0))
    full = lambda a: pl.BlockSpec(a.shape, lambda b, l: (0,) * a.ndim)
    consts = (p["gn_g"], p["gn_b"], p["bd64"], p["w_pa"], p["w_pb"], p["w_out"], p["norm2_g"], p["rw_hi"],
              p["rw_lo"])
    return pl.pallas_call(
        _merge_body,
        grid=(bn // bb, nl),
        in_specs=[tok(RW_WIDTH)] * 3 + [tok(GLA_VW), tok(MG_W), tok(d),
                                        pl.BlockSpec((bb, 6, d), lambda b, l: (b, 0, 0))] + [full(c) for c in consts],
        out_specs=[tok(d),
                   pl.BlockSpec((bb * ll * PCH, LANES), lambda b, l: (b * nl + l, 0)),
                   pl.BlockSpec((bb * ll, N_EXPERTS), lambda b, l: (b * nl + l, 0))],
        out_shape=[jax.ShapeDtypeStruct((bn, seq, d), F32),
                   jax.ShapeDtypeStruct((tn * PCH, LANES), I32),
                   jax.ShapeDtypeStruct((tn, N_EXPERTS), F32)],
        compiler_params=_cparams(("arbitrary", "arbitrary"), VMEM_LIMIT),
        name="merge_outproj_router",
    )(y, g, bonus, o_b, mg, x, mod, *consts)


def _route_body(lg_ref, rb_ref, e_o, rk_o, w_o, cnt_o, carry):
    tm = lg_ref.shape[0]
    ne = N_EXPERTS

    @pl.when(pl.program_id(0) == 0)
    def _():
        carry[...] = jnp.zeros_like(carry)

    neg = -jnp.inf
    scores = _sigmoid(lg_ref[...])
    sel = scores + rb_ref[...]
    lane_i = lax.broadcasted_iota(I32, (tm, ne), 1)
    lane = lane_i.astype(F32)
    grp = (lane_i >> _log2(GROUP_SIZE)).astype(F32)

    def first_max(x):
        m = jnp.max(x, axis=-1, keepdims=True)
        idx = jnp.min(jnp.where(x == m, lane, float(ne)), axis=-1, keepdims=True)
        return m, idx

    gs = jnp.full((tm, ne), neg, F32)
    for gidx in range(N_GROUPS):
        sg = jnp.where(grp == float(gidx), sel, neg)
        m1, i1 = first_max(sg)
        m2 = jnp.max(jnp.where(lane == i1, neg, sg), axis=-1, keepdims=True)
        gs = jnp.where(lane == float(gidx), m1 + m2, gs)
    cur = jnp.full((tm, ne), neg, F32)
    for _ in range(TOPK_GROUPS):
        _, gi = first_max(gs)
        cur = jnp.where(grp == gi, sel, cur)
        gs = jnp.where(lane == gi, neg, gs)

    pm = jnp.zeros((tm, ne), F32)
    eidx, wts = [], []
    for _ in range(TOP_K):
        _, ei = first_max(cur)
        hit = lane == ei
        pm = jnp.where(hit, 1.0, pm)
        eidx.append(ei)
        wts.append(jnp.sum(jnp.where(hit, scores, 0.0), axis=-1, keepdims=True))
        cur = jnp.where(hit, neg, cur)
    wsum = wts[0]
    for w in wts[1:]:
        wsum = wsum + w

    ri = lax.broadcasted_iota(I32, (tm, tm), 0)
    ci = lax.broadcasted_iota(I32, (tm, tm), 1)
    below = jnp.where(ri > ci, 1.0, 0.0)
    rank = _mm(below, pm, passes=1) + carry[...]
    carry[...] = carry[...] + jnp.sum(pm, axis=0, keepdims=True)
    cnt_o[...] = carry[...]

    ol = lax.broadcasted_iota(I32, (tm, LANES), 1)
    e_out = jnp.zeros((tm, LANES), I32)
    r_out = jnp.zeros((tm, LANES), I32)
    w_out = jnp.zeros((tm, LANES), F32)
    for kk in range(TOP_K):
        rk = jnp.sum(jnp.where(lane == eidx[kk], rank, 0.0), axis=-1, keepdims=True)
        e_out = jnp.where(ol == kk, eidx[kk].astype(I32), e_out)
        r_out = jnp.where(ol == kk, rk.astype(I32), r_out)
        w_out = jnp.where(ol == kk, wts[kk] / wsum * ROUTED_SCALE, w_out)
    e_o[...] = e_out
    rk_o[...] = r_out
    w_o[...] = w_out


def _route_call(logits, router_b):
    tn, ne = logits.shape
    tm = TOK_TILE
    assert tn % tm == 0
    tok = lambda w: pl.BlockSpec((tm, w), lambda i: (i, 0))
    one = pl.BlockSpec((1, ne), lambda i: (0, 0))
    return pl.pallas_call(
        _route_body,
        grid=(tn // tm,),
        in_specs=[tok(ne), one],
        out_specs=[tok(LANES), tok(LANES), tok(LANES), one],
        out_shape=[jax.ShapeDtypeStruct((tn, LANES), I32), jax.ShapeDtypeStruct((tn, LANES), I32),
                   jax.ShapeDtypeStruct((tn, LANES), F32), jax.ShapeDtypeStruct((1, ne), F32)],
        scratch_shapes=[pltpu.VMEM((1, ne), F32)],
        compiler_params=_cparams(("arbitrary",)),
        name="moe_route",
    )(logits, router_b.reshape(1, ne))


def _dispatch_body(e_ref, rk_ref, ps_ref, h2_ref, xs_hbm, sem, *, tm):
    def issue(m, carry):
        for kk in range(TOP_K):
            j = m * TOP_K + kk
            row = ps_ref[e_ref[0, 0, j]] + rk_ref[0, 0, j]
            pltpu.make_async_copy(_slab(h2_ref, m), _slab(xs_hbm, row), sem).start()
        return carry

    lax.fori_loop(0, tm, issue, 0)
    all_rows = xs_hbm.at[pl.ds(0, tm * TOP_K * PCH)]
    pltpu.make_async_copy(all_rows, all_rows, sem).wait()


def _assign_specs(tm, index_map):
    blk = pl.BlockSpec((1, 1, tm * TOP_K), index_map, memory_space=pltpu.SMEM)
    return blk, pl.BlockSpec(memory_space=pltpu.SMEM)


def _dispatch_call(eidx, rank, pad_start, h2s, n_rows):
    tn = h2s.shape[0] // PCH
    tm = TOK_TILE
    blk, whole = _assign_specs(tm, lambda i: (i, 0, 0))
    shp = (tn // tm, 1, tm * TOP_K)
    return pl.pallas_call(
        functools.partial(_dispatch_body, tm=tm),
        grid=(tn // tm,),
        in_specs=[blk, blk, whole, pl.BlockSpec((tm * PCH, LANES), lambda i: (i, 0))],
        out_specs=pl.BlockSpec(memory_space=pl.ANY),
        out_shape=jax.ShapeDtypeStruct((n_rows * PCH, LANES), I32),
        scratch_shapes=[pltpu.SemaphoreType.DMA],
        compiler_params=_cparams(("arbitrary",)),
        name="moe_dispatch",
    )(eidx.reshape(shp), rank.reshape(shp), pad_start, h2s)


def _expert_body(bi_ref, be_ref, nr_ref, xs_ref, wg_ref, wu_ref, wd_ref, ob_ref):
    del bi_ref, be_ref
    nr = nr_ref[pl.program_id(0)]

    @pl.when(nr > 0)
    def _():
        live = lax.broadcasted_iota(I32, (MOE_BLK, LANES), 0) < nr
        x = _rows_from_packed(xs_ref, MOE_BLK, live)
        hg = jnp.dot(x, wg_ref[0].astype(BF16), preferred_element_type=F32)
        hu = jnp.dot(x, wu_ref[0].astype(BF16), preferred_element_type=F32)
        hh = (hg * _sigmoid(hg) * hu).astype(BF16)
        _rows_to_packed(ob_ref, jnp.dot(hh, wd_ref[0].astype(BF16), preferred_element_type=F32))


def _expert_call(block_i, block_e, block_rows, xs, wg, wu, wd):
    nb = xs.shape[0] // (MOE_BLK * PCH)
    d, ff = wg.shape[1], wg.shape[2]
    rows = pl.BlockSpec((MOE_BLK * PCH, LANES), lambda i, bi, be, nr: (bi[i], 0))
    wspec = lambda a, b: pl.BlockSpec((1, a, b), lambda i, bi, be, nr: (be[i], 0, 0))
    grid_spec = pltpu.PrefetchScalarGridSpec(
        num_scalar_prefetch=3,
        grid=(nb,),
        in_specs=[rows, wspec(d, ff), wspec(d, ff), wspec(ff, d)],
        out_specs=rows,
    )
    return pl.pallas_call(
        _expert_body,
        grid_spec=grid_spec,
        out_shape=jax.ShapeDtypeStruct(xs.shape, I32),
        compiler_params=_cparams(("arbitrary",), VMEM_LIMIT),
        name="moe_experts",
    )(block_i, block_e, block_rows, xs, wg, wu, wd)


def _combine_body(e_ref, rk_ref, wt_ref, ps_ref, ob_hbm, h2_ref, x1_ref, mod_ref, sg_ref, su_ref, sd_ref, fg_ref,
                  out_ref, gbuf, rbuf, sem, *, tm):
    bb, ll, d = x1_ref.shape

    def issue(m, carry):
        for kk in range(TOP_K):
            j = m * TOP_K + kk
            row = ps_ref[e_ref[0, 0, j]] + rk_ref[0, 0, j]
            pltpu.make_async_copy(_slab(ob_hbm, row), _slab(gbuf, j), sem).start()
        return carry

    lax.fori_loop(0, tm, issue, 0)
    pltpu.make_async_copy(ob_hbm.at[pl.ds(0, tm * TOP_K * PCH)], gbuf, sem).wait()

    def mix(m, carry):
        lo, hi = _unpack_pair(_slab(gbuf, m * TOP_K)[...])
        wt = wt_ref[0, 0, m * TOP_K]
        acc_lo, acc_hi = wt * lo, wt * hi
        for kk in range(1, TOP_K):
            lo, hi = _unpack_pair(_slab(gbuf, m * TOP_K + kk)[...])
            wt = wt_ref[0, 0, m * TOP_K + kk]
            acc_lo, acc_hi = acc_lo + wt * lo, acc_hi + wt * hi
        _fslab(rbuf, m)[...] = jnp.concatenate([acc_lo, acc_hi], axis=0)
        return carry

    lax.fori_loop(0, tm, mix, 0)

    routed = jnp.concatenate([rbuf[pl.ds(c, tm, stride=CHUNKS), :] for c in range(CHUNKS)], axis=1)
    h2 = _rows_from_packed(h2_ref, tm)
    hg = jnp.dot(h2, sg_ref[...], preferred_element_type=F32)
    hu = jnp.dot(h2, su_ref[...], preferred_element_type=F32)
    shared = jnp.dot((hg * _sigmoid(hg) * hu).astype(BF16), sd_ref[...], preferred_element_type=F32)
    ff = (routed + shared).reshape(bb, ll, d)
    x2 = x1_ref[...] + mod_ref[:, 5:6, :] * ff
    out_ref[...] = x2 * lax.rsqrt(jnp.mean(x2 * x2, axis=-1, keepdims=True) + NORM_EPS) * fg_ref[...]


def _combine_call(eidx, rank, wts, pad_start, ob, h2s, x1, mod, p):
    bn, seq, d = x1.shape
    tm = CMB_TILE
    bb, ll = _tile(bn, seq, tm)
    nl = seq // ll
    tn = bn * seq
    smem, whole = _assign_specs(tm, lambda b, l: (b * nl + l, 0, 0))
    tok = pl.BlockSpec((bb, ll, d), lambda b, l: (b, l, 0))
    full = lambda a: pl.BlockSpec(a.shape, lambda b, l: (0,) * a.ndim)
    consts = (p["sh_gate"], p["sh_up"], p["sh_down"], p["final_g"])
    shp = (tn // tm, 1, tm * TOP_K)
    return pl.pallas_call(
        functools.partial(_combine_body, tm=tm),
        grid=(bn // bb, nl),
        in_specs=[smem, smem, smem, whole, pl.BlockSpec(memory_space=pl.ANY),
                  pl.BlockSpec((tm * PCH, LANES), lambda b, l: (b * nl + l, 0)),
                  tok, pl.BlockSpec((bb, 6, d), lambda b, l: (b, 0, 0))] + [full(c) for c in consts],
        out_specs=tok,
        out_shape=jax.ShapeDtypeStruct((bn, seq, d), F32),
        scratch_shapes=[pltpu.VMEM((tm * TOP_K * PCH, LANES), I32), pltpu.VMEM((tm * CHUNKS, LANES), F32),
                        pltpu.SemaphoreType.DMA],
        compiler_params=_cparams(("arbitrary", "arbitrary"), VMEM_LIMIT),
        name="moe_combine_final",
    )(eidx.reshape(shp), rank.reshape(shp), wts.reshape(shp), pad_start, ob, h2s, x1, mod, *consts)


def _rw_state_to_pairs(s):
    bn = s.shape[0]
    s = s.reshape(bn, RW_HEADS // 2, 2, RW_HEAD, RW_HEAD)
    z = jnp.zeros_like(s[:, :, 0])
    return jnp.concatenate([jnp.concatenate([s[:, :, 0], z], axis=-1),
                            jnp.concatenate([z, s[:, :, 1]], axis=-1)], axis=-2)


def _rw_state_from_pairs(sp):
    bn = sp.shape[0]
    s = jnp.stack([sp[:, :, :RW_HEAD, :RW_HEAD], sp[:, :, RW_HEAD:, RW_HEAD:]], axis=2)
    return s.reshape(bn, RW_HEADS, RW_HEAD, RW_HEAD)


def _gla_state_to_pairs(s):
    bn = s.shape[0]
    t = jnp.swapaxes(s, -1, -2).reshape(bn, GLA_HEADS // 2, 2, GLA_DV, GLA_DK)
    z = jnp.zeros_like(t[:, :, 0])
    return jnp.concatenate([jnp.concatenate([t[:, :, 0], z], axis=-1),
                            jnp.concatenate([z, t[:, :, 1]], axis=-1)], axis=-2)


def _gla_state_from_pairs(sp):
    bn = sp.shape[0]
    t = jnp.stack([sp[:, :, :GLA_DV, :GLA_DK], sp[:, :, GLA_DV:, GLA_DK:]], axis=2)
    return jnp.swapaxes(t.reshape(bn, GLA_HEADS, GLA_DV, GLA_DK), -1, -2)


def _layer_params(l, ada_w, ada_b, norm1_g, norm2_g, w_in, mu_shift, rw_w0, rw_w_up, rw_a0, rw_a_up, rw_g_up,
                  rw_k_k, rw_k_a, rw_r_k, rw_gn_g, rw_gn_b, gla_a_up, gla_a_bias, gla_norm_g, w_pa, w_pb, w_out,
                  router_w, router_b, exp_gate, exp_up, exp_down, sh_gate, sh_up, sh_down):
    d = D_MODEL
    wi = w_in[l]
    gla0 = RW_SHIFT_COLS
    xal0 = gla0 + QKV_W
    pad = jnp.zeros((d, XAL_W - GLA_GATE_RANK), F32)
    w_pack = jnp.concatenate([wi[:, :xal0], wi[:, xal0:xal0 + GLA_GATE_RANK], pad,
                              wi[:, xal0 + GLA_GATE_RANK:]], axis=1).astype(BF16)
    zr = jnp.zeros((RW_W_RANK, RW_WIDTH), F32)
    hid = jnp.arange(RW_WIDTH) // RW_HEAD
    row = lambda a: a.reshape(1, -1)
    rw_hi = router_w[l].astype(BF16)
    return dict(
        ada_w=ada_w[l], ada_b=ada_b[l], norm1_g=norm1_g[l], norm2_g=norm2_g[l].reshape(1, 1, d), w_pack=w_pack,
        mu=mu_shift[l].reshape(1, 1, -1), w0=row(rw_w0[l]), wup=jnp.concatenate([rw_w_up[l], zr], axis=0),
        a0=row(rw_a0[l]), aup=jnp.concatenate([zr, rw_a_up[l]], axis=0), gup=rw_g_up[l].astype(BF16),
        kk=row(rw_k_k[l]), ka=row(rw_k_a[l]), rk=row(rw_r_k[l]),
        bd64=(hid[:, None] == hid[None, :]).astype(BF16),
        gn_g=row(rw_gn_g[l]), gn_b=row(rw_gn_b[l]),
        gla_aup=jnp.concatenate([gla_a_up[l], jnp.zeros((XAL_W - GLA_GATE_RANK, GLA_KW), F32)], axis=0),
        gla_ab=row(gla_a_bias[l]), gla_ng=row(gla_norm_g[l]),
        w_pa=w_pa[l].astype(BF16), w_pb=w_pb[l].astype(BF16), w_out=w_out[l].astype(BF16),
        rw_hi=rw_hi, rw_lo=(router_w[l] - rw_hi.astype(F32)).astype(BF16), router_b=router_b[l],
        exp_gate=exp_gate[l], exp_up=exp_up[l], exp_down=exp_down[l],
        sh_gate=sh_gate[l].astype(BF16), sh_up=sh_up[l].astype(BF16), sh_down=sh_down[l].astype(BF16),
    )


def _mixer_group(x, mod, s_rw, s_sh, s_gla, p):
    pa, qkv, xal, gg, mg = _inproj_call(x, mod, p["norm1_g"], p["w_pack"])
    r, lw, k2, v, a_s, b_s, g, bonus, new_sh = _rwprep_call(pa, s_sh, p)
    y, rw_new = _rwscan_call(r, lw, k2, v, a_s, b_s, _rw_state_to_pairs(s_rw))
    o_b, gla_new = _gla_call(qkv, xal, gg, _gla_state_to_pairs(s_gla), p)
    x1, h2s, logits = _merge_call(y, g, bonus, o_b, mg, x, mod, p)
    states = (_rw_state_from_pairs(rw_new), new_sh[:, 0, :], _gla_state_from_pairs(gla_new))
    return x1, h2s, logits, states


def _moe(h2s, logits, p):
    tn = h2s.shape[0] // PCH
    eidx, rank, wts, counts = _route_call(logits, p["router_b"])
    eidx, rank, wts = eidx[:, :TOP_K], rank[:, :TOP_K], wts[:, :TOP_K]
    counts = counts[0].astype(I32)
    padded = (counts + MOE_BLK - 1) // MOE_BLK * MOE_BLK
    pad_end = jnp.cumsum(padded)
    pad_start = (pad_end - padded).astype(I32)
    nb = (tn * TOP_K + N_EXPERTS * (MOE_BLK - 1)) // MOE_BLK + 1
    first_row = jnp.arange(nb, dtype=I32) * MOE_BLK
    block_e = jnp.minimum(jnp.searchsorted(pad_end, first_row, side="right"), N_EXPERTS - 1).astype(I32)
    block_rows = jnp.clip(pad_start[block_e] + counts[block_e] - first_row, 0, MOE_BLK).astype(I32)
    block_i = jnp.minimum(jnp.arange(nb, dtype=I32), pad_end[-1] // MOE_BLK - 1).astype(I32)
    xs = _dispatch_call(eidx, rank, pad_start, h2s, nb * MOE_BLK)
    ob = _expert_call(block_i, block_e, block_rows, xs, p["exp_gate"], p["exp_up"], p["exp_down"])
    return ob, (eidx, rank, wts, pad_start)


def kernel(x_prompt, x_sample, c_prompt, c_sample, state_rwkv, state_shift, state_gla, ada_w, ada_b, norm1_g,
           norm2_g, w_in, mu_shift, rw_w0, rw_w_up, rw_a0, rw_a_up, rw_g_up, rw_k_k, rw_k_a, rw_r_k, rw_gn_g,
           rw_gn_b, gla_a_up, gla_a_bias, gla_norm_g, w_pa, w_pb, w_out, router_w, router_b, exp_gate, exp_up,
           exp_down, sh_gate, sh_up, sh_down, final_g):
    depth = ada_w.shape[0]
    bp, bs = x_prompt.shape[0], x_sample.shape[0]
    tp = bp * x_prompt.shape[1]
    xs_g = [x_prompt, x_sample]
    c_all = jnp.concatenate([c_prompt, c_sample], axis=0)
    zeros = lambda shape: jnp.zeros(shape, x_prompt.dtype)
    new_states = [[], []]
    fg = final_g.reshape(1, 1, D_MODEL)
    for l in range(depth):
        p = _layer_params(l, ada_w, ada_b, norm1_g, norm2_g, w_in, mu_shift, rw_w0, rw_w_up, rw_a0, rw_a_up,
                          rw_g_up, rw_k_k, rw_k_a, rw_r_k, rw_gn_g, rw_gn_b, gla_a_up, gla_a_bias, gla_norm_g,
                          w_pa, w_pb, w_out, router_w, router_b, exp_gate, exp_up, exp_down, sh_gate, sh_up,
                          sh_down)
        p["final_g"] = fg
        mod_all = _mod_call(c_all, p["ada_w"], p["ada_b"])
        mods = [mod_all[:bp], mod_all[bp:]]
        states_in = [
            (zeros((bp, RW_HEADS, RW_HEAD, RW_HEAD)), zeros((bp, RW_SHIFT_COLS)),
             zeros((bp, GLA_HEADS, GLA_DK, GLA_DV))),
            (state_rwkv[l], state_shift[l], state_gla[l]),
        ]
        x1s, h2ss, lgs = [], [], []
        for gi in range(2):
            x1, h2s, logits, st = _mixer_group(xs_g[gi], mods[gi], *states_in[gi], p)
            x1s.append(x1)
            h2ss.append(h2s)
            lgs.append(logits)
            new_states[gi].append(st)
        h2_all = jnp.concatenate(h2ss, axis=0)
        ob, (eidx, rank, wts, pad_start) = _moe(h2_all, jnp.concatenate(lgs, axis=0), p)
        assert depth == 1, "the fused final norm assumes a single layer"
        xs_g = [
            _combine_call(eidx[:tp], rank[:tp], wts[:tp], pad_start, ob, h2ss[0], x1s[0], mods[0], p),
            _combine_call(eidx[tp:], rank[tp:], wts[tp:], pad_start, ob, h2ss[1], x1s[1], mods[1], p),
        ]
    stack = lambda gi, j: jnp.stack([s[j] for s in new_states[gi]])
    return (xs_g[0], xs_g[1], stack(0, 0), stack(0, 1), stack(0, 2), stack(1, 0), stack(1, 1), stack(1, 2))
```

```python
import functools

import jax
import jax.numpy as jnp
from jax import lax
from jax.experimental import pallas as pl
from jax.experimental.pallas import tpu as pltpu

F32, BF16, I32 = jnp.float32, jnp.bfloat16, jnp.int32

D_MODEL = 1024
RW_HEADS, RW_HEAD = 8, 64
RW_WIDTH = RW_HEADS * RW_HEAD
RW_W_RANK, RW_A_RANK, RW_G_RANK = 64, 64, 128
RW_GN_EPS = 64e-5
GLA_HEADS, GLA_DK, GLA_DV = 4, 64, 128
GLA_KW, GLA_VW = GLA_HEADS * GLA_DK, GLA_HEADS * GLA_DV
GLA_GATE_RANK = 16
GLA_GATE_TAU = 16.0
GLA_CHUNK = 16
RW_SHIFT_COLS = 3 * RW_WIDTH + RW_W_RANK + RW_A_RANK + RW_G_RANK
N_EXPERTS, TOP_K, N_GROUPS, TOPK_GROUPS = 256, 8, 8, 4
GROUP_SIZE = N_EXPERTS // N_GROUPS
EXPERT_FF = 256
ROUTED_SCALE = 2.5
NORM_EPS = 1e-6

LANES = 128
SUBLANES = 8
CHUNKS = D_MODEL // LANES
PCH = CHUNKS // 2
UNIT = 64
RW_SCAN_PASSES = (1, 1, 1, 1, 1)
GLA_UNITS_PER_STEP = 4
RW_UNITS_PER_STEP = 4
VMEM_LIMIT = 56 * 1024 * 1024

PA_W, QKV_W, XAL_W, GG_W, MG_W = RW_SHIFT_COLS, 2 * GLA_KW + GLA_VW, LANES, GLA_VW, 2 * D_MODEL
PACK_OFFS = (0, PA_W, PA_W + QKV_W, PA_W + QKV_W + XAL_W, PA_W + QKV_W + XAL_W + GG_W)
PACK_W = PA_W + QKV_W + XAL_W + GG_W + MG_W

TOK_TILE = 256
MOE_BLK = 256
CMB_TILE = 128

_DN = {
    "nn": (((1,), (0,)), ((), ())),
    "nt": (((1,), (1,)), ((), ())),
    "tn": (((0,), (0,)), ((), ())),
}


def _split(x, pieces):
    out, rem = [], x
    for i in range(pieces):
        p = rem.astype(BF16)
        out.append(p)
        if i + 1 < pieces:
            rem = rem - p.astype(F32)
    return out


def _mm(a, b, form="nn", passes=1):
    dn = _DN[form]
    if passes == 6:
        return lax.dot_general(a.astype(F32), b.astype(F32), dn, precision=lax.Precision.HIGHEST,
                               preferred_element_type=F32)
    if passes == 1:
        return lax.dot_general(a.astype(BF16), b.astype(BF16), dn, preferred_element_type=F32)
    ah, al = _split(a, 2)
    bh, bl = _split(b, 2)
    out = lax.dot_general(ah, bh, dn, preferred_element_type=F32)
    out = out + lax.dot_general(ah, bl, dn, preferred_element_type=F32)
    return out + lax.dot_general(al, bh, dn, preferred_element_type=F32)


def _mm01(m01, x, pieces=3):
    m = m01.astype(BF16)
    out = None
    for p in _split(x, pieces):
        t = lax.dot_general(m, p, _DN["nn"], preferred_element_type=F32)
        out = t if out is None else out + t
    return out


def _xmm01(x, m01, pieces=2):
    m = m01.astype(BF16)
    out = None
    for p in _split(x, pieces):
        t = lax.dot_general(p, m, _DN["nn"], preferred_element_type=F32)
        out = t if out is None else out + t
    return out


HI16 = -65536


def _bf16_bits(x):
    return lax.bitcast_convert_type(x.astype(BF16).astype(F32), I32)


def _unpack_pair(w):
    return lax.bitcast_convert_type(w << 16, F32), lax.bitcast_convert_type(w & HI16, F32)


def _rows_to_packed(ref, x):
    for c in range(PCH):
        lo = _bf16_bits(x[:, c * LANES:(c + 1) * LANES])
        hi = _bf16_bits(x[:, (c + PCH) * LANES:(c + PCH + 1) * LANES])
        ref[pl.ds(c, x.shape[0], stride=PCH), :] = ((lo >> 16) & 0xFFFF) | (hi & HI16)


def _rows_from_packed(ref, n, live=None):
    lows, highs = [], []
    for c in range(PCH):
        w = ref[pl.ds(c, n, stride=PCH), :]
        if live is not None:
            w = jnp.where(live, w, 0)
        lo, hi = _unpack_pair(w)
        lows.append(lo.astype(BF16))
        highs.append(hi.astype(BF16))
    return jnp.concatenate(lows + highs, axis=1)


def _slab(ref, row):
    return ref.at[pl.ds(pl.multiple_of(row * PCH, PCH), PCH)]


def _fslab(ref, row):
    return ref.at[pl.ds(pl.multiple_of(row * CHUNKS, CHUNKS), CHUNKS)]


def _sigmoid(x):
    return 1.0 / (1.0 + jnp.exp(-x))


def _softplus(x):
    return jnp.maximum(x, 0.0) + jnp.log(1.0 + jnp.exp(-jnp.abs(x)))


def _log2(n):
    assert n > 0 and n & (n - 1) == 0, n
    return n.bit_length() - 1


def _cparams(sem, vmem=None):
    return pltpu.CompilerParams(dimension_semantics=sem, vmem_limit_bytes=vmem)


def _mod_body(c_ref, w_ref, b_ref, o_ref):
    c = c_ref[...]
    o_ref[0] = _mm(c * _sigmoid(c), w_ref[...], passes=6) + b_ref[...]


def _mod_call(c_all, ada_w, ada_b):
    bt, d = c_all.shape
    out = pl.pallas_call(
        _mod_body,
        grid=(6,),
        in_specs=[pl.BlockSpec((bt, d), lambda k: (0, 0)),
                  pl.BlockSpec((d, d), lambda k: (0, k)),
                  pl.BlockSpec((1, d), lambda k: (0, k))],
        out_specs=pl.BlockSpec((1, bt, d), lambda k: (k, 0, 0)),
        out_shape=jax.ShapeDtypeStruct((6, bt, d), F32),
        compiler_params=_cparams(("arbitrary",)),
        name="adaln_mod",
    )(c_all, ada_w, ada_b.reshape(1, 6 * d))
    return jnp.transpose(out, (1, 0, 2))


def _inproj_body(x_ref, mod_ref, g_ref, w_ref, pa_ref, qkv_ref, xal_ref, gg_ref, mg_ref):
    bb, ll, d = x_ref.shape
    x = x_ref[...]
    y = x * lax.rsqrt(jnp.mean(x * x, axis=-1, keepdims=True) + NORM_EPS) * g_ref[...]
    h = y * (1.0 + mod_ref[:, 1:2, :]) + mod_ref[:, 0:1, :]
    hb = h.reshape(bb * ll, d).astype(BF16)
    for ref, off in zip((pa_ref, qkv_ref, xal_ref, gg_ref, mg_ref), PACK_OFFS):
        w = ref.shape[-1]
        ref[...] = jnp.dot(hb, w_ref[:, off:off + w], preferred_element_type=F32).reshape(bb, ll, w)


def _tile(bn, seq, tile):
    if seq >= tile:
        assert seq % tile == 0
        return 1, tile
    assert tile % seq == 0 and bn % (tile // seq) == 0
    return tile // seq, seq


def _inproj_call(x, mod, norm_g, w_pack):
    bn, seq, d = x.shape
    bb, ll = _tile(bn, seq, TOK_TILE)
    tok = lambda w: pl.BlockSpec((bb, ll, w), lambda b, l: (b, l, 0))
    widths = (PA_W, QKV_W, XAL_W, GG_W, MG_W)
    return pl.pallas_call(
        _inproj_body,
        grid=(bn // bb, seq // ll),
        in_specs=[tok(d),
                  pl.BlockSpec((bb, 6, d), lambda b, l: (b, 0, 0)),
                  pl.BlockSpec((1, 1, d), lambda b, l: (0, 0, 0)),
                  pl.BlockSpec((d, PACK_W), lambda b, l: (0, 0))],
        out_specs=[tok(w) for w in widths],
        out_shape=[jax.ShapeDtypeStruct((bn, seq, w), F32) for w in widths],
        compiler_params=_cparams(("arbitrary", "arbitrary"), VMEM_LIMIT),
        name="norm_inproj",
    )(x, mod, norm_g.reshape(1, 1, d), w_pack)


def _rwprep_body(pa_ref, sh_ref, mu_ref, w0_ref, wup_ref, a0_ref, aup_ref, gup_ref, kk_ref, ka_ref, rk_ref,
                 bd_ref, r_o, lw_o, k_o, v_o, a_o, b_o, g_o, bon_o, nsh_o, carry):
    bb, ll, wd = pa_ref.shape
    n = bb * ll
    hw = RW_WIDTH

    @pl.when(pl.program_id(1) == 0)
    def _():
        carry[...] = sh_ref[...]

    pa = pa_ref[...]
    rolled = pltpu.roll(pa.reshape(n, wd), 1, 0).reshape(bb, ll, wd)
    tok = lax.broadcasted_iota(I32, (bb, ll, wd), 1)
    prev = jnp.where(tok == 0, carry[...], rolled)
    last = pa_ref[:, ll - 1:ll, :]
    carry[...] = last
    nsh_o[...] = last
    xs = (pa + (prev - pa) * mu_ref[...]).reshape(n, wd)

    r, k, v = xs[:, 0:hw], xs[:, hw:2 * hw], xs[:, 2 * hw:3 * hw]
    xwa = xs[:, 3 * hw:3 * hw + LANES]
    xg = xs[:, 3 * hw + LANES:]
    w_log = -_softplus(-(w0_ref[...] + _mm(jnp.tanh(xwa), wup_ref[...], passes=3))) - 0.5
    lw = -jnp.exp(w_log)
    a = _sigmoid(a0_ref[...] + _mm(xwa, aup_ref[...], passes=3))
    g = _mm(_sigmoid(xg), gup_ref[...])
    bd = bd_ref[...]
    kkv = k * kk_ref[...]
    kkn = kkv * lax.rsqrt(jnp.maximum(_xmm01(kkv * kkv, bd), 1e-24))
    k2 = k * (1.0 + (a - 1.0) * ka_ref[...])
    bonus = _xmm01(r * k2 * rk_ref[...], bd) * v
    for ref, val in ((r_o, r), (lw_o, lw), (k_o, k2), (v_o, v), (a_o, -kkn), (b_o, kkn * a), (g_o, g),
                     (bon_o, bonus)):
        ref[...] = val.reshape(bb, ll, hw)


def _rwprep_call(pa, s_sh, p):
    bn, seq, wd = pa.shape
    bb, ll = _tile(bn, seq, TOK_TILE)
    hw = RW_WIDTH
    tok = lambda w: pl.BlockSpec((bb, ll, w), lambda b, l: (b, l, 0))
    row = lambda w: pl.BlockSpec((bb, 1, w), lambda b, l: (b, 0, 0))
    full = lambda a: pl.BlockSpec(a.shape, lambda b, l: (0,) * a.ndim)
    consts = (p["mu"], p["w0"], p["wup"], p["a0"], p["aup"], p["gup"], p["kk"], p["ka"], p["rk"], p["bd64"])
    outs = pl.pallas_call(
        _rwprep_body,
        grid=(bn // bb, seq // ll),
        in_specs=[tok(wd), row(wd)] + [full(c) for c in consts],
        out_specs=[tok(hw)] * 8 + [row(wd)],
        out_shape=[jax.ShapeDtypeStruct((bn, seq, hw), F32)] * 8 + [jax.ShapeDtypeStruct((bn, 1, wd), F32)],
        scratch_shapes=[pltpu.VMEM((bb, 1, wd), F32)],
        compiler_params=_cparams(("arbitrary", "arbitrary"), VMEM_LIMIT),
        name="rwkv_prep",
    )(pa, s_sh.reshape(bn, 1, wd), *consts)
    return outs


def _unit_masks(n, tl):
    ri = lax.broadcasted_iota(I32, (n, n), 0)
    ci = lax.broadcasted_iota(I32, (n, n), 1)
    same = (ri >> _log2(tl)) == (ci >> _log2(tl))
    return same, same & (ri > ci), same & (ri >= ci)


def _rwscan_body(r_ref, lw_ref, k_ref, v_ref, a_ref, b_ref, s0_ref, y_ref, sn_ref, st, *, nu, nseq, tl, passes):
    n = nseq * tl
    n2 = 2 * n
    p_aa, p_inv, p_apply, p_state, p_y = passes

    @pl.when(pl.program_id(1) == 0)
    def _():
        st[...] = s0_ref[...]

    same, _, incl = _unit_masks(n, tl)
    m_cum = jnp.where(incl, 1.0, 0.0)
    m_seq = jnp.where(same, 1.0, 0.0)
    ri = lax.broadcasted_iota(I32, (n2, n2), 0)
    ci = lax.broadcasted_iota(I32, (n2, n2), 1)
    rt, ct = ri & (n - 1), ci & (n - 1)
    dsame = ((rt >> _log2(tl)) == (ct >> _log2(tl))) & ((ri >> _log2(n)) == (ci >> _log2(n)))
    strict_d = dsame & (rt > ct)
    incl_d = dsame & (rt >= ct)
    eye_d = jnp.where(ri == ci, 1.0, 0.0)
    lane = lax.broadcasted_iota(I32, (1, LANES), 1)
    m0 = jnp.where(lane < RW_HEAD, 1.0, 0.0)
    m1 = 1.0 - m0

    def dup(x):
        return jnp.concatenate([x * m0, x * m1], axis=0)

    def seq_rows(x, q):
        if nseq == 1:
            return x
        return jnp.concatenate([x[q * tl:(q + 1) * tl], x[n + q * tl:n + (q + 1) * tl]], axis=0)

    def unit_rows(parts):
        if nseq == 1:
            return parts[0]
        return jnp.concatenate([p[0:tl] for p in parts] + [p[tl:2 * tl] for p in parts], axis=0)

    chains = [(u, p) for u in range(nu) for p in range(RW_HEADS // 2)]
    ids = range(len(chains))
    cat0 = lambda *xs: jnp.concatenate(xs, axis=0)

    def ld(ref, c):
        u, p = chains[c]
        return ref[u * nseq:(u + 1) * nseq, :, p * LANES:(p + 1) * LANES].reshape(n, LANES)

    lw = [ld(lw_ref, c) for c in ids]
    cum = [_mm01(m_cum, x) for x in lw]
    tot = [_mm01(m_seq, x) for x in lw]
    e_c = [jnp.exp(x) for x in cum]
    e_n = [jnp.exp(-x) for x in cum]
    e_l = [jnp.exp(t - x) for t, x in zip(tot, cum)]
    at_d = [dup(ld(a_ref, c) * jnp.exp(cum[c] - lw[c])) for c in ids]
    rt_d = [dup(ld(r_ref, c) * e_c[c]) for c in ids]
    bt_d = [dup(ld(b_ref, c) * e_n[c]) for c in ids]
    kt_d = [dup(ld(k_ref, c) * e_n[c]) for c in ids]
    bh_d = [dup(ld(b_ref, c) * e_l[c]) for c in ids]
    kh_d = [dup(ld(k_ref, c) * e_l[c]) for c in ids]
    v_d = [dup(ld(v_ref, c)) for c in ids]
    aa = [_mm(cat0(at_d[c], rt_d[c]), cat0(bt_d[c], kt_d[c]), "nt", p_aa) for c in ids]
    a_ab = [jnp.where(strict_d, x[0:n2, 0:n2], 0.0) for x in aa]
    a_ak = [jnp.where(strict_d, x[0:n2, n2:], 0.0) for x in aa]
    a_rb = [jnp.where(incl_d, x[n2:, 0:n2], 0.0) for x in aa]
    a_rk = [jnp.where(incl_d, x[n2:, n2:], 0.0) for x in aa]
    zy = [_mm(cat0(a_ak[c], a_rk[c]), v_d[c], passes=p_apply) for c in ids]
    tinv = [eye_d + x for x in a_ab]
    nk = a_ab
    for _ in range(_log2(tl) - 1):
        nk = [_mm(x, x, passes=p_inv) for x in nk]
        tinv = [t + _mm(t, x, passes=p_inv) for t, x in zip(tinv, nk)]
    wu = [_mm(tinv[c], jnp.concatenate([at_d[c], zy[c][0:n2]], axis=1), passes=p_apply) for c in ids]
    seqs = range(nseq)
    srow = lambda c, q: (chains[c][0] * nseq + q, chains[c][1])
    s_old = [[st[srow(c, q)] for q in seqs] for c in ids]
    xs = [[_mm(cat0(seq_rows(wu[c][:, 0:LANES], q), seq_rows(rt_d[c], q)), s_old[c][q], "nt", p_state)
           for q in seqs] for c in ids]
    u_q = [[xs[c][q][0:2 * tl] + seq_rows(wu[c][:, LANES:], q) for q in seqs] for c in ids]
    for c in ids:
        for q in seqs:
            g_c = jnp.exp(tot[c][q * tl:q * tl + 1, :])
            st[srow(c, q)] = s_old[c][q] * g_c + _mm(cat0(u_q[c][q], seq_rows(v_d[c], q)),
                                                     cat0(seq_rows(bh_d[c], q), seq_rows(kh_d[c], q)), "tn", p_state)
    for c in ids:
        u, p = chains[c]
        y_d = (unit_rows([xs[c][q][2 * tl:] for q in seqs]) + _mm(a_rb[c], unit_rows(u_q[c]), passes=p_y)
               + zy[c][n2:])
        y_ref[u * nseq:(u + 1) * nseq, :, p * LANES:(p + 1) * LANES] = (y_d[0:n] + y_d[n:]).reshape(nseq, tl, LANES)

    @pl.when(pl.program_id(1) == pl.num_programs(1) - 1)
    def _():
        sn_ref[...] = st[...]


def _unit_shape(bn, seq):
    if seq >= UNIT:
        assert seq % UNIT == 0
        return 1, UNIT
    assert UNIT % seq == 0 and bn % (UNIT // seq) == 0
    return UNIT // seq, seq


def _rwscan_call(r, lw, k2, v, a_s, b_s, s0_bd, passes=RW_SCAN_PASSES):
    bn, seq, hw = r.shape
    nseq, tl = _unit_shape(bn, seq)
    nu = RW_UNITS_PER_STEP if bn % (RW_UNITS_PER_STEP * nseq) == 0 else 1
    rows = nu * nseq
    tok = pl.BlockSpec((rows, tl, hw), lambda b, c: (b, c, 0))
    stt = pl.BlockSpec((rows, RW_HEADS // 2, LANES, LANES), lambda b, c: (b, 0, 0, 0))
    return pl.pallas_call(
        functools.partial(_rwscan_body, nu=nu, nseq=nseq, tl=tl, passes=passes),
        grid=(bn // rows, seq // tl),
        in_specs=[tok] * 6 + [stt],
        out_specs=[tok, stt],
        out_shape=[jax.ShapeDtypeStruct((bn, seq, hw), F32), jax.ShapeDtypeStruct(s0_bd.shape, F32)],
        scratch_shapes=[pltpu.VMEM((rows, RW_HEADS // 2, LANES, LANES), F32)],
        compiler_params=_cparams(("arbitrary", "arbitrary"), VMEM_LIMIT),
        name="rwkv_scan",
    )(r, lw, k2, v, a_s, b_s, s0_bd)


def _gla_body(qkv_ref, xal_ref, gate_ref, aup_ref, ab_ref, ng_ref, s0_ref, o_ref, sn_ref, st, *, nu, nseq, tl, cs):
    n = nseq * tl
    n2 = 2 * n
    nsub = tl // cs

    @pl.when(pl.program_id(1) == 0)
    def _():
        st[...] = s0_ref[...]

    same, _, incl = _unit_masks(n, cs)
    m_cum = jnp.where(incl, 1.0, 0.0)
    m_sub = jnp.where(same, 1.0, 0.0)
    ri = lax.broadcasted_iota(I32, (n2, n2), 0)
    ci = lax.broadcasted_iota(I32, (n2, n2), 1)
    rt, ct = ri & (n - 1), ci & (n - 1)
    causal_d = ((rt >> _log2(cs)) == (ct >> _log2(cs))) & ((ri >> _log2(n)) == (ci >> _log2(n))) & (rt >= ct)
    lane = lax.broadcasted_iota(I32, (1, LANES), 1)
    m0 = jnp.where(lane < GLA_DK, 1.0, 0.0)
    m1 = 1.0 - m0
    sr = lax.broadcasted_iota(I32, (2 * GLA_DV, LANES), 0)
    sc = lax.broadcasted_iota(I32, (2 * GLA_DV, LANES), 1)
    st_mask = jnp.where((sr >> _log2(GLA_DV)) == (sc >> _log2(GLA_DK)), 1.0, 0.0)

    def dup(x):
        return jnp.concatenate([x * m0, x * m1], axis=0)

    chains = [(u, p) for u in range(nu) for p in range(GLA_HEADS // 2)]
    ids = range(len(chains))
    urows = lambda u: slice(u * nseq, (u + 1) * nseq)
    ng = ng_ref[...]
    la_all = [-_softplus(-(_mm(xal_ref[urows(u), :, :].reshape(n, LANES), aup_ref[...], passes=3) + ab_ref[...]))
              * (1.0 / GLA_GATE_TAU) for u in range(nu)]

    def ld(ref, c, off, width):
        return ref[urows(chains[c][0]), :, off:off + width].reshape(n, width)

    q = [ld(qkv_ref, c, chains[c][1] * LANES, LANES) * (GLA_DK ** -0.5) for c in ids]
    k = [ld(qkv_ref, c, GLA_KW + chains[c][1] * LANES, LANES) for c in ids]
    vp = [ld(qkv_ref, c, 2 * GLA_KW + chains[c][1] * 2 * GLA_DV, 2 * GLA_DV) for c in ids]
    la = [la_all[u][:, p * LANES:(p + 1) * LANES] for u, p in chains]
    bc = [_mm01(m_cum, x) for x in la]
    bl = [_mm01(m_sub, x) for x in la]
    qe = [q[c] * jnp.exp(bc[c]) for c in ids]
    ke = [k[c] * jnp.exp(-bc[c]) for c in ids]
    kd = [k[c] * jnp.exp(bl[c] - bc[c]) for c in ids]
    att = [jnp.where(causal_d, _mm(dup(qe[c]), dup(ke[c]), "nt", passes=1), 0.0) for c in ids]
    v_st = [jnp.concatenate([x[:, 0:GLA_DV], x[:, GLA_DV:]], axis=0) for x in vp]
    o_st = [_mm(att[c], v_st[c], passes=1) for c in ids]
    upd = [[_mm(vp[c][r0:r0 + cs], kd[c][r0:r0 + cs], "tn", passes=1) for r0 in range(0, n, cs)] for c in ids]
    inter = [[None] * (n // cs) for _ in ids]
    for sq in range(nseq):
        s = [st[chains[c][0] * nseq + sq, chains[c][1]] for c in ids]
        for j in range(nsub):
            i = sq * nsub + j
            r0 = i * cs
            for c in ids:
                inter[c][i] = _mm(qe[c][r0:r0 + cs], s[c], "nt", passes=1)
                s[c] = s[c] * jnp.exp(bl[c][r0:r0 + 1, :]) + st_mask * upd[c][i]
        for c in ids:
            st[chains[c][0] * nseq + sq, chains[c][1]] = s[c]
    for c in ids:
        u, p = chains[c]
        o = o_st[c] + jnp.concatenate([x[:, 0:GLA_DV] for x in inter[c]] + [x[:, GLA_DV:] for x in inter[c]], axis=0)
        o = o * lax.rsqrt(jnp.mean(o * o, axis=-1, keepdims=True) + NORM_EPS) * ng
        goff = p * 2 * GLA_DV
        gp = ld(gate_ref, c, goff, 2 * GLA_DV)
        g_st = jnp.concatenate([gp[:, 0:GLA_DV], gp[:, GLA_DV:]], axis=0)
        ob = o * (g_st * _sigmoid(g_st))
        o_ref[urows(u), :, goff:goff + GLA_DV] = ob[0:n].reshape(nseq, tl, GLA_DV)
        o_ref[urows(u), :, goff + GLA_DV:goff + 2 * GLA_DV] = ob[n:].reshape(nseq, tl, GLA_DV)

    @pl.when(pl.program_id(1) == pl.num_programs(1) - 1)
    def _():
        sn_ref[...] = st[...]


def _gla_call(qkv, xal, gate, s0_t, p):
    bn, seq, _ = qkv.shape
    nseq, tl = _unit_shape(bn, seq)
    cs = min(GLA_CHUNK, seq)
    assert tl % cs == 0
    nu = GLA_UNITS_PER_STEP if bn % (GLA_UNITS_PER_STEP * nseq) == 0 else 1
    rows = nu * nseq
    tok = lambda w: pl.BlockSpec((rows, tl, w), lambda b, c: (b, c, 0))
    full = lambda a: pl.BlockSpec(a.shape, lambda b, c: (0,) * a.ndim)
    stt = pl.BlockSpec((rows, GLA_HEADS // 2, 2 * GLA_DV, LANES), lambda b, c: (b, 0, 0, 0))
    consts = (p["gla_aup"], p["gla_ab"], p["gla_ng"])
    return pl.pallas_call(
        functools.partial(_gla_body, nu=nu, nseq=nseq, tl=tl, cs=cs),
        grid=(bn // rows, seq // tl),
        in_specs=[tok(QKV_W), tok(XAL_W), tok(GG_W)] + [full(c) for c in consts] + [stt],
        out_specs=[tok(GLA_VW), stt],
        out_shape=[jax.ShapeDtypeStruct((bn, seq, GLA_VW), F32), jax.ShapeDtypeStruct(s0_t.shape, F32)],
        scratch_shapes=[pltpu.VMEM((rows, GLA_HEADS // 2, 2 * GLA_DV, LANES), F32)],
        compiler_params=_cparams(("arbitrary", "arbitrary"), VMEM_LIMIT),
        name="gla_chunked",
    )(qkv, xal, gate, *consts, s0_t)


def _merge_body(y_ref, g_ref, bon_ref, ob_ref, mg_ref, x_ref, mod_ref, gng_ref, gnb_ref, bd_ref, wpa_ref,
                wpb_ref, wout_ref, n2_ref, rwh_ref, rwl_ref, x1_o, h2_o, lg_o):
    bb, ll, d = x_ref.shape
    n = bb * ll
    hw = RW_WIDTH
    bd = bd_ref[...]
    y = y_ref[...].reshape(n, hw)
    mu = _xmm01(y, bd, pieces=3) * (1.0 / RW_HEAD)
    dv = y - mu
    var = _xmm01(dv * dv, bd) * (1.0 / RW_HEAD)
    yn = dv * lax.rsqrt(var + RW_GN_EPS) * gng_ref[...] + gnb_ref[...]
    o_a = (yn + bon_ref[...].reshape(n, hw)) * g_ref[...].reshape(n, hw)
    o_b = ob_ref[...].reshape(n, GLA_VW)
    mg = mg_ref[...].reshape(n, 2 * d)
    merged = _sigmoid(mg[:, 0:d]) * _mm(o_a, wpa_ref[...]) + _sigmoid(mg[:, d:]) * _mm(o_b, wpb_ref[...])
    mix = _mm(merged, wout_ref[...]).reshape(bb, ll, d)
    x1 = x_ref[...] + mod_ref[:, 2:3, :] * mix
    x1_o[...] = x1
    yn2 = x1 * lax.rsqrt(jnp.mean(x1 * x1, axis=-1, keepdims=True) + NORM_EPS) * n2_ref[...]
    h2 = (yn2 * (1.0 + mod_ref[:, 4:5, :]) + mod_ref[:, 3:4, :]).reshape(n, d)
    hh, hl = _split(h2, 2)
    rwh, rwl = rwh_ref[...], rwl_ref[...]
    lg_o[...] = (jnp.dot(hh, rwh, preferred_element_type=F32) + jnp.dot(hh, rwl, preferred_element_type=F32)
                 + jnp.dot(hl, rwh, preferred_element_type=F32))
    _rows_to_packed(h2_o, h2)


def _merge_call(y, g, bonus, o_b, mg, x, mod, p):
    bn, seq, d = x.shape
    bb, ll = _tile(bn, seq, TOK_TILE)
    nl = seq // ll
    tn = bn * seq
    tok = lambda w: pl.BlockSpec((bb, ll, w), lambda b, l: (b, l, 0))
    full = lambda a: pl.BlockSpec(a.shape, lambda b, l: (0,) * a.ndim)
    consts = (p["gn_g"], p["gn_b"], p["bd64"], p["w_pa"], p["w_pb"], p["w_out"], p["norm2_g"], p["rw_hi"],
              p["rw_lo"])
    return pl.pallas_call(
        _merge_body,
        grid=(bn // bb, nl),
        in_specs=[tok(RW_WIDTH)] * 3 + [tok(GLA_VW), tok(MG_W), tok(d),
                                        pl.BlockSpec((bb, 6, d), lambda b, l: (b, 0, 0))] + [full(c) for c in consts],
        out_specs=[tok(d),
                   pl.BlockSpec((bb * ll * PCH, LANES), lambda b, l: (b * nl + l, 0)),
                   pl.BlockSpec((bb * ll, N_EXPERTS), lambda b, l: (b * nl + l, 0))],
        out_shape=[jax.ShapeDtypeStruct((bn, seq, d), F32),
                   jax.ShapeDtypeStruct((tn * PCH, LANES), I32),
                   jax.ShapeDtypeStruct((tn, N_EXPERTS), F32)],
        compiler_params=_cparams(("arbitrary", "arbitrary"), VMEM_LIMIT),
        name="merge_outproj_router",
    )(y, g, bonus, o_b, mg, x, mod, *consts)


def _route_body(lg_ref, rb_ref, e_o, rk_o, w_o, cnt_o, carry):
    tm = lg_ref.shape[0]
    ne = N_EXPERTS

    @pl.when(pl.program_id(0) == 0)
    def _():
        carry[...] = jnp.zeros_like(carry)

    neg = -jnp.inf
    scores = _sigmoid(lg_ref[...])
    sel = scores + rb_ref[...]
    lane_i = lax.broadcasted_iota(I32, (tm, ne), 1)
    lane = lane_i.astype(F32)
    grp = (lane_i >> _log2(GROUP_SIZE)).astype(F32)

    def first_max(x):
        m = jnp.max(x, axis=-1, keepdims=True)
        idx = jnp.min(jnp.where(x == m, lane, float(ne)), axis=-1, keepdims=True)
        return m, idx

    gs = jnp.full((tm, ne), neg, F32)
    for gidx in range(N_GROUPS):
        sg = jnp.where(grp == float(gidx), sel, neg)
        m1, i1 = first_max(sg)
        m2 = jnp.max(jnp.where(lane == i1, neg, sg), axis=-1, keepdims=True)
        gs = jnp.where(lane == float(gidx), m1 + m2, gs)
    cur = jnp.full((tm, ne), neg, F32)
    for _ in range(TOPK_GROUPS):
        _, gi = first_max(gs)
        cur = jnp.where(grp == gi, sel, cur)
        gs = jnp.where(lane == gi, neg, gs)

    pm = jnp.zeros((tm, ne), F32)
    eidx, wts = [], []
    for _ in range(TOP_K):
        _, ei = first_max(cur)
        hit = lane == ei
        pm = jnp.where(hit, 1.0, pm)
        eidx.append(ei)
        wts.append(jnp.sum(jnp.where(hit, scores, 0.0), axis=-1, keepdims=True))
        cur = jnp.where(hit, neg, cur)
    wsum = wts[0]
    for w in wts[1:]:
        wsum = wsum + w

    ri = lax.broadcasted_iota(I32, (tm, tm), 0)
    ci = lax.broadcasted_iota(I32, (tm, tm), 1)
    below = jnp.where(ri > ci, 1.0, 0.0)
    rank = _mm(below, pm, passes=1) + carry[...]
    carry[...] = carry[...] + jnp.sum(pm, axis=0, keepdims=True)
    cnt_o[...] = carry[...]

    ol = lax.broadcasted_iota(I32, (tm, LANES), 1)
    e_out = jnp.zeros((tm, LANES), I32)
    r_out = jnp.zeros((tm, LANES), I32)
    w_out = jnp.zeros((tm, LANES), F32)
    for kk in range(TOP_K):
        rk = jnp.sum(jnp.where(lane == eidx[kk], rank, 0.0), axis=-1, keepdims=True)
        e_out = jnp.where(ol == kk, eidx[kk].astype(I32), e_out)
        r_out = jnp.where(ol == kk, rk.astype(I32), r_out)
        w_out = jnp.where(ol == kk, wts[kk] / wsum * ROUTED_SCALE, w_out)
    e_o[...] = e_out
    rk_o[...] = r_out
    w_o[...] = w_out


def _route_call(logits, router_b):
    tn, ne = logits.shape
    tm = TOK_TILE
    assert tn % tm == 0
    tok = lambda w: pl.BlockSpec((tm, w), lambda i: (i, 0))
    one = pl.BlockSpec((1, ne), lambda i: (0, 0))
    return pl.pallas_call(
        _route_body,
        grid=(tn // tm,),
        in_specs=[tok(ne), one],
        out_specs=[tok(LANES), tok(LANES), tok(LANES), one],
        out_shape=[jax.ShapeDtypeStruct((tn, LANES), I32), jax.ShapeDtypeStruct((tn, LANES), I32),
                   jax.ShapeDtypeStruct((tn, LANES), F32), jax.ShapeDtypeStruct((1, ne), F32)],
        scratch_shapes=[pltpu.VMEM((1, ne), F32)],
        compiler_params=_cparams(("arbitrary",)),
        name="moe_route",
    )(logits, router_b.reshape(1, ne))


def _dispatch_body(e_ref, rk_ref, ps_ref, h2_ref, xs_hbm, sem, *, tm):
    def issue(m, carry):
        for kk in range(TOP_K):
            j = m * TOP_K + kk
            row = ps_ref[e_ref[0, 0, j]] + rk_ref[0, 0, j]
            pltpu.make_async_copy(_slab(h2_ref, m), _slab(xs_hbm, row), sem).start(priority=kk % 2)
        return carry

    lax.fori_loop(0, tm, issue, 0)
    all_rows = xs_hbm.at[pl.ds(0, tm * TOP_K * PCH)]
    pltpu.make_async_copy(all_rows, all_rows, sem).wait()


def _assign_specs(tm, index_map):
    blk = pl.BlockSpec((1, 1, tm * TOP_K), index_map, memory_space=pltpu.SMEM)
    return blk, pl.BlockSpec(memory_space=pltpu.SMEM)


def _dispatch_call(eidx, rank, pad_start, h2s, n_rows):
    tn = h2s.shape[0] // PCH
    tm = TOK_TILE
    blk, whole = _assign_specs(tm, lambda i: (i, 0, 0))
    shp = (tn // tm, 1, tm * TOP_K)
    return pl.pallas_call(
        functools.partial(_dispatch_body, tm=tm),
        grid=(tn // tm,),
        in_specs=[blk, blk, whole, pl.BlockSpec((tm * PCH, LANES), lambda i: (i, 0))],
        out_specs=pl.BlockSpec(memory_space=pl.ANY),
        out_shape=jax.ShapeDtypeStruct((n_rows * PCH, LANES), I32),
        scratch_shapes=[pltpu.SemaphoreType.DMA],
        compiler_params=_cparams(("arbitrary",)),
        name="moe_dispatch",
    )(eidx.reshape(shp), rank.reshape(shp), pad_start, h2s)


def _expert_body(bi_ref, nr_ref, ld_ref, nx_ref, xs_ref, wg_hbm, wu_hbm, wd_hbm, ob_ref, wg_buf, wu_buf, wd_buf,
                 wg_bf, wu_bf, wd_bf, sem):
    del bi_ref
    i = pl.program_id(0)
    nr = nr_ref[i]
    slot = ld_ref[i]

    def fetch(e, s):
        return (pltpu.make_async_copy(wg_hbm.at[e], wg_buf.at[s], sem.at[s]),
                pltpu.make_async_copy(wu_hbm.at[e], wu_buf.at[s], sem.at[s]),
                pltpu.make_async_copy(wd_hbm.at[e], wd_buf.at[s], sem.at[s]))

    @pl.when(i == 0)
    def _():
        for cp in fetch(nx_ref[nx_ref.shape[0] - 1], 0):
            cp.start()

    @pl.when(slot >= 0)
    def _():
        for cp in fetch(0, slot):
            cp.wait()

        @pl.when(nx_ref[i] >= 0)
        def _():
            for cp in fetch(nx_ref[i], 1 - slot):
                cp.start()

        wg_bf[...] = wg_buf[slot].astype(BF16)
        wu_bf[...] = wu_buf[slot].astype(BF16)
        wd_bf[...] = wd_buf[slot].astype(BF16)

    @pl.when(nr > 0)
    def _():
        live = lax.broadcasted_iota(I32, (MOE_BLK, LANES), 0) < nr
        x = _rows_from_packed(xs_ref, MOE_BLK, live)
        hg = jnp.dot(x, wg_bf[...], preferred_element_type=F32)
        hu = jnp.dot(x, wu_bf[...], preferred_element_type=F32)
        hh = (hg * _sigmoid(hg) * hu).astype(BF16)
        _rows_to_packed(ob_ref, jnp.dot(hh, wd_bf[...], preferred_element_type=F32))


def _expert_tables(counts, pad_start, pad_end, nb):
    ne = counts.shape[0]
    first_row = jnp.arange(nb, dtype=I32) * MOE_BLK
    block_e = jnp.minimum(jnp.searchsorted(pad_end, first_row, side="right"), ne - 1).astype(I32)
    block_rows = jnp.clip(pad_start[block_e] + counts[block_e] - first_row, 0, MOE_BLK).astype(I32)
    block_i = jnp.minimum(jnp.arange(nb, dtype=I32), pad_end[-1] // MOE_BLK - 1).astype(I32)
    has = counts > 0
    ordinal = jnp.cumsum(has.astype(I32)) - 1
    ids = jnp.where(has, jnp.arange(ne, dtype=I32), ne)
    nxt = jnp.concatenate([lax.cummin(ids, reverse=True)[1:], jnp.full((1,), ne, I32)])
    nxt = jnp.where(nxt < ne, nxt, -1).astype(I32)
    starts = (first_row == pad_start[block_e]) & (block_rows > 0)
    load_slot = jnp.where(starts, ordinal[block_e] % 2, -1).astype(I32)
    first_e = jnp.min(ids).astype(I32)
    next_e = jnp.concatenate([jnp.where(starts, nxt[block_e], -1).astype(I32), first_e[None]])
    return block_i, block_rows, load_slot, next_e


def _expert_call(tables, xs, wg, wu, wd):
    nb = xs.shape[0] // (MOE_BLK * PCH)
    d, ff = wg.shape[1], wg.shape[2]
    rows = pl.BlockSpec((MOE_BLK * PCH, LANES), lambda i, bi, nr, ld, nx: (bi[i], 0))
    hbm = pl.BlockSpec(memory_space=pl.ANY)
    grid_spec = pltpu.PrefetchScalarGridSpec(
        num_scalar_prefetch=4,
        grid=(nb,),
        in_specs=[rows, hbm, hbm, hbm],
        out_specs=rows,
        scratch_shapes=[pltpu.VMEM((2, d, ff), F32), pltpu.VMEM((2, d, ff), F32), pltpu.VMEM((2, ff, d), F32),
                        pltpu.VMEM((d, ff), BF16), pltpu.VMEM((d, ff), BF16), pltpu.VMEM((ff, d), BF16),
                        pltpu.SemaphoreType.DMA((2,))],
    )
    return pl.pallas_call(
        _expert_body,
        grid_spec=grid_spec,
        out_shape=jax.ShapeDtypeStruct(xs.shape, I32),
        compiler_params=_cparams(("arbitrary",), VMEM_LIMIT),
        name="moe_experts",
    )(*tables, xs, wg, wu, wd)


def _combine_body(e_ref, rk_ref, wt_ref, ps_ref, ob_hbm, h2_ref, x1_ref, mod_ref, sg_ref, su_ref, sd_ref, fg_ref,
                  out_ref, gbuf, rbuf, sem, *, tm):
    bb, ll, d = x1_ref.shape

    def issue(m, carry):
        for kk in range(TOP_K):
            j = m * TOP_K + kk
            row = ps_ref[e_ref[0, 0, j]] + rk_ref[0, 0, j]
            pltpu.make_async_copy(_slab(ob_hbm, row), _slab(gbuf, j), sem).start(priority=kk % 2)
        return carry

    lax.fori_loop(0, tm, issue, 0)
    pltpu.make_async_copy(ob_hbm.at[pl.ds(0, tm * TOP_K * PCH)], gbuf, sem).wait()

    def mix(m, carry):
        lo, hi = _unpack_pair(_slab(gbuf, m * TOP_K)[...])
        wt = wt_ref[0, 0, m * TOP_K]
        acc_lo, acc_hi = wt * lo, wt * hi
        for kk in range(1, TOP_K):
            lo, hi = _unpack_pair(_slab(gbuf, m * TOP_K + kk)[...])
            wt = wt_ref[0, 0, m * TOP_K + kk]
            acc_lo, acc_hi = acc_lo + wt * lo, acc_hi + wt * hi
        _fslab(rbuf, m)[...] = jnp.concatenate([acc_lo, acc_hi], axis=0)
        return carry

    lax.fori_loop(0, tm, mix, 0)

    routed = jnp.concatenate([rbuf[pl.ds(c, tm, stride=CHUNKS), :] for c in range(CHUNKS)], axis=1)
    h2 = _rows_from_packed(h2_ref, tm)
    hg = jnp.dot(h2, sg_ref[...], preferred_element_type=F32)
    hu = jnp.dot(h2, su_ref[...], preferred_element_type=F32)
    shared = jnp.dot((hg * _sigmoid(hg) * hu).astype(BF16), sd_ref[...], preferred_element_type=F32)
    ff = (routed + shared).reshape(bb, ll, d)
    x2 = x1_ref[...] + mod_ref[:, 5:6, :] * ff
    out_ref[...] = x2 * lax.rsqrt(jnp.mean(x2 * x2, axis=-1, keepdims=True) + NORM_EPS) * fg_ref[...]


def _combine_call(eidx, rank, wts, pad_start, ob, h2s, x1, mod, p):
    bn, seq, d = x1.shape
    tm = CMB_TILE
    bb, ll = _tile(bn, seq, tm)
    nl = seq // ll
    tn = bn * seq
    smem, whole = _assign_specs(tm, lambda b, l: (b * nl + l, 0, 0))
    tok = pl.BlockSpec((bb, ll, d), lambda b, l: (b, l, 0))
    full = lambda a: pl.BlockSpec(a.shape, lambda b, l: (0,) * a.ndim)
    consts = (p["sh_gate"], p["sh_up"], p["sh_down"], p["final_g"])
    shp = (tn // tm, 1, tm * TOP_K)
    return pl.pallas_call(
        functools.partial(_combine_body, tm=tm),
        grid=(bn // bb, nl),
        in_specs=[smem, smem, smem, whole, pl.BlockSpec(memory_space=pl.ANY),
                  pl.BlockSpec((tm * PCH, LANES), lambda b, l: (b * nl + l, 0)),
                  tok, pl.BlockSpec((bb, 6, d), lambda b, l: (b, 0, 0))] + [full(c) for c in consts],
        out_specs=tok,
        out_shape=jax.ShapeDtypeStruct((bn, seq, d), F32),
        scratch_shapes=[pltpu.VMEM((tm * TOP_K * PCH, LANES), I32), pltpu.VMEM((tm * CHUNKS, LANES), F32),
                        pltpu.SemaphoreType.DMA],
        compiler_params=_cparams(("arbitrary", "arbitrary"), VMEM_LIMIT),
        name="moe_combine_final",
    )(eidx.reshape(shp), rank.reshape(shp), wts.reshape(shp), pad_start, ob, h2s, x1, mod, *consts)


def _rw_state_to_pairs(s):
    bn = s.shape[0]
    s = s.reshape(bn, RW_HEADS // 2, 2, RW_HEAD, RW_HEAD)
    z = jnp.zeros_like(s[:, :, 0])
    return jnp.concatenate([jnp.concatenate([s[:, :, 0], z], axis=-1),
                            jnp.concatenate([z, s[:, :, 1]], axis=-1)], axis=-2)


def _rw_state_from_pairs(sp):
    bn = sp.shape[0]
    s = jnp.stack([sp[:, :, :RW_HEAD, :RW_HEAD], sp[:, :, RW_HEAD:, RW_HEAD:]], axis=2)
    return s.reshape(bn, RW_HEADS, RW_HEAD, RW_HEAD)


def _gla_state_to_pairs(s):
    bn = s.shape[0]
    t = jnp.swapaxes(s, -1, -2).reshape(bn, GLA_HEADS // 2, 2, GLA_DV, GLA_DK)
    z = jnp.zeros_like(t[:, :, 0])
    return jnp.concatenate([jnp.concatenate([t[:, :, 0], z], axis=-1),
                            jnp.concatenate([z, t[:, :, 1]], axis=-1)], axis=-2)


def _gla_state_from_pairs(sp):
    bn = sp.shape[0]
    t = jnp.stack([sp[:, :, :GLA_DV, :GLA_DK], sp[:, :, GLA_DV:, GLA_DK:]], axis=2)
    return jnp.swapaxes(t.reshape(bn, GLA_HEADS, GLA_DV, GLA_DK), -1, -2)


def _layer_params(l, ada_w, ada_b, norm1_g, norm2_g, w_in, mu_shift, rw_w0, rw_w_up, rw_a0, rw_a_up, rw_g_up,
                  rw_k_k, rw_k_a, rw_r_k, rw_gn_g, rw_gn_b, gla_a_up, gla_a_bias, gla_norm_g, w_pa, w_pb, w_out,
                  router_w, router_b, exp_gate, exp_up, exp_down, sh_gate, sh_up, sh_down):
    d = D_MODEL
    wi = w_in[l]
    gla0 = RW_SHIFT_COLS
    xal0 = gla0 + QKV_W
    pad = jnp.zeros((d, XAL_W - GLA_GATE_RANK), F32)
    w_pack = jnp.concatenate([wi[:, :xal0], wi[:, xal0:xal0 + GLA_GATE_RANK], pad,
                              wi[:, xal0 + GLA_GATE_RANK:]], axis=1).astype(BF16)
    zr = jnp.zeros((RW_W_RANK, RW_WIDTH), F32)
    hid = jnp.arange(RW_WIDTH) // RW_HEAD
    row = lambda a: a.reshape(1, -1)
    rw_hi = router_w[l].astype(BF16)
    return dict(
        ada_w=ada_w[l], ada_b=ada_b[l], norm1_g=norm1_g[l], norm2_g=norm2_g[l].reshape(1, 1, d), w_pack=w_pack,
        mu=mu_shift[l].reshape(1, 1, -1), w0=row(rw_w0[l]), wup=jnp.concatenate([rw_w_up[l], zr], axis=0),
        a0=row(rw_a0[l]), aup=jnp.concatenate([zr, rw_a_up[l]], axis=0), gup=rw_g_up[l].astype(BF16),
        kk=row(rw_k_k[l]), ka=row(rw_k_a[l]), rk=row(rw_r_k[l]),
        bd64=(hid[:, None] == hid[None, :]).astype(BF16),
        gn_g=row(rw_gn_g[l]), gn_b=row(rw_gn_b[l]),
        gla_aup=jnp.concatenate([gla_a_up[l], jnp.zeros((XAL_W - GLA_GATE_RANK, GLA_KW), F32)], axis=0),
        gla_ab=row(gla_a_bias[l]), gla_ng=row(gla_norm_g[l]),
        w_pa=w_pa[l].astype(BF16), w_pb=w_pb[l].astype(BF16), w_out=w_out[l].astype(BF16),
        rw_hi=rw_hi, rw_lo=(router_w[l] - rw_hi.astype(F32)).astype(BF16), router_b=router_b[l],
        exp_gate=exp_gate[l], exp_up=exp_up[l], exp_down=exp_down[l],
        sh_gate=sh_gate[l].astype(BF16), sh_up=sh_up[l].astype(BF16), sh_down=sh_down[l].astype(BF16),
    )


def _mixer_group(x, mod, s_rw, s_sh, s_gla, p):
    pa, qkv, xal, gg, mg = _inproj_call(x, mod, p["norm1_g"], p["w_pack"])
    r, lw, k2, v, a_s, b_s, g, bonus, new_sh = _rwprep_call(pa, s_sh, p)
    y, rw_new = _rwscan_call(r, lw, k2, v, a_s, b_s, _rw_state_to_pairs(s_rw))
    o_b, gla_new = _gla_call(qkv, xal, gg, _gla_state_to_pairs(s_gla), p)
    x1, h2s, logits = _merge_call(y, g, bonus, o_b, mg, x, mod, p)
    states = (_rw_state_from_pairs(rw_new), new_sh[:, 0, :], _gla_state_from_pairs(gla_new))
    return x1, h2s, logits, states


def _moe(h2s, logits, p):
    tn = h2s.shape[0] // PCH
    eidx, rank, wts, counts = _route_call(logits, p["router_b"])
    eidx, rank, wts = eidx[:, :TOP_K], rank[:, :TOP_K], wts[:, :TOP_K]
    counts = counts[0].astype(I32)
    padded = (counts + MOE_BLK - 1) // MOE_BLK * MOE_BLK
    pad_end = jnp.cumsum(padded)
    pad_start = (pad_end - padded).astype(I32)
    nb = (tn * TOP_K + N_EXPERTS * (MOE_BLK - 1)) // MOE_BLK + 1
    tables = _expert_tables(counts, pad_start, pad_end, nb)
    xs = _dispatch_call(eidx, rank, pad_start, h2s, nb * MOE_BLK)
    ob = _expert_call(tables, xs, p["exp_gate"], p["exp_up"], p["exp_down"])
    return ob, (eidx, rank, wts, pad_start)


def kernel(x_prompt, x_sample, c_prompt, c_sample, state_rwkv, state_shift, state_gla, ada_w, ada_b, norm1_g,
           norm2_g, w_in, mu_shift, rw_w0, rw_w_up, rw_a0, rw_a_up, rw_g_up, rw_k_k, rw_k_a, rw_r_k, rw_gn_g,
           rw_gn_b, gla_a_up, gla_a_bias, gla_norm_g, w_pa, w_pb, w_out, router_w, router_b, exp_gate, exp_up,
           exp_down, sh_gate, sh_up, sh_down, final_g):
    depth = ada_w.shape[0]
    bp, bs = x_prompt.shape[0], x_sample.shape[0]
    tp = bp * x_prompt.shape[1]
    xs_g = [x_prompt, x_sample]
    c_all = jnp.concatenate([c_prompt, c_sample], axis=0)
    zeros = lambda shape: jnp.zeros(shape, x_prompt.dtype)
    new_states = [[], []]
    fg = final_g.reshape(1, 1, D_MODEL)
    for l in range(depth):
        p = _layer_params(l, ada_w, ada_b, norm1_g, norm2_g, w_in, mu_shift, rw_w0, rw_w_up, rw_a0, rw_a_up,
                          rw_g_up, rw_k_k, rw_k_a, rw_r_k, rw_gn_g, rw_gn_b, gla_a_up, gla_a_bias, gla_norm_g,
                          w_pa, w_pb, w_out, router_w, router_b, exp_gate, exp_up, exp_down, sh_gate, sh_up,
                          sh_down)
        p["final_g"] = fg
        mod_all = _mod_call(c_all, p["ada_w"], p["ada_b"])
        mods = [mod_all[:bp], mod_all[bp:]]
        states_in = [
            (zeros((bp, RW_HEADS, RW_HEAD, RW_HEAD)), zeros((bp, RW_SHIFT_COLS)),
             zeros((bp, GLA_HEADS, GLA_DK, GLA_DV))),
            (state_rwkv[l], state_shift[l], state_gla[l]),
        ]
        x1s, h2ss, lgs = [], [], []
        for gi in range(2):
            x1, h2s, logits, st = _mixer_group(xs_g[gi], mods[gi], *states_in[gi], p)
            x1s.append(x1)
            h2ss.append(h2s)
            lgs.append(logits)
            new_states[gi].append(st)
        h2_all = jnp.concatenate(h2ss, axis=0)
        ob, (eidx, rank, wts, pad_start) = _moe(h2_all, jnp.concatenate(lgs, axis=0), p)
        assert depth == 1, "the fused final norm assumes a single layer"
        xs_g = [
            _combine_call(eidx[:tp], rank[:tp], wts[:tp], pad_start, ob, h2ss[0], x1s[0], mods[0], p),
            _combine_call(eidx[tp:], rank[tp:], wts[tp:], pad_start, ob, h2ss[1], x1s[1], mods[1], p),
        ]
    stack = lambda gi, j: jnp.stack([s[j] for s in new_states[gi]])
    return (xs_g[0], xs_g[1], stack(0, 0), stack(0, 1), stack(0, 2), stack(1, 0), stack(1, 1), stack(1, 2))
```

```python
import functools

import jax
import jax.numpy as jnp
from jax import lax
from jax.experimental import pallas as pl
from jax.experimental.pallas import tpu as pltpu

F32, BF16, I32 = jnp.float32, jnp.bfloat16, jnp.int32

D_MODEL = 1024
RW_HEADS, RW_HEAD = 8, 64
RW_WIDTH = RW_HEADS * RW_HEAD
RW_W_RANK, RW_A_RANK, RW_G_RANK = 64, 64, 128
RW_GN_EPS = 64e-5
GLA_HEADS, GLA_DK, GLA_DV = 4, 64, 128
GLA_KW, GLA_VW = GLA_HEADS * GLA_DK, GLA_HEADS * GLA_DV
GLA_GATE_RANK = 16
GLA_GATE_TAU = 16.0
GLA_CHUNK = 16
RW_SHIFT_COLS = 3 * RW_WIDTH + RW_W_RANK + RW_A_RANK + RW_G_RANK
N_EXPERTS, TOP_K, N_GROUPS, TOPK_GROUPS = 256, 8, 8, 4
GROUP_SIZE = N_EXPERTS // N_GROUPS
EXPERT_FF = 256
ROUTED_SCALE = 2.5
NORM_EPS = 1e-6

LANES = 128
SUBLANES = 8
CHUNKS = D_MODEL // LANES
PCH = CHUNKS // 2
UNIT = 64
RW_SCAN_PASSES = (1, 1, 1, 1, 1)
GLA_UNITS_PER_STEP = 4
RW_UNITS_PER_STEP = 4
VMEM_LIMIT = 56 * 1024 * 1024

PA_W, QKV_W, XAL_W, GG_W, MG_W = RW_SHIFT_COLS, 2 * GLA_KW + GLA_VW, LANES, GLA_VW, 2 * D_MODEL
PACK_OFFS = (0, PA_W, PA_W + QKV_W, PA_W + QKV_W + XAL_W, PA_W + QKV_W + XAL_W + GG_W)
PACK_W = PA_W + QKV_W + XAL_W + GG_W + MG_W

TOK_TILE = 256
MOE_BLK = 256
CMB_TILE = 128

_DN = {
    "nn": (((1,), (0,)), ((), ())),
    "nt": (((1,), (1,)), ((), ())),
    "tn": (((0,), (0,)), ((), ())),
}


def _split(x, pieces):
    out, rem = [], x
    for i in range(pieces):
        p = rem.astype(BF16)
        out.append(p)
        if i + 1 < pieces:
            rem = rem - p.astype(F32)
    return out


def _mm(a, b, form="nn", passes=1):
    dn = _DN[form]
    if passes == 6:
        return lax.dot_general(a.astype(F32), b.astype(F32), dn, precision=lax.Precision.HIGHEST,
                               preferred_element_type=F32)
    if passes == 1:
        return lax.dot_general(a.astype(BF16), b.astype(BF16), dn, preferred_element_type=F32)
    ah, al = _split(a, 2)
    bh, bl = _split(b, 2)
    out = lax.dot_general(ah, bh, dn, preferred_element_type=F32)
    out = out + lax.dot_general(ah, bl, dn, preferred_element_type=F32)
    return out + lax.dot_general(al, bh, dn, preferred_element_type=F32)


def _mm01(m01, x, pieces=3):
    m = m01.astype(BF16)
    out = None
    for p in _split(x, pieces):
        t = lax.dot_general(m, p, _DN["nn"], preferred_element_type=F32)
        out = t if out is None else out + t
    return out


def _xmm01(x, m01, pieces=2):
    m = m01.astype(BF16)
    out = None
    for p in _split(x, pieces):
        t = lax.dot_general(p, m, _DN["nn"], preferred_element_type=F32)
        out = t if out is None else out + t
    return out


HI16 = -65536


def _bf16_bits(x):
    return lax.bitcast_convert_type(x.astype(BF16).astype(F32), I32)


def _unpack_pair(w):
    return lax.bitcast_convert_type(w << 16, F32), lax.bitcast_convert_type(w & HI16, F32)


def _rows_to_packed(ref, x):
    for c in range(PCH):
        lo = _bf16_bits(x[:, c * LANES:(c + 1) * LANES])
        hi = _bf16_bits(x[:, (c + PCH) * LANES:(c + PCH + 1) * LANES])
        ref[pl.ds(c, x.shape[0], stride=PCH), :] = ((lo >> 16) & 0xFFFF) | (hi & HI16)


def _rows_from_packed(ref, n, live=None):
    lows, highs = [], []
    for c in range(PCH):
        w = ref[pl.ds(c, n, stride=PCH), :]
        if live is not None:
            w = jnp.where(live, w, 0)
        lo, hi = _unpack_pair(w)
        lows.append(lo.astype(BF16))
        highs.append(hi.astype(BF16))
    return jnp.concatenate(lows + highs, axis=1)


def _slab(ref, row):
    return ref.at[pl.ds(pl.multiple_of(row * PCH, PCH), PCH)]


def _fslab(ref, row):
    return ref.at[pl.ds(pl.multiple_of(row * CHUNKS, CHUNKS), CHUNKS)]


def _sigmoid(x):
    return 1.0 / (1.0 + jnp.exp(-x))


def _softplus(x):
    return jnp.maximum(x, 0.0) + jnp.log(1.0 + jnp.exp(-jnp.abs(x)))


def _log2(n):
    assert n > 0 and n & (n - 1) == 0, n
    return n.bit_length() - 1


def _cparams(sem, vmem=None):
    return pltpu.CompilerParams(dimension_semantics=sem, vmem_limit_bytes=vmem)


def _mod_body(c_ref, w_ref, b_ref, o_ref):
    c = c_ref[...]
    o_ref[0] = _mm(c * _sigmoid(c), w_ref[...], passes=6) + b_ref[...]


def _mod_call(c_all, ada_w, ada_b):
    bt, d = c_all.shape
    out = pl.pallas_call(
        _mod_body,
        grid=(6,),
        in_specs=[pl.BlockSpec((bt, d), lambda k: (0, 0)),
                  pl.BlockSpec((d, d), lambda k: (0, k)),
                  pl.BlockSpec((1, d), lambda k: (0, k))],
        out_specs=pl.BlockSpec((1, bt, d), lambda k: (k, 0, 0)),
        out_shape=jax.ShapeDtypeStruct((6, bt, d), F32),
        compiler_params=_cparams(("arbitrary",)),
        name="adaln_mod",
    )(c_all, ada_w, ada_b.reshape(1, 6 * d))
    return jnp.transpose(out, (1, 0, 2))


def _inproj_body(x_ref, mod_ref, g_ref, w_ref, pa_ref, qkv_ref, xal_ref, gg_ref, mg_ref):
    bb, ll, d = x_ref.shape
    x = x_ref[...]
    y = x * lax.rsqrt(jnp.mean(x * x, axis=-1, keepdims=True) + NORM_EPS) * g_ref[...]
    h = y * (1.0 + mod_ref[:, 1:2, :]) + mod_ref[:, 0:1, :]
    hb = h.reshape(bb * ll, d).astype(BF16)
    for ref, off in zip((pa_ref, qkv_ref, xal_ref, gg_ref, mg_ref), PACK_OFFS):
        w = ref.shape[-1]
        ref[...] = jnp.dot(hb, w_ref[:, off:off + w], preferred_element_type=F32).reshape(bb, ll, w)


def _tile(bn, seq, tile):
    if seq >= tile:
        assert seq % tile == 0
        return 1, tile
    assert tile % seq == 0 and bn % (tile // seq) == 0
    return tile // seq, seq


def _inproj_call(x, mod, norm_g, w_pack):
    bn, seq, d = x.shape
    bb, ll = _tile(bn, seq, TOK_TILE)
    tok = lambda w: pl.BlockSpec((bb, ll, w), lambda b, l: (b, l, 0))
    widths = (PA_W, QKV_W, XAL_W, GG_W, MG_W)
    return pl.pallas_call(
        _inproj_body,
        grid=(bn // bb, seq // ll),
        in_specs=[tok(d),
                  pl.BlockSpec((bb, 6, d), lambda b, l: (b, 0, 0)),
                  pl.BlockSpec((1, 1, d), lambda b, l: (0, 0, 0)),
                  pl.BlockSpec((d, PACK_W), lambda b, l: (0, 0))],
        out_specs=[tok(w) for w in widths],
        out_shape=[jax.ShapeDtypeStruct((bn, seq, w), F32) for w in widths],
        compiler_params=_cparams(("arbitrary", "arbitrary"), VMEM_LIMIT),
        name="norm_inproj",
    )(x, mod, norm_g.reshape(1, 1, d), w_pack)


def _rwprep_body(pa_ref, sh_ref, mu_ref, w0_ref, wup_ref, a0_ref, aup_ref, gup_ref, kk_ref, ka_ref, rk_ref,
                 bd_ref, r_o, lw_o, k_o, v_o, a_o, b_o, g_o, bon_o, nsh_o, carry):
    bb, ll, wd = pa_ref.shape
    n = bb * ll
    hw = RW_WIDTH

    @pl.when(pl.program_id(1) == 0)
    def _():
        carry[...] = sh_ref[...]

    pa = pa_ref[...]
    rolled = pltpu.roll(pa.reshape(n, wd), 1, 0).reshape(bb, ll, wd)
    tok = lax.broadcasted_iota(I32, (bb, ll, wd), 1)
    prev = jnp.where(tok == 0, carry[...], rolled)
    last = pa_ref[:, ll - 1:ll, :]
    carry[...] = last
    nsh_o[...] = last
    xs = (pa + (prev - pa) * mu_ref[...]).reshape(n, wd)

    r, k, v = xs[:, 0:hw], xs[:, hw:2 * hw], xs[:, 2 * hw:3 * hw]
    xwa = xs[:, 3 * hw:3 * hw + LANES]
    xg = xs[:, 3 * hw + LANES:]
    w_log = -_softplus(-(w0_ref[...] + _mm(jnp.tanh(xwa), wup_ref[...], passes=3))) - 0.5
    lw = -jnp.exp(w_log)
    a = _sigmoid(a0_ref[...] + _mm(xwa, aup_ref[...], passes=3))
    g = _mm(_sigmoid(xg), gup_ref[...])
    bd = bd_ref[...]
    kkv = k * kk_ref[...]
    kkn = kkv * lax.rsqrt(jnp.maximum(_xmm01(kkv * kkv, bd), 1e-24))
    k2 = k * (1.0 + (a - 1.0) * ka_ref[...])
    bonus = _xmm01(r * k2 * rk_ref[...], bd) * v
    for ref, val in ((r_o, r), (lw_o, lw), (k_o, k2), (v_o, v), (a_o, -kkn), (b_o, kkn * a), (g_o, g),
                     (bon_o, bonus)):
        ref[...] = val.reshape(bb, ll, hw)


def _rwprep_call(pa, s_sh, p):
    bn, seq, wd = pa.shape
    bb, ll = _tile(bn, seq, TOK_TILE)
    hw = RW_WIDTH
    tok = lambda w: pl.BlockSpec((bb, ll, w), lambda b, l: (b, l, 0))
    row = lambda w: pl.BlockSpec((bb, 1, w), lambda b, l: (b, 0, 0))
    full = lambda a: pl.BlockSpec(a.shape, lambda b, l: (0,) * a.ndim)
    consts = (p["mu"], p["w0"], p["wup"], p["a0"], p["aup"], p["gup"], p["kk"], p["ka"], p["rk"], p["bd64"])
    outs = pl.pallas_call(
        _rwprep_body,
        grid=(bn // bb, seq // ll),
        in_specs=[tok(wd), row(wd)] + [full(c) for c in consts],
        out_specs=[tok(hw)] * 8 + [row(wd)],
        out_shape=[jax.ShapeDtypeStruct((bn, seq, hw), F32)] * 8 + [jax.ShapeDtypeStruct((bn, 1, wd), F32)],
        scratch_shapes=[pltpu.VMEM((bb, 1, wd), F32)],
        compiler_params=_cparams(("arbitrary", "arbitrary"), VMEM_LIMIT),
        name="rwkv_prep",
    )(pa, s_sh.reshape(bn, 1, wd), *consts)
    return outs


def _unit_masks(n, tl):
    ri = lax.broadcasted_iota(I32, (n, n), 0)
    ci = lax.broadcasted_iota(I32, (n, n), 1)
    same = (ri >> _log2(tl)) == (ci >> _log2(tl))
    return same, same & (ri > ci), same & (ri >= ci)


def _rwscan_body(r_ref, lw_ref, k_ref, v_ref, a_ref, b_ref, s0_ref, y_ref, sn_ref, st, *, nu, nseq, tl, passes):
    n = nseq * tl
    n2 = 2 * n
    p_aa, p_inv, p_apply, p_state, p_y = passes

    @pl.when(pl.program_id(1) == 0)
    def _():
        st[...] = s0_ref[...]

    same, _, incl = _unit_masks(n, tl)
    m_cum = jnp.where(incl, 1.0, 0.0)
    m_seq = jnp.where(same, 1.0, 0.0)
    ri = lax.broadcasted_iota(I32, (n2, n2), 0)
    ci = lax.broadcasted_iota(I32, (n2, n2), 1)
    rt, ct = ri & (n - 1), ci & (n - 1)
    dsame = ((rt >> _log2(tl)) == (ct >> _log2(tl))) & ((ri >> _log2(n)) == (ci >> _log2(n)))
    strict_d = dsame & (rt > ct)
    incl_d = dsame & (rt >= ct)
    eye_d = jnp.where(ri == ci, 1.0, 0.0)
    lane = lax.broadcasted_iota(I32, (1, LANES), 1)
    m0 = jnp.where(lane < RW_HEAD, 1.0, 0.0)
    m1 = 1.0 - m0

    def dup(x):
        return jnp.concatenate([x * m0, x * m1], axis=0)

    def seq_rows(x, q):
        if nseq == 1:
            return x
        return jnp.concatenate([x[q * tl:(q + 1) * tl], x[n + q * tl:n + (q + 1) * tl]], axis=0)

    def unit_rows(parts):
        if nseq == 1:
            return parts[0]
        return jnp.concatenate([p[0:tl] for p in parts] + [p[tl:2 * tl] for p in parts], axis=0)

    chains = [(u, p) for u in range(nu) for p in range(RW_HEADS // 2)]
    ids = range(len(chains))
    cat0 = lambda *xs: jnp.concatenate(xs, axis=0)

    def ld(ref, c):
        u, p = chains[c]
        return ref[u * nseq:(u + 1) * nseq, :, p * LANES:(p + 1) * LANES].reshape(n, LANES)

    lw = [ld(lw_ref, c) for c in ids]
    cum = [_mm01(m_cum, x) for x in lw]
    tot = [_mm01(m_seq, x) for x in lw]
    e_c = [jnp.exp(x) for x in cum]
    e_n = [jnp.exp(-x) for x in cum]
    e_l = [jnp.exp(t - x) for t, x in zip(tot, cum)]
    at_d = [dup(ld(a_ref, c) * jnp.exp(cum[c] - lw[c])) for c in ids]
    rt_d = [dup(ld(r_ref, c) * e_c[c]) for c in ids]
    bt_d = [dup(ld(b_ref, c) * e_n[c]) for c in ids]
    kt_d = [dup(ld(k_ref, c) * e_n[c]) for c in ids]
    bh_d = [dup(ld(b_ref, c) * e_l[c]) for c in ids]
    kh_d = [dup(ld(k_ref, c) * e_l[c]) for c in ids]
    v_d = [dup(ld(v_ref, c)) for c in ids]
    aa = [_mm(cat0(at_d[c], rt_d[c]), cat0(bt_d[c], kt_d[c]), "nt", p_aa) for c in ids]
    a_ab = [jnp.where(strict_d, x[0:n2, 0:n2], 0.0) for x in aa]
    a_ak = [jnp.where(strict_d, x[0:n2, n2:], 0.0) for x in aa]
    a_rb = [jnp.where(incl_d, x[n2:, 0:n2], 0.0) for x in aa]
    a_rk = [jnp.where(incl_d, x[n2:, n2:], 0.0) for x in aa]
    zy = [_mm(cat0(a_ak[c], a_rk[c]), v_d[c], passes=p_apply) for c in ids]
    tinv = [eye_d + x for x in a_ab]
    nk = a_ab
    for _ in range(_log2(tl) - 1):
        nk = [_mm(x, x, passes=p_inv) for x in nk]
        tinv = [t + _mm(t, x, passes=p_inv) for t, x in zip(tinv, nk)]
    wu = [_mm(tinv[c], jnp.concatenate([at_d[c], zy[c][0:n2]], axis=1), passes=p_apply) for c in ids]
    seqs = range(nseq)
    srow = lambda c, q: (chains[c][0] * nseq + q, chains[c][1])
    s_old = [[st[srow(c, q)] for q in seqs] for c in ids]
    xs = [[_mm(cat0(seq_rows(wu[c][:, 0:LANES], q), seq_rows(rt_d[c], q)), s_old[c][q], "nt", p_state)
           for q in seqs] for c in ids]
    u_q = [[xs[c][q][0:2 * tl] + seq_rows(wu[c][:, LANES:], q) for q in seqs] for c in ids]
    for c in ids:
        for q in seqs:
            g_c = jnp.exp(tot[c][q * tl:q * tl + 1, :])
            st[srow(c, q)] = s_old[c][q] * g_c + _mm(cat0(u_q[c][q], seq_rows(v_d[c], q)),
                                                     cat0(seq_rows(bh_d[c], q), seq_rows(kh_d[c], q)), "tn", p_state)
    for c in ids:
        u, p = chains[c]
        y_d = (unit_rows([xs[c][q][2 * tl:] for q in seqs]) + _mm(a_rb[c], unit_rows(u_q[c]), passes=p_y)
               + zy[c][n2:])
        y_ref[u * nseq:(u + 1) * nseq, :, p * LANES:(p + 1) * LANES] = (y_d[0:n] + y_d[n:]).reshape(nseq, tl, LANES)

    @pl.when(pl.program_id(1) == pl.num_programs(1) - 1)
    def _():
        sn_ref[...] = st[...]


def _unit_shape(bn, seq):
    if seq >= UNIT:
        assert seq % UNIT == 0
        return 1, UNIT
    assert UNIT % seq == 0 and bn % (UNIT // seq) == 0
    return UNIT // seq, seq


def _rwscan_call(r, lw, k2, v, a_s, b_s, s0_bd, passes=RW_SCAN_PASSES):
    bn, seq, hw = r.shape
    nseq, tl = _unit_shape(bn, seq)
    nu = RW_UNITS_PER_STEP if bn % (RW_UNITS_PER_STEP * nseq) == 0 else 1
    rows = nu * nseq
    tok = pl.BlockSpec((rows, tl, hw), lambda b, c: (b, c, 0))
    stt = pl.BlockSpec((rows, RW_HEADS // 2, LANES, LANES), lambda b, c: (b, 0, 0, 0))
    return pl.pallas_call(
        functools.partial(_rwscan_body, nu=nu, nseq=nseq, tl=tl, passes=passes),
        grid=(bn // rows, seq // tl),
        in_specs=[tok] * 6 + [stt],
        out_specs=[tok, stt],
        out_shape=[jax.ShapeDtypeStruct((bn, seq, hw), F32), jax.ShapeDtypeStruct(s0_bd.shape, F32)],
        scratch_shapes=[pltpu.VMEM((rows, RW_HEADS // 2, LANES, LANES), F32)],
        compiler_params=_cparams(("arbitrary", "arbitrary"), VMEM_LIMIT),
        name="rwkv_scan",
    )(r, lw, k2, v, a_s, b_s, s0_bd)


def _gla_body(qkv_ref, xal_ref, gate_ref, aup_ref, ab_ref, ng_ref, s0_ref, o_ref, sn_ref, st, *, nu, nseq, tl, cs):
    n = nseq * tl
    n2 = 2 * n
    nsub = tl // cs

    @pl.when(pl.program_id(1) == 0)
    def _():
        st[...] = s0_ref[...]

    same, _, incl = _unit_masks(n, cs)
    m_cum = jnp.where(incl, 1.0, 0.0)
    m_sub = jnp.where(same, 1.0, 0.0)
    ri = lax.broadcasted_iota(I32, (n2, n2), 0)
    ci = lax.broadcasted_iota(I32, (n2, n2), 1)
    rt, ct = ri & (n - 1), ci & (n - 1)
    causal_d = ((rt >> _log2(cs)) == (ct >> _log2(cs))) & ((ri >> _log2(n)) == (ci >> _log2(n))) & (rt >= ct)
    lane = lax.broadcasted_iota(I32, (1, LANES), 1)
    m0 = jnp.where(lane < GLA_DK, 1.0, 0.0)
    m1 = 1.0 - m0
    sr = lax.broadcasted_iota(I32, (2 * GLA_DV, LANES), 0)
    sc = lax.broadcasted_iota(I32, (2 * GLA_DV, LANES), 1)
    st_mask = jnp.where((sr >> _log2(GLA_DV)) == (sc >> _log2(GLA_DK)), 1.0, 0.0)

    def dup(x):
        return jnp.concatenate([x * m0, x * m1], axis=0)

    chains = [(u, p) for u in range(nu) for p in range(GLA_HEADS // 2)]
    ids = range(len(chains))
    urows = lambda u: slice(u * nseq, (u + 1) * nseq)
    ng = ng_ref[...]
    la_all = [-_softplus(-(_mm(xal_ref[urows(u), :, :].reshape(n, LANES), aup_ref[...], passes=3) + ab_ref[...]))
              * (1.0 / GLA_GATE_TAU) for u in range(nu)]

    def ld(ref, c, off, width):
        return ref[urows(chains[c][0]), :, off:off + width].reshape(n, width)

    q = [ld(qkv_ref, c, chains[c][1] * LANES, LANES) * (GLA_DK ** -0.5) for c in ids]
    k = [ld(qkv_ref, c, GLA_KW + chains[c][1] * LANES, LANES) for c in ids]
    vp = [ld(qkv_ref, c, 2 * GLA_KW + chains[c][1] * 2 * GLA_DV, 2 * GLA_DV) for c in ids]
    la = [la_all[u][:, p * LANES:(p + 1) * LANES] for u, p in chains]
    bc = [_mm01(m_cum, x) for x in la]
    bl = [_mm01(m_sub, x) for x in la]
    qe = [q[c] * jnp.exp(bc[c]) for c in ids]
    ke = [k[c] * jnp.exp(-bc[c]) for c in ids]
    kd = [k[c] * jnp.exp(bl[c] - bc[c]) for c in ids]
    att = [jnp.where(causal_d, _mm(dup(qe[c]), dup(ke[c]), "nt", passes=1), 0.0) for c in ids]
    v_st = [jnp.concatenate([x[:, 0:GLA_DV], x[:, GLA_DV:]], axis=0) for x in vp]
    o_st = [_mm(att[c], v_st[c], passes=1) for c in ids]
    upd = [[_mm(vp[c][r0:r0 + cs], kd[c][r0:r0 + cs], "tn", passes=1) for r0 in range(0, n, cs)] for c in ids]
    inter = [[None] * (n // cs) for _ in ids]
    for sq in range(nseq):
        s = [st[chains[c][0] * nseq + sq, chains[c][1]] for c in ids]
        for j in range(nsub):
            i = sq * nsub + j
            r0 = i * cs
            for c in ids:
                inter[c][i] = _mm(qe[c][r0:r0 + cs], s[c], "nt", passes=1)
                s[c] = s[c] * jnp.exp(bl[c][r0:r0 + 1, :]) + st_mask * upd[c][i]
        for c in ids:
            st[chains[c][0] * nseq + sq, chains[c][1]] = s[c]
    for c in ids:
        u, p = chains[c]
        o = o_st[c] + jnp.concatenate([x[:, 0:GLA_DV] for x in inter[c]] + [x[:, GLA_DV:] for x in inter[c]], axis=0)
        o = o * lax.rsqrt(jnp.mean(o * o, axis=-1, keepdims=True) + NORM_EPS) * ng
        goff = p * 2 * GLA_DV
        gp = ld(gate_ref, c, goff, 2 * GLA_DV)
        g_st = jnp.concatenate([gp[:, 0:GLA_DV], gp[:, GLA_DV:]], axis=0)
        ob = o * (g_st * _sigmoid(g_st))
        o_ref[urows(u), :, goff:goff + GLA_DV] = ob[0:n].reshape(nseq, tl, GLA_DV)
        o_ref[urows(u), :, goff + GLA_DV:goff + 2 * GLA_DV] = ob[n:].reshape(nseq, tl, GLA_DV)

    @pl.when(pl.program_id(1) == pl.num_programs(1) - 1)
    def _():
        sn_ref[...] = st[...]


def _gla_call(qkv, xal, gate, s0_t, p):
    bn, seq, _ = qkv.shape
    nseq, tl = _unit_shape(bn, seq)
    cs = min(GLA_CHUNK, seq)
    assert tl % cs == 0
    nu = GLA_UNITS_PER_STEP if bn % (GLA_UNITS_PER_STEP * nseq) == 0 else 1
    rows = nu * nseq
    tok = lambda w: pl.BlockSpec((rows, tl, w), lambda b, c: (b, c, 0))
    full = lambda a: pl.BlockSpec(a.shape, lambda b, c: (0,) * a.ndim)
    stt = pl.BlockSpec((rows, GLA_HEADS // 2, 2 * GLA_DV, LANES), lambda b, c: (b, 0, 0, 0))
    consts = (p["gla_aup"], p["gla_ab"], p["gla_ng"])
    return pl.pallas_call(
        functools.partial(_gla_body, nu=nu, nseq=nseq, tl=tl, cs=cs),
        grid=(bn // rows, seq // tl),
        in_specs=[tok(QKV_W), tok(XAL_W), tok(GG_W)] + [full(c) for c in consts] + [stt],
        out_specs=[tok(GLA_VW), stt],
        out_shape=[jax.ShapeDtypeStruct((bn, seq, GLA_VW), F32), jax.ShapeDtypeStruct(s0_t.shape, F32)],
        scratch_shapes=[pltpu.VMEM((rows, GLA_HEADS // 2, 2 * GLA_DV, LANES), F32)],
        compiler_params=_cparams(("arbitrary", "arbitrary"), VMEM_LIMIT),
        name="gla_chunked",
    )(qkv, xal, gate, *consts, s0_t)


def _merge_body(y_ref, g_ref, bon_ref, ob_ref, mg_ref, x_ref, mod_ref, gng_ref, gnb_ref, bd_ref, wpa_ref,
                wpb_ref, wout_ref, n2_ref, rwh_ref, rwl_ref, x1_o, h2_o, lg_o):
    bb, ll, d = x_ref.shape
    n = bb * ll
    hw = RW_WIDTH
    bd = bd_ref[...]
    y = y_ref[...].reshape(n, hw)
    mu = _xmm01(y, bd, pieces=3) * (1.0 / RW_HEAD)
    dv = y - mu
    var = _xmm01(dv * dv, bd) * (1.0 / RW_HEAD)
    yn = dv * lax.rsqrt(var + RW_GN_EPS) * gng_ref[...] + gnb_ref[...]
    o_a = (yn + bon_ref[...].reshape(n, hw)) * g_ref[...].reshape(n, hw)
    o_b = ob_ref[...].reshape(n, GLA_VW)
    mg = mg_ref[...].reshape(n, 2 * d)
    merged = _sigmoid(mg[:, 0:d]) * _mm(o_a, wpa_ref[...]) + _sigmoid(mg[:, d:]) * _mm(o_b, wpb_ref[...])
    mix = _mm(merged, wout_ref[...]).reshape(bb, ll, d)
    x1 = x_ref[...] + mod_ref[:, 2:3, :] * mix
    x1_o[...] = x1
    yn2 = x1 * lax.rsqrt(jnp.mean(x1 * x1, axis=-1, keepdims=True) + NORM_EPS) * n2_ref[...]
    h2 = (yn2 * (1.0 + mod_ref[:, 4:5, :]) + mod_ref[:, 3:4, :]).reshape(n, d)
    hh, hl = _split(h2, 2)
    rwh, rwl = rwh_ref[...], rwl_ref[...]
    lg_o[...] = (jnp.dot(hh, rwh, preferred_element_type=F32) + jnp.dot(hh, rwl, preferred_element_type=F32)
                 + jnp.dot(hl, rwh, preferred_element_type=F32))
    _rows_to_packed(h2_o, h2)


def _merge_call(y, g, bonus, o_b, mg, x, mod, p):
    bn, seq, d = x.shape
    bb, ll = _tile(bn, seq, TOK_TILE)
    nl = seq // ll
    tn = bn * seq
    tok = lambda w: pl.BlockSpec((bb, ll, w), lambda b, l: (b, l, 0))
    full = lambda a: pl.BlockSpec(a.shape, lambda b, l: (0,) * a.ndim)
    consts = (p["gn_g"], p["gn_b"], p["bd64"], p["w_pa"], p["w_pb"], p["w_out"], p["norm2_g"], p["rw_hi"],
              p["rw_lo"])
    return pl.pallas_call(
        _merge_body,
        grid=(bn // bb, nl),
        in_specs=[tok(RW_WIDTH)] * 3 + [tok(GLA_VW), tok(MG_W), tok(d),
                                        pl.BlockSpec((bb, 6, d), lambda b, l: (b, 0, 0))] + [full(c) for c in consts],
        out_specs=[tok(d),
                   pl.BlockSpec((bb * ll * PCH, LANES), lambda b, l: (b * nl + l, 0)),
                   pl.BlockSpec((bb * ll, N_EXPERTS), lambda b, l: (b * nl + l, 0))],
        out_shape=[jax.ShapeDtypeStruct((bn, seq, d), F32),
                   jax.ShapeDtypeStruct((tn * PCH, LANES), I32),
                   jax.ShapeDtypeStruct((tn, N_EXPERTS), F32)],
        compiler_params=_cparams(("arbitrary", "arbitrary"), VMEM_LIMIT),
        name="merge_outproj_router",
    )(y, g, bonus, o_b, mg, x, mod, *consts)


def _route_body(lg_ref, rb_ref, e_o, rk_o, w_o, cnt_o, carry):
    tm = lg_ref.shape[0]
    ne = N_EXPERTS

    @pl.when(pl.program_id(0) == 0)
    def _():
        carry[...] = jnp.zeros_like(carry)

    neg = -jnp.inf
    scores = _sigmoid(lg_ref[...])
    sel = scores + rb_ref[...]
    lane_i = lax.broadcasted_iota(I32, (tm, ne), 1)
    lane = lane_i.astype(F32)
    grp = (lane_i >> _log2(GROUP_SIZE)).astype(F32)

    def first_max(x):
        m = jnp.max(x, axis=-1, keepdims=True)
        idx = jnp.min(jnp.where(x == m, lane, float(ne)), axis=-1, keepdims=True)
        return m, idx

    gs = jnp.full((tm, ne), neg, F32)
    for gidx in range(N_GROUPS):
        sg = jnp.where(grp == float(gidx), sel, neg)
        m1, i1 = first_max(sg)
        m2 = jnp.max(jnp.where(lane == i1, neg, sg), axis=-1, keepdims=True)
        gs = jnp.where(lane == float(gidx), m1 + m2, gs)
    cur = jnp.full((tm, ne), neg, F32)
    for _ in range(TOPK_GROUPS):
        _, gi = first_max(gs)
        cur = jnp.where(grp == gi, sel, cur)
        gs = jnp.where(lane == gi, neg, gs)

    pm = jnp.zeros((tm, ne), F32)
    eidx, wts = [], []
    for _ in range(TOP_K):
        _, ei = first_max(cur)
        hit = lane == ei
        pm = jnp.where(hit, 1.0, pm)
        eidx.append(ei)
        wts.append(jnp.sum(jnp.where(hit, scores, 0.0), axis=-1, keepdims=True))
        cur = jnp.where(hit, neg, cur)
    wsum = wts[0]
    for w in wts[1:]:
        wsum = wsum + w

    ri = lax.broadcasted_iota(I32, (tm, tm), 0)
    ci = lax.broadcasted_iota(I32, (tm, tm), 1)
    below = jnp.where(ri > ci, 1.0, 0.0)
    rank = _mm(below, pm, passes=1) + carry[...]
    carry[...] = carry[...] + jnp.sum(pm, axis=0, keepdims=True)
    cnt_o[...] = carry[...]

    ol = lax.broadcasted_iota(I32, (tm, LANES), 1)
    e_out = jnp.zeros((tm, LANES), I32)
    r_out = jnp.zeros((tm, LANES), I32)
    w_out = jnp.zeros((tm, LANES), F32)
    for kk in range(TOP_K):
        rk = jnp.sum(jnp.where(lane == eidx[kk], rank, 0.0), axis=-1, keepdims=True)
        e_out = jnp.where(ol == kk, eidx[kk].astype(I32), e_out)
        r_out = jnp.where(ol == kk, rk.astype(I32), r_out)
        w_out = jnp.where(ol == kk, wts[kk] / wsum * ROUTED_SCALE, w_out)
    e_o[...] = e_out
    rk_o[...] = r_out
    w_o[...] = w_out


def _route_call(logits, router_b):
    tn, ne = logits.shape
    tm = TOK_TILE
    assert tn % tm == 0
    tok = lambda w: pl.BlockSpec((tm, w), lambda i: (i, 0))
    one = pl.BlockSpec((1, ne), lambda i: (0, 0))
    return pl.pallas_call(
        _route_body,
        grid=(tn // tm,),
        in_specs=[tok(ne), one],
        out_specs=[tok(LANES), tok(LANES), tok(LANES), one],
        out_shape=[jax.ShapeDtypeStruct((tn, LANES), I32), jax.ShapeDtypeStruct((tn, LANES), I32),
                   jax.ShapeDtypeStruct((tn, LANES), F32), jax.ShapeDtypeStruct((1, ne), F32)],
        scratch_shapes=[pltpu.VMEM((1, ne), F32)],
        compiler_params=_cparams(("arbitrary",)),
        name="moe_route",
    )(logits, router_b.reshape(1, ne))


def _dispatch_body(e_ref, rk_ref, ps_ref, h2_ref, xs_hbm, sem, *, tm):
    def issue(m, carry):
        for kk in range(TOP_K):
            j = m * TOP_K + kk
            row = ps_ref[e_ref[0, 0, j]] + rk_ref[0, 0, j]
            pltpu.make_async_copy(_slab(h2_ref, m), _slab(xs_hbm, row), sem).start(priority=kk % 2)
        return carry

    lax.fori_loop(0, tm, issue, 0)
    all_rows = xs_hbm.at[pl.ds(0, tm * TOP_K * PCH)]
    pltpu.make_async_copy(all_rows, all_rows, sem).wait()


def _assign_specs(tm, index_map):
    blk = pl.BlockSpec((1, 1, tm * TOP_K), index_map, memory_space=pltpu.SMEM)
    return blk, pl.BlockSpec(memory_space=pltpu.SMEM)


def _dispatch_call(eidx, rank, pad_start, h2s, n_rows):
    tn = h2s.shape[0] // PCH
    tm = TOK_TILE
    blk, whole = _assign_specs(tm, lambda i: (i, 0, 0))
    shp = (tn // tm, 1, tm * TOP_K)
    return pl.pallas_call(
        functools.partial(_dispatch_body, tm=tm),
        grid=(tn // tm,),
        in_specs=[blk, blk, whole, pl.BlockSpec((tm * PCH, LANES), lambda i: (i, 0))],
        out_specs=pl.BlockSpec(memory_space=pl.ANY),
        out_shape=jax.ShapeDtypeStruct((n_rows * PCH, LANES), I32),
        scratch_shapes=[pltpu.SemaphoreType.DMA],
        compiler_params=_cparams(("arbitrary",)),
        name="moe_dispatch",
    )(eidx.reshape(shp), rank.reshape(shp), pad_start, h2s)


def _expert_body(bi_ref, nr_ref, ld_ref, nx_ref, xs_ref, wg_hbm, wu_hbm, wd_hbm, ob_ref, wg_buf, wu_buf, wd_buf,
                 wg_bf, wu_bf, wd_bf, sem):
    del bi_ref
    i = pl.program_id(0)
    nr = nr_ref[i]
    slot = ld_ref[i]

    def fetch(e, s):
        return (pltpu.make_async_copy(wg_hbm.at[e], wg_buf.at[s], sem.at[s]),
                pltpu.make_async_copy(wu_hbm.at[e], wu_buf.at[s], sem.at[s]),
                pltpu.make_async_copy(wd_hbm.at[e], wd_buf.at[s], sem.at[s]))

    @pl.when(i == 0)
    def _():
        for cp in fetch(nx_ref[nx_ref.shape[0] - 1], 0):
            cp.start()

    @pl.when(slot >= 0)
    def _():
        for cp in fetch(0, slot):
            cp.wait()

        @pl.when(nx_ref[i] >= 0)
        def _():
            for cp in fetch(nx_ref[i], 1 - slot):
                cp.start()

        wg_bf[...] = wg_buf[slot].astype(BF16)
        wu_bf[...] = wu_buf[slot].astype(BF16)
        wd_bf[...] = wd_buf[slot].astype(BF16)

    @pl.when(nr > 0)
    def _():
        live = lax.broadcasted_iota(I32, (MOE_BLK, LANES), 0) < nr
        x = _rows_from_packed(xs_ref, MOE_BLK, live)
        hg = jnp.dot(x, wg_bf[...], preferred_element_type=F32)
        hu = jnp.dot(x, wu_bf[...], preferred_element_type=F32)
        hh = (hg * _sigmoid(hg) * hu).astype(BF16)
        _rows_to_packed(ob_ref, jnp.dot(hh, wd_bf[...], preferred_element_type=F32))


def _expert_tables(counts, pad_start, pad_end, nb):
    ne = counts.shape[0]
    first_row = jnp.arange(nb, dtype=I32) * MOE_BLK
    block_e = jnp.minimum(jnp.sum(pad_end[None, :] <= first_row[:, None], axis=1), ne - 1).astype(I32)
    block_rows = jnp.clip(pad_start[block_e] + counts[block_e] - first_row, 0, MOE_BLK).astype(I32)
    block_i = jnp.minimum(jnp.arange(nb, dtype=I32), pad_end[-1] // MOE_BLK - 1).astype(I32)
    has = counts > 0
    ordinal = jnp.cumsum(has.astype(I32)) - 1
    ids = jnp.where(has, jnp.arange(ne, dtype=I32), ne)
    nxt = jnp.concatenate([lax.cummin(ids, reverse=True)[1:], jnp.full((1,), ne, I32)])
    nxt = jnp.where(nxt < ne, nxt, -1).astype(I32)
    starts = (first_row == pad_start[block_e]) & (block_rows > 0)
    load_slot = jnp.where(starts, ordinal[block_e] % 2, -1).astype(I32)
    first_e = jnp.min(ids).astype(I32)
    next_e = jnp.concatenate([jnp.where(starts, nxt[block_e], -1).astype(I32), first_e[None]])
    return block_i, block_rows, load_slot, next_e


def _expert_call(tables, xs, wg, wu, wd):
    nb = xs.shape[0] // (MOE_BLK * PCH)
    d, ff = wg.shape[1], wg.shape[2]
    rows = pl.BlockSpec((MOE_BLK * PCH, LANES), lambda i, bi, nr, ld, nx: (bi[i], 0))
    hbm = pl.BlockSpec(memory_space=pl.ANY)
    grid_spec = pltpu.PrefetchScalarGridSpec(
        num_scalar_prefetch=4,
        grid=(nb,),
        in_specs=[rows, hbm, hbm, hbm],
        out_specs=rows,
        scratch_shapes=[pltpu.VMEM((2, d, ff), F32), pltpu.VMEM((2, d, ff), F32), pltpu.VMEM((2, ff, d), F32),
                        pltpu.VMEM((d, ff), BF16), pltpu.VMEM((d, ff), BF16), pltpu.VMEM((ff, d), BF16),
                        pltpu.SemaphoreType.DMA((2,))],
    )
    return pl.pallas_call(
        _expert_body,
        grid_spec=grid_spec,
        out_shape=jax.ShapeDtypeStruct(xs.shape, I32),
        compiler_params=_cparams(("arbitrary",), VMEM_LIMIT),
        name="moe_experts",
    )(*tables, xs, wg, wu, wd)


def _combine_body(e_ref, rk_ref, en_ref, rkn_ref, wt_ref, ps_ref, ob_hbm, h2_ref, x1_ref, mod_ref, sg_ref, su_ref,
                  sd_ref, fg_ref, out_ref, gbuf, rbuf, sem, *, tm, nl):
    bb, ll, d = x1_ref.shape
    step = pl.program_id(0) * nl + pl.program_id(1)
    last = pl.num_programs(0) * nl - 1
    slot = lax.rem(step, 2)

    def request(e_tab, rk_tab, m, s):
        for kk in range(TOP_K):
            j = m * TOP_K + kk
            row = ps_ref[e_tab[0, 0, j]] + rk_tab[0, 0, j]
            pltpu.make_async_copy(_slab(ob_hbm, row), _slab(gbuf.at[s], j), sem.at[s]).start(priority=kk % 2)

    def mix(m):
        rows = gbuf.at[slot]
        lo, hi = _unpack_pair(_slab(rows, m * TOP_K)[...])
        wt = wt_ref[0, 0, m * TOP_K]
        acc_lo, acc_hi = wt * lo, wt * hi
        for kk in range(1, TOP_K):
            lo, hi = _unpack_pair(_slab(rows, m * TOP_K + kk)[...])
            wt = wt_ref[0, 0, m * TOP_K + kk]
            acc_lo, acc_hi = acc_lo + wt * lo, acc_hi + wt * hi
        _fslab(rbuf, m)[...] = jnp.concatenate([acc_lo, acc_hi], axis=0)

    @pl.when(step == 0)
    def _():
        def first(m, carry):
            request(e_ref, rk_ref, m, 0)
            return carry
        lax.fori_loop(0, tm, first, 0)

    pltpu.make_async_copy(ob_hbm.at[pl.ds(0, tm * TOP_K * PCH)], gbuf.at[slot], sem.at[slot]).wait()

    @pl.when(step < last)
    def _():
        def both(m, carry):
            request(en_ref, rkn_ref, m, 1 - slot)
            mix(m)
            return carry
        lax.fori_loop(0, tm, both, 0)

    @pl.when(step == last)
    def _():
        def only(m, carry):
            mix(m)
            return carry
        lax.fori_loop(0, tm, only, 0)

    routed = jnp.concatenate([rbuf[pl.ds(c, tm, stride=CHUNKS), :] for c in range(CHUNKS)], axis=1)
    h2 = _rows_from_packed(h2_ref, tm)
    hg = jnp.dot(h2, sg_ref[...], preferred_element_type=F32)
    hu = jnp.dot(h2, su_ref[...], preferred_element_type=F32)
    shared = jnp.dot((hg * _sigmoid(hg) * hu).astype(BF16), sd_ref[...], preferred_element_type=F32)
    ff = (routed + shared).reshape(bb, ll, d)
    x2 = x1_ref[...] + mod_ref[:, 5:6, :] * ff
    out_ref[...] = x2 * lax.rsqrt(jnp.mean(x2 * x2, axis=-1, keepdims=True) + NORM_EPS) * fg_ref[...]


def _combine_call(eidx, rank, wts, pad_start, ob, h2s, x1, mod, p):
    bn, seq, d = x1.shape
    tm = CMB_TILE
    bb, ll = _tile(bn, seq, tm)
    nl = seq // ll
    tn = bn * seq
    nsteps = tn // tm
    smem, whole = _assign_specs(tm, lambda b, l: (b * nl + l, 0, 0))
    smem_next, _ = _assign_specs(tm, lambda b, l: (jnp.minimum(b * nl + l + 1, nsteps - 1), 0, 0))
    tok = pl.BlockSpec((bb, ll, d), lambda b, l: (b, l, 0))
    full = lambda a: pl.BlockSpec(a.shape, lambda b, l: (0,) * a.ndim)
    consts = (p["sh_gate"], p["sh_up"], p["sh_down"], p["final_g"])
    shp = (nsteps, 1, tm * TOP_K)
    e3, r3 = eidx.reshape(shp), rank.reshape(shp)
    return pl.pallas_call(
        functools.partial(_combine_body, tm=tm, nl=nl),
        grid=(bn // bb, nl),
        in_specs=[smem, smem, smem_next, smem_next, smem, whole, pl.BlockSpec(memory_space=pl.ANY),
                  pl.BlockSpec((tm * PCH, LANES), lambda b, l: (b * nl + l, 0)),
                  tok, pl.BlockSpec((bb, 6, d), lambda b, l: (b, 0, 0))] + [full(c) for c in consts],
        out_specs=tok,
        out_shape=jax.ShapeDtypeStruct((bn, seq, d), F32),
        scratch_shapes=[pltpu.VMEM((2, tm * TOP_K * PCH, LANES), I32), pltpu.VMEM((tm * CHUNKS, LANES), F32),
                        pltpu.SemaphoreType.DMA((2,))],
        compiler_params=_cparams(("arbitrary", "arbitrary"), VMEM_LIMIT),
        name="moe_combine_final",
    )(e3, r3, e3, r3, wts.reshape(shp), pad_start, ob, h2s, x1, mod, *consts)


def _rw_state_to_pairs(s):
    bn = s.shape[0]
    s = s.reshape(bn, RW_HEADS // 2, 2, RW_HEAD, RW_HEAD)
    z = jnp.zeros_like(s[:, :, 0])
    return jnp.concatenate([jnp.concatenate([s[:, :, 0], z], axis=-1),
                            jnp.concatenate([z, s[:, :, 1]], axis=-1)], axis=-2)


def _rw_state_from_pairs(sp):
    bn = sp.shape[0]
    s = jnp.stack([sp[:, :, :RW_HEAD, :RW_HEAD], sp[:, :, RW_HEAD:, RW_HEAD:]], axis=2)
    return s.reshape(bn, RW_HEADS, RW_HEAD, RW_HEAD)


def _gla_state_to_pairs(s):
    bn = s.shape[0]
    t = jnp.swapaxes(s, -1, -2).reshape(bn, GLA_HEADS // 2, 2, GLA_DV, GLA_DK)
    z = jnp.zeros_like(t[:, :, 0])
    return jnp.concatenate([jnp.concatenate([t[:, :, 0], z], axis=-1),
                            jnp.concatenate([z, t[:, :, 1]], axis=-1)], axis=-2)


def _gla_state_from_pairs(sp):
    bn = sp.shape[0]
    t = jnp.stack([sp[:, :, :GLA_DV, :GLA_DK], sp[:, :, GLA_DV:, GLA_DK:]], axis=2)
    return jnp.swapaxes(t.reshape(bn, GLA_HEADS, GLA_DV, GLA_DK), -1, -2)


def _layer_params(l, ada_w, ada_b, norm1_g, norm2_g, w_in, mu_shift, rw_w0, rw_w_up, rw_a0, rw_a_up, rw_g_up,
                  rw_k_k, rw_k_a, rw_r_k, rw_gn_g, rw_gn_b, gla_a_up, gla_a_bias, gla_norm_g, w_pa, w_pb, w_out,
                  router_w, router_b, exp_gate, exp_up, exp_down, sh_gate, sh_up, sh_down):
    d = D_MODEL
    wi = w_in[l]
    gla0 = RW_SHIFT_COLS
    xal0 = gla0 + QKV_W
    pad = jnp.zeros((d, XAL_W - GLA_GATE_RANK), F32)
    w_pack = jnp.concatenate([wi[:, :xal0], wi[:, xal0:xal0 + GLA_GATE_RANK], pad,
                              wi[:, xal0 + GLA_GATE_RANK:]], axis=1).astype(BF16)
    zr = jnp.zeros((RW_W_RANK, RW_WIDTH), F32)
    hid = jnp.arange(RW_WIDTH) // RW_HEAD
    row = lambda a: a.reshape(1, -1)
    rw_hi = router_w[l].astype(BF16)
    return dict(
        ada_w=ada_w[l], ada_b=ada_b[l], norm1_g=norm1_g[l], norm2_g=norm2_g[l].reshape(1, 1, d), w_pack=w_pack,
        mu=mu_shift[l].reshape(1, 1, -1), w0=row(rw_w0[l]), wup=jnp.concatenate([rw_w_up[l], zr], axis=0),
        a0=row(rw_a0[l]), aup=jnp.concatenate([zr, rw_a_up[l]], axis=0), gup=rw_g_up[l].astype(BF16),
        kk=row(rw_k_k[l]), ka=row(rw_k_a[l]), rk=row(rw_r_k[l]),
        bd64=(hid[:, None] == hid[None, :]).astype(BF16),
        gn_g=row(rw_gn_g[l]), gn_b=row(rw_gn_b[l]),
        gla_aup=jnp.concatenate([gla_a_up[l], jnp.zeros((XAL_W - GLA_GATE_RANK, GLA_KW), F32)], axis=0),
        gla_ab=row(gla_a_bias[l]), gla_ng=row(gla_norm_g[l]),
        w_pa=w_pa[l].astype(BF16), w_pb=w_pb[l].astype(BF16), w_out=w_out[l].astype(BF16),
        rw_hi=rw_hi, rw_lo=(router_w[l] - rw_hi.astype(F32)).astype(BF16), router_b=router_b[l],
        exp_gate=exp_gate[l], exp_up=exp_up[l], exp_down=exp_down[l],
        sh_gate=sh_gate[l].astype(BF16), sh_up=sh_up[l].astype(BF16), sh_down=sh_down[l].astype(BF16),
    )


def _mixer_group(x, mod, s_rw, s_sh, s_gla, p):
    pa, qkv, xal, gg, mg = _inproj_call(x, mod, p["norm1_g"], p["w_pack"])
    r, lw, k2, v, a_s, b_s, g, bonus, new_sh = _rwprep_call(pa, s_sh, p)
    y, rw_new = _rwscan_call(r, lw, k2, v, a_s, b_s, _rw_state_to_pairs(s_rw))
    o_b, gla_new = _gla_call(qkv, xal, gg, _gla_state_to_pairs(s_gla), p)
    x1, h2s, logits = _merge_call(y, g, bonus, o_b, mg, x, mod, p)
    states = (_rw_state_from_pairs(rw_new), new_sh[:, 0, :], _gla_state_from_pairs(gla_new))
    return x1, h2s, logits, states


def _moe(h2s, logits, p):
    tn = h2s.shape[0] // PCH
    eidx, rank, wts, counts = _route_call(logits, p["router_b"])
    eidx, rank, wts = eidx[:, :TOP_K], rank[:, :TOP_K], wts[:, :TOP_K]
    counts = counts[0].astype(I32)
    padded = (counts + MOE_BLK - 1) // MOE_BLK * MOE_BLK
    pad_end = jnp.cumsum(padded)
    pad_start = (pad_end - padded).astype(I32)
    nb = (tn * TOP_K + N_EXPERTS * (MOE_BLK - 1)) // MOE_BLK + 1
    tables = _expert_tables(counts, pad_start, pad_end, nb)
    xs = _dispatch_call(eidx, rank, pad_start, h2s, nb * MOE_BLK)
    ob = _expert_call(tables, xs, p["exp_gate"], p["exp_up"], p["exp_down"])
    return ob, (eidx, rank, wts, pad_start)


def kernel(x_prompt, x_sample, c_prompt, c_sample, state_rwkv, state_shift, state_gla, ada_w, ada_b, norm1_g,
           norm2_g, w_in, mu_shift, rw_w0, rw_w_up, rw_a0, rw_a_up, rw_g_up, rw_k_k, rw_k_a, rw_r_k, rw_gn_g,
           rw_gn_b, gla_a_up, gla_a_bias, gla_norm_g, w_pa, w_pb, w_out, router_w, router_b, exp_gate, exp_up,
           exp_down, sh_gate, sh_up, sh_down, final_g):
    depth = ada_w.shape[0]
    bp, bs = x_prompt.shape[0], x_sample.shape[0]
    tp = bp * x_prompt.shape[1]
    xs_g = [x_prompt, x_sample]
    c_all = jnp.concatenate([c_prompt, c_sample], axis=0)
    zeros = lambda shape: jnp.zeros(shape, x_prompt.dtype)
    new_states = [[], []]
    fg = final_g.reshape(1, 1, D_MODEL)
    for l in range(depth):
        p = _layer_params(l, ada_w, ada_b, norm1_g, norm2_g, w_in, mu_shift, rw_w0, rw_w_up, rw_a0, rw_a_up,
                          rw_g_up, rw_k_k, rw_k_a, rw_r_k, rw_gn_g, rw_gn_b, gla_a_up, gla_a_bias, gla_norm_g,
                          w_pa, w_pb, w_out, router_w, router_b, exp_gate, exp_up, exp_down, sh_gate, sh_up,
                          sh_down)
        p["final_g"] = fg
        mod_all = _mod_call(c_all, p["ada_w"], p["ada_b"])
        mods = [mod_all[:bp], mod_all[bp:]]
        states_in = [
            (zeros((bp, RW_HEADS, RW_HEAD, RW_HEAD)), zeros((bp, RW_SHIFT_COLS)),
             zeros((bp, GLA_HEADS, GLA_DK, GLA_DV))),
            (state_rwkv[l], state_shift[l], state_gla[l]),
        ]
        x1s, h2ss, lgs = [], [], []
        for gi in range(2):
            x1, h2s, logits, st = _mixer_group(xs_g[gi], mods[gi], *states_in[gi], p)
            x1s.append(x1)
            h2ss.append(h2s)
            lgs.append(logits)
            new_states[gi].append(st)
        h2_all = jnp.concatenate(h2ss, axis=0)
        ob, (eidx, rank, wts, pad_start) = _moe(h2_all, jnp.concatenate(lgs, axis=0), p)
        assert depth == 1, "the fused final norm assumes a single layer"
        xs_g = [
            _combine_call(eidx[:tp], rank[:tp], wts[:tp], pad_start, ob, h2ss[0], x1s[0], mods[0], p),
            _combine_call(eidx[tp:], rank[tp:], wts[tp:], pad_start, ob, h2ss[1], x1s[1], mods[1], p),
        ]
    stack = lambda gi, j: jnp.stack([s[j] for s in new_states[gi]])
    return (xs_g[0], xs_g[1], stack(0, 0), stack(0, 1), stack(0, 2), stack(1, 0), stack(1, 1), stack(1, 2))
```

```python
import functools

import jax
import jax.numpy as jnp
from jax import lax
from jax.experimental import pallas as pl
from jax.experimental.pallas import tpu as pltpu

F32, BF16, I32 = jnp.float32, jnp.bfloat16, jnp.int32

D_MODEL = 1024
RW_HEADS, RW_HEAD = 8, 64
RW_WIDTH = RW_HEADS * RW_HEAD
RW_W_RANK, RW_A_RANK, RW_G_RANK = 64, 64, 128
RW_GN_EPS = 64e-5
GLA_HEADS, GLA_DK, GLA_DV = 4, 64, 128
GLA_KW, GLA_VW = GLA_HEADS * GLA_DK, GLA_HEADS * GLA_DV
GLA_GATE_RANK = 16
GLA_GATE_TAU = 16.0
GLA_CHUNK = 16
RW_SHIFT_COLS = 3 * RW_WIDTH + RW_W_RANK + RW_A_RANK + RW_G_RANK
N_EXPERTS, TOP_K, N_GROUPS, TOPK_GROUPS = 256, 8, 8, 4
GROUP_SIZE = N_EXPERTS // N_GROUPS
EXPERT_FF = 256
ROUTED_SCALE = 2.5
NORM_EPS = 1e-6

LANES = 128
SUBLANES = 8
CHUNKS = D_MODEL // LANES
PCH = CHUNKS // 2
UNIT = 64
RW_SCAN_PASSES = (1, 1, 1, 1, 1)
GLA_UNITS_PER_STEP = 4
RW_UNITS_PER_STEP = 4
VMEM_LIMIT = 56 * 1024 * 1024

PA_W, QKV_W, XAL_W, GG_W, MG_W = RW_SHIFT_COLS, 2 * GLA_KW + GLA_VW, LANES, GLA_VW, 2 * D_MODEL
PACK_OFFS = (0, PA_W, PA_W + QKV_W, PA_W + QKV_W + XAL_W, PA_W + QKV_W + XAL_W + GG_W)
PACK_W = PA_W + QKV_W + XAL_W + GG_W + MG_W

TOK_TILE = 256
MOE_BLK = 256
EXPERT_PARTS = 1
CMB_TILE = 128

_DN = {
    "nn": (((1,), (0,)), ((), ())),
    "nt": (((1,), (1,)), ((), ())),
    "tn": (((0,), (0,)), ((), ())),
}


def _split(x, pieces):
    out, rem = [], x
    for i in range(pieces):
        p = rem.astype(BF16)
        out.append(p)
        if i + 1 < pieces:
            rem = rem - p.astype(F32)
    return out


def _mm(a, b, form="nn", passes=1):
    dn = _DN[form]
    if passes == 6:
        return lax.dot_general(a.astype(F32), b.astype(F32), dn, precision=lax.Precision.HIGHEST,
                               preferred_element_type=F32)
    if passes == 1:
        return lax.dot_general(a.astype(BF16), b.astype(BF16), dn, preferred_element_type=F32)
    ah, al = _split(a, 2)
    bh, bl = _split(b, 2)
    out = lax.dot_general(ah, bh, dn, preferred_element_type=F32)
    out = out + lax.dot_general(ah, bl, dn, preferred_element_type=F32)
    return out + lax.dot_general(al, bh, dn, preferred_element_type=F32)


def _mm01(m01, x, pieces=3):
    m = m01.astype(BF16)
    out = None
    for p in _split(x, pieces):
        t = lax.dot_general(m, p, _DN["nn"], preferred_element_type=F32)
        out = t if out is None else out + t
    return out


def _xmm01(x, m01, pieces=2):
    m = m01.astype(BF16)
    out = None
    for p in _split(x, pieces):
        t = lax.dot_general(p, m, _DN["nn"], preferred_element_type=F32)
        out = t if out is None else out + t
    return out


HI16 = -65536


def _bf16_bits(x):
    return lax.bitcast_convert_type(x.astype(BF16).astype(F32), I32)


def _unpack_pair(w):
    return lax.bitcast_convert_type(w << 16, F32), lax.bitcast_convert_type(w & HI16, F32)


def _rows_to_packed(ref, x, first=0):
    for c in range(PCH):
        lo = _bf16_bits(x[:, c * LANES:(c + 1) * LANES])
        hi = _bf16_bits(x[:, (c + PCH) * LANES:(c + PCH + 1) * LANES])
        ref[pl.ds(first * PCH + c, x.shape[0], stride=PCH), :] = ((lo >> 16) & 0xFFFF) | (hi & HI16)


def _rows_from_packed(ref, n, live=None, first=0):
    lows, highs = [], []
    for c in range(PCH):
        w = ref[pl.ds(first * PCH + c, n, stride=PCH), :]
        if live is not None:
            w = jnp.where(live, w, 0)
        lo, hi = _unpack_pair(w)
        lows.append(lo.astype(BF16))
        highs.append(hi.astype(BF16))
    return jnp.concatenate(lows + highs, axis=1)


def _slab(ref, row):
    return ref.at[pl.ds(pl.multiple_of(row * PCH, PCH), PCH)]


def _fslab(ref, row):
    return ref.at[pl.ds(pl.multiple_of(row * CHUNKS, CHUNKS), CHUNKS)]


def _sigmoid(x):
    return 1.0 / (1.0 + jnp.exp(-x))


def _softplus(x):
    return jnp.maximum(x, 0.0) + jnp.log(1.0 + jnp.exp(-jnp.abs(x)))


def _log2(n):
    assert n > 0 and n & (n - 1) == 0, n
    return n.bit_length() - 1


def _cparams(sem, vmem=None):
    return pltpu.CompilerParams(dimension_semantics=sem, vmem_limit_bytes=vmem)


def _mod_body(c_ref, w_ref, b_ref, o_ref):
    c = c_ref[...]
    o_ref[0] = _mm(c * _sigmoid(c), w_ref[...], passes=6) + b_ref[...]


def _mod_call(c_all, ada_w, ada_b):
    bt, d = c_all.shape
    out = pl.pallas_call(
        _mod_body,
        grid=(6,),
        in_specs=[pl.BlockSpec((bt, d), lambda k: (0, 0)),
                  pl.BlockSpec((d, d), lambda k: (0, k)),
                  pl.BlockSpec((1, d), lambda k: (0, k))],
        out_specs=pl.BlockSpec((1, bt, d), lambda k: (k, 0, 0)),
        out_shape=jax.ShapeDtypeStruct((6, bt, d), F32),
        compiler_params=_cparams(("arbitrary",)),
        name="adaln_mod",
    )(c_all, ada_w, ada_b.reshape(1, 6 * d))
    return jnp.transpose(out, (1, 0, 2))


def _inproj_body(x_ref, mod_ref, g_ref, w_ref, pa_ref, qkv_ref, xal_ref, gg_ref, mg_ref):
    bb, ll, d = x_ref.shape
    x = x_ref[...]
    y = x * lax.rsqrt(jnp.mean(x * x, axis=-1, keepdims=True) + NORM_EPS) * g_ref[...]
    h = y * (1.0 + mod_ref[:, 1:2, :]) + mod_ref[:, 0:1, :]
    hb = h.reshape(bb * ll, d).astype(BF16)
    for ref, off in zip((pa_ref, qkv_ref, xal_ref, gg_ref, mg_ref), PACK_OFFS):
        w = ref.shape[-1]
        ref[...] = jnp.dot(hb, w_ref[:, off:off + w], preferred_element_type=F32).reshape(bb, ll, w)


def _tile(bn, seq, tile):
    if seq >= tile:
        assert seq % tile == 0
        return 1, tile
    assert tile % seq == 0 and bn % (tile // seq) == 0
    return tile // seq, seq


def _inproj_call(x, mod, norm_g, w_pack):
    bn, seq, d = x.shape
    bb, ll = _tile(bn, seq, TOK_TILE)
    tok = lambda w: pl.BlockSpec((bb, ll, w), lambda b, l: (b, l, 0))
    widths = (PA_W, QKV_W, XAL_W, GG_W, MG_W)
    return pl.pallas_call(
        _inproj_body,
        grid=(bn // bb, seq // ll),
        in_specs=[tok(d),
                  pl.BlockSpec((bb, 6, d), lambda b, l: (b, 0, 0)),
                  pl.BlockSpec((1, 1, d), lambda b, l: (0, 0, 0)),
                  pl.BlockSpec((d, PACK_W), lambda b, l: (0, 0))],
        out_specs=[tok(w) for w in widths],
        out_shape=[jax.ShapeDtypeStruct((bn, seq, w), F32) for w in widths],
        compiler_params=_cparams(("arbitrary", "arbitrary"), VMEM_LIMIT),
        name="norm_inproj",
    )(x, mod, norm_g.reshape(1, 1, d), w_pack)


def _rwprep_body(pa_ref, sh_ref, mu_ref, w0_ref, wup_ref, a0_ref, aup_ref, gup_ref, kk_ref, ka_ref, rk_ref,
                 bd_ref, r_o, lw_o, k_o, v_o, a_o, b_o, g_o, bon_o, nsh_o, carry):
    bb, ll, wd = pa_ref.shape
    n = bb * ll
    hw = RW_WIDTH

    @pl.when(pl.program_id(1) == 0)
    def _():
        carry[...] = sh_ref[...]

    pa = pa_ref[...]
    rolled = pltpu.roll(pa.reshape(n, wd), 1, 0).reshape(bb, ll, wd)
    tok = lax.broadcasted_iota(I32, (bb, ll, wd), 1)
    prev = jnp.where(tok == 0, carry[...], rolled)
    last = pa_ref[:, ll - 1:ll, :]
    carry[...] = last
    nsh_o[...] = last
    xs = (pa + (prev - pa) * mu_ref[...]).reshape(n, wd)

    r, k, v = xs[:, 0:hw], xs[:, hw:2 * hw], xs[:, 2 * hw:3 * hw]
    xwa = xs[:, 3 * hw:3 * hw + LANES]
    xg = xs[:, 3 * hw + LANES:]
    w_log = -_softplus(-(w0_ref[...] + _mm(jnp.tanh(xwa), wup_ref[...], passes=3))) - 0.5
    lw = -jnp.exp(w_log)
    a = _sigmoid(a0_ref[...] + _mm(xwa, aup_ref[...], passes=3))
    g = _mm(_sigmoid(xg), gup_ref[...])
    bd = bd_ref[...]
    kkv = k * kk_ref[...]
    kkn = kkv * lax.rsqrt(jnp.maximum(_xmm01(kkv * kkv, bd), 1e-24))
    k2 = k * (1.0 + (a - 1.0) * ka_ref[...])
    bonus = _xmm01(r * k2 * rk_ref[...], bd) * v
    for ref, val in ((r_o, r), (lw_o, lw), (k_o, k2), (v_o, v), (a_o, -kkn), (b_o, kkn * a), (g_o, g),
                     (bon_o, bonus)):
        ref[...] = val.reshape(bb, ll, hw)


def _rwprep_call(pa, s_sh, p):
    bn, seq, wd = pa.shape
    bb, ll = _tile(bn, seq, TOK_TILE)
    hw = RW_WIDTH
    tok = lambda w: pl.BlockSpec((bb, ll, w), lambda b, l: (b, l, 0))
    row = lambda w: pl.BlockSpec((bb, 1, w), lambda b, l: (b, 0, 0))
    full = lambda a: pl.BlockSpec(a.shape, lambda b, l: (0,) * a.ndim)
    consts = (p["mu"], p["w0"], p["wup"], p["a0"], p["aup"], p["gup"], p["kk"], p["ka"], p["rk"], p["bd64"])
    outs = pl.pallas_call(
        _rwprep_body,
        grid=(bn // bb, seq // ll),
        in_specs=[tok(wd), row(wd)] + [full(c) for c in consts],
        out_specs=[tok(hw)] * 8 + [row(wd)],
        out_shape=[jax.ShapeDtypeStruct((bn, seq, hw), F32)] * 8 + [jax.ShapeDtypeStruct((bn, 1, wd), F32)],
        scratch_shapes=[pltpu.VMEM((bb, 1, wd), F32)],
        compiler_params=_cparams(("arbitrary", "arbitrary"), VMEM_LIMIT),
        name="rwkv_prep",
    )(pa, s_sh.reshape(bn, 1, wd), *consts)
    return outs


def _unit_masks(n, tl):
    ri = lax.broadcasted_iota(I32, (n, n), 0)
    ci = lax.broadcasted_iota(I32, (n, n), 1)
    same = (ri >> _log2(tl)) == (ci >> _log2(tl))
    return same, same & (ri > ci), same & (ri >= ci)


def _rwscan_body(r_ref, lw_ref, k_ref, v_ref, a_ref, b_ref, s0_ref, y_ref, sn_ref, st, *, nu, nseq, tl, passes):
    n = nseq * tl
    n2 = 2 * n
    p_aa, p_inv, p_apply, p_state, p_y = passes

    hd = RW_HEAD

    @pl.when(pl.program_id(1) == 0)
    def _():
        zero = jnp.zeros((hd, hd), F32)
        for q in range(nu * nseq):
            for p in range(RW_HEADS // 2):
                st[q, p] = jnp.concatenate(
                    [jnp.concatenate([s0_ref[q, 2 * p], zero], axis=1),
                     jnp.concatenate([zero, s0_ref[q, 2 * p + 1]], axis=1)], axis=0)

    same, _, incl = _unit_masks(n, tl)
    m_cum = jnp.where(incl, 1.0, 0.0)
    m_seq = jnp.where(same, 1.0, 0.0)
    ri = lax.broadcasted_iota(I32, (n2, n2), 0)
    ci = lax.broadcasted_iota(I32, (n2, n2), 1)
    rt, ct = ri & (n - 1), ci & (n - 1)
    dsame = ((rt >> _log2(tl)) == (ct >> _log2(tl))) & ((ri >> _log2(n)) == (ci >> _log2(n)))
    strict_d = dsame & (rt > ct)
    incl_d = dsame & (rt >= ct)
    eye_d = jnp.where(ri == ci, 1.0, 0.0)
    lane = lax.broadcasted_iota(I32, (1, LANES), 1)
    m0 = jnp.where(lane < RW_HEAD, 1.0, 0.0)
    m1 = 1.0 - m0

    def dup(x):
        return jnp.concatenate([x * m0, x * m1], axis=0)

    def seq_rows(x, q):
        if nseq == 1:
            return x
        return jnp.concatenate([x[q * tl:(q + 1) * tl], x[n + q * tl:n + (q + 1) * tl]], axis=0)

    def unit_rows(parts):
        if nseq == 1:
            return parts[0]
        return jnp.concatenate([p[0:tl] for p in parts] + [p[tl:2 * tl] for p in parts], axis=0)

    chains = [(u, p) for u in range(nu) for p in range(RW_HEADS // 2)]
    ids = range(len(chains))
    cat0 = lambda *xs: jnp.concatenate(xs, axis=0)

    def ld(ref, c):
        u, p = chains[c]
        return ref[u * nseq:(u + 1) * nseq, :, p * LANES:(p + 1) * LANES].reshape(n, LANES)

    lw = [ld(lw_ref, c) for c in ids]
    cum = [_mm01(m_cum, x) for x in lw]
    tot = [_mm01(m_seq, x) for x in lw]
    e_c = [jnp.exp(x) for x in cum]
    e_n = [jnp.exp(-x) for x in cum]
    e_l = [jnp.exp(t - x) for t, x in zip(tot, cum)]
    at_d = [dup(ld(a_ref, c) * jnp.exp(cum[c] - lw[c])) for c in ids]
    rt_d = [dup(ld(r_ref, c) * e_c[c]) for c in ids]
    bt_d = [dup(ld(b_ref, c) * e_n[c]) for c in ids]
    kt_d = [dup(ld(k_ref, c) * e_n[c]) for c in ids]
    bh_d = [dup(ld(b_ref, c) * e_l[c]) for c in ids]
    kh_d = [dup(ld(k_ref, c) * e_l[c]) for c in ids]
    v_d = [dup(ld(v_ref, c)) for c in ids]
    aa = [_mm(cat0(at_d[c], rt_d[c]), cat0(bt_d[c], kt_d[c]), "nt", p_aa) for c in ids]
    a_ab = [jnp.where(strict_d, x[0:n2, 0:n2], 0.0) for x in aa]
    a_ak = [jnp.where(strict_d, x[0:n2, n2:], 0.0) for x in aa]
    a_rb = [jnp.where(incl_d, x[n2:, 0:n2], 0.0) for x in aa]
    a_rk = [jnp.where(incl_d, x[n2:, n2:], 0.0) for x in aa]
    zy = [_mm(cat0(a_ak[c], a_rk[c]), v_d[c], passes=p_apply) for c in ids]
    tinv = [eye_d + x for x in a_ab]
    nk = a_ab
    for _ in range(_log2(tl) - 1):
        nk = [_mm(x, x, passes=p_inv) for x in nk]
        tinv = [t + _mm(t, x, passes=p_inv) for t, x in zip(tinv, nk)]
    wu = [_mm(tinv[c], jnp.concatenate([at_d[c], zy[c][0:n2]], axis=1), passes=p_apply) for c in ids]
    seqs = range(nseq)
    srow = lambda c, q: (chains[c][0] * nseq + q, chains[c][1])
    s_old = [[st[srow(c, q)] for q in seqs] for c in ids]
    xs = [[_mm(cat0(seq_rows(wu[c][:, 0:LANES], q), seq_rows(rt_d[c], q)), s_old[c][q], "nt", p_state)
           for q in seqs] for c in ids]
    u_q = [[xs[c][q][0:2 * tl] + seq_rows(wu[c][:, LANES:], q) for q in seqs] for c in ids]
    for c in ids:
        for q in seqs:
            g_c = jnp.exp(tot[c][q * tl:q * tl + 1, :])
            st[srow(c, q)] = s_old[c][q] * g_c + _mm(cat0(u_q[c][q], seq_rows(v_d[c], q)),
                                                     cat0(seq_rows(bh_d[c], q), seq_rows(kh_d[c], q)), "tn", p_state)
    for c in ids:
        u, p = chains[c]
        y_d = (unit_rows([xs[c][q][2 * tl:] for q in seqs]) + _mm(a_rb[c], unit_rows(u_q[c]), passes=p_y)
               + zy[c][n2:])
        y_ref[u * nseq:(u + 1) * nseq, :, p * LANES:(p + 1) * LANES] = (y_d[0:n] + y_d[n:]).reshape(nseq, tl, LANES)

    @pl.when(pl.program_id(1) == pl.num_programs(1) - 1)
    def _():
        for q in range(nu * nseq):
            for p in range(RW_HEADS // 2):
                s = st[q, p]
                sn_ref[q, 2 * p] = s[0:hd, 0:hd]
                sn_ref[q, 2 * p + 1] = s[hd:, hd:]


def _unit_shape(bn, seq):
    if seq >= UNIT:
        assert seq % UNIT == 0
        return 1, UNIT
    assert UNIT % seq == 0 and bn % (UNIT // seq) == 0
    return UNIT // seq, seq


def _rwscan_call(r, lw, k2, v, a_s, b_s, s0, passes=RW_SCAN_PASSES):
    bn, seq, hw = r.shape
    nseq, tl = _unit_shape(bn, seq)
    nu = RW_UNITS_PER_STEP if bn % (RW_UNITS_PER_STEP * nseq) == 0 else 1
    rows = nu * nseq
    tok = pl.BlockSpec((rows, tl, hw), lambda b, c: (b, c, 0))
    stt = pl.BlockSpec((rows, RW_HEADS, RW_HEAD, RW_HEAD), lambda b, c: (b, 0, 0, 0))
    return pl.pallas_call(
        functools.partial(_rwscan_body, nu=nu, nseq=nseq, tl=tl, passes=passes),
        grid=(bn // rows, seq // tl),
        in_specs=[tok] * 6 + [stt],
        out_specs=[tok, stt],
        out_shape=[jax.ShapeDtypeStruct((bn, seq, hw), F32), jax.ShapeDtypeStruct(s0.shape, F32)],
        scratch_shapes=[pltpu.VMEM((rows, RW_HEADS // 2, LANES, LANES), F32)],
        compiler_params=_cparams(("arbitrary", "arbitrary"), VMEM_LIMIT),
        name="rwkv_scan",
    )(r, lw, k2, v, a_s, b_s, s0)


def _gla_body(qkv_ref, xal_ref, gate_ref, aup_ref, ab_ref, ng_ref, s0_ref, o_ref, sn_ref, st, *, nu, nseq, tl, cs):
    n = nseq * tl
    n2 = 2 * n
    nsub = tl // cs

    @pl.when(pl.program_id(1) == 0)
    def _():
        zero = jnp.zeros((GLA_DV, GLA_DK), F32)
        for q in range(nu * nseq):
            for p in range(GLA_HEADS // 2):
                st[q, p] = jnp.concatenate(
                    [jnp.concatenate([s0_ref[q, 2 * p].T, zero], axis=1),
                     jnp.concatenate([zero, s0_ref[q, 2 * p + 1].T], axis=1)], axis=0)

    same, _, incl = _unit_masks(n, cs)
    m_cum = jnp.where(incl, 1.0, 0.0)
    m_sub = jnp.where(same, 1.0, 0.0)
    ri = lax.broadcasted_iota(I32, (n2, n2), 0)
    ci = lax.broadcasted_iota(I32, (n2, n2), 1)
    rt, ct = ri & (n - 1), ci & (n - 1)
    causal_d = ((rt >> _log2(cs)) == (ct >> _log2(cs))) & ((ri >> _log2(n)) == (ci >> _log2(n))) & (rt >= ct)
    lane = lax.broadcasted_iota(I32, (1, LANES), 1)
    m0 = jnp.where(lane < GLA_DK, 1.0, 0.0)
    m1 = 1.0 - m0
    sr = lax.broadcasted_iota(I32, (2 * GLA_DV, LANES), 0)
    sc = lax.broadcasted_iota(I32, (2 * GLA_DV, LANES), 1)
    st_mask = jnp.where((sr >> _log2(GLA_DV)) == (sc >> _log2(GLA_DK)), 1.0, 0.0)

    def dup(x):
        return jnp.concatenate([x * m0, x * m1], axis=0)

    chains = [(u, p) for u in range(nu) for p in range(GLA_HEADS // 2)]
    ids = range(len(chains))
    urows = lambda u: slice(u * nseq, (u + 1) * nseq)
    ng = ng_ref[...]
    la_all = [-_softplus(-(_mm(xal_ref[urows(u), :, :].reshape(n, LANES), aup_ref[...], passes=3) + ab_ref[...]))
              * (1.0 / GLA_GATE_TAU) for u in range(nu)]

    def ld(ref, c, off, width):
        return ref[urows(chains[c][0]), :, off:off + width].reshape(n, width)

    q = [ld(qkv_ref, c, chains[c][1] * LANES, LANES) * (GLA_DK ** -0.5) for c in ids]
    k = [ld(qkv_ref, c, GLA_KW + chains[c][1] * LANES, LANES) for c in ids]
    vp = [ld(qkv_ref, c, 2 * GLA_KW + chains[c][1] * 2 * GLA_DV, 2 * GLA_DV) for c in ids]
    la = [la_all[u][:, p * LANES:(p + 1) * LANES] for u, p in chains]
    bc = [_mm01(m_cum, x) for x in la]
    bl = [_mm01(m_sub, x) for x in la]
    qe = [q[c] * jnp.exp(bc[c]) for c in ids]
    ke = [k[c] * jnp.exp(-bc[c]) for c in ids]
    kd = [k[c] * jnp.exp(bl[c] - bc[c]) for c in ids]
    att = [jnp.where(causal_d, _mm(dup(qe[c]), dup(ke[c]), "nt", passes=1), 0.0) for c in ids]
    v_st = [jnp.concatenate([x[:, 0:GLA_DV], x[:, GLA_DV:]], axis=0) for x in vp]
    o_st = [_mm(att[c], v_st[c], passes=1) for c in ids]
    upd = [[_mm(vp[c][r0:r0 + cs], kd[c][r0:r0 + cs], "tn", passes=1) for r0 in range(0, n, cs)] for c in ids]
    inter = [[None] * (n // cs) for _ in ids]
    for sq in range(nseq):
        s = [st[chains[c][0] * nseq + sq, chains[c][1]] for c in ids]
        for j in range(nsub):
            i = sq * nsub + j
            r0 = i * cs
            for c in ids:
                inter[c][i] = _mm(qe[c][r0:r0 + cs], s[c], "nt", passes=1)
                s[c] = s[c] * jnp.exp(bl[c][r0:r0 + 1, :]) + st_mask * upd[c][i]
        for c in ids:
            st[chains[c][0] * nseq + sq, chains[c][1]] = s[c]
    for c in ids:
        u, p = chains[c]
        o = o_st[c] + jnp.concatenate([x[:, 0:GLA_DV] for x in inter[c]] + [x[:, GLA_DV:] for x in inter[c]], axis=0)
        o = o * lax.rsqrt(jnp.mean(o * o, axis=-1, keepdims=True) + NORM_EPS) * ng
        goff = p * 2 * GLA_DV
        gp = ld(gate_ref, c, goff, 2 * GLA_DV)
        g_st = jnp.concatenate([gp[:, 0:GLA_DV], gp[:, GLA_DV:]], axis=0)
        ob = o * (g_st * _sigmoid(g_st))
        o_ref[urows(u), :, goff:goff + GLA_DV] = ob[0:n].reshape(nseq, tl, GLA_DV)
        o_ref[urows(u), :, goff + GLA_DV:goff + 2 * GLA_DV] = ob[n:].reshape(nseq, tl, GLA_DV)

    @pl.when(pl.program_id(1) == pl.num_programs(1) - 1)
    def _():
        for q in range(nu * nseq):
            for p in range(GLA_HEADS // 2):
                s = st[q, p]
                sn_ref[q, 2 * p] = s[0:GLA_DV, 0:GLA_DK].T
                sn_ref[q, 2 * p + 1] = s[GLA_DV:, GLA_DK:].T


def _gla_call(qkv, xal, gate, s0, p):
    bn, seq, _ = qkv.shape
    nseq, tl = _unit_shape(bn, seq)
    cs = min(GLA_CHUNK, seq)
    assert tl % cs == 0
    nu = GLA_UNITS_PER_STEP if bn % (GLA_UNITS_PER_STEP * nseq) == 0 else 1
    rows = nu * nseq
    tok = lambda w: pl.BlockSpec((rows, tl, w), lambda b, c: (b, c, 0))
    full = lambda a: pl.BlockSpec(a.shape, lambda b, c: (0,) * a.ndim)
    stt = pl.BlockSpec((rows, GLA_HEADS, GLA_DK, GLA_DV), lambda b, c: (b, 0, 0, 0))
    consts = (p["gla_aup"], p["gla_ab"], p["gla_ng"])
    return pl.pallas_call(
        functools.partial(_gla_body, nu=nu, nseq=nseq, tl=tl, cs=cs),
        grid=(bn // rows, seq // tl),
        in_specs=[tok(QKV_W), tok(XAL_W), tok(GG_W)] + [full(c) for c in consts] + [stt],
        out_specs=[tok(GLA_VW), stt],
        out_shape=[jax.ShapeDtypeStruct((bn, seq, GLA_VW), F32), jax.ShapeDtypeStruct(s0.shape, F32)],
        scratch_shapes=[pltpu.VMEM((rows, GLA_HEADS // 2, 2 * GLA_DV, LANES), F32)],
        compiler_params=_cparams(("arbitrary", "arbitrary"), VMEM_LIMIT),
        name="gla_chunked",
    )(qkv, xal, gate, *consts, s0)


def _merge_body(y_ref, g_ref, bon_ref, ob_ref, mg_ref, x_ref, mod_ref, gng_ref, gnb_ref, bd_ref, wpa_ref,
                wpb_ref, wout_ref, n2_ref, rwh_ref, rwl_ref, x1_o, h2_o, lg_o):
    bb, ll, d = x_ref.shape
    n = bb * ll
    hw = RW_WIDTH
    bd = bd_ref[...]
    y = y_ref[...].reshape(n, hw)
    mu = _xmm01(y, bd, pieces=3) * (1.0 / RW_HEAD)
    dv = y - mu
    var = _xmm01(dv * dv, bd) * (1.0 / RW_HEAD)
    yn = dv * lax.rsqrt(var + RW_GN_EPS) * gng_ref[...] + gnb_ref[...]
    o_a = (yn + bon_ref[...].reshape(n, hw)) * g_ref[...].reshape(n, hw)
    o_b = ob_ref[...].reshape(n, GLA_VW)
    mg = mg_ref[...].reshape(n, 2 * d)
    merged = _sigmoid(mg[:, 0:d]) * _mm(o_a, wpa_ref[...]) + _sigmoid(mg[:, d:]) * _mm(o_b, wpb_ref[...])
    mix = _mm(merged, wout_ref[...]).reshape(bb, ll, d)
    x1 = x_ref[...] + mod_ref[:, 2:3, :] * mix
    x1_o[...] = x1
    yn2 = x1 * lax.rsqrt(jnp.mean(x1 * x1, axis=-1, keepdims=True) + NORM_EPS) * n2_ref[...]
    h2 = (yn2 * (1.0 + mod_ref[:, 4:5, :]) + mod_ref[:, 3:4, :]).reshape(n, d)
    hh, hl = _split(h2, 2)
    rwh, rwl = rwh_ref[...], rwl_ref[...]
    lg_o[...] = (jnp.dot(hh, rwh, preferred_element_type=F32) + jnp.dot(hh, rwl, preferred_element_type=F32)
                 + jnp.dot(hl, rwh, preferred_element_type=F32))
    _rows_to_packed(h2_o, h2)


def _merge_call(y, g, bonus, o_b, mg, x, mod, p):
    bn, seq, d = x.shape
    bb, ll = _tile(bn, seq, TOK_TILE)
    nl = seq // ll
    tn = bn * seq
    tok = lambda w: pl.BlockSpec((bb, ll, w), lambda b, l: (b, l, 0))
    full = lambda a: pl.BlockSpec(a.shape, lambda b, l: (0,) * a.ndim)
    consts = (p["gn_g"], p["gn_b"], p["bd64"], p["w_pa"], p["w_pb"], p["w_out"], p["norm2_g"], p["rw_hi"],
              p["rw_lo"])
    return pl.pallas_call(
        _merge_body,
        grid=(bn // bb, nl),
        in_specs=[tok(RW_WIDTH)] * 3 + [tok(GLA_VW), tok(MG_W), tok(d),
                                        pl.BlockSpec((bb, 6, d), lambda b, l: (b, 0, 0))] + [full(c) for c in consts],
        out_specs=[tok(d),
                   pl.BlockSpec((bb * ll * PCH, LANES), lambda b, l: (b * nl + l, 0)),
                   pl.BlockSpec((bb * ll, N_EXPERTS), lambda b, l: (b * nl + l, 0))],
        out_shape=[jax.ShapeDtypeStruct((bn, seq, d), F32),
                   jax.ShapeDtypeStruct((tn * PCH, LANES), I32),
                   jax.ShapeDtypeStruct((tn, N_EXPERTS), F32)],
        compiler_params=_cparams(("arbitrary", "arbitrary"), VMEM_LIMIT),
        name="merge_outproj_router",
    )(y, g, bonus, o_b, mg, x, mod, *consts)


def _route_body(lg_ref, rb_ref, e_o, rk_o, w_o, cnt_o, carry):
    tm = lg_ref.shape[0]
    ne = N_EXPERTS

    @pl.when(pl.program_id(0) == 0)
    def _():
        carry[...] = jnp.zeros_like(carry)

    neg = -jnp.inf
    scores = _sigmoid(lg_ref[...])
    sel = scores + rb_ref[...]
    lane_i = lax.broadcasted_iota(I32, (tm, ne), 1)
    lane = lane_i.astype(F32)
    grp = (lane_i >> _log2(GROUP_SIZE)).astype(F32)

    def first_max(x):
        m = jnp.max(x, axis=-1, keepdims=True)
        idx = jnp.min(jnp.where(x == m, lane, float(ne)), axis=-1, keepdims=True)
        return m, idx

    gs = jnp.full((tm, ne), neg, F32)
    for gidx in range(N_GROUPS):
        sg = jnp.where(grp == float(gidx), sel, neg)
        m1, i1 = first_max(sg)
        m2 = jnp.max(jnp.where(lane == i1, neg, sg), axis=-1, keepdims=True)
        gs = jnp.where(lane == float(gidx), m1 + m2, gs)
    cur = jnp.full((tm, ne), neg, F32)
    for _ in range(TOPK_GROUPS):
        _, gi = first_max(gs)
        cur = jnp.where(grp == gi, sel, cur)
        gs = jnp.where(lane == gi, neg, gs)

    pm = jnp.zeros((tm, ne), F32)
    eidx, wts = [], []
    for _ in range(TOP_K):
        _, ei = first_max(cur)
        hit = lane == ei
        pm = jnp.where(hit, 1.0, pm)
        eidx.append(ei)
        wts.append(jnp.sum(jnp.where(hit, scores, 0.0), axis=-1, keepdims=True))
        cur = jnp.where(hit, neg, cur)
    wsum = wts[0]
    for w in wts[1:]:
        wsum = wsum + w

    ri = lax.broadcasted_iota(I32, (tm, tm), 0)
    ci = lax.broadcasted_iota(I32, (tm, tm), 1)
    below = jnp.where(ri > ci, 1.0, 0.0)
    rank = _mm(below, pm, passes=1) + carry[...]
    carry[...] = carry[...] + jnp.sum(pm, axis=0, keepdims=True)
    cnt_o[...] = carry[...]

    ol = lax.broadcasted_iota(I32, (tm, LANES), 1)
    e_out = jnp.zeros((tm, LANES), F32)
    r_out = jnp.zeros((tm, LANES), F32)
    w_out = jnp.zeros((tm, LANES), F32)
    for kk in range(TOP_K):
        rk = jnp.sum(jnp.where(lane == eidx[kk], rank, 0.0), axis=-1, keepdims=True)
        e_out = jnp.where(ol == kk, eidx[kk], e_out)
        r_out = jnp.where(ol == kk, rk, r_out)
        w_out = jnp.where(ol == kk, wts[kk] / wsum * ROUTED_SCALE, w_out)
    e_o[0] = e_out.T[0:TOP_K].astype(I32)
    rk_o[0] = r_out.T[0:TOP_K].astype(I32)
    w_o[0] = w_out.T[0:TOP_K]


def _route_call(logits, router_b):
    tn, ne = logits.shape
    tm = TOK_TILE
    assert tn % tm == 0
    tok = lambda w: pl.BlockSpec((tm, w), lambda i: (i, 0))
    one = pl.BlockSpec((1, ne), lambda i: (0, 0))
    tab = pl.BlockSpec((1, TOP_K, tm), lambda i: (i, 0, 0))
    tab_shape = (tn // tm, TOP_K, tm)
    return pl.pallas_call(
        _route_body,
        grid=(tn // tm,),
        in_specs=[tok(ne), one],
        out_specs=[tab, tab, tab, one],
        out_shape=[jax.ShapeDtypeStruct(tab_shape, I32), jax.ShapeDtypeStruct(tab_shape, I32),
                   jax.ShapeDtypeStruct(tab_shape, F32), jax.ShapeDtypeStruct((1, ne), F32)],
        scratch_shapes=[pltpu.VMEM((1, ne), F32)],
        compiler_params=_cparams(("arbitrary",)),
        name="moe_route",
    )(logits, router_b.reshape(1, ne))


def _dispatch_body(e_ref, rk_ref, ps_ref, h2_ref, xs_hbm, sem, *, tm):
    def issue(m, carry):
        for kk in range(TOP_K):
            row = ps_ref[e_ref[0, kk, m]] + rk_ref[0, kk, m]
            pltpu.make_async_copy(_slab(h2_ref, m), _slab(xs_hbm, row), sem).start(priority=kk % 2)
        return carry

    lax.fori_loop(0, tm, issue, 0)
    all_rows = xs_hbm.at[pl.ds(0, tm * TOP_K * PCH)]
    pltpu.make_async_copy(all_rows, all_rows, sem).wait()


def _assign_specs(tm, index_map):
    blk = pl.BlockSpec((1, TOP_K, tm), index_map, memory_space=pltpu.SMEM)
    return blk, pl.BlockSpec(memory_space=pltpu.SMEM)


def _dispatch_call(eidx, rank, pad_start, h2s, n_rows):
    tn = h2s.shape[0] // PCH
    tm = TOK_TILE
    assert eidx.shape == (tn // tm, TOP_K, tm)
    blk, whole = _assign_specs(tm, lambda i: (i, 0, 0))
    return pl.pallas_call(
        functools.partial(_dispatch_body, tm=tm),
        grid=(tn // tm,),
        in_specs=[blk, blk, whole, pl.BlockSpec((tm * PCH, LANES), lambda i: (i, 0))],
        out_specs=pl.BlockSpec(memory_space=pl.ANY),
        out_shape=jax.ShapeDtypeStruct((n_rows * PCH, LANES), I32),
        scratch_shapes=[pltpu.SemaphoreType.DMA],
        compiler_params=_cparams(("arbitrary",)),
        name="moe_dispatch",
    )(eidx, rank, pad_start, h2s)


def _expert_body(bi_ref, nr_ref, ld_ref, nx_ref, xs_ref, wg_hbm, wu_hbm, wd_hbm, ob_ref, wg_buf, wu_buf, wd_buf,
                 wg_bf, wu_bf, wd_bf, sem):
    del bi_ref
    i = pl.program_id(0)
    nr = nr_ref[i]
    slot = ld_ref[i]

    def fetch(e, s):
        return (pltpu.make_async_copy(wg_hbm.at[e], wg_buf.at[s], sem.at[s]),
                pltpu.make_async_copy(wu_hbm.at[e], wu_buf.at[s], sem.at[s]),
                pltpu.make_async_copy(wd_hbm.at[e], wd_buf.at[s], sem.at[s]))

    @pl.when(i == 0)
    def _():
        for cp in fetch(nx_ref[nx_ref.shape[0] - 1], 0):
            cp.start()

    @pl.when(slot >= 0)
    def _():
        for cp in fetch(0, slot):
            cp.wait()

        @pl.when(nx_ref[i] >= 0)
        def _():
            for cp in fetch(nx_ref[i], 1 - slot):
                cp.start()

        wg_bf[...] = wg_buf[slot].astype(BF16)
        wu_bf[...] = wu_buf[slot].astype(BF16)
        wd_bf[...] = wd_buf[slot].astype(BF16)

    @pl.when(nr > 0)
    def _():
        part = MOE_BLK // EXPERT_PARTS
        firsts = [q * part for q in range(EXPERT_PARTS)]
        rid = lax.broadcasted_iota(I32, (part, LANES), 0)
        x = [_rows_from_packed(xs_ref, part, rid < nr - f, f) for f in firsts]
        hg = [jnp.dot(v, wg_bf[...], preferred_element_type=F32) for v in x]
        hu = [jnp.dot(v, wu_bf[...], preferred_element_type=F32) for v in x]
        hh = [(g * _sigmoid(g) * u).astype(BF16) for g, u in zip(hg, hu)]
        out = [jnp.dot(v, wd_bf[...], preferred_element_type=F32) for v in hh]
        for f, v in zip(firsts, out):
            _rows_to_packed(ob_ref, v, f)


def _expert_tables(counts, pad_start, pad_end, nb):
    ne = counts.shape[0]
    first_row = jnp.arange(nb, dtype=I32) * MOE_BLK
    block_e = jnp.minimum(jnp.sum(pad_end[None, :] <= first_row[:, None], axis=1), ne - 1).astype(I32)
    block_rows = jnp.clip(pad_start[block_e] + counts[block_e] - first_row, 0, MOE_BLK).astype(I32)
    block_i = jnp.minimum(jnp.arange(nb, dtype=I32), pad_end[-1] // MOE_BLK - 1).astype(I32)
    has = counts > 0
    ordinal = jnp.cumsum(has.astype(I32)) - 1
    ids = jnp.where(has, jnp.arange(ne, dtype=I32), ne)
    nxt = jnp.concatenate([lax.cummin(ids, reverse=True)[1:], jnp.full((1,), ne, I32)])
    nxt = jnp.where(nxt < ne, nxt, -1).astype(I32)
    starts = (first_row == pad_start[block_e]) & (block_rows > 0)
    load_slot = jnp.where(starts, ordinal[block_e] % 2, -1).astype(I32)
    first_e = jnp.min(ids).astype(I32)
    next_e = jnp.concatenate([jnp.where(starts, nxt[block_e], -1).astype(I32), first_e[None]])
    return block_i, block_rows, load_slot, next_e


def _expert_call(tables, xs, wg, wu, wd):
    nb = xs.shape[0] // (MOE_BLK * PCH)
    d, ff = wg.shape[1], wg.shape[2]
    rows = pl.BlockSpec((MOE_BLK * PCH, LANES), lambda i, bi, nr, ld, nx: (bi[i], 0))
    hbm = pl.BlockSpec(memory_space=pl.ANY)
    grid_spec = pltpu.PrefetchScalarGridSpec(
        num_scalar_prefetch=4,
        grid=(nb,),
        in_specs=[rows, hbm, hbm, hbm],
        out_specs=rows,
        scratch_shapes=[pltpu.VMEM((2, d, ff), F32), pltpu.VMEM((2, d, ff), F32), pltpu.VMEM((2, ff, d), F32),
                        pltpu.VMEM((d, ff), BF16), pltpu.VMEM((d, ff), BF16), pltpu.VMEM((ff, d), BF16),
                        pltpu.SemaphoreType.DMA((2,))],
    )
    return pl.pallas_call(
        _expert_body,
        grid_spec=grid_spec,
        out_shape=jax.ShapeDtypeStruct(xs.shape, I32),
        compiler_params=_cparams(("arbitrary",), VMEM_LIMIT),
        name="moe_experts",
    )(*tables, xs, wg, wu, wd)


def _combine_body(e_ref, rk_ref, en_ref, rkn_ref, wt_ref, ps_ref, ob_hbm, h2_ref, x1_ref, mod_ref, sg_ref, su_ref,
                  sd_ref, fg_ref, out_ref, gbuf, rbuf, sem, *, tm, nl):
    bb, ll, d = x1_ref.shape
    step = pl.program_id(0) * nl + pl.program_id(1)
    last = pl.num_programs(0) * nl - 1
    slot = lax.rem(step, 2)

    def request(e_tab, rk_tab, m, s):
        for kk in range(TOP_K):
            row = ps_ref[e_tab[0, kk, m]] + rk_tab[0, kk, m]
            pltpu.make_async_copy(_slab(ob_hbm, row), _slab(gbuf.at[s], m * TOP_K + kk),
                                  sem.at[s]).start(priority=kk % 2)

    def mix(m):
        rows = gbuf.at[slot]
        lo, hi = _unpack_pair(_slab(rows, m * TOP_K)[...])
        wt = wt_ref[0, 0, m]
        acc_lo, acc_hi = wt * lo, wt * hi
        for kk in range(1, TOP_K):
            lo, hi = _unpack_pair(_slab(rows, m * TOP_K + kk)[...])
            wt = wt_ref[0, kk, m]
            acc_lo, acc_hi = acc_lo + wt * lo, acc_hi + wt * hi
        _fslab(rbuf, m)[...] = jnp.concatenate([acc_lo, acc_hi], axis=0)

    @pl.when(step == 0)
    def _():
        def first(m, carry):
            request(e_ref, rk_ref, m, 0)
            return carry
        lax.fori_loop(0, tm, first, 0)

    pltpu.make_async_copy(ob_hbm.at[pl.ds(0, tm * TOP_K * PCH)], gbuf.at[slot], sem.at[slot]).wait()

    @pl.when(step < last)
    def _():
        def both(m, carry):
            request(en_ref, rkn_ref, m, 1 - slot)
            mix(m)
            return carry
        lax.fori_loop(0, tm, both, 0)

    @pl.when(step == last)
    def _():
        def only(m, carry):
            mix(m)
            return carry
        lax.fori_loop(0, tm, only, 0)

    routed = jnp.concatenate([rbuf[pl.ds(c, tm, stride=CHUNKS), :] for c in range(CHUNKS)], axis=1)
    h2 = _rows_from_packed(h2_ref, tm)
    hg = jnp.dot(h2, sg_ref[...], preferred_element_type=F32)
    hu = jnp.dot(h2, su_ref[...], preferred_element_type=F32)
    shared = jnp.dot((hg * _sigmoid(hg) * hu).astype(BF16), sd_ref[...], preferred_element_type=F32)
    ff = (routed + shared).reshape(bb, ll, d)
    x2 = x1_ref[...] + mod_ref[:, 5:6, :] * ff
    out_ref[...] = x2 * lax.rsqrt(jnp.mean(x2 * x2, axis=-1, keepdims=True) + NORM_EPS) * fg_ref[...]


def _combine_call(eidx, rank, wts, first_tok, pad_start, ob, h2s, x1, mod, p):
    bn, seq, d = x1.shape
    tm = CMB_TILE
    bb, ll = _tile(bn, seq, tm)
    nl = seq // ll
    tn = bn * seq
    nsteps = tn // tm
    per = eidx.shape[2] // tm
    assert first_tok % tm == 0 and eidx.shape[2] % tm == 0
    tile = lambda g: ((first_tok // tm + g) // per, 0, (first_tok // tm + g) % per)
    smem, whole = _assign_specs(tm, lambda b, l: tile(b * nl + l))
    smem_next, _ = _assign_specs(tm, lambda b, l: tile(jnp.minimum(b * nl + l + 1, nsteps - 1)))
    tok = pl.BlockSpec((bb, ll, d), lambda b, l: (b, l, 0))
    full = lambda a: pl.BlockSpec(a.shape, lambda b, l: (0,) * a.ndim)
    consts = (p["sh_gate"], p["sh_up"], p["sh_down"], p["final_g"])
    return pl.pallas_call(
        functools.partial(_combine_body, tm=tm, nl=nl),
        grid=(bn // bb, nl),
        in_specs=[smem, smem, smem_next, smem_next, smem, whole, pl.BlockSpec(memory_space=pl.ANY),
                  pl.BlockSpec((tm * PCH, LANES), lambda b, l: (b * nl + l, 0)),
                  tok, pl.BlockSpec((bb, 6, d), lambda b, l: (b, 0, 0))] + [full(c) for c in consts],
        out_specs=tok,
        out_shape=jax.ShapeDtypeStruct((bn, seq, d), F32),
        scratch_shapes=[pltpu.VMEM((2, tm * TOP_K * PCH, LANES), I32), pltpu.VMEM((tm * CHUNKS, LANES), F32),
                        pltpu.SemaphoreType.DMA((2,))],
        compiler_params=_cparams(("arbitrary", "arbitrary"), VMEM_LIMIT),
        name="moe_combine_final",
    )(eidx, rank, eidx, rank, wts, pad_start, ob, h2s, x1, mod, *consts)


def _layer_params(l, ada_w, ada_b, norm1_g, norm2_g, w_in, mu_shift, rw_w0, rw_w_up, rw_a0, rw_a_up, rw_g_up,
                  rw_k_k, rw_k_a, rw_r_k, rw_gn_g, rw_gn_b, gla_a_up, gla_a_bias, gla_norm_g, w_pa, w_pb, w_out,
                  router_w, router_b, exp_gate, exp_up, exp_down, sh_gate, sh_up, sh_down):
    d = D_MODEL
    wi = w_in[l]
    gla0 = RW_SHIFT_COLS
    xal0 = gla0 + QKV_W
    pad = jnp.zeros((d, XAL_W - GLA_GATE_RANK), F32)
    w_pack = jnp.concatenate([wi[:, :xal0], wi[:, xal0:xal0 + GLA_GATE_RANK], pad,
                              wi[:, xal0 + GLA_GATE_RANK:]], axis=1).astype(BF16)
    zr = jnp.zeros((RW_W_RANK, RW_WIDTH), F32)
    hid = jnp.arange(RW_WIDTH) // RW_HEAD
    row = lambda a: a.reshape(1, -1)
    rw_hi = router_w[l].astype(BF16)
    return dict(
        ada_w=ada_w[l], ada_b=ada_b[l], norm1_g=norm1_g[l], norm2_g=norm2_g[l].reshape(1, 1, d), w_pack=w_pack,
        mu=mu_shift[l].reshape(1, 1, -1), w0=row(rw_w0[l]), wup=jnp.concatenate([rw_w_up[l], zr], axis=0),
        a0=row(rw_a0[l]), aup=jnp.concatenate([zr, rw_a_up[l]], axis=0), gup=rw_g_up[l].astype(BF16),
        kk=row(rw_k_k[l]), ka=row(rw_k_a[l]), rk=row(rw_r_k[l]),
        bd64=(hid[:, None] == hid[None, :]).astype(BF16),
        gn_g=row(rw_gn_g[l]), gn_b=row(rw_gn_b[l]),
        gla_aup=jnp.concatenate([gla_a_up[l], jnp.zeros((XAL_W - GLA_GATE_RANK, GLA_KW), F32)], axis=0),
        gla_ab=row(gla_a_bias[l]), gla_ng=row(gla_norm_g[l]),
        w_pa=w_pa[l].astype(BF16), w_pb=w_pb[l].astype(BF16), w_out=w_out[l].astype(BF16),
        rw_hi=rw_hi, rw_lo=(router_w[l] - rw_hi.astype(F32)).astype(BF16), router_b=router_b[l],
        exp_gate=exp_gate[l], exp_up=exp_up[l], exp_down=exp_down[l],
        sh_gate=sh_gate[l].astype(BF16), sh_up=sh_up[l].astype(BF16), sh_down=sh_down[l].astype(BF16),
    )


def _mixer_group(x, mod, s_rw, s_sh, s_gla, p):
    pa, qkv, xal, gg, mg = _inproj_call(x, mod, p["norm1_g"], p["w_pack"])
    r, lw, k2, v, a_s, b_s, g, bonus, new_sh = _rwprep_call(pa, s_sh, p)
    y, rw_new = _rwscan_call(r, lw, k2, v, a_s, b_s, s_rw)
    o_b, gla_new = _gla_call(qkv, xal, gg, s_gla, p)
    x1, h2s, logits = _merge_call(y, g, bonus, o_b, mg, x, mod, p)
    states = (rw_new, new_sh[:, 0, :], gla_new)
    return x1, h2s, logits, states


def _moe(h2s, logits, p):
    tn = h2s.shape[0] // PCH
    eidx, rank, wts, counts = _route_call(logits, p["router_b"])
    counts = counts[0].astype(I32)
    padded = (counts + MOE_BLK - 1) // MOE_BLK * MOE_BLK
    pad_end = jnp.cumsum(padded)
    pad_start = (pad_end - padded).astype(I32)
    nb = (tn * TOP_K + N_EXPERTS * (MOE_BLK - 1)) // MOE_BLK + 1
    tables = _expert_tables(counts, pad_start, pad_end, nb)
    xs = _dispatch_call(eidx, rank, pad_start, h2s, nb * MOE_BLK)
    ob = _expert_call(tables, xs, p["exp_gate"], p["exp_up"], p["exp_down"])
    return ob, (eidx, rank, wts, pad_start)


def kernel(x_prompt, x_sample, c_prompt, c_sample, state_rwkv, state_shift, state_gla, ada_w, ada_b, norm1_g,
           norm2_g, w_in, mu_shift, rw_w0, rw_w_up, rw_a0, rw_a_up, rw_g_up, rw_k_k, rw_k_a, rw_r_k, rw_gn_g,
           rw_gn_b, gla_a_up, gla_a_bias, gla_norm_g, w_pa, w_pb, w_out, router_w, router_b, exp_gate, exp_up,
           exp_down, sh_gate, sh_up, sh_down, final_g):
    depth = ada_w.shape[0]
    bp, bs = x_prompt.shape[0], x_sample.shape[0]
    tp = bp * x_prompt.shape[1]
    xs_g = [x_prompt, x_sample]
    c_all = jnp.concatenate([c_prompt, c_sample], axis=0)
    zeros = lambda shape: jnp.zeros(shape, x_prompt.dtype)
    new_states = [[], []]
    fg = final_g.reshape(1, 1, D_MODEL)
    for l in range(depth):
        p = _layer_params(l, ada_w, ada_b, norm1_g, norm2_g, w_in, mu_shift, rw_w0, rw_w_up, rw_a0, rw_a_up,
                          rw_g_up, rw_k_k, rw_k_a, rw_r_k, rw_gn_g, rw_gn_b, gla_a_up, gla_a_bias, gla_norm_g,
                          w_pa, w_pb, w_out, router_w, router_b, exp_gate, exp_up, exp_down, sh_gate, sh_up,
                          sh_down)
        p["final_g"] = fg
        mod_all = _mod_call(c_all, p["ada_w"], p["ada_b"])
        mods = [mod_all[:bp], mod_all[bp:]]
        states_in = [
            (zeros((bp, RW_HEADS, RW_HEAD, RW_HEAD)), zeros((bp, RW_SHIFT_COLS)),
             zeros((bp, GLA_HEADS, GLA_DK, GLA_DV))),
            (state_rwkv[l], state_shift[l], state_gla[l]),
        ]
        x1s, h2ss, lgs = [], [], []
        for gi in range(2):
            x1, h2s, logits, st = _mixer_group(xs_g[gi], mods[gi], *states_in[gi], p)
            x1s.append(x1)
            h2ss.append(h2s)
            lgs.append(logits)
            new_states[gi].append(st)
        h2_all = jnp.concatenate(h2ss, axis=0)
        ob, (eidx, rank, wts, pad_start) = _moe(h2_all, jnp.concatenate(lgs, axis=0), p)
        assert depth == 1, "the fused final norm assumes a single layer"
        xs_g = [
            _combine_call(eidx, rank, wts, 0, pad_start, ob, h2ss[0], x1s[0], mods[0], p),
            _combine_call(eidx, rank, wts, tp, pad_start, ob, h2ss[1], x1s[1], mods[1], p),
        ]
    stack = lambda gi, j: jnp.stack([s[j] for s in new_states[gi]])
    return (xs_g[0], xs_g[1], stack(0, 0), stack(0, 1), stack(0, 2), stack(1, 0), stack(1, 1), stack(1, 2))
```

```python
import functools

import jax
import jax.numpy as jnp
from jax import lax
from jax.experimental import pallas as pl
from jax.experimental.pallas import tpu as pltpu

F32, BF16, I32 = jnp.float32, jnp.bfloat16, jnp.int32

D_MODEL = 1024
RW_HEADS, RW_HEAD = 8, 64
RW_WIDTH = RW_HEADS * RW_HEAD
RW_W_RANK, RW_A_RANK, RW_G_RANK = 64, 64, 128
RW_GN_EPS = 64e-5
GLA_HEADS, GLA_DK, GLA_DV = 4, 64, 128
GLA_KW, GLA_VW = GLA_HEADS * GLA_DK, GLA_HEADS * GLA_DV
GLA_GATE_RANK = 16
GLA_GATE_TAU = 16.0
GLA_CHUNK = 16
RW_SHIFT_COLS = 3 * RW_WIDTH + RW_W_RANK + RW_A_RANK + RW_G_RANK
N_EXPERTS, TOP_K, N_GROUPS, TOPK_GROUPS = 256, 8, 8, 4
GROUP_SIZE = N_EXPERTS // N_GROUPS
EXPERT_FF = 256
ROUTED_SCALE = 2.5
NORM_EPS = 1e-6

LANES = 128
SUBLANES = 8
CHUNKS = D_MODEL // LANES
PCH = CHUNKS // 2
UNIT = 64
RW_SCAN_PASSES = (1, 1, 1, 1, 1)
GLA_UNITS_PER_STEP = 4
RW_UNITS_PER_STEP = 4
VMEM_LIMIT = 56 * 1024 * 1024

PA_W, QKV_W, XAL_W, GG_W, MG_W = RW_SHIFT_COLS, 2 * GLA_KW + GLA_VW, LANES, GLA_VW, 2 * D_MODEL
PACK_OFFS = (0, PA_W, PA_W + QKV_W, PA_W + QKV_W + XAL_W, PA_W + QKV_W + XAL_W + GG_W)
PACK_W = PA_W + QKV_W + XAL_W + GG_W + MG_W

TOK_TILE = 256
MOE_BLK = 256
EXPERT_PARTS = 1
WEIGHT_SLOTS = 3
CMB_TILE = 128

_DN = {
    "nn": (((1,), (0,)), ((), ())),
    "nt": (((1,), (1,)), ((), ())),
    "tn": (((0,), (0,)), ((), ())),
}


def _split(x, pieces):
    out, rem = [], x
    for i in range(pieces):
        p = rem.astype(BF16)
        out.append(p)
        if i + 1 < pieces:
            rem = rem - p.astype(F32)
    return out


def _mm(a, b, form="nn", passes=1):
    dn = _DN[form]
    if passes == 6:
        return lax.dot_general(a.astype(F32), b.astype(F32), dn, precision=lax.Precision.HIGHEST,
                               preferred_element_type=F32)
    if passes == 1:
        return lax.dot_general(a.astype(BF16), b.astype(BF16), dn, preferred_element_type=F32)
    ah, al = _split(a, 2)
    bh, bl = _split(b, 2)
    out = lax.dot_general(ah, bh, dn, preferred_element_type=F32)
    out = out + lax.dot_general(ah, bl, dn, preferred_element_type=F32)
    return out + lax.dot_general(al, bh, dn, preferred_element_type=F32)


def _mm01(m01, x, pieces=3):
    m = m01.astype(BF16)
    out = None
    for p in _split(x, pieces):
        t = lax.dot_general(m, p, _DN["nn"], preferred_element_type=F32)
        out = t if out is None else out + t
    return out


def _xmm01(x, m01, pieces=2):
    m = m01.astype(BF16)
    out = None
    for p in _split(x, pieces):
        t = lax.dot_general(p, m, _DN["nn"], preferred_element_type=F32)
        out = t if out is None else out + t
    return out


HI16 = -65536


def _bf16_bits(x):
    return lax.bitcast_convert_type(x.astype(BF16).astype(F32), I32)


def _unpack_pair(w):
    return lax.bitcast_convert_type(w << 16, F32), lax.bitcast_convert_type(w & HI16, F32)


def _rows_to_packed(ref, x, first=0):
    for c in range(PCH):
        lo = _bf16_bits(x[:, c * LANES:(c + 1) * LANES])
        hi = _bf16_bits(x[:, (c + PCH) * LANES:(c + PCH + 1) * LANES])
        ref[pl.ds(first * PCH + c, x.shape[0], stride=PCH), :] = ((lo >> 16) & 0xFFFF) | (hi & HI16)


def _rows_from_packed(ref, n, live=None, first=0):
    lows, highs = [], []
    for c in range(PCH):
        w = ref[pl.ds(first * PCH + c, n, stride=PCH), :]
        if live is not None:
            w = jnp.where(live, w, 0)
        lo, hi = _unpack_pair(w)
        lows.append(lo.astype(BF16))
        highs.append(hi.astype(BF16))
    return jnp.concatenate(lows + highs, axis=1)


def _slab(ref, row):
    return ref.at[pl.ds(pl.multiple_of(row * PCH, PCH), PCH)]


def _fslab(ref, row):
    return ref.at[pl.ds(pl.multiple_of(row * CHUNKS, CHUNKS), CHUNKS)]


def _sigmoid(x):
    return 1.0 / (1.0 + jnp.exp(-x))


def _softplus(x):
    return jnp.maximum(x, 0.0) + jnp.log(1.0 + jnp.exp(-jnp.abs(x)))


def _log2(n):
    assert n > 0 and n & (n - 1) == 0, n
    return n.bit_length() - 1


def _cparams(sem, vmem=None):
    return pltpu.CompilerParams(dimension_semantics=sem, vmem_limit_bytes=vmem)


def _mod_body(c_ref, w_ref, b_ref, o_ref):
    c = c_ref[...]
    o_ref[0] = _mm(c * _sigmoid(c), w_ref[...], passes=6) + b_ref[...]


def _mod_call(c_all, ada_w, ada_b):
    bt, d = c_all.shape
    out = pl.pallas_call(
        _mod_body,
        grid=(6,),
        in_specs=[pl.BlockSpec((bt, d), lambda k: (0, 0)),
                  pl.BlockSpec((d, d), lambda k: (0, k)),
                  pl.BlockSpec((1, d), lambda k: (0, k))],
        out_specs=pl.BlockSpec((1, bt, d), lambda k: (k, 0, 0)),
        out_shape=jax.ShapeDtypeStruct((6, bt, d), F32),
        compiler_params=_cparams(("arbitrary",)),
        name="adaln_mod",
    )(c_all, ada_w, ada_b.reshape(1, 6 * d))
    return jnp.transpose(out, (1, 0, 2))


def _inproj_body(x_ref, mod_ref, g_ref, w_ref, pa_ref, qkv_ref, xal_ref, gg_ref, mg_ref):
    bb, ll, d = x_ref.shape
    x = x_ref[...]
    y = x * lax.rsqrt(jnp.mean(x * x, axis=-1, keepdims=True) + NORM_EPS) * g_ref[...]
    h = y * (1.0 + mod_ref[:, 1:2, :]) + mod_ref[:, 0:1, :]
    hb = h.reshape(bb * ll, d).astype(BF16)
    for ref, off in zip((pa_ref, qkv_ref, xal_ref, gg_ref, mg_ref), PACK_OFFS):
        w = ref.shape[-1]
        ref[...] = jnp.dot(hb, w_ref[:, off:off + w], preferred_element_type=F32).reshape(bb, ll, w)


def _tile(bn, seq, tile):
    if seq >= tile:
        assert seq % tile == 0
        return 1, tile
    assert tile % seq == 0 and bn % (tile // seq) == 0
    return tile // seq, seq


def _inproj_call(x, mod, norm_g, w_pack):
    bn, seq, d = x.shape
    bb, ll = _tile(bn, seq, TOK_TILE)
    tok = lambda w: pl.BlockSpec((bb, ll, w), lambda b, l: (b, l, 0))
    widths = (PA_W, QKV_W, XAL_W, GG_W, MG_W)
    return pl.pallas_call(
        _inproj_body,
        grid=(bn // bb, seq // ll),
        in_specs=[tok(d),
                  pl.BlockSpec((bb, 6, d), lambda b, l: (b, 0, 0)),
                  pl.BlockSpec((1, 1, d), lambda b, l: (0, 0, 0)),
                  pl.BlockSpec((d, PACK_W), lambda b, l: (0, 0))],
        out_specs=[tok(w) for w in widths],
        out_shape=[jax.ShapeDtypeStruct((bn, seq, w), F32) for w in widths],
        compiler_params=_cparams(("arbitrary", "arbitrary"), VMEM_LIMIT),
        name="norm_inproj",
    )(x, mod, norm_g.reshape(1, 1, d), w_pack)


def _rwprep_body(pa_ref, sh_ref, mu_ref, w0_ref, wup_ref, a0_ref, aup_ref, gup_ref, kk_ref, ka_ref, rk_ref,
                 bd_ref, r_o, lw_o, k_o, v_o, a_o, b_o, g_o, bon_o, nsh_o, carry):
    bb, ll, wd = pa_ref.shape
    n = bb * ll
    hw = RW_WIDTH

    @pl.when(pl.program_id(1) == 0)
    def _():
        carry[...] = sh_ref[...]

    pa = pa_ref[...]
    rolled = pltpu.roll(pa.reshape(n, wd), 1, 0).reshape(bb, ll, wd)
    tok = lax.broadcasted_iota(I32, (bb, ll, wd), 1)
    prev = jnp.where(tok == 0, carry[...], rolled)
    last = pa_ref[:, ll - 1:ll, :]
    carry[...] = last
    nsh_o[...] = last
    xs = (pa + (prev - pa) * mu_ref[...]).reshape(n, wd)

    r, k, v = xs[:, 0:hw], xs[:, hw:2 * hw], xs[:, 2 * hw:3 * hw]
    xwa = xs[:, 3 * hw:3 * hw + LANES]
    xg = xs[:, 3 * hw + LANES:]
    w_log = -_softplus(-(w0_ref[...] + _mm(jnp.tanh(xwa), wup_ref[...], passes=3))) - 0.5
    lw = -jnp.exp(w_log)
    a = _sigmoid(a0_ref[...] + _mm(xwa, aup_ref[...], passes=3))
    g = _mm(_sigmoid(xg), gup_ref[...])
    bd = bd_ref[...]
    kkv = k * kk_ref[...]
    kkn = kkv * lax.rsqrt(jnp.maximum(_xmm01(kkv * kkv, bd), 1e-24))
    k2 = k * (1.0 + (a - 1.0) * ka_ref[...])
    bonus = _xmm01(r * k2 * rk_ref[...], bd) * v
    for ref, val in ((r_o, r), (lw_o, lw), (k_o, k2), (v_o, v), (a_o, -kkn), (b_o, kkn * a), (g_o, g),
                     (bon_o, bonus)):
        ref[...] = val.reshape(bb, ll, hw)


def _rwprep_call(pa, s_sh, p):
    bn, seq, wd = pa.shape
    bb, ll = _tile(bn, seq, TOK_TILE)
    hw = RW_WIDTH
    tok = lambda w: pl.BlockSpec((bb, ll, w), lambda b, l: (b, l, 0))
    row = lambda w: pl.BlockSpec((bb, 1, w), lambda b, l: (b, 0, 0))
    full = lambda a: pl.BlockSpec(a.shape, lambda b, l: (0,) * a.ndim)
    consts = (p["mu"], p["w0"], p["wup"], p["a0"], p["aup"], p["gup"], p["kk"], p["ka"], p["rk"], p["bd64"])
    outs = pl.pallas_call(
        _rwprep_body,
        grid=(bn // bb, seq // ll),
        in_specs=[tok(wd), row(wd)] + [full(c) for c in consts],
        out_specs=[tok(hw)] * 8 + [row(wd)],
        out_shape=[jax.ShapeDtypeStruct((bn, seq, hw), F32)] * 8 + [jax.ShapeDtypeStruct((bn, 1, wd), F32)],
        scratch_shapes=[pltpu.VMEM((bb, 1, wd), F32)],
        compiler_params=_cparams(("arbitrary", "arbitrary"), VMEM_LIMIT),
        name="rwkv_prep",
    )(pa, s_sh.reshape(bn, 1, wd), *consts)
    return outs


def _unit_masks(n, tl):
    ri = lax.broadcasted_iota(I32, (n, n), 0)
    ci = lax.broadcasted_iota(I32, (n, n), 1)
    same = (ri >> _log2(tl)) == (ci >> _log2(tl))
    return same, same & (ri > ci), same & (ri >= ci)


def _rwscan_body(r_ref, lw_ref, k_ref, v_ref, a_ref, b_ref, s0_ref, y_ref, sn_ref, st, *, nu, nseq, tl, passes):
    n = nseq * tl
    n2 = 2 * n
    p_aa, p_inv, p_apply, p_state, p_y = passes

    hd = RW_HEAD

    @pl.when(pl.program_id(1) == 0)
    def _():
        zero = jnp.zeros((hd, hd), F32)
        for q in range(nu * nseq):
            for p in range(RW_HEADS // 2):
                st[q, p] = jnp.concatenate(
                    [jnp.concatenate([s0_ref[q, 2 * p], zero], axis=1),
                     jnp.concatenate([zero, s0_ref[q, 2 * p + 1]], axis=1)], axis=0)

    same, _, incl = _unit_masks(n, tl)
    m_cum = jnp.where(incl, 1.0, 0.0)
    m_seq = jnp.where(same, 1.0, 0.0)
    ri = lax.broadcasted_iota(I32, (n2, n2), 0)
    ci = lax.broadcasted_iota(I32, (n2, n2), 1)
    rt, ct = ri & (n - 1), ci & (n - 1)
    dsame = ((rt >> _log2(tl)) == (ct >> _log2(tl))) & ((ri >> _log2(n)) == (ci >> _log2(n)))
    strict_d = dsame & (rt > ct)
    incl_d = dsame & (rt >= ct)
    eye_d = jnp.where(ri == ci, 1.0, 0.0)
    lane = lax.broadcasted_iota(I32, (1, LANES), 1)
    m0 = jnp.where(lane < RW_HEAD, 1.0, 0.0)
    m1 = 1.0 - m0

    def dup(x):
        return jnp.concatenate([x * m0, x * m1], axis=0)

    def seq_rows(x, q):
        if nseq == 1:
            return x
        return jnp.concatenate([x[q * tl:(q + 1) * tl], x[n + q * tl:n + (q + 1) * tl]], axis=0)

    def unit_rows(parts):
        if nseq == 1:
            return parts[0]
        return jnp.concatenate([p[0:tl] for p in parts] + [p[tl:2 * tl] for p in parts], axis=0)

    chains = [(u, p) for u in range(nu) for p in range(RW_HEADS // 2)]
    ids = range(len(chains))
    cat0 = lambda *xs: jnp.concatenate(xs, axis=0)

    def ld(ref, c):
        u, p = chains[c]
        return ref[u * nseq:(u + 1) * nseq, :, p * LANES:(p + 1) * LANES].reshape(n, LANES)

    lw = [ld(lw_ref, c) for c in ids]
    cum = [_mm01(m_cum, x) for x in lw]
    tot = [_mm01(m_seq, x) for x in lw]
    e_c = [jnp.exp(x) for x in cum]
    e_n = [jnp.exp(-x) for x in cum]
    e_l = [jnp.exp(t - x) for t, x in zip(tot, cum)]
    at_d = [dup(ld(a_ref, c) * jnp.exp(cum[c] - lw[c])) for c in ids]
    rt_d = [dup(ld(r_ref, c) * e_c[c]) for c in ids]
    bt_d = [dup(ld(b_ref, c) * e_n[c]) for c in ids]
    kt_d = [dup(ld(k_ref, c) * e_n[c]) for c in ids]
    bh_d = [dup(ld(b_ref, c) * e_l[c]) for c in ids]
    kh_d = [dup(ld(k_ref, c) * e_l[c]) for c in ids]
    v_d = [dup(ld(v_ref, c)) for c in ids]
    aa = [_mm(cat0(at_d[c], rt_d[c]), cat0(bt_d[c], kt_d[c]), "nt", p_aa) for c in ids]
    a_ab = [jnp.where(strict_d, x[0:n2, 0:n2], 0.0) for x in aa]
    a_ak = [jnp.where(strict_d, x[0:n2, n2:], 0.0) for x in aa]
    a_rb = [jnp.where(incl_d, x[n2:, 0:n2], 0.0) for x in aa]
    a_rk = [jnp.where(incl_d, x[n2:, n2:], 0.0) for x in aa]
    zy = [_mm(cat0(a_ak[c], a_rk[c]), v_d[c], passes=p_apply) for c in ids]
    tinv = [eye_d + x for x in a_ab]
    nk = a_ab
    for _ in range(_log2(tl) - 1):
        nk = [_mm(x, x, passes=p_inv) for x in nk]
        tinv = [t + _mm(t, x, passes=p_inv) for t, x in zip(tinv, nk)]
    wu = [_mm(tinv[c], jnp.concatenate([at_d[c], zy[c][0:n2]], axis=1), passes=p_apply) for c in ids]
    seqs = range(nseq)
    srow = lambda c, q: (chains[c][0] * nseq + q, chains[c][1])
    s_old = [[st[srow(c, q)] for q in seqs] for c in ids]
    xs = [[_mm(cat0(seq_rows(wu[c][:, 0:LANES], q), seq_rows(rt_d[c], q)), s_old[c][q], "nt", p_state)
           for q in seqs] for c in ids]
    u_q = [[xs[c][q][0:2 * tl] + seq_rows(wu[c][:, LANES:], q) for q in seqs] for c in ids]
    for c in ids:
        for q in seqs:
            g_c = jnp.exp(tot[c][q * tl:q * tl + 1, :])
            st[srow(c, q)] = s_old[c][q] * g_c + _mm(cat0(u_q[c][q], seq_rows(v_d[c], q)),
                                                     cat0(seq_rows(bh_d[c], q), seq_rows(kh_d[c], q)), "tn", p_state)
    for c in ids:
        u, p = chains[c]
        y_d = (unit_rows([xs[c][q][2 * tl:] for q in seqs]) + _mm(a_rb[c], unit_rows(u_q[c]), passes=p_y)
               + zy[c][n2:])
        y_ref[u * nseq:(u + 1) * nseq, :, p * LANES:(p + 1) * LANES] = (y_d[0:n] + y_d[n:]).reshape(nseq, tl, LANES)

    @pl.when(pl.program_id(1) == pl.num_programs(1) - 1)
    def _():
        for q in range(nu * nseq):
            for p in range(RW_HEADS // 2):
                s = st[q, p]
                sn_ref[q, 2 * p] = s[0:hd, 0:hd]
                sn_ref[q, 2 * p + 1] = s[hd:, hd:]


def _unit_shape(bn, seq):
    if seq >= UNIT:
        assert seq % UNIT == 0
        return 1, UNIT
    assert UNIT % seq == 0 and bn % (UNIT // seq) == 0
    return UNIT // seq, seq


def _rwscan_call(r, lw, k2, v, a_s, b_s, s0, passes=RW_SCAN_PASSES):
    bn, seq, hw = r.shape
    nseq, tl = _unit_shape(bn, seq)
    nu = RW_UNITS_PER_STEP if bn % (RW_UNITS_PER_STEP * nseq) == 0 else 1
    rows = nu * nseq
    tok = pl.BlockSpec((rows, tl, hw), lambda b, c: (b, c, 0))
    stt = pl.BlockSpec((rows, RW_HEADS, RW_HEAD, RW_HEAD), lambda b, c: (b, 0, 0, 0))
    return pl.pallas_call(
        functools.partial(_rwscan_body, nu=nu, nseq=nseq, tl=tl, passes=passes),
        grid=(bn // rows, seq // tl),
        in_specs=[tok] * 6 + [stt],
        out_specs=[tok, stt],
        out_shape=[jax.ShapeDtypeStruct((bn, seq, hw), F32), jax.ShapeDtypeStruct(s0.shape, F32)],
        scratch_shapes=[pltpu.VMEM((rows, RW_HEADS // 2, LANES, LANES), F32)],
        compiler_params=_cparams(("arbitrary", "arbitrary"), VMEM_LIMIT),
        name="rwkv_scan",
    )(r, lw, k2, v, a_s, b_s, s0)


def _gla_body(qkv_ref, xal_ref, gate_ref, aup_ref, ab_ref, ng_ref, s0_ref, o_ref, sn_ref, st, *, nu, nseq, tl, cs):
    n = nseq * tl
    n2 = 2 * n
    nsub = tl // cs

    @pl.when(pl.program_id(1) == 0)
    def _():
        zero = jnp.zeros((GLA_DV, GLA_DK), F32)
        for q in range(nu * nseq):
            for p in range(GLA_HEADS // 2):
                st[q, p] = jnp.concatenate(
                    [jnp.concatenate([s0_ref[q, 2 * p].T, zero], axis=1),
                     jnp.concatenate([zero, s0_ref[q, 2 * p + 1].T], axis=1)], axis=0)

    same, _, incl = _unit_masks(n, cs)
    m_cum = jnp.where(incl, 1.0, 0.0)
    m_sub = jnp.where(same, 1.0, 0.0)
    ri = lax.broadcasted_iota(I32, (n2, n2), 0)
    ci = lax.broadcasted_iota(I32, (n2, n2), 1)
    rt, ct = ri & (n - 1), ci & (n - 1)
    causal_d = ((rt >> _log2(cs)) == (ct >> _log2(cs))) & ((ri >> _log2(n)) == (ci >> _log2(n))) & (rt >= ct)
    lane = lax.broadcasted_iota(I32, (1, LANES), 1)
    m0 = jnp.where(lane < GLA_DK, 1.0, 0.0)
    m1 = 1.0 - m0
    sr = lax.broadcasted_iota(I32, (2 * GLA_DV, LANES), 0)
    sc = lax.broadcasted_iota(I32, (2 * GLA_DV, LANES), 1)
    st_mask = jnp.where((sr >> _log2(GLA_DV)) == (sc >> _log2(GLA_DK)), 1.0, 0.0)

    def dup(x):
        return jnp.concatenate([x * m0, x * m1], axis=0)

    chains = [(u, p) for u in range(nu) for p in range(GLA_HEADS // 2)]
    ids = range(len(chains))
    urows = lambda u: slice(u * nseq, (u + 1) * nseq)
    ng = ng_ref[...]
    la_all = [-_softplus(-(_mm(xal_ref[urows(u), :, :].reshape(n, LANES), aup_ref[...], passes=3) + ab_ref[...]))
              * (1.0 / GLA_GATE_TAU) for u in range(nu)]

    def ld(ref, c, off, width):
        return ref[urows(chains[c][0]), :, off:off + width].reshape(n, width)

    q = [ld(qkv_ref, c, chains[c][1] * LANES, LANES) * (GLA_DK ** -0.5) for c in ids]
    k = [ld(qkv_ref, c, GLA_KW + chains[c][1] * LANES, LANES) for c in ids]
    vp = [ld(qkv_ref, c, 2 * GLA_KW + chains[c][1] * 2 * GLA_DV, 2 * GLA_DV) for c in ids]
    la = [la_all[u][:, p * LANES:(p + 1) * LANES] for u, p in chains]
    bc = [_mm01(m_cum, x) for x in la]
    bl = [_mm01(m_sub, x) for x in la]
    qe = [q[c] * jnp.exp(bc[c]) for c in ids]
    ke = [k[c] * jnp.exp(-bc[c]) for c in ids]
    kd = [k[c] * jnp.exp(bl[c] - bc[c]) for c in ids]
    att = [jnp.where(causal_d, _mm(dup(qe[c]), dup(ke[c]), "nt", passes=1), 0.0) for c in ids]
    v_st = [jnp.concatenate([x[:, 0:GLA_DV], x[:, GLA_DV:]], axis=0) for x in vp]
    o_st = [_mm(att[c], v_st[c], passes=1) for c in ids]
    upd = [[_mm(vp[c][r0:r0 + cs], kd[c][r0:r0 + cs], "tn", passes=1) for r0 in range(0, n, cs)] for c in ids]
    inter = [[None] * (n // cs) for _ in ids]
    for sq in range(nseq):
        s = [st[chains[c][0] * nseq + sq, chains[c][1]] for c in ids]
        for j in range(nsub):
            i = sq * nsub + j
            r0 = i * cs
            for c in ids:
                inter[c][i] = _mm(qe[c][r0:r0 + cs], s[c], "nt", passes=1)
                s[c] = s[c] * jnp.exp(bl[c][r0:r0 + 1, :]) + st_mask * upd[c][i]
        for c in ids:
            st[chains[c][0] * nseq + sq, chains[c][1]] = s[c]
    for c in ids:
        u, p = chains[c]
        o = o_st[c] + jnp.concatenate([x[:, 0:GLA_DV] for x in inter[c]] + [x[:, GLA_DV:] for x in inter[c]], axis=0)
        o = o * lax.rsqrt(jnp.mean(o * o, axis=-1, keepdims=True) + NORM_EPS) * ng
        goff = p * 2 * GLA_DV
        gp = ld(gate_ref, c, goff, 2 * GLA_DV)
        g_st = jnp.concatenate([gp[:, 0:GLA_DV], gp[:, GLA_DV:]], axis=0)
        ob = o * (g_st * _sigmoid(g_st))
        o_ref[urows(u), :, goff:goff + GLA_DV] = ob[0:n].reshape(nseq, tl, GLA_DV)
        o_ref[urows(u), :, goff + GLA_DV:goff + 2 * GLA_DV] = ob[n:].reshape(nseq, tl, GLA_DV)

    @pl.when(pl.program_id(1) == pl.num_programs(1) - 1)
    def _():
        for q in range(nu * nseq):
            for p in range(GLA_HEADS // 2):
                s = st[q, p]
                sn_ref[q, 2 * p] = s[0:GLA_DV, 0:GLA_DK].T
                sn_ref[q, 2 * p + 1] = s[GLA_DV:, GLA_DK:].T


def _gla_call(qkv, xal, gate, s0, p):
    bn, seq, _ = qkv.shape
    nseq, tl = _unit_shape(bn, seq)
    cs = min(GLA_CHUNK, seq)
    assert tl % cs == 0
    nu = GLA_UNITS_PER_STEP if bn % (GLA_UNITS_PER_STEP * nseq) == 0 else 1
    rows = nu * nseq
    tok = lambda w: pl.BlockSpec((rows, tl, w), lambda b, c: (b, c, 0))
    full = lambda a: pl.BlockSpec(a.shape, lambda b, c: (0,) * a.ndim)
    stt = pl.BlockSpec((rows, GLA_HEADS, GLA_DK, GLA_DV), lambda b, c: (b, 0, 0, 0))
    consts = (p["gla_aup"], p["gla_ab"], p["gla_ng"])
    return pl.pallas_call(
        functools.partial(_gla_body, nu=nu, nseq=nseq, tl=tl, cs=cs),
        grid=(bn // rows, seq // tl),
        in_specs=[tok(QKV_W), tok(XAL_W), tok(GG_W)] + [full(c) for c in consts] + [stt],
        out_specs=[tok(GLA_VW), stt],
        out_shape=[jax.ShapeDtypeStruct((bn, seq, GLA_VW), F32), jax.ShapeDtypeStruct(s0.shape, F32)],
        scratch_shapes=[pltpu.VMEM((rows, GLA_HEADS // 2, 2 * GLA_DV, LANES), F32)],
        compiler_params=_cparams(("arbitrary", "arbitrary"), VMEM_LIMIT),
        name="gla_chunked",
    )(qkv, xal, gate, *consts, s0)


def _merge_body(y_ref, g_ref, bon_ref, ob_ref, mg_ref, x_ref, mod_ref, gng_ref, gnb_ref, bd_ref, wpa_ref,
                wpb_ref, wout_ref, n2_ref, rwh_ref, rwl_ref, x1_o, h2_o, lg_o):
    bb, ll, d = x_ref.shape
    n = bb * ll
    hw = RW_WIDTH
    bd = bd_ref[...]
    y = y_ref[...].reshape(n, hw)
    mu = _xmm01(y, bd, pieces=3) * (1.0 / RW_HEAD)
    dv = y - mu
    var = _xmm01(dv * dv, bd) * (1.0 / RW_HEAD)
    yn = dv * lax.rsqrt(var + RW_GN_EPS) * gng_ref[...] + gnb_ref[...]
    o_a = (yn + bon_ref[...].reshape(n, hw)) * g_ref[...].reshape(n, hw)
    o_b = ob_ref[...].reshape(n, GLA_VW)
    mg = mg_ref[...].reshape(n, 2 * d)
    merged = _sigmoid(mg[:, 0:d]) * _mm(o_a, wpa_ref[...]) + _sigmoid(mg[:, d:]) * _mm(o_b, wpb_ref[...])
    mix = _mm(merged, wout_ref[...]).reshape(bb, ll, d)
    x1 = x_ref[...] + mod_ref[:, 2:3, :] * mix
    x1_o[...] = x1
    yn2 = x1 * lax.rsqrt(jnp.mean(x1 * x1, axis=-1, keepdims=True) + NORM_EPS) * n2_ref[...]
    h2 = (yn2 * (1.0 + mod_ref[:, 4:5, :]) + mod_ref[:, 3:4, :]).reshape(n, d)
    hh, hl = _split(h2, 2)
    rwh, rwl = rwh_ref[...], rwl_ref[...]
    nt = lambda a, b: lax.dot_general(a, b, _DN["nt"], preferred_element_type=F32)
    lg_o[...] = nt(rwh, hh) + nt(rwl, hh) + nt(rwh, hl)
    _rows_to_packed(h2_o, h2)


def _merge_call(y, g, bonus, o_b, mg, x, mod, p):
    bn, seq, d = x.shape
    bb, ll = _tile(bn, seq, TOK_TILE)
    nl = seq // ll
    tn = bn * seq
    tok = lambda w: pl.BlockSpec((bb, ll, w), lambda b, l: (b, l, 0))
    full = lambda a: pl.BlockSpec(a.shape, lambda b, l: (0,) * a.ndim)
    consts = (p["gn_g"], p["gn_b"], p["bd64"], p["w_pa"], p["w_pb"], p["w_out"], p["norm2_g"], p["rw_hi"],
              p["rw_lo"])
    return pl.pallas_call(
        _merge_body,
        grid=(bn // bb, nl),
        in_specs=[tok(RW_WIDTH)] * 3 + [tok(GLA_VW), tok(MG_W), tok(d),
                                        pl.BlockSpec((bb, 6, d), lambda b, l: (b, 0, 0))] + [full(c) for c in consts],
        out_specs=[tok(d),
                   pl.BlockSpec((bb * ll * PCH, LANES), lambda b, l: (b * nl + l, 0)),
                   pl.BlockSpec((N_EXPERTS, bb * ll), lambda b, l: (0, b * nl + l))],
        out_shape=[jax.ShapeDtypeStruct((bn, seq, d), F32),
                   jax.ShapeDtypeStruct((tn * PCH, LANES), I32),
                   jax.ShapeDtypeStruct((N_EXPERTS, tn), F32)],
        compiler_params=_cparams(("arbitrary", "arbitrary"), VMEM_LIMIT),
        name="merge_outproj_router",
    )(y, g, bonus, o_b, mg, x, mod, *consts)


def _route_body(lg_ref, rb_ref, e_o, rk_o, w_o, cnt_o, carry):
    ne, tm = lg_ref.shape

    @pl.when(pl.program_id(0) == 0)
    def _():
        carry[...] = jnp.zeros_like(carry)

    neg = -jnp.inf
    scores = _sigmoid(lg_ref[...])
    sel = scores + rb_ref[...]
    row_i = lax.broadcasted_iota(I32, (ne, tm), 0)
    row = row_i.astype(F32)
    grp = (row_i >> _log2(GROUP_SIZE)).astype(F32)

    def first_max(x, ids, none):
        m = jnp.max(x, axis=0, keepdims=True)
        return m, jnp.min(jnp.where(x == m, ids, none), axis=0, keepdims=True)

    gs = []
    gids = lax.broadcasted_iota(I32, (GROUP_SIZE, tm), 0)
    for gidx in range(N_GROUPS):
        rows = slice(gidx * GROUP_SIZE, (gidx + 1) * GROUP_SIZE)
        sg = _sigmoid(lg_ref[rows, :]) + rb_ref[rows, :]
        ids = (gids + gidx * GROUP_SIZE).astype(F32)
        m1, i1 = first_max(sg, ids, float(ne))
        gs.append(m1 + jnp.max(jnp.where(ids == i1, neg, sg), axis=0, keepdims=True))
    gs = jnp.concatenate(gs, axis=0)
    gid = lax.broadcasted_iota(I32, (N_GROUPS, tm), 0).astype(F32)
    cur = jnp.full((ne, tm), neg, F32)
    for _ in range(TOPK_GROUPS):
        _, gi = first_max(gs, gid, float(N_GROUPS))
        cur = jnp.where(grp == gi, sel, cur)
        gs = jnp.where(gid == gi, neg, gs)

    pm = jnp.zeros((ne, tm), F32)
    eidx, wts = [], []
    for _ in range(TOP_K):
        _, ei = first_max(cur, row, float(ne))
        hit = row == ei
        pm = jnp.where(hit, 1.0, pm)
        eidx.append(ei)
        wts.append(jnp.sum(jnp.where(hit, scores, 0.0), axis=0, keepdims=True))
        cur = jnp.where(hit, neg, cur)
    wsum = wts[0]
    for w in wts[1:]:
        wsum = wsum + w

    ri = lax.broadcasted_iota(I32, (tm, tm), 0)
    ci = lax.broadcasted_iota(I32, (tm, tm), 1)
    earlier = jnp.where(ri < ci, 1.0, 0.0)
    rank = _mm(pm, earlier, passes=1) + carry[...]
    carry[...] = carry[...] + jnp.sum(pm, axis=1, keepdims=True)
    cnt_o[...] = carry[...]

    rks = [jnp.sum(jnp.where(row == e, rank, 0.0), axis=0, keepdims=True) for e in eidx]
    e_o[0] = jnp.concatenate(eidx, axis=0).astype(I32)
    rk_o[0] = jnp.concatenate(rks, axis=0).astype(I32)
    w_o[0] = jnp.concatenate([w / wsum * ROUTED_SCALE for w in wts], axis=0)


def _route_call(logits_t, router_b):
    ne, tn = logits_t.shape
    tm = TOK_TILE
    assert tn % tm == 0
    col = pl.BlockSpec((ne, 1), lambda i: (0, 0))
    tab = pl.BlockSpec((1, TOP_K, tm), lambda i: (i, 0, 0))
    tab_shape = (tn // tm, TOP_K, tm)
    return pl.pallas_call(
        _route_body,
        grid=(tn // tm,),
        in_specs=[pl.BlockSpec((ne, tm), lambda i: (0, i)), col],
        out_specs=[tab, tab, tab, col],
        out_shape=[jax.ShapeDtypeStruct(tab_shape, I32), jax.ShapeDtypeStruct(tab_shape, I32),
                   jax.ShapeDtypeStruct(tab_shape, F32), jax.ShapeDtypeStruct((ne, 1), F32)],
        scratch_shapes=[pltpu.VMEM((ne, 1), F32)],
        compiler_params=_cparams(("arbitrary",)),
        name="moe_route",
    )(logits_t, router_b.reshape(ne, 1))


def _dispatch_body(e_ref, rk_ref, ps_ref, h2_ref, xs_hbm, sem, *, tm):
    def issue(m, carry):
        for kk in range(TOP_K):
            row = ps_ref[e_ref[0, kk, m]] + rk_ref[0, kk, m]
            pltpu.make_async_copy(_slab(h2_ref, m), _slab(xs_hbm, row), sem).start(priority=kk % 2)
        return carry

    lax.fori_loop(0, tm, issue, 0)
    all_rows = xs_hbm.at[pl.ds(0, tm * TOP_K * PCH)]
    pltpu.make_async_copy(all_rows, all_rows, sem).wait()


def _assign_specs(tm, index_map):
    blk = pl.BlockSpec((1, TOP_K, tm), index_map, memory_space=pltpu.SMEM)
    return blk, pl.BlockSpec(memory_space=pltpu.SMEM)


def _dispatch_call(eidx, rank, pad_start, h2s, n_rows):
    tn = h2s.shape[0] // PCH
    tm = TOK_TILE
    assert eidx.shape == (tn // tm, TOP_K, tm)
    blk, whole = _assign_specs(tm, lambda i: (i, 0, 0))
    return pl.pallas_call(
        functools.partial(_dispatch_body, tm=tm),
        grid=(tn // tm,),
        in_specs=[blk, blk, whole, pl.BlockSpec((tm * PCH, LANES), lambda i: (i, 0))],
        out_specs=pl.BlockSpec(memory_space=pl.ANY),
        out_shape=jax.ShapeDtypeStruct((n_rows * PCH, LANES), I32),
        scratch_shapes=[pltpu.SemaphoreType.DMA],
        compiler_params=_cparams(("arbitrary",)),
        name="moe_dispatch",
    )(eidx, rank, pad_start, h2s)


def _expert_body(bi_ref, nr_ref, ld_ref, nx_ref, xs_ref, wg_hbm, wu_hbm, wd_hbm, ob_ref, wg_buf, wu_buf, wd_buf,
                 wg_bf, wu_bf, wd_bf, sem):
    del bi_ref
    i = pl.program_id(0)
    nr = nr_ref[i]
    slot = ld_ref[i]

    def fetch(e, s):
        return (pltpu.make_async_copy(wg_hbm.at[e], wg_buf.at[s], sem.at[s]),
                pltpu.make_async_copy(wu_hbm.at[e], wu_buf.at[s], sem.at[s]),
                pltpu.make_async_copy(wd_hbm.at[e], wd_buf.at[s], sem.at[s]))

    @pl.when(i == 0)
    def _():
        for s in range(WEIGHT_SLOTS - 1):
            e0 = nx_ref[nx_ref.shape[0] - (WEIGHT_SLOTS - 1) + s]

            @pl.when(e0 >= 0)
            def _():
                for cp in fetch(e0, s):
                    cp.start()

    @pl.when(slot >= 0)
    def _():
        for cp in fetch(0, slot):
            cp.wait()

        @pl.when(nx_ref[i] >= 0)
        def _():
            for cp in fetch(nx_ref[i], lax.rem(slot + WEIGHT_SLOTS - 1, WEIGHT_SLOTS)):
                cp.start()

        wg_bf[...] = wg_buf[slot].astype(BF16)
        wu_bf[...] = wu_buf[slot].astype(BF16)
        wd_bf[...] = wd_buf[slot].astype(BF16)

    @pl.when(nr > 0)
    def _():
        part = MOE_BLK // EXPERT_PARTS
        firsts = [q * part for q in range(EXPERT_PARTS)]
        rid = lax.broadcasted_iota(I32, (part, LANES), 0)
        x = [_rows_from_packed(xs_ref, part, rid < nr - f, f) for f in firsts]
        hg = [jnp.dot(v, wg_bf[...], preferred_element_type=F32) for v in x]
        hu = [jnp.dot(v, wu_bf[...], preferred_element_type=F32) for v in x]
        hh = [(g * _sigmoid(g) * u).astype(BF16) for g, u in zip(hg, hu)]
        out = [jnp.dot(v, wd_bf[...], preferred_element_type=F32) for v in hh]
        for f, v in zip(firsts, out):
            _rows_to_packed(ob_ref, v, f)


def _expert_tables(counts, pad_start, pad_end, nb):
    ne = counts.shape[0]
    first_row = jnp.arange(nb, dtype=I32) * MOE_BLK
    block_e = jnp.minimum(jnp.sum(pad_end[None, :] <= first_row[:, None], axis=1), ne - 1).astype(I32)
    block_rows = jnp.clip(pad_start[block_e] + counts[block_e] - first_row, 0, MOE_BLK).astype(I32)
    block_i = jnp.minimum(jnp.arange(nb, dtype=I32), pad_end[-1] // MOE_BLK - 1).astype(I32)
    has = counts > 0
    ordinal = jnp.cumsum(has.astype(I32)) - 1
    ids = jnp.where(has, jnp.arange(ne, dtype=I32), ne)
    nxt = jnp.concatenate([lax.cummin(ids, reverse=True)[1:], jnp.full((1,), ne, I32)])
    nxt = jnp.concatenate([nxt, jnp.full((1,), ne, I32)])
    hop = lambda e, k: functools.reduce(lambda x, _: nxt[x], range(k), e)
    starts = (first_row == pad_start[block_e]) & (block_rows > 0)
    load_slot = jnp.where(starts, ordinal[block_e] % WEIGHT_SLOTS, -1).astype(I32)
    ahead = hop(block_e, WEIGHT_SLOTS - 1)
    first = jnp.min(ids)
    lead = jnp.stack([hop(first, k) for k in range(WEIGHT_SLOTS - 1)])
    next_e = jnp.concatenate([jnp.where(starts, ahead, ne), lead])
    next_e = jnp.where(next_e < ne, next_e, -1).astype(I32)
    return block_i, block_rows, load_slot, next_e


def _expert_call(tables, xs, wg, wu, wd):
    nb = xs.shape[0] // (MOE_BLK * PCH)
    d, ff = wg.shape[1], wg.shape[2]
    rows = pl.BlockSpec((MOE_BLK * PCH, LANES), lambda i, bi, nr, ld, nx: (bi[i], 0))
    hbm = pl.BlockSpec(memory_space=pl.ANY)
    grid_spec = pltpu.PrefetchScalarGridSpec(
        num_scalar_prefetch=4,
        grid=(nb,),
        in_specs=[rows, hbm, hbm, hbm],
        out_specs=rows,
        scratch_shapes=[pltpu.VMEM((WEIGHT_SLOTS, d, ff), F32), pltpu.VMEM((WEIGHT_SLOTS, d, ff), F32),
                        pltpu.VMEM((WEIGHT_SLOTS, ff, d), F32),
                        pltpu.VMEM((d, ff), BF16), pltpu.VMEM((d, ff), BF16), pltpu.VMEM((ff, d), BF16),
                        pltpu.SemaphoreType.DMA((WEIGHT_SLOTS,))],
    )
    return pl.pallas_call(
        _expert_body,
        grid_spec=grid_spec,
        out_shape=jax.ShapeDtypeStruct(xs.shape, I32),
        compiler_params=_cparams(("arbitrary",), VMEM_LIMIT),
        name="moe_experts",
    )(*tables, xs, wg, wu, wd)


def _combine_body(e_ref, rk_ref, en_ref, rkn_ref, wt_ref, ps_ref, ob_hbm, h2_ref, x1_ref, mod_ref, sg_ref, su_ref,
                  sd_ref, fg_ref, out_ref, gbuf, rbuf, sem, *, tm, nl):
    bb, ll, d = x1_ref.shape
    step = pl.program_id(0) * nl + pl.program_id(1)
    last = pl.num_programs(0) * nl - 1
    slot = lax.rem(step, 2)

    def request(e_tab, rk_tab, m, s):
        for kk in range(TOP_K):
            row = ps_ref[e_tab[0, kk, m]] + rk_tab[0, kk, m]
            pltpu.make_async_copy(_slab(ob_hbm, row), _slab(gbuf.at[s], m * TOP_K + kk),
                                  sem.at[s]).start(priority=kk % 2)

    def mix(m):
        rows = gbuf.at[slot]
        lo, hi = _unpack_pair(_slab(rows, m * TOP_K)[...])
        wt = wt_ref[0, 0, m]
        acc_lo, acc_hi = wt * lo, wt * hi
        for kk in range(1, TOP_K):
            lo, hi = _unpack_pair(_slab(rows, m * TOP_K + kk)[...])
            wt = wt_ref[0, kk, m]
            acc_lo, acc_hi = acc_lo + wt * lo, acc_hi + wt * hi
        _fslab(rbuf, m)[...] = jnp.concatenate([acc_lo, acc_hi], axis=0)

    @pl.when(step == 0)
    def _():
        def first(m, carry):
            request(e_ref, rk_ref, m, 0)
            return carry
        lax.fori_loop(0, tm, first, 0)

    pltpu.make_async_copy(ob_hbm.at[pl.ds(0, tm * TOP_K * PCH)], gbuf.at[slot], sem.at[slot]).wait()

    @pl.when(step < last)
    def _():
        def ahead(m, carry):
            request(en_ref, rkn_ref, m, 1 - slot)
            return carry
        lax.fori_loop(0, tm, ahead, 0)

    def sum_rows(m, carry):
        mix(m)
        return carry

    lax.fori_loop(0, tm, sum_rows, 0)

    routed = jnp.concatenate([rbuf[pl.ds(c, tm, stride=CHUNKS), :] for c in range(CHUNKS)], axis=1)
    h2 = _rows_from_packed(h2_ref, tm)
    hg = jnp.dot(h2, sg_ref[...], preferred_element_type=F32)
    hu = jnp.dot(h2, su_ref[...], preferred_element_type=F32)
    shared = jnp.dot((hg * _sigmoid(hg) * hu).astype(BF16), sd_ref[...], preferred_element_type=F32)
    ff = (routed + shared).reshape(bb, ll, d)
    x2 = x1_ref[...] + mod_ref[:, 5:6, :] * ff
    out_ref[...] = x2 * lax.rsqrt(jnp.mean(x2 * x2, axis=-1, keepdims=True) + NORM_EPS) * fg_ref[...]


def _combine_call(eidx, rank, wts, first_tok, pad_start, ob, h2s, x1, mod, p):
    bn, seq, d = x1.shape
    tm = CMB_TILE
    bb, ll = _tile(bn, seq, tm)
    nl = seq // ll
    tn = bn * seq
    nsteps = tn // tm
    per = eidx.shape[2] // tm
    assert first_tok % tm == 0 and eidx.shape[2] % tm == 0
    tile = lambda g: ((first_tok // tm + g) // per, 0, (first_tok // tm + g) % per)
    smem, whole = _assign_specs(tm, lambda b, l: tile(b * nl + l))
    smem_next, _ = _assign_specs(tm, lambda b, l: tile(jnp.minimum(b * nl + l + 1, nsteps - 1)))
    tok = pl.BlockSpec((bb, ll, d), lambda b, l: (b, l, 0))
    full = lambda a: pl.BlockSpec(a.shape, lambda b, l: (0,) * a.ndim)
    consts = (p["sh_gate"], p["sh_up"], p["sh_down"], p["final_g"])
    return pl.pallas_call(
        functools.partial(_combine_body, tm=tm, nl=nl),
        grid=(bn // bb, nl),
        in_specs=[smem, smem, smem_next, smem_next, smem, whole, pl.BlockSpec(memory_space=pl.ANY),
                  pl.BlockSpec((tm * PCH, LANES), lambda b, l: (b * nl + l, 0)),
                  tok, pl.BlockSpec((bb, 6, d), lambda b, l: (b, 0, 0))] + [full(c) for c in consts],
        out_specs=tok,
        out_shape=jax.ShapeDtypeStruct((bn, seq, d), F32),
        scratch_shapes=[pltpu.VMEM((2, tm * TOP_K * PCH, LANES), I32), pltpu.VMEM((tm * CHUNKS, LANES), F32),
                        pltpu.SemaphoreType.DMA((2,))],
        compiler_params=_cparams(("arbitrary", "arbitrary"), VMEM_LIMIT),
        name="moe_combine_final",
    )(eidx, rank, eidx, rank, wts, pad_start, ob, h2s, x1, mod, *consts)


def _layer_params(l, ada_w, ada_b, norm1_g, norm2_g, w_in, mu_shift, rw_w0, rw_w_up, rw_a0, rw_a_up, rw_g_up,
                  rw_k_k, rw_k_a, rw_r_k, rw_gn_g, rw_gn_b, gla_a_up, gla_a_bias, gla_norm_g, w_pa, w_pb, w_out,
                  router_w, router_b, exp_gate, exp_up, exp_down, sh_gate, sh_up, sh_down):
    d = D_MODEL
    wi = w_in[l]
    gla0 = RW_SHIFT_COLS
    xal0 = gla0 + QKV_W
    pad = jnp.zeros((d, XAL_W - GLA_GATE_RANK), F32)
    w_pack = jnp.concatenate([wi[:, :xal0], wi[:, xal0:xal0 + GLA_GATE_RANK], pad,
                              wi[:, xal0 + GLA_GATE_RANK:]], axis=1).astype(BF16)
    zr = jnp.zeros((RW_W_RANK, RW_WIDTH), F32)
    hid = jnp.arange(RW_WIDTH) // RW_HEAD
    row = lambda a: a.reshape(1, -1)
    rw_t = router_w[l].T
    rw_hi = rw_t.astype(BF16)
    return dict(
        ada_w=ada_w[l], ada_b=ada_b[l], norm1_g=norm1_g[l], norm2_g=norm2_g[l].reshape(1, 1, d), w_pack=w_pack,
        mu=mu_shift[l].reshape(1, 1, -1), w0=row(rw_w0[l]), wup=jnp.concatenate([rw_w_up[l], zr], axis=0),
        a0=row(rw_a0[l]), aup=jnp.concatenate([zr, rw_a_up[l]], axis=0), gup=rw_g_up[l].astype(BF16),
        kk=row(rw_k_k[l]), ka=row(rw_k_a[l]), rk=row(rw_r_k[l]),
        bd64=(hid[:, None] == hid[None, :]).astype(BF16),
        gn_g=row(rw_gn_g[l]), gn_b=row(rw_gn_b[l]),
        gla_aup=jnp.concatenate([gla_a_up[l], jnp.zeros((XAL_W - GLA_GATE_RANK, GLA_KW), F32)], axis=0),
        gla_ab=row(gla_a_bias[l]), gla_ng=row(gla_norm_g[l]),
        w_pa=w_pa[l].astype(BF16), w_pb=w_pb[l].astype(BF16), w_out=w_out[l].astype(BF16),
        rw_hi=rw_hi, rw_lo=(rw_t - rw_hi.astype(F32)).astype(BF16), router_b=router_b[l],
        exp_gate=exp_gate[l], exp_up=exp_up[l], exp_down=exp_down[l],
        sh_gate=sh_gate[l].astype(BF16), sh_up=sh_up[l].astype(BF16), sh_down=sh_down[l].astype(BF16),
    )


def _mixer_group(x, mod, s_rw, s_sh, s_gla, p):
    pa, qkv, xal, gg, mg = _inproj_call(x, mod, p["norm1_g"], p["w_pack"])
    r, lw, k2, v, a_s, b_s, g, bonus, new_sh = _rwprep_call(pa, s_sh, p)
    y, rw_new = _rwscan_call(r, lw, k2, v, a_s, b_s, s_rw)
    o_b, gla_new = _gla_call(qkv, xal, gg, s_gla, p)
    x1, h2s, logits = _merge_call(y, g, bonus, o_b, mg, x, mod, p)
    states = (rw_new, new_sh[:, 0, :], gla_new)
    return x1, h2s, logits, states


def _moe(h2s, logits, p):
    tn = h2s.shape[0] // PCH
    eidx, rank, wts, counts = _route_call(logits, p["router_b"])
    counts = counts[:, 0].astype(I32)
    padded = (counts + MOE_BLK - 1) // MOE_BLK * MOE_BLK
    pad_end = jnp.cumsum(padded)
    pad_start = (pad_end - padded).astype(I32)
    nb = (tn * TOP_K + N_EXPERTS * (MOE_BLK - 1)) // MOE_BLK + 1
    tables = _expert_tables(counts, pad_start, pad_end, nb)
    xs = _dispatch_call(eidx, rank, pad_start, h2s, nb * MOE_BLK)
    ob = _expert_call(tables, xs, p["exp_gate"], p["exp_up"], p["exp_down"])
    return ob, (eidx, rank, wts, pad_start)


def kernel(x_prompt, x_sample, c_prompt, c_sample, state_rwkv, state_shift, state_gla, ada_w, ada_b, norm1_g,
           norm2_g, w_in, mu_shift, rw_w0, rw_w_up, rw_a0, rw_a_up, rw_g_up, rw_k_k, rw_k_a, rw_r_k, rw_gn_g,
           rw_gn_b, gla_a_up, gla_a_bias, gla_norm_g, w_pa, w_pb, w_out, router_w, router_b, exp_gate, exp_up,
           exp_down, sh_gate, sh_up, sh_down, final_g):
    depth = ada_w.shape[0]
    bp, bs = x_prompt.shape[0], x_sample.shape[0]
    tp = bp * x_prompt.shape[1]
    xs_g = [x_prompt, x_sample]
    c_all = jnp.concatenate([c_prompt, c_sample], axis=0)
    zeros = lambda shape: jnp.zeros(shape, x_prompt.dtype)
    new_states = [[], []]
    fg = final_g.reshape(1, 1, D_MODEL)
    for l in range(depth):
        p = _layer_params(l, ada_w, ada_b, norm1_g, norm2_g, w_in, mu_shift, rw_w0, rw_w_up, rw_a0, rw_a_up,
                          rw_g_up, rw_k_k, rw_k_a, rw_r_k, rw_gn_g, rw_gn_b, gla_a_up, gla_a_bias, gla_norm_g,
                          w_pa, w_pb, w_out, router_w, router_b, exp_gate, exp_up, exp_down, sh_gate, sh_up,
                          sh_down)
        p["final_g"] = fg
        mod_all = _mod_call(c_all, p["ada_w"], p["ada_b"])
        mods = [mod_all[:bp], mod_all[bp:]]
        states_in = [
            (zeros((bp, RW_HEADS, RW_HEAD, RW_HEAD)), zeros((bp, RW_SHIFT_COLS)),
             zeros((bp, GLA_HEADS, GLA_DK, GLA_DV))),
            (state_rwkv[l], state_shift[l], state_gla[l]),
        ]
        x1s, h2ss, lgs = [], [], []
        for gi in range(2):
            x1, h2s, logits, st = _mixer_group(xs_g[gi], mods[gi], *states_in[gi], p)
            x1s.append(x1)
            h2ss.append(h2s)
            lgs.append(logits)
            new_states[gi].append(st)
        h2_all = jnp.concatenate(h2ss, axis=0)
        ob, (eidx, rank, wts, pad_start) = _moe(h2_all, jnp.concatenate(lgs, axis=1), p)
        assert depth == 1, "the fused final norm assumes a single layer"
        xs_g = [
            _combine_call(eidx, rank, wts, 0, pad_start, ob, h2ss[0], x1s[0], mods[0], p),
            _combine_call(eidx, rank, wts, tp, pad_start, ob, h2ss[1], x1s[1], mods[1], p),
        ]
    stack = lambda gi, j: jnp.stack([s[j] for s in new_states[gi]])
    return (xs_g[0], xs_g[1], stack(0, 0), stack(0, 1), stack(0, 2), stack(1, 0), stack(1, 1), stack(1, 2))
```

```python
import functools

import jax
import jax.numpy as jnp
from jax import lax
from jax.experimental import pallas as pl
from jax.experimental.pallas import tpu as pltpu

F32, BF16, I32 = jnp.float32, jnp.bfloat16, jnp.int32

D_MODEL = 1024
RW_HEADS, RW_HEAD = 8, 64
RW_WIDTH = RW_HEADS * RW_HEAD
RW_W_RANK, RW_A_RANK, RW_G_RANK = 64, 64, 128
RW_GN_EPS = 64e-5
GLA_HEADS, GLA_DK, GLA_DV = 4, 64, 128
GLA_KW, GLA_VW = GLA_HEADS * GLA_DK, GLA_HEADS * GLA_DV
GLA_GATE_RANK = 16
GLA_GATE_TAU = 16.0
GLA_CHUNK = 16
RW_SHIFT_COLS = 3 * RW_WIDTH + RW_W_RANK + RW_A_RANK + RW_G_RANK
N_EXPERTS, TOP_K, N_GROUPS, TOPK_GROUPS = 256, 8, 8, 4
GROUP_SIZE = N_EXPERTS // N_GROUPS
EXPERT_FF = 256
ROUTED_SCALE = 2.5
NORM_EPS = 1e-6

LANES = 128
SUBLANES = 8
CHUNKS = D_MODEL // LANES
PCH = CHUNKS // 2
UNIT = 64
RW_SCAN_PASSES = (1, 1, 1, 1, 1)
GLA_UNITS_PER_STEP = 4
RW_UNITS_PER_STEP = 4
VMEM_LIMIT = 56 * 1024 * 1024

PA_W, QKV_W, XAL_W, GG_W, MG_W = RW_SHIFT_COLS, 2 * GLA_KW + GLA_VW, LANES, GLA_VW, 2 * D_MODEL
PACK_OFFS = (0, PA_W, PA_W + QKV_W, PA_W + QKV_W + XAL_W, PA_W + QKV_W + XAL_W + GG_W)
PACK_W = PA_W + QKV_W + XAL_W + GG_W + MG_W

TOK_TILE = 256
MOE_BLK = 256
EXPERT_PARTS = 1
WEIGHT_SLOTS = 3
ROW_SLOTS = 3
CMB_TILE = 128

_DN = {
    "nn": (((1,), (0,)), ((), ())),
    "nt": (((1,), (1,)), ((), ())),
    "tn": (((0,), (0,)), ((), ())),
}


def _split(x, pieces):
    out, rem = [], x
    for i in range(pieces):
        p = rem.astype(BF16)
        out.append(p)
        if i + 1 < pieces:
            rem = rem - p.astype(F32)
    return out


def _mm(a, b, form="nn", passes=1):
    dn = _DN[form]
    if passes == 6:
        return lax.dot_general(a.astype(F32), b.astype(F32), dn, precision=lax.Precision.HIGHEST,
                               preferred_element_type=F32)
    if passes == 1:
        return lax.dot_general(a.astype(BF16), b.astype(BF16), dn, preferred_element_type=F32)
    ah, al = _split(a, 2)
    bh, bl = _split(b, 2)
    out = lax.dot_general(ah, bh, dn, preferred_element_type=F32)
    out = out + lax.dot_general(ah, bl, dn, preferred_element_type=F32)
    return out + lax.dot_general(al, bh, dn, preferred_element_type=F32)


def _mm01(m01, x, pieces=3):
    m = m01.astype(BF16)
    out = None
    for p in _split(x, pieces):
        t = lax.dot_general(m, p, _DN["nn"], preferred_element_type=F32)
        out = t if out is None else out + t
    return out


def _xmm01(x, m01, pieces=2):
    m = m01.astype(BF16)
    out = None
    for p in _split(x, pieces):
        t = lax.dot_general(p, m, _DN["nn"], preferred_element_type=F32)
        out = t if out is None else out + t
    return out


HI16 = -65536


def _bf16_bits(x):
    return lax.bitcast_convert_type(x.astype(BF16).astype(F32), I32)


def _unpack_pair(w):
    return lax.bitcast_convert_type(w << 16, F32), lax.bitcast_convert_type(w & HI16, F32)


def _rows_to_packed(ref, x, first=0):
    for c in range(PCH):
        lo = _bf16_bits(x[:, c * LANES:(c + 1) * LANES])
        hi = _bf16_bits(x[:, (c + PCH) * LANES:(c + PCH + 1) * LANES])
        ref[pl.ds(first * PCH + c, x.shape[0], stride=PCH), :] = ((lo >> 16) & 0xFFFF) | (hi & HI16)


def _rows_from_packed(ref, n, live=None, first=0):
    lows, highs = [], []
    for c in range(PCH):
        w = ref[pl.ds(first * PCH + c, n, stride=PCH), :]
        if live is not None:
            w = jnp.where(live, w, 0)
        lo, hi = _unpack_pair(w)
        lows.append(lo.astype(BF16))
        highs.append(hi.astype(BF16))
    return jnp.concatenate(lows + highs, axis=1)


def _slab(ref, row):
    return ref.at[pl.ds(pl.multiple_of(row * PCH, PCH), PCH)]


def _fslab(ref, row):
    return ref.at[pl.ds(pl.multiple_of(row * CHUNKS, CHUNKS), CHUNKS)]


def _sigmoid(x):
    return 1.0 / (1.0 + jnp.exp(-x))


def _softplus(x):
    return jnp.maximum(x, 0.0) + jnp.log(1.0 + jnp.exp(-jnp.abs(x)))


def _log2(n):
    assert n > 0 and n & (n - 1) == 0, n
    return n.bit_length() - 1


def _cparams(sem, vmem=None):
    return pltpu.CompilerParams(dimension_semantics=sem, vmem_limit_bytes=vmem)


def _mod_body(c_ref, w_ref, b_ref, o_ref):
    c = c_ref[...]
    o_ref[0] = _mm(c * _sigmoid(c), w_ref[...], passes=6) + b_ref[...]


def _mod_call(c_all, ada_w, ada_b):
    bt, d = c_all.shape
    out = pl.pallas_call(
        _mod_body,
        grid=(6,),
        in_specs=[pl.BlockSpec((bt, d), lambda k: (0, 0)),
                  pl.BlockSpec((d, d), lambda k: (0, k)),
                  pl.BlockSpec((1, d), lambda k: (0, k))],
        out_specs=pl.BlockSpec((1, bt, d), lambda k: (k, 0, 0)),
        out_shape=jax.ShapeDtypeStruct((6, bt, d), F32),
        compiler_params=_cparams(("arbitrary",)),
        name="adaln_mod",
    )(c_all, ada_w, ada_b.reshape(1, 6 * d))
    return jnp.transpose(out, (1, 0, 2))


def _inproj_body(x_ref, mod_ref, g_ref, w_ref, pa_ref, qkv_ref, xal_ref, gg_ref, mg_ref):
    bb, ll, d = x_ref.shape
    x = x_ref[...]
    y = x * lax.rsqrt(jnp.mean(x * x, axis=-1, keepdims=True) + NORM_EPS) * g_ref[...]
    h = y * (1.0 + mod_ref[:, 1:2, :]) + mod_ref[:, 0:1, :]
    hb = h.reshape(bb * ll, d).astype(BF16)
    for ref, off in zip((pa_ref, qkv_ref, xal_ref, gg_ref, mg_ref), PACK_OFFS):
        w = ref.shape[-1]
        ref[...] = jnp.dot(hb, w_ref[:, off:off + w], preferred_element_type=F32).reshape(bb, ll, w)


def _tile(bn, seq, tile):
    if seq >= tile:
        assert seq % tile == 0
        return 1, tile
    assert tile % seq == 0 and bn % (tile // seq) == 0
    return tile // seq, seq


def _inproj_call(x, mod, norm_g, w_pack):
    bn, seq, d = x.shape
    bb, ll = _tile(bn, seq, TOK_TILE)
    tok = lambda w: pl.BlockSpec((bb, ll, w), lambda b, l: (b, l, 0))
    widths = (PA_W, QKV_W, XAL_W, GG_W, MG_W)
    return pl.pallas_call(
        _inproj_body,
        grid=(bn // bb, seq // ll),
        in_specs=[tok(d),
                  pl.BlockSpec((bb, 6, d), lambda b, l: (b, 0, 0)),
                  pl.BlockSpec((1, 1, d), lambda b, l: (0, 0, 0)),
                  pl.BlockSpec((d, PACK_W), lambda b, l: (0, 0))],
        out_specs=[tok(w) for w in widths],
        out_shape=[jax.ShapeDtypeStruct((bn, seq, w), F32) for w in widths],
        compiler_params=_cparams(("arbitrary", "arbitrary"), VMEM_LIMIT),
        name="norm_inproj",
    )(x, mod, norm_g.reshape(1, 1, d), w_pack)


def _rwprep_body(pa_ref, sh_ref, mu_ref, w0_ref, wup_ref, a0_ref, aup_ref, gup_ref, kk_ref, ka_ref, rk_ref,
                 bd_ref, r_o, lw_o, k_o, v_o, a_o, b_o, g_o, bon_o, nsh_o, carry):
    bb, ll, wd = pa_ref.shape
    n = bb * ll
    hw = RW_WIDTH

    @pl.when(pl.program_id(1) == 0)
    def _():
        carry[...] = sh_ref[...]

    pa = pa_ref[...]
    rolled = pltpu.roll(pa.reshape(n, wd), 1, 0).reshape(bb, ll, wd)
    tok = lax.broadcasted_iota(I32, (bb, ll, wd), 1)
    prev = jnp.where(tok == 0, carry[...], rolled)
    last = pa_ref[:, ll - 1:ll, :]
    carry[...] = last
    nsh_o[...] = last
    xs = (pa + (prev - pa) * mu_ref[...]).reshape(n, wd)

    r, k, v = xs[:, 0:hw], xs[:, hw:2 * hw], xs[:, 2 * hw:3 * hw]
    xwa = xs[:, 3 * hw:3 * hw + LANES]
    xg = xs[:, 3 * hw + LANES:]
    w_log = -_softplus(-(w0_ref[...] + _mm(jnp.tanh(xwa), wup_ref[...], passes=3))) - 0.5
    lw = -jnp.exp(w_log)
    a = _sigmoid(a0_ref[...] + _mm(xwa, aup_ref[...], passes=3))
    g = _mm(_sigmoid(xg), gup_ref[...])
    bd = bd_ref[...]
    kkv = k * kk_ref[...]
    kkn = kkv * lax.rsqrt(jnp.maximum(_xmm01(kkv * kkv, bd), 1e-24))
    k2 = k * (1.0 + (a - 1.0) * ka_ref[...])
    bonus = _xmm01(r * k2 * rk_ref[...], bd) * v
    for ref, val in ((r_o, r), (lw_o, lw), (k_o, k2), (v_o, v), (a_o, -kkn), (b_o, kkn * a), (g_o, g),
                     (bon_o, bonus)):
        ref[...] = val.reshape(bb, ll, hw)


def _rwprep_call(pa, s_sh, p):
    bn, seq, wd = pa.shape
    bb, ll = _tile(bn, seq, TOK_TILE)
    hw = RW_WIDTH
    tok = lambda w: pl.BlockSpec((bb, ll, w), lambda b, l: (b, l, 0))
    row = lambda w: pl.BlockSpec((bb, 1, w), lambda b, l: (b, 0, 0))
    full = lambda a: pl.BlockSpec(a.shape, lambda b, l: (0,) * a.ndim)
    consts = (p["mu"], p["w0"], p["wup"], p["a0"], p["aup"], p["gup"], p["kk"], p["ka"], p["rk"], p["bd64"])
    outs = pl.pallas_call(
        _rwprep_body,
        grid=(bn // bb, seq // ll),
        in_specs=[tok(wd), row(wd)] + [full(c) for c in consts],
        out_specs=[tok(hw)] * 8 + [row(wd)],
        out_shape=[jax.ShapeDtypeStruct((bn, seq, hw), F32)] * 8 + [jax.ShapeDtypeStruct((bn, 1, wd), F32)],
        scratch_shapes=[pltpu.VMEM((bb, 1, wd), F32)],
        compiler_params=_cparams(("arbitrary", "arbitrary"), VMEM_LIMIT),
        name="rwkv_prep",
    )(pa, s_sh.reshape(bn, 1, wd), *consts)
    return outs


def _unit_masks(n, tl):
    ri = lax.broadcasted_iota(I32, (n, n), 0)
    ci = lax.broadcasted_iota(I32, (n, n), 1)
    same = (ri >> _log2(tl)) == (ci >> _log2(tl))
    return same, same & (ri > ci), same & (ri >= ci)


def _rwscan_body(r_ref, lw_ref, k_ref, v_ref, a_ref, b_ref, s0_ref, y_ref, sn_ref, st, *, nu, nseq, tl, passes):
    n = nseq * tl
    n2 = 2 * n
    p_aa, p_inv, p_apply, p_state, p_y = passes

    hd = RW_HEAD

    @pl.when(pl.program_id(1) == 0)
    def _():
        zero = jnp.zeros((hd, hd), F32)
        for q in range(nu * nseq):
            for p in range(RW_HEADS // 2):
                st[q, p] = jnp.concatenate(
                    [jnp.concatenate([s0_ref[q, 2 * p], zero], axis=1),
                     jnp.concatenate([zero, s0_ref[q, 2 * p + 1]], axis=1)], axis=0)

    same, _, incl = _unit_masks(n, tl)
    m_cum = jnp.where(incl, 1.0, 0.0)
    m_seq = jnp.where(same, 1.0, 0.0)
    ri = lax.broadcasted_iota(I32, (n2, n2), 0)
    ci = lax.broadcasted_iota(I32, (n2, n2), 1)
    rt, ct = ri & (n - 1), ci & (n - 1)
    dsame = ((rt >> _log2(tl)) == (ct >> _log2(tl))) & ((ri >> _log2(n)) == (ci >> _log2(n)))
    strict_d = dsame & (rt > ct)
    incl_d = dsame & (rt >= ct)
    eye_d = jnp.where(ri == ci, 1.0, 0.0)
    lane = lax.broadcasted_iota(I32, (1, LANES), 1)
    m0 = jnp.where(lane < RW_HEAD, 1.0, 0.0)
    m1 = 1.0 - m0

    def dup(x):
        return jnp.concatenate([x * m0, x * m1], axis=0)

    def seq_rows(x, q):
        if nseq == 1:
            return x
        return jnp.concatenate([x[q * tl:(q + 1) * tl], x[n + q * tl:n + (q + 1) * tl]], axis=0)

    def unit_rows(parts):
        if nseq == 1:
            return parts[0]
        return jnp.concatenate([p[0:tl] for p in parts] + [p[tl:2 * tl] for p in parts], axis=0)

    chains = [(u, p) for u in range(nu) for p in range(RW_HEADS // 2)]
    ids = range(len(chains))
    cat0 = lambda *xs: jnp.concatenate(xs, axis=0)

    def ld(ref, c):
        u, p = chains[c]
        return ref[u * nseq:(u + 1) * nseq, :, p * LANES:(p + 1) * LANES].reshape(n, LANES)

    lw = [ld(lw_ref, c) for c in ids]
    cum = [_mm01(m_cum, x) for x in lw]
    tot = [_mm01(m_seq, x) for x in lw]
    e_c = [jnp.exp(x) for x in cum]
    e_n = [jnp.exp(-x) for x in cum]
    e_l = [jnp.exp(t - x) for t, x in zip(tot, cum)]
    at_d = [dup(ld(a_ref, c) * jnp.exp(cum[c] - lw[c])) for c in ids]
    rt_d = [dup(ld(r_ref, c) * e_c[c]) for c in ids]
    bt_d = [dup(ld(b_ref, c) * e_n[c]) for c in ids]
    kt_d = [dup(ld(k_ref, c) * e_n[c]) for c in ids]
    bh_d = [dup(ld(b_ref, c) * e_l[c]) for c in ids]
    kh_d = [dup(ld(k_ref, c) * e_l[c]) for c in ids]
    v_d = [dup(ld(v_ref, c)) for c in ids]
    aa = [_mm(cat0(at_d[c], rt_d[c]), cat0(bt_d[c], kt_d[c]), "nt", p_aa) for c in ids]
    a_ab = [jnp.where(strict_d, x[0:n2, 0:n2], 0.0) for x in aa]
    a_ak = [jnp.where(strict_d, x[0:n2, n2:], 0.0) for x in aa]
    a_rb = [jnp.where(incl_d, x[n2:, 0:n2], 0.0) for x in aa]
    a_rk = [jnp.where(incl_d, x[n2:, n2:], 0.0) for x in aa]
    zy = [_mm(cat0(a_ak[c], a_rk[c]), v_d[c], passes=p_apply) for c in ids]
    tinv = [eye_d + x for x in a_ab]
    nk = a_ab
    for _ in range(_log2(tl) - 1):
        nk = [_mm(x, x, passes=p_inv) for x in nk]
        tinv = [t + _mm(t, x, passes=p_inv) for t, x in zip(tinv, nk)]
    wu = [_mm(tinv[c], jnp.concatenate([at_d[c], zy[c][0:n2]], axis=1), passes=p_apply) for c in ids]
    seqs = range(nseq)
    srow = lambda c, q: (chains[c][0] * nseq + q, chains[c][1])
    s_old = [[st[srow(c, q)] for q in seqs] for c in ids]
    xs = [[_mm(cat0(seq_rows(wu[c][:, 0:LANES], q), seq_rows(rt_d[c], q)), s_old[c][q], "nt", p_state)
           for q in seqs] for c in ids]
    u_q = [[xs[c][q][0:2 * tl] + seq_rows(wu[c][:, LANES:], q) for q in seqs] for c in ids]
    for c in ids:
        for q in seqs:
            g_c = jnp.exp(tot[c][q * tl:q * tl + 1, :])
            st[srow(c, q)] = s_old[c][q] * g_c + _mm(cat0(u_q[c][q], seq_rows(v_d[c], q)),
                                                     cat0(seq_rows(bh_d[c], q), seq_rows(kh_d[c], q)), "tn", p_state)
    for c in ids:
        u, p = chains[c]
        y_d = (unit_rows([xs[c][q][2 * tl:] for q in seqs]) + _mm(a_rb[c], unit_rows(u_q[c]), passes=p_y)
               + zy[c][n2:])
        y_ref[u * nseq:(u + 1) * nseq, :, p * LANES:(p + 1) * LANES] = (y_d[0:n] + y_d[n:]).reshape(nseq, tl, LANES)

    @pl.when(pl.program_id(1) == pl.num_programs(1) - 1)
    def _():
        for q in range(nu * nseq):
            for p in range(RW_HEADS // 2):
                s = st[q, p]
                sn_ref[q, 2 * p] = s[0:hd, 0:hd]
                sn_ref[q, 2 * p + 1] = s[hd:, hd:]


def _unit_shape(bn, seq):
    if seq >= UNIT:
        assert seq % UNIT == 0
        return 1, UNIT
    assert UNIT % seq == 0 and bn % (UNIT // seq) == 0
    return UNIT // seq, seq


def _rwscan_call(r, lw, k2, v, a_s, b_s, s0, passes=RW_SCAN_PASSES):
    bn, seq, hw = r.shape
    nseq, tl = _unit_shape(bn, seq)
    nu = RW_UNITS_PER_STEP if bn % (RW_UNITS_PER_STEP * nseq) == 0 else 1
    rows = nu * nseq
    tok = pl.BlockSpec((rows, tl, hw), lambda b, c: (b, c, 0))
    stt = pl.BlockSpec((rows, RW_HEADS, RW_HEAD, RW_HEAD), lambda b, c: (b, 0, 0, 0))
    return pl.pallas_call(
        functools.partial(_rwscan_body, nu=nu, nseq=nseq, tl=tl, passes=passes),
        grid=(bn // rows, seq // tl),
        in_specs=[tok] * 6 + [stt],
        out_specs=[tok, stt],
        out_shape=[jax.ShapeDtypeStruct((bn, seq, hw), F32), jax.ShapeDtypeStruct(s0.shape, F32)],
        scratch_shapes=[pltpu.VMEM((rows, RW_HEADS // 2, LANES, LANES), F32)],
        compiler_params=_cparams(("arbitrary", "arbitrary"), VMEM_LIMIT),
        name="rwkv_scan",
    )(r, lw, k2, v, a_s, b_s, s0)


def _gla_body(qkv_ref, xal_ref, gate_ref, aup_ref, ab_ref, ng_ref, s0_ref, o_ref, sn_ref, st, *, nu, nseq, tl, cs):
    n = nseq * tl
    n2 = 2 * n
    nsub = tl // cs

    @pl.when(pl.program_id(1) == 0)
    def _():
        zero = jnp.zeros((GLA_DV, GLA_DK), F32)
        for q in range(nu * nseq):
            for p in range(GLA_HEADS // 2):
                st[q, p] = jnp.concatenate(
                    [jnp.concatenate([s0_ref[q, 2 * p].T, zero], axis=1),
                     jnp.concatenate([zero, s0_ref[q, 2 * p + 1].T], axis=1)], axis=0)

    same, _, incl = _unit_masks(n, cs)
    m_cum = jnp.where(incl, 1.0, 0.0)
    m_sub = jnp.where(same, 1.0, 0.0)
    ri = lax.broadcasted_iota(I32, (n2, n2), 0)
    ci = lax.broadcasted_iota(I32, (n2, n2), 1)
    rt, ct = ri & (n - 1), ci & (n - 1)
    causal_d = ((rt >> _log2(cs)) == (ct >> _log2(cs))) & ((ri >> _log2(n)) == (ci >> _log2(n))) & (rt >= ct)
    lane = lax.broadcasted_iota(I32, (1, LANES), 1)
    m0 = jnp.where(lane < GLA_DK, 1.0, 0.0)
    m1 = 1.0 - m0
    sr = lax.broadcasted_iota(I32, (2 * GLA_DV, LANES), 0)
    sc = lax.broadcasted_iota(I32, (2 * GLA_DV, LANES), 1)
    st_mask = jnp.where((sr >> _log2(GLA_DV)) == (sc >> _log2(GLA_DK)), 1.0, 0.0)

    def dup(x):
        return jnp.concatenate([x * m0, x * m1], axis=0)

    chains = [(u, p) for u in range(nu) for p in range(GLA_HEADS // 2)]
    ids = range(len(chains))
    urows = lambda u: slice(u * nseq, (u + 1) * nseq)
    ng = ng_ref[...]
    la_all = [-_softplus(-(_mm(xal_ref[urows(u), :, :].reshape(n, LANES), aup_ref[...], passes=3) + ab_ref[...]))
              * (1.0 / GLA_GATE_TAU) for u in range(nu)]

    def ld(ref, c, off, width):
        return ref[urows(chains[c][0]), :, off:off + width].reshape(n, width)

    q = [ld(qkv_ref, c, chains[c][1] * LANES, LANES) * (GLA_DK ** -0.5) for c in ids]
    k = [ld(qkv_ref, c, GLA_KW + chains[c][1] * LANES, LANES) for c in ids]
    vp = [ld(qkv_ref, c, 2 * GLA_KW + chains[c][1] * 2 * GLA_DV, 2 * GLA_DV) for c in ids]
    la = [la_all[u][:, p * LANES:(p + 1) * LANES] for u, p in chains]
    bc = [_mm01(m_cum, x) for x in la]
    bl = [_mm01(m_sub, x) for x in la]
    qe = [q[c] * jnp.exp(bc[c]) for c in ids]
    ke = [k[c] * jnp.exp(-bc[c]) for c in ids]
    kd = [k[c] * jnp.exp(bl[c] - bc[c]) for c in ids]
    att = [jnp.where(causal_d, _mm(dup(qe[c]), dup(ke[c]), "nt", passes=1), 0.0) for c in ids]
    v_st = [jnp.concatenate([x[:, 0:GLA_DV], x[:, GLA_DV:]], axis=0) for x in vp]
    o_st = [_mm(att[c], v_st[c], passes=1) for c in ids]
    upd = [[_mm(vp[c][r0:r0 + cs], kd[c][r0:r0 + cs], "tn", passes=1) for r0 in range(0, n, cs)] for c in ids]
    inter = [[None] * (n // cs) for _ in ids]
    for sq in range(nseq):
        s = [st[chains[c][0] * nseq + sq, chains[c][1]] for c in ids]
        for j in range(nsub):
            i = sq * nsub + j
            r0 = i * cs
            for c in ids:
                inter[c][i] = _mm(qe[c][r0:r0 + cs], s[c], "nt", passes=1)
                s[c] = s[c] * jnp.exp(bl[c][r0:r0 + 1, :]) + st_mask * upd[c][i]
        for c in ids:
            st[chains[c][0] * nseq + sq, chains[c][1]] = s[c]
    for c in ids:
        u, p = chains[c]
        o = o_st[c] + jnp.concatenate([x[:, 0:GLA_DV] for x in inter[c]] + [x[:, GLA_DV:] for x in inter[c]], axis=0)
        o = o * lax.rsqrt(jnp.mean(o * o, axis=-1, keepdims=True) + NORM_EPS) * ng
        goff = p * 2 * GLA_DV
        gp = ld(gate_ref, c, goff, 2 * GLA_DV)
        g_st = jnp.concatenate([gp[:, 0:GLA_DV], gp[:, GLA_DV:]], axis=0)
        ob = o * (g_st * _sigmoid(g_st))
        o_ref[urows(u), :, goff:goff + GLA_DV] = ob[0:n].reshape(nseq, tl, GLA_DV)
        o_ref[urows(u), :, goff + GLA_DV:goff + 2 * GLA_DV] = ob[n:].reshape(nseq, tl, GLA_DV)

    @pl.when(pl.program_id(1) == pl.num_programs(1) - 1)
    def _():
        for q in range(nu * nseq):
            for p in range(GLA_HEADS // 2):
                s = st[q, p]
                sn_ref[q, 2 * p] = s[0:GLA_DV, 0:GLA_DK].T
                sn_ref[q, 2 * p + 1] = s[GLA_DV:, GLA_DK:].T


def _gla_call(qkv, xal, gate, s0, p):
    bn, seq, _ = qkv.shape
    nseq, tl = _unit_shape(bn, seq)
    cs = min(GLA_CHUNK, seq)
    assert tl % cs == 0
    nu = GLA_UNITS_PER_STEP if bn % (GLA_UNITS_PER_STEP * nseq) == 0 else 1
    rows = nu * nseq
    tok = lambda w: pl.BlockSpec((rows, tl, w), lambda b, c: (b, c, 0))
    full = lambda a: pl.BlockSpec(a.shape, lambda b, c: (0,) * a.ndim)
    stt = pl.BlockSpec((rows, GLA_HEADS, GLA_DK, GLA_DV), lambda b, c: (b, 0, 0, 0))
    consts = (p["gla_aup"], p["gla_ab"], p["gla_ng"])
    return pl.pallas_call(
        functools.partial(_gla_body, nu=nu, nseq=nseq, tl=tl, cs=cs),
        grid=(bn // rows, seq // tl),
        in_specs=[tok(QKV_W), tok(XAL_W), tok(GG_W)] + [full(c) for c in consts] + [stt],
        out_specs=[tok(GLA_VW), stt],
        out_shape=[jax.ShapeDtypeStruct((bn, seq, GLA_VW), F32), jax.ShapeDtypeStruct(s0.shape, F32)],
        scratch_shapes=[pltpu.VMEM((rows, GLA_HEADS // 2, 2 * GLA_DV, LANES), F32)],
        compiler_params=_cparams(("arbitrary", "arbitrary"), VMEM_LIMIT),
        name="gla_chunked",
    )(qkv, xal, gate, *consts, s0)


def _merge_body(y_ref, g_ref, bon_ref, ob_ref, mg_ref, x_ref, mod_ref, gng_ref, gnb_ref, bd_ref, wpa_ref,
                wpb_ref, wout_ref, n2_ref, rwh_ref, rwl_ref, x1_o, h2_o, lg_o):
    bb, ll, d = x_ref.shape
    n = bb * ll
    hw = RW_WIDTH
    bd = bd_ref[...]
    y = y_ref[...].reshape(n, hw)
    mu = _xmm01(y, bd, pieces=3) * (1.0 / RW_HEAD)
    dv = y - mu
    var = _xmm01(dv * dv, bd) * (1.0 / RW_HEAD)
    yn = dv * lax.rsqrt(var + RW_GN_EPS) * gng_ref[...] + gnb_ref[...]
    o_a = (yn + bon_ref[...].reshape(n, hw)) * g_ref[...].reshape(n, hw)
    o_b = ob_ref[...].reshape(n, GLA_VW)
    mg = mg_ref[...].reshape(n, 2 * d)
    merged = _sigmoid(mg[:, 0:d]) * _mm(o_a, wpa_ref[...]) + _sigmoid(mg[:, d:]) * _mm(o_b, wpb_ref[...])
    mix = _mm(merged, wout_ref[...]).reshape(bb, ll, d)
    x1 = x_ref[...] + mod_ref[:, 2:3, :] * mix
    x1_o[...] = x1
    yn2 = x1 * lax.rsqrt(jnp.mean(x1 * x1, axis=-1, keepdims=True) + NORM_EPS) * n2_ref[...]
    h2 = (yn2 * (1.0 + mod_ref[:, 4:5, :]) + mod_ref[:, 3:4, :]).reshape(n, d)
    hh, hl = _split(h2, 2)
    rwh, rwl = rwh_ref[...], rwl_ref[...]
    nt = lambda a, b: lax.dot_general(a, b, _DN["nt"], preferred_element_type=F32)
    lg_o[...] = nt(rwh, hh) + nt(rwl, hh) + nt(rwh, hl)
    _rows_to_packed(h2_o, h2)


def _merge_call(y, g, bonus, o_b, mg, x, mod, p):
    bn, seq, d = x.shape
    bb, ll = _tile(bn, seq, TOK_TILE)
    nl = seq // ll
    tn = bn * seq
    tok = lambda w: pl.BlockSpec((bb, ll, w), lambda b, l: (b, l, 0))
    full = lambda a: pl.BlockSpec(a.shape, lambda b, l: (0,) * a.ndim)
    consts = (p["gn_g"], p["gn_b"], p["bd64"], p["w_pa"], p["w_pb"], p["w_out"], p["norm2_g"], p["rw_hi"],
              p["rw_lo"])
    return pl.pallas_call(
        _merge_body,
        grid=(bn // bb, nl),
        in_specs=[tok(RW_WIDTH)] * 3 + [tok(GLA_VW), tok(MG_W), tok(d),
                                        pl.BlockSpec((bb, 6, d), lambda b, l: (b, 0, 0))] + [full(c) for c in consts],
        out_specs=[tok(d),
                   pl.BlockSpec((bb * ll * PCH, LANES), lambda b, l: (b * nl + l, 0)),
                   pl.BlockSpec((N_EXPERTS, bb * ll), lambda b, l: (0, b * nl + l))],
        out_shape=[jax.ShapeDtypeStruct((bn, seq, d), F32),
                   jax.ShapeDtypeStruct((tn * PCH, LANES), I32),
                   jax.ShapeDtypeStruct((N_EXPERTS, tn), F32)],
        compiler_params=_cparams(("arbitrary", "arbitrary"), VMEM_LIMIT),
        name="merge_outproj_router",
    )(y, g, bonus, o_b, mg, x, mod, *consts)


def _route_body(lg_ref, rb_ref, e_o, rk_o, w_o, cnt_o, carry):
    ne, tm = lg_ref.shape

    @pl.when(pl.program_id(0) == 0)
    def _():
        carry[...] = jnp.zeros_like(carry)

    neg = -jnp.inf
    scores = _sigmoid(lg_ref[...])
    sel = scores + rb_ref[...]
    row_i = lax.broadcasted_iota(I32, (ne, tm), 0)
    row = row_i.astype(F32)
    grp = (row_i >> _log2(GROUP_SIZE)).astype(F32)

    def first_max(x, ids, none):
        m = jnp.max(x, axis=0, keepdims=True)
        return m, jnp.min(jnp.where(x == m, ids, none), axis=0, keepdims=True)

    gs = []
    gids = lax.broadcasted_iota(I32, (GROUP_SIZE, tm), 0)
    for gidx in range(N_GROUPS):
        rows = slice(gidx * GROUP_SIZE, (gidx + 1) * GROUP_SIZE)
        sg = _sigmoid(lg_ref[rows, :]) + rb_ref[rows, :]
        ids = (gids + gidx * GROUP_SIZE).astype(F32)
        m1, i1 = first_max(sg, ids, float(ne))
        gs.append(m1 + jnp.max(jnp.where(ids == i1, neg, sg), axis=0, keepdims=True))
    gs = jnp.concatenate(gs, axis=0)
    gid = lax.broadcasted_iota(I32, (N_GROUPS, tm), 0).astype(F32)
    cur = jnp.full((ne, tm), neg, F32)
    for _ in range(TOPK_GROUPS):
        _, gi = first_max(gs, gid, float(N_GROUPS))
        cur = jnp.where(grp == gi, sel, cur)
        gs = jnp.where(gid == gi, neg, gs)

    pm = jnp.zeros((ne, tm), F32)
    eidx, wts = [], []
    for _ in range(TOP_K):
        _, ei = first_max(cur, row, float(ne))
        hit = row == ei
        pm = jnp.where(hit, 1.0, pm)
        eidx.append(ei)
        wts.append(jnp.sum(jnp.where(hit, scores, 0.0), axis=0, keepdims=True))
        cur = jnp.where(hit, neg, cur)
    wsum = wts[0]
    for w in wts[1:]:
        wsum = wsum + w

    ri = lax.broadcasted_iota(I32, (tm, tm), 0)
    ci = lax.broadcasted_iota(I32, (tm, tm), 1)
    earlier = jnp.where(ri < ci, 1.0, 0.0)
    rank = _mm(pm, earlier, passes=1) + carry[...]
    carry[...] = carry[...] + jnp.sum(pm, axis=1, keepdims=True)
    cnt_o[...] = carry[...]

    rks = [jnp.sum(jnp.where(row == e, rank, 0.0), axis=0, keepdims=True) for e in eidx]
    e_o[0] = jnp.concatenate(eidx, axis=0).astype(I32)
    rk_o[0] = jnp.concatenate(rks, axis=0).astype(I32)
    w_o[0] = jnp.concatenate([w / wsum * ROUTED_SCALE for w in wts], axis=0)


def _route_call(logits_t, router_b):
    ne, tn = logits_t.shape
    tm = TOK_TILE
    assert tn % tm == 0
    col = pl.BlockSpec((ne, 1), lambda i: (0, 0))
    tab = pl.BlockSpec((1, TOP_K, tm), lambda i: (i, 0, 0))
    tab_shape = (tn // tm, TOP_K, tm)
    return pl.pallas_call(
        _route_body,
        grid=(tn // tm,),
        in_specs=[pl.BlockSpec((ne, tm), lambda i: (0, i)), col],
        out_specs=[tab, tab, tab, col],
        out_shape=[jax.ShapeDtypeStruct(tab_shape, I32), jax.ShapeDtypeStruct(tab_shape, I32),
                   jax.ShapeDtypeStruct(tab_shape, F32), jax.ShapeDtypeStruct((ne, 1), F32)],
        scratch_shapes=[pltpu.VMEM((ne, 1), F32)],
        compiler_params=_cparams(("arbitrary",)),
        name="moe_route",
    )(logits_t, router_b.reshape(ne, 1))


def _dispatch_body(e_ref, rk_ref, ps_ref, h2_ref, xs_hbm, sem, *, tm):
    def issue(m, carry):
        for kk in range(TOP_K):
            row = ps_ref[e_ref[0, kk, m]] + rk_ref[0, kk, m]
            pltpu.make_async_copy(_slab(h2_ref, m), _slab(xs_hbm, row), sem).start(priority=kk % 2)
        return carry

    lax.fori_loop(0, tm, issue, 0)
    all_rows = xs_hbm.at[pl.ds(0, tm * TOP_K * PCH)]
    pltpu.make_async_copy(all_rows, all_rows, sem).wait()


def _assign_specs(tm, index_map):
    blk = pl.BlockSpec((1, TOP_K, tm), index_map, memory_space=pltpu.SMEM)
    return blk, pl.BlockSpec(memory_space=pltpu.SMEM)


def _dispatch_call(eidx, rank, pad_start, h2s, n_rows):
    tn = h2s.shape[0] // PCH
    tm = TOK_TILE
    assert eidx.shape == (tn // tm, TOP_K, tm)
    blk, whole = _assign_specs(tm, lambda i: (i, 0, 0))
    return pl.pallas_call(
        functools.partial(_dispatch_body, tm=tm),
        grid=(tn // tm,),
        in_specs=[blk, blk, whole, pl.BlockSpec((tm * PCH, LANES), lambda i: (i, 0))],
        out_specs=pl.BlockSpec(memory_space=pl.ANY),
        out_shape=jax.ShapeDtypeStruct((n_rows * PCH, LANES), I32),
        scratch_shapes=[pltpu.SemaphoreType.DMA],
        compiler_params=_cparams(("arbitrary",)),
        name="moe_dispatch",
    )(eidx, rank, pad_start, h2s)


def _expert_body(bi_ref, nr_ref, ld_ref, nx_ref, xs_hbm, wg_hbm, wu_hbm, wd_hbm, ob_ref, wg_buf, wu_buf, wd_buf,
                 wg_bf, wu_bf, wd_bf, xbuf, sem, xsem):
    i = pl.program_id(0)
    nsteps = pl.num_programs(0)
    nr = nr_ref[i]
    slot = ld_ref[i]
    blk_rows = MOE_BLK * PCH

    def row_block(j):
        s = lax.rem(j, ROW_SLOTS)
        src = xs_hbm.at[pl.ds(pl.multiple_of(bi_ref[j] * blk_rows, blk_rows), blk_rows)]
        return pltpu.make_async_copy(src, xbuf.at[s], xsem.at[s])

    @pl.when(i == 0)
    def _():
        for j in range(ROW_SLOTS - 1):
            row_block(j).start()

    @pl.when(i + ROW_SLOTS - 1 < nsteps)
    def _():
        row_block(i + ROW_SLOTS - 1).start()

    def fetch(e, s):
        return (pltpu.make_async_copy(wg_hbm.at[e], wg_buf.at[s], sem.at[s]),
                pltpu.make_async_copy(wu_hbm.at[e], wu_buf.at[s], sem.at[s]),
                pltpu.make_async_copy(wd_hbm.at[e], wd_buf.at[s], sem.at[s]))

    @pl.when(i == 0)
    def _():
        for s in range(WEIGHT_SLOTS - 1):
            e0 = nx_ref[nx_ref.shape[0] - (WEIGHT_SLOTS - 1) + s]

            @pl.when(e0 >= 0)
            def _():
                for cp in fetch(e0, s):
                    cp.start()

    @pl.when(slot >= 0)
    def _():
        for cp in fetch(0, slot):
            cp.wait()

        @pl.when(nx_ref[i] >= 0)
        def _():
            for cp in fetch(nx_ref[i], lax.rem(slot + WEIGHT_SLOTS - 1, WEIGHT_SLOTS)):
                cp.start()

        wg_bf[...] = wg_buf[slot].astype(BF16)
        wu_bf[...] = wu_buf[slot].astype(BF16)
        wd_bf[...] = wd_buf[slot].astype(BF16)

    row_block(i).wait()

    @pl.when(nr > 0)
    def _():
        part = MOE_BLK // EXPERT_PARTS
        firsts = [q * part for q in range(EXPERT_PARTS)]
        rid = lax.broadcasted_iota(I32, (part, LANES), 0)
        xs_ref = xbuf.at[lax.rem(i, ROW_SLOTS)]
        x = [_rows_from_packed(xs_ref, part, rid < nr - f, f) for f in firsts]
        hg = [jnp.dot(v, wg_bf[...], preferred_element_type=F32) for v in x]
        hu = [jnp.dot(v, wu_bf[...], preferred_element_type=F32) for v in x]
        hh = [(g * _sigmoid(g) * u).astype(BF16) for g, u in zip(hg, hu)]
        out = [jnp.dot(v, wd_bf[...], preferred_element_type=F32) for v in hh]
        for f, v in zip(firsts, out):
            _rows_to_packed(ob_ref, v, f)


def _expert_tables(counts, pad_start, pad_end, nb):
    ne = counts.shape[0]
    first_row = jnp.arange(nb, dtype=I32) * MOE_BLK
    block_e = jnp.minimum(jnp.sum(pad_end[None, :] <= first_row[:, None], axis=1), ne - 1).astype(I32)
    block_rows = jnp.clip(pad_start[block_e] + counts[block_e] - first_row, 0, MOE_BLK).astype(I32)
    block_i = jnp.minimum(jnp.arange(nb, dtype=I32), pad_end[-1] // MOE_BLK - 1).astype(I32)
    has = counts > 0
    ordinal = jnp.cumsum(has.astype(I32)) - 1
    ids = jnp.where(has, jnp.arange(ne, dtype=I32), ne)
    nxt = jnp.concatenate([lax.cummin(ids, reverse=True)[1:], jnp.full((1,), ne, I32)])
    nxt = jnp.concatenate([nxt, jnp.full((1,), ne, I32)])
    hop = lambda e, k: functools.reduce(lambda x, _: nxt[x], range(k), e)
    starts = (first_row == pad_start[block_e]) & (block_rows > 0)
    load_slot = jnp.where(starts, ordinal[block_e] % WEIGHT_SLOTS, -1).astype(I32)
    ahead = hop(block_e, WEIGHT_SLOTS - 1)
    first = jnp.min(ids)
    lead = jnp.stack([hop(first, k) for k in range(WEIGHT_SLOTS - 1)])
    next_e = jnp.concatenate([jnp.where(starts, ahead, ne), lead])
    next_e = jnp.where(next_e < ne, next_e, -1).astype(I32)
    return block_i, block_rows, load_slot, next_e


def _expert_call(tables, xs, wg, wu, wd):
    nb = xs.shape[0] // (MOE_BLK * PCH)
    assert nb >= ROW_SLOTS
    d, ff = wg.shape[1], wg.shape[2]
    rows = pl.BlockSpec((MOE_BLK * PCH, LANES), lambda i, bi, nr, ld, nx: (bi[i], 0))
    hbm = pl.BlockSpec(memory_space=pl.ANY)
    grid_spec = pltpu.PrefetchScalarGridSpec(
        num_scalar_prefetch=4,
        grid=(nb,),
        in_specs=[hbm, hbm, hbm, hbm],
        out_specs=rows,
        scratch_shapes=[pltpu.VMEM((WEIGHT_SLOTS, d, ff), F32), pltpu.VMEM((WEIGHT_SLOTS, d, ff), F32),
                        pltpu.VMEM((WEIGHT_SLOTS, ff, d), F32),
                        pltpu.VMEM((d, ff), BF16), pltpu.VMEM((d, ff), BF16), pltpu.VMEM((ff, d), BF16),
                        pltpu.VMEM((ROW_SLOTS, MOE_BLK * PCH, LANES), I32),
                        pltpu.SemaphoreType.DMA((WEIGHT_SLOTS,)), pltpu.SemaphoreType.DMA((ROW_SLOTS,))],
    )
    return pl.pallas_call(
        _expert_body,
        grid_spec=grid_spec,
        out_shape=jax.ShapeDtypeStruct(xs.shape, I32),
        compiler_params=_cparams(("arbitrary",), VMEM_LIMIT),
        name="moe_experts",
    )(*tables, xs, wg, wu, wd)


def _combine_body(e_ref, rk_ref, en_ref, rkn_ref, wt_ref, ps_ref, ob_hbm, h2_ref, x1_ref, mod_ref, sg_ref, su_ref,
                  sd_ref, fg_ref, out_ref, gbuf, rbuf, sem, *, tm, nl):
    bb, ll, d = x1_ref.shape
    step = pl.program_id(0) * nl + pl.program_id(1)
    last = pl.num_programs(0) * nl - 1
    slot = lax.rem(step, 2)

    def request(e_tab, rk_tab, m, s):
        for kk in range(TOP_K):
            row = ps_ref[e_tab[0, kk, m]] + rk_tab[0, kk, m]
            pltpu.make_async_copy(_slab(ob_hbm, row), _slab(gbuf.at[s], m * TOP_K + kk),
                                  sem.at[s]).start(priority=kk % 2)

    def mix(m):
        rows = gbuf.at[slot]
        lo, hi = _unpack_pair(_slab(rows, m * TOP_K)[...])
        wt = wt_ref[0, 0, m]
        acc_lo, acc_hi = wt * lo, wt * hi
        for kk in range(1, TOP_K):
            lo, hi = _unpack_pair(_slab(rows, m * TOP_K + kk)[...])
            wt = wt_ref[0, kk, m]
            acc_lo, acc_hi = acc_lo + wt * lo, acc_hi + wt * hi
        _fslab(rbuf, m)[...] = jnp.concatenate([acc_lo, acc_hi], axis=0)

    @pl.when(step == 0)
    def _():
        def first(m, carry):
            request(e_ref, rk_ref, m, 0)
            return carry
        lax.fori_loop(0, tm, first, 0)

    pltpu.make_async_copy(ob_hbm.at[pl.ds(0, tm * TOP_K * PCH)], gbuf.at[slot], sem.at[slot]).wait()

    @pl.when(step < last)
    def _():
        def both(m, carry):
            request(en_ref, rkn_ref, m, 1 - slot)
            mix(m)
            return carry
        lax.fori_loop(0, tm, both, 0)

    @pl.when(step == last)
    def _():
        def only(m, carry):
            mix(m)
            return carry
        lax.fori_loop(0, tm, only, 0)

    routed = jnp.concatenate([rbuf[pl.ds(c, tm, stride=CHUNKS), :] for c in range(CHUNKS)], axis=1)
    h2 = _rows_from_packed(h2_ref, tm)
    hg = jnp.dot(h2, sg_ref[...], preferred_element_type=F32)
    hu = jnp.dot(h2, su_ref[...], preferred_element_type=F32)
    shared = jnp.dot((hg * _sigmoid(hg) * hu).astype(BF16), sd_ref[...], preferred_element_type=F32)
    ff = (routed + shared).reshape(bb, ll, d)
    x2 = x1_ref[...] + mod_ref[:, 5:6, :] * ff
    out_ref[...] = x2 * lax.rsqrt(jnp.mean(x2 * x2, axis=-1, keepdims=True) + NORM_EPS) * fg_ref[...]


def _combine_call(eidx, rank, wts, first_tok, pad_start, ob, h2s, x1, mod, p):
    bn, seq, d = x1.shape
    tm = CMB_TILE
    bb, ll = _tile(bn, seq, tm)
    nl = seq // ll
    tn = bn * seq
    nsteps = tn // tm
    per = eidx.shape[2] // tm
    assert first_tok % tm == 0 and eidx.shape[2] % tm == 0
    tile = lambda g: ((first_tok // tm + g) // per, 0, (first_tok // tm + g) % per)
    smem, whole = _assign_specs(tm, lambda b, l: tile(b * nl + l))
    smem_next, _ = _assign_specs(tm, lambda b, l: tile(jnp.minimum(b * nl + l + 1, nsteps - 1)))
    tok = pl.BlockSpec((bb, ll, d), lambda b, l: (b, l, 0))
    full = lambda a: pl.BlockSpec(a.shape, lambda b, l: (0,) * a.ndim)
    consts = (p["sh_gate"], p["sh_up"], p["sh_down"], p["final_g"])
    return pl.pallas_call(
        functools.partial(_combine_body, tm=tm, nl=nl),
        grid=(bn // bb, nl),
        in_specs=[smem, smem, smem_next, smem_next, smem, whole, pl.BlockSpec(memory_space=pl.ANY),
                  pl.BlockSpec((tm * PCH, LANES), lambda b, l: (b * nl + l, 0)),
                  tok, pl.BlockSpec((bb, 6, d), lambda b, l: (b, 0, 0))] + [full(c) for c in consts],
        out_specs=tok,
        out_shape=jax.ShapeDtypeStruct((bn, seq, d), F32),
        scratch_shapes=[pltpu.VMEM((2, tm * TOP_K * PCH, LANES), I32), pltpu.VMEM((tm * CHUNKS, LANES), F32),
                        pltpu.SemaphoreType.DMA((2,))],
        compiler_params=_cparams(("arbitrary", "arbitrary"), VMEM_LIMIT),
        name="moe_combine_final",
    )(eidx, rank, eidx, rank, wts, pad_start, ob, h2s, x1, mod, *consts)


def _layer_params(l, ada_w, ada_b, norm1_g, norm2_g, w_in, mu_shift, rw_w0, rw_w_up, rw_a0, rw_a_up, rw_g_up,
                  rw_k_k, rw_k_a, rw_r_k, rw_gn_g, rw_gn_b, gla_a_up, gla_a_bias, gla_norm_g, w_pa, w_pb, w_out,
                  router_w, router_b, exp_gate, exp_up, exp_down, sh_gate, sh_up, sh_down):
    d = D_MODEL
    wi = w_in[l]
    gla0 = RW_SHIFT_COLS
    xal0 = gla0 + QKV_W
    pad = jnp.zeros((d, XAL_W - GLA_GATE_RANK), F32)
    w_pack = jnp.concatenate([wi[:, :xal0], wi[:, xal0:xal0 + GLA_GATE_RANK], pad,
                              wi[:, xal0 + GLA_GATE_RANK:]], axis=1).astype(BF16)
    zr = jnp.zeros((RW_W_RANK, RW_WIDTH), F32)
    hid = jnp.arange(RW_WIDTH) // RW_HEAD
    row = lambda a: a.reshape(1, -1)
    rw_t = router_w[l].T
    rw_hi = rw_t.astype(BF16)
    return dict(
        ada_w=ada_w[l], ada_b=ada_b[l], norm1_g=norm1_g[l], norm2_g=norm2_g[l].reshape(1, 1, d), w_pack=w_pack,
        mu=mu_shift[l].reshape(1, 1, -1), w0=row(rw_w0[l]), wup=jnp.concatenate([rw_w_up[l], zr], axis=0),
        a0=row(rw_a0[l]), aup=jnp.concatenate([zr, rw_a_up[l]], axis=0), gup=rw_g_up[l].astype(BF16),
        kk=row(rw_k_k[l]), ka=row(rw_k_a[l]), rk=row(rw_r_k[l]),
        bd64=(hid[:, None] == hid[None, :]).astype(BF16),
        gn_g=row(rw_gn_g[l]), gn_b=row(rw_gn_b[l]),
        gla_aup=jnp.concatenate([gla_a_up[l], jnp.zeros((XAL_W - GLA_GATE_RANK, GLA_KW), F32)], axis=0),
        gla_ab=row(gla_a_bias[l]), gla_ng=row(gla_norm_g[l]),
        w_pa=w_pa[l].astype(BF16), w_pb=w_pb[l].astype(BF16), w_out=w_out[l].astype(BF16),
        rw_hi=rw_hi, rw_lo=(rw_t - rw_hi.astype(F32)).astype(BF16), router_b=router_b[l],
        exp_gate=exp_gate[l], exp_up=exp_up[l], exp_down=exp_down[l],
        sh_gate=sh_gate[l].astype(BF16), sh_up=sh_up[l].astype(BF16), sh_down=sh_down[l].astype(BF16),
    )


def _mixer_group(x, mod, s_rw, s_sh, s_gla, p):
    pa, qkv, xal, gg, mg = _inproj_call(x, mod, p["norm1_g"], p["w_pack"])
    r, lw, k2, v, a_s, b_s, g, bonus, new_sh = _rwprep_call(pa, s_sh, p)
    y, rw_new = _rwscan_call(r, lw, k2, v, a_s, b_s, s_rw)
    o_b, gla_new = _gla_call(qkv, xal, gg, s_gla, p)
    x1, h2s, logits = _merge_call(y, g, bonus, o_b, mg, x, mod, p)
    states = (rw_new, new_sh[:, 0, :], gla_new)
    return x1, h2s, logits, states


def _moe(h2s, logits, p):
    tn = h2s.shape[0] // PCH
    eidx, rank, wts, counts = _route_call(logits, p["router_b"])
    counts = counts[:, 0].astype(I32)
    padded = (counts + MOE_BLK - 1) // MOE_BLK * MOE_BLK
    pad_end = jnp.cumsum(padded)
    pad_start = (pad_end - padded).astype(I32)
    nb = (tn * TOP_K + N_EXPERTS * (MOE_BLK - 1)) // MOE_BLK + 1
    tables = _expert_tables(counts, pad_start, pad_end, nb)
    xs = _dispatch_call(eidx, rank, pad_start, h2s, nb * MOE_BLK)
    ob = _expert_call(tables, xs, p["exp_gate"], p["exp_up"], p["exp_down"])
    return ob, (eidx, rank, wts, pad_start)


def kernel(x_prompt, x_sample, c_prompt, c_sample, state_rwkv, state_shift, state_gla, ada_w, ada_b, norm1_g,
           norm2_g, w_in, mu_shift, rw_w0, rw_w_up, rw_a0, rw_a_up, rw_g_up, rw_k_k, rw_k_a, rw_r_k, rw_gn_g,
           rw_gn_b, gla_a_up, gla_a_bias, gla_norm_g, w_pa, w_pb, w_out, router_w, router_b, exp_gate, exp_up,
           exp_down, sh_gate, sh_up, sh_down, final_g):
    depth = ada_w.shape[0]
    bp, bs = x_prompt.shape[0], x_sample.shape[0]
    tp = bp * x_prompt.shape[1]
    xs_g = [x_prompt, x_sample]
    c_all = jnp.concatenate([c_prompt, c_sample], axis=0)
    zeros = lambda shape: jnp.zeros(shape, x_prompt.dtype)
    new_states = [[], []]
    fg = final_g.reshape(1, 1, D_MODEL)
    for l in range(depth):
        p = _layer_params(l, ada_w, ada_b, norm1_g, norm2_g, w_in, mu_shift, rw_w0, rw_w_up, rw_a0, rw_a_up,
                          rw_g_up, rw_k_k, rw_k_a, rw_r_k, rw_gn_g, rw_gn_b, gla_a_up, gla_a_bias, gla_norm_g,
                          w_pa, w_pb, w_out, router_w, router_b, exp_gate, exp_up, exp_down, sh_gate, sh_up,
                          sh_down)
        p["final_g"] = fg
        mod_all = _mod_call(c_all, p["ada_w"], p["ada_b"])
        mods = [mod_all[:bp], mod_all[bp:]]
        states_in = [
            (zeros((bp, RW_HEADS, RW_HEAD, RW_HEAD)), zeros((bp, RW_SHIFT_COLS)),
             zeros((bp, GLA_HEADS, GLA_DK, GLA_DV))),
            (state_rwkv[l], state_shift[l], state_gla[l]),
        ]
        x1s, h2ss, lgs = [], [], []
        for gi in range(2):
            x1, h2s, logits, st = _mixer_group(xs_g[gi], mods[gi], *states_in[gi], p)
            x1s.append(x1)
            h2ss.append(h2s)
            lgs.append(logits)
            new_states[gi].append(st)
        h2_all = jnp.concatenate(h2ss, axis=0)
        ob, (eidx, rank, wts, pad_start) = _moe(h2_all, jnp.concatenate(lgs, axis=1), p)
        assert depth == 1, "the fused final norm assumes a single layer"
        xs_g = [
            _combine_call(eidx, rank, wts, 0, pad_start, ob, h2ss[0], x1s[0], mods[0], p),
            _combine_call(eidx, rank, wts, tp, pad_start, ob, h2ss[1], x1s[1], mods[1], p),
        ]
    stack = lambda gi, j: jnp.stack([s[j] for s in new_states[gi]])
    return (xs_g[0], xs_g[1], stack(0, 0), stack(0, 1), stack(0, 2), stack(1, 0), stack(1, 1), stack(1, 2))
```

```python
import functools

import jax
import jax.numpy as jnp
from jax import lax
from jax.experimental import pallas as pl
from jax.experimental.pallas import tpu as pltpu

F32, BF16, I32 = jnp.float32, jnp.bfloat16, jnp.int32

D_MODEL = 1024
RW_HEADS, RW_HEAD = 8, 64
RW_WIDTH = RW_HEADS * RW_HEAD
RW_W_RANK, RW_A_RANK, RW_G_RANK = 64, 64, 128
RW_GN_EPS = 64e-5
GLA_HEADS, GLA_DK, GLA_DV = 4, 64, 128
GLA_KW, GLA_VW = GLA_HEADS * GLA_DK, GLA_HEADS * GLA_DV
GLA_GATE_RANK = 16
GLA_GATE_TAU = 16.0
GLA_CHUNK = 16
RW_SHIFT_COLS = 3 * RW_WIDTH + RW_W_RANK + RW_A_RANK + RW_G_RANK
N_EXPERTS, TOP_K, N_GROUPS, TOPK_GROUPS = 256, 8, 8, 4
GROUP_SIZE = N_EXPERTS // N_GROUPS
EXPERT_FF = 256
ROUTED_SCALE = 2.5
NORM_EPS = 1e-6

LANES = 128
SUBLANES = 8
CHUNKS = D_MODEL // LANES
PCH = CHUNKS // 2
UNIT = 64
RW_SCAN_PASSES = (1, 1, 1, 1, 1)
GLA_UNITS_PER_STEP = 4
RW_UNITS_PER_STEP = 4
VMEM_LIMIT = 56 * 1024 * 1024

PA_W, QKV_W, XAL_W, GG_W, MG_W = RW_SHIFT_COLS, 2 * GLA_KW + GLA_VW, LANES, GLA_VW, 2 * D_MODEL
PACK_OFFS = (0, PA_W, PA_W + QKV_W, PA_W + QKV_W + XAL_W, PA_W + QKV_W + XAL_W + GG_W)
PACK_W = PA_W + QKV_W + XAL_W + GG_W + MG_W

TOK_TILE = 256
MOE_BLK = 256
EXPERT_PARTS = 1
WEIGHT_SLOTS = 3
ROW_SLOTS = 3
CMB_TILE = 128

_DN = {
    "nn": (((1,), (0,)), ((), ())),
    "nt": (((1,), (1,)), ((), ())),
    "tn": (((0,), (0,)), ((), ())),
}


def _split(x, pieces):
    out, rem = [], x
    for i in range(pieces):
        p = rem.astype(BF16)
        out.append(p)
        if i + 1 < pieces:
            rem = rem - p.astype(F32)
    return out


def _mm(a, b, form="nn", passes=1):
    dn = _DN[form]
    if passes == 6:
        return lax.dot_general(a.astype(F32), b.astype(F32), dn, precision=lax.Precision.HIGHEST,
                               preferred_element_type=F32)
    if passes == 1:
        return lax.dot_general(a.astype(BF16), b.astype(BF16), dn, preferred_element_type=F32)
    ah, al = _split(a, 2)
    bh, bl = _split(b, 2)
    out = lax.dot_general(ah, bh, dn, preferred_element_type=F32)
    out = out + lax.dot_general(ah, bl, dn, preferred_element_type=F32)
    return out + lax.dot_general(al, bh, dn, preferred_element_type=F32)


def _mm01(m01, x, pieces=3):
    m = m01.astype(BF16)
    out = None
    for p in _split(x, pieces):
        t = lax.dot_general(m, p, _DN["nn"], preferred_element_type=F32)
        out = t if out is None else out + t
    return out


def _xmm01(x, m01, pieces=2):
    m = m01.astype(BF16)
    out = None
    for p in _split(x, pieces):
        t = lax.dot_general(p, m, _DN["nn"], preferred_element_type=F32)
        out = t if out is None else out + t
    return out


HI16 = -65536


def _bf16_bits(x):
    return lax.bitcast_convert_type(x.astype(BF16).astype(F32), I32)


def _unpack_pair(w):
    return lax.bitcast_convert_type(w << 16, F32), lax.bitcast_convert_type(w & HI16, F32)


def _rows_to_packed(ref, x, first=0):
    for c in range(PCH):
        lo = _bf16_bits(x[:, c * LANES:(c + 1) * LANES])
        hi = _bf16_bits(x[:, (c + PCH) * LANES:(c + PCH + 1) * LANES])
        ref[pl.ds(first * PCH + c, x.shape[0], stride=PCH), :] = ((lo >> 16) & 0xFFFF) | (hi & HI16)


def _rows_from_packed(ref, n, live=None, first=0):
    lows, highs = [], []
    for c in range(PCH):
        w = ref[pl.ds(first * PCH + c, n, stride=PCH), :]
        if live is not None:
            w = jnp.where(live, w, 0)
        lo, hi = _unpack_pair(w)
        lows.append(lo.astype(BF16))
        highs.append(hi.astype(BF16))
    return jnp.concatenate(lows + highs, axis=1)


def _slab(ref, row):
    return ref.at[pl.ds(pl.multiple_of(row * PCH, PCH), PCH)]


def _fslab(ref, row):
    return ref.at[pl.ds(pl.multiple_of(row * CHUNKS, CHUNKS), CHUNKS)]


def _sigmoid(x):
    return 1.0 / (1.0 + jnp.exp(-x))


def _softplus(x):
    return jnp.maximum(x, 0.0) + jnp.log(1.0 + jnp.exp(-jnp.abs(x)))


def _log2(n):
    assert n > 0 and n & (n - 1) == 0, n
    return n.bit_length() - 1


def _cparams(sem, vmem=None):
    return pltpu.CompilerParams(dimension_semantics=sem, vmem_limit_bytes=vmem)


def _mod_body(c_ref, w_ref, b_ref, o_ref):
    c = c_ref[...]
    o_ref[0] = _mm(c * _sigmoid(c), w_ref[...], passes=6) + b_ref[...]


def _mod_call(c_all, ada_w, ada_b):
    bt, d = c_all.shape
    out = pl.pallas_call(
        _mod_body,
        grid=(6,),
        in_specs=[pl.BlockSpec((bt, d), lambda k: (0, 0)),
                  pl.BlockSpec((d, d), lambda k: (0, k)),
                  pl.BlockSpec((1, d), lambda k: (0, k))],
        out_specs=pl.BlockSpec((1, bt, d), lambda k: (k, 0, 0)),
        out_shape=jax.ShapeDtypeStruct((6, bt, d), F32),
        compiler_params=_cparams(("arbitrary",)),
        name="adaln_mod",
    )(c_all, ada_w, ada_b.reshape(1, 6 * d))
    return jnp.transpose(out, (1, 0, 2))


def _inproj_body(x_ref, mod_ref, g_ref, w_ref, pa_ref, qkv_ref, xal_ref, gg_ref, mg_ref):
    bb, ll, d = x_ref.shape
    x = x_ref[...]
    y = x * lax.rsqrt(jnp.mean(x * x, axis=-1, keepdims=True) + NORM_EPS) * g_ref[...]
    h = y * (1.0 + mod_ref[:, 1:2, :]) + mod_ref[:, 0:1, :]
    hb = h.reshape(bb * ll, d).astype(BF16)
    for ref, off in zip((pa_ref, qkv_ref, xal_ref, gg_ref, mg_ref), PACK_OFFS):
        w = ref.shape[-1]
        ref[...] = jnp.dot(hb, w_ref[:, off:off + w], preferred_element_type=F32).reshape(bb, ll, w)


def _tile(bn, seq, tile):
    if seq >= tile:
        assert seq % tile == 0
        return 1, tile
    assert tile % seq == 0 and bn % (tile // seq) == 0
    return tile // seq, seq


def _inproj_call(x, mod, norm_g, w_pack):
    bn, seq, d = x.shape
    bb, ll = _tile(bn, seq, TOK_TILE)
    tok = lambda w: pl.BlockSpec((bb, ll, w), lambda b, l: (b, l, 0))
    widths = (PA_W, QKV_W, XAL_W, GG_W, MG_W)
    return pl.pallas_call(
        _inproj_body,
        grid=(bn // bb, seq // ll),
        in_specs=[tok(d),
                  pl.BlockSpec((bb, 6, d), lambda b, l: (b, 0, 0)),
                  pl.BlockSpec((1, 1, d), lambda b, l: (0, 0, 0)),
                  pl.BlockSpec((d, PACK_W), lambda b, l: (0, 0))],
        out_specs=[tok(w) for w in widths],
        out_shape=[jax.ShapeDtypeStruct((bn, seq, w), F32) for w in widths],
        compiler_params=_cparams(("arbitrary", "arbitrary"), VMEM_LIMIT),
        name="norm_inproj",
    )(x, mod, norm_g.reshape(1, 1, d), w_pack)


def _rwprep_body(pa_ref, sh_ref, mu_ref, w0_ref, wup_ref, a0_ref, aup_ref, gup_ref, kk_ref, ka_ref, rk_ref,
                 bd_ref, r_o, lw_o, k_o, v_o, a_o, b_o, g_o, bon_o, nsh_o, carry):
    bb, ll, wd = pa_ref.shape
    n = bb * ll
    hw = RW_WIDTH

    @pl.when(pl.program_id(1) == 0)
    def _():
        carry[...] = sh_ref[...]

    pa = pa_ref[...]
    rolled = pltpu.roll(pa.reshape(n, wd), 1, 0).reshape(bb, ll, wd)
    tok = lax.broadcasted_iota(I32, (bb, ll, wd), 1)
    prev = jnp.where(tok == 0, carry[...], rolled)
    last = pa_ref[:, ll - 1:ll, :]
    carry[...] = last
    nsh_o[...] = last
    xs = (pa + (prev - pa) * mu_ref[...]).reshape(n, wd)

    r, k, v = xs[:, 0:hw], xs[:, hw:2 * hw], xs[:, 2 * hw:3 * hw]
    xwa = xs[:, 3 * hw:3 * hw + LANES]
    xg = xs[:, 3 * hw + LANES:]
    w_log = -_softplus(-(w0_ref[...] + _mm(jnp.tanh(xwa), wup_ref[...], passes=3))) - 0.5
    lw = -jnp.exp(w_log)
    a = _sigmoid(a0_ref[...] + _mm(xwa, aup_ref[...], passes=3))
    g = _mm(_sigmoid(xg), gup_ref[...])
    bd = bd_ref[...]
    kkv = k * kk_ref[...]
    kkn = kkv * lax.rsqrt(jnp.maximum(_xmm01(kkv * kkv, bd), 1e-24))
    k2 = k * (1.0 + (a - 1.0) * ka_ref[...])
    bonus = _xmm01(r * k2 * rk_ref[...], bd) * v
    for ref, val in ((r_o, r), (lw_o, lw), (k_o, k2), (v_o, v), (a_o, -kkn), (b_o, kkn * a), (g_o, g),
                     (bon_o, bonus)):
        ref[...] = val.reshape(bb, ll, hw)


def _rwprep_call(pa, s_sh, p):
    bn, seq, wd = pa.shape
    bb, ll = _tile(bn, seq, TOK_TILE)
    hw = RW_WIDTH
    tok = lambda w: pl.BlockSpec((bb, ll, w), lambda b, l: (b, l, 0))
    row = lambda w: pl.BlockSpec((bb, 1, w), lambda b, l: (b, 0, 0))
    full = lambda a: pl.BlockSpec(a.shape, lambda b, l: (0,) * a.ndim)
    consts = (p["mu"], p["w0"], p["wup"], p["a0"], p["aup"], p["gup"], p["kk"], p["ka"], p["rk"], p["bd64"])
    outs = pl.pallas_call(
        _rwprep_body,
        grid=(bn // bb, seq // ll),
        in_specs=[tok(wd), row(wd)] + [full(c) for c in consts],
        out_specs=[tok(hw)] * 8 + [row(wd)],
        out_shape=[jax.ShapeDtypeStruct((bn, seq, hw), F32)] * 8 + [jax.ShapeDtypeStruct((bn, 1, wd), F32)],
        scratch_shapes=[pltpu.VMEM((bb, 1, wd), F32)],
        compiler_params=_cparams(("arbitrary", "arbitrary"), VMEM_LIMIT),
        name="rwkv_prep",
    )(pa, s_sh.reshape(bn, 1, wd), *consts)
    return outs


def _unit_masks(n, tl):
    ri = lax.broadcasted_iota(I32, (n, n), 0)
    ci = lax.broadcasted_iota(I32, (n, n), 1)
    same = (ri >> _log2(tl)) == (ci >> _log2(tl))
    return same, same & (ri > ci), same & (ri >= ci)


def _rwscan_body(r_ref, lw_ref, k_ref, v_ref, a_ref, b_ref, s0_ref, y_ref, sn_ref, st, *, nu, nseq, tl, passes):
    n = nseq * tl
    n2 = 2 * n
    p_aa, p_inv, p_apply, p_state, p_y = passes

    hd = RW_HEAD

    @pl.when(pl.program_id(1) == 0)
    def _():
        zero = jnp.zeros((hd, hd), F32)
        for q in range(nu * nseq):
            for p in range(RW_HEADS // 2):
                st[q, p] = jnp.concatenate(
                    [jnp.concatenate([s0_ref[q, 2 * p], zero], axis=1),
                     jnp.concatenate([zero, s0_ref[q, 2 * p + 1]], axis=1)], axis=0)

    same, _, incl = _unit_masks(n, tl)
    m_cum = jnp.where(incl, 1.0, 0.0)
    m_seq = jnp.where(same, 1.0, 0.0)
    ri = lax.broadcasted_iota(I32, (n2, n2), 0)
    ci = lax.broadcasted_iota(I32, (n2, n2), 1)
    rt, ct = ri & (n - 1), ci & (n - 1)
    dsame = ((rt >> _log2(tl)) == (ct >> _log2(tl))) & ((ri >> _log2(n)) == (ci >> _log2(n)))
    strict_d = dsame & (rt > ct)
    incl_d = dsame & (rt >= ct)
    eye_d = jnp.where(ri == ci, 1.0, 0.0)
    lane = lax.broadcasted_iota(I32, (1, LANES), 1)
    m0 = jnp.where(lane < RW_HEAD, 1.0, 0.0)
    m1 = 1.0 - m0

    def dup(x):
        return jnp.concatenate([x * m0, x * m1], axis=0)

    def seq_rows(x, q):
        if nseq == 1:
            return x
        return jnp.concatenate([x[q * tl:(q + 1) * tl], x[n + q * tl:n + (q + 1) * tl]], axis=0)

    def unit_rows(parts):
        if nseq == 1:
            return parts[0]
        return jnp.concatenate([p[0:tl] for p in parts] + [p[tl:2 * tl] for p in parts], axis=0)

    chains = [(u, p) for u in range(nu) for p in range(RW_HEADS // 2)]
    ids = range(len(chains))
    cat0 = lambda *xs: jnp.concatenate(xs, axis=0)

    def ld(ref, c):
        u, p = chains[c]
        return ref[u * nseq:(u + 1) * nseq, :, p * LANES:(p + 1) * LANES].reshape(n, LANES)

    lw = [ld(lw_ref, c) for c in ids]
    cum = [_mm01(m_cum, x) for x in lw]
    tot = [_mm01(m_seq, x) for x in lw]
    e_c = [jnp.exp(x) for x in cum]
    e_n = [jnp.exp(-x) for x in cum]
    e_l = [jnp.exp(t - x) for t, x in zip(tot, cum)]
    at_d = [dup(ld(a_ref, c) * jnp.exp(cum[c] - lw[c])) for c in ids]
    rt_d = [dup(ld(r_ref, c) * e_c[c]) for c in ids]
    bt_d = [dup(ld(b_ref, c) * e_n[c]) for c in ids]
    kt_d = [dup(ld(k_ref, c) * e_n[c]) for c in ids]
    bh_d = [dup(ld(b_ref, c) * e_l[c]) for c in ids]
    kh_d = [dup(ld(k_ref, c) * e_l[c]) for c in ids]
    v_d = [dup(ld(v_ref, c)) for c in ids]
    aa = [_mm(cat0(at_d[c], rt_d[c]), cat0(bt_d[c], kt_d[c]), "nt", p_aa) for c in ids]
    a_ab = [jnp.where(strict_d, x[0:n2, 0:n2], 0.0) for x in aa]
    a_ak = [jnp.where(strict_d, x[0:n2, n2:], 0.0) for x in aa]
    a_rb = [jnp.where(incl_d, x[n2:, 0:n2], 0.0) for x in aa]
    a_rk = [jnp.where(incl_d, x[n2:, n2:], 0.0) for x in aa]
    zy = [_mm(cat0(a_ak[c], a_rk[c]), v_d[c], passes=p_apply) for c in ids]
    tinv = [eye_d + x for x in a_ab]
    nk = a_ab
    for _ in range(_log2(tl) - 1):
        nk = [_mm(x, x, passes=p_inv) for x in nk]
        tinv = [t + _mm(t, x, passes=p_inv) for t, x in zip(tinv, nk)]
    wu = [_mm(tinv[c], jnp.concatenate([at_d[c], zy[c][0:n2]], axis=1), passes=p_apply) for c in ids]
    seqs = range(nseq)
    srow = lambda c, q: (chains[c][0] * nseq + q, chains[c][1])
    s_old = [[st[srow(c, q)] for q in seqs] for c in ids]
    xs = [[_mm(cat0(seq_rows(wu[c][:, 0:LANES], q), seq_rows(rt_d[c], q)), s_old[c][q], "nt", p_state)
           for q in seqs] for c in ids]
    u_q = [[xs[c][q][0:2 * tl] + seq_rows(wu[c][:, LANES:], q) for q in seqs] for c in ids]
    for c in ids:
        for q in seqs:
            g_c = jnp.exp(tot[c][q * tl:q * tl + 1, :])
            st[srow(c, q)] = s_old[c][q] * g_c + _mm(cat0(u_q[c][q], seq_rows(v_d[c], q)),
                                                     cat0(seq_rows(bh_d[c], q), seq_rows(kh_d[c], q)), "tn", p_state)
    for c in ids:
        u, p = chains[c]
        y_d = (unit_rows([xs[c][q][2 * tl:] for q in seqs]) + _mm(a_rb[c], unit_rows(u_q[c]), passes=p_y)
               + zy[c][n2:])
        y_ref[u * nseq:(u + 1) * nseq, :, p * LANES:(p + 1) * LANES] = (y_d[0:n] + y_d[n:]).reshape(nseq, tl, LANES)

    @pl.when(pl.program_id(1) == pl.num_programs(1) - 1)
    def _():
        for q in range(nu * nseq):
            for p in range(RW_HEADS // 2):
                s = st[q, p]
                sn_ref[q, 2 * p] = s[0:hd, 0:hd]
                sn_ref[q, 2 * p + 1] = s[hd:, hd:]


def _unit_shape(bn, seq):
    if seq >= UNIT:
        assert seq % UNIT == 0
        return 1, UNIT
    assert UNIT % seq == 0 and bn % (UNIT // seq) == 0
    return UNIT // seq, seq


def _rwscan_call(r, lw, k2, v, a_s, b_s, s0, passes=RW_SCAN_PASSES):
    bn, seq, hw = r.shape
    nseq, tl = _unit_shape(bn, seq)
    nu = RW_UNITS_PER_STEP if bn % (RW_UNITS_PER_STEP * nseq) == 0 else 1
    rows = nu * nseq
    tok = pl.BlockSpec((rows, tl, hw), lambda b, c: (b, c, 0))
    stt = pl.BlockSpec((rows, RW_HEADS, RW_HEAD, RW_HEAD), lambda b, c: (b, 0, 0, 0))
    return pl.pallas_call(
        functools.partial(_rwscan_body, nu=nu, nseq=nseq, tl=tl, passes=passes),
        grid=(bn // rows, seq // tl),
        in_specs=[tok] * 6 + [stt],
        out_specs=[tok, stt],
        out_shape=[jax.ShapeDtypeStruct((bn, seq, hw), F32), jax.ShapeDtypeStruct(s0.shape, F32)],
        scratch_shapes=[pltpu.VMEM((rows, RW_HEADS // 2, LANES, LANES), F32)],
        compiler_params=_cparams(("arbitrary", "arbitrary"), VMEM_LIMIT),
        name="rwkv_scan",
    )(r, lw, k2, v, a_s, b_s, s0)


def _gla_body(qkv_ref, xal_ref, gate_ref, aup_ref, ab_ref, ng_ref, s0_ref, o_ref, sn_ref, st, *, nu, nseq, tl, cs):
    n = nseq * tl
    n2 = 2 * n
    nsub = tl // cs

    @pl.when(pl.program_id(1) == 0)
    def _():
        zero = jnp.zeros((GLA_DV, GLA_DK), F32)
        for q in range(nu * nseq):
            for p in range(GLA_HEADS // 2):
                st[q, p] = jnp.concatenate(
                    [jnp.concatenate([s0_ref[q, 2 * p].T, zero], axis=1),
                     jnp.concatenate([zero, s0_ref[q, 2 * p + 1].T], axis=1)], axis=0)

    same, _, incl = _unit_masks(n, cs)
    m_cum = jnp.where(incl, 1.0, 0.0)
    m_sub = jnp.where(same, 1.0, 0.0)
    ri = lax.broadcasted_iota(I32, (n2, n2), 0)
    ci = lax.broadcasted_iota(I32, (n2, n2), 1)
    rt, ct = ri & (n - 1), ci & (n - 1)
    causal_d = ((rt >> _log2(cs)) == (ct >> _log2(cs))) & ((ri >> _log2(n)) == (ci >> _log2(n))) & (rt >= ct)
    lane = lax.broadcasted_iota(I32, (1, LANES), 1)
    m0 = jnp.where(lane < GLA_DK, 1.0, 0.0)
    m1 = 1.0 - m0
    sr = lax.broadcasted_iota(I32, (2 * GLA_DV, LANES), 0)
    sc = lax.broadcasted_iota(I32, (2 * GLA_DV, LANES), 1)
    st_mask = jnp.where((sr >> _log2(GLA_DV)) == (sc >> _log2(GLA_DK)), 1.0, 0.0)

    def dup(x):
        return jnp.concatenate([x * m0, x * m1], axis=0)

    chains = [(u, p) for u in range(nu) for p in range(GLA_HEADS // 2)]
    ids = range(len(chains))
    urows = lambda u: slice(u * nseq, (u + 1) * nseq)
    ng = ng_ref[...]
    la_all = [-_softplus(-(_mm(xal_ref[urows(u), :, :].reshape(n, LANES), aup_ref[...], passes=3) + ab_ref[...]))
              * (1.0 / GLA_GATE_TAU) for u in range(nu)]

    def ld(ref, c, off, width):
        return ref[urows(chains[c][0]), :, off:off + width].reshape(n, width)

    q = [ld(qkv_ref, c, chains[c][1] * LANES, LANES) * (GLA_DK ** -0.5) for c in ids]
    k = [ld(qkv_ref, c, GLA_KW + chains[c][1] * LANES, LANES) for c in ids]
    vp = [ld(qkv_ref, c, 2 * GLA_KW + chains[c][1] * 2 * GLA_DV, 2 * GLA_DV) for c in ids]
    la = [la_all[u][:, p * LANES:(p + 1) * LANES] for u, p in chains]
    bc = [_mm01(m_cum, x) for x in la]
    bl = [_mm01(m_sub, x) for x in la]
    qe = [q[c] * jnp.exp(bc[c]) for c in ids]
    ke = [k[c] * jnp.exp(-bc[c]) for c in ids]
    kd = [k[c] * jnp.exp(bl[c] - bc[c]) for c in ids]
    att = [jnp.where(causal_d, _mm(dup(qe[c]), dup(ke[c]), "nt", passes=1), 0.0) for c in ids]
    v_st = [jnp.concatenate([x[:, 0:GLA_DV], x[:, GLA_DV:]], axis=0) for x in vp]
    o_st = [_mm(att[c], v_st[c], passes=1) for c in ids]
    upd = [[_mm(vp[c][r0:r0 + cs], kd[c][r0:r0 + cs], "tn", passes=1) for r0 in range(0, n, cs)] for c in ids]
    inter = [[None] * (n // cs) for _ in ids]
    for sq in range(nseq):
        s = [st[chains[c][0] * nseq + sq, chains[c][1]] for c in ids]
        for j in range(nsub):
            i = sq * nsub + j
            r0 = i * cs
            for c in ids:
                inter[c][i] = _mm(qe[c][r0:r0 + cs], s[c], "nt", passes=1)
                s[c] = s[c] * jnp.exp(bl[c][r0:r0 + 1, :]) + st_mask * upd[c][i]
        for c in ids:
            st[chains[c][0] * nseq + sq, chains[c][1]] = s[c]
    for c in ids:
        u, p = chains[c]
        o = o_st[c] + jnp.concatenate([x[:, 0:GLA_DV] for x in inter[c]] + [x[:, GLA_DV:] for x in inter[c]], axis=0)
        o = o * lax.rsqrt(jnp.mean(o * o, axis=-1, keepdims=True) + NORM_EPS) * ng
        goff = p * 2 * GLA_DV
        gp = ld(gate_ref, c, goff, 2 * GLA_DV)
        g_st = jnp.concatenate([gp[:, 0:GLA_DV], gp[:, GLA_DV:]], axis=0)
        ob = o * (g_st * _sigmoid(g_st))
        o_ref[urows(u), :, goff:goff + GLA_DV] = ob[0:n].reshape(nseq, tl, GLA_DV)
        o_ref[urows(u), :, goff + GLA_DV:goff + 2 * GLA_DV] = ob[n:].reshape(nseq, tl, GLA_DV)

    @pl.when(pl.program_id(1) == pl.num_programs(1) - 1)
    def _():
        for q in range(nu * nseq):
            for p in range(GLA_HEADS // 2):
                s = st[q, p]
                sn_ref[q, 2 * p] = s[0:GLA_DV, 0:GLA_DK].T
                sn_ref[q, 2 * p + 1] = s[GLA_DV:, GLA_DK:].T


def _gla_call(qkv, xal, gate, s0, p):
    bn, seq, _ = qkv.shape
    nseq, tl = _unit_shape(bn, seq)
    cs = min(GLA_CHUNK, seq)
    assert tl % cs == 0
    nu = GLA_UNITS_PER_STEP if bn % (GLA_UNITS_PER_STEP * nseq) == 0 else 1
    rows = nu * nseq
    tok = lambda w: pl.BlockSpec((rows, tl, w), lambda b, c: (b, c, 0))
    full = lambda a: pl.BlockSpec(a.shape, lambda b, c: (0,) * a.ndim)
    stt = pl.BlockSpec((rows, GLA_HEADS, GLA_DK, GLA_DV), lambda b, c: (b, 0, 0, 0))
    consts = (p["gla_aup"], p["gla_ab"], p["gla_ng"])
    return pl.pallas_call(
        functools.partial(_gla_body, nu=nu, nseq=nseq, tl=tl, cs=cs),
        grid=(bn // rows, seq // tl),
        in_specs=[tok(QKV_W), tok(XAL_W), tok(GG_W)] + [full(c) for c in consts] + [stt],
        out_specs=[tok(GLA_VW), stt],
        out_shape=[jax.ShapeDtypeStruct((bn, seq, GLA_VW), F32), jax.ShapeDtypeStruct(s0.shape, F32)],
        scratch_shapes=[pltpu.VMEM((rows, GLA_HEADS // 2, 2 * GLA_DV, LANES), F32)],
        compiler_params=_cparams(("arbitrary", "arbitrary"), VMEM_LIMIT),
        name="gla_chunked",
    )(qkv, xal, gate, *consts, s0)


def _merge_body(y_ref, g_ref, bon_ref, ob_ref, mg_ref, x_ref, mod_ref, gng_ref, gnb_ref, bd_ref, wpa_ref,
                wpb_ref, wout_ref, n2_ref, rwh_ref, rwl_ref, x1_o, h2_o, lg_o):
    bb, ll, d = x_ref.shape
    n = bb * ll
    hw = RW_WIDTH
    bd = bd_ref[...]
    y = y_ref[...].reshape(n, hw)
    mu = _xmm01(y, bd, pieces=3) * (1.0 / RW_HEAD)
    dv = y - mu
    var = _xmm01(dv * dv, bd) * (1.0 / RW_HEAD)
    yn = dv * lax.rsqrt(var + RW_GN_EPS) * gng_ref[...] + gnb_ref[...]
    o_a = (yn + bon_ref[...].reshape(n, hw)) * g_ref[...].reshape(n, hw)
    o_b = ob_ref[...].reshape(n, GLA_VW)
    mg = mg_ref[...].reshape(n, 2 * d)
    merged = _sigmoid(mg[:, 0:d]) * _mm(o_a, wpa_ref[...]) + _sigmoid(mg[:, d:]) * _mm(o_b, wpb_ref[...])
    mix = _mm(merged, wout_ref[...]).reshape(bb, ll, d)
    x1 = x_ref[...] + mod_ref[:, 2:3, :] * mix
    x1_o[...] = x1
    yn2 = x1 * lax.rsqrt(jnp.mean(x1 * x1, axis=-1, keepdims=True) + NORM_EPS) * n2_ref[...]
    h2 = (yn2 * (1.0 + mod_ref[:, 4:5, :]) + mod_ref[:, 3:4, :]).reshape(n, d)
    hh, hl = _split(h2, 2)
    rwh, rwl = rwh_ref[...], rwl_ref[...]
    nt = lambda a, b: lax.dot_general(a, b, _DN["nt"], preferred_element_type=F32)
    lg_o[...] = nt(rwh, hh) + nt(rwl, hh) + nt(rwh, hl)
    _rows_to_packed(h2_o, h2)


def _merge_call(y, g, bonus, o_b, mg, x, mod, p):
    bn, seq, d = x.shape
    bb, ll = _tile(bn, seq, TOK_TILE)
    nl = seq // ll
    tn = bn * seq
    tok = lambda w: pl.BlockSpec((bb, ll, w), lambda b, l: (b, l, 0))
    full = lambda a: pl.BlockSpec(a.shape, lambda b, l: (0,) * a.ndim)
    consts = (p["gn_g"], p["gn_b"], p["bd64"], p["w_pa"], p["w_pb"], p["w_out"], p["norm2_g"], p["rw_hi"],
              p["rw_lo"])
    return pl.pallas_call(
        _merge_body,
        grid=(bn // bb, nl),
        in_specs=[tok(RW_WIDTH)] * 3 + [tok(GLA_VW), tok(MG_W), tok(d),
                                        pl.BlockSpec((bb, 6, d), lambda b, l: (b, 0, 0))] + [full(c) for c in consts],
        out_specs=[tok(d),
                   pl.BlockSpec((bb * ll * PCH, LANES), lambda b, l: (b * nl + l, 0)),
                   pl.BlockSpec((N_EXPERTS, bb * ll), lambda b, l: (0, b * nl + l))],
        out_shape=[jax.ShapeDtypeStruct((bn, seq, d), F32),
                   jax.ShapeDtypeStruct((tn * PCH, LANES), I32),
                   jax.ShapeDtypeStruct((N_EXPERTS, tn), F32)],
        compiler_params=_cparams(("arbitrary", "arbitrary"), VMEM_LIMIT),
        name="merge_outproj_router",
    )(y, g, bonus, o_b, mg, x, mod, *consts)


def _route_body(lg_ref, rb_ref, e_o, rk_o, w_o, cnt_o, carry):
    ne, tm = lg_ref.shape

    @pl.when(pl.program_id(0) == 0)
    def _():
        carry[...] = jnp.zeros_like(carry)

    neg = -jnp.inf
    scores = _sigmoid(lg_ref[...])
    sel = scores + rb_ref[...]
    row_i = lax.broadcasted_iota(I32, (ne, tm), 0)
    row = row_i.astype(F32)
    grp = (row_i >> _log2(GROUP_SIZE)).astype(F32)

    def first_max(x, ids, none):
        m = jnp.max(x, axis=0, keepdims=True)
        return m, jnp.min(jnp.where(x == m, ids, none), axis=0, keepdims=True)

    gs = []
    gids = lax.broadcasted_iota(I32, (GROUP_SIZE, tm), 0)
    for gidx in range(N_GROUPS):
        rows = slice(gidx * GROUP_SIZE, (gidx + 1) * GROUP_SIZE)
        sg = _sigmoid(lg_ref[rows, :]) + rb_ref[rows, :]
        ids = (gids + gidx * GROUP_SIZE).astype(F32)
        m1, i1 = first_max(sg, ids, float(ne))
        gs.append(m1 + jnp.max(jnp.where(ids == i1, neg, sg), axis=0, keepdims=True))
    gs = jnp.concatenate(gs, axis=0)
    gid = lax.broadcasted_iota(I32, (N_GROUPS, tm), 0).astype(F32)
    cur = jnp.full((ne, tm), neg, F32)
    for _ in range(TOPK_GROUPS):
        _, gi = first_max(gs, gid, float(N_GROUPS))
        cur = jnp.where(grp == gi, sel, cur)
        gs = jnp.where(gid == gi, neg, gs)

    pm = jnp.zeros((ne, tm), F32)
    eidx, wts = [], []
    for _ in range(TOP_K):
        _, ei = first_max(cur, row, float(ne))
        hit = row == ei
        pm = jnp.where(hit, 1.0, pm)
        eidx.append(ei)
        wts.append(jnp.sum(jnp.where(hit, scores, 0.0), axis=0, keepdims=True))
        cur = jnp.where(hit, neg, cur)
    wsum = wts[0]
    for w in wts[1:]:
        wsum = wsum + w

    ri = lax.broadcasted_iota(I32, (tm, tm), 0)
    ci = lax.broadcasted_iota(I32, (tm, tm), 1)
    earlier = jnp.where(ri < ci, 1.0, 0.0)
    rank = _mm(pm, earlier, passes=1) + carry[...]
    carry[...] = carry[...] + jnp.sum(pm, axis=1, keepdims=True)
    cnt_o[...] = carry[...]

    rks = [jnp.sum(jnp.where(row == e, rank, 0.0), axis=0, keepdims=True) for e in eidx]
    e_o[0] = jnp.concatenate(eidx, axis=0).astype(I32)
    rk_o[0] = jnp.concatenate(rks, axis=0).astype(I32)
    w_o[0] = jnp.concatenate([w / wsum * ROUTED_SCALE for w in wts], axis=0)


def _route_call(logits_t, router_b):
    ne, tn = logits_t.shape
    tm = TOK_TILE
    assert tn % tm == 0
    col = pl.BlockSpec((ne, 1), lambda i: (0, 0))
    tab = pl.BlockSpec((1, TOP_K, tm), lambda i: (i, 0, 0))
    tab_shape = (tn // tm, TOP_K, tm)
    return pl.pallas_call(
        _route_body,
        grid=(tn // tm,),
        in_specs=[pl.BlockSpec((ne, tm), lambda i: (0, i)), col],
        out_specs=[tab, tab, tab, col],
        out_shape=[jax.ShapeDtypeStruct(tab_shape, I32), jax.ShapeDtypeStruct(tab_shape, I32),
                   jax.ShapeDtypeStruct(tab_shape, F32), jax.ShapeDtypeStruct((ne, 1), F32)],
        scratch_shapes=[pltpu.VMEM((ne, 1), F32)],
        compiler_params=_cparams(("arbitrary",)),
        name="moe_route",
    )(logits_t, router_b.reshape(ne, 1))


def _dest_body(e_ref, rk_ref, ps_ref, d_o):
    ne, tm = ps_ref.shape[0], e_ref.shape[2]
    ids = lax.broadcasted_iota(I32, (ne, tm), 0)
    ps = ps_ref[...]
    first = [jnp.sum(jnp.where(ids == e_ref[0, kk:kk + 1, :], ps, 0.0), axis=0, keepdims=True)
             for kk in range(TOP_K)]
    d_o[0] = (jnp.concatenate(first, axis=0).astype(I32) + rk_ref[0]) * PCH


def _dest_call(eidx, rank, pad_start):
    nt, _, tm = eidx.shape
    ne = pad_start.shape[0]
    tab = pl.BlockSpec((1, TOP_K, tm), lambda i: (i, 0, 0))
    return pl.pallas_call(
        _dest_body,
        grid=(nt,),
        in_specs=[tab, tab, pl.BlockSpec((ne, 1), lambda i: (0, 0))],
        out_specs=tab,
        out_shape=jax.ShapeDtypeStruct(eidx.shape, I32),
        compiler_params=_cparams(("arbitrary",)),
        name="moe_dest",
    )(eidx, rank, pad_start.astype(F32).reshape(ne, 1))


def _pslab(ref, offset):
    return ref.at[pl.ds(pl.multiple_of(offset, PCH), PCH)]


def _dispatch_body(d_ref, h2_ref, xs_hbm, sem, *, tm):
    def issue(m, carry):
        for kk in range(TOP_K):
            pltpu.make_async_copy(_slab(h2_ref, m), _pslab(xs_hbm, d_ref[0, kk, m]), sem).start(priority=kk % 2)
        return carry

    lax.fori_loop(0, tm, issue, 0)
    all_rows = xs_hbm.at[pl.ds(0, tm * TOP_K * PCH)]
    pltpu.make_async_copy(all_rows, all_rows, sem).wait()


def _assign_spec(tm, index_map):
    return pl.BlockSpec((1, TOP_K, tm), index_map, memory_space=pltpu.SMEM)


def _dispatch_call(dest, h2s, n_rows):
    tn = h2s.shape[0] // PCH
    tm = TOK_TILE
    assert dest.shape == (tn // tm, TOP_K, tm)
    blk = _assign_spec(tm, lambda i: (i, 0, 0))
    return pl.pallas_call(
        functools.partial(_dispatch_body, tm=tm),
        grid=(tn // tm,),
        in_specs=[blk, pl.BlockSpec((tm * PCH, LANES), lambda i: (i, 0))],
        out_specs=pl.BlockSpec(memory_space=pl.ANY),
        out_shape=jax.ShapeDtypeStruct((n_rows * PCH, LANES), I32),
        scratch_shapes=[pltpu.SemaphoreType.DMA],
        compiler_params=_cparams(("arbitrary",)),
        name="moe_dispatch",
    )(dest, h2s)


def _expert_body(bi_ref, nr_ref, ld_ref, nx_ref, xs_hbm, wg_hbm, wu_hbm, wd_hbm, ob_ref, wg_buf, wu_buf, wd_buf,
                 wg_bf, wu_bf, wd_bf, xbuf, sem, xsem):
    i = pl.program_id(0)
    nsteps = pl.num_programs(0)
    nr = nr_ref[i]
    slot = ld_ref[i]
    blk_rows = MOE_BLK * PCH

    def row_block(j):
        s = lax.rem(j, ROW_SLOTS)
        src = xs_hbm.at[pl.ds(pl.multiple_of(bi_ref[j] * blk_rows, blk_rows), blk_rows)]
        return pltpu.make_async_copy(src, xbuf.at[s], xsem.at[s])

    @pl.when(i == 0)
    def _():
        for j in range(ROW_SLOTS - 1):
            row_block(j).start()

    @pl.when(i + ROW_SLOTS - 1 < nsteps)
    def _():
        row_block(i + ROW_SLOTS - 1).start()

    def fetch(e, s):
        return (pltpu.make_async_copy(wg_hbm.at[e], wg_buf.at[s], sem.at[s]),
                pltpu.make_async_copy(wu_hbm.at[e], wu_buf.at[s], sem.at[s]),
                pltpu.make_async_copy(wd_hbm.at[e], wd_buf.at[s], sem.at[s]))

    @pl.when(i == 0)
    def _():
        for s in range(WEIGHT_SLOTS - 1):
            e0 = nx_ref[nx_ref.shape[0] - (WEIGHT_SLOTS - 1) + s]

            @pl.when(e0 >= 0)
            def _():
                for cp in fetch(e0, s):
                    cp.start()

    @pl.when(slot >= 0)
    def _():
        for cp in fetch(0, slot):
            cp.wait()

        @pl.when(nx_ref[i] >= 0)
        def _():
            for cp in fetch(nx_ref[i], lax.rem(slot + WEIGHT_SLOTS - 1, WEIGHT_SLOTS)):
                cp.start()

        wg_bf[...] = wg_buf[slot].astype(BF16)
        wu_bf[...] = wu_buf[slot].astype(BF16)
        wd_bf[...] = wd_buf[slot].astype(BF16)

    row_block(i).wait()

    @pl.when(nr > 0)
    def _():
        part = MOE_BLK // EXPERT_PARTS
        firsts = [q * part for q in range(EXPERT_PARTS)]
        rid = lax.broadcasted_iota(I32, (part, LANES), 0)
        xs_ref = xbuf.at[lax.rem(i, ROW_SLOTS)]
        x = [_rows_from_packed(xs_ref, part, rid < nr - f, f) for f in firsts]
        hg = [jnp.dot(v, wg_bf[...], preferred_element_type=F32) for v in x]
        hu = [jnp.dot(v, wu_bf[...], preferred_element_type=F32) for v in x]
        hh = [(g * _sigmoid(g) * u).astype(BF16) for g, u in zip(hg, hu)]
        out = [jnp.dot(v, wd_bf[...], preferred_element_type=F32) for v in hh]
        for f, v in zip(firsts, out):
            _rows_to_packed(ob_ref, v, f)


def _expert_tables(counts, pad_start, pad_end, nb):
    ne = counts.shape[0]
    first_row = jnp.arange(nb, dtype=I32) * MOE_BLK
    block_e = jnp.minimum(jnp.sum(pad_end[None, :] <= first_row[:, None], axis=1), ne - 1).astype(I32)
    block_rows = jnp.clip(pad_start[block_e] + counts[block_e] - first_row, 0, MOE_BLK).astype(I32)
    block_i = jnp.minimum(jnp.arange(nb, dtype=I32), pad_end[-1] // MOE_BLK - 1).astype(I32)
    has = counts > 0
    ordinal = jnp.cumsum(has.astype(I32)) - 1
    ids = jnp.where(has, jnp.arange(ne, dtype=I32), ne)
    nxt = jnp.concatenate([lax.cummin(ids, reverse=True)[1:], jnp.full((1,), ne, I32)])
    nxt = jnp.concatenate([nxt, jnp.full((1,), ne, I32)])
    hop = lambda e, k: functools.reduce(lambda x, _: nxt[x], range(k), e)
    starts = (first_row == pad_start[block_e]) & (block_rows > 0)
    load_slot = jnp.where(starts, ordinal[block_e] % WEIGHT_SLOTS, -1).astype(I32)
    ahead = hop(block_e, WEIGHT_SLOTS - 1)
    first = jnp.min(ids)
    lead = jnp.stack([hop(first, k) for k in range(WEIGHT_SLOTS - 1)])
    next_e = jnp.concatenate([jnp.where(starts, ahead, ne), lead])
    next_e = jnp.where(next_e < ne, next_e, -1).astype(I32)
    return block_i, block_rows, load_slot, next_e


def _expert_call(tables, xs, wg, wu, wd):
    nb = xs.shape[0] // (MOE_BLK * PCH)
    assert nb >= ROW_SLOTS
    d, ff = wg.shape[1], wg.shape[2]
    rows = pl.BlockSpec((MOE_BLK * PCH, LANES), lambda i, bi, nr, ld, nx: (bi[i], 0))
    hbm = pl.BlockSpec(memory_space=pl.ANY)
    grid_spec = pltpu.PrefetchScalarGridSpec(
        num_scalar_prefetch=4,
        grid=(nb,),
        in_specs=[hbm, hbm, hbm, hbm],
        out_specs=rows,
        scratch_shapes=[pltpu.VMEM((WEIGHT_SLOTS, d, ff), F32), pltpu.VMEM((WEIGHT_SLOTS, d, ff), F32),
                        pltpu.VMEM((WEIGHT_SLOTS, ff, d), F32),
                        pltpu.VMEM((d, ff), BF16), pltpu.VMEM((d, ff), BF16), pltpu.VMEM((ff, d), BF16),
                        pltpu.VMEM((ROW_SLOTS, MOE_BLK * PCH, LANES), I32),
                        pltpu.SemaphoreType.DMA((WEIGHT_SLOTS,)), pltpu.SemaphoreType.DMA((ROW_SLOTS,))],
    )
    return pl.pallas_call(
        _expert_body,
        grid_spec=grid_spec,
        out_shape=jax.ShapeDtypeStruct(xs.shape, I32),
        compiler_params=_cparams(("arbitrary",), VMEM_LIMIT),
        name="moe_experts",
    )(*tables, xs, wg, wu, wd)


def _combine_body(d_ref, dn_ref, wt_ref, ob_hbm, h2_ref, x1_ref, mod_ref, sg_ref, su_ref,
                  sd_ref, fg_ref, out_ref, gbuf, rbuf, sem, *, tm, nl):
    bb, ll, d = x1_ref.shape
    step = pl.program_id(0) * nl + pl.program_id(1)
    last = pl.num_programs(0) * nl - 1
    slot = lax.rem(step, 2)

    def request(d_tab, m, s):
        for kk in range(TOP_K):
            pltpu.make_async_copy(_pslab(ob_hbm, d_tab[0, kk, m]), _slab(gbuf.at[s], m * TOP_K + kk),
                                  sem.at[s]).start(priority=kk % 2)

    def mix(m):
        rows = gbuf.at[slot]
        lo, hi = _unpack_pair(_slab(rows, m * TOP_K)[...])
        wt = wt_ref[0, 0, m]
        acc_lo, acc_hi = wt * lo, wt * hi
        for kk in range(1, TOP_K):
            lo, hi = _unpack_pair(_slab(rows, m * TOP_K + kk)[...])
            wt = wt_ref[0, kk, m]
            acc_lo, acc_hi = acc_lo + wt * lo, acc_hi + wt * hi
        _fslab(rbuf, m)[...] = jnp.concatenate([acc_lo, acc_hi], axis=0)

    @pl.when(step == 0)
    def _():
        def first(m, carry):
            request(d_ref, m, 0)
            return carry
        lax.fori_loop(0, tm, first, 0)

    pltpu.make_async_copy(ob_hbm.at[pl.ds(0, tm * TOP_K * PCH)], gbuf.at[slot], sem.at[slot]).wait()

    @pl.when(step < last)
    def _():
        def both(m, carry):
            request(dn_ref, m, 1 - slot)
            mix(m)
            return carry
        lax.fori_loop(0, tm, both, 0)

    @pl.when(step == last)
    def _():
        def only(m, carry):
            mix(m)
            return carry
        lax.fori_loop(0, tm, only, 0)

    routed = jnp.concatenate([rbuf[pl.ds(c, tm, stride=CHUNKS), :] for c in range(CHUNKS)], axis=1)
    h2 = _rows_from_packed(h2_ref, tm)
    hg = jnp.dot(h2, sg_ref[...], preferred_element_type=F32)
    hu = jnp.dot(h2, su_ref[...], preferred_element_type=F32)
    shared = jnp.dot((hg * _sigmoid(hg) * hu).astype(BF16), sd_ref[...], preferred_element_type=F32)
    ff = (routed + shared).reshape(bb, ll, d)
    x2 = x1_ref[...] + mod_ref[:, 5:6, :] * ff
    out_ref[...] = x2 * lax.rsqrt(jnp.mean(x2 * x2, axis=-1, keepdims=True) + NORM_EPS) * fg_ref[...]


def _combine_call(dest, wts, first_tok, ob, h2s, x1, mod, p):
    bn, seq, d = x1.shape
    tm = CMB_TILE
    bb, ll = _tile(bn, seq, tm)
    nl = seq // ll
    tn = bn * seq
    nsteps = tn // tm
    per = dest.shape[2] // tm
    assert first_tok % tm == 0 and dest.shape[2] % tm == 0
    tile = lambda g: ((first_tok // tm + g) // per, 0, (first_tok // tm + g) % per)
    smem = _assign_spec(tm, lambda b, l: tile(b * nl + l))
    smem_next = _assign_spec(tm, lambda b, l: tile(jnp.minimum(b * nl + l + 1, nsteps - 1)))
    tok = pl.BlockSpec((bb, ll, d), lambda b, l: (b, l, 0))
    full = lambda a: pl.BlockSpec(a.shape, lambda b, l: (0,) * a.ndim)
    consts = (p["sh_gate"], p["sh_up"], p["sh_down"], p["final_g"])
    return pl.pallas_call(
        functools.partial(_combine_body, tm=tm, nl=nl),
        grid=(bn // bb, nl),
        in_specs=[smem, smem_next, smem, pl.BlockSpec(memory_space=pl.ANY),
                  pl.BlockSpec((tm * PCH, LANES), lambda b, l: (b * nl + l, 0)),
                  tok, pl.BlockSpec((bb, 6, d), lambda b, l: (b, 0, 0))] + [full(c) for c in consts],
        out_specs=tok,
        out_shape=jax.ShapeDtypeStruct((bn, seq, d), F32),
        scratch_shapes=[pltpu.VMEM((2, tm * TOP_K * PCH, LANES), I32), pltpu.VMEM((tm * CHUNKS, LANES), F32),
                        pltpu.SemaphoreType.DMA((2,))],
        compiler_params=_cparams(("arbitrary", "arbitrary"), VMEM_LIMIT),
        name="moe_combine_final",
    )(dest, dest, wts, ob, h2s, x1, mod, *consts)


def _layer_params(l, ada_w, ada_b, norm1_g, norm2_g, w_in, mu_shift, rw_w0, rw_w_up, rw_a0, rw_a_up, rw_g_up,
                  rw_k_k, rw_k_a, rw_r_k, rw_gn_g, rw_gn_b, gla_a_up, gla_a_bias, gla_norm_g, w_pa, w_pb, w_out,
                  router_w, router_b, exp_gate, exp_up, exp_down, sh_gate, sh_up, sh_down):
    d = D_MODEL
    wi = w_in[l]
    gla0 = RW_SHIFT_COLS
    xal0 = gla0 + QKV_W
    pad = jnp.zeros((d, XAL_W - GLA_GATE_RANK), F32)
    w_pack = jnp.concatenate([wi[:, :xal0], wi[:, xal0:xal0 + GLA_GATE_RANK], pad,
                              wi[:, xal0 + GLA_GATE_RANK:]], axis=1).astype(BF16)
    zr = jnp.zeros((RW_W_RANK, RW_WIDTH), F32)
    hid = jnp.arange(RW_WIDTH) // RW_HEAD
    row = lambda a: a.reshape(1, -1)
    rw_t = router_w[l].T
    rw_hi = rw_t.astype(BF16)
    return dict(
        ada_w=ada_w[l], ada_b=ada_b[l], norm1_g=norm1_g[l], norm2_g=norm2_g[l].reshape(1, 1, d), w_pack=w_pack,
        mu=mu_shift[l].reshape(1, 1, -1), w0=row(rw_w0[l]), wup=jnp.concatenate([rw_w_up[l], zr], axis=0),
        a0=row(rw_a0[l]), aup=jnp.concatenate([zr, rw_a_up[l]], axis=0), gup=rw_g_up[l].astype(BF16),
        kk=row(rw_k_k[l]), ka=row(rw_k_a[l]), rk=row(rw_r_k[l]),
        bd64=(hid[:, None] == hid[None, :]).astype(BF16),
        gn_g=row(rw_gn_g[l]), gn_b=row(rw_gn_b[l]),
        gla_aup=jnp.concatenate([gla_a_up[l], jnp.zeros((XAL_W - GLA_GATE_RANK, GLA_KW), F32)], axis=0),
        gla_ab=row(gla_a_bias[l]), gla_ng=row(gla_norm_g[l]),
        w_pa=w_pa[l].astype(BF16), w_pb=w_pb[l].astype(BF16), w_out=w_out[l].astype(BF16),
        rw_hi=rw_hi, rw_lo=(rw_t - rw_hi.astype(F32)).astype(BF16), router_b=router_b[l],
        exp_gate=exp_gate[l], exp_up=exp_up[l], exp_down=exp_down[l],
        sh_gate=sh_gate[l].astype(BF16), sh_up=sh_up[l].astype(BF16), sh_down=sh_down[l].astype(BF16),
    )


def _mixer_group(x, mod, s_rw, s_sh, s_gla, p):
    pa, qkv, xal, gg, mg = _inproj_call(x, mod, p["norm1_g"], p["w_pack"])
    r, lw, k2, v, a_s, b_s, g, bonus, new_sh = _rwprep_call(pa, s_sh, p)
    y, rw_new = _rwscan_call(r, lw, k2, v, a_s, b_s, s_rw)
    o_b, gla_new = _gla_call(qkv, xal, gg, s_gla, p)
    x1, h2s, logits = _merge_call(y, g, bonus, o_b, mg, x, mod, p)
    states = (rw_new, new_sh[:, 0, :], gla_new)
    return x1, h2s, logits, states


def _moe(h2s, logits, p):
    tn = h2s.shape[0] // PCH
    eidx, rank, wts, counts = _route_call(logits, p["router_b"])
    counts = counts[:, 0].astype(I32)
    padded = (counts + MOE_BLK - 1) // MOE_BLK * MOE_BLK
    pad_end = jnp.cumsum(padded)
    pad_start = (pad_end - padded).astype(I32)
    nb = (tn * TOP_K + N_EXPERTS * (MOE_BLK - 1)) // MOE_BLK + 1
    tables = _expert_tables(counts, pad_start, pad_end, nb)
    dest = _dest_call(eidx, rank, pad_start)
    xs = _dispatch_call(dest, h2s, nb * MOE_BLK)
    ob = _expert_call(tables, xs, p["exp_gate"], p["exp_up"], p["exp_down"])
    return ob, dest, wts


def kernel(x_prompt, x_sample, c_prompt, c_sample, state_rwkv, state_shift, state_gla, ada_w, ada_b, norm1_g,
           norm2_g, w_in, mu_shift, rw_w0, rw_w_up, rw_a0, rw_a_up, rw_g_up, rw_k_k, rw_k_a, rw_r_k, rw_gn_g,
           rw_gn_b, gla_a_up, gla_a_bias, gla_norm_g, w_pa, w_pb, w_out, router_w, router_b, exp_gate, exp_up,
           exp_down, sh_gate, sh_up, sh_down, final_g):
    depth = ada_w.shape[0]
    bp, bs = x_prompt.shape[0], x_sample.shape[0]
    tp = bp * x_prompt.shape[1]
    xs_g = [x_prompt, x_sample]
    c_all = jnp.concatenate([c_prompt, c_sample], axis=0)
    zeros = lambda shape: jnp.zeros(shape, x_prompt.dtype)
    new_states = [[], []]
    fg = final_g.reshape(1, 1, D_MODEL)
    for l in range(depth):
        p = _layer_params(l, ada_w, ada_b, norm1_g, norm2_g, w_in, mu_shift, rw_w0, rw_w_up, rw_a0, rw_a_up,
                          rw_g_up, rw_k_k, rw_k_a, rw_r_k, rw_gn_g, rw_gn_b, gla_a_up, gla_a_bias, gla_norm_g,
                          w_pa, w_pb, w_out, router_w, router_b, exp_gate, exp_up, exp_down, sh_gate, sh_up,
                          sh_down)
        p["final_g"] = fg
        mod_all = _mod_call(c_all, p["ada_w"], p["ada_b"])
        mods = [mod_all[:bp], mod_all[bp:]]
        states_in = [
            (zeros((bp, RW_HEADS, RW_HEAD, RW_HEAD)), zeros((bp, RW_SHIFT_COLS)),
             zeros((bp, GLA_HEADS, GLA_DK, GLA_DV))),
            (state_rwkv[l], state_shift[l], state_gla[l]),
        ]
        x1s, h2ss, lgs = [], [], []
        for gi in range(2):
            x1, h2s, logits, st = _mixer_group(xs_g[gi], mods[gi], *states_in[gi], p)
            x1s.append(x1)
            h2ss.append(h2s)
            lgs.append(logits)
            new_states[gi].append(st)
        h2_all = jnp.concatenate(h2ss, axis=0)
        ob, dest, wts = _moe(h2_all, jnp.concatenate(lgs, axis=1), p)
        assert depth == 1, "the fused final norm assumes a single layer"
        xs_g = [
            _combine_call(dest, wts, 0, ob, h2ss[0], x1s[0], mods[0], p),
            _combine_call(dest, wts, tp, ob, h2ss[1], x1s[1], mods[1], p),
        ]
    stack = lambda gi, j: jnp.stack([s[j] for s in new_states[gi]])
    return (xs_g[0], xs_g[1], stack(0, 0), stack(0, 1), stack(0, 2), stack(1, 0), stack(1, 1), stack(1, 2))
```

```python
import functools

import jax
import jax.numpy as jnp
from jax import lax
from jax.experimental import pallas as pl
from jax.experimental.pallas import tpu as pltpu

F32, BF16, I32 = jnp.float32, jnp.bfloat16, jnp.int32

D_MODEL = 1024
RW_HEADS, RW_HEAD = 8, 64
RW_WIDTH = RW_HEADS * RW_HEAD
RW_W_RANK, RW_A_RANK, RW_G_RANK = 64, 64, 128
RW_GN_EPS = 64e-5
GLA_HEADS, GLA_DK, GLA_DV = 4, 64, 128
GLA_KW, GLA_VW = GLA_HEADS * GLA_DK, GLA_HEADS * GLA_DV
GLA_GATE_RANK = 16
GLA_GATE_TAU = 16.0
GLA_CHUNK = 16
RW_SHIFT_COLS = 3 * RW_WIDTH + RW_W_RANK + RW_A_RANK + RW_G_RANK
N_EXPERTS, TOP_K, N_GROUPS, TOPK_GROUPS = 256, 8, 8, 4
GROUP_SIZE = N_EXPERTS // N_GROUPS
EXPERT_FF = 256
ROUTED_SCALE = 2.5
NORM_EPS = 1e-6

LANES = 128
SUBLANES = 8
CHUNKS = D_MODEL // LANES
PCH = CHUNKS // 2
UNIT = 64
RW_SCAN_PASSES = (1, 1, 1, 1, 1)
GLA_UNITS_PER_STEP = 4
RW_UNITS_PER_STEP = 4
VMEM_LIMIT = 56 * 1024 * 1024

PA_W, QKV_W, XAL_W, GG_W, MG_W = RW_SHIFT_COLS, 2 * GLA_KW + GLA_VW, LANES, GLA_VW, 2 * D_MODEL
PACK_OFFS = (0, PA_W, PA_W + QKV_W, PA_W + QKV_W + XAL_W, PA_W + QKV_W + XAL_W + GG_W)
PACK_W = PA_W + QKV_W + XAL_W + GG_W + MG_W

TOK_TILE = 256
MOE_BLK = 256
EXPERT_PARTS = 1
WEIGHT_SLOTS = 3
ROW_SLOTS = 3
CMB_TILE = 128

_DN = {
    "nn": (((1,), (0,)), ((), ())),
    "nt": (((1,), (1,)), ((), ())),
    "tn": (((0,), (0,)), ((), ())),
}


def _split(x, pieces):
    out, rem = [], x
    for i in range(pieces):
        p = rem.astype(BF16)
        out.append(p)
        if i + 1 < pieces:
            rem = rem - p.astype(F32)
    return out


def _mm(a, b, form="nn", passes=1):
    dn = _DN[form]
    if passes == 6:
        return lax.dot_general(a.astype(F32), b.astype(F32), dn, precision=lax.Precision.HIGHEST,
                               preferred_element_type=F32)
    if passes == 1:
        return lax.dot_general(a.astype(BF16), b.astype(BF16), dn, preferred_element_type=F32)
    ah, al = _split(a, 2)
    bh, bl = _split(b, 2)
    out = lax.dot_general(ah, bh, dn, preferred_element_type=F32)
    out = out + lax.dot_general(ah, bl, dn, preferred_element_type=F32)
    return out + lax.dot_general(al, bh, dn, preferred_element_type=F32)


def _mm01(m01, x, pieces=3):
    m = m01.astype(BF16)
    out = None
    for p in _split(x, pieces):
        t = lax.dot_general(m, p, _DN["nn"], preferred_element_type=F32)
        out = t if out is None else out + t
    return out


def _xmm01(x, m01, pieces=2):
    m = m01.astype(BF16)
    out = None
    for p in _split(x, pieces):
        t = lax.dot_general(p, m, _DN["nn"], preferred_element_type=F32)
        out = t if out is None else out + t
    return out


HI16 = -65536


def _bf16_bits(x):
    return lax.bitcast_convert_type(x.astype(BF16).astype(F32), I32)


def _unpack_pair(w):
    return lax.bitcast_convert_type(w << 16, F32), lax.bitcast_convert_type(w & HI16, F32)


def _rows_to_packed(ref, x, first=0):
    for c in range(PCH):
        lo = _bf16_bits(x[:, c * LANES:(c + 1) * LANES])
        hi = _bf16_bits(x[:, (c + PCH) * LANES:(c + PCH + 1) * LANES])
        ref[pl.ds(first * PCH + c, x.shape[0], stride=PCH), :] = ((lo >> 16) & 0xFFFF) | (hi & HI16)


def _rows_from_packed(ref, n, live=None, first=0):
    lows, highs = [], []
    for c in range(PCH):
        w = ref[pl.ds(first * PCH + c, n, stride=PCH), :]
        if live is not None:
            w = jnp.where(live, w, 0)
        lo, hi = _unpack_pair(w)
        lows.append(lo.astype(BF16))
        highs.append(hi.astype(BF16))
    return jnp.concatenate(lows + highs, axis=1)


def _slab(ref, row):
    return ref.at[pl.ds(pl.multiple_of(row * PCH, PCH), PCH)]


def _fslab(ref, row):
    return ref.at[pl.ds(pl.multiple_of(row * CHUNKS, CHUNKS), CHUNKS)]


def _sigmoid(x):
    return 1.0 / (1.0 + jnp.exp(-x))


def _softplus(x):
    return jnp.maximum(x, 0.0) + jnp.log(1.0 + jnp.exp(-jnp.abs(x)))


def _log2(n):
    assert n > 0 and n & (n - 1) == 0, n
    return n.bit_length() - 1


def _cparams(sem, vmem=None):
    return pltpu.CompilerParams(dimension_semantics=sem, vmem_limit_bytes=vmem)


def _mod_body(c_ref, w_ref, b_ref, o_ref):
    c = c_ref[...]
    o_ref[0] = _mm(c * _sigmoid(c), w_ref[...], passes=6) + b_ref[...]


def _mod_call(c_all, ada_w, ada_b):
    bt, d = c_all.shape
    out = pl.pallas_call(
        _mod_body,
        grid=(6,),
        in_specs=[pl.BlockSpec((bt, d), lambda k: (0, 0)),
                  pl.BlockSpec((d, d), lambda k: (0, k)),
                  pl.BlockSpec((1, d), lambda k: (0, k))],
        out_specs=pl.BlockSpec((1, bt, d), lambda k: (k, 0, 0)),
        out_shape=jax.ShapeDtypeStruct((6, bt, d), F32),
        compiler_params=_cparams(("arbitrary",)),
        name="adaln_mod",
    )(c_all, ada_w, ada_b.reshape(1, 6 * d))
    return jnp.transpose(out, (1, 0, 2))


def _inproj_body(x_ref, mod_ref, g_ref, w_ref, pa_ref, qkv_ref, xal_ref, gg_ref, mg_ref):
    bb, ll, d = x_ref.shape
    x = x_ref[...]
    y = x * lax.rsqrt(jnp.mean(x * x, axis=-1, keepdims=True) + NORM_EPS) * g_ref[...]
    h = y * (1.0 + mod_ref[:, 1:2, :]) + mod_ref[:, 0:1, :]
    hb = h.reshape(bb * ll, d).astype(BF16)
    for ref, off in zip((pa_ref, qkv_ref, xal_ref, gg_ref, mg_ref), PACK_OFFS):
        w = ref.shape[-1]
        ref[...] = jnp.dot(hb, w_ref[:, off:off + w], preferred_element_type=F32).reshape(bb, ll, w)


def _tile(bn, seq, tile):
    if seq >= tile:
        assert seq % tile == 0
        return 1, tile
    assert tile % seq == 0 and bn % (tile // seq) == 0
    return tile // seq, seq


def _rwprep_body(pa_ref, sh_ref, mu_ref, w0_ref, wup_ref, a0_ref, aup_ref, gup_ref, kk_ref, ka_ref, rk_ref,
                 bd_ref, r_o, lw_o, k_o, v_o, a_o, b_o, g_o, bon_o, nsh_o, carry):
    bb, ll, wd = pa_ref.shape
    n = bb * ll
    hw = RW_WIDTH

    @pl.when(pl.program_id(1) == 0)
    def _():
        carry[...] = sh_ref[...]

    pa = pa_ref[...]
    rolled = pltpu.roll(pa.reshape(n, wd), 1, 0).reshape(bb, ll, wd)
    tok = lax.broadcasted_iota(I32, (bb, ll, wd), 1)
    prev = jnp.where(tok == 0, carry[...], rolled)
    last = pa_ref[:, ll - 1:ll, :]
    carry[...] = last
    nsh_o[...] = last
    xs = (pa + (prev - pa) * mu_ref[...]).reshape(n, wd)

    r, k, v = xs[:, 0:hw], xs[:, hw:2 * hw], xs[:, 2 * hw:3 * hw]
    xwa = xs[:, 3 * hw:3 * hw + LANES]
    xg = xs[:, 3 * hw + LANES:]
    w_log = -_softplus(-(w0_ref[...] + _mm(jnp.tanh(xwa), wup_ref[...], passes=3))) - 0.5
    lw = -jnp.exp(w_log)
    a = _sigmoid(a0_ref[...] + _mm(xwa, aup_ref[...], passes=3))
    g = _mm(_sigmoid(xg), gup_ref[...])
    bd = bd_ref[...]
    kkv = k * kk_ref[...]
    kkn = kkv * lax.rsqrt(jnp.maximum(_xmm01(kkv * kkv, bd, pieces=1), 1e-24))
    k2 = k * (1.0 + (a - 1.0) * ka_ref[...])
    bonus = _xmm01(r * k2 * rk_ref[...], bd, pieces=1) * v
    for ref, val in ((r_o, r), (lw_o, lw), (k_o, k2), (v_o, v), (a_o, -kkn), (b_o, kkn * a), (g_o, g),
                     (bon_o, bonus)):
        ref[...] = val.reshape(bb, ll, hw)


def _inproj_prep_body(x_ref, mod_ref, g_ref, w_ref, sh_ref, mu_ref, w0_ref, wup_ref, a0_ref, aup_ref, gup_ref, kk_ref,
                      ka_ref, rk_ref, bd_ref, qkv_o, xal_o, gg_o, mg_o, r_o, lw_o, k_o, v_o, a_o, b_o, g_o, bon_o,
                      nsh_o, pa_s, carry):
    _inproj_body(x_ref, mod_ref, g_ref, w_ref, pa_s, qkv_o, xal_o, gg_o, mg_o)
    _rwprep_body(pa_s, sh_ref, mu_ref, w0_ref, wup_ref, a0_ref, aup_ref, gup_ref, kk_ref, ka_ref, rk_ref, bd_ref,
                 r_o, lw_o, k_o, v_o, a_o, b_o, g_o, bon_o, nsh_o, carry)


def _inproj_prep_call(x, mod, s_sh, p):
    bn, seq, d = x.shape
    bb, ll = _tile(bn, seq, TOK_TILE)
    hw, wd = RW_WIDTH, PA_W
    tok = lambda w: pl.BlockSpec((bb, ll, w), lambda b, l: (b, l, 0))
    row = lambda w: pl.BlockSpec((bb, 1, w), lambda b, l: (b, 0, 0))
    full = lambda a: pl.BlockSpec(a.shape, lambda b, l: (0,) * a.ndim)
    consts = (p["mu"], p["w0"], p["wup"], p["a0"], p["aup"], p["gup"], p["kk"], p["ka"], p["rk"], p["bd64"])
    proj_w = (QKV_W, XAL_W, GG_W, MG_W)
    shapes = lambda ws: [jax.ShapeDtypeStruct((bn, seq, w), F32) for w in ws]
    return pl.pallas_call(
        _inproj_prep_body,
        grid=(bn // bb, seq // ll),
        in_specs=[tok(d), pl.BlockSpec((bb, 6, d), lambda b, l: (b, 0, 0)), full(p["norm1_g"]), full(p["w_pack"]),
                  row(wd)] + [full(c) for c in consts],
        out_specs=[tok(w) for w in proj_w] + [tok(hw)] * 8 + [row(wd)],
        out_shape=shapes(proj_w) + shapes((hw,) * 8) + [jax.ShapeDtypeStruct((bn, 1, wd), F32)],
        scratch_shapes=[pltpu.VMEM((bb, ll, wd), F32), pltpu.VMEM((bb, 1, wd), F32)],
        compiler_params=_cparams(("arbitrary", "arbitrary"), VMEM_LIMIT),
        name="norm_inproj_prep",
    )(x, mod, p["norm1_g"], p["w_pack"], s_sh.reshape(bn, 1, wd), *consts)


def _unit_masks(n, tl):
    ri = lax.broadcasted_iota(I32, (n, n), 0)
    ci = lax.broadcasted_iota(I32, (n, n), 1)
    same = (ri >> _log2(tl)) == (ci >> _log2(tl))
    return same, same & (ri > ci), same & (ri >= ci)


def _rwscan_body(r_ref, lw_ref, k_ref, v_ref, a_ref, b_ref, s0_ref, y_ref, sn_ref, st, *, nu, nseq, tl, passes):
    n = nseq * tl
    n2 = 2 * n
    p_aa, p_inv, p_apply, p_state, p_y = passes

    hd = RW_HEAD

    @pl.when(pl.program_id(1) == 0)
    def _():
        zero = jnp.zeros((hd, hd), F32)
        for q in range(nu * nseq):
            for p in range(RW_HEADS // 2):
                st[q, p] = jnp.concatenate(
                    [jnp.concatenate([s0_ref[q, 2 * p], zero], axis=1),
                     jnp.concatenate([zero, s0_ref[q, 2 * p + 1]], axis=1)], axis=0)

    same, _, incl = _unit_masks(n, tl)
    m_cum = jnp.where(incl, 1.0, 0.0)
    m_seq = jnp.where(same, 1.0, 0.0)
    ri = lax.broadcasted_iota(I32, (n2, n2), 0)
    ci = lax.broadcasted_iota(I32, (n2, n2), 1)
    rt, ct = ri & (n - 1), ci & (n - 1)
    dsame = ((rt >> _log2(tl)) == (ct >> _log2(tl))) & ((ri >> _log2(n)) == (ci >> _log2(n)))
    strict_d = dsame & (rt > ct)
    incl_d = dsame & (rt >= ct)
    eye_d = jnp.where(ri == ci, 1.0, 0.0)
    lane = lax.broadcasted_iota(I32, (1, LANES), 1)
    m0 = jnp.where(lane < RW_HEAD, 1.0, 0.0)
    m1 = 1.0 - m0

    def dup(x):
        return jnp.concatenate([x * m0, x * m1], axis=0)

    def seq_rows(x, q):
        if nseq == 1:
            return x
        return jnp.concatenate([x[q * tl:(q + 1) * tl], x[n + q * tl:n + (q + 1) * tl]], axis=0)

    def unit_rows(parts):
        if nseq == 1:
            return parts[0]
        return jnp.concatenate([p[0:tl] for p in parts] + [p[tl:2 * tl] for p in parts], axis=0)

    chains = [(u, p) for u in range(nu) for p in range(RW_HEADS // 2)]
    ids = range(len(chains))
    cat0 = lambda *xs: jnp.concatenate(xs, axis=0)

    def ld(ref, c):
        u, p = chains[c]
        return ref[u * nseq:(u + 1) * nseq, :, p * LANES:(p + 1) * LANES].reshape(n, LANES)

    lw = [ld(lw_ref, c) for c in ids]
    cum = [_mm01(m_cum, x) for x in lw]
    tot = [_mm01(m_seq, x) for x in lw]
    e_c = [jnp.exp(x) for x in cum]
    e_n = [jnp.exp(-x) for x in cum]
    e_l = [jnp.exp(t - x) for t, x in zip(tot, cum)]
    at_d = [dup(ld(a_ref, c) * jnp.exp(cum[c] - lw[c])) for c in ids]
    rt_d = [dup(ld(r_ref, c) * e_c[c]) for c in ids]
    bt_d = [dup(ld(b_ref, c) * e_n[c]) for c in ids]
    kt_d = [dup(ld(k_ref, c) * e_n[c]) for c in ids]
    bh_d = [dup(ld(b_ref, c) * e_l[c]) for c in ids]
    kh_d = [dup(ld(k_ref, c) * e_l[c]) for c in ids]
    v_d = [dup(ld(v_ref, c)) for c in ids]
    aa = [_mm(cat0(at_d[c], rt_d[c]), cat0(bt_d[c], kt_d[c]), "nt", p_aa) for c in ids]
    a_ab = [jnp.where(strict_d, x[0:n2, 0:n2], 0.0) for x in aa]
    a_ak = [jnp.where(strict_d, x[0:n2, n2:], 0.0) for x in aa]
    a_rb = [jnp.where(incl_d, x[n2:, 0:n2], 0.0) for x in aa]
    a_rk = [jnp.where(incl_d, x[n2:, n2:], 0.0) for x in aa]
    zy = [_mm(cat0(a_ak[c], a_rk[c]), v_d[c], passes=p_apply) for c in ids]
    tinv = [eye_d + x for x in a_ab]
    nk = a_ab
    for _ in range(_log2(tl) - 1):
        nk = [_mm(x, x, passes=p_inv) for x in nk]
        tinv = [t + _mm(t, x, passes=p_inv) for t, x in zip(tinv, nk)]
    wu = [_mm(tinv[c], jnp.concatenate([at_d[c], zy[c][0:n2]], axis=1), passes=p_apply) for c in ids]
    seqs = range(nseq)
    srow = lambda c, q: (chains[c][0] * nseq + q, chains[c][1])
    s_old = [[st[srow(c, q)] for q in seqs] for c in ids]
    xs = [[_mm(cat0(seq_rows(wu[c][:, 0:LANES], q), seq_rows(rt_d[c], q)), s_old[c][q], "nt", p_state)
           for q in seqs] for c in ids]
    u_q = [[xs[c][q][0:2 * tl] + seq_rows(wu[c][:, LANES:], q) for q in seqs] for c in ids]
    for c in ids:
        for q in seqs:
            g_c = jnp.exp(tot[c][q * tl:q * tl + 1, :])
            st[srow(c, q)] = s_old[c][q] * g_c + _mm(cat0(u_q[c][q], seq_rows(v_d[c], q)),
                                                     cat0(seq_rows(bh_d[c], q), seq_rows(kh_d[c], q)), "tn", p_state)
    for c in ids:
        u, p = chains[c]
        y_d = (unit_rows([xs[c][q][2 * tl:] for q in seqs]) + _mm(a_rb[c], unit_rows(u_q[c]), passes=p_y)
               + zy[c][n2:])
        y_ref[u * nseq:(u + 1) * nseq, :, p * LANES:(p + 1) * LANES] = (y_d[0:n] + y_d[n:]).reshape(nseq, tl, LANES)

    @pl.when(pl.program_id(1) == pl.num_programs(1) - 1)
    def _():
        for q in range(nu * nseq):
            for p in range(RW_HEADS // 2):
                s = st[q, p]
                sn_ref[q, 2 * p] = s[0:hd, 0:hd]
                sn_ref[q, 2 * p + 1] = s[hd:, hd:]


def _unit_shape(bn, seq):
    if seq >= UNIT:
        assert seq % UNIT == 0
        return 1, UNIT
    assert UNIT % seq == 0 and bn % (UNIT // seq) == 0
    return UNIT // seq, seq


def _rwscan_call(r, lw, k2, v, a_s, b_s, s0, passes=RW_SCAN_PASSES):
    bn, seq, hw = r.shape
    nseq, tl = _unit_shape(bn, seq)
    nu = RW_UNITS_PER_STEP if bn % (RW_UNITS_PER_STEP * nseq) == 0 else 1
    rows = nu * nseq
    tok = pl.BlockSpec((rows, tl, hw), lambda b, c: (b, c, 0))
    stt = pl.BlockSpec((rows, RW_HEADS, RW_HEAD, RW_HEAD), lambda b, c: (b, 0, 0, 0))
    return pl.pallas_call(
        functools.partial(_rwscan_body, nu=nu, nseq=nseq, tl=tl, passes=passes),
        grid=(bn // rows, seq // tl),
        in_specs=[tok] * 6 + [stt],
        out_specs=[tok, stt],
        out_shape=[jax.ShapeDtypeStruct((bn, seq, hw), F32), jax.ShapeDtypeStruct(s0.shape, F32)],
        scratch_shapes=[pltpu.VMEM((rows, RW_HEADS // 2, LANES, LANES), F32)],
        compiler_params=_cparams(("arbitrary", "arbitrary"), VMEM_LIMIT),
        name="rwkv_scan",
    )(r, lw, k2, v, a_s, b_s, s0)


def _gla_body(qkv_ref, xal_ref, gate_ref, aup_ref, ab_ref, ng_ref, s0_ref, o_ref, sn_ref, st, *, nu, nseq, tl, cs):
    n = nseq * tl
    n2 = 2 * n
    nsub = tl // cs

    @pl.when(pl.program_id(1) == 0)
    def _():
        zero = jnp.zeros((GLA_DV, GLA_DK), F32)
        for q in range(nu * nseq):
            for p in range(GLA_HEADS // 2):
                st[q, p] = jnp.concatenate(
                    [jnp.concatenate([s0_ref[q, 2 * p].T, zero], axis=1),
                     jnp.concatenate([zero, s0_ref[q, 2 * p + 1].T], axis=1)], axis=0)

    same, _, incl = _unit_masks(n, cs)
    m_cum = jnp.where(incl, 1.0, 0.0)
    m_sub = jnp.where(same, 1.0, 0.0)
    ri = lax.broadcasted_iota(I32, (n2, n2), 0)
    ci = lax.broadcasted_iota(I32, (n2, n2), 1)
    rt, ct = ri & (n - 1), ci & (n - 1)
    causal_d = ((rt >> _log2(cs)) == (ct >> _log2(cs))) & ((ri >> _log2(n)) == (ci >> _log2(n))) & (rt >= ct)
    lane = lax.broadcasted_iota(I32, (1, LANES), 1)
    m0 = jnp.where(lane < GLA_DK, 1.0, 0.0)
    m1 = 1.0 - m0
    sr = lax.broadcasted_iota(I32, (2 * GLA_DV, LANES), 0)
    sc = lax.broadcasted_iota(I32, (2 * GLA_DV, LANES), 1)
    st_mask = jnp.where((sr >> _log2(GLA_DV)) == (sc >> _log2(GLA_DK)), 1.0, 0.0)

    def dup(x):
        return jnp.concatenate([x * m0, x * m1], axis=0)

    chains = [(u, p) for u in range(nu) for p in range(GLA_HEADS // 2)]
    ids = range(len(chains))
    urows = lambda u: slice(u * nseq, (u + 1) * nseq)
    ng = ng_ref[...]
    la_all = [-_softplus(-(_mm(xal_ref[urows(u), :, :].reshape(n, LANES), aup_ref[...], passes=3) + ab_ref[...]))
              * (1.0 / GLA_GATE_TAU) for u in range(nu)]

    def ld(ref, c, off, width):
        return ref[urows(chains[c][0]), :, off:off + width].reshape(n, width)

    q = [ld(qkv_ref, c, chains[c][1] * LANES, LANES) * (GLA_DK ** -0.5) for c in ids]
    k = [ld(qkv_ref, c, GLA_KW + chains[c][1] * LANES, LANES) for c in ids]
    vp = [ld(qkv_ref, c, 2 * GLA_KW + chains[c][1] * 2 * GLA_DV, 2 * GLA_DV) for c in ids]
    la = [la_all[u][:, p * LANES:(p + 1) * LANES] for u, p in chains]
    bc = [_mm01(m_cum, x) for x in la]
    bl = [_mm01(m_sub, x) for x in la]
    qe = [q[c] * jnp.exp(bc[c]) for c in ids]
    ke = [k[c] * jnp.exp(-bc[c]) for c in ids]
    kd = [k[c] * jnp.exp(bl[c] - bc[c]) for c in ids]
    att = [jnp.where(causal_d, _mm(dup(qe[c]), dup(ke[c]), "nt", passes=1), 0.0) for c in ids]
    v_st = [jnp.concatenate([x[:, 0:GLA_DV], x[:, GLA_DV:]], axis=0) for x in vp]
    o_st = [_mm(att[c], v_st[c], passes=1) for c in ids]
    upd = [[_mm(vp[c][r0:r0 + cs], kd[c][r0:r0 + cs], "tn", passes=1) for r0 in range(0, n, cs)] for c in ids]
    inter = [[None] * (n // cs) for _ in ids]
    for sq in range(nseq):
        s = [st[chains[c][0] * nseq + sq, chains[c][1]] for c in ids]
        for j in range(nsub):
            i = sq * nsub + j
            r0 = i * cs
            for c in ids:
                inter[c][i] = _mm(qe[c][r0:r0 + cs], s[c], "nt", passes=1)
                s[c] = s[c] * jnp.exp(bl[c][r0:r0 + 1, :]) + st_mask * upd[c][i]
        for c in ids:
            st[chains[c][0] * nseq + sq, chains[c][1]] = s[c]
    for c in ids:
        u, p = chains[c]
        o = o_st[c] + jnp.concatenate([x[:, 0:GLA_DV] for x in inter[c]] + [x[:, GLA_DV:] for x in inter[c]], axis=0)
        o = o * lax.rsqrt(jnp.mean(o * o, axis=-1, keepdims=True) + NORM_EPS) * ng
        goff = p * 2 * GLA_DV
        gp = ld(gate_ref, c, goff, 2 * GLA_DV)
        g_st = jnp.concatenate([gp[:, 0:GLA_DV], gp[:, GLA_DV:]], axis=0)
        ob = o * (g_st * _sigmoid(g_st))
        o_ref[urows(u), :, goff:goff + GLA_DV] = ob[0:n].reshape(nseq, tl, GLA_DV)
        o_ref[urows(u), :, goff + GLA_DV:goff + 2 * GLA_DV] = ob[n:].reshape(nseq, tl, GLA_DV)

    @pl.when(pl.program_id(1) == pl.num_programs(1) - 1)
    def _():
        for q in range(nu * nseq):
            for p in range(GLA_HEADS // 2):
                s = st[q, p]
                sn_ref[q, 2 * p] = s[0:GLA_DV, 0:GLA_DK].T
                sn_ref[q, 2 * p + 1] = s[GLA_DV:, GLA_DK:].T


def _gla_call(qkv, xal, gate, s0, p):
    bn, seq, _ = qkv.shape
    nseq, tl = _unit_shape(bn, seq)
    cs = min(GLA_CHUNK, seq)
    assert tl % cs == 0
    nu = GLA_UNITS_PER_STEP if bn % (GLA_UNITS_PER_STEP * nseq) == 0 else 1
    rows = nu * nseq
    tok = lambda w: pl.BlockSpec((rows, tl, w), lambda b, c: (b, c, 0))
    full = lambda a: pl.BlockSpec(a.shape, lambda b, c: (0,) * a.ndim)
    stt = pl.BlockSpec((rows, GLA_HEADS, GLA_DK, GLA_DV), lambda b, c: (b, 0, 0, 0))
    consts = (p["gla_aup"], p["gla_ab"], p["gla_ng"])
    return pl.pallas_call(
        functools.partial(_gla_body, nu=nu, nseq=nseq, tl=tl, cs=cs),
        grid=(bn // rows, seq // tl),
        in_specs=[tok(QKV_W), tok(XAL_W), tok(GG_W)] + [full(c) for c in consts] + [stt],
        out_specs=[tok(GLA_VW), stt],
        out_shape=[jax.ShapeDtypeStruct((bn, seq, GLA_VW), F32), jax.ShapeDtypeStruct(s0.shape, F32)],
        scratch_shapes=[pltpu.VMEM((rows, GLA_HEADS // 2, 2 * GLA_DV, LANES), F32)],
        compiler_params=_cparams(("arbitrary", "arbitrary"), VMEM_LIMIT),
        name="gla_chunked",
    )(qkv, xal, gate, *consts, s0)


def _merge_body(y_ref, g_ref, bon_ref, ob_ref, mg_ref, x_ref, mod_ref, gng_ref, gnb_ref, bd_ref, wpa_ref,
                wpb_ref, wout_ref, n2_ref, rwh_ref, rwl_ref, x1_o, h2_o, lg_o):
    bb, ll, d = x_ref.shape
    n = bb * ll
    hw = RW_WIDTH
    bd = bd_ref[...]
    y = y_ref[...].reshape(n, hw)
    mu = _xmm01(y, bd, pieces=2) * (1.0 / RW_HEAD)
    dv = y - mu
    var = _xmm01(dv * dv, bd, pieces=1) * (1.0 / RW_HEAD)
    yn = dv * lax.rsqrt(var + RW_GN_EPS) * gng_ref[...] + gnb_ref[...]
    o_a = (yn + bon_ref[...].reshape(n, hw)) * g_ref[...].reshape(n, hw)
    o_b = ob_ref[...].reshape(n, GLA_VW)
    mg = mg_ref[...].reshape(n, 2 * d)
    merged = _sigmoid(mg[:, 0:d]) * _mm(o_a, wpa_ref[...]) + _sigmoid(mg[:, d:]) * _mm(o_b, wpb_ref[...])
    mix = _mm(merged, wout_ref[...]).reshape(bb, ll, d)
    x1 = x_ref[...] + mod_ref[:, 2:3, :] * mix
    x1_o[...] = x1
    yn2 = x1 * lax.rsqrt(jnp.mean(x1 * x1, axis=-1, keepdims=True) + NORM_EPS) * n2_ref[...]
    h2 = (yn2 * (1.0 + mod_ref[:, 4:5, :]) + mod_ref[:, 3:4, :]).reshape(n, d)
    hh, hl = _split(h2, 2)
    rwh, rwl = rwh_ref[...], rwl_ref[...]
    nt = lambda a, b: lax.dot_general(a, b, _DN["nt"], preferred_element_type=F32)
    lg_o[...] = nt(rwh, hh) + nt(rwl, hh) + nt(rwh, hl)
    _rows_to_packed(h2_o, h2)


def _merge_call(y, g, bonus, o_b, mg, x, mod, p):
    bn, seq, d = x.shape
    bb, ll = _tile(bn, seq, TOK_TILE)
    nl = seq // ll
    tn = bn * seq
    tok = lambda w: pl.BlockSpec((bb, ll, w), lambda b, l: (b, l, 0))
    full = lambda a: pl.BlockSpec(a.shape, lambda b, l: (0,) * a.ndim)
    consts = (p["gn_g"], p["gn_b"], p["bd64"], p["w_pa"], p["w_pb"], p["w_out"], p["norm2_g"], p["rw_hi"],
              p["rw_lo"])
    return pl.pallas_call(
        _merge_body,
        grid=(bn // bb, nl),
        in_specs=[tok(RW_WIDTH)] * 3 + [tok(GLA_VW), tok(MG_W), tok(d),
                                        pl.BlockSpec((bb, 6, d), lambda b, l: (b, 0, 0))] + [full(c) for c in consts],
        out_specs=[tok(d),
                   pl.BlockSpec((bb * ll * PCH, LANES), lambda b, l: (b * nl + l, 0)),
                   pl.BlockSpec((N_EXPERTS, bb * ll), lambda b, l: (0, b * nl + l))],
        out_shape=[jax.ShapeDtypeStruct((bn, seq, d), F32),
                   jax.ShapeDtypeStruct((tn * PCH, LANES), I32),
                   jax.ShapeDtypeStruct((N_EXPERTS, tn), F32)],
        compiler_params=_cparams(("arbitrary", "arbitrary"), VMEM_LIMIT),
        name="merge_outproj_router",
    )(y, g, bonus, o_b, mg, x, mod, *consts)


def _route_body(lg_ref, rb_ref, e_o, rk_o, w_o, cnt_o, carry):
    ne, tm = lg_ref.shape

    @pl.when(pl.program_id(0) == 0)
    def _():
        carry[...] = jnp.zeros_like(carry)

    neg = -jnp.inf
    scores = _sigmoid(lg_ref[...])
    sel = scores + rb_ref[...]
    row_i = lax.broadcasted_iota(I32, (ne, tm), 0)
    row = row_i.astype(F32)
    grp = (row_i >> _log2(GROUP_SIZE)).astype(F32)

    def first_max(x, ids, none):
        m = jnp.max(x, axis=0, keepdims=True)
        return m, jnp.min(jnp.where(x == m, ids, none), axis=0, keepdims=True)

    gs = []
    gids = lax.broadcasted_iota(I32, (GROUP_SIZE, tm), 0)
    for gidx in range(N_GROUPS):
        rows = slice(gidx * GROUP_SIZE, (gidx + 1) * GROUP_SIZE)
        sg = _sigmoid(lg_ref[rows, :]) + rb_ref[rows, :]
        ids = (gids + gidx * GROUP_SIZE).astype(F32)
        m1, i1 = first_max(sg, ids, float(ne))
        gs.append(m1 + jnp.max(jnp.where(ids == i1, neg, sg), axis=0, keepdims=True))
    gs = jnp.concatenate(gs, axis=0)
    gid = lax.broadcasted_iota(I32, (N_GROUPS, tm), 0).astype(F32)
    cur = jnp.full((ne, tm), neg, F32)
    for _ in range(TOPK_GROUPS):
        _, gi = first_max(gs, gid, float(N_GROUPS))
        cur = jnp.where(grp == gi, sel, cur)
        gs = jnp.where(gid == gi, neg, gs)

    pm = jnp.zeros((ne, tm), F32)
    eidx, wts = [], []
    for _ in range(TOP_K):
        _, ei = first_max(cur, row, float(ne))
        hit = row == ei
        pm = jnp.where(hit, 1.0, pm)
        eidx.append(ei)
        wts.append(jnp.sum(jnp.where(hit, scores, 0.0), axis=0, keepdims=True))
        cur = jnp.where(hit, neg, cur)
    wsum = wts[0]
    for w in wts[1:]:
        wsum = wsum + w

    ri = lax.broadcasted_iota(I32, (tm, tm), 0)
    ci = lax.broadcasted_iota(I32, (tm, tm), 1)
    earlier = jnp.where(ri < ci, 1.0, 0.0)
    rank = _mm(pm, earlier, passes=1) + carry[...]
    carry[...] = carry[...] + jnp.sum(pm, axis=1, keepdims=True)
    cnt_o[...] = carry[...]

    rks = [jnp.sum(jnp.where(row == e, rank, 0.0), axis=0, keepdims=True) for e in eidx]
    e_o[0] = jnp.concatenate(eidx, axis=0).astype(I32)
    rk_o[0] = jnp.concatenate(rks, axis=0).astype(I32)
    w_o[0] = jnp.concatenate([w / wsum * ROUTED_SCALE for w in wts], axis=0)


def _route_call(logits_t, router_b):
    ne, tn = logits_t.shape
    tm = TOK_TILE
    assert tn % tm == 0
    col = pl.BlockSpec((ne, 1), lambda i: (0, 0))
    tab = pl.BlockSpec((1, TOP_K, tm), lambda i: (i, 0, 0))
    tab_shape = (tn // tm, TOP_K, tm)
    return pl.pallas_call(
        _route_body,
        grid=(tn // tm,),
        in_specs=[pl.BlockSpec((ne, tm), lambda i: (0, i)), col],
        out_specs=[tab, tab, tab, col],
        out_shape=[jax.ShapeDtypeStruct(tab_shape, I32), jax.ShapeDtypeStruct(tab_shape, I32),
                   jax.ShapeDtypeStruct(tab_shape, F32), jax.ShapeDtypeStruct((ne, 1), F32)],
        scratch_shapes=[pltpu.VMEM((ne, 1), F32)],
        compiler_params=_cparams(("arbitrary",)),
        name="moe_route",
    )(logits_t, router_b.reshape(ne, 1))


def _dest_body(e_ref, rk_ref, ps_ref, d_o):
    ne, tm = ps_ref.shape[0], e_ref.shape[2]
    ids = lax.broadcasted_iota(I32, (ne, tm), 0)
    ps = ps_ref[...]
    first = [jnp.sum(jnp.where(ids == e_ref[0, kk:kk + 1, :], ps, 0.0), axis=0, keepdims=True)
             for kk in range(TOP_K)]
    d_o[0] = (jnp.concatenate(first, axis=0).astype(I32) + rk_ref[0]) * PCH


def _dest_call(eidx, rank, pad_start):
    nt, _, tm = eidx.shape
    ne = pad_start.shape[0]
    tab = pl.BlockSpec((1, TOP_K, tm), lambda i: (i, 0, 0))
    return pl.pallas_call(
        _dest_body,
        grid=(nt,),
        in_specs=[tab, tab, pl.BlockSpec((ne, 1), lambda i: (0, 0))],
        out_specs=tab,
        out_shape=jax.ShapeDtypeStruct(eidx.shape, I32),
        compiler_params=_cparams(("arbitrary",)),
        name="moe_dest",
    )(eidx, rank, pad_start.astype(F32).reshape(ne, 1))


def _pslab(ref, offset):
    return ref.at[pl.ds(pl.multiple_of(offset, PCH), PCH)]


def _dispatch_body(d_ref, h2_ref, xs_hbm, sem, *, tm):
    def issue(m, carry):
        for kk in range(TOP_K):
            pltpu.make_async_copy(_slab(h2_ref, m), _pslab(xs_hbm, d_ref[0, kk, m]), sem).start(priority=kk % 2)
        return carry

    lax.fori_loop(0, tm, issue, 0)
    all_rows = xs_hbm.at[pl.ds(0, tm * TOP_K * PCH)]
    pltpu.make_async_copy(all_rows, all_rows, sem).wait()


def _assign_spec(tm, index_map):
    return pl.BlockSpec((1, TOP_K, tm), index_map, memory_space=pltpu.SMEM)


def _dispatch_call(dest, h2s, n_rows):
    tn = h2s.shape[0] // PCH
    tm = TOK_TILE
    assert dest.shape == (tn // tm, TOP_K, tm)
    blk = _assign_spec(tm, lambda i: (i, 0, 0))
    return pl.pallas_call(
        functools.partial(_dispatch_body, tm=tm),
        grid=(tn // tm,),
        in_specs=[blk, pl.BlockSpec((tm * PCH, LANES), lambda i: (i, 0))],
        out_specs=pl.BlockSpec(memory_space=pl.ANY),
        out_shape=jax.ShapeDtypeStruct((n_rows * PCH, LANES), I32),
        scratch_shapes=[pltpu.SemaphoreType.DMA],
        compiler_params=_cparams(("arbitrary",)),
        name="moe_dispatch",
    )(dest, h2s)


def _expert_body(bi_ref, nr_ref, ld_ref, nx_ref, xs_hbm, wg_hbm, wu_hbm, wd_hbm, ob_ref, wg_buf, wu_buf, wd_buf,
                 wg_bf, wu_bf, wd_bf, xbuf, sem, xsem):
    i = pl.program_id(0)
    nsteps = pl.num_programs(0)
    nr = nr_ref[i]
    slot = ld_ref[i]
    blk_rows = MOE_BLK * PCH

    def row_block(j):
        s = lax.rem(j, ROW_SLOTS)
        src = xs_hbm.at[pl.ds(pl.multiple_of(bi_ref[j] * blk_rows, blk_rows), blk_rows)]
        return pltpu.make_async_copy(src, xbuf.at[s], xsem.at[s])

    @pl.when(i == 0)
    def _():
        for j in range(ROW_SLOTS - 1):
            row_block(j).start()

    @pl.when(i + ROW_SLOTS - 1 < nsteps)
    def _():
        row_block(i + ROW_SLOTS - 1).start()

    def fetch(e, s):
        return (pltpu.make_async_copy(wg_hbm.at[e], wg_buf.at[s], sem.at[s]),
                pltpu.make_async_copy(wu_hbm.at[e], wu_buf.at[s], sem.at[s]),
                pltpu.make_async_copy(wd_hbm.at[e], wd_buf.at[s], sem.at[s]))

    @pl.when(i == 0)
    def _():
        for s in range(WEIGHT_SLOTS - 1):
            e0 = nx_ref[nx_ref.shape[0] - (WEIGHT_SLOTS - 1) + s]

            @pl.when(e0 >= 0)
            def _():
                for cp in fetch(e0, s):
                    cp.start()

    @pl.when(slot >= 0)
    def _():
        for cp in fetch(0, slot):
            cp.wait()

        @pl.when(nx_ref[i] >= 0)
        def _():
            for cp in fetch(nx_ref[i], lax.rem(slot + WEIGHT_SLOTS - 1, WEIGHT_SLOTS)):
                cp.start()

        wg_bf[...] = wg_buf[slot].astype(BF16)
        wu_bf[...] = wu_buf[slot].astype(BF16)
        wd_bf[...] = wd_buf[slot].astype(BF16)

    row_block(i).wait()

    @pl.when(nr > 0)
    def _():
        part = MOE_BLK // EXPERT_PARTS
        firsts = [q * part for q in range(EXPERT_PARTS)]
        rid = lax.broadcasted_iota(I32, (part, LANES), 0)
        xs_ref = xbuf.at[lax.rem(i, ROW_SLOTS)]
        x = [_rows_from_packed(xs_ref, part, rid < nr - f, f) for f in firsts]
        hg = [jnp.dot(v, wg_bf[...], preferred_element_type=F32) for v in x]
        hu = [jnp.dot(v, wu_bf[...], preferred_element_type=F32) for v in x]
        hh = [(g * _sigmoid(g) * u).astype(BF16) for g, u in zip(hg, hu)]
        out = [jnp.dot(v, wd_bf[...], preferred_element_type=F32) for v in hh]
        for f, v in zip(firsts, out):
            _rows_to_packed(ob_ref, v, f)


def _expert_tables(counts, pad_start, pad_end, nb):
    ne = counts.shape[0]
    first_row = jnp.arange(nb, dtype=I32) * MOE_BLK
    block_e = jnp.minimum(jnp.sum(pad_end[None, :] <= first_row[:, None], axis=1), ne - 1).astype(I32)
    block_rows = jnp.clip(pad_start[block_e] + counts[block_e] - first_row, 0, MOE_BLK).astype(I32)
    block_i = jnp.minimum(jnp.arange(nb, dtype=I32), pad_end[-1] // MOE_BLK - 1).astype(I32)
    has = counts > 0
    ordinal = jnp.cumsum(has.astype(I32)) - 1
    ids = jnp.where(has, jnp.arange(ne, dtype=I32), ne)
    nxt = jnp.concatenate([lax.cummin(ids, reverse=True)[1:], jnp.full((1,), ne, I32)])
    nxt = jnp.concatenate([nxt, jnp.full((1,), ne, I32)])
    hop = lambda e, k: functools.reduce(lambda x, _: nxt[x], range(k), e)
    starts = (first_row == pad_start[block_e]) & (block_rows > 0)
    load_slot = jnp.where(starts, ordinal[block_e] % WEIGHT_SLOTS, -1).astype(I32)
    ahead = hop(block_e, WEIGHT_SLOTS - 1)
    first = jnp.min(ids)
    lead = jnp.stack([hop(first, k) for k in range(WEIGHT_SLOTS - 1)])
    next_e = jnp.concatenate([jnp.where(starts, ahead, ne), lead])
    next_e = jnp.where(next_e < ne, next_e, -1).astype(I32)
    return block_i, block_rows, load_slot, next_e


def _expert_call(tables, xs, wg, wu, wd):
    nb = xs.shape[0] // (MOE_BLK * PCH)
    assert nb >= ROW_SLOTS
    d, ff = wg.shape[1], wg.shape[2]
    rows = pl.BlockSpec((MOE_BLK * PCH, LANES), lambda i, bi, nr, ld, nx: (bi[i], 0))
    hbm = pl.BlockSpec(memory_space=pl.ANY)
    grid_spec = pltpu.PrefetchScalarGridSpec(
        num_scalar_prefetch=4,
        grid=(nb,),
        in_specs=[hbm, hbm, hbm, hbm],
        out_specs=rows,
        scratch_shapes=[pltpu.VMEM((WEIGHT_SLOTS, d, ff), F32), pltpu.VMEM((WEIGHT_SLOTS, d, ff), F32),
                        pltpu.VMEM((WEIGHT_SLOTS, ff, d), F32),
                        pltpu.VMEM((d, ff), BF16), pltpu.VMEM((d, ff), BF16), pltpu.VMEM((ff, d), BF16),
                        pltpu.VMEM((ROW_SLOTS, MOE_BLK * PCH, LANES), I32),
                        pltpu.SemaphoreType.DMA((WEIGHT_SLOTS,)), pltpu.SemaphoreType.DMA((ROW_SLOTS,))],
    )
    return pl.pallas_call(
        _expert_body,
        grid_spec=grid_spec,
        out_shape=jax.ShapeDtypeStruct(xs.shape, I32),
        compiler_params=_cparams(("arbitrary",), VMEM_LIMIT),
        name="moe_experts",
    )(*tables, xs, wg, wu, wd)


def _combine_body(d_ref, dn_ref, wt_ref, ob_hbm, h2_ref, x1_ref, mod_ref, sg_ref, su_ref,
                  sd_ref, fg_ref, out_ref, gbuf, rbuf, sem, *, tm, nl):
    bb, ll, d = x1_ref.shape
    step = pl.program_id(0) * nl + pl.program_id(1)
    last = pl.num_programs(0) * nl - 1
    parity = lax.rem(step, 2)

    def request(d_tab, m, s):
        for kk in range(TOP_K):
            pltpu.make_async_copy(_pslab(ob_hbm, d_tab[0, kk, m]), _slab(gbuf.at[s], m * TOP_K + kk),
                                  sem.at[s]).start(priority=kk % 2)

    def mix(m, s):
        rows = gbuf.at[s]
        lo, hi = _unpack_pair(_slab(rows, m * TOP_K)[...])
        wt = wt_ref[0, 0, m]
        acc_lo, acc_hi = wt * lo, wt * hi
        for kk in range(1, TOP_K):
            lo, hi = _unpack_pair(_slab(rows, m * TOP_K + kk)[...])
            wt = wt_ref[0, kk, m]
            acc_lo, acc_hi = acc_lo + wt * lo, acc_hi + wt * hi
        _fslab(rbuf, m)[...] = jnp.concatenate([acc_lo, acc_hi], axis=0)

    @pl.when(step == 0)
    def _():
        def first(m, carry):
            request(d_ref, m, 0)
            return carry
        lax.fori_loop(0, tm, first, 0)

    def run(slot):
        pltpu.make_async_copy(ob_hbm.at[pl.ds(0, tm * TOP_K * PCH)], gbuf.at[slot], sem.at[slot]).wait()

        @pl.when(step < last)
        def _():
            def both(m, carry):
                request(dn_ref, m, 1 - slot)
                mix(m, slot)
                return carry
            lax.fori_loop(0, tm, both, 0)

        @pl.when(step == last)
        def _():
            def only(m, carry):
                mix(m, slot)
                return carry
            lax.fori_loop(0, tm, only, 0)

    for slot in range(2):
        pl.when(parity == slot)(functools.partial(run, slot))

    routed = jnp.concatenate([rbuf[pl.ds(c, tm, stride=CHUNKS), :] for c in range(CHUNKS)], axis=1)
    h2 = _rows_from_packed(h2_ref, tm)
    hg = jnp.dot(h2, sg_ref[...], preferred_element_type=F32)
    hu = jnp.dot(h2, su_ref[...], preferred_element_type=F32)
    shared = jnp.dot((hg * _sigmoid(hg) * hu).astype(BF16), sd_ref[...], preferred_element_type=F32)
    ff = (routed + shared).reshape(bb, ll, d)
    x2 = x1_ref[...] + mod_ref[:, 5:6, :] * ff
    out_ref[...] = x2 * lax.rsqrt(jnp.mean(x2 * x2, axis=-1, keepdims=True) + NORM_EPS) * fg_ref[...]


def _combine_call(dest, wts, first_tok, ob, h2s, x1, mod, p):
    bn, seq, d = x1.shape
    tm = CMB_TILE
    bb, ll = _tile(bn, seq, tm)
    nl = seq // ll
    tn = bn * seq
    nsteps = tn // tm
    per = dest.shape[2] // tm
    assert first_tok % tm == 0 and dest.shape[2] % tm == 0
    tile = lambda g: ((first_tok // tm + g) // per, 0, (first_tok // tm + g) % per)
    smem = _assign_spec(tm, lambda b, l: tile(b * nl + l))
    smem_next = _assign_spec(tm, lambda b, l: tile(jnp.minimum(b * nl + l + 1, nsteps - 1)))
    tok = pl.BlockSpec((bb, ll, d), lambda b, l: (b, l, 0))
    full = lambda a: pl.BlockSpec(a.shape, lambda b, l: (0,) * a.ndim)
    consts = (p["sh_gate"], p["sh_up"], p["sh_down"], p["final_g"])
    return pl.pallas_call(
        functools.partial(_combine_body, tm=tm, nl=nl),
        grid=(bn // bb, nl),
        in_specs=[smem, smem_next, smem, pl.BlockSpec(memory_space=pl.ANY),
                  pl.BlockSpec((tm * PCH, LANES), lambda b, l: (b * nl + l, 0)),
                  tok, pl.BlockSpec((bb, 6, d), lambda b, l: (b, 0, 0))] + [full(c) for c in consts],
        out_specs=tok,
        out_shape=jax.ShapeDtypeStruct((bn, seq, d), F32),
        scratch_shapes=[pltpu.VMEM((2, tm * TOP_K * PCH, LANES), I32), pltpu.VMEM((tm * CHUNKS, LANES), F32),
                        pltpu.SemaphoreType.DMA((2,))],
        compiler_params=_cparams(("arbitrary", "arbitrary"), VMEM_LIMIT),
        name="moe_combine_final",
    )(dest, dest, wts, ob, h2s, x1, mod, *consts)


def _layer_params(l, ada_w, ada_b, norm1_g, norm2_g, w_in, mu_shift, rw_w0, rw_w_up, rw_a0, rw_a_up, rw_g_up,
                  rw_k_k, rw_k_a, rw_r_k, rw_gn_g, rw_gn_b, gla_a_up, gla_a_bias, gla_norm_g, w_pa, w_pb, w_out,
                  router_w, router_b, exp_gate, exp_up, exp_down, sh_gate, sh_up, sh_down):
    d = D_MODEL
    wi = w_in[l]
    gla0 = RW_SHIFT_COLS
    xal0 = gla0 + QKV_W
    pad = jnp.zeros((d, XAL_W - GLA_GATE_RANK), F32)
    w_pack = jnp.concatenate([wi[:, :xal0], wi[:, xal0:xal0 + GLA_GATE_RANK], pad,
                              wi[:, xal0 + GLA_GATE_RANK:]], axis=1).astype(BF16)
    zr = jnp.zeros((RW_W_RANK, RW_WIDTH), F32)
    hid = jnp.arange(RW_WIDTH) // RW_HEAD
    row = lambda a: a.reshape(1, -1)
    rw_t = router_w[l].T
    rw_hi = rw_t.astype(BF16)
    return dict(
        ada_w=ada_w[l], ada_b=ada_b[l], norm1_g=norm1_g[l].reshape(1, 1, d),
        norm2_g=norm2_g[l].reshape(1, 1, d), w_pack=w_pack,
        mu=mu_shift[l].reshape(1, 1, -1), w0=row(rw_w0[l]), wup=jnp.concatenate([rw_w_up[l], zr], axis=0),
        a0=row(rw_a0[l]), aup=jnp.concatenate([zr, rw_a_up[l]], axis=0), gup=rw_g_up[l].astype(BF16),
        kk=row(rw_k_k[l]), ka=row(rw_k_a[l]), rk=row(rw_r_k[l]),
        bd64=(hid[:, None] == hid[None, :]).astype(BF16),
        gn_g=row(rw_gn_g[l]), gn_b=row(rw_gn_b[l]),
        gla_aup=jnp.concatenate([gla_a_up[l], jnp.zeros((XAL_W - GLA_GATE_RANK, GLA_KW), F32)], axis=0),
        gla_ab=row(gla_a_bias[l]), gla_ng=row(gla_norm_g[l]),
        w_pa=w_pa[l].astype(BF16), w_pb=w_pb[l].astype(BF16), w_out=w_out[l].astype(BF16),
        rw_hi=rw_hi, rw_lo=(rw_t - rw_hi.astype(F32)).astype(BF16), router_b=router_b[l],
        exp_gate=exp_gate[l], exp_up=exp_up[l], exp_down=exp_down[l],
        sh_gate=sh_gate[l].astype(BF16), sh_up=sh_up[l].astype(BF16), sh_down=sh_down[l].astype(BF16),
    )


def _mixer_group(x, mod, s_rw, s_sh, s_gla, p):
    qkv, xal, gg, mg, r, lw, k2, v, a_s, b_s, g, bonus, new_sh = _inproj_prep_call(x, mod, s_sh, p)
    y, rw_new = _rwscan_call(r, lw, k2, v, a_s, b_s, s_rw)
    o_b, gla_new = _gla_call(qkv, xal, gg, s_gla, p)
    x1, h2s, logits = _merge_call(y, g, bonus, o_b, mg, x, mod, p)
    states = (rw_new, new_sh[:, 0, :], gla_new)
    return x1, h2s, logits, states


def _moe(h2s, logits, p):
    tn = h2s.shape[0] // PCH
    eidx, rank, wts, counts = _route_call(logits, p["router_b"])
    counts = counts[:, 0].astype(I32)
    padded = (counts + MOE_BLK - 1) // MOE_BLK * MOE_BLK
    pad_end = jnp.cumsum(padded)
    pad_start = (pad_end - padded).astype(I32)
    nb = (tn * TOP_K + N_EXPERTS * (MOE_BLK - 1)) // MOE_BLK + 1
    tables = _expert_tables(counts, pad_start, pad_end, nb)
    dest = _dest_call(eidx, rank, pad_start)
    xs = _dispatch_call(dest, h2s, nb * MOE_BLK)
    ob = _expert_call(tables, xs, p["exp_gate"], p["exp_up"], p["exp_down"])
    return ob, dest, wts


def kernel(x_prompt, x_sample, c_prompt, c_sample, state_rwkv, state_shift, state_gla, ada_w, ada_b, norm1_g,
           norm2_g, w_in, mu_shift, rw_w0, rw_w_up, rw_a0, rw_a_up, rw_g_up, rw_k_k, rw_k_a, rw_r_k, rw_gn_g,
           rw_gn_b, gla_a_up, gla_a_bias, gla_norm_g, w_pa, w_pb, w_out, router_w, router_b, exp_gate, exp_up,
           exp_down, sh_gate, sh_up, sh_down, final_g):
    depth = ada_w.shape[0]
    bp, bs = x_prompt.shape[0], x_sample.shape[0]
    tp = bp * x_prompt.shape[1]
    xs_g = [x_prompt, x_sample]
    c_all = jnp.concatenate([c_prompt, c_sample], axis=0)
    zeros = lambda shape: jnp.zeros(shape, x_prompt.dtype)
    new_states = [[], []]
    fg = final_g.reshape(1, 1, D_MODEL)
    for l in range(depth):
        p = _layer_params(l, ada_w, ada_b, norm1_g, norm2_g, w_in, mu_shift, rw_w0, rw_w_up, rw_a0, rw_a_up,
                          rw_g_up, rw_k_k, rw_k_a, rw_r_k, rw_gn_g, rw_gn_b, gla_a_up, gla_a_bias, gla_norm_g,
                          w_pa, w_pb, w_out, router_w, router_b, exp_gate, exp_up, exp_down, sh_gate, sh_up,
                          sh_down)
        p["final_g"] = fg
        mod_all = _mod_call(c_all, p["ada_w"], p["ada_b"])
        mods = [mod_all[:bp], mod_all[bp:]]
        states_in = [
            (zeros((bp, RW_HEADS, RW_HEAD, RW_HEAD)), zeros((bp, RW_SHIFT_COLS)),
             zeros((bp, GLA_HEADS, GLA_DK, GLA_DV))),
            (state_rwkv[l], state_shift[l], state_gla[l]),
        ]
        x1s, h2ss, lgs = [], [], []
        for gi in range(2):
            x1, h2s, logits, st = _mixer_group(xs_g[gi], mods[gi], *states_in[gi], p)
            x1s.append(x1)
            h2ss.append(h2s)
            lgs.append(logits)
            new_states[gi].append(st)
        h2_all = jnp.concatenate(h2ss, axis=0)
        ob, dest, wts = _moe(h2_all, jnp.concatenate(lgs, axis=1), p)
        assert depth == 1, "the fused final norm assumes a single layer"
        xs_g = [
            _combine_call(dest, wts, 0, ob, h2ss[0], x1s[0], mods[0], p),
            _combine_call(dest, wts, tp, ob, h2ss[1], x1s[1], mods[1], p),
        ]
    stack = lambda gi, j: jnp.stack([s[j] for s in new_states[gi]])
    return (xs_g[0], xs_g[1], stack(0, 0), stack(0, 1), stack(0, 2), stack(1, 0), stack(1, 1), stack(1, 2))
```

```python
import functools

import jax
import jax.numpy as jnp
from jax import lax
from jax.experimental import pallas as pl
from jax.experimental.pallas import tpu as pltpu

F32, BF16, I32 = jnp.float32, jnp.bfloat16, jnp.int32

D_MODEL = 1024
RW_HEADS, RW_HEAD = 8, 64
RW_WIDTH = RW_HEADS * RW_HEAD
RW_W_RANK, RW_A_RANK, RW_G_RANK = 64, 64, 128
RW_GN_EPS = 64e-5
GLA_HEADS, GLA_DK, GLA_DV = 4, 64, 128
GLA_KW, GLA_VW = GLA_HEADS * GLA_DK, GLA_HEADS * GLA_DV
GLA_GATE_RANK = 16
GLA_GATE_TAU = 16.0
GLA_CHUNK = 16
RW_SHIFT_COLS = 3 * RW_WIDTH + RW_W_RANK + RW_A_RANK + RW_G_RANK
N_EXPERTS, TOP_K, N_GROUPS, TOPK_GROUPS = 256, 8, 8, 4
GROUP_SIZE = N_EXPERTS // N_GROUPS
EXPERT_FF = 256
ROUTED_SCALE = 2.5
NORM_EPS = 1e-6

LANES = 128
SUBLANES = 8
CHUNKS = D_MODEL // LANES
PCH = CHUNKS // 2
UNIT = 64
RW_SCAN_PASSES = (1, 1, 1, 1, 1)
GLA_UNITS_PER_STEP = 4
RW_UNITS_PER_STEP = 4
VMEM_LIMIT = 56 * 1024 * 1024

PA_W, QKV_W, XAL_W, GG_W, MG_W = RW_SHIFT_COLS, 2 * GLA_KW + GLA_VW, LANES, GLA_VW, 2 * D_MODEL
PACK_OFFS = (0, PA_W, PA_W + QKV_W, PA_W + QKV_W + XAL_W, PA_W + QKV_W + XAL_W + GG_W)
PACK_W = PA_W + QKV_W + XAL_W + GG_W + MG_W

TOK_TILE = 256
MOE_BLK = 256
EXPERT_PARTS = 1
WEIGHT_SLOTS = 3
ROW_SLOTS = 3
CMB_TILE = 256

_DN = {
    "nn": (((1,), (0,)), ((), ())),
    "nt": (((1,), (1,)), ((), ())),
    "tn": (((0,), (0,)), ((), ())),
}


def _split(x, pieces):
    out, rem = [], x
    for i in range(pieces):
        p = rem.astype(BF16)
        out.append(p)
        if i + 1 < pieces:
            rem = rem - p.astype(F32)
    return out


def _mm(a, b, form="nn", passes=1):
    dn = _DN[form]
    if passes == 6:
        return lax.dot_general(a.astype(F32), b.astype(F32), dn, precision=lax.Precision.HIGHEST,
                               preferred_element_type=F32)
    if passes == 1:
        return lax.dot_general(a.astype(BF16), b.astype(BF16), dn, preferred_element_type=F32)
    ah, al = _split(a, 2)
    bh, bl = _split(b, 2)
    out = lax.dot_general(ah, bh, dn, preferred_element_type=F32)
    out = out + lax.dot_general(ah, bl, dn, preferred_element_type=F32)
    return out + lax.dot_general(al, bh, dn, preferred_element_type=F32)


def _mm01(m01, x, pieces=3):
    m = m01.astype(BF16)
    out = None
    for p in _split(x, pieces):
        t = lax.dot_general(m, p, _DN["nn"], preferred_element_type=F32)
        out = t if out is None else out + t
    return out


def _xmm01(x, m01, pieces=2):
    m = m01.astype(BF16)
    out = None
    for p in _split(x, pieces):
        t = lax.dot_general(p, m, _DN["nn"], preferred_element_type=F32)
        out = t if out is None else out + t
    return out


HI16 = -65536


def _bf16_bits(x):
    return lax.bitcast_convert_type(x.astype(BF16).astype(F32), I32)


def _unpack_pair(w):
    return lax.bitcast_convert_type(w << 16, F32), lax.bitcast_convert_type(w & HI16, F32)


def _rows_to_packed(ref, x, first=0):
    for c in range(PCH):
        lo = _bf16_bits(x[:, c * LANES:(c + 1) * LANES])
        hi = _bf16_bits(x[:, (c + PCH) * LANES:(c + PCH + 1) * LANES])
        ref[pl.ds(first * PCH + c, x.shape[0], stride=PCH), :] = ((lo >> 16) & 0xFFFF) | (hi & HI16)


def _rows_from_packed(ref, n, live=None, first=0):
    lows, highs = [], []
    for c in range(PCH):
        w = ref[pl.ds(first * PCH + c, n, stride=PCH), :]
        if live is not None:
            w = jnp.where(live, w, 0)
        lo, hi = _unpack_pair(w)
        lows.append(lo.astype(BF16))
        highs.append(hi.astype(BF16))
    return jnp.concatenate(lows + highs, axis=1)


def _slab(ref, row):
    return ref.at[pl.ds(pl.multiple_of(row * PCH, PCH), PCH)]


def _fslab(ref, row):
    return ref.at[pl.ds(pl.multiple_of(row * CHUNKS, CHUNKS), CHUNKS)]


def _sigmoid(x):
    return 1.0 / (1.0 + jnp.exp(-x))


def _softplus(x):
    return jnp.maximum(x, 0.0) + jnp.log(1.0 + jnp.exp(-jnp.abs(x)))


def _log2(n):
    assert n > 0 and n & (n - 1) == 0, n
    return n.bit_length() - 1


def _cparams(sem, vmem=None):
    return pltpu.CompilerParams(dimension_semantics=sem, vmem_limit_bytes=vmem)


def _mod_body(c_ref, w_ref, b_ref, o_ref):
    c = c_ref[...]
    o_ref[0] = _mm(c * _sigmoid(c), w_ref[...], passes=6) + b_ref[...]


def _mod_call(c_all, ada_w, ada_b):
    bt, d = c_all.shape
    out = pl.pallas_call(
        _mod_body,
        grid=(6,),
        in_specs=[pl.BlockSpec((bt, d), lambda k: (0, 0)),
                  pl.BlockSpec((d, d), lambda k: (0, k)),
                  pl.BlockSpec((1, d), lambda k: (0, k))],
        out_specs=pl.BlockSpec((1, bt, d), lambda k: (k, 0, 0)),
        out_shape=jax.ShapeDtypeStruct((6, bt, d), F32),
        compiler_params=_cparams(("arbitrary",)),
        name="adaln_mod",
    )(c_all, ada_w, ada_b.reshape(1, 6 * d))
    return jnp.transpose(out, (1, 0, 2))


def _inproj_body(x_ref, mod_ref, g_ref, w_ref, pa_ref, qkv_ref, xal_ref, gg_ref, mg_ref):
    bb, ll, d = x_ref.shape
    x = x_ref[...]
    y = x * lax.rsqrt(jnp.mean(x * x, axis=-1, keepdims=True) + NORM_EPS) * g_ref[...]
    h = y * (1.0 + mod_ref[:, 1:2, :]) + mod_ref[:, 0:1, :]
    hb = h.reshape(bb * ll, d).astype(BF16)
    for ref, off in zip((pa_ref, qkv_ref, xal_ref, gg_ref, mg_ref), PACK_OFFS):
        w = ref.shape[-1]
        ref[...] = jnp.dot(hb, w_ref[:, off:off + w], preferred_element_type=F32).reshape(bb, ll, w)


def _tile(bn, seq, tile):
    if seq >= tile:
        assert seq % tile == 0
        return 1, tile
    assert tile % seq == 0 and bn % (tile // seq) == 0
    return tile // seq, seq


def _rwprep_body(pa_ref, sh_ref, mu_ref, w0_ref, wup_ref, a0_ref, aup_ref, gup_ref, kk_ref, ka_ref, rk_ref,
                 bd_ref, r_o, lw_o, k_o, v_o, a_o, b_o, g_o, bon_o, nsh_o, carry):
    bb, ll, wd = pa_ref.shape
    n = bb * ll
    hw = RW_WIDTH

    @pl.when(pl.program_id(1) == 0)
    def _():
        carry[...] = sh_ref[...]

    pa = pa_ref[...]
    rolled = pltpu.roll(pa.reshape(n, wd), 1, 0).reshape(bb, ll, wd)
    tok = lax.broadcasted_iota(I32, (bb, ll, wd), 1)
    prev = jnp.where(tok == 0, carry[...], rolled)
    last = pa_ref[:, ll - 1:ll, :]
    carry[...] = last
    nsh_o[...] = last
    xs = (pa + (prev - pa) * mu_ref[...]).reshape(n, wd)

    r, k, v = xs[:, 0:hw], xs[:, hw:2 * hw], xs[:, 2 * hw:3 * hw]
    xwa = xs[:, 3 * hw:3 * hw + LANES]
    xg = xs[:, 3 * hw + LANES:]
    w_log = -_softplus(-(w0_ref[...] + _mm(jnp.tanh(xwa), wup_ref[...], passes=3))) - 0.5
    lw = -jnp.exp(w_log)
    a = _sigmoid(a0_ref[...] + _mm(xwa, aup_ref[...], passes=3))
    g = _mm(_sigmoid(xg), gup_ref[...])
    bd = bd_ref[...]
    kkv = k * kk_ref[...]
    kkn = kkv * lax.rsqrt(jnp.maximum(_xmm01(kkv * kkv, bd, pieces=1), 1e-24))
    k2 = k * (1.0 + (a - 1.0) * ka_ref[...])
    bonus = _xmm01(r * k2 * rk_ref[...], bd, pieces=1) * v
    for ref, val in ((r_o, r), (lw_o, lw), (k_o, k2), (v_o, v), (a_o, -kkn), (b_o, kkn * a), (g_o, g),
                     (bon_o, bonus)):
        ref[...] = val.reshape(bb, ll, hw)


def _inproj_prep_body(x_ref, mod_ref, g_ref, w_ref, sh_ref, mu_ref, w0_ref, wup_ref, a0_ref, aup_ref, gup_ref, kk_ref,
                      ka_ref, rk_ref, bd_ref, qkv_o, xal_o, gg_o, mg_o, r_o, lw_o, k_o, v_o, a_o, b_o, g_o, bon_o,
                      nsh_o, pa_s, carry):
    _inproj_body(x_ref, mod_ref, g_ref, w_ref, pa_s, qkv_o, xal_o, gg_o, mg_o)
    _rwprep_body(pa_s, sh_ref, mu_ref, w0_ref, wup_ref, a0_ref, aup_ref, gup_ref, kk_ref, ka_ref, rk_ref, bd_ref,
                 r_o, lw_o, k_o, v_o, a_o, b_o, g_o, bon_o, nsh_o, carry)


def _inproj_prep_call(x, mod, s_sh, p):
    bn, seq, d = x.shape
    bb, ll = _tile(bn, seq, TOK_TILE)
    hw, wd = RW_WIDTH, PA_W
    tok = lambda w: pl.BlockSpec((bb, ll, w), lambda b, l: (b, l, 0))
    row = lambda w: pl.BlockSpec((bb, 1, w), lambda b, l: (b, 0, 0))
    full = lambda a: pl.BlockSpec(a.shape, lambda b, l: (0,) * a.ndim)
    consts = (p["mu"], p["w0"], p["wup"], p["a0"], p["aup"], p["gup"], p["kk"], p["ka"], p["rk"], p["bd64"])
    proj_w = (QKV_W, XAL_W, GG_W, MG_W)
    shapes = lambda ws: [jax.ShapeDtypeStruct((bn, seq, w), F32) for w in ws]
    return pl.pallas_call(
        _inproj_prep_body,
        grid=(bn // bb, seq // ll),
        in_specs=[tok(d), pl.BlockSpec((bb, 6, d), lambda b, l: (b, 0, 0)), full(p["norm1_g"]), full(p["w_pack"]),
                  row(wd)] + [full(c) for c in consts],
        out_specs=[tok(w) for w in proj_w] + [tok(hw)] * 8 + [row(wd)],
        out_shape=shapes(proj_w) + shapes((hw,) * 8) + [jax.ShapeDtypeStruct((bn, 1, wd), F32)],
        scratch_shapes=[pltpu.VMEM((bb, ll, wd), F32), pltpu.VMEM((bb, 1, wd), F32)],
        compiler_params=_cparams(("arbitrary", "arbitrary"), VMEM_LIMIT),
        name="norm_inproj_prep",
    )(x, mod, p["norm1_g"], p["w_pack"], s_sh.reshape(bn, 1, wd), *consts)


def _unit_masks(n, tl):
    ri = lax.broadcasted_iota(I32, (n, n), 0)
    ci = lax.broadcasted_iota(I32, (n, n), 1)
    same = (ri >> _log2(tl)) == (ci >> _log2(tl))
    return same, same & (ri > ci), same & (ri >= ci)


def _rwscan_body(r_ref, lw_ref, k_ref, v_ref, a_ref, b_ref, s0_ref, y_ref, sn_ref, st, *, nu, nseq, tl, passes):
    n = nseq * tl
    n2 = 2 * n
    p_aa, p_inv, p_apply, p_state, p_y = passes

    hd = RW_HEAD

    @pl.when(pl.program_id(1) == 0)
    def _():
        zero = jnp.zeros((hd, hd), F32)
        for q in range(nu * nseq):
            for p in range(RW_HEADS // 2):
                st[q, p] = jnp.concatenate(
                    [jnp.concatenate([s0_ref[q, 2 * p], zero], axis=1),
                     jnp.concatenate([zero, s0_ref[q, 2 * p + 1]], axis=1)], axis=0)

    same, _, incl = _unit_masks(n, tl)
    m_cum = jnp.where(incl, 1.0, 0.0)
    m_seq = jnp.where(same, 1.0, 0.0)
    ri = lax.broadcasted_iota(I32, (n2, n2), 0)
    ci = lax.broadcasted_iota(I32, (n2, n2), 1)
    rt, ct = ri & (n - 1), ci & (n - 1)
    dsame = ((rt >> _log2(tl)) == (ct >> _log2(tl))) & ((ri >> _log2(n)) == (ci >> _log2(n)))
    strict_d = dsame & (rt > ct)
    incl_d = dsame & (rt >= ct)
    eye_d = jnp.where(ri == ci, 1.0, 0.0)
    lane = lax.broadcasted_iota(I32, (1, LANES), 1)
    m0 = jnp.where(lane < RW_HEAD, 1.0, 0.0)
    m1 = 1.0 - m0

    def dup(x):
        return jnp.concatenate([x * m0, x * m1], axis=0)

    def seq_rows(x, q):
        if nseq == 1:
            return x
        return jnp.concatenate([x[q * tl:(q + 1) * tl], x[n + q * tl:n + (q + 1) * tl]], axis=0)

    def unit_rows(parts):
        if nseq == 1:
            return parts[0]
        return jnp.concatenate([p[0:tl] for p in parts] + [p[tl:2 * tl] for p in parts], axis=0)

    chains = [(u, p) for u in range(nu) for p in range(RW_HEADS // 2)]
    ids = range(len(chains))
    cat0 = lambda *xs: jnp.concatenate(xs, axis=0)

    def ld(ref, c):
        u, p = chains[c]
        return ref[u * nseq:(u + 1) * nseq, :, p * LANES:(p + 1) * LANES].reshape(n, LANES)

    lw = [ld(lw_ref, c) for c in ids]
    cum = [_mm01(m_cum, x) for x in lw]
    tot = [_mm01(m_seq, x) for x in lw]
    e_c = [jnp.exp(x) for x in cum]
    e_n = [jnp.exp(-x) for x in cum]
    e_l = [jnp.exp(t - x) for t, x in zip(tot, cum)]
    at_d = [dup(ld(a_ref, c) * jnp.exp(cum[c] - lw[c])) for c in ids]
    rt_d = [dup(ld(r_ref, c) * e_c[c]) for c in ids]
    bt_d = [dup(ld(b_ref, c) * e_n[c]) for c in ids]
    kt_d = [dup(ld(k_ref, c) * e_n[c]) for c in ids]
    bh_d = [dup(ld(b_ref, c) * e_l[c]) for c in ids]
    kh_d = [dup(ld(k_ref, c) * e_l[c]) for c in ids]
    v_d = [dup(ld(v_ref, c)) for c in ids]
    aa = [_mm(cat0(at_d[c], rt_d[c]), cat0(bt_d[c], kt_d[c]), "nt", p_aa) for c in ids]
    a_ab = [jnp.where(strict_d, x[0:n2, 0:n2], 0.0) for x in aa]
    a_ak = [jnp.where(strict_d, x[0:n2, n2:], 0.0) for x in aa]
    a_rb = [jnp.where(incl_d, x[n2:, 0:n2], 0.0) for x in aa]
    a_rk = [jnp.where(incl_d, x[n2:, n2:], 0.0) for x in aa]
    zy = [_mm(cat0(a_ak[c], a_rk[c]), v_d[c], passes=p_apply) for c in ids]
    tinv = [eye_d + x for x in a_ab]
    nk = a_ab
    for _ in range(_log2(tl) - 1):
        nk = [_mm(x, x, passes=p_inv) for x in nk]
        tinv = [t + _mm(t, x, passes=p_inv) for t, x in zip(tinv, nk)]
    wu = [_mm(tinv[c], jnp.concatenate([at_d[c], zy[c][0:n2]], axis=1), passes=p_apply) for c in ids]
    seqs = range(nseq)
    srow = lambda c, q: (chains[c][0] * nseq + q, chains[c][1])
    s_old = [[st[srow(c, q)] for q in seqs] for c in ids]
    xs = [[_mm(cat0(seq_rows(wu[c][:, 0:LANES], q), seq_rows(rt_d[c], q)), s_old[c][q], "nt", p_state)
           for q in seqs] for c in ids]
    u_q = [[xs[c][q][0:2 * tl] + seq_rows(wu[c][:, LANES:], q) for q in seqs] for c in ids]
    for c in ids:
        for q in seqs:
            g_c = jnp.exp(tot[c][q * tl:q * tl + 1, :])
            st[srow(c, q)] = s_old[c][q] * g_c + _mm(cat0(u_q[c][q], seq_rows(v_d[c], q)),
                                                     cat0(seq_rows(bh_d[c], q), seq_rows(kh_d[c], q)), "tn", p_state)
    for c in ids:
        u, p = chains[c]
        y_d = (unit_rows([xs[c][q][2 * tl:] for q in seqs]) + _mm(a_rb[c], unit_rows(u_q[c]), passes=p_y)
               + zy[c][n2:])
        y_ref[u * nseq:(u + 1) * nseq, :, p * LANES:(p + 1) * LANES] = (y_d[0:n] + y_d[n:]).reshape(nseq, tl, LANES)

    @pl.when(pl.program_id(1) == pl.num_programs(1) - 1)
    def _():
        for q in range(nu * nseq):
            for p in range(RW_HEADS // 2):
                s = st[q, p]
                sn_ref[q, 2 * p] = s[0:hd, 0:hd]
                sn_ref[q, 2 * p + 1] = s[hd:, hd:]


def _unit_shape(bn, seq):
    if seq >= UNIT:
        assert seq % UNIT == 0
        return 1, UNIT
    assert UNIT % seq == 0 and bn % (UNIT // seq) == 0
    return UNIT // seq, seq


def _rwscan_call(r, lw, k2, v, a_s, b_s, s0, passes=RW_SCAN_PASSES):
    bn, seq, hw = r.shape
    nseq, tl = _unit_shape(bn, seq)
    nu = RW_UNITS_PER_STEP if bn % (RW_UNITS_PER_STEP * nseq) == 0 else 1
    rows = nu * nseq
    tok = pl.BlockSpec((rows, tl, hw), lambda b, c: (b, c, 0))
    stt = pl.BlockSpec((rows, RW_HEADS, RW_HEAD, RW_HEAD), lambda b, c: (b, 0, 0, 0))
    return pl.pallas_call(
        functools.partial(_rwscan_body, nu=nu, nseq=nseq, tl=tl, passes=passes),
        grid=(bn // rows, seq // tl),
        in_specs=[tok] * 6 + [stt],
        out_specs=[tok, stt],
        out_shape=[jax.ShapeDtypeStruct((bn, seq, hw), F32), jax.ShapeDtypeStruct(s0.shape, F32)],
        scratch_shapes=[pltpu.VMEM((rows, RW_HEADS // 2, LANES, LANES), F32)],
        compiler_params=_cparams(("arbitrary", "arbitrary"), VMEM_LIMIT),
        name="rwkv_scan",
    )(r, lw, k2, v, a_s, b_s, s0)


def _gla_body(qkv_ref, xal_ref, gate_ref, aup_ref, ab_ref, ng_ref, s0_ref, o_ref, sn_ref, st, *, nu, nseq, tl, cs):
    n = nseq * tl
    n2 = 2 * n
    nsub = tl // cs

    @pl.when(pl.program_id(1) == 0)
    def _():
        zero = jnp.zeros((GLA_DV, GLA_DK), F32)
        for q in range(nu * nseq):
            for p in range(GLA_HEADS // 2):
                st[q, p] = jnp.concatenate(
                    [jnp.concatenate([s0_ref[q, 2 * p].T, zero], axis=1),
                     jnp.concatenate([zero, s0_ref[q, 2 * p + 1].T], axis=1)], axis=0)

    same, _, incl = _unit_masks(n, cs)
    m_cum = jnp.where(incl, 1.0, 0.0)
    m_sub = jnp.where(same, 1.0, 0.0)
    ri = lax.broadcasted_iota(I32, (n2, n2), 0)
    ci = lax.broadcasted_iota(I32, (n2, n2), 1)
    rt, ct = ri & (n - 1), ci & (n - 1)
    causal_d = ((rt >> _log2(cs)) == (ct >> _log2(cs))) & ((ri >> _log2(n)) == (ci >> _log2(n))) & (rt >= ct)
    lane = lax.broadcasted_iota(I32, (1, LANES), 1)
    m0 = jnp.where(lane < GLA_DK, 1.0, 0.0)
    m1 = 1.0 - m0
    sr = lax.broadcasted_iota(I32, (2 * GLA_DV, LANES), 0)
    sc = lax.broadcasted_iota(I32, (2 * GLA_DV, LANES), 1)
    st_mask = jnp.where((sr >> _log2(GLA_DV)) == (sc >> _log2(GLA_DK)), 1.0, 0.0)

    def dup(x):
        return jnp.concatenate([x * m0, x * m1], axis=0)

    chains = [(u, p) for u in range(nu) for p in range(GLA_HEADS // 2)]
    ids = range(len(chains))
    urows = lambda u: slice(u * nseq, (u + 1) * nseq)
    ng = ng_ref[...]
    la_all = [-_softplus(-(_mm(xal_ref[urows(u), :, :].reshape(n, LANES), aup_ref[...], passes=3) + ab_ref[...]))
              * (1.0 / GLA_GATE_TAU) for u in range(nu)]

    def ld(ref, c, off, width):
        return ref[urows(chains[c][0]), :, off:off + width].reshape(n, width)

    q = [ld(qkv_ref, c, chains[c][1] * LANES, LANES) * (GLA_DK ** -0.5) for c in ids]
    k = [ld(qkv_ref, c, GLA_KW + chains[c][1] * LANES, LANES) for c in ids]
    vp = [ld(qkv_ref, c, 2 * GLA_KW + chains[c][1] * 2 * GLA_DV, 2 * GLA_DV) for c in ids]
    la = [la_all[u][:, p * LANES:(p + 1) * LANES] for u, p in chains]
    bc = [_mm01(m_cum, x) for x in la]
    bl = [_mm01(m_sub, x) for x in la]
    qe = [q[c] * jnp.exp(bc[c]) for c in ids]
    ke = [k[c] * jnp.exp(-bc[c]) for c in ids]
    kd = [k[c] * jnp.exp(bl[c] - bc[c]) for c in ids]
    att = [jnp.where(causal_d, _mm(dup(qe[c]), dup(ke[c]), "nt", passes=1), 0.0) for c in ids]
    v_st = [jnp.concatenate([x[:, 0:GLA_DV], x[:, GLA_DV:]], axis=0) for x in vp]
    o_st = [_mm(att[c], v_st[c], passes=1) for c in ids]
    upd = [[_mm(vp[c][r0:r0 + cs], kd[c][r0:r0 + cs], "tn", passes=1) for r0 in range(0, n, cs)] for c in ids]
    inter = [[None] * (n // cs) for _ in ids]
    for sq in range(nseq):
        s = [st[chains[c][0] * nseq + sq, chains[c][1]] for c in ids]
        for j in range(nsub):
            i = sq * nsub + j
            r0 = i * cs
            for c in ids:
                inter[c][i] = _mm(qe[c][r0:r0 + cs], s[c], "nt", passes=1)
                s[c] = s[c] * jnp.exp(bl[c][r0:r0 + 1, :]) + st_mask * upd[c][i]
        for c in ids:
            st[chains[c][0] * nseq + sq, chains[c][1]] = s[c]
    for c in ids:
        u, p = chains[c]
        o = o_st[c] + jnp.concatenate([x[:, 0:GLA_DV] for x in inter[c]] + [x[:, GLA_DV:] for x in inter[c]], axis=0)
        o = o * lax.rsqrt(jnp.mean(o * o, axis=-1, keepdims=True) + NORM_EPS) * ng
        goff = p * 2 * GLA_DV
        gp = ld(gate_ref, c, goff, 2 * GLA_DV)
        g_st = jnp.concatenate([gp[:, 0:GLA_DV], gp[:, GLA_DV:]], axis=0)
        ob = o * (g_st * _sigmoid(g_st))
        o_ref[urows(u), :, goff:goff + GLA_DV] = ob[0:n].reshape(nseq, tl, GLA_DV)
        o_ref[urows(u), :, goff + GLA_DV:goff + 2 * GLA_DV] = ob[n:].reshape(nseq, tl, GLA_DV)

    @pl.when(pl.program_id(1) == pl.num_programs(1) - 1)
    def _():
        for q in range(nu * nseq):
            for p in range(GLA_HEADS // 2):
                s = st[q, p]
                sn_ref[q, 2 * p] = s[0:GLA_DV, 0:GLA_DK].T
                sn_ref[q, 2 * p + 1] = s[GLA_DV:, GLA_DK:].T


def _gla_call(qkv, xal, gate, s0, p):
    bn, seq, _ = qkv.shape
    nseq, tl = _unit_shape(bn, seq)
    cs = min(GLA_CHUNK, seq)
    assert tl % cs == 0
    nu = GLA_UNITS_PER_STEP if bn % (GLA_UNITS_PER_STEP * nseq) == 0 else 1
    rows = nu * nseq
    tok = lambda w: pl.BlockSpec((rows, tl, w), lambda b, c: (b, c, 0))
    full = lambda a: pl.BlockSpec(a.shape, lambda b, c: (0,) * a.ndim)
    stt = pl.BlockSpec((rows, GLA_HEADS, GLA_DK, GLA_DV), lambda b, c: (b, 0, 0, 0))
    consts = (p["gla_aup"], p["gla_ab"], p["gla_ng"])
    return pl.pallas_call(
        functools.partial(_gla_body, nu=nu, nseq=nseq, tl=tl, cs=cs),
        grid=(bn // rows, seq // tl),
        in_specs=[tok(QKV_W), tok(XAL_W), tok(GG_W)] + [full(c) for c in consts] + [stt],
        out_specs=[tok(GLA_VW), stt],
        out_shape=[jax.ShapeDtypeStruct((bn, seq, GLA_VW), F32), jax.ShapeDtypeStruct(s0.shape, F32)],
        scratch_shapes=[pltpu.VMEM((rows, GLA_HEADS // 2, 2 * GLA_DV, LANES), F32)],
        compiler_params=_cparams(("arbitrary", "arbitrary"), VMEM_LIMIT),
        name="gla_chunked",
    )(qkv, xal, gate, *consts, s0)


def _merge_body(y_ref, g_ref, bon_ref, ob_ref, mg_ref, x_ref, mod_ref, gng_ref, gnb_ref, bd_ref, wpa_ref,
                wpb_ref, wout_ref, n2_ref, rwh_ref, rwl_ref, *rest):
    x1_o, h2_o, lg_o = rest[-3:]
    bb, ll, d = x_ref.shape
    n = bb * ll
    hw = RW_WIDTH
    bd = bd_ref[...]
    y = y_ref[...].reshape(n, hw)
    mu = _xmm01(y, bd, pieces=2) * (1.0 / RW_HEAD)
    dv = y - mu
    var = _xmm01(dv * dv, bd, pieces=1) * (1.0 / RW_HEAD)
    yn = dv * lax.rsqrt(var + RW_GN_EPS) * gng_ref[...] + gnb_ref[...]
    o_a = (yn + bon_ref[...].reshape(n, hw)) * g_ref[...].reshape(n, hw)
    o_b = ob_ref[...].reshape(n, GLA_VW)
    mg = mg_ref[...].reshape(n, 2 * d)
    merged = _sigmoid(mg[:, 0:d]) * _mm(o_a, wpa_ref[...]) + _sigmoid(mg[:, d:]) * _mm(o_b, wpb_ref[...])
    mix = _mm(merged, wout_ref[...]).reshape(bb, ll, d)
    x1 = x_ref[...] + mod_ref[:, 2:3, :] * mix
    x1_o[...] = x1
    yn2 = x1 * lax.rsqrt(jnp.mean(x1 * x1, axis=-1, keepdims=True) + NORM_EPS) * n2_ref[...]
    h2 = (yn2 * (1.0 + mod_ref[:, 4:5, :]) + mod_ref[:, 3:4, :]).reshape(n, d)
    hh, hl = _split(h2, 2)
    rwh, rwl = rwh_ref[...], rwl_ref[...]
    nt = lambda a, b: lax.dot_general(a, b, _DN["nt"], preferred_element_type=F32)
    lg_o[...] = nt(rwh, hh) + nt(rwl, hh) + nt(rwh, hl)
    _rows_to_packed(h2_o, h2)


def _merge_call(y, g, bonus, o_b, mg, x, mod, p, tn, first_tok, shared):
    bn, seq, d = x.shape
    bb, ll = _tile(bn, seq, TOK_TILE)
    nl = seq // ll
    assert first_tok % (bb * ll) == 0
    t0 = first_tok // (bb * ll)
    n_in = 7 + 9
    extra = [] if shared is None else list(shared)
    alias = {} if shared is None else {n_in: 1, n_in + 1: 2}
    tok = lambda w: pl.BlockSpec((bb, ll, w), lambda b, l: (b, l, 0))
    full = lambda a: pl.BlockSpec(a.shape, lambda b, l: (0,) * a.ndim)
    consts = (p["gn_g"], p["gn_b"], p["bd64"], p["w_pa"], p["w_pb"], p["w_out"], p["norm2_g"], p["rw_hi"],
              p["rw_lo"])
    return pl.pallas_call(
        _merge_body,
        grid=(bn // bb, nl),
        in_specs=[tok(RW_WIDTH)] * 3 + [tok(GLA_VW), tok(MG_W), tok(d),
                                        pl.BlockSpec((bb, 6, d), lambda b, l: (b, 0, 0))] + [full(c) for c in consts]
        + [pl.BlockSpec(memory_space=pl.ANY)] * len(extra),
        out_specs=[tok(d),
                   pl.BlockSpec((bb * ll * PCH, LANES), lambda b, l: (t0 + b * nl + l, 0)),
                   pl.BlockSpec((N_EXPERTS, bb * ll), lambda b, l: (0, t0 + b * nl + l))],
        out_shape=[jax.ShapeDtypeStruct((bn, seq, d), F32),
                   jax.ShapeDtypeStruct((tn * PCH, LANES), I32),
                   jax.ShapeDtypeStruct((N_EXPERTS, tn), F32)],
        input_output_aliases=alias,
        compiler_params=_cparams(("arbitrary", "arbitrary"), VMEM_LIMIT),
        name="merge_outproj_router",
    )(y, g, bonus, o_b, mg, x, mod, *consts, *extra)


def _route_body(lg_ref, rb_ref, e_o, rk_o, w_o, cnt_o, carry):
    ne, tm = lg_ref.shape

    @pl.when(pl.program_id(0) == 0)
    def _():
        carry[...] = jnp.zeros_like(carry)

    neg = -jnp.inf
    scores = _sigmoid(lg_ref[...])
    sel = scores + rb_ref[...]
    row_i = lax.broadcasted_iota(I32, (ne, tm), 0)
    row = row_i.astype(F32)
    grp = (row_i >> _log2(GROUP_SIZE)).astype(F32)

    def first_max(x, ids, none):
        m = jnp.max(x, axis=0, keepdims=True)
        return m, jnp.min(jnp.where(x == m, ids, none), axis=0, keepdims=True)

    gs = []
    gids = lax.broadcasted_iota(I32, (GROUP_SIZE, tm), 0)
    for gidx in range(N_GROUPS):
        rows = slice(gidx * GROUP_SIZE, (gidx + 1) * GROUP_SIZE)
        sg = _sigmoid(lg_ref[rows, :]) + rb_ref[rows, :]
        ids = (gids + gidx * GROUP_SIZE).astype(F32)
        m1, i1 = first_max(sg, ids, float(ne))
        gs.append(m1 + jnp.max(jnp.where(ids == i1, neg, sg), axis=0, keepdims=True))
    gs = jnp.concatenate(gs, axis=0)
    gid = lax.broadcasted_iota(I32, (N_GROUPS, tm), 0).astype(F32)
    cur = jnp.full((ne, tm), neg, F32)
    for _ in range(TOPK_GROUPS):
        _, gi = first_max(gs, gid, float(N_GROUPS))
        cur = jnp.where(grp == gi, sel, cur)
        gs = jnp.where(gid == gi, neg, gs)

    pm = jnp.zeros((ne, tm), F32)
    eidx, wts = [], []
    for _ in range(TOP_K):
        _, ei = first_max(cur, row, float(ne))
        hit = row == ei
        pm = jnp.where(hit, 1.0, pm)
        eidx.append(ei)
        wts.append(jnp.sum(jnp.where(hit, scores, 0.0), axis=0, keepdims=True))
        cur = jnp.where(hit, neg, cur)
    wsum = wts[0]
    for w in wts[1:]:
        wsum = wsum + w

    ri = lax.broadcasted_iota(I32, (tm, tm), 0)
    ci = lax.broadcasted_iota(I32, (tm, tm), 1)
    earlier = jnp.where(ri < ci, 1.0, 0.0)
    rank = _mm(pm, earlier, passes=1) + carry[...]
    carry[...] = carry[...] + jnp.sum(pm, axis=1, keepdims=True)
    cnt_o[...] = carry[...]

    rks = [jnp.sum(jnp.where(row == e, rank, 0.0), axis=0, keepdims=True) for e in eidx]
    e_o[0] = jnp.concatenate(eidx, axis=0).astype(I32)
    rk_o[0] = jnp.concatenate(rks, axis=0).astype(I32)
    w_o[0] = jnp.concatenate([w / wsum * ROUTED_SCALE for w in wts], axis=0)


def _route_call(logits_t, router_b):
    ne, tn = logits_t.shape
    tm = TOK_TILE
    assert tn % tm == 0
    col = pl.BlockSpec((ne, 1), lambda i: (0, 0))
    tab = pl.BlockSpec((1, TOP_K, tm), lambda i: (i, 0, 0))
    tab_shape = (tn // tm, TOP_K, tm)
    return pl.pallas_call(
        _route_body,
        grid=(tn // tm,),
        in_specs=[pl.BlockSpec((ne, tm), lambda i: (0, i)), col],
        out_specs=[tab, tab, tab, col],
        out_shape=[jax.ShapeDtypeStruct(tab_shape, I32), jax.ShapeDtypeStruct(tab_shape, I32),
                   jax.ShapeDtypeStruct(tab_shape, F32), jax.ShapeDtypeStruct((ne, 1), F32)],
        scratch_shapes=[pltpu.VMEM((ne, 1), F32)],
        compiler_params=_cparams(("arbitrary",)),
        name="moe_route",
    )(logits_t, router_b.reshape(ne, 1))


def _dest_body(e_ref, rk_ref, ps_ref, d_o):
    ne, tm = ps_ref.shape[0], e_ref.shape[2]
    ids = lax.broadcasted_iota(I32, (ne, tm), 0)
    ps = ps_ref[...]
    first = [jnp.sum(jnp.where(ids == e_ref[0, kk:kk + 1, :], ps, 0.0), axis=0, keepdims=True)
             for kk in range(TOP_K)]
    d_o[0] = (jnp.concatenate(first, axis=0).astype(I32) + rk_ref[0]) * PCH


def _dest_call(eidx, rank, pad_start):
    nt, _, tm = eidx.shape
    ne = pad_start.shape[0]
    tab = pl.BlockSpec((1, TOP_K, tm), lambda i: (i, 0, 0))
    return pl.pallas_call(
        _dest_body,
        grid=(nt,),
        in_specs=[tab, tab, pl.BlockSpec((ne, 1), lambda i: (0, 0))],
        out_specs=tab,
        out_shape=jax.ShapeDtypeStruct(eidx.shape, I32),
        compiler_params=_cparams(("arbitrary",)),
        name="moe_dest",
    )(eidx, rank, pad_start.astype(F32).reshape(ne, 1))


def _pslab(ref, offset):
    return ref.at[pl.ds(pl.multiple_of(offset, PCH), PCH)]


def _dispatch_body(d_ref, h2_ref, xs_hbm, sem, *, tm):
    def issue(m, carry):
        for kk in range(TOP_K):
            pltpu.make_async_copy(_slab(h2_ref, m), _pslab(xs_hbm, d_ref[0, kk, m]), sem).start(priority=kk % 2)
        return carry

    lax.fori_loop(0, tm, issue, 0)
    all_rows = xs_hbm.at[pl.ds(0, tm * TOP_K * PCH)]
    pltpu.make_async_copy(all_rows, all_rows, sem).wait()


def _assign_spec(tm, index_map):
    return pl.BlockSpec((1, TOP_K, tm), index_map, memory_space=pltpu.SMEM)


def _dispatch_call(dest, h2s, n_rows):
    tn = h2s.shape[0] // PCH
    tm = TOK_TILE
    assert dest.shape == (tn // tm, TOP_K, tm)
    blk = _assign_spec(tm, lambda i: (i, 0, 0))
    return pl.pallas_call(
        functools.partial(_dispatch_body, tm=tm),
        grid=(tn // tm,),
        in_specs=[blk, pl.BlockSpec((tm * PCH, LANES), lambda i: (i, 0))],
        out_specs=pl.BlockSpec(memory_space=pl.ANY),
        out_shape=jax.ShapeDtypeStruct((n_rows * PCH, LANES), I32),
        scratch_shapes=[pltpu.SemaphoreType.DMA],
        compiler_params=_cparams(("arbitrary",)),
        name="moe_dispatch",
    )(dest, h2s)


def _expert_body(bi_ref, nr_ref, ld_ref, nx_ref, xs_hbm, wg_hbm, wu_hbm, wd_hbm, ob_ref, wg_buf, wu_buf, wd_buf,
                 wg_bf, wu_bf, wd_bf, xbuf, sem, xsem):
    i = pl.program_id(0)
    nsteps = pl.num_programs(0)
    nr = nr_ref[i]
    slot = ld_ref[i]
    blk_rows = MOE_BLK * PCH

    def row_block(j):
        s = lax.rem(j, ROW_SLOTS)
        src = xs_hbm.at[pl.ds(pl.multiple_of(bi_ref[j] * blk_rows, blk_rows), blk_rows)]
        return pltpu.make_async_copy(src, xbuf.at[s], xsem.at[s])

    @pl.when(i == 0)
    def _():
        for j in range(ROW_SLOTS - 1):
            row_block(j).start()

    @pl.when(i + ROW_SLOTS - 1 < nsteps)
    def _():
        row_block(i + ROW_SLOTS - 1).start()

    def fetch(e, s):
        return (pltpu.make_async_copy(wg_hbm.at[e], wg_buf.at[s], sem.at[s]),
                pltpu.make_async_copy(wu_hbm.at[e], wu_buf.at[s], sem.at[s]),
                pltpu.make_async_copy(wd_hbm.at[e], wd_buf.at[s], sem.at[s]))

    @pl.when(i == 0)
    def _():
        for s in range(WEIGHT_SLOTS - 1):
            e0 = nx_ref[nx_ref.shape[0] - (WEIGHT_SLOTS - 1) + s]

            @pl.when(e0 >= 0)
            def _():
                for cp in fetch(e0, s):
                    cp.start()

    @pl.when(slot >= 0)
    def _():
        for cp in fetch(0, slot):
            cp.wait()

        @pl.when(nx_ref[i] >= 0)
        def _():
            for cp in fetch(nx_ref[i], lax.rem(slot + WEIGHT_SLOTS - 1, WEIGHT_SLOTS)):
                cp.start()

        wg_bf[...] = wg_buf[slot].astype(BF16)
        wu_bf[...] = wu_buf[slot].astype(BF16)
        wd_bf[...] = wd_buf[slot].astype(BF16)

    row_block(i).wait()

    @pl.when(nr > 0)
    def _():
        part = MOE_BLK // EXPERT_PARTS
        firsts = [q * part for q in range(EXPERT_PARTS)]
        rid = lax.broadcasted_iota(I32, (part, LANES), 0)
        xs_ref = xbuf.at[lax.rem(i, ROW_SLOTS)]
        x = [_rows_from_packed(xs_ref, part, rid < nr - f, f) for f in firsts]
        hg = [jnp.dot(v, wg_bf[...], preferred_element_type=F32) for v in x]
        hu = [jnp.dot(v, wu_bf[...], preferred_element_type=F32) for v in x]
        hh = [(g * _sigmoid(g) * u).astype(BF16) for g, u in zip(hg, hu)]
        out = [jnp.dot(v, wd_bf[...], preferred_element_type=F32) for v in hh]
        for f, v in zip(firsts, out):
            _rows_to_packed(ob_ref, v, f)


def _expert_tables(counts, pad_start, pad_end, nb):
    ne = counts.shape[0]
    experts = jnp.arange(ne, dtype=I32)
    first_row = jnp.arange(nb, dtype=I32) * MOE_BLK
    block_e = jnp.minimum(jnp.sum(pad_end[None, :] <= first_row[:, None], axis=1), ne - 1).astype(I32)
    mine = block_e[:, None] == experts[None, :]
    pick = lambda v: jnp.sum(jnp.where(mine, v[None, :], 0), axis=1)
    has = counts > 0
    ordinal = jnp.cumsum(has.astype(I32)) - 1
    start_b, count_b, ord_b = pick(pad_start), pick(counts), pick(ordinal)
    block_rows = jnp.clip(start_b + count_b - first_row, 0, MOE_BLK).astype(I32)
    block_i = jnp.minimum(jnp.arange(nb, dtype=I32), pad_end[-1] // MOE_BLK - 1).astype(I32)
    starts = (first_row == start_b) & (block_rows > 0)
    load_slot = jnp.where(starts, ord_b % WEIGHT_SLOTS, -1).astype(I32)
    nth = lambda want: jnp.max(jnp.where(has[None, :] & (ordinal[None, :] == want[:, None]), experts[None, :], -1),
                               axis=1)
    ahead = jnp.where(starts, nth(ord_b + WEIGHT_SLOTS - 1), -1)
    lead = nth(jnp.arange(WEIGHT_SLOTS - 1, dtype=I32))
    return block_i, block_rows, load_slot, jnp.concatenate([ahead, lead]).astype(I32)


def _expert_call(tables, xs, wg, wu, wd):
    nb = xs.shape[0] // (MOE_BLK * PCH)
    assert nb >= ROW_SLOTS
    d, ff = wg.shape[1], wg.shape[2]
    rows = pl.BlockSpec((MOE_BLK * PCH, LANES), lambda i, bi, nr, ld, nx: (bi[i], 0))
    hbm = pl.BlockSpec(memory_space=pl.ANY)
    grid_spec = pltpu.PrefetchScalarGridSpec(
        num_scalar_prefetch=4,
        grid=(nb,),
        in_specs=[hbm, hbm, hbm, hbm],
        out_specs=rows,
        scratch_shapes=[pltpu.VMEM((WEIGHT_SLOTS, d, ff), F32), pltpu.VMEM((WEIGHT_SLOTS, d, ff), F32),
                        pltpu.VMEM((WEIGHT_SLOTS, ff, d), F32),
                        pltpu.VMEM((d, ff), BF16), pltpu.VMEM((d, ff), BF16), pltpu.VMEM((ff, d), BF16),
                        pltpu.VMEM((ROW_SLOTS, MOE_BLK * PCH, LANES), I32),
                        pltpu.SemaphoreType.DMA((WEIGHT_SLOTS,)), pltpu.SemaphoreType.DMA((ROW_SLOTS,))],
    )
    return pl.pallas_call(
        _expert_body,
        grid_spec=grid_spec,
        out_shape=jax.ShapeDtypeStruct(xs.shape, I32),
        compiler_params=_cparams(("arbitrary",), VMEM_LIMIT),
        name="moe_experts",
    )(*tables, xs, wg, wu, wd)


def _combine_body(d_ref, dn_ref, wt_ref, ob_hbm, h2_ref, x1_ref, mod_ref, sg_ref, su_ref,
                  sd_ref, fg_ref, out_ref, gbuf, rbuf, sem, *, tm, nl):
    bb, ll, d = x1_ref.shape
    step = pl.program_id(0) * nl + pl.program_id(1)
    last = pl.num_programs(0) * nl - 1
    parity = lax.rem(step, 2)

    def request(d_tab, m, s):
        for kk in range(TOP_K):
            pltpu.make_async_copy(_pslab(ob_hbm, d_tab[0, kk, m]), _slab(gbuf.at[s], m * TOP_K + kk),
                                  sem.at[s]).start(priority=kk % 2)

    def mix(m, s):
        rows = gbuf.at[s]
        lo, hi = _unpack_pair(_slab(rows, m * TOP_K)[...])
        wt = wt_ref[0, 0, m]
        acc_lo, acc_hi = wt * lo, wt * hi
        for kk in range(1, TOP_K):
            lo, hi = _unpack_pair(_slab(rows, m * TOP_K + kk)[...])
            wt = wt_ref[0, kk, m]
            acc_lo, acc_hi = acc_lo + wt * lo, acc_hi + wt * hi
        _fslab(rbuf, m)[...] = jnp.concatenate([acc_lo, acc_hi], axis=0)

    @pl.when(step == 0)
    def _():
        def first(m, carry):
            request(d_ref, m, 0)
            return carry
        lax.fori_loop(0, tm, first, 0)

    def run(slot):
        pltpu.make_async_copy(ob_hbm.at[pl.ds(0, tm * TOP_K * PCH)], gbuf.at[slot], sem.at[slot]).wait()

        @pl.when(step < last)
        def _():
            def both(m, carry):
                request(dn_ref, m, 1 - slot)
                mix(m, slot)
                return carry
            lax.fori_loop(0, tm, both, 0)

        @pl.when(step == last)
        def _():
            def only(m, carry):
                mix(m, slot)
                return carry
            lax.fori_loop(0, tm, only, 0)

    for slot in range(2):
        pl.when(parity == slot)(functools.partial(run, slot))

    routed = jnp.concatenate([rbuf[pl.ds(c, tm, stride=CHUNKS), :] for c in range(CHUNKS)], axis=1)
    h2 = _rows_from_packed(h2_ref, tm)
    hg = jnp.dot(h2, sg_ref[...], preferred_element_type=F32)
    hu = jnp.dot(h2, su_ref[...], preferred_element_type=F32)
    shared = jnp.dot((hg * _sigmoid(hg) * hu).astype(BF16), sd_ref[...], preferred_element_type=F32)
    ff = (routed + shared).reshape(bb, ll, d)
    x2 = x1_ref[...] + mod_ref[:, 5:6, :] * ff
    out_ref[...] = x2 * lax.rsqrt(jnp.mean(x2 * x2, axis=-1, keepdims=True) + NORM_EPS) * fg_ref[...]


def _combine_call(dest, wts, first_tok, ob, h2s, x1, mod, p):
    bn, seq, d = x1.shape
    tm = CMB_TILE
    bb, ll = _tile(bn, seq, tm)
    nl = seq // ll
    tn = bn * seq
    nsteps = tn // tm
    per = dest.shape[2] // tm
    assert first_tok % tm == 0 and dest.shape[2] % tm == 0
    tile = lambda g: ((first_tok // tm + g) // per, 0, (first_tok // tm + g) % per)
    smem = _assign_spec(tm, lambda b, l: tile(b * nl + l))
    smem_next = _assign_spec(tm, lambda b, l: tile(jnp.minimum(b * nl + l + 1, nsteps - 1)))
    tok = pl.BlockSpec((bb, ll, d), lambda b, l: (b, l, 0))
    full = lambda a: pl.BlockSpec(a.shape, lambda b, l: (0,) * a.ndim)
    consts = (p["sh_gate"], p["sh_up"], p["sh_down"], p["final_g"])
    return pl.pallas_call(
        functools.partial(_combine_body, tm=tm, nl=nl),
        grid=(bn // bb, nl),
        in_specs=[smem, smem_next, smem, pl.BlockSpec(memory_space=pl.ANY),
                  pl.BlockSpec((tm * PCH, LANES), lambda b, l: (first_tok // tm + b * nl + l, 0)),
                  tok, pl.BlockSpec((bb, 6, d), lambda b, l: (b, 0, 0))] + [full(c) for c in consts],
        out_specs=tok,
        out_shape=jax.ShapeDtypeStruct((bn, seq, d), F32),
        scratch_shapes=[pltpu.VMEM((2, tm * TOP_K * PCH, LANES), I32), pltpu.VMEM((tm * CHUNKS, LANES), F32),
                        pltpu.SemaphoreType.DMA((2,))],
        compiler_params=_cparams(("arbitrary", "arbitrary"), VMEM_LIMIT),
        name="moe_combine_final",
    )(dest, dest, wts, ob, h2s, x1, mod, *consts)


def _layer_params(l, ada_w, ada_b, norm1_g, norm2_g, w_in, mu_shift, rw_w0, rw_w_up, rw_a0, rw_a_up, rw_g_up,
                  rw_k_k, rw_k_a, rw_r_k, rw_gn_g, rw_gn_b, gla_a_up, gla_a_bias, gla_norm_g, w_pa, w_pb, w_out,
                  router_w, router_b, exp_gate, exp_up, exp_down, sh_gate, sh_up, sh_down):
    d = D_MODEL
    wi = w_in[l]
    gla0 = RW_SHIFT_COLS
    xal0 = gla0 + QKV_W
    pad = jnp.zeros((d, XAL_W - GLA_GATE_RANK), F32)
    w_pack = jnp.concatenate([wi[:, :xal0], wi[:, xal0:xal0 + GLA_GATE_RANK], pad,
                              wi[:, xal0 + GLA_GATE_RANK:]], axis=1).astype(BF16)
    zr = jnp.zeros((RW_W_RANK, RW_WIDTH), F32)
    hid = jnp.arange(RW_WIDTH) // RW_HEAD
    row = lambda a: a.reshape(1, -1)
    rw_t = router_w[l].T
    rw_hi = rw_t.astype(BF16)
    return dict(
        ada_w=ada_w[l], ada_b=ada_b[l], norm1_g=norm1_g[l].reshape(1, 1, d),
        norm2_g=norm2_g[l].reshape(1, 1, d), w_pack=w_pack,
        mu=mu_shift[l].reshape(1, 1, -1), w0=row(rw_w0[l]), wup=jnp.concatenate([rw_w_up[l], zr], axis=0),
        a0=row(rw_a0[l]), aup=jnp.concatenate([zr, rw_a_up[l]], axis=0), gup=rw_g_up[l].astype(BF16),
        kk=row(rw_k_k[l]), ka=row(rw_k_a[l]), rk=row(rw_r_k[l]),
        bd64=(hid[:, None] == hid[None, :]).astype(BF16),
        gn_g=row(rw_gn_g[l]), gn_b=row(rw_gn_b[l]),
        gla_aup=jnp.concatenate([gla_a_up[l], jnp.zeros((XAL_W - GLA_GATE_RANK, GLA_KW), F32)], axis=0),
        gla_ab=row(gla_a_bias[l]), gla_ng=row(gla_norm_g[l]),
        w_pa=w_pa[l].astype(BF16), w_pb=w_pb[l].astype(BF16), w_out=w_out[l].astype(BF16),
        rw_hi=rw_hi, rw_lo=(rw_t - rw_hi.astype(F32)).astype(BF16), router_b=router_b[l],
        exp_gate=exp_gate[l], exp_up=exp_up[l], exp_down=exp_down[l],
        sh_gate=sh_gate[l].astype(BF16), sh_up=sh_up[l].astype(BF16), sh_down=sh_down[l].astype(BF16),
    )


def _mixer_group(x, mod, s_rw, s_sh, s_gla, p, tn, first_tok, shared):
    qkv, xal, gg, mg, r, lw, k2, v, a_s, b_s, g, bonus, new_sh = _inproj_prep_call(x, mod, s_sh, p)
    y, rw_new = _rwscan_call(r, lw, k2, v, a_s, b_s, s_rw)
    o_b, gla_new = _gla_call(qkv, xal, gg, s_gla, p)
    x1, h2s, logits = _merge_call(y, g, bonus, o_b, mg, x, mod, p, tn, first_tok, shared)
    states = (rw_new, new_sh[:, 0, :], gla_new)
    return x1, h2s, logits, states


def _moe(h2s, logits, p):
    tn = h2s.shape[0] // PCH
    eidx, rank, wts, counts = _route_call(logits, p["router_b"])
    counts = counts[:, 0].astype(I32)
    padded = (counts + MOE_BLK - 1) // MOE_BLK * MOE_BLK
    pad_end = jnp.cumsum(padded)
    pad_start = (pad_end - padded).astype(I32)
    nb = (tn * TOP_K + N_EXPERTS * (MOE_BLK - 1)) // MOE_BLK + 1
    tables = _expert_tables(counts, pad_start, pad_end, nb)
    dest = _dest_call(eidx, rank, pad_start)
    xs = _dispatch_call(dest, h2s, nb * MOE_BLK)
    ob = _expert_call(tables, xs, p["exp_gate"], p["exp_up"], p["exp_down"])
    return ob, dest, wts


def kernel(x_prompt, x_sample, c_prompt, c_sample, state_rwkv, state_shift, state_gla, ada_w, ada_b, norm1_g,
           norm2_g, w_in, mu_shift, rw_w0, rw_w_up, rw_a0, rw_a_up, rw_g_up, rw_k_k, rw_k_a, rw_r_k, rw_gn_g,
           rw_gn_b, gla_a_up, gla_a_bias, gla_norm_g, w_pa, w_pb, w_out, router_w, router_b, exp_gate, exp_up,
           exp_down, sh_gate, sh_up, sh_down, final_g):
    depth = ada_w.shape[0]
    bp, bs = x_prompt.shape[0], x_sample.shape[0]
    tp = bp * x_prompt.shape[1]
    tn = tp + bs * x_sample.shape[1]
    xs_g = [x_prompt, x_sample]
    c_all = jnp.concatenate([c_prompt, c_sample], axis=0)
    zeros = lambda shape: jnp.zeros(shape, x_prompt.dtype)
    new_states = [[], []]
    fg = final_g.reshape(1, 1, D_MODEL)
    for l in range(depth):
        p = _layer_params(l, ada_w, ada_b, norm1_g, norm2_g, w_in, mu_shift, rw_w0, rw_w_up, rw_a0, rw_a_up,
                          rw_g_up, rw_k_k, rw_k_a, rw_r_k, rw_gn_g, rw_gn_b, gla_a_up, gla_a_bias, gla_norm_g,
                          w_pa, w_pb, w_out, router_w, router_b, exp_gate, exp_up, exp_down, sh_gate, sh_up,
                          sh_down)
        p["final_g"] = fg
        mod_all = _mod_call(c_all, p["ada_w"], p["ada_b"])
        mods = [mod_all[:bp], mod_all[bp:]]
        states_in = [
            (zeros((bp, RW_HEADS, RW_HEAD, RW_HEAD)), zeros((bp, RW_SHIFT_COLS)),
             zeros((bp, GLA_HEADS, GLA_DK, GLA_DV))),
            (state_rwkv[l], state_shift[l], state_gla[l]),
        ]
        x1s, shared = [], None
        firsts = [0, tp]
        for gi in range(2):
            x1, h2_all, lg_all, st = _mixer_group(xs_g[gi], mods[gi], *states_in[gi], p, tn, firsts[gi], shared)
            shared = (h2_all, lg_all)
            x1s.append(x1)
            new_states[gi].append(st)
        ob, dest, wts = _moe(*shared, p)
        assert depth == 1, "the fused final norm assumes a single layer"
        xs_g = [_combine_call(dest, wts, firsts[gi], ob, shared[0], x1s[gi], mods[gi], p) for gi in range(2)]
    stack = lambda gi, j: new_states[gi][0][j][None] if depth == 1 else jnp.stack([s[j] for s in new_states[gi]])
    return (xs_g[0], xs_g[1], stack(0, 0), stack(0, 1), stack(0, 2), stack(1, 0), stack(1, 1), stack(1, 2))
```

```python
import functools

import jax
import jax.numpy as jnp
from jax import lax
from jax.experimental import pallas as pl
from jax.experimental.pallas import tpu as pltpu

F32, BF16, I32 = jnp.float32, jnp.bfloat16, jnp.int32

D_MODEL = 1024
RW_HEADS, RW_HEAD = 8, 64
RW_WIDTH = RW_HEADS * RW_HEAD
RW_W_RANK, RW_A_RANK, RW_G_RANK = 64, 64, 128
RW_GN_EPS = 64e-5
GLA_HEADS, GLA_DK, GLA_DV = 4, 64, 128
GLA_KW, GLA_VW = GLA_HEADS * GLA_DK, GLA_HEADS * GLA_DV
GLA_GATE_RANK = 16
GLA_GATE_TAU = 16.0
GLA_CHUNK = 16
RW_SHIFT_COLS = 3 * RW_WIDTH + RW_W_RANK + RW_A_RANK + RW_G_RANK
N_EXPERTS, TOP_K, N_GROUPS, TOPK_GROUPS = 256, 8, 8, 4
GROUP_SIZE = N_EXPERTS // N_GROUPS
EXPERT_FF = 256
ROUTED_SCALE = 2.5
NORM_EPS = 1e-6

LANES = 128
SUBLANES = 8
CHUNKS = D_MODEL // LANES
PCH = CHUNKS // 2
UNIT = 64
RW_SCAN_PASSES = (1, 1, 1, 1, 1)
GLA_UNITS_PER_STEP = 4
RW_UNITS_PER_STEP = 4
VMEM_LIMIT = 56 * 1024 * 1024

PA_W, QKV_W, XAL_W, GG_W, MG_W = RW_SHIFT_COLS, 2 * GLA_KW + GLA_VW, LANES, GLA_VW, 2 * D_MODEL
PACK_OFFS = (0, PA_W, PA_W + QKV_W, PA_W + QKV_W + XAL_W, PA_W + QKV_W + XAL_W + GG_W)
PACK_W = PA_W + QKV_W + XAL_W + GG_W + MG_W

TOK_TILE = 256
MOE_BLK = 256
EXPERT_PARTS = 1
WEIGHT_SLOTS = 3
ROW_SLOTS = 3
CMB_TILE = 256
CMB_GROUP = 16

_DN = {
    "nn": (((1,), (0,)), ((), ())),
    "nt": (((1,), (1,)), ((), ())),
    "tn": (((0,), (0,)), ((), ())),
}


def _split(x, pieces):
    out, rem = [], x
    for i in range(pieces):
        p = rem.astype(BF16)
        out.append(p)
        if i + 1 < pieces:
            rem = rem - p.astype(F32)
    return out


def _mm(a, b, form="nn", passes=1):
    dn = _DN[form]
    if passes == 6:
        return lax.dot_general(a.astype(F32), b.astype(F32), dn, precision=lax.Precision.HIGHEST,
                               preferred_element_type=F32)
    if passes == 1:
        return lax.dot_general(a.astype(BF16), b.astype(BF16), dn, preferred_element_type=F32)
    ah, al = _split(a, 2)
    bh, bl = _split(b, 2)
    out = lax.dot_general(ah, bh, dn, preferred_element_type=F32)
    out = out + lax.dot_general(ah, bl, dn, preferred_element_type=F32)
    return out + lax.dot_general(al, bh, dn, preferred_element_type=F32)


def _mm01(m01, x, pieces=3):
    m = m01.astype(BF16)
    out = None
    for p in _split(x, pieces):
        t = lax.dot_general(m, p, _DN["nn"], preferred_element_type=F32)
        out = t if out is None else out + t
    return out


def _xmm01(x, m01, pieces=2):
    m = m01.astype(BF16)
    out = None
    for p in _split(x, pieces):
        t = lax.dot_general(p, m, _DN["nn"], preferred_element_type=F32)
        out = t if out is None else out + t
    return out


HI16 = -65536


def _bf16_bits(x):
    return lax.bitcast_convert_type(x.astype(BF16).astype(F32), I32)


def _unpack_pair(w):
    return lax.bitcast_convert_type(w << 16, F32), lax.bitcast_convert_type(w & HI16, F32)


def _rows_to_packed(ref, x, first=0):
    for c in range(PCH):
        lo = _bf16_bits(x[:, c * LANES:(c + 1) * LANES])
        hi = _bf16_bits(x[:, (c + PCH) * LANES:(c + PCH + 1) * LANES])
        ref[pl.ds(first * PCH + c, x.shape[0], stride=PCH), :] = ((lo >> 16) & 0xFFFF) | (hi & HI16)


def _rows_from_packed(ref, n, live=None, first=0):
    lows, highs = [], []
    for c in range(PCH):
        w = ref[pl.ds(first * PCH + c, n, stride=PCH), :]
        if live is not None:
            w = jnp.where(live, w, 0)
        lo, hi = _unpack_pair(w)
        lows.append(lo.astype(BF16))
        highs.append(hi.astype(BF16))
    return jnp.concatenate(lows + highs, axis=1)


def _slab(ref, row):
    return ref.at[pl.ds(pl.multiple_of(row * PCH, PCH), PCH)]


def _fslab(ref, row):
    return ref.at[pl.ds(pl.multiple_of(row * CHUNKS, CHUNKS), CHUNKS)]


def _sigmoid(x):
    return 1.0 / (1.0 + jnp.exp(-x))


def _softplus(x):
    return jnp.maximum(x, 0.0) + jnp.log(1.0 + jnp.exp(-jnp.abs(x)))


def _log2(n):
    assert n > 0 and n & (n - 1) == 0, n
    return n.bit_length() - 1


def _cparams(sem, vmem=None):
    return pltpu.CompilerParams(dimension_semantics=sem, vmem_limit_bytes=vmem)


def _mod_body(c_ref, w_ref, b_ref, o_ref):
    c = c_ref[...]
    o_ref[0] = _mm(c * _sigmoid(c), w_ref[...], passes=6) + b_ref[...]


def _mod_call(c_all, ada_w, ada_b):
    bt, d = c_all.shape
    out = pl.pallas_call(
        _mod_body,
        grid=(6,),
        in_specs=[pl.BlockSpec((bt, d), lambda k: (0, 0)),
                  pl.BlockSpec((d, d), lambda k: (0, k)),
                  pl.BlockSpec((1, d), lambda k: (0, k))],
        out_specs=pl.BlockSpec((1, bt, d), lambda k: (k, 0, 0)),
        out_shape=jax.ShapeDtypeStruct((6, bt, d), F32),
        compiler_params=_cparams(("arbitrary",)),
        name="adaln_mod",
    )(c_all, ada_w, ada_b.reshape(1, 6 * d))
    return jnp.transpose(out, (1, 0, 2))


def _inproj_body(x_ref, mod_ref, g_ref, w_ref, pa_ref, qkv_ref, xal_ref, gg_ref, mg_ref):
    bb, ll, d = x_ref.shape
    x = x_ref[...]
    y = x * lax.rsqrt(jnp.mean(x * x, axis=-1, keepdims=True) + NORM_EPS) * g_ref[...]
    h = y * (1.0 + mod_ref[:, 1:2, :]) + mod_ref[:, 0:1, :]
    hb = h.reshape(bb * ll, d).astype(BF16)
    for ref, off in zip((pa_ref, qkv_ref, xal_ref, gg_ref, mg_ref), PACK_OFFS):
        w = ref.shape[-1]
        ref[...] = jnp.dot(hb, w_ref[:, off:off + w], preferred_element_type=F32).reshape(bb, ll, w)


def _tile(bn, seq, tile):
    if seq >= tile:
        assert seq % tile == 0
        return 1, tile
    assert tile % seq == 0 and bn % (tile // seq) == 0
    return tile // seq, seq


def _rwprep_body(pa_ref, sh_ref, mu_ref, w0_ref, wup_ref, a0_ref, aup_ref, gup_ref, kk_ref, ka_ref, rk_ref,
                 bd_ref, r_o, lw_o, k_o, v_o, a_o, b_o, g_o, bon_o, nsh_o, carry):
    bb, ll, wd = pa_ref.shape
    n = bb * ll
    hw = RW_WIDTH

    @pl.when(pl.program_id(1) == 0)
    def _():
        carry[...] = sh_ref[...]

    pa = pa_ref[...]
    rolled = pltpu.roll(pa.reshape(n, wd), 1, 0).reshape(bb, ll, wd)
    tok = lax.broadcasted_iota(I32, (bb, ll, wd), 1)
    prev = jnp.where(tok == 0, carry[...], rolled)
    last = pa_ref[:, ll - 1:ll, :]
    carry[...] = last
    nsh_o[...] = last
    xs = (pa + (prev - pa) * mu_ref[...]).reshape(n, wd)

    r, k, v = xs[:, 0:hw], xs[:, hw:2 * hw], xs[:, 2 * hw:3 * hw]
    xwa = xs[:, 3 * hw:3 * hw + LANES]
    xg = xs[:, 3 * hw + LANES:]
    w_log = -_softplus(-(w0_ref[...] + _mm(jnp.tanh(xwa), wup_ref[...], passes=3))) - 0.5
    lw = -jnp.exp(w_log)
    a = _sigmoid(a0_ref[...] + _mm(xwa, aup_ref[...], passes=3))
    g = _mm(_sigmoid(xg), gup_ref[...])
    bd = bd_ref[...]
    kkv = k * kk_ref[...]
    kkn = kkv * lax.rsqrt(jnp.maximum(_xmm01(kkv * kkv, bd, pieces=1), 1e-24))
    k2 = k * (1.0 + (a - 1.0) * ka_ref[...])
    bonus = _xmm01(r * k2 * rk_ref[...], bd, pieces=1) * v
    for ref, val in ((r_o, r), (lw_o, lw), (k_o, k2), (v_o, v), (a_o, -kkn), (b_o, kkn * a), (g_o, g),
                     (bon_o, bonus)):
        ref[...] = val.reshape(bb, ll, hw)


def _inproj_prep_body(x_ref, mod_ref, g_ref, w_ref, sh_ref, mu_ref, w0_ref, wup_ref, a0_ref, aup_ref, gup_ref, kk_ref,
                      ka_ref, rk_ref, bd_ref, qkv_o, xal_o, gg_o, mg_o, r_o, lw_o, k_o, v_o, a_o, b_o, g_o, bon_o,
                      nsh_o, pa_s, carry):
    _inproj_body(x_ref, mod_ref, g_ref, w_ref, pa_s, qkv_o, xal_o, gg_o, mg_o)
    _rwprep_body(pa_s, sh_ref, mu_ref, w0_ref, wup_ref, a0_ref, aup_ref, gup_ref, kk_ref, ka_ref, rk_ref, bd_ref,
                 r_o, lw_o, k_o, v_o, a_o, b_o, g_o, bon_o, nsh_o, carry)


def _inproj_prep_call(x, mod, s_sh, p):
    bn, seq, d = x.shape
    bb, ll = _tile(bn, seq, TOK_TILE)
    hw, wd = RW_WIDTH, PA_W
    tok = lambda w: pl.BlockSpec((bb, ll, w), lambda b, l: (b, l, 0))
    row = lambda w: pl.BlockSpec((bb, 1, w), lambda b, l: (b, 0, 0))
    full = lambda a: pl.BlockSpec(a.shape, lambda b, l: (0,) * a.ndim)
    consts = (p["mu"], p["w0"], p["wup"], p["a0"], p["aup"], p["gup"], p["kk"], p["ka"], p["rk"], p["bd64"])
    proj_w = (QKV_W, XAL_W, GG_W, MG_W)
    shapes = lambda ws: [jax.ShapeDtypeStruct((bn, seq, w), F32) for w in ws]
    return pl.pallas_call(
        _inproj_prep_body,
        grid=(bn // bb, seq // ll),
        in_specs=[tok(d), pl.BlockSpec((bb, 6, d), lambda b, l: (b, 0, 0)), full(p["norm1_g"]), full(p["w_pack"]),
                  row(wd)] + [full(c) for c in consts],
        out_specs=[tok(w) for w in proj_w] + [tok(hw)] * 8 + [row(wd)],
        out_shape=shapes(proj_w) + shapes((hw,) * 8) + [jax.ShapeDtypeStruct((bn, 1, wd), F32)],
        scratch_shapes=[pltpu.VMEM((bb, ll, wd), F32), pltpu.VMEM((bb, 1, wd), F32)],
        compiler_params=_cparams(("arbitrary", "arbitrary"), VMEM_LIMIT),
        name="norm_inproj_prep",
    )(x, mod, p["norm1_g"], p["w_pack"], s_sh.reshape(bn, 1, wd), *consts)


def _unit_masks(n, tl):
    ri = lax.broadcasted_iota(I32, (n, n), 0)
    ci = lax.broadcasted_iota(I32, (n, n), 1)
    same = (ri >> _log2(tl)) == (ci >> _log2(tl))
    return same, same & (ri > ci), same & (ri >= ci)


def _rwscan_body(r_ref, lw_ref, k_ref, v_ref, a_ref, b_ref, s0_ref, y_ref, sn_ref, st, *, nu, nseq, tl, passes):
    n = nseq * tl
    n2 = 2 * n
    p_aa, p_inv, p_apply, p_state, p_y = passes

    hd = RW_HEAD

    @pl.when(pl.program_id(1) == 0)
    def _():
        zero = jnp.zeros((hd, hd), F32)
        for q in range(nu * nseq):
            for p in range(RW_HEADS // 2):
                st[q, p] = jnp.concatenate(
                    [jnp.concatenate([s0_ref[q, 2 * p], zero], axis=1),
                     jnp.concatenate([zero, s0_ref[q, 2 * p + 1]], axis=1)], axis=0)

    same, _, incl = _unit_masks(n, tl)
    m_cum = jnp.where(incl, 1.0, 0.0)
    m_seq = jnp.where(same, 1.0, 0.0)
    ri = lax.broadcasted_iota(I32, (n2, n2), 0)
    ci = lax.broadcasted_iota(I32, (n2, n2), 1)
    rt, ct = ri & (n - 1), ci & (n - 1)
    dsame = ((rt >> _log2(tl)) == (ct >> _log2(tl))) & ((ri >> _log2(n)) == (ci >> _log2(n)))
    strict_d = dsame & (rt > ct)
    incl_d = dsame & (rt >= ct)
    eye_d = jnp.where(ri == ci, 1.0, 0.0)
    lane = lax.broadcasted_iota(I32, (1, LANES), 1)
    m0 = jnp.where(lane < RW_HEAD, 1.0, 0.0)
    m1 = 1.0 - m0

    def dup(x):
        return jnp.concatenate([x * m0, x * m1], axis=0)

    def seq_rows(x, q):
        if nseq == 1:
            return x
        return jnp.concatenate([x[q * tl:(q + 1) * tl], x[n + q * tl:n + (q + 1) * tl]], axis=0)

    def unit_rows(parts):
        if nseq == 1:
            return parts[0]
        return jnp.concatenate([p[0:tl] for p in parts] + [p[tl:2 * tl] for p in parts], axis=0)

    chains = [(u, p) for u in range(nu) for p in range(RW_HEADS // 2)]
    ids = range(len(chains))
    cat0 = lambda *xs: jnp.concatenate(xs, axis=0)

    def ld(ref, c):
        u, p = chains[c]
        return ref[u * nseq:(u + 1) * nseq, :, p * LANES:(p + 1) * LANES].reshape(n, LANES)

    lw = [ld(lw_ref, c) for c in ids]
    cum = [_mm01(m_cum, x) for x in lw]
    tot = [_mm01(m_seq, x) for x in lw]
    e_c = [jnp.exp(x) for x in cum]
    e_n = [jnp.exp(-x) for x in cum]
    e_l = [jnp.exp(t - x) for t, x in zip(tot, cum)]
    at_d = [dup(ld(a_ref, c) * jnp.exp(cum[c] - lw[c])) for c in ids]
    rt_d = [dup(ld(r_ref, c) * e_c[c]) for c in ids]
    bt_d = [dup(ld(b_ref, c) * e_n[c]) for c in ids]
    kt_d = [dup(ld(k_ref, c) * e_n[c]) for c in ids]
    bh_d = [dup(ld(b_ref, c) * e_l[c]) for c in ids]
    kh_d = [dup(ld(k_ref, c) * e_l[c]) for c in ids]
    v_d = [dup(ld(v_ref, c)) for c in ids]
    aa = [_mm(cat0(at_d[c], rt_d[c]), cat0(bt_d[c], kt_d[c]), "nt", p_aa) for c in ids]
    a_ab = [jnp.where(strict_d, x[0:n2, 0:n2], 0.0) for x in aa]
    a_ak = [jnp.where(strict_d, x[0:n2, n2:], 0.0) for x in aa]
    a_rb = [jnp.where(incl_d, x[n2:, 0:n2], 0.0) for x in aa]
    a_rk = [jnp.where(incl_d, x[n2:, n2:], 0.0) for x in aa]
    zy = [_mm(cat0(a_ak[c], a_rk[c]), v_d[c], passes=p_apply) for c in ids]
    tinv = [eye_d + x for x in a_ab]
    nk = a_ab
    for _ in range(_log2(tl) - 1):
        nk = [_mm(x, x, passes=p_inv) for x in nk]
        tinv = [t + _mm(t, x, passes=p_inv) for t, x in zip(tinv, nk)]
    wu = [_mm(tinv[c], jnp.concatenate([at_d[c], zy[c][0:n2]], axis=1), passes=p_apply) for c in ids]
    seqs = range(nseq)
    srow = lambda c, q: (chains[c][0] * nseq + q, chains[c][1])
    s_old = [[st[srow(c, q)] for q in seqs] for c in ids]
    xs = [[_mm(cat0(seq_rows(wu[c][:, 0:LANES], q), seq_rows(rt_d[c], q)), s_old[c][q], "nt", p_state)
           for q in seqs] for c in ids]
    u_q = [[xs[c][q][0:2 * tl] + seq_rows(wu[c][:, LANES:], q) for q in seqs] for c in ids]
    for c in ids:
        for q in seqs:
            g_c = jnp.exp(tot[c][q * tl:q * tl + 1, :])
            st[srow(c, q)] = s_old[c][q] * g_c + _mm(cat0(u_q[c][q], seq_rows(v_d[c], q)),
                                                     cat0(seq_rows(bh_d[c], q), seq_rows(kh_d[c], q)), "tn", p_state)
    for c in ids:
        u, p = chains[c]
        y_d = (unit_rows([xs[c][q][2 * tl:] for q in seqs]) + _mm(a_rb[c], unit_rows(u_q[c]), passes=p_y)
               + zy[c][n2:])
        y_ref[u * nseq:(u + 1) * nseq, :, p * LANES:(p + 1) * LANES] = (y_d[0:n] + y_d[n:]).reshape(nseq, tl, LANES)

    @pl.when(pl.program_id(1) == pl.num_programs(1) - 1)
    def _():
        for q in range(nu * nseq):
            for p in range(RW_HEADS // 2):
                s = st[q, p]
                sn_ref[q, 2 * p] = s[0:hd, 0:hd]
                sn_ref[q, 2 * p + 1] = s[hd:, hd:]


def _unit_shape(bn, seq):
    if seq >= UNIT:
        assert seq % UNIT == 0
        return 1, UNIT
    assert UNIT % seq == 0 and bn % (UNIT // seq) == 0
    return UNIT // seq, seq


def _rwscan_call(r, lw, k2, v, a_s, b_s, s0, passes=RW_SCAN_PASSES):
    bn, seq, hw = r.shape
    nseq, tl = _unit_shape(bn, seq)
    nu = RW_UNITS_PER_STEP if bn % (RW_UNITS_PER_STEP * nseq) == 0 else 1
    rows = nu * nseq
    tok = pl.BlockSpec((rows, tl, hw), lambda b, c: (b, c, 0))
    stt = pl.BlockSpec((rows, RW_HEADS, RW_HEAD, RW_HEAD), lambda b, c: (b, 0, 0, 0))
    return pl.pallas_call(
        functools.partial(_rwscan_body, nu=nu, nseq=nseq, tl=tl, passes=passes),
        grid=(bn // rows, seq // tl),
        in_specs=[tok] * 6 + [stt],
        out_specs=[tok, stt],
        out_shape=[jax.ShapeDtypeStruct((bn, seq, hw), F32), jax.ShapeDtypeStruct(s0.shape, F32)],
        scratch_shapes=[pltpu.VMEM((rows, RW_HEADS // 2, LANES, LANES), F32)],
        compiler_params=_cparams(("arbitrary", "arbitrary"), VMEM_LIMIT),
        name="rwkv_scan",
    )(r, lw, k2, v, a_s, b_s, s0)


def _gla_body(qkv_ref, xal_ref, gate_ref, aup_ref, ab_ref, ng_ref, s0_ref, o_ref, sn_ref, st, *, nu, nseq, tl, cs):
    n = nseq * tl
    n2 = 2 * n
    nsub = tl // cs

    @pl.when(pl.program_id(1) == 0)
    def _():
        zero = jnp.zeros((GLA_DV, GLA_DK), F32)
        for q in range(nu * nseq):
            for p in range(GLA_HEADS // 2):
                st[q, p] = jnp.concatenate(
                    [jnp.concatenate([s0_ref[q, 2 * p].T, zero], axis=1),
                     jnp.concatenate([zero, s0_ref[q, 2 * p + 1].T], axis=1)], axis=0)

    same, _, incl = _unit_masks(n, cs)
    m_cum = jnp.where(incl, 1.0, 0.0)
    m_sub = jnp.where(same, 1.0, 0.0)
    ri = lax.broadcasted_iota(I32, (n2, n2), 0)
    ci = lax.broadcasted_iota(I32, (n2, n2), 1)
    rt, ct = ri & (n - 1), ci & (n - 1)
    causal_d = ((rt >> _log2(cs)) == (ct >> _log2(cs))) & ((ri >> _log2(n)) == (ci >> _log2(n))) & (rt >= ct)
    lane = lax.broadcasted_iota(I32, (1, LANES), 1)
    m0 = jnp.where(lane < GLA_DK, 1.0, 0.0)
    m1 = 1.0 - m0
    sr = lax.broadcasted_iota(I32, (2 * GLA_DV, LANES), 0)
    sc = lax.broadcasted_iota(I32, (2 * GLA_DV, LANES), 1)
    st_mask = jnp.where((sr >> _log2(GLA_DV)) == (sc >> _log2(GLA_DK)), 1.0, 0.0)

    def dup(x):
        return jnp.concatenate([x * m0, x * m1], axis=0)

    chains = [(u, p) for u in range(nu) for p in range(GLA_HEADS // 2)]
    ids = range(len(chains))
    urows = lambda u: slice(u * nseq, (u + 1) * nseq)
    ng = ng_ref[...]
    la_all = [-_softplus(-(_mm(xal_ref[urows(u), :, :].reshape(n, LANES), aup_ref[...], passes=3) + ab_ref[...]))
              * (1.0 / GLA_GATE_TAU) for u in range(nu)]

    def ld(ref, c, off, width):
        return ref[urows(chains[c][0]), :, off:off + width].reshape(n, width)

    q = [ld(qkv_ref, c, chains[c][1] * LANES, LANES) * (GLA_DK ** -0.5) for c in ids]
    k = [ld(qkv_ref, c, GLA_KW + chains[c][1] * LANES, LANES) for c in ids]
    vp = [ld(qkv_ref, c, 2 * GLA_KW + chains[c][1] * 2 * GLA_DV, 2 * GLA_DV) for c in ids]
    la = [la_all[u][:, p * LANES:(p + 1) * LANES] for u, p in chains]
    bc = [_mm01(m_cum, x) for x in la]
    bl = [_mm01(m_sub, x) for x in la]
    qe = [q[c] * jnp.exp(bc[c]) for c in ids]
    ke = [k[c] * jnp.exp(-bc[c]) for c in ids]
    kd = [k[c] * jnp.exp(bl[c] - bc[c]) for c in ids]
    att = [jnp.where(causal_d, _mm(dup(qe[c]), dup(ke[c]), "nt", passes=1), 0.0) for c in ids]
    v_st = [jnp.concatenate([x[:, 0:GLA_DV], x[:, GLA_DV:]], axis=0) for x in vp]
    o_st = [_mm(att[c], v_st[c], passes=1) for c in ids]
    upd = [[_mm(vp[c][r0:r0 + cs], kd[c][r0:r0 + cs], "tn", passes=1) for r0 in range(0, n, cs)] for c in ids]
    inter = [[None] * (n // cs) for _ in ids]
    for sq in range(nseq):
        s = [st[chains[c][0] * nseq + sq, chains[c][1]] for c in ids]
        for j in range(nsub):
            i = sq * nsub + j
            r0 = i * cs
            for c in ids:
                inter[c][i] = _mm(qe[c][r0:r0 + cs], s[c], "nt", passes=1)
                s[c] = s[c] * jnp.exp(bl[c][r0:r0 + 1, :]) + st_mask * upd[c][i]
        for c in ids:
            st[chains[c][0] * nseq + sq, chains[c][1]] = s[c]
    for c in ids:
        u, p = chains[c]
        o = o_st[c] + jnp.concatenate([x[:, 0:GLA_DV] for x in inter[c]] + [x[:, GLA_DV:] for x in inter[c]], axis=0)
        o = o * lax.rsqrt(jnp.mean(o * o, axis=-1, keepdims=True) + NORM_EPS) * ng
        goff = p * 2 * GLA_DV
        gp = ld(gate_ref, c, goff, 2 * GLA_DV)
        g_st = jnp.concatenate([gp[:, 0:GLA_DV], gp[:, GLA_DV:]], axis=0)
        ob = o * (g_st * _sigmoid(g_st))
        o_ref[urows(u), :, goff:goff + GLA_DV] = ob[0:n].reshape(nseq, tl, GLA_DV)
        o_ref[urows(u), :, goff + GLA_DV:goff + 2 * GLA_DV] = ob[n:].reshape(nseq, tl, GLA_DV)

    @pl.when(pl.program_id(1) == pl.num_programs(1) - 1)
    def _():
        for q in range(nu * nseq):
            for p in range(GLA_HEADS // 2):
                s = st[q, p]
                sn_ref[q, 2 * p] = s[0:GLA_DV, 0:GLA_DK].T
                sn_ref[q, 2 * p + 1] = s[GLA_DV:, GLA_DK:].T


def _gla_call(qkv, xal, gate, s0, p):
    bn, seq, _ = qkv.shape
    nseq, tl = _unit_shape(bn, seq)
    cs = min(GLA_CHUNK, seq)
    assert tl % cs == 0
    nu = GLA_UNITS_PER_STEP if bn % (GLA_UNITS_PER_STEP * nseq) == 0 else 1
    rows = nu * nseq
    tok = lambda w: pl.BlockSpec((rows, tl, w), lambda b, c: (b, c, 0))
    full = lambda a: pl.BlockSpec(a.shape, lambda b, c: (0,) * a.ndim)
    stt = pl.BlockSpec((rows, GLA_HEADS, GLA_DK, GLA_DV), lambda b, c: (b, 0, 0, 0))
    consts = (p["gla_aup"], p["gla_ab"], p["gla_ng"])
    return pl.pallas_call(
        functools.partial(_gla_body, nu=nu, nseq=nseq, tl=tl, cs=cs),
        grid=(bn // rows, seq // tl),
        in_specs=[tok(QKV_W), tok(XAL_W), tok(GG_W)] + [full(c) for c in consts] + [stt],
        out_specs=[tok(GLA_VW), stt],
        out_shape=[jax.ShapeDtypeStruct((bn, seq, GLA_VW), F32), jax.ShapeDtypeStruct(s0.shape, F32)],
        scratch_shapes=[pltpu.VMEM((rows, GLA_HEADS // 2, 2 * GLA_DV, LANES), F32)],
        compiler_params=_cparams(("arbitrary", "arbitrary"), VMEM_LIMIT),
        name="gla_chunked",
    )(qkv, xal, gate, *consts, s0)


def _merge_body(y_ref, g_ref, bon_ref, ob_ref, mg_ref, x_ref, mod_ref, gng_ref, gnb_ref, bd_ref, wpa_ref,
                wpb_ref, wout_ref, n2_ref, rwh_ref, rwl_ref, *rest):
    x1_o, h2_o, lg_o = rest[-3:]
    bb, ll, d = x_ref.shape
    n = bb * ll
    hw = RW_WIDTH
    bd = bd_ref[...]
    y = y_ref[...].reshape(n, hw)
    mu = _xmm01(y, bd, pieces=2) * (1.0 / RW_HEAD)
    dv = y - mu
    var = _xmm01(dv * dv, bd, pieces=1) * (1.0 / RW_HEAD)
    yn = dv * lax.rsqrt(var + RW_GN_EPS) * gng_ref[...] + gnb_ref[...]
    o_a = (yn + bon_ref[...].reshape(n, hw)) * g_ref[...].reshape(n, hw)
    o_b = ob_ref[...].reshape(n, GLA_VW)
    mg = mg_ref[...].reshape(n, 2 * d)
    merged = _sigmoid(mg[:, 0:d]) * _mm(o_a, wpa_ref[...]) + _sigmoid(mg[:, d:]) * _mm(o_b, wpb_ref[...])
    mix = _mm(merged, wout_ref[...]).reshape(bb, ll, d)
    x1 = x_ref[...] + mod_ref[:, 2:3, :] * mix
    x1_o[...] = x1
    yn2 = x1 * lax.rsqrt(jnp.mean(x1 * x1, axis=-1, keepdims=True) + NORM_EPS) * n2_ref[...]
    h2 = (yn2 * (1.0 + mod_ref[:, 4:5, :]) + mod_ref[:, 3:4, :]).reshape(n, d)
    hh, hl = _split(h2, 2)
    rwh, rwl = rwh_ref[...], rwl_ref[...]
    nt = lambda a, b: lax.dot_general(a, b, _DN["nt"], preferred_element_type=F32)
    lg_o[...] = nt(rwh, hh) + nt(rwl, hh) + nt(rwh, hl)
    _rows_to_packed(h2_o, h2)


def _merge_call(y, g, bonus, o_b, mg, x, mod, p, tn, first_tok, shared):
    bn, seq, d = x.shape
    bb, ll = _tile(bn, seq, TOK_TILE)
    nl = seq // ll
    assert first_tok % (bb * ll) == 0
    t0 = first_tok // (bb * ll)
    n_in = 7 + 9
    extra = [] if shared is None else list(shared)
    alias = {} if shared is None else {n_in: 1, n_in + 1: 2}
    tok = lambda w: pl.BlockSpec((bb, ll, w), lambda b, l: (b, l, 0))
    full = lambda a: pl.BlockSpec(a.shape, lambda b, l: (0,) * a.ndim)
    consts = (p["gn_g"], p["gn_b"], p["bd64"], p["w_pa"], p["w_pb"], p["w_out"], p["norm2_g"], p["rw_hi"],
              p["rw_lo"])
    return pl.pallas_call(
        _merge_body,
        grid=(bn // bb, nl),
        in_specs=[tok(RW_WIDTH)] * 3 + [tok(GLA_VW), tok(MG_W), tok(d),
                                        pl.BlockSpec((bb, 6, d), lambda b, l: (b, 0, 0))] + [full(c) for c in consts]
        + [pl.BlockSpec(memory_space=pl.ANY)] * len(extra),
        out_specs=[tok(d),
                   pl.BlockSpec((bb * ll * PCH, LANES), lambda b, l: (t0 + b * nl + l, 0)),
                   pl.BlockSpec((N_EXPERTS, bb * ll), lambda b, l: (0, t0 + b * nl + l))],
        out_shape=[jax.ShapeDtypeStruct((bn, seq, d), F32),
                   jax.ShapeDtypeStruct((tn * PCH, LANES), I32),
                   jax.ShapeDtypeStruct((N_EXPERTS, tn), F32)],
        input_output_aliases=alias,
        compiler_params=_cparams(("arbitrary", "arbitrary"), VMEM_LIMIT),
        name="merge_outproj_router",
    )(y, g, bonus, o_b, mg, x, mod, *consts, *extra)


def _route_body(lg_ref, rb_ref, e_o, rk_o, w_o, cnt_o, carry):
    ne, tm = lg_ref.shape

    @pl.when(pl.program_id(0) == 0)
    def _():
        carry[...] = jnp.zeros_like(carry)

    neg = -jnp.inf
    scores = _sigmoid(lg_ref[...])
    sel = scores + rb_ref[...]
    row_i = lax.broadcasted_iota(I32, (ne, tm), 0)
    row = row_i.astype(F32)
    grp = (row_i >> _log2(GROUP_SIZE)).astype(F32)

    def first_max(x, ids, none):
        m = jnp.max(x, axis=0, keepdims=True)
        return m, jnp.min(jnp.where(x == m, ids, none), axis=0, keepdims=True)

    gs = []
    gids = lax.broadcasted_iota(I32, (GROUP_SIZE, tm), 0)
    for gidx in range(N_GROUPS):
        rows = slice(gidx * GROUP_SIZE, (gidx + 1) * GROUP_SIZE)
        sg = _sigmoid(lg_ref[rows, :]) + rb_ref[rows, :]
        ids = (gids + gidx * GROUP_SIZE).astype(F32)
        m1, i1 = first_max(sg, ids, float(ne))
        gs.append(m1 + jnp.max(jnp.where(ids == i1, neg, sg), axis=0, keepdims=True))
    gs = jnp.concatenate(gs, axis=0)
    gid = lax.broadcasted_iota(I32, (N_GROUPS, tm), 0).astype(F32)
    cur = jnp.full((ne, tm), neg, F32)
    for _ in range(TOPK_GROUPS):
        _, gi = first_max(gs, gid, float(N_GROUPS))
        cur = jnp.where(grp == gi, sel, cur)
        gs = jnp.where(gid == gi, neg, gs)

    pm = jnp.zeros((ne, tm), F32)
    eidx, wts = [], []
    for _ in range(TOP_K):
        _, ei = first_max(cur, row, float(ne))
        hit = row == ei
        pm = jnp.where(hit, 1.0, pm)
        eidx.append(ei)
        wts.append(jnp.sum(jnp.where(hit, scores, 0.0), axis=0, keepdims=True))
        cur = jnp.where(hit, neg, cur)
    wsum = wts[0]
    for w in wts[1:]:
        wsum = wsum + w

    ri = lax.broadcasted_iota(I32, (tm, tm), 0)
    ci = lax.broadcasted_iota(I32, (tm, tm), 1)
    earlier = jnp.where(ri < ci, 1.0, 0.0)
    rank = _mm(pm, earlier, passes=1) + carry[...]
    carry[...] = carry[...] + jnp.sum(pm, axis=1, keepdims=True)
    cnt_o[...] = carry[...]

    rks = [jnp.sum(jnp.where(row == e, rank, 0.0), axis=0, keepdims=True) for e in eidx]
    e_o[0] = jnp.concatenate(eidx, axis=0).astype(I32)
    rk_o[0] = jnp.concatenate(rks, axis=0).astype(I32)
    w_o[0] = jnp.concatenate([w / wsum * ROUTED_SCALE for w in wts], axis=0)


def _route_call(logits_t, router_b):
    ne, tn = logits_t.shape
    tm = TOK_TILE
    assert tn % tm == 0
    col = pl.BlockSpec((ne, 1), lambda i: (0, 0))
    tab = pl.BlockSpec((1, TOP_K, tm), lambda i: (i, 0, 0))
    tab_shape = (tn // tm, TOP_K, tm)
    return pl.pallas_call(
        _route_body,
        grid=(tn // tm,),
        in_specs=[pl.BlockSpec((ne, tm), lambda i: (0, i)), col],
        out_specs=[tab, tab, tab, col],
        out_shape=[jax.ShapeDtypeStruct(tab_shape, I32), jax.ShapeDtypeStruct(tab_shape, I32),
                   jax.ShapeDtypeStruct(tab_shape, F32), jax.ShapeDtypeStruct((ne, 1), F32)],
        scratch_shapes=[pltpu.VMEM((ne, 1), F32)],
        compiler_params=_cparams(("arbitrary",)),
        name="moe_route",
    )(logits_t, router_b.reshape(ne, 1))


def _dest_body(e_ref, rk_ref, ps_ref, d_o):
    ne, tm = ps_ref.shape[0], e_ref.shape[2]
    ids = lax.broadcasted_iota(I32, (ne, tm), 0)
    ps = ps_ref[...]
    first = [jnp.sum(jnp.where(ids == e_ref[0, kk:kk + 1, :], ps, 0.0), axis=0, keepdims=True)
             for kk in range(TOP_K)]
    d_o[0] = (jnp.concatenate(first, axis=0).astype(I32) + rk_ref[0]) * PCH


def _dest_call(eidx, rank, pad_start):
    nt, _, tm = eidx.shape
    ne = pad_start.shape[0]
    tab = pl.BlockSpec((1, TOP_K, tm), lambda i: (i, 0, 0))
    return pl.pallas_call(
        _dest_body,
        grid=(nt,),
        in_specs=[tab, tab, pl.BlockSpec((ne, 1), lambda i: (0, 0))],
        out_specs=tab,
        out_shape=jax.ShapeDtypeStruct(eidx.shape, I32),
        compiler_params=_cparams(("arbitrary",)),
        name="moe_dest",
    )(eidx, rank, pad_start.astype(F32).reshape(ne, 1))


def _pslab(ref, offset):
    return ref.at[pl.ds(pl.multiple_of(offset, PCH), PCH)]


def _dispatch_body(d_ref, h2_ref, xs_hbm, sem, *, tm):
    def issue(m, carry):
        for kk in range(TOP_K):
            pltpu.make_async_copy(_slab(h2_ref, m), _pslab(xs_hbm, d_ref[0, kk, m]), sem).start(priority=kk % 2)
        return carry

    lax.fori_loop(0, tm, issue, 0)
    all_rows = xs_hbm.at[pl.ds(0, tm * TOP_K * PCH)]
    pltpu.make_async_copy(all_rows, all_rows, sem).wait()


def _assign_spec(tm, index_map):
    return pl.BlockSpec((1, TOP_K, tm), index_map, memory_space=pltpu.SMEM)


def _dispatch_call(dest, h2s, n_rows):
    tn = h2s.shape[0] // PCH
    tm = TOK_TILE
    assert dest.shape == (tn // tm, TOP_K, tm)
    blk = _assign_spec(tm, lambda i: (i, 0, 0))
    return pl.pallas_call(
        functools.partial(_dispatch_body, tm=tm),
        grid=(tn // tm,),
        in_specs=[blk, pl.BlockSpec((tm * PCH, LANES), lambda i: (i, 0))],
        out_specs=pl.BlockSpec(memory_space=pl.ANY),
        out_shape=jax.ShapeDtypeStruct((n_rows * PCH, LANES), I32),
        scratch_shapes=[pltpu.SemaphoreType.DMA],
        compiler_params=_cparams(("arbitrary",)),
        name="moe_dispatch",
    )(dest, h2s)


def _expert_body(bi_ref, nr_ref, ld_ref, nx_ref, xs_hbm, wg_hbm, wu_hbm, wd_hbm, ob_ref, wg_buf, wu_buf, wd_buf,
                 wg_bf, wu_bf, wd_bf, xbuf, sem, xsem):
    i = pl.program_id(0)
    nsteps = pl.num_programs(0)
    nr = nr_ref[i]
    slot = ld_ref[i]
    blk_rows = MOE_BLK * PCH

    def row_block(j):
        s = lax.rem(j, ROW_SLOTS)
        src = xs_hbm.at[pl.ds(pl.multiple_of(bi_ref[j] * blk_rows, blk_rows), blk_rows)]
        return pltpu.make_async_copy(src, xbuf.at[s], xsem.at[s])

    @pl.when(i == 0)
    def _():
        for j in range(ROW_SLOTS - 1):
            row_block(j).start()

    @pl.when(i + ROW_SLOTS - 1 < nsteps)
    def _():
        row_block(i + ROW_SLOTS - 1).start()

    def fetch(e, s):
        return (pltpu.make_async_copy(wg_hbm.at[e], wg_buf.at[s], sem.at[s]),
                pltpu.make_async_copy(wu_hbm.at[e], wu_buf.at[s], sem.at[s]),
                pltpu.make_async_copy(wd_hbm.at[e], wd_buf.at[s], sem.at[s]))

    @pl.when(i == 0)
    def _():
        for s in range(WEIGHT_SLOTS - 1):
            e0 = nx_ref[nx_ref.shape[0] - (WEIGHT_SLOTS - 1) + s]

            @pl.when(e0 >= 0)
            def _():
                for cp in fetch(e0, s):
                    cp.start()

    @pl.when(slot >= 0)
    def _():
        for cp in fetch(0, slot):
            cp.wait()

        @pl.when(nx_ref[i] >= 0)
        def _():
            for cp in fetch(nx_ref[i], lax.rem(slot + WEIGHT_SLOTS - 1, WEIGHT_SLOTS)):
                cp.start()

        wg_bf[...] = wg_buf[slot].astype(BF16)
        wu_bf[...] = wu_buf[slot].astype(BF16)
        wd_bf[...] = wd_buf[slot].astype(BF16)

    row_block(i).wait()

    @pl.when(nr > 0)
    def _():
        part = MOE_BLK // EXPERT_PARTS
        firsts = [q * part for q in range(EXPERT_PARTS)]
        rid = lax.broadcasted_iota(I32, (part, LANES), 0)
        xs_ref = xbuf.at[lax.rem(i, ROW_SLOTS)]
        x = [_rows_from_packed(xs_ref, part, rid < nr - f, f) for f in firsts]
        hg = [jnp.dot(v, wg_bf[...], preferred_element_type=F32) for v in x]
        hu = [jnp.dot(v, wu_bf[...], preferred_element_type=F32) for v in x]
        hh = [(g * _sigmoid(g) * u).astype(BF16) for g, u in zip(hg, hu)]
        out = [jnp.dot(v, wd_bf[...], preferred_element_type=F32) for v in hh]
        for f, v in zip(firsts, out):
            _rows_to_packed(ob_ref, v, f)


def _expert_tables(counts, pad_start, pad_end, nb):
    ne = counts.shape[0]
    experts = jnp.arange(ne, dtype=I32)
    first_row = jnp.arange(nb, dtype=I32) * MOE_BLK
    block_e = jnp.minimum(jnp.sum(pad_end[None, :] <= first_row[:, None], axis=1), ne - 1).astype(I32)
    mine = block_e[:, None] == experts[None, :]
    pick = lambda v: jnp.sum(jnp.where(mine, v[None, :], 0), axis=1)
    has = counts > 0
    ordinal = jnp.cumsum(has.astype(I32)) - 1
    start_b, count_b, ord_b = pick(pad_start), pick(counts), pick(ordinal)
    block_rows = jnp.clip(start_b + count_b - first_row, 0, MOE_BLK).astype(I32)
    block_i = jnp.minimum(jnp.arange(nb, dtype=I32), pad_end[-1] // MOE_BLK - 1).astype(I32)
    starts = (first_row == start_b) & (block_rows > 0)
    load_slot = jnp.where(starts, ord_b % WEIGHT_SLOTS, -1).astype(I32)
    nth = lambda want: jnp.max(jnp.where(has[None, :] & (ordinal[None, :] == want[:, None]), experts[None, :], -1),
                               axis=1)
    ahead = jnp.where(starts, nth(ord_b + WEIGHT_SLOTS - 1), -1)
    lead = nth(jnp.arange(WEIGHT_SLOTS - 1, dtype=I32))
    return block_i, block_rows, load_slot, jnp.concatenate([ahead, lead]).astype(I32)


def _expert_call(tables, xs, wg, wu, wd):
    nb = xs.shape[0] // (MOE_BLK * PCH)
    assert nb >= ROW_SLOTS
    d, ff = wg.shape[1], wg.shape[2]
    rows = pl.BlockSpec((MOE_BLK * PCH, LANES), lambda i, bi, nr, ld, nx: (bi[i], 0))
    hbm = pl.BlockSpec(memory_space=pl.ANY)
    grid_spec = pltpu.PrefetchScalarGridSpec(
        num_scalar_prefetch=4,
        grid=(nb,),
        in_specs=[hbm, hbm, hbm, hbm],
        out_specs=rows,
        scratch_shapes=[pltpu.VMEM((WEIGHT_SLOTS, d, ff), F32), pltpu.VMEM((WEIGHT_SLOTS, d, ff), F32),
                        pltpu.VMEM((WEIGHT_SLOTS, ff, d), F32),
                        pltpu.VMEM((d, ff), BF16), pltpu.VMEM((d, ff), BF16), pltpu.VMEM((ff, d), BF16),
                        pltpu.VMEM((ROW_SLOTS, MOE_BLK * PCH, LANES), I32),
                        pltpu.SemaphoreType.DMA((WEIGHT_SLOTS,)), pltpu.SemaphoreType.DMA((ROW_SLOTS,))],
    )
    return pl.pallas_call(
        _expert_body,
        grid_spec=grid_spec,
        out_shape=jax.ShapeDtypeStruct(xs.shape, I32),
        compiler_params=_cparams(("arbitrary",), VMEM_LIMIT),
        name="moe_experts",
    )(*tables, xs, wg, wu, wd)


def _combine_body(d_ref, dn_ref, wt_ref, ob_hbm, h2_ref, x1_ref, mod_ref, sg_ref, su_ref,
                  sd_ref, fg_ref, out_ref, gbuf, rbuf, wcol, sem, *, tm, nl):
    bb, ll, d = x1_ref.shape
    step = pl.program_id(0) * nl + pl.program_id(1)
    last = pl.num_programs(0) * nl - 1
    parity = lax.rem(step, 2)
    grp = CMB_GROUP

    def request(d_tab, g, s):
        for j in range(grp):
            m = g * grp + j
            for kk in range(TOP_K):
                pltpu.make_async_copy(_pslab(ob_hbm, d_tab[0, kk, m]), _slab(gbuf.at[s], kk * tm + m),
                                      sem.at[s]).start(priority=kk % 2)

    def mix(g, s):
        r0 = pl.multiple_of(g * grp, grp)
        w = wcol[pl.ds(r0, grp), :]
        wk = [w[:, kk:kk + 1] for kk in range(TOP_K)]
        for c in range(PCH):
            acc_lo = acc_hi = None
            for kk in range(TOP_K):
                words = gbuf[s, pl.ds((kk * tm + r0) * PCH + c, grp, stride=PCH), :]
                lo, hi = _unpack_pair(words)
                acc_lo = wk[kk] * lo if acc_lo is None else acc_lo + wk[kk] * lo
                acc_hi = wk[kk] * hi if acc_hi is None else acc_hi + wk[kk] * hi
            rbuf[pl.ds(r0, grp), c * LANES:(c + 1) * LANES] = acc_lo
            rbuf[pl.ds(r0, grp), (c + PCH) * LANES:(c + PCH + 1) * LANES] = acc_hi

    @pl.when(step == 0)
    def _():
        def first(g, carry):
            request(d_ref, g, 0)
            return carry
        lax.fori_loop(0, tm // grp, first, 0)

    ri = lax.broadcasted_iota(I32, (tm, tm), 0)
    ci = lax.broadcasted_iota(I32, (tm, tm), 1)
    eye = jnp.where(ri == ci, 1.0, 0.0).astype(BF16)
    wc = None
    for piece in _split(wt_ref[0], 3):
        t = lax.dot_general(eye, piece, _DN["nt"], preferred_element_type=F32)
        wc = t if wc is None else wc + t
    wcol[...] = wc

    def run(slot):
        pltpu.make_async_copy(ob_hbm.at[pl.ds(0, tm * TOP_K * PCH)], gbuf.at[slot], sem.at[slot]).wait()

        @pl.when(step < last)
        def _():
            def both(g, carry):
                request(dn_ref, g, 1 - slot)
                mix(g, slot)
                return carry
            lax.fori_loop(0, tm // grp, both, 0)

        @pl.when(step == last)
        def _():
            def only(g, carry):
                mix(g, slot)
                return carry
            lax.fori_loop(0, tm // grp, only, 0)

    for slot in range(2):
        pl.when(parity == slot)(functools.partial(run, slot))

    routed = rbuf[...]
    h2 = _rows_from_packed(h2_ref, tm)
    hg = jnp.dot(h2, sg_ref[...], preferred_element_type=F32)
    hu = jnp.dot(h2, su_ref[...], preferred_element_type=F32)
    shared = jnp.dot((hg * _sigmoid(hg) * hu).astype(BF16), sd_ref[...], preferred_element_type=F32)
    ff = (routed + shared).reshape(bb, ll, d)
    x2 = x1_ref[...] + mod_ref[:, 5:6, :] * ff
    out_ref[...] = x2 * lax.rsqrt(jnp.mean(x2 * x2, axis=-1, keepdims=True) + NORM_EPS) * fg_ref[...]


def _combine_call(dest, wts, first_tok, ob, h2s, x1, mod, p):
    bn, seq, d = x1.shape
    tm = CMB_TILE
    bb, ll = _tile(bn, seq, tm)
    nl = seq // ll
    tn = bn * seq
    nsteps = tn // tm
    per = dest.shape[2] // tm
    assert first_tok % tm == 0 and dest.shape[2] % tm == 0
    tile = lambda g: ((first_tok // tm + g) // per, 0, (first_tok // tm + g) % per)
    smem = _assign_spec(tm, lambda b, l: tile(b * nl + l))
    smem_next = _assign_spec(tm, lambda b, l: tile(jnp.minimum(b * nl + l + 1, nsteps - 1)))
    wblk = pl.BlockSpec((1, TOP_K, tm), lambda b, l: tile(b * nl + l))
    tok = pl.BlockSpec((bb, ll, d), lambda b, l: (b, l, 0))
    full = lambda a: pl.BlockSpec(a.shape, lambda b, l: (0,) * a.ndim)
    consts = (p["sh_gate"], p["sh_up"], p["sh_down"], p["final_g"])
    return pl.pallas_call(
        functools.partial(_combine_body, tm=tm, nl=nl),
        grid=(bn // bb, nl),
        in_specs=[smem, smem_next, wblk, pl.BlockSpec(memory_space=pl.ANY),
                  pl.BlockSpec((tm * PCH, LANES), lambda b, l: (first_tok // tm + b * nl + l, 0)),
                  tok, pl.BlockSpec((bb, 6, d), lambda b, l: (b, 0, 0))] + [full(c) for c in consts],
        out_specs=tok,
        out_shape=jax.ShapeDtypeStruct((bn, seq, d), F32),
        scratch_shapes=[pltpu.VMEM((2, tm * TOP_K * PCH, LANES), I32), pltpu.VMEM((tm, d), F32),
                        pltpu.VMEM((tm, TOP_K), F32), pltpu.SemaphoreType.DMA((2,))],
        compiler_params=_cparams(("arbitrary", "arbitrary"), VMEM_LIMIT),
        name="moe_combine_final",
    )(dest, dest, wts, ob, h2s, x1, mod, *consts)


def _layer_params(l, ada_w, ada_b, norm1_g, norm2_g, w_in, mu_shift, rw_w0, rw_w_up, rw_a0, rw_a_up, rw_g_up,
                  rw_k_k, rw_k_a, rw_r_k, rw_gn_g, rw_gn_b, gla_a_up, gla_a_bias, gla_norm_g, w_pa, w_pb, w_out,
                  router_w, router_b, exp_gate, exp_up, exp_down, sh_gate, sh_up, sh_down):
    d = D_MODEL
    wi = w_in[l]
    gla0 = RW_SHIFT_COLS
    xal0 = gla0 + QKV_W
    pad = jnp.zeros((d, XAL_W - GLA_GATE_RANK), F32)
    w_pack = jnp.concatenate([wi[:, :xal0], wi[:, xal0:xal0 + GLA_GATE_RANK], pad,
                              wi[:, xal0 + GLA_GATE_RANK:]], axis=1).astype(BF16)
    zr = jnp.zeros((RW_W_RANK, RW_WIDTH), F32)
    hid = jnp.arange(RW_WIDTH) // RW_HEAD
    row = lambda a: a.reshape(1, -1)
    rw_t = router_w[l].T
    rw_hi = rw_t.astype(BF16)
    return dict(
        ada_w=ada_w[l], ada_b=ada_b[l], norm1_g=norm1_g[l].reshape(1, 1, d),
        norm2_g=norm2_g[l].reshape(1, 1, d), w_pack=w_pack,
        mu=mu_shift[l].reshape(1, 1, -1), w0=row(rw_w0[l]), wup=jnp.concatenate([rw_w_up[l], zr], axis=0),
        a0=row(rw_a0[l]), aup=jnp.concatenate([zr, rw_a_up[l]], axis=0), gup=rw_g_up[l].astype(BF16),
        kk=row(rw_k_k[l]), ka=row(rw_k_a[l]), rk=row(rw_r_k[l]),
        bd64=(hid[:, None] == hid[None, :]).astype(BF16),
        gn_g=row(rw_gn_g[l]), gn_b=row(rw_gn_b[l]),
        gla_aup=jnp.concatenate([gla_a_up[l], jnp.zeros((XAL_W - GLA_GATE_RANK, GLA_KW), F32)], axis=0),
        gla_ab=row(gla_a_bias[l]), gla_ng=row(gla_norm_g[l]),
        w_pa=w_pa[l].astype(BF16), w_pb=w_pb[l].astype(BF16), w_out=w_out[l].astype(BF16),
        rw_hi=rw_hi, rw_lo=(rw_t - rw_hi.astype(F32)).astype(BF16), router_b=router_b[l],
        exp_gate=exp_gate[l], exp_up=exp_up[l], exp_down=exp_down[l],
        sh_gate=sh_gate[l].astype(BF16), sh_up=sh_up[l].astype(BF16), sh_down=sh_down[l].astype(BF16),
    )


def _mixer_group(x, mod, s_rw, s_sh, s_gla, p, tn, first_tok, shared):
    qkv, xal, gg, mg, r, lw, k2, v, a_s, b_s, g, bonus, new_sh = _inproj_prep_call(x, mod, s_sh, p)
    y, rw_new = _rwscan_call(r, lw, k2, v, a_s, b_s, s_rw)
    o_b, gla_new = _gla_call(qkv, xal, gg, s_gla, p)
    x1, h2s, logits = _merge_call(y, g, bonus, o_b, mg, x, mod, p, tn, first_tok, shared)
    states = (rw_new, new_sh[:, 0, :], gla_new)
    return x1, h2s, logits, states


def _moe(h2s, logits, p):
    tn = h2s.shape[0] // PCH
    eidx, rank, wts, counts = _route_call(logits, p["router_b"])
    counts = counts[:, 0].astype(I32)
    padded = (counts + MOE_BLK - 1) // MOE_BLK * MOE_BLK
    pad_end = jnp.cumsum(padded)
    pad_start = (pad_end - padded).astype(I32)
    nb = (tn * TOP_K + N_EXPERTS * (MOE_BLK - 1)) // MOE_BLK + 1
    tables = _expert_tables(counts, pad_start, pad_end, nb)
    dest = _dest_call(eidx, rank, pad_start)
    xs = _dispatch_call(dest, h2s, nb * MOE_BLK)
    ob = _expert_call(tables, xs, p["exp_gate"], p["exp_up"], p["exp_down"])
    return ob, dest, wts


def kernel(x_prompt, x_sample, c_prompt, c_sample, state_rwkv, state_shift, state_gla, ada_w, ada_b, norm1_g,
           norm2_g, w_in, mu_shift, rw_w0, rw_w_up, rw_a0, rw_a_up, rw_g_up, rw_k_k, rw_k_a, rw_r_k, rw_gn_g,
           rw_gn_b, gla_a_up, gla_a_bias, gla_norm_g, w_pa, w_pb, w_out, router_w, router_b, exp_gate, exp_up,
           exp_down, sh_gate, sh_up, sh_down, final_g):
    depth = ada_w.shape[0]
    bp, bs = x_prompt.shape[0], x_sample.shape[0]
    tp = bp * x_prompt.shape[1]
    tn = tp + bs * x_sample.shape[1]
    xs_g = [x_prompt, x_sample]
    c_all = jnp.concatenate([c_prompt, c_sample], axis=0)
    zeros = lambda shape: jnp.zeros(shape, x_prompt.dtype)
    new_states = [[], []]
    fg = final_g.reshape(1, 1, D_MODEL)
    for l in range(depth):
        p = _layer_params(l, ada_w, ada_b, norm1_g, norm2_g, w_in, mu_shift, rw_w0, rw_w_up, rw_a0, rw_a_up,
                          rw_g_up, rw_k_k, rw_k_a, rw_r_k, rw_gn_g, rw_gn_b, gla_a_up, gla_a_bias, gla_norm_g,
                          w_pa, w_pb, w_out, router_w, router_b, exp_gate, exp_up, exp_down, sh_gate, sh_up,
                          sh_down)
        p["final_g"] = fg
        mod_all = _mod_call(c_all, p["ada_w"], p["ada_b"])
        mods = [mod_all[:bp], mod_all[bp:]]
        states_in = [
            (zeros((bp, RW_HEADS, RW_HEAD, RW_HEAD)), zeros((bp, RW_SHIFT_COLS)),
             zeros((bp, GLA_HEADS, GLA_DK, GLA_DV))),
            (state_rwkv[l], state_shift[l], state_gla[l]),
        ]
        x1s, shared = [], None
        firsts = [0, tp]
        for gi in range(2):
            x1, h2_all, lg_all, st = _mixer_group(xs_g[gi], mods[gi], *states_in[gi], p, tn, firsts[gi], shared)
            shared = (h2_all, lg_all)
            x1s.append(x1)
            new_states[gi].append(st)
        ob, dest, wts = _moe(*shared, p)
        assert depth == 1, "the fused final norm assumes a single layer"
        xs_g = [_combine_call(dest, wts, firsts[gi], ob, shared[0], x1s[gi], mods[gi], p) for gi in range(2)]
    stack = lambda gi, j: new_states[gi][0][j][None] if depth == 1 else jnp.stack([s[j] for s in new_states[gi]])
    return (xs_g[0], xs_g[1], stack(0, 0), stack(0, 1), stack(0, 2), stack(1, 0), stack(1, 1), stack(1, 2))
```

```python
import functools

import jax
import jax.numpy as jnp
from jax import lax
from jax.experimental import pallas as pl
from jax.experimental.pallas import tpu as pltpu

F32, BF16, I32 = jnp.float32, jnp.bfloat16, jnp.int32

D_MODEL = 1024
RW_HEADS, RW_HEAD = 8, 64
RW_WIDTH = RW_HEADS * RW_HEAD
RW_W_RANK, RW_A_RANK, RW_G_RANK = 64, 64, 128
RW_GN_EPS = 64e-5
GLA_HEADS, GLA_DK, GLA_DV = 4, 64, 128
GLA_KW, GLA_VW = GLA_HEADS * GLA_DK, GLA_HEADS * GLA_DV
GLA_GATE_RANK = 16
GLA_GATE_TAU = 16.0
GLA_CHUNK = 16
RW_SHIFT_COLS = 3 * RW_WIDTH + RW_W_RANK + RW_A_RANK + RW_G_RANK
N_EXPERTS, TOP_K, N_GROUPS, TOPK_GROUPS = 256, 8, 8, 4
GROUP_SIZE = N_EXPERTS // N_GROUPS
EXPERT_FF = 256
ROUTED_SCALE = 2.5
NORM_EPS = 1e-6

LANES = 128
SUBLANES = 8
CHUNKS = D_MODEL // LANES
PCH = CHUNKS // 2
UNIT = 64
RW_SCAN_PASSES = (1, 1, 1, 1, 1)
GLA_UNITS_PER_STEP = 4
RW_UNITS_PER_STEP = 4
VMEM_LIMIT = 56 * 1024 * 1024

PA_W, QKV_W, XAL_W, GG_W, MG_W = RW_SHIFT_COLS, 2 * GLA_KW + GLA_VW, LANES, GLA_VW, 2 * D_MODEL
PACK_OFFS = (0, PA_W, PA_W + QKV_W, PA_W + QKV_W + XAL_W, PA_W + QKV_W + XAL_W + GG_W)
PACK_W = PA_W + QKV_W + XAL_W + GG_W + MG_W

TOK_TILE = 256
MOE_BLK = 256
EXPERT_PARTS = 1
WEIGHT_SLOTS = 3
ROW_SLOTS = 3
CMB_TILE = 256
CMB_GROUP = 16

_DN = {
    "nn": (((1,), (0,)), ((), ())),
    "nt": (((1,), (1,)), ((), ())),
    "tn": (((0,), (0,)), ((), ())),
}


def _split(x, pieces):
    out, rem = [], x
    for i in range(pieces):
        p = rem.astype(BF16)
        out.append(p)
        if i + 1 < pieces:
            rem = rem - p.astype(F32)
    return out


def _mm(a, b, form="nn", passes=1):
    dn = _DN[form]
    if passes == 6:
        return lax.dot_general(a.astype(F32), b.astype(F32), dn, precision=lax.Precision.HIGHEST,
                               preferred_element_type=F32)
    if passes == 1:
        return lax.dot_general(a.astype(BF16), b.astype(BF16), dn, preferred_element_type=F32)
    ah, al = _split(a, 2)
    bh, bl = _split(b, 2)
    out = lax.dot_general(ah, bh, dn, preferred_element_type=F32)
    out = out + lax.dot_general(ah, bl, dn, preferred_element_type=F32)
    return out + lax.dot_general(al, bh, dn, preferred_element_type=F32)


def _mm01(m01, x, pieces=3):
    m = m01.astype(BF16)
    out = None
    for p in _split(x, pieces):
        t = lax.dot_general(m, p, _DN["nn"], preferred_element_type=F32)
        out = t if out is None else out + t
    return out


def _xmm01(x, m01, pieces=2):
    m = m01.astype(BF16)
    out = None
    for p in _split(x, pieces):
        t = lax.dot_general(p, m, _DN["nn"], preferred_element_type=F32)
        out = t if out is None else out + t
    return out


HI16 = -65536


def _bf16_bits(x):
    return lax.bitcast_convert_type(x.astype(BF16).astype(F32), I32)


def _unpack_pair(w):
    return lax.bitcast_convert_type(w << 16, F32), lax.bitcast_convert_type(w & HI16, F32)


def _rows_to_packed(ref, x, first=0):
    for c in range(PCH):
        lo = _bf16_bits(x[:, c * LANES:(c + 1) * LANES])
        hi = _bf16_bits(x[:, (c + PCH) * LANES:(c + PCH + 1) * LANES])
        ref[pl.ds(first * PCH + c, x.shape[0], stride=PCH), :] = ((lo >> 16) & 0xFFFF) | (hi & HI16)


def _rows_from_packed(ref, n, live=None, first=0):
    lows, highs = [], []
    for c in range(PCH):
        w = ref[pl.ds(first * PCH + c, n, stride=PCH), :]
        if live is not None:
            w = jnp.where(live, w, 0)
        lo, hi = _unpack_pair(w)
        lows.append(lo.astype(BF16))
        highs.append(hi.astype(BF16))
    return jnp.concatenate(lows + highs, axis=1)


def _slab(ref, row):
    return ref.at[pl.ds(pl.multiple_of(row * PCH, PCH), PCH)]


def _fslab(ref, row):
    return ref.at[pl.ds(pl.multiple_of(row * CHUNKS, CHUNKS), CHUNKS)]


def _sigmoid(x):
    return 1.0 / (1.0 + jnp.exp(-x))


def _softplus(x):
    return jnp.maximum(x, 0.0) + jnp.log(1.0 + jnp.exp(-jnp.abs(x)))


def _log2(n):
    assert n > 0 and n & (n - 1) == 0, n
    return n.bit_length() - 1


def _cparams(sem, vmem=None):
    return pltpu.CompilerParams(dimension_semantics=sem, vmem_limit_bytes=vmem)


def _mod_body(c_ref, w_ref, b_ref, o_ref):
    c = c_ref[...]
    o_ref[0] = _mm(c * _sigmoid(c), w_ref[...], passes=3) + b_ref[...]


def _mod_call(c_all, ada_w, ada_b):
    bt, d = c_all.shape
    out = pl.pallas_call(
        _mod_body,
        grid=(6,),
        in_specs=[pl.BlockSpec((bt, d), lambda k: (0, 0)),
                  pl.BlockSpec((d, d), lambda k: (0, k)),
                  pl.BlockSpec((1, d), lambda k: (0, k))],
        out_specs=pl.BlockSpec((1, bt, d), lambda k: (k, 0, 0)),
        out_shape=jax.ShapeDtypeStruct((6, bt, d), F32),
        compiler_params=_cparams(("arbitrary",)),
        name="adaln_mod",
    )(c_all, ada_w, ada_b.reshape(1, 6 * d))
    return jnp.transpose(out, (1, 0, 2))


def _inproj_body(x_ref, mod_ref, g_ref, w_ref, pa_ref, qkv_ref, xal_ref, gg_ref, mg_ref):
    bb, ll, d = x_ref.shape
    x = x_ref[...]
    y = x * lax.rsqrt(jnp.mean(x * x, axis=-1, keepdims=True) + NORM_EPS) * g_ref[...]
    h = y * (1.0 + mod_ref[:, 1:2, :]) + mod_ref[:, 0:1, :]
    hb = h.reshape(bb * ll, d).astype(BF16)
    for ref, off in zip((pa_ref, qkv_ref, xal_ref, gg_ref, mg_ref), PACK_OFFS):
        w = ref.shape[-1]
        ref[...] = jnp.dot(hb, w_ref[:, off:off + w], preferred_element_type=F32).reshape(bb, ll, w)


def _tile(bn, seq, tile):
    if seq >= tile:
        assert seq % tile == 0
        return 1, tile
    assert tile % seq == 0 and bn % (tile // seq) == 0
    return tile // seq, seq


def _rwprep_body(pa_ref, sh_ref, mu_ref, w0_ref, wup_ref, a0_ref, aup_ref, gup_ref, kk_ref, ka_ref, rk_ref,
                 bd_ref, r_o, lw_o, k_o, v_o, a_o, b_o, g_o, bon_o, nsh_o, carry):
    bb, ll, wd = pa_ref.shape
    n = bb * ll
    hw = RW_WIDTH

    @pl.when(pl.program_id(1) == 0)
    def _():
        carry[...] = sh_ref[...]

    pa = pa_ref[...]
    rolled = pltpu.roll(pa.reshape(n, wd), 1, 0).reshape(bb, ll, wd)
    tok = lax.broadcasted_iota(I32, (bb, ll, wd), 1)
    prev = jnp.where(tok == 0, carry[...], rolled)
    last = pa_ref[:, ll - 1:ll, :]
    carry[...] = last
    nsh_o[...] = last
    xs = (pa + (prev - pa) * mu_ref[...]).reshape(n, wd)

    r, k, v = xs[:, 0:hw], xs[:, hw:2 * hw], xs[:, 2 * hw:3 * hw]
    xwa = xs[:, 3 * hw:3 * hw + LANES]
    xg = xs[:, 3 * hw + LANES:]
    w_log = -_softplus(-(w0_ref[...] + _mm(jnp.tanh(xwa), wup_ref[...], passes=3))) - 0.5
    lw = -jnp.exp(w_log)
    a = _sigmoid(a0_ref[...] + _mm(xwa, aup_ref[...], passes=3))
    g = _mm(_sigmoid(xg), gup_ref[...])
    bd = bd_ref[...]
    kkv = k * kk_ref[...]
    kkn = kkv * lax.rsqrt(jnp.maximum(_xmm01(kkv * kkv, bd, pieces=1), 1e-24))
    k2 = k * (1.0 + (a - 1.0) * ka_ref[...])
    bonus = _xmm01(r * k2 * rk_ref[...], bd, pieces=1) * v
    for ref, val in ((r_o, r), (lw_o, lw), (k_o, k2), (v_o, v), (a_o, -kkn), (b_o, kkn * a), (g_o, g),
                     (bon_o, bonus)):
        ref[...] = val.reshape(bb, ll, hw)


def _inproj_prep_body(x_ref, mod_ref, g_ref, w_ref, sh_ref, mu_ref, w0_ref, wup_ref, a0_ref, aup_ref, gup_ref, kk_ref,
                      ka_ref, rk_ref, bd_ref, qkv_o, xal_o, gg_o, mg_o, r_o, lw_o, k_o, v_o, a_o, b_o, g_o, bon_o,
                      nsh_o, pa_s, carry):
    _inproj_body(x_ref, mod_ref, g_ref, w_ref, pa_s, qkv_o, xal_o, gg_o, mg_o)
    _rwprep_body(pa_s, sh_ref, mu_ref, w0_ref, wup_ref, a0_ref, aup_ref, gup_ref, kk_ref, ka_ref, rk_ref, bd_ref,
                 r_o, lw_o, k_o, v_o, a_o, b_o, g_o, bon_o, nsh_o, carry)


def _inproj_prep_call(x, mod, s_sh, p):
    bn, seq, d = x.shape
    bb, ll = _tile(bn, seq, TOK_TILE)
    hw, wd = RW_WIDTH, PA_W
    tok = lambda w: pl.BlockSpec((bb, ll, w), lambda b, l: (b, l, 0))
    row = lambda w: pl.BlockSpec((bb, 1, w), lambda b, l: (b, 0, 0))
    full = lambda a: pl.BlockSpec(a.shape, lambda b, l: (0,) * a.ndim)
    consts = (p["mu"], p["w0"], p["wup"], p["a0"], p["aup"], p["gup"], p["kk"], p["ka"], p["rk"], p["bd64"])
    proj_w = (QKV_W, XAL_W, GG_W, MG_W)
    shapes = lambda ws: [jax.ShapeDtypeStruct((bn, seq, w), F32) for w in ws]
    return pl.pallas_call(
        _inproj_prep_body,
        grid=(bn // bb, seq // ll),
        in_specs=[tok(d), pl.BlockSpec((bb, 6, d), lambda b, l: (b, 0, 0)), full(p["norm1_g"]), full(p["w_pack"]),
                  row(wd)] + [full(c) for c in consts],
        out_specs=[tok(w) for w in proj_w] + [tok(hw)] * 8 + [row(wd)],
        out_shape=shapes(proj_w) + shapes((hw,) * 8) + [jax.ShapeDtypeStruct((bn, 1, wd), F32)],
        scratch_shapes=[pltpu.VMEM((bb, ll, wd), F32), pltpu.VMEM((bb, 1, wd), F32)],
        compiler_params=_cparams(("arbitrary", "arbitrary"), VMEM_LIMIT),
        name="norm_inproj_prep",
    )(x, mod, p["norm1_g"], p["w_pack"], s_sh.reshape(bn, 1, wd), *consts)


def _unit_masks(n, tl):
    ri = lax.broadcasted_iota(I32, (n, n), 0)
    ci = lax.broadcasted_iota(I32, (n, n), 1)
    same = (ri >> _log2(tl)) == (ci >> _log2(tl))
    return same, same & (ri > ci), same & (ri >= ci)


def _rwscan_body(r_ref, lw_ref, k_ref, v_ref, a_ref, b_ref, s0_ref, y_ref, sn_ref, st, *, nu, nseq, tl, passes):
    n = nseq * tl
    n2 = 2 * n
    p_aa, p_inv, p_apply, p_state, p_y = passes

    hd = RW_HEAD

    @pl.when(pl.program_id(1) == 0)
    def _():
        zero = jnp.zeros((hd, hd), F32)
        for q in range(nu * nseq):
            for p in range(RW_HEADS // 2):
                st[q, p] = jnp.concatenate(
                    [jnp.concatenate([s0_ref[q, 2 * p], zero], axis=1),
                     jnp.concatenate([zero, s0_ref[q, 2 * p + 1]], axis=1)], axis=0)

    same, _, incl = _unit_masks(n, tl)
    m_cum = jnp.where(incl, 1.0, 0.0)
    m_seq = jnp.where(same, 1.0, 0.0)
    ri = lax.broadcasted_iota(I32, (n2, n2), 0)
    ci = lax.broadcasted_iota(I32, (n2, n2), 1)
    rt, ct = ri & (n - 1), ci & (n - 1)
    dsame = ((rt >> _log2(tl)) == (ct >> _log2(tl))) & ((ri >> _log2(n)) == (ci >> _log2(n)))
    strict_d = dsame & (rt > ct)
    incl_d = dsame & (rt >= ct)
    eye_d = jnp.where(ri == ci, 1.0, 0.0)
    lane = lax.broadcasted_iota(I32, (1, LANES), 1)
    m0 = jnp.where(lane < RW_HEAD, 1.0, 0.0)
    m1 = 1.0 - m0

    def dup(x):
        return jnp.concatenate([x * m0, x * m1], axis=0)

    def seq_rows(x, q):
        if nseq == 1:
            return x
        return jnp.concatenate([x[q * tl:(q + 1) * tl], x[n + q * tl:n + (q + 1) * tl]], axis=0)

    def unit_rows(parts):
        if nseq == 1:
            return parts[0]
        return jnp.concatenate([p[0:tl] for p in parts] + [p[tl:2 * tl] for p in parts], axis=0)

    chains = [(u, p) for u in range(nu) for p in range(RW_HEADS // 2)]
    ids = range(len(chains))
    cat0 = lambda *xs: jnp.concatenate(xs, axis=0)

    def ld(ref, c):
        u, p = chains[c]
        return ref[u * nseq:(u + 1) * nseq, :, p * LANES:(p + 1) * LANES].reshape(n, LANES)

    lw = [ld(lw_ref, c) for c in ids]
    cum = [_mm01(m_cum, x) for x in lw]
    tot = [_mm01(m_seq, x) for x in lw]
    e_c = [jnp.exp(x) for x in cum]
    e_n = [jnp.exp(-x) for x in cum]
    e_l = [jnp.exp(t - x) for t, x in zip(tot, cum)]
    at_d = [dup(ld(a_ref, c) * jnp.exp(cum[c] - lw[c])) for c in ids]
    rt_d = [dup(ld(r_ref, c) * e_c[c]) for c in ids]
    bt_d = [dup(ld(b_ref, c) * e_n[c]) for c in ids]
    kt_d = [dup(ld(k_ref, c) * e_n[c]) for c in ids]
    bh_d = [dup(ld(b_ref, c) * e_l[c]) for c in ids]
    kh_d = [dup(ld(k_ref, c) * e_l[c]) for c in ids]
    v_d = [dup(ld(v_ref, c)) for c in ids]
    aa = [_mm(cat0(at_d[c], rt_d[c]), cat0(bt_d[c], kt_d[c]), "nt", p_aa) for c in ids]
    a_ab = [jnp.where(strict_d, x[0:n2, 0:n2], 0.0) for x in aa]
    a_ak = [jnp.where(strict_d, x[0:n2, n2:], 0.0) for x in aa]
    a_rb = [jnp.where(incl_d, x[n2:, 0:n2], 0.0) for x in aa]
    a_rk = [jnp.where(incl_d, x[n2:, n2:], 0.0) for x in aa]
    zy = [_mm(cat0(a_ak[c], a_rk[c]), v_d[c], passes=p_apply) for c in ids]
    tinv = [eye_d + x for x in a_ab]
    nk = a_ab
    for _ in range(_log2(tl) - 1):
        nk = [_mm(x, x, passes=p_inv) for x in nk]
        tinv = [t + _mm(t, x, passes=p_inv) for t, x in zip(tinv, nk)]
    wu = [_mm(tinv[c], jnp.concatenate([at_d[c], zy[c][0:n2]], axis=1), passes=p_apply) for c in ids]
    seqs = range(nseq)
    srow = lambda c, q: (chains[c][0] * nseq + q, chains[c][1])
    s_old = [[st[srow(c, q)] for q in seqs] for c in ids]
    xs = [[_mm(cat0(seq_rows(wu[c][:, 0:LANES], q), seq_rows(rt_d[c], q)), s_old[c][q], "nt", p_state)
           for q in seqs] for c in ids]
    u_q = [[xs[c][q][0:2 * tl] + seq_rows(wu[c][:, LANES:], q) for q in seqs] for c in ids]
    for c in ids:
        for q in seqs:
            g_c = jnp.exp(tot[c][q * tl:q * tl + 1, :])
            st[srow(c, q)] = s_old[c][q] * g_c + _mm(cat0(u_q[c][q], seq_rows(v_d[c], q)),
                                                     cat0(seq_rows(bh_d[c], q), seq_rows(kh_d[c], q)), "tn", p_state)
    for c in ids:
        u, p = chains[c]
        y_d = (unit_rows([xs[c][q][2 * tl:] for q in seqs]) + _mm(a_rb[c], unit_rows(u_q[c]), passes=p_y)
               + zy[c][n2:])
        y_ref[u * nseq:(u + 1) * nseq, :, p * LANES:(p + 1) * LANES] = (y_d[0:n] + y_d[n:]).reshape(nseq, tl, LANES)

    @pl.when(pl.program_id(1) == pl.num_programs(1) - 1)
    def _():
        for q in range(nu * nseq):
            for p in range(RW_HEADS // 2):
                s = st[q, p]
                sn_ref[q, 2 * p] = s[0:hd, 0:hd]
                sn_ref[q, 2 * p + 1] = s[hd:, hd:]


def _unit_shape(bn, seq):
    if seq >= UNIT:
        assert seq % UNIT == 0
        return 1, UNIT
    assert UNIT % seq == 0 and bn % (UNIT // seq) == 0
    return UNIT // seq, seq


def _rwscan_call(r, lw, k2, v, a_s, b_s, s0, passes=RW_SCAN_PASSES):
    bn, seq, hw = r.shape
    nseq, tl = _unit_shape(bn, seq)
    nu = RW_UNITS_PER_STEP if bn % (RW_UNITS_PER_STEP * nseq) == 0 else 1
    rows = nu * nseq
    tok = pl.BlockSpec((rows, tl, hw), lambda b, c: (b, c, 0))
    stt = pl.BlockSpec((rows, RW_HEADS, RW_HEAD, RW_HEAD), lambda b, c: (b, 0, 0, 0))
    return pl.pallas_call(
        functools.partial(_rwscan_body, nu=nu, nseq=nseq, tl=tl, passes=passes),
        grid=(bn // rows, seq // tl),
        in_specs=[tok] * 6 + [stt],
        out_specs=[tok, stt],
        out_shape=[jax.ShapeDtypeStruct((bn, seq, hw), F32), jax.ShapeDtypeStruct(s0.shape, F32)],
        scratch_shapes=[pltpu.VMEM((rows, RW_HEADS // 2, LANES, LANES), F32)],
        compiler_params=_cparams(("arbitrary", "arbitrary"), VMEM_LIMIT),
        name="rwkv_scan",
    )(r, lw, k2, v, a_s, b_s, s0)


def _gla_body(qkv_ref, xal_ref, gate_ref, aup_ref, ab_ref, ng_ref, s0_ref, o_ref, sn_ref, st, *, nu, nseq, tl, cs):
    n = nseq * tl
    n2 = 2 * n
    nsub = tl // cs

    @pl.when(pl.program_id(1) == 0)
    def _():
        zero = jnp.zeros((GLA_DV, GLA_DK), F32)
        for q in range(nu * nseq):
            for p in range(GLA_HEADS // 2):
                st[q, p] = jnp.concatenate(
                    [jnp.concatenate([s0_ref[q, 2 * p].T, zero], axis=1),
                     jnp.concatenate([zero, s0_ref[q, 2 * p + 1].T], axis=1)], axis=0)

    same, _, incl = _unit_masks(n, cs)
    m_cum = jnp.where(incl, 1.0, 0.0)
    m_sub = jnp.where(same, 1.0, 0.0)
    ri = lax.broadcasted_iota(I32, (n2, n2), 0)
    ci = lax.broadcasted_iota(I32, (n2, n2), 1)
    rt, ct = ri & (n - 1), ci & (n - 1)
    causal_d = ((rt >> _log2(cs)) == (ct >> _log2(cs))) & ((ri >> _log2(n)) == (ci >> _log2(n))) & (rt >= ct)
    lane = lax.broadcasted_iota(I32, (1, LANES), 1)
    m0 = jnp.where(lane < GLA_DK, 1.0, 0.0)
    m1 = 1.0 - m0
    sr = lax.broadcasted_iota(I32, (2 * GLA_DV, LANES), 0)
    sc = lax.broadcasted_iota(I32, (2 * GLA_DV, LANES), 1)
    st_mask = jnp.where((sr >> _log2(GLA_DV)) == (sc >> _log2(GLA_DK)), 1.0, 0.0)

    def dup(x):
        return jnp.concatenate([x * m0, x * m1], axis=0)

    chains = [(u, p) for u in range(nu) for p in range(GLA_HEADS // 2)]
    ids = range(len(chains))
    urows = lambda u: slice(u * nseq, (u + 1) * nseq)
    ng = ng_ref[...]
    la_all = [-_softplus(-(_mm(xal_ref[urows(u), :, :].reshape(n, LANES), aup_ref[...], passes=3) + ab_ref[...]))
              * (1.0 / GLA_GATE_TAU) for u in range(nu)]

    def ld(ref, c, off, width):
        return ref[urows(chains[c][0]), :, off:off + width].reshape(n, width)

    q = [ld(qkv_ref, c, chains[c][1] * LANES, LANES) * (GLA_DK ** -0.5) for c in ids]
    k = [ld(qkv_ref, c, GLA_KW + chains[c][1] * LANES, LANES) for c in ids]
    vp = [ld(qkv_ref, c, 2 * GLA_KW + chains[c][1] * 2 * GLA_DV, 2 * GLA_DV) for c in ids]
    la = [la_all[u][:, p * LANES:(p + 1) * LANES] for u, p in chains]
    bc = [_mm01(m_cum, x) for x in la]
    bl = [_mm01(m_sub, x) for x in la]
    qe = [q[c] * jnp.exp(bc[c]) for c in ids]
    ke = [k[c] * jnp.exp(-bc[c]) for c in ids]
    kd = [k[c] * jnp.exp(bl[c] - bc[c]) for c in ids]
    att = [jnp.where(causal_d, _mm(dup(qe[c]), dup(ke[c]), "nt", passes=1), 0.0) for c in ids]
    v_st = [jnp.concatenate([x[:, 0:GLA_DV], x[:, GLA_DV:]], axis=0) for x in vp]
    o_st = [_mm(att[c], v_st[c], passes=1) for c in ids]
    upd = [[_mm(vp[c][r0:r0 + cs], kd[c][r0:r0 + cs], "tn", passes=1) for r0 in range(0, n, cs)] for c in ids]
    inter = [[None] * (n // cs) for _ in ids]
    for sq in range(nseq):
        s = [st[chains[c][0] * nseq + sq, chains[c][1]] for c in ids]
        for j in range(nsub):
            i = sq * nsub + j
            r0 = i * cs
            for c in ids:
                inter[c][i] = _mm(qe[c][r0:r0 + cs], s[c], "nt", passes=1)
                s[c] = s[c] * jnp.exp(bl[c][r0:r0 + 1, :]) + st_mask * upd[c][i]
        for c in ids:
            st[chains[c][0] * nseq + sq, chains[c][1]] = s[c]
    for c in ids:
        u, p = chains[c]
        o = o_st[c] + jnp.concatenate([x[:, 0:GLA_DV] for x in inter[c]] + [x[:, GLA_DV:] for x in inter[c]], axis=0)
        o = o * lax.rsqrt(jnp.mean(o * o, axis=-1, keepdims=True) + NORM_EPS) * ng
        goff = p * 2 * GLA_DV
        gp = ld(gate_ref, c, goff, 2 * GLA_DV)
        g_st = jnp.concatenate([gp[:, 0:GLA_DV], gp[:, GLA_DV:]], axis=0)
        ob = o * (g_st * _sigmoid(g_st))
        o_ref[urows(u), :, goff:goff + GLA_DV] = ob[0:n].reshape(nseq, tl, GLA_DV)
        o_ref[urows(u), :, goff + GLA_DV:goff + 2 * GLA_DV] = ob[n:].reshape(nseq, tl, GLA_DV)

    @pl.when(pl.program_id(1) == pl.num_programs(1) - 1)
    def _():
        for q in range(nu * nseq):
            for p in range(GLA_HEADS // 2):
                s = st[q, p]
                sn_ref[q, 2 * p] = s[0:GLA_DV, 0:GLA_DK].T
                sn_ref[q, 2 * p + 1] = s[GLA_DV:, GLA_DK:].T


def _gla_call(qkv, xal, gate, s0, p):
    bn, seq, _ = qkv.shape
    nseq, tl = _unit_shape(bn, seq)
    cs = min(GLA_CHUNK, seq)
    assert tl % cs == 0
    nu = GLA_UNITS_PER_STEP if bn % (GLA_UNITS_PER_STEP * nseq) == 0 else 1
    rows = nu * nseq
    tok = lambda w: pl.BlockSpec((rows, tl, w), lambda b, c: (b, c, 0))
    full = lambda a: pl.BlockSpec(a.shape, lambda b, c: (0,) * a.ndim)
    stt = pl.BlockSpec((rows, GLA_HEADS, GLA_DK, GLA_DV), lambda b, c: (b, 0, 0, 0))
    consts = (p["gla_aup"], p["gla_ab"], p["gla_ng"])
    return pl.pallas_call(
        functools.partial(_gla_body, nu=nu, nseq=nseq, tl=tl, cs=cs),
        grid=(bn // rows, seq // tl),
        in_specs=[tok(QKV_W), tok(XAL_W), tok(GG_W)] + [full(c) for c in consts] + [stt],
        out_specs=[tok(GLA_VW), stt],
        out_shape=[jax.ShapeDtypeStruct((bn, seq, GLA_VW), F32), jax.ShapeDtypeStruct(s0.shape, F32)],
        scratch_shapes=[pltpu.VMEM((rows, GLA_HEADS // 2, 2 * GLA_DV, LANES), F32)],
        compiler_params=_cparams(("arbitrary", "arbitrary"), VMEM_LIMIT),
        name="gla_chunked",
    )(qkv, xal, gate, *consts, s0)


def _merge_body(y_ref, g_ref, bon_ref, ob_ref, mg_ref, x_ref, mod_ref, gng_ref, gnb_ref, bd_ref, wpa_ref,
                wpb_ref, wout_ref, n2_ref, rwh_ref, rwl_ref, *rest):
    x1_o, h2_o, lg_o = rest[-3:]
    bb, ll, d = x_ref.shape
    n = bb * ll
    hw = RW_WIDTH
    bd = bd_ref[...]
    y = y_ref[...].reshape(n, hw)
    mu = _xmm01(y, bd, pieces=2) * (1.0 / RW_HEAD)
    dv = y - mu
    var = _xmm01(dv * dv, bd, pieces=1) * (1.0 / RW_HEAD)
    yn = dv * lax.rsqrt(var + RW_GN_EPS) * gng_ref[...] + gnb_ref[...]
    o_a = (yn + bon_ref[...].reshape(n, hw)) * g_ref[...].reshape(n, hw)
    o_b = ob_ref[...].reshape(n, GLA_VW)
    mg = mg_ref[...].reshape(n, 2 * d)
    merged = _sigmoid(mg[:, 0:d]) * _mm(o_a, wpa_ref[...]) + _sigmoid(mg[:, d:]) * _mm(o_b, wpb_ref[...])
    mix = _mm(merged, wout_ref[...]).reshape(bb, ll, d)
    x1 = x_ref[...] + mod_ref[:, 2:3, :] * mix
    x1_o[...] = x1
    yn2 = x1 * lax.rsqrt(jnp.mean(x1 * x1, axis=-1, keepdims=True) + NORM_EPS) * n2_ref[...]
    h2 = (yn2 * (1.0 + mod_ref[:, 4:5, :]) + mod_ref[:, 3:4, :]).reshape(n, d)
    hh, hl = _split(h2, 2)
    rwh, rwl = rwh_ref[...], rwl_ref[...]
    nt = lambda a, b: lax.dot_general(a, b, _DN["nt"], preferred_element_type=F32)
    lg_o[...] = nt(rwh, hh) + nt(rwl, hh) + nt(rwh, hl)
    _rows_to_packed(h2_o, h2)


def _merge_call(y, g, bonus, o_b, mg, x, mod, p, tn, first_tok, shared):
    bn, seq, d = x.shape
    bb, ll = _tile(bn, seq, TOK_TILE)
    nl = seq // ll
    assert first_tok % (bb * ll) == 0
    t0 = first_tok // (bb * ll)
    n_in = 7 + 9
    extra = [] if shared is None else list(shared)
    alias = {} if shared is None else {n_in: 1, n_in + 1: 2}
    tok = lambda w: pl.BlockSpec((bb, ll, w), lambda b, l: (b, l, 0))
    full = lambda a: pl.BlockSpec(a.shape, lambda b, l: (0,) * a.ndim)
    consts = (p["gn_g"], p["gn_b"], p["bd64"], p["w_pa"], p["w_pb"], p["w_out"], p["norm2_g"], p["rw_hi"],
              p["rw_lo"])
    return pl.pallas_call(
        _merge_body,
        grid=(bn // bb, nl),
        in_specs=[tok(RW_WIDTH)] * 3 + [tok(GLA_VW), tok(MG_W), tok(d),
                                        pl.BlockSpec((bb, 6, d), lambda b, l: (b, 0, 0))] + [full(c) for c in consts]
        + [pl.BlockSpec(memory_space=pl.ANY)] * len(extra),
        out_specs=[tok(d),
                   pl.BlockSpec((bb * ll * PCH, LANES), lambda b, l: (t0 + b * nl + l, 0)),
                   pl.BlockSpec((N_EXPERTS, bb * ll), lambda b, l: (0, t0 + b * nl + l))],
        out_shape=[jax.ShapeDtypeStruct((bn, seq, d), F32),
                   jax.ShapeDtypeStruct((tn * PCH, LANES), I32),
                   jax.ShapeDtypeStruct((N_EXPERTS, tn), F32)],
        input_output_aliases=alias,
        compiler_params=_cparams(("arbitrary", "arbitrary"), VMEM_LIMIT),
        name="merge_outproj_router",
    )(y, g, bonus, o_b, mg, x, mod, *consts, *extra)


def _route_body(lg_ref, rb_ref, e_o, rk_o, w_o, cnt_o, carry):
    ne, tm = lg_ref.shape

    @pl.when(pl.program_id(0) == 0)
    def _():
        carry[...] = jnp.zeros_like(carry)

    neg = -jnp.inf
    scores = _sigmoid(lg_ref[...])
    sel = scores + rb_ref[...]
    row_i = lax.broadcasted_iota(I32, (ne, tm), 0)
    row = row_i.astype(F32)
    grp = (row_i >> _log2(GROUP_SIZE)).astype(F32)

    def first_max(x, ids, none):
        m = jnp.max(x, axis=0, keepdims=True)
        return m, jnp.min(jnp.where(x == m, ids, none), axis=0, keepdims=True)

    gs = []
    gids = lax.broadcasted_iota(I32, (GROUP_SIZE, tm), 0)
    for gidx in range(N_GROUPS):
        rows = slice(gidx * GROUP_SIZE, (gidx + 1) * GROUP_SIZE)
        sg = _sigmoid(lg_ref[rows, :]) + rb_ref[rows, :]
        ids = (gids + gidx * GROUP_SIZE).astype(F32)
        m1, i1 = first_max(sg, ids, float(ne))
        gs.append(m1 + jnp.max(jnp.where(ids == i1, neg, sg), axis=0, keepdims=True))
    gs = jnp.concatenate(gs, axis=0)
    gid = lax.broadcasted_iota(I32, (N_GROUPS, tm), 0).astype(F32)
    cur = jnp.full((ne, tm), neg, F32)
    for _ in range(TOPK_GROUPS):
        _, gi = first_max(gs, gid, float(N_GROUPS))
        cur = jnp.where(grp == gi, sel, cur)
        gs = jnp.where(gid == gi, neg, gs)

    pm = jnp.zeros((ne, tm), F32)
    eidx, wts = [], []
    for _ in range(TOP_K):
        _, ei = first_max(cur, row, float(ne))
        hit = row == ei
        pm = jnp.where(hit, 1.0, pm)
        eidx.append(ei)
        wts.append(jnp.sum(jnp.where(hit, scores, 0.0), axis=0, keepdims=True))
        cur = jnp.where(hit, neg, cur)
    wsum = wts[0]
    for w in wts[1:]:
        wsum = wsum + w

    ri = lax.broadcasted_iota(I32, (tm, tm), 0)
    ci = lax.broadcasted_iota(I32, (tm, tm), 1)
    earlier = jnp.where(ri < ci, 1.0, 0.0)
    rank = _mm(pm, earlier, passes=1) + carry[...]
    carry[...] = carry[...] + jnp.sum(pm, axis=1, keepdims=True)
    cnt_o[...] = carry[...]

    rks = [jnp.sum(jnp.where(row == e, rank, 0.0), axis=0, keepdims=True) for e in eidx]
    e_o[0] = jnp.concatenate(eidx, axis=0).astype(I32)
    rk_o[0] = jnp.concatenate(rks, axis=0).astype(I32)
    w_o[0] = jnp.concatenate([w / wsum * ROUTED_SCALE for w in wts], axis=0)


def _route_call(logits_t, router_b):
    ne, tn = logits_t.shape
    tm = TOK_TILE
    assert tn % tm == 0
    col = pl.BlockSpec((ne, 1), lambda i: (0, 0))
    tab = pl.BlockSpec((1, TOP_K, tm), lambda i: (i, 0, 0))
    tab_shape = (tn // tm, TOP_K, tm)
    return pl.pallas_call(
        _route_body,
        grid=(tn // tm,),
        in_specs=[pl.BlockSpec((ne, tm), lambda i: (0, i)), col],
        out_specs=[tab, tab, tab, col],
        out_shape=[jax.ShapeDtypeStruct(tab_shape, I32), jax.ShapeDtypeStruct(tab_shape, I32),
                   jax.ShapeDtypeStruct(tab_shape, F32), jax.ShapeDtypeStruct((ne, 1), F32)],
        scratch_shapes=[pltpu.VMEM((ne, 1), F32)],
        compiler_params=_cparams(("arbitrary",)),
        name="moe_route",
    )(logits_t, router_b.reshape(ne, 1))


def _dest_body(e_ref, rk_ref, ps_ref, d_o):
    ne, tm = ps_ref.shape[0], e_ref.shape[2]
    ids = lax.broadcasted_iota(I32, (ne, tm), 0)
    ps = ps_ref[...]
    for t in range(e_ref.shape[0]):
        first = [jnp.sum(jnp.where(ids == e_ref[t, kk:kk + 1, :], ps, 0.0), axis=0, keepdims=True)
                 for kk in range(TOP_K)]
        d_o[t] = (jnp.concatenate(first, axis=0).astype(I32) + rk_ref[t]) * PCH


def _dest_call(eidx, rank, pad_start):
    nt, _, tm = eidx.shape
    ne = pad_start.shape[0]
    per = next(k for k in (4, 2, 1) if nt % k == 0)
    tab = pl.BlockSpec((per, TOP_K, tm), lambda i: (i, 0, 0))
    return pl.pallas_call(
        _dest_body,
        grid=(nt // per,),
        in_specs=[tab, tab, pl.BlockSpec((ne, 1), lambda i: (0, 0))],
        out_specs=tab,
        out_shape=jax.ShapeDtypeStruct(eidx.shape, I32),
        compiler_params=_cparams(("arbitrary",)),
        name="moe_dest",
    )(eidx, rank, pad_start.astype(F32).reshape(ne, 1))


def _pslab(ref, offset):
    return ref.at[pl.ds(pl.multiple_of(offset, PCH), PCH)]


def _dispatch_body(d_ref, h2_ref, xs_hbm, sem, *, tm):
    def issue(m, carry):
        for kk in range(TOP_K):
            pltpu.make_async_copy(_slab(h2_ref, m), _pslab(xs_hbm, d_ref[0, kk, m]), sem).start(priority=kk % 2)
        return carry

    lax.fori_loop(0, tm, issue, 0)
    all_rows = xs_hbm.at[pl.ds(0, tm * TOP_K * PCH)]
    pltpu.make_async_copy(all_rows, all_rows, sem).wait()


def _assign_spec(tm, index_map):
    return pl.BlockSpec((1, TOP_K, tm), index_map, memory_space=pltpu.SMEM)


def _dispatch_call(dest, h2s, n_rows):
    tn = h2s.shape[0] // PCH
    tm = TOK_TILE
    assert dest.shape == (tn // tm, TOP_K, tm)
    blk = _assign_spec(tm, lambda i: (i, 0, 0))
    return pl.pallas_call(
        functools.partial(_dispatch_body, tm=tm),
        grid=(tn // tm,),
        in_specs=[blk, pl.BlockSpec((tm * PCH, LANES), lambda i: (i, 0))],
        out_specs=pl.BlockSpec(memory_space=pl.ANY),
        out_shape=jax.ShapeDtypeStruct((n_rows * PCH, LANES), I32),
        scratch_shapes=[pltpu.SemaphoreType.DMA],
        compiler_params=_cparams(("arbitrary",)),
        name="moe_dispatch",
    )(dest, h2s)


def _expert_body(bi_ref, nr_ref, ld_ref, nx_ref, xs_hbm, wg_hbm, wu_hbm, wd_hbm, ob_ref, wg_buf, wu_buf, wd_buf,
                 wg_bf, wu_bf, wd_bf, xbuf, sem, xsem):
    i = pl.program_id(0)
    nsteps = pl.num_programs(0)
    nr = nr_ref[i]
    slot = ld_ref[i]
    blk_rows = MOE_BLK * PCH

    def row_block(j):
        s = lax.rem(j, ROW_SLOTS)
        src = xs_hbm.at[pl.ds(pl.multiple_of(bi_ref[j] * blk_rows, blk_rows), blk_rows)]
        return pltpu.make_async_copy(src, xbuf.at[s], xsem.at[s])

    @pl.when(i == 0)
    def _():
        for j in range(ROW_SLOTS - 1):
            row_block(j).start()

    @pl.when(i + ROW_SLOTS - 1 < nsteps)
    def _():
        row_block(i + ROW_SLOTS - 1).start()

    def fetch(e, s):
        return (pltpu.make_async_copy(wg_hbm.at[e], wg_buf.at[s], sem.at[s]),
                pltpu.make_async_copy(wu_hbm.at[e], wu_buf.at[s], sem.at[s]),
                pltpu.make_async_copy(wd_hbm.at[e], wd_buf.at[s], sem.at[s]))

    @pl.when(i == 0)
    def _():
        for s in range(WEIGHT_SLOTS - 1):
            e0 = nx_ref[nx_ref.shape[0] - (WEIGHT_SLOTS - 1) + s]

            @pl.when(e0 >= 0)
            def _():
                for k, cp in enumerate(fetch(e0, s)):
                    cp.start(priority=k % 2)

    @pl.when(slot >= 0)
    def _():
        for cp in fetch(0, slot):
            cp.wait()

        @pl.when(nx_ref[i] >= 0)
        def _():
            for k, cp in enumerate(fetch(nx_ref[i], lax.rem(slot + WEIGHT_SLOTS - 1, WEIGHT_SLOTS))):
                cp.start(priority=k % 2)

        wg_bf[...] = wg_buf[slot].astype(BF16)
        wu_bf[...] = wu_buf[slot].astype(BF16)
        wd_bf[...] = wd_buf[slot].astype(BF16)

    row_block(i).wait()

    @pl.when(nr > 0)
    def _():
        part = MOE_BLK // EXPERT_PARTS
        firsts = [q * part for q in range(EXPERT_PARTS)]
        rid = lax.broadcasted_iota(I32, (part, LANES), 0)
        xs_ref = xbuf.at[lax.rem(i, ROW_SLOTS)]
        x = [_rows_from_packed(xs_ref, part, rid < nr - f, f) for f in firsts]
        hg = [jnp.dot(v, wg_bf[...], preferred_element_type=F32) for v in x]
        hu = [jnp.dot(v, wu_bf[...], preferred_element_type=F32) for v in x]
        hh = [(g * _sigmoid(g) * u).astype(BF16) for g, u in zip(hg, hu)]
        out = [jnp.dot(v, wd_bf[...], preferred_element_type=F32) for v in hh]
        for f, v in zip(firsts, out):
            _rows_to_packed(ob_ref, v, f)


def _expert_tables(counts, pad_start, pad_end, nb):
    ne = counts.shape[0]
    experts = jnp.arange(ne, dtype=I32)
    first_row = jnp.arange(nb, dtype=I32) * MOE_BLK
    block_e = jnp.minimum(jnp.sum(pad_end[None, :] <= first_row[:, None], axis=1), ne - 1).astype(I32)
    mine = block_e[:, None] == experts[None, :]
    pick = lambda v: jnp.sum(jnp.where(mine, v[None, :], 0), axis=1)
    has = counts > 0
    ordinal = jnp.cumsum(has.astype(I32)) - 1
    start_b, count_b, ord_b = pick(pad_start), pick(counts), pick(ordinal)
    block_rows = jnp.clip(start_b + count_b - first_row, 0, MOE_BLK).astype(I32)
    block_i = jnp.minimum(jnp.arange(nb, dtype=I32), pad_end[-1] // MOE_BLK - 1).astype(I32)
    starts = (first_row == start_b) & (block_rows > 0)
    load_slot = jnp.where(starts, ord_b % WEIGHT_SLOTS, -1).astype(I32)
    nth = lambda want: jnp.max(jnp.where(has[None, :] & (ordinal[None, :] == want[:, None]), experts[None, :], -1),
                               axis=1)
    ahead = jnp.where(starts, nth(ord_b + WEIGHT_SLOTS - 1), -1)
    lead = nth(jnp.arange(WEIGHT_SLOTS - 1, dtype=I32))
    return block_i, block_rows, load_slot, jnp.concatenate([ahead, lead]).astype(I32)


def _expert_call(tables, xs, wg, wu, wd):
    nb = xs.shape[0] // (MOE_BLK * PCH)
    assert nb >= ROW_SLOTS
    d, ff = wg.shape[1], wg.shape[2]
    rows = pl.BlockSpec((MOE_BLK * PCH, LANES), lambda i, bi, nr, ld, nx: (bi[i], 0))
    hbm = pl.BlockSpec(memory_space=pl.ANY)
    grid_spec = pltpu.PrefetchScalarGridSpec(
        num_scalar_prefetch=4,
        grid=(nb,),
        in_specs=[hbm, hbm, hbm, hbm],
        out_specs=rows,
        scratch_shapes=[pltpu.VMEM((WEIGHT_SLOTS, d, ff), F32), pltpu.VMEM((WEIGHT_SLOTS, d, ff), F32),
                        pltpu.VMEM((WEIGHT_SLOTS, ff, d), F32),
                        pltpu.VMEM((d, ff), BF16), pltpu.VMEM((d, ff), BF16), pltpu.VMEM((ff, d), BF16),
                        pltpu.VMEM((ROW_SLOTS, MOE_BLK * PCH, LANES), I32),
                        pltpu.SemaphoreType.DMA((WEIGHT_SLOTS,)), pltpu.SemaphoreType.DMA((ROW_SLOTS,))],
    )
    return pl.pallas_call(
        _expert_body,
        grid_spec=grid_spec,
        out_shape=jax.ShapeDtypeStruct(xs.shape, I32),
        compiler_params=_cparams(("arbitrary",), VMEM_LIMIT),
        name="moe_experts",
    )(*tables, xs, wg, wu, wd)


def _combine_body(d_ref, dn_ref, wt_ref, ob_hbm, h2_ref, x1_ref, mod_ref, sg_ref, su_ref,
                  sd_ref, fg_ref, out_ref, gbuf, rbuf, wcol, sem, *, tm, nl):
    bb, ll, d = x1_ref.shape
    step = pl.program_id(0) * nl + pl.program_id(1)
    last = pl.num_programs(0) * nl - 1
    parity = lax.rem(step, 2)
    grp = CMB_GROUP

    def request(d_tab, g, s):
        for j in range(grp):
            m = g * grp + j
            for kk in range(TOP_K):
                pltpu.make_async_copy(_pslab(ob_hbm, d_tab[0, kk, m]), _slab(gbuf.at[s], kk * tm + m),
                                      sem.at[s]).start(priority=kk % 2)

    def mix(g, s):
        r0 = pl.multiple_of(g * grp, grp)
        w = wcol[pl.ds(r0, grp), :]
        wk = [w[:, kk:kk + 1] for kk in range(TOP_K)]
        for c in range(PCH):
            acc_lo = acc_hi = None
            for kk in range(TOP_K):
                words = gbuf[s, pl.ds((kk * tm + r0) * PCH + c, grp, stride=PCH), :]
                lo, hi = _unpack_pair(words)
                acc_lo = wk[kk] * lo if acc_lo is None else acc_lo + wk[kk] * lo
                acc_hi = wk[kk] * hi if acc_hi is None else acc_hi + wk[kk] * hi
            rbuf[pl.ds(r0, grp), c * LANES:(c + 1) * LANES] = acc_lo
            rbuf[pl.ds(r0, grp), (c + PCH) * LANES:(c + PCH + 1) * LANES] = acc_hi

    @pl.when(step == 0)
    def _():
        def first(g, carry):
            request(d_ref, g, 0)
            return carry
        lax.fori_loop(0, tm // grp, first, 0)

    ri = lax.broadcasted_iota(I32, (tm, tm), 0)
    ci = lax.broadcasted_iota(I32, (tm, tm), 1)
    eye = jnp.where(ri == ci, 1.0, 0.0).astype(BF16)
    wc = None
    for piece in _split(wt_ref[0], 3):
        t = lax.dot_general(eye, piece, _DN["nt"], preferred_element_type=F32)
        wc = t if wc is None else wc + t
    wcol[...] = wc

    def run(slot):
        pltpu.make_async_copy(ob_hbm.at[pl.ds(0, tm * TOP_K * PCH)], gbuf.at[slot], sem.at[slot]).wait()

        @pl.when(step < last)
        def _():
            def both(g, carry):
                request(dn_ref, g, 1 - slot)
                mix(g, slot)
                return carry
            lax.fori_loop(0, tm // grp, both, 0)

        @pl.when(step == last)
        def _():
            def only(g, carry):
                mix(g, slot)
                return carry
            lax.fori_loop(0, tm // grp, only, 0)

    for slot in range(2):
        pl.when(parity == slot)(functools.partial(run, slot))

    routed = rbuf[...]
    h2 = _rows_from_packed(h2_ref, tm)
    hg = jnp.dot(h2, sg_ref[...], preferred_element_type=F32)
    hu = jnp.dot(h2, su_ref[...], preferred_element_type=F32)
    shared = jnp.dot((hg * _sigmoid(hg) * hu).astype(BF16), sd_ref[...], preferred_element_type=F32)
    ff = (routed + shared).reshape(bb, ll, d)
    x2 = x1_ref[...] + mod_ref[:, 5:6, :] * ff
    out_ref[...] = x2 * lax.rsqrt(jnp.mean(x2 * x2, axis=-1, keepdims=True) + NORM_EPS) * fg_ref[...]


def _combine_call(dest, wts, first_tok, ob, h2s, x1, mod, p):
    bn, seq, d = x1.shape
    tm = CMB_TILE
    bb, ll = _tile(bn, seq, tm)
    nl = seq // ll
    tn = bn * seq
    nsteps = tn // tm
    per = dest.shape[2] // tm
    assert first_tok % tm == 0 and dest.shape[2] % tm == 0
    tile = lambda g: ((first_tok // tm + g) // per, 0, (first_tok // tm + g) % per)
    smem = _assign_spec(tm, lambda b, l: tile(b * nl + l))
    smem_next = _assign_spec(tm, lambda b, l: tile(jnp.minimum(b * nl + l + 1, nsteps - 1)))
    wblk = pl.BlockSpec((1, TOP_K, tm), lambda b, l: tile(b * nl + l))
    tok = pl.BlockSpec((bb, ll, d), lambda b, l: (b, l, 0))
    full = lambda a: pl.BlockSpec(a.shape, lambda b, l: (0,) * a.ndim)
    consts = (p["sh_gate"], p["sh_up"], p["sh_down"], p["final_g"])
    return pl.pallas_call(
        functools.partial(_combine_body, tm=tm, nl=nl),
        grid=(bn // bb, nl),
        in_specs=[smem, smem_next, wblk, pl.BlockSpec(memory_space=pl.ANY),
                  pl.BlockSpec((tm * PCH, LANES), lambda b, l: (first_tok // tm + b * nl + l, 0)),
                  tok, pl.BlockSpec((bb, 6, d), lambda b, l: (b, 0, 0))] + [full(c) for c in consts],
        out_specs=tok,
        out_shape=jax.ShapeDtypeStruct((bn, seq, d), F32),
        scratch_shapes=[pltpu.VMEM((2, tm * TOP_K * PCH, LANES), I32), pltpu.VMEM((tm, d), F32),
                        pltpu.VMEM((tm, TOP_K), F32), pltpu.SemaphoreType.DMA((2,))],
        compiler_params=_cparams(("arbitrary", "arbitrary"), VMEM_LIMIT),
        name="moe_combine_final",
    )(dest, dest, wts, ob, h2s, x1, mod, *consts)


def _layer_params(l, ada_w, ada_b, norm1_g, norm2_g, w_in, mu_shift, rw_w0, rw_w_up, rw_a0, rw_a_up, rw_g_up,
                  rw_k_k, rw_k_a, rw_r_k, rw_gn_g, rw_gn_b, gla_a_up, gla_a_bias, gla_norm_g, w_pa, w_pb, w_out,
                  router_w, router_b, exp_gate, exp_up, exp_down, sh_gate, sh_up, sh_down):
    d = D_MODEL
    wi = w_in[l]
    gla0 = RW_SHIFT_COLS
    xal0 = gla0 + QKV_W
    pad = jnp.zeros((d, XAL_W - GLA_GATE_RANK), F32)
    w_pack = jnp.concatenate([wi[:, :xal0], wi[:, xal0:xal0 + GLA_GATE_RANK], pad,
                              wi[:, xal0 + GLA_GATE_RANK:]], axis=1).astype(BF16)
    zr = jnp.zeros((RW_W_RANK, RW_WIDTH), F32)
    hid = jnp.arange(RW_WIDTH) // RW_HEAD
    row = lambda a: a.reshape(1, -1)
    rw_t = router_w[l].T
    rw_hi = rw_t.astype(BF16)
    return dict(
        ada_w=ada_w[l], ada_b=ada_b[l], norm1_g=norm1_g[l].reshape(1, 1, d),
        norm2_g=norm2_g[l].reshape(1, 1, d), w_pack=w_pack,
        mu=mu_shift[l].reshape(1, 1, -1), w0=row(rw_w0[l]), wup=jnp.concatenate([rw_w_up[l], zr], axis=0),
        a0=row(rw_a0[l]), aup=jnp.concatenate([zr, rw_a_up[l]], axis=0), gup=rw_g_up[l].astype(BF16),
        kk=row(rw_k_k[l]), ka=row(rw_k_a[l]), rk=row(rw_r_k[l]),
        bd64=(hid[:, None] == hid[None, :]).astype(BF16),
        gn_g=row(rw_gn_g[l]), gn_b=row(rw_gn_b[l]),
        gla_aup=jnp.concatenate([gla_a_up[l], jnp.zeros((XAL_W - GLA_GATE_RANK, GLA_KW), F32)], axis=0),
        gla_ab=row(gla_a_bias[l]), gla_ng=row(gla_norm_g[l]),
        w_pa=w_pa[l].astype(BF16), w_pb=w_pb[l].astype(BF16), w_out=w_out[l].astype(BF16),
        rw_hi=rw_hi, rw_lo=(rw_t - rw_hi.astype(F32)).astype(BF16), router_b=router_b[l],
        exp_gate=exp_gate[l], exp_up=exp_up[l], exp_down=exp_down[l],
        sh_gate=sh_gate[l].astype(BF16), sh_up=sh_up[l].astype(BF16), sh_down=sh_down[l].astype(BF16),
    )


def _mixer_group(x, mod, s_rw, s_sh, s_gla, p, tn, first_tok, shared):
    qkv, xal, gg, mg, r, lw, k2, v, a_s, b_s, g, bonus, new_sh = _inproj_prep_call(x, mod, s_sh, p)
    y, rw_new = _rwscan_call(r, lw, k2, v, a_s, b_s, s_rw)
    o_b, gla_new = _gla_call(qkv, xal, gg, s_gla, p)
    x1, h2s, logits = _merge_call(y, g, bonus, o_b, mg, x, mod, p, tn, first_tok, shared)
    states = (rw_new, new_sh[:, 0, :], gla_new)
    return x1, h2s, logits, states


def _moe(h2s, logits, p):
    tn = h2s.shape[0] // PCH
    eidx, rank, wts, counts = _route_call(logits, p["router_b"])
    counts = counts[:, 0].astype(I32)
    padded = (counts + MOE_BLK - 1) // MOE_BLK * MOE_BLK
    pad_end = jnp.cumsum(padded)
    pad_start = (pad_end - padded).astype(I32)
    nb = (tn * TOP_K + N_EXPERTS * (MOE_BLK - 1)) // MOE_BLK + 1
    tables = _expert_tables(counts, pad_start, pad_end, nb)
    dest = _dest_call(eidx, rank, pad_start)
    xs = _dispatch_call(dest, h2s, nb * MOE_BLK)
    ob = _expert_call(tables, xs, p["exp_gate"], p["exp_up"], p["exp_down"])
    return ob, dest, wts


def kernel(x_prompt, x_sample, c_prompt, c_sample, state_rwkv, state_shift, state_gla, ada_w, ada_b, norm1_g,
           norm2_g, w_in, mu_shift, rw_w0, rw_w_up, rw_a0, rw_a_up, rw_g_up, rw_k_k, rw_k_a, rw_r_k, rw_gn_g,
           rw_gn_b, gla_a_up, gla_a_bias, gla_norm_g, w_pa, w_pb, w_out, router_w, router_b, exp_gate, exp_up,
           exp_down, sh_gate, sh_up, sh_down, final_g):
    depth = ada_w.shape[0]
    bp, bs = x_prompt.shape[0], x_sample.shape[0]
    tp = bp * x_prompt.shape[1]
    tn = tp + bs * x_sample.shape[1]
    xs_g = [x_prompt, x_sample]
    c_all = jnp.concatenate([c_prompt, c_sample], axis=0)
    zeros = lambda shape: jnp.zeros(shape, x_prompt.dtype)
    new_states = [[], []]
    fg = final_g.reshape(1, 1, D_MODEL)
    for l in range(depth):
        p = _layer_params(l, ada_w, ada_b, norm1_g, norm2_g, w_in, mu_shift, rw_w0, rw_w_up, rw_a0, rw_a_up,
                          rw_g_up, rw_k_k, rw_k_a, rw_r_k, rw_gn_g, rw_gn_b, gla_a_up, gla_a_bias, gla_norm_g,
                          w_pa, w_pb, w_out, router_w, router_b, exp_gate, exp_up, exp_down, sh_gate, sh_up,
                          sh_down)
        p["final_g"] = fg
        mod_all = _mod_call(c_all, p["ada_w"], p["ada_b"])
        mods = [mod_all[:bp], mod_all[bp:]]
        states_in = [
            (zeros((bp, RW_HEADS, RW_HEAD, RW_HEAD)), zeros((bp, RW_SHIFT_COLS)),
             zeros((bp, GLA_HEADS, GLA_DK, GLA_DV))),
            (state_rwkv[l], state_shift[l], state_gla[l]),
        ]
        x1s, shared = [], None
        firsts = [0, tp]
        for gi in range(2):
            x1, h2_all, lg_all, st = _mixer_group(xs_g[gi], mods[gi], *states_in[gi], p, tn, firsts[gi], shared)
            shared = (h2_all, lg_all)
            x1s.append(x1)
            new_states[gi].append(st)
        ob, dest, wts = _moe(*shared, p)
        assert depth == 1, "the fused final norm assumes a single layer"
        xs_g = [_combine_call(dest, wts, firsts[gi], ob, shared[0], x1s[gi], mods[gi], p) for gi in range(2)]
    stack = lambda gi, j: new_states[gi][0][j][None] if depth == 1 else jnp.stack([s[j] for s in new_states[gi]])
    return (xs_g[0], xs_g[1], stack(0, 0), stack(0, 1), stack(0, 2), stack(1, 0), stack(1, 1), stack(1, 2))
```

```python
import functools

import jax
import jax.numpy as jnp
from jax import lax
from jax.experimental import pallas as pl
from jax.experimental.pallas import tpu as pltpu

F32, BF16, I32 = jnp.float32, jnp.bfloat16, jnp.int32

D_MODEL = 1024
RW_HEADS, RW_HEAD = 8, 64
RW_WIDTH = RW_HEADS * RW_HEAD
RW_W_RANK, RW_A_RANK, RW_G_RANK = 64, 64, 128
RW_GN_EPS = 64e-5
GLA_HEADS, GLA_DK, GLA_DV = 4, 64, 128
GLA_KW, GLA_VW = GLA_HEADS * GLA_DK, GLA_HEADS * GLA_DV
GLA_GATE_RANK = 16
GLA_GATE_TAU = 16.0
GLA_CHUNK = 16
RW_SHIFT_COLS = 3 * RW_WIDTH + RW_W_RANK + RW_A_RANK + RW_G_RANK
N_EXPERTS, TOP_K, N_GROUPS, TOPK_GROUPS = 256, 8, 8, 4
GROUP_SIZE = N_EXPERTS // N_GROUPS
EXPERT_FF = 256
ROUTED_SCALE = 2.5
NORM_EPS = 1e-6

LANES = 128
SUBLANES = 8
CHUNKS = D_MODEL // LANES
PCH = CHUNKS // 2
UNIT = 64
RW_SCAN_PASSES = (1, 1, 1, 1, 1)
GLA_UNITS_PER_STEP = 4
RW_UNITS_PER_STEP = 4
VMEM_LIMIT = 56 * 1024 * 1024

PA_W, QKV_W, XAL_W, GG_W, MG_W = RW_SHIFT_COLS, 2 * GLA_KW + GLA_VW, LANES, GLA_VW, 2 * D_MODEL
PACK_OFFS = (0, PA_W, PA_W + QKV_W, PA_W + QKV_W + XAL_W, PA_W + QKV_W + XAL_W + GG_W)
PACK_W = PA_W + QKV_W + XAL_W + GG_W + MG_W

TOK_TILE = 256
MOE_BLK = 512
EXPERT_PARTS = 1
WEIGHT_SLOTS = 3
ROW_SLOTS = 3
CMB_TILE = 256
CMB_GROUP = 16

_DN = {
    "nn": (((1,), (0,)), ((), ())),
    "nt": (((1,), (1,)), ((), ())),
    "tn": (((0,), (0,)), ((), ())),
}


def _split(x, pieces):
    out, rem = [], x
    for i in range(pieces):
        p = rem.astype(BF16)
        out.append(p)
        if i + 1 < pieces:
            rem = rem - p.astype(F32)
    return out


def _mm(a, b, form="nn", passes=1):
    dn = _DN[form]
    if passes == 6:
        return lax.dot_general(a.astype(F32), b.astype(F32), dn, precision=lax.Precision.HIGHEST,
                               preferred_element_type=F32)
    if passes == 1:
        return lax.dot_general(a.astype(BF16), b.astype(BF16), dn, preferred_element_type=F32)
    ah, al = _split(a, 2)
    bh, bl = _split(b, 2)
    out = lax.dot_general(ah, bh, dn, preferred_element_type=F32)
    out = out + lax.dot_general(ah, bl, dn, preferred_element_type=F32)
    return out + lax.dot_general(al, bh, dn, preferred_element_type=F32)


def _mm01(m01, x, pieces=3):
    m = m01.astype(BF16)
    out = None
    for p in _split(x, pieces):
        t = lax.dot_general(m, p, _DN["nn"], preferred_element_type=F32)
        out = t if out is None else out + t
    return out


def _xmm01(x, m01, pieces=2):
    m = m01.astype(BF16)
    out = None
    for p in _split(x, pieces):
        t = lax.dot_general(p, m, _DN["nn"], preferred_element_type=F32)
        out = t if out is None else out + t
    return out


HI16 = -65536


def _bf16_bits(x):
    return lax.bitcast_convert_type(x.astype(BF16).astype(F32), I32)


def _unpack_pair(w):
    return lax.bitcast_convert_type(w << 16, F32), lax.bitcast_convert_type(w & HI16, F32)


def _rows_to_packed(ref, x, first=0):
    for c in range(PCH):
        lo = _bf16_bits(x[:, c * LANES:(c + 1) * LANES])
        hi = _bf16_bits(x[:, (c + PCH) * LANES:(c + PCH + 1) * LANES])
        ref[pl.ds(first * PCH + c, x.shape[0], stride=PCH), :] = ((lo >> 16) & 0xFFFF) | (hi & HI16)


def _rows_from_packed(ref, n, live=None, first=0):
    lows, highs = [], []
    for c in range(PCH):
        w = ref[pl.ds(first * PCH + c, n, stride=PCH), :]
        if live is not None:
            w = jnp.where(live, w, 0)
        lo, hi = _unpack_pair(w)
        lows.append(lo.astype(BF16))
        highs.append(hi.astype(BF16))
    return jnp.concatenate(lows + highs, axis=1)


def _slab(ref, row):
    return ref.at[pl.ds(pl.multiple_of(row * PCH, PCH), PCH)]


def _fslab(ref, row):
    return ref.at[pl.ds(pl.multiple_of(row * CHUNKS, CHUNKS), CHUNKS)]


def _sigmoid(x):
    return 1.0 / (1.0 + jnp.exp(-x))


def _softplus(x):
    return jnp.maximum(x, 0.0) + jnp.log(1.0 + jnp.exp(-jnp.abs(x)))


def _log2(n):
    assert n > 0 and n & (n - 1) == 0, n
    return n.bit_length() - 1


def _cparams(sem, vmem=None):
    return pltpu.CompilerParams(dimension_semantics=sem, vmem_limit_bytes=vmem)


def _mod_body(c_ref, w_ref, b_ref, o_ref):
    c = c_ref[...]
    o_ref[0] = _mm(c * _sigmoid(c), w_ref[...], passes=3) + b_ref[...]


def _mod_call(c_all, ada_w, ada_b):
    bt, d = c_all.shape
    out = pl.pallas_call(
        _mod_body,
        grid=(6,),
        in_specs=[pl.BlockSpec((bt, d), lambda k: (0, 0)),
                  pl.BlockSpec((d, d), lambda k: (0, k)),
                  pl.BlockSpec((1, d), lambda k: (0, k))],
        out_specs=pl.BlockSpec((1, bt, d), lambda k: (k, 0, 0)),
        out_shape=jax.ShapeDtypeStruct((6, bt, d), F32),
        compiler_params=_cparams(("arbitrary",)),
        name="adaln_mod",
    )(c_all, ada_w, ada_b.reshape(1, 6 * d))
    return jnp.transpose(out, (1, 0, 2))


def _inproj_body(x_ref, mod_ref, g_ref, w_ref, pa_ref, qkv_ref, xal_ref, gg_ref, mg_ref):
    bb, ll, d = x_ref.shape
    x = x_ref[...]
    y = x * lax.rsqrt(jnp.mean(x * x, axis=-1, keepdims=True) + NORM_EPS) * g_ref[...]
    h = y * (1.0 + mod_ref[:, 1:2, :]) + mod_ref[:, 0:1, :]
    hb = h.reshape(bb * ll, d).astype(BF16)
    for ref, off in zip((pa_ref, qkv_ref, xal_ref, gg_ref, mg_ref), PACK_OFFS):
        w = ref.shape[-1]
        ref[...] = jnp.dot(hb, w_ref[:, off:off + w], preferred_element_type=F32).reshape(bb, ll, w)


def _tile(bn, seq, tile):
    if seq >= tile:
        assert seq % tile == 0
        return 1, tile
    assert tile % seq == 0 and bn % (tile // seq) == 0
    return tile // seq, seq


def _rwprep_body(pa_ref, sh_ref, mu_ref, w0_ref, wup_ref, a0_ref, aup_ref, gup_ref, kk_ref, ka_ref, rk_ref,
                 bd_ref, r_o, lw_o, k_o, v_o, a_o, b_o, g_o, bon_o, nsh_o, carry):
    bb, ll, wd = pa_ref.shape
    n = bb * ll
    hw = RW_WIDTH

    @pl.when(pl.program_id(1) == 0)
    def _():
        carry[...] = sh_ref[...]

    pa = pa_ref[...]
    rolled = pltpu.roll(pa.reshape(n, wd), 1, 0).reshape(bb, ll, wd)
    tok = lax.broadcasted_iota(I32, (bb, ll, wd), 1)
    prev = jnp.where(tok == 0, carry[...], rolled)
    last = pa_ref[:, ll - 1:ll, :]
    carry[...] = last
    nsh_o[...] = last
    xs = (pa + (prev - pa) * mu_ref[...]).reshape(n, wd)

    r, k, v = xs[:, 0:hw], xs[:, hw:2 * hw], xs[:, 2 * hw:3 * hw]
    xwa = xs[:, 3 * hw:3 * hw + LANES]
    xg = xs[:, 3 * hw + LANES:]
    w_log = -_softplus(-(w0_ref[...] + _mm(jnp.tanh(xwa), wup_ref[...], passes=3))) - 0.5
    lw = -jnp.exp(w_log)
    a = _sigmoid(a0_ref[...] + _mm(xwa, aup_ref[...], passes=3))
    g = _mm(_sigmoid(xg), gup_ref[...])
    bd = bd_ref[...]
    kkv = k * kk_ref[...]
    kkn = kkv * lax.rsqrt(jnp.maximum(_xmm01(kkv * kkv, bd, pieces=1), 1e-24))
    k2 = k * (1.0 + (a - 1.0) * ka_ref[...])
    bonus = _xmm01(r * k2 * rk_ref[...], bd, pieces=1) * v
    for ref, val in ((r_o, r), (lw_o, lw), (k_o, k2), (v_o, v), (a_o, -kkn), (b_o, kkn * a), (g_o, g),
                     (bon_o, bonus)):
        ref[...] = val.reshape(bb, ll, hw)


def _inproj_prep_body(x_ref, mod_ref, g_ref, w_ref, sh_ref, mu_ref, w0_ref, wup_ref, a0_ref, aup_ref, gup_ref, kk_ref,
                      ka_ref, rk_ref, bd_ref, qkv_o, xal_o, gg_o, mg_o, r_o, lw_o, k_o, v_o, a_o, b_o, g_o, bon_o,
                      nsh_o, pa_s, carry):
    _inproj_body(x_ref, mod_ref, g_ref, w_ref, pa_s, qkv_o, xal_o, gg_o, mg_o)
    _rwprep_body(pa_s, sh_ref, mu_ref, w0_ref, wup_ref, a0_ref, aup_ref, gup_ref, kk_ref, ka_ref, rk_ref, bd_ref,
                 r_o, lw_o, k_o, v_o, a_o, b_o, g_o, bon_o, nsh_o, carry)


def _inproj_prep_call(x, mod, s_sh, p):
    bn, seq, d = x.shape
    bb, ll = _tile(bn, seq, TOK_TILE)
    hw, wd = RW_WIDTH, PA_W
    tok = lambda w: pl.BlockSpec((bb, ll, w), lambda b, l: (b, l, 0))
    row = lambda w: pl.BlockSpec((bb, 1, w), lambda b, l: (b, 0, 0))
    full = lambda a: pl.BlockSpec(a.shape, lambda b, l: (0,) * a.ndim)
    consts = (p["mu"], p["w0"], p["wup"], p["a0"], p["aup"], p["gup"], p["kk"], p["ka"], p["rk"], p["bd64"])
    proj_w = (QKV_W, XAL_W, GG_W, MG_W)
    shapes = lambda ws: [jax.ShapeDtypeStruct((bn, seq, w), F32) for w in ws]
    return pl.pallas_call(
        _inproj_prep_body,
        grid=(bn // bb, seq // ll),
        in_specs=[tok(d), pl.BlockSpec((bb, 6, d), lambda b, l: (b, 0, 0)), full(p["norm1_g"]), full(p["w_pack"]),
                  row(wd)] + [full(c) for c in consts],
        out_specs=[tok(w) for w in proj_w] + [tok(hw)] * 8 + [row(wd)],
        out_shape=shapes(proj_w) + shapes((hw,) * 8) + [jax.ShapeDtypeStruct((bn, 1, wd), F32)],
        scratch_shapes=[pltpu.VMEM((bb, ll, wd), F32), pltpu.VMEM((bb, 1, wd), F32)],
        compiler_params=_cparams(("arbitrary", "arbitrary"), VMEM_LIMIT),
        name="norm_inproj_prep",
    )(x, mod, p["norm1_g"], p["w_pack"], s_sh.reshape(bn, 1, wd), *consts)


def _unit_masks(n, tl):
    ri = lax.broadcasted_iota(I32, (n, n), 0)
    ci = lax.broadcasted_iota(I32, (n, n), 1)
    same = (ri >> _log2(tl)) == (ci >> _log2(tl))
    return same, same & (ri > ci), same & (ri >= ci)


def _rwscan_body(r_ref, lw_ref, k_ref, v_ref, a_ref, b_ref, s0_ref, y_ref, sn_ref, st, *, nu, nseq, tl, passes):
    n = nseq * tl
    n2 = 2 * n
    p_aa, p_inv, p_apply, p_state, p_y = passes

    hd = RW_HEAD

    @pl.when(pl.program_id(1) == 0)
    def _():
        zero = jnp.zeros((hd, hd), F32)
        for q in range(nu * nseq):
            for p in range(RW_HEADS // 2):
                st[q, p] = jnp.concatenate(
                    [jnp.concatenate([s0_ref[q, 2 * p], zero], axis=1),
                     jnp.concatenate([zero, s0_ref[q, 2 * p + 1]], axis=1)], axis=0)

    same, _, incl = _unit_masks(n, tl)
    m_cum = jnp.where(incl, 1.0, 0.0)
    m_seq = jnp.where(same, 1.0, 0.0)
    ri = lax.broadcasted_iota(I32, (n2, n2), 0)
    ci = lax.broadcasted_iota(I32, (n2, n2), 1)
    rt, ct = ri & (n - 1), ci & (n - 1)
    dsame = ((rt >> _log2(tl)) == (ct >> _log2(tl))) & ((ri >> _log2(n)) == (ci >> _log2(n)))
    strict_d = dsame & (rt > ct)
    incl_d = dsame & (rt >= ct)
    eye_d = jnp.where(ri == ci, 1.0, 0.0)
    lane = lax.broadcasted_iota(I32, (1, LANES), 1)
    m0 = jnp.where(lane < RW_HEAD, 1.0, 0.0)
    m1 = 1.0 - m0

    def dup(x):
        return jnp.concatenate([x * m0, x * m1], axis=0)

    def seq_rows(x, q):
        if nseq == 1:
            return x
        return jnp.concatenate([x[q * tl:(q + 1) * tl], x[n + q * tl:n + (q + 1) * tl]], axis=0)

    def unit_rows(parts):
        if nseq == 1:
            return parts[0]
        return jnp.concatenate([p[0:tl] for p in parts] + [p[tl:2 * tl] for p in parts], axis=0)

    chains = [(u, p) for u in range(nu) for p in range(RW_HEADS // 2)]
    ids = range(len(chains))
    cat0 = lambda *xs: jnp.concatenate(xs, axis=0)

    def ld(ref, c):
        u, p = chains[c]
        return ref[u * nseq:(u + 1) * nseq, :, p * LANES:(p + 1) * LANES].reshape(n, LANES)

    lw = [ld(lw_ref, c) for c in ids]
    cum = [_mm01(m_cum, x) for x in lw]
    tot = [_mm01(m_seq, x) for x in lw]
    e_c = [jnp.exp(x) for x in cum]
    e_n = [jnp.exp(-x) for x in cum]
    e_l = [jnp.exp(t - x) for t, x in zip(tot, cum)]
    at_d = [dup(ld(a_ref, c) * jnp.exp(cum[c] - lw[c])) for c in ids]
    rt_d = [dup(ld(r_ref, c) * e_c[c]) for c in ids]
    bt_d = [dup(ld(b_ref, c) * e_n[c]) for c in ids]
    kt_d = [dup(ld(k_ref, c) * e_n[c]) for c in ids]
    bh_d = [dup(ld(b_ref, c) * e_l[c]) for c in ids]
    kh_d = [dup(ld(k_ref, c) * e_l[c]) for c in ids]
    v_d = [dup(ld(v_ref, c)) for c in ids]
    aa = [_mm(cat0(at_d[c], rt_d[c]), cat0(bt_d[c], kt_d[c]), "nt", p_aa) for c in ids]
    a_ab = [jnp.where(strict_d, x[0:n2, 0:n2], 0.0) for x in aa]
    a_ak = [jnp.where(strict_d, x[0:n2, n2:], 0.0) for x in aa]
    a_rb = [jnp.where(incl_d, x[n2:, 0:n2], 0.0) for x in aa]
    a_rk = [jnp.where(incl_d, x[n2:, n2:], 0.0) for x in aa]
    zy = [_mm(cat0(a_ak[c], a_rk[c]), v_d[c], passes=p_apply) for c in ids]
    tinv = [eye_d + x for x in a_ab]
    nk = a_ab
    for _ in range(_log2(tl) - 1):
        nk = [_mm(x, x, passes=p_inv) for x in nk]
        tinv = [t + _mm(t, x, passes=p_inv) for t, x in zip(tinv, nk)]
    wu = [_mm(tinv[c], jnp.concatenate([at_d[c], zy[c][0:n2]], axis=1), passes=p_apply) for c in ids]
    seqs = range(nseq)
    srow = lambda c, q: (chains[c][0] * nseq + q, chains[c][1])
    s_old = [[st[srow(c, q)] for q in seqs] for c in ids]
    xs = [[_mm(cat0(seq_rows(wu[c][:, 0:LANES], q), seq_rows(rt_d[c], q)), s_old[c][q], "nt", p_state)
           for q in seqs] for c in ids]
    u_q = [[xs[c][q][0:2 * tl] + seq_rows(wu[c][:, LANES:], q) for q in seqs] for c in ids]
    for c in ids:
        for q in seqs:
            g_c = jnp.exp(tot[c][q * tl:q * tl + 1, :])
            st[srow(c, q)] = s_old[c][q] * g_c + _mm(cat0(u_q[c][q], seq_rows(v_d[c], q)),
                                                     cat0(seq_rows(bh_d[c], q), seq_rows(kh_d[c], q)), "tn", p_state)
    for c in ids:
        u, p = chains[c]
        y_d = (unit_rows([xs[c][q][2 * tl:] for q in seqs]) + _mm(a_rb[c], unit_rows(u_q[c]), passes=p_y)
               + zy[c][n2:])
        y_ref[u * nseq:(u + 1) * nseq, :, p * LANES:(p + 1) * LANES] = (y_d[0:n] + y_d[n:]).reshape(nseq, tl, LANES)

    @pl.when(pl.program_id(1) == pl.num_programs(1) - 1)
    def _():
        for q in range(nu * nseq):
            for p in range(RW_HEADS // 2):
                s = st[q, p]
                sn_ref[q, 2 * p] = s[0:hd, 0:hd]
                sn_ref[q, 2 * p + 1] = s[hd:, hd:]


def _unit_shape(bn, seq):
    if seq >= UNIT:
        assert seq % UNIT == 0
        return 1, UNIT
    assert UNIT % seq == 0 and bn % (UNIT // seq) == 0
    return UNIT // seq, seq


def _rwscan_call(r, lw, k2, v, a_s, b_s, s0, passes=RW_SCAN_PASSES):
    bn, seq, hw = r.shape
    nseq, tl = _unit_shape(bn, seq)
    nu = RW_UNITS_PER_STEP if bn % (RW_UNITS_PER_STEP * nseq) == 0 else 1
    rows = nu * nseq
    tok = pl.BlockSpec((rows, tl, hw), lambda b, c: (b, c, 0))
    stt = pl.BlockSpec((rows, RW_HEADS, RW_HEAD, RW_HEAD), lambda b, c: (b, 0, 0, 0))
    return pl.pallas_call(
        functools.partial(_rwscan_body, nu=nu, nseq=nseq, tl=tl, passes=passes),
        grid=(bn // rows, seq // tl),
        in_specs=[tok] * 6 + [stt],
        out_specs=[tok, stt],
        out_shape=[jax.ShapeDtypeStruct((bn, seq, hw), F32), jax.ShapeDtypeStruct(s0.shape, F32)],
        scratch_shapes=[pltpu.VMEM((rows, RW_HEADS // 2, LANES, LANES), F32)],
        compiler_params=_cparams(("arbitrary", "arbitrary"), VMEM_LIMIT),
        name="rwkv_scan",
    )(r, lw, k2, v, a_s, b_s, s0)


def _gla_body(qkv_ref, xal_ref, gate_ref, aup_ref, ab_ref, ng_ref, s0_ref, o_ref, sn_ref, st, *, nu, nseq, tl, cs):
    n = nseq * tl
    n2 = 2 * n
    nsub = tl // cs

    @pl.when(pl.program_id(1) == 0)
    def _():
        zero = jnp.zeros((GLA_DV, GLA_DK), F32)
        for q in range(nu * nseq):
            for p in range(GLA_HEADS // 2):
                st[q, p] = jnp.concatenate(
                    [jnp.concatenate([s0_ref[q, 2 * p].T, zero], axis=1),
                     jnp.concatenate([zero, s0_ref[q, 2 * p + 1].T], axis=1)], axis=0)

    same, _, incl = _unit_masks(n, cs)
    m_cum = jnp.where(incl, 1.0, 0.0)
    m_sub = jnp.where(same, 1.0, 0.0)
    ri = lax.broadcasted_iota(I32, (n2, n2), 0)
    ci = lax.broadcasted_iota(I32, (n2, n2), 1)
    rt, ct = ri & (n - 1), ci & (n - 1)
    causal_d = ((rt >> _log2(cs)) == (ct >> _log2(cs))) & ((ri >> _log2(n)) == (ci >> _log2(n))) & (rt >= ct)
    lane = lax.broadcasted_iota(I32, (1, LANES), 1)
    m0 = jnp.where(lane < GLA_DK, 1.0, 0.0)
    m1 = 1.0 - m0
    sr = lax.broadcasted_iota(I32, (2 * GLA_DV, LANES), 0)
    sc = lax.broadcasted_iota(I32, (2 * GLA_DV, LANES), 1)
    st_mask = jnp.where((sr >> _log2(GLA_DV)) == (sc >> _log2(GLA_DK)), 1.0, 0.0)

    def dup(x):
        return jnp.concatenate([x * m0, x * m1], axis=0)

    chains = [(u, p) for u in range(nu) for p in range(GLA_HEADS // 2)]
    ids = range(len(chains))
    urows = lambda u: slice(u * nseq, (u + 1) * nseq)
    ng = ng_ref[...]
    la_all = [-_softplus(-(_mm(xal_ref[urows(u), :, :].reshape(n, LANES), aup_ref[...], passes=3) + ab_ref[...]))
              * (1.0 / GLA_GATE_TAU) for u in range(nu)]

    def ld(ref, c, off, width):
        return ref[urows(chains[c][0]), :, off:off + width].reshape(n, width)

    q = [ld(qkv_ref, c, chains[c][1] * LANES, LANES) * (GLA_DK ** -0.5) for c in ids]
    k = [ld(qkv_ref, c, GLA_KW + chains[c][1] * LANES, LANES) for c in ids]
    vp = [ld(qkv_ref, c, 2 * GLA_KW + chains[c][1] * 2 * GLA_DV, 2 * GLA_DV) for c in ids]
    la = [la_all[u][:, p * LANES:(p + 1) * LANES] for u, p in chains]
    bc = [_mm01(m_cum, x) for x in la]
    bl = [_mm01(m_sub, x) for x in la]
    qe = [q[c] * jnp.exp(bc[c]) for c in ids]
    ke = [k[c] * jnp.exp(-bc[c]) for c in ids]
    kd = [k[c] * jnp.exp(bl[c] - bc[c]) for c in ids]
    att = [jnp.where(causal_d, _mm(dup(qe[c]), dup(ke[c]), "nt", passes=1), 0.0) for c in ids]
    v_st = [jnp.concatenate([x[:, 0:GLA_DV], x[:, GLA_DV:]], axis=0) for x in vp]
    o_st = [_mm(att[c], v_st[c], passes=1) for c in ids]
    upd = [[_mm(vp[c][r0:r0 + cs], kd[c][r0:r0 + cs], "tn", passes=1) for r0 in range(0, n, cs)] for c in ids]
    inter = [[None] * (n // cs) for _ in ids]
    for sq in range(nseq):
        s = [st[chains[c][0] * nseq + sq, chains[c][1]] for c in ids]
        for j in range(nsub):
            i = sq * nsub + j
            r0 = i * cs
            for c in ids:
                inter[c][i] = _mm(qe[c][r0:r0 + cs], s[c], "nt", passes=1)
                s[c] = s[c] * jnp.exp(bl[c][r0:r0 + 1, :]) + st_mask * upd[c][i]
        for c in ids:
            st[chains[c][0] * nseq + sq, chains[c][1]] = s[c]
    for c in ids:
        u, p = chains[c]
        o = o_st[c] + jnp.concatenate([x[:, 0:GLA_DV] for x in inter[c]] + [x[:, GLA_DV:] for x in inter[c]], axis=0)
        o = o * lax.rsqrt(jnp.mean(o * o, axis=-1, keepdims=True) + NORM_EPS) * ng
        goff = p * 2 * GLA_DV
        gp = ld(gate_ref, c, goff, 2 * GLA_DV)
        g_st = jnp.concatenate([gp[:, 0:GLA_DV], gp[:, GLA_DV:]], axis=0)
        ob = o * (g_st * _sigmoid(g_st))
        o_ref[urows(u), :, goff:goff + GLA_DV] = ob[0:n].reshape(nseq, tl, GLA_DV)
        o_ref[urows(u), :, goff + GLA_DV:goff + 2 * GLA_DV] = ob[n:].reshape(nseq, tl, GLA_DV)

    @pl.when(pl.program_id(1) == pl.num_programs(1) - 1)
    def _():
        for q in range(nu * nseq):
            for p in range(GLA_HEADS // 2):
                s = st[q, p]
                sn_ref[q, 2 * p] = s[0:GLA_DV, 0:GLA_DK].T
                sn_ref[q, 2 * p + 1] = s[GLA_DV:, GLA_DK:].T


def _gla_call(qkv, xal, gate, s0, p):
    bn, seq, _ = qkv.shape
    nseq, tl = _unit_shape(bn, seq)
    cs = min(GLA_CHUNK, seq)
    assert tl % cs == 0
    nu = GLA_UNITS_PER_STEP if bn % (GLA_UNITS_PER_STEP * nseq) == 0 else 1
    rows = nu * nseq
    tok = lambda w: pl.BlockSpec((rows, tl, w), lambda b, c: (b, c, 0))
    full = lambda a: pl.BlockSpec(a.shape, lambda b, c: (0,) * a.ndim)
    stt = pl.BlockSpec((rows, GLA_HEADS, GLA_DK, GLA_DV), lambda b, c: (b, 0, 0, 0))
    consts = (p["gla_aup"], p["gla_ab"], p["gla_ng"])
    return pl.pallas_call(
        functools.partial(_gla_body, nu=nu, nseq=nseq, tl=tl, cs=cs),
        grid=(bn // rows, seq // tl),
        in_specs=[tok(QKV_W), tok(XAL_W), tok(GG_W)] + [full(c) for c in consts] + [stt],
        out_specs=[tok(GLA_VW), stt],
        out_shape=[jax.ShapeDtypeStruct((bn, seq, GLA_VW), F32), jax.ShapeDtypeStruct(s0.shape, F32)],
        scratch_shapes=[pltpu.VMEM((rows, GLA_HEADS // 2, 2 * GLA_DV, LANES), F32)],
        compiler_params=_cparams(("arbitrary", "arbitrary"), VMEM_LIMIT),
        name="gla_chunked",
    )(qkv, xal, gate, *consts, s0)


def _merge_body(y_ref, g_ref, bon_ref, ob_ref, mg_ref, x_ref, mod_ref, gng_ref, gnb_ref, bd_ref, wpa_ref,
                wpb_ref, wout_ref, n2_ref, rwh_ref, rwl_ref, *rest):
    x1_o, h2_o, lg_o = rest[-3:]
    bb, ll, d = x_ref.shape
    n = bb * ll
    hw = RW_WIDTH
    bd = bd_ref[...]
    y = y_ref[...].reshape(n, hw)
    mu = _xmm01(y, bd, pieces=2) * (1.0 / RW_HEAD)
    dv = y - mu
    var = _xmm01(dv * dv, bd, pieces=1) * (1.0 / RW_HEAD)
    yn = dv * lax.rsqrt(var + RW_GN_EPS) * gng_ref[...] + gnb_ref[...]
    o_a = (yn + bon_ref[...].reshape(n, hw)) * g_ref[...].reshape(n, hw)
    o_b = ob_ref[...].reshape(n, GLA_VW)
    mg = mg_ref[...].reshape(n, 2 * d)
    merged = _sigmoid(mg[:, 0:d]) * _mm(o_a, wpa_ref[...]) + _sigmoid(mg[:, d:]) * _mm(o_b, wpb_ref[...])
    mix = _mm(merged, wout_ref[...]).reshape(bb, ll, d)
    x1 = x_ref[...] + mod_ref[:, 2:3, :] * mix
    x1_o[...] = x1
    yn2 = x1 * lax.rsqrt(jnp.mean(x1 * x1, axis=-1, keepdims=True) + NORM_EPS) * n2_ref[...]
    h2 = (yn2 * (1.0 + mod_ref[:, 4:5, :]) + mod_ref[:, 3:4, :]).reshape(n, d)
    hh, hl = _split(h2, 2)
    rwh, rwl = rwh_ref[...], rwl_ref[...]
    nt = lambda a, b: lax.dot_general(a, b, _DN["nt"], preferred_element_type=F32)
    lg_o[...] = nt(rwh, hh) + nt(rwl, hh) + nt(rwh, hl)
    _rows_to_packed(h2_o, h2)


def _merge_call(y, g, bonus, o_b, mg, x, mod, p, tn, first_tok, shared):
    bn, seq, d = x.shape
    bb, ll = _tile(bn, seq, TOK_TILE)
    nl = seq // ll
    assert first_tok % (bb * ll) == 0
    t0 = first_tok // (bb * ll)
    n_in = 7 + 9
    extra = [] if shared is None else list(shared)
    alias = {} if shared is None else {n_in: 1, n_in + 1: 2}
    tok = lambda w: pl.BlockSpec((bb, ll, w), lambda b, l: (b, l, 0))
    full = lambda a: pl.BlockSpec(a.shape, lambda b, l: (0,) * a.ndim)
    consts = (p["gn_g"], p["gn_b"], p["bd64"], p["w_pa"], p["w_pb"], p["w_out"], p["norm2_g"], p["rw_hi"],
              p["rw_lo"])
    return pl.pallas_call(
        _merge_body,
        grid=(bn // bb, nl),
        in_specs=[tok(RW_WIDTH)] * 3 + [tok(GLA_VW), tok(MG_W), tok(d),
                                        pl.BlockSpec((bb, 6, d), lambda b, l: (b, 0, 0))] + [full(c) for c in consts]
        + [pl.BlockSpec(memory_space=pl.ANY)] * len(extra),
        out_specs=[tok(d),
                   pl.BlockSpec((bb * ll * PCH, LANES), lambda b, l: (t0 + b * nl + l, 0)),
                   pl.BlockSpec((N_EXPERTS, bb * ll), lambda b, l: (0, t0 + b * nl + l))],
        out_shape=[jax.ShapeDtypeStruct((bn, seq, d), F32),
                   jax.ShapeDtypeStruct((tn * PCH, LANES), I32),
                   jax.ShapeDtypeStruct((N_EXPERTS, tn), F32)],
        input_output_aliases=alias,
        compiler_params=_cparams(("arbitrary", "arbitrary"), VMEM_LIMIT),
        name="merge_outproj_router",
    )(y, g, bonus, o_b, mg, x, mod, *consts, *extra)


def _route_body(lg_ref, rb_ref, e_o, rk_o, w_o, cnt_o, carry):
    ne, tm = lg_ref.shape

    @pl.when(pl.program_id(0) == 0)
    def _():
        carry[...] = jnp.zeros_like(carry)

    neg = -jnp.inf
    scores = _sigmoid(lg_ref[...])
    sel = scores + rb_ref[...]
    row_i = lax.broadcasted_iota(I32, (ne, tm), 0)
    row = row_i.astype(F32)
    grp = (row_i >> _log2(GROUP_SIZE)).astype(F32)

    def first_max(x, ids, none):
        m = jnp.max(x, axis=0, keepdims=True)
        return m, jnp.min(jnp.where(x == m, ids, none), axis=0, keepdims=True)

    gs = []
    gids = lax.broadcasted_iota(I32, (GROUP_SIZE, tm), 0)
    for gidx in range(N_GROUPS):
        rows = slice(gidx * GROUP_SIZE, (gidx + 1) * GROUP_SIZE)
        sg = _sigmoid(lg_ref[rows, :]) + rb_ref[rows, :]
        ids = (gids + gidx * GROUP_SIZE).astype(F32)
        m1, i1 = first_max(sg, ids, float(ne))
        gs.append(m1 + jnp.max(jnp.where(ids == i1, neg, sg), axis=0, keepdims=True))
    gs = jnp.concatenate(gs, axis=0)
    gid = lax.broadcasted_iota(I32, (N_GROUPS, tm), 0).astype(F32)
    cur = jnp.full((ne, tm), neg, F32)
    for _ in range(TOPK_GROUPS):
        _, gi = first_max(gs, gid, float(N_GROUPS))
        cur = jnp.where(grp == gi, sel, cur)
        gs = jnp.where(gid == gi, neg, gs)

    pm = jnp.zeros((ne, tm), F32)
    eidx, wts = [], []
    for _ in range(TOP_K):
        _, ei = first_max(cur, row, float(ne))
        hit = row == ei
        pm = jnp.where(hit, 1.0, pm)
        eidx.append(ei)
        wts.append(jnp.sum(jnp.where(hit, scores, 0.0), axis=0, keepdims=True))
        cur = jnp.where(hit, neg, cur)
    wsum = wts[0]
    for w in wts[1:]:
        wsum = wsum + w

    ri = lax.broadcasted_iota(I32, (tm, tm), 0)
    ci = lax.broadcasted_iota(I32, (tm, tm), 1)
    earlier = jnp.where(ri < ci, 1.0, 0.0)
    rank = _mm(pm, earlier, passes=1) + carry[...]
    carry[...] = carry[...] + jnp.sum(pm, axis=1, keepdims=True)
    cnt_o[...] = carry[...]

    rks = [jnp.sum(jnp.where(row == e, rank, 0.0), axis=0, keepdims=True) for e in eidx]
    e_o[0] = jnp.concatenate(eidx, axis=0).astype(I32)
    rk_o[0] = jnp.concatenate(rks, axis=0).astype(I32)
    w_o[0] = jnp.concatenate([w / wsum * ROUTED_SCALE for w in wts], axis=0)


def _route_call(logits_t, router_b):
    ne, tn = logits_t.shape
    tm = TOK_TILE
    assert tn % tm == 0
    col = pl.BlockSpec((ne, 1), lambda i: (0, 0))
    tab = pl.BlockSpec((1, TOP_K, tm), lambda i: (i, 0, 0))
    tab_shape = (tn // tm, TOP_K, tm)
    return pl.pallas_call(
        _route_body,
        grid=(tn // tm,),
        in_specs=[pl.BlockSpec((ne, tm), lambda i: (0, i)), col],
        out_specs=[tab, tab, tab, col],
        out_shape=[jax.ShapeDtypeStruct(tab_shape, I32), jax.ShapeDtypeStruct(tab_shape, I32),
                   jax.ShapeDtypeStruct(tab_shape, F32), jax.ShapeDtypeStruct((ne, 1), F32)],
        scratch_shapes=[pltpu.VMEM((ne, 1), F32)],
        compiler_params=_cparams(("arbitrary",)),
        name="moe_route",
    )(logits_t, router_b.reshape(ne, 1))


def _dest_body(e_ref, rk_ref, ps_ref, d_o):
    ne, tm = ps_ref.shape[0], e_ref.shape[2]
    ids = lax.broadcasted_iota(I32, (ne, tm), 0)
    ps = ps_ref[...]
    for t in range(e_ref.shape[0]):
        first = [jnp.sum(jnp.where(ids == e_ref[t, kk:kk + 1, :], ps, 0.0), axis=0, keepdims=True)
                 for kk in range(TOP_K)]
        d_o[t] = (jnp.concatenate(first, axis=0).astype(I32) + rk_ref[t]) * PCH


def _dest_call(eidx, rank, pad_start):
    nt, _, tm = eidx.shape
    ne = pad_start.shape[0]
    per = next(k for k in (4, 2, 1) if nt % k == 0)
    tab = pl.BlockSpec((per, TOP_K, tm), lambda i: (i, 0, 0))
    return pl.pallas_call(
        _dest_body,
        grid=(nt // per,),
        in_specs=[tab, tab, pl.BlockSpec((ne, 1), lambda i: (0, 0))],
        out_specs=tab,
        out_shape=jax.ShapeDtypeStruct(eidx.shape, I32),
        compiler_params=_cparams(("arbitrary",)),
        name="moe_dest",
    )(eidx, rank, pad_start.astype(F32).reshape(ne, 1))


def _pslab(ref, offset):
    return ref.at[pl.ds(pl.multiple_of(offset, PCH), PCH)]


def _dispatch_body(d_ref, h2_ref, xs_hbm, sem, *, tm):
    def issue(m, carry):
        for kk in range(TOP_K):
            pltpu.make_async_copy(_slab(h2_ref, m), _pslab(xs_hbm, d_ref[0, kk, m]), sem).start(priority=kk % 2)
        return carry

    lax.fori_loop(0, tm, issue, 0)
    all_rows = xs_hbm.at[pl.ds(0, tm * TOP_K * PCH)]
    pltpu.make_async_copy(all_rows, all_rows, sem).wait()


def _assign_spec(tm, index_map):
    return pl.BlockSpec((1, TOP_K, tm), index_map, memory_space=pltpu.SMEM)


def _dispatch_call(dest, h2s, n_rows):
    tn = h2s.shape[0] // PCH
    tm = TOK_TILE
    assert dest.shape == (tn // tm, TOP_K, tm)
    blk = _assign_spec(tm, lambda i: (i, 0, 0))
    return pl.pallas_call(
        functools.partial(_dispatch_body, tm=tm),
        grid=(tn // tm,),
        in_specs=[blk, pl.BlockSpec((tm * PCH, LANES), lambda i: (i, 0))],
        out_specs=pl.BlockSpec(memory_space=pl.ANY),
        out_shape=jax.ShapeDtypeStruct((n_rows * PCH, LANES), I32),
        scratch_shapes=[pltpu.SemaphoreType.DMA],
        compiler_params=_cparams(("arbitrary",)),
        name="moe_dispatch",
    )(dest, h2s)


def _expert_body(bi_ref, nr_ref, ld_ref, nx_ref, xs_hbm, wg_hbm, wu_hbm, wd_hbm, ob_ref, wg_buf, wu_buf, wd_buf,
                 wg_bf, wu_bf, wd_bf, xbuf, sem, xsem):
    i = pl.program_id(0)
    nsteps = pl.num_programs(0)
    nr = nr_ref[i]
    slot = ld_ref[i]
    blk_rows = MOE_BLK * PCH

    def row_block(j):
        s = lax.rem(j, ROW_SLOTS)
        src = xs_hbm.at[pl.ds(pl.multiple_of(bi_ref[j] * blk_rows, blk_rows), blk_rows)]
        return pltpu.make_async_copy(src, xbuf.at[s], xsem.at[s])

    @pl.when(i == 0)
    def _():
        for j in range(ROW_SLOTS - 1):
            row_block(j).start()

    @pl.when(i + ROW_SLOTS - 1 < nsteps)
    def _():
        row_block(i + ROW_SLOTS - 1).start()

    def fetch(e, s):
        return (pltpu.make_async_copy(wg_hbm.at[e], wg_buf.at[s], sem.at[s]),
                pltpu.make_async_copy(wu_hbm.at[e], wu_buf.at[s], sem.at[s]),
                pltpu.make_async_copy(wd_hbm.at[e], wd_buf.at[s], sem.at[s]))

    @pl.when(i == 0)
    def _():
        for s in range(WEIGHT_SLOTS - 1):
            e0 = nx_ref[nx_ref.shape[0] - (WEIGHT_SLOTS - 1) + s]

            @pl.when(e0 >= 0)
            def _():
                for k, cp in enumerate(fetch(e0, s)):
                    cp.start(priority=k % 2)

    @pl.when(slot >= 0)
    def _():
        for cp in fetch(0, slot):
            cp.wait()

        @pl.when(nx_ref[i] >= 0)
        def _():
            for k, cp in enumerate(fetch(nx_ref[i], lax.rem(slot + WEIGHT_SLOTS - 1, WEIGHT_SLOTS))):
                cp.start(priority=k % 2)

        wg_bf[...] = wg_buf[slot].astype(BF16)
        wu_bf[...] = wu_buf[slot].astype(BF16)
        wd_bf[...] = wd_buf[slot].astype(BF16)

    row_block(i).wait()

    @pl.when(nr > 0)
    def _():
        part = MOE_BLK // EXPERT_PARTS
        firsts = [q * part for q in range(EXPERT_PARTS)]
        rid = lax.broadcasted_iota(I32, (part, LANES), 0)
        xs_ref = xbuf.at[lax.rem(i, ROW_SLOTS)]
        x = [_rows_from_packed(xs_ref, part, rid < nr - f, f) for f in firsts]
        hg = [jnp.dot(v, wg_bf[...], preferred_element_type=F32) for v in x]
        hu = [jnp.dot(v, wu_bf[...], preferred_element_type=F32) for v in x]
        hh = [(g * _sigmoid(g) * u).astype(BF16) for g, u in zip(hg, hu)]
        out = [jnp.dot(v, wd_bf[...], preferred_element_type=F32) for v in hh]
        for f, v in zip(firsts, out):
            _rows_to_packed(ob_ref, v, f)


def _expert_tables(counts, pad_start, pad_end, nb):
    ne = counts.shape[0]
    experts = jnp.arange(ne, dtype=I32)
    first_row = jnp.arange(nb, dtype=I32) * MOE_BLK
    block_e = jnp.minimum(jnp.sum(pad_end[None, :] <= first_row[:, None], axis=1), ne - 1).astype(I32)
    mine = block_e[:, None] == experts[None, :]
    pick = lambda v: jnp.sum(jnp.where(mine, v[None, :], 0), axis=1)
    has = counts > 0
    ordinal = jnp.cumsum(has.astype(I32)) - 1
    start_b, count_b, ord_b = pick(pad_start), pick(counts), pick(ordinal)
    block_rows = jnp.clip(start_b + count_b - first_row, 0, MOE_BLK).astype(I32)
    block_i = jnp.minimum(jnp.arange(nb, dtype=I32), pad_end[-1] // MOE_BLK - 1).astype(I32)
    starts = (first_row == start_b) & (block_rows > 0)
    load_slot = jnp.where(starts, ord_b % WEIGHT_SLOTS, -1).astype(I32)
    nth = lambda want: jnp.max(jnp.where(has[None, :] & (ordinal[None, :] == want[:, None]), experts[None, :], -1),
                               axis=1)
    ahead = jnp.where(starts, nth(ord_b + WEIGHT_SLOTS - 1), -1)
    lead = nth(jnp.arange(WEIGHT_SLOTS - 1, dtype=I32))
    return block_i, block_rows, load_slot, jnp.concatenate([ahead, lead]).astype(I32)


def _expert_call(tables, xs, wg, wu, wd):
    nb = xs.shape[0] // (MOE_BLK * PCH)
    assert nb >= ROW_SLOTS
    d, ff = wg.shape[1], wg.shape[2]
    rows = pl.BlockSpec((MOE_BLK * PCH, LANES), lambda i, bi, nr, ld, nx: (bi[i], 0))
    hbm = pl.BlockSpec(memory_space=pl.ANY)
    grid_spec = pltpu.PrefetchScalarGridSpec(
        num_scalar_prefetch=4,
        grid=(nb,),
        in_specs=[hbm, hbm, hbm, hbm],
        out_specs=rows,
        scratch_shapes=[pltpu.VMEM((WEIGHT_SLOTS, d, ff), F32), pltpu.VMEM((WEIGHT_SLOTS, d, ff), F32),
                        pltpu.VMEM((WEIGHT_SLOTS, ff, d), F32),
                        pltpu.VMEM((d, ff), BF16), pltpu.VMEM((d, ff), BF16), pltpu.VMEM((ff, d), BF16),
                        pltpu.VMEM((ROW_SLOTS, MOE_BLK * PCH, LANES), I32),
                        pltpu.SemaphoreType.DMA((WEIGHT_SLOTS,)), pltpu.SemaphoreType.DMA((ROW_SLOTS,))],
    )
    return pl.pallas_call(
        _expert_body,
        grid_spec=grid_spec,
        out_shape=jax.ShapeDtypeStruct(xs.shape, I32),
        compiler_params=_cparams(("arbitrary",), VMEM_LIMIT),
        name="moe_experts",
    )(*tables, xs, wg, wu, wd)


def _combine_body(d_ref, dn_ref, wt_ref, ob_hbm, h2_ref, x1_ref, mod_ref, sg_ref, su_ref,
                  sd_ref, fg_ref, out_ref, gbuf, rbuf, wcol, sem, *, tm, nl):
    bb, ll, d = x1_ref.shape
    step = pl.program_id(0) * nl + pl.program_id(1)
    last = pl.num_programs(0) * nl - 1
    parity = lax.rem(step, 2)
    grp = CMB_GROUP

    def request(d_tab, g, s):
        for j in range(grp):
            m = g * grp + j
            for kk in range(TOP_K):
                pltpu.make_async_copy(_pslab(ob_hbm, d_tab[0, kk, m]), _slab(gbuf.at[s], kk * tm + m),
                                      sem.at[s]).start(priority=kk % 2)

    def mix(g, s):
        r0 = pl.multiple_of(g * grp, grp)
        w = wcol[pl.ds(r0, grp), :]
        wk = [w[:, kk:kk + 1] for kk in range(TOP_K)]
        for c in range(PCH):
            acc_lo = acc_hi = None
            for kk in range(TOP_K):
                words = gbuf[s, pl.ds((kk * tm + r0) * PCH + c, grp, stride=PCH), :]
                lo, hi = _unpack_pair(words)
                acc_lo = wk[kk] * lo if acc_lo is None else acc_lo + wk[kk] * lo
                acc_hi = wk[kk] * hi if acc_hi is None else acc_hi + wk[kk] * hi
            rbuf[pl.ds(r0, grp), c * LANES:(c + 1) * LANES] = acc_lo
            rbuf[pl.ds(r0, grp), (c + PCH) * LANES:(c + PCH + 1) * LANES] = acc_hi

    @pl.when(step == 0)
    def _():
        def first(g, carry):
            request(d_ref, g, 0)
            return carry
        lax.fori_loop(0, tm // grp, first, 0)

    ri = lax.broadcasted_iota(I32, (tm, tm), 0)
    ci = lax.broadcasted_iota(I32, (tm, tm), 1)
    eye = jnp.where(ri == ci, 1.0, 0.0).astype(BF16)
    wc = None
    for piece in _split(wt_ref[0], 3):
        t = lax.dot_general(eye, piece, _DN["nt"], preferred_element_type=F32)
        wc = t if wc is None else wc + t
    wcol[...] = wc

    def run(slot):
        pltpu.make_async_copy(ob_hbm.at[pl.ds(0, tm * TOP_K * PCH)], gbuf.at[slot], sem.at[slot]).wait()

        @pl.when(step < last)
        def _():
            def both(g, carry):
                request(dn_ref, g, 1 - slot)
                mix(g, slot)
                return carry
            lax.fori_loop(0, tm // grp, both, 0)

        @pl.when(step == last)
        def _():
            def only(g, carry):
                mix(g, slot)
                return carry
            lax.fori_loop(0, tm // grp, only, 0)

    for slot in range(2):
        pl.when(parity == slot)(functools.partial(run, slot))

    routed = rbuf[...]
    h2 = _rows_from_packed(h2_ref, tm)
    hg = jnp.dot(h2, sg_ref[...], preferred_element_type=F32)
    hu = jnp.dot(h2, su_ref[...], preferred_element_type=F32)
    shared = jnp.dot((hg * _sigmoid(hg) * hu).astype(BF16), sd_ref[...], preferred_element_type=F32)
    ff = (routed + shared).reshape(bb, ll, d)
    x2 = x1_ref[...] + mod_ref[:, 5:6, :] * ff
    out_ref[...] = x2 * lax.rsqrt(jnp.mean(x2 * x2, axis=-1, keepdims=True) + NORM_EPS) * fg_ref[...]


def _combine_call(dest, wts, first_tok, ob, h2s, x1, mod, p):
    bn, seq, d = x1.shape
    tm = CMB_TILE
    bb, ll = _tile(bn, seq, tm)
    nl = seq // ll
    tn = bn * seq
    nsteps = tn // tm
    per = dest.shape[2] // tm
    assert first_tok % tm == 0 and dest.shape[2] % tm == 0
    tile = lambda g: ((first_tok // tm + g) // per, 0, (first_tok // tm + g) % per)
    smem = _assign_spec(tm, lambda b, l: tile(b * nl + l))
    smem_next = _assign_spec(tm, lambda b, l: tile(jnp.minimum(b * nl + l + 1, nsteps - 1)))
    wblk = pl.BlockSpec((1, TOP_K, tm), lambda b, l: tile(b * nl + l))
    tok = pl.BlockSpec((bb, ll, d), lambda b, l: (b, l, 0))
    full = lambda a: pl.BlockSpec(a.shape, lambda b, l: (0,) * a.ndim)
    consts = (p["sh_gate"], p["sh_up"], p["sh_down"], p["final_g"])
    return pl.pallas_call(
        functools.partial(_combine_body, tm=tm, nl=nl),
        grid=(bn // bb, nl),
        in_specs=[smem, smem_next, wblk, pl.BlockSpec(memory_space=pl.ANY),
                  pl.BlockSpec((tm * PCH, LANES), lambda b, l: (first_tok // tm + b * nl + l, 0)),
                  tok, pl.BlockSpec((bb, 6, d), lambda b, l: (b, 0, 0))] + [full(c) for c in consts],
        out_specs=tok,
        out_shape=jax.ShapeDtypeStruct((bn, seq, d), F32),
        scratch_shapes=[pltpu.VMEM((2, tm * TOP_K * PCH, LANES), I32), pltpu.VMEM((tm, d), F32),
                        pltpu.VMEM((tm, TOP_K), F32), pltpu.SemaphoreType.DMA((2,))],
        compiler_params=_cparams(("arbitrary", "arbitrary"), VMEM_LIMIT),
        name="moe_combine_final",
    )(dest, dest, wts, ob, h2s, x1, mod, *consts)


def _layer_params(l, ada_w, ada_b, norm1_g, norm2_g, w_in, mu_shift, rw_w0, rw_w_up, rw_a0, rw_a_up, rw_g_up,
                  rw_k_k, rw_k_a, rw_r_k, rw_gn_g, rw_gn_b, gla_a_up, gla_a_bias, gla_norm_g, w_pa, w_pb, w_out,
                  router_w, router_b, exp_gate, exp_up, exp_down, sh_gate, sh_up, sh_down):
    d = D_MODEL
    wi = w_in[l]
    gla0 = RW_SHIFT_COLS
    xal0 = gla0 + QKV_W
    pad = jnp.zeros((d, XAL_W - GLA_GATE_RANK), F32)
    w_pack = jnp.concatenate([wi[:, :xal0], wi[:, xal0:xal0 + GLA_GATE_RANK], pad,
                              wi[:, xal0 + GLA_GATE_RANK:]], axis=1).astype(BF16)
    zr = jnp.zeros((RW_W_RANK, RW_WIDTH), F32)
    hid = jnp.arange(RW_WIDTH) // RW_HEAD
    row = lambda a: a.reshape(1, -1)
    rw_t = router_w[l].T
    rw_hi = rw_t.astype(BF16)
    return dict(
        ada_w=ada_w[l], ada_b=ada_b[l], norm1_g=norm1_g[l].reshape(1, 1, d),
        norm2_g=norm2_g[l].reshape(1, 1, d), w_pack=w_pack,
        mu=mu_shift[l].reshape(1, 1, -1), w0=row(rw_w0[l]), wup=jnp.concatenate([rw_w_up[l], zr], axis=0),
        a0=row(rw_a0[l]), aup=jnp.concatenate([zr, rw_a_up[l]], axis=0), gup=rw_g_up[l].astype(BF16),
        kk=row(rw_k_k[l]), ka=row(rw_k_a[l]), rk=row(rw_r_k[l]),
        bd64=(hid[:, None] == hid[None, :]).astype(BF16),
        gn_g=row(rw_gn_g[l]), gn_b=row(rw_gn_b[l]),
        gla_aup=jnp.concatenate([gla_a_up[l], jnp.zeros((XAL_W - GLA_GATE_RANK, GLA_KW), F32)], axis=0),
        gla_ab=row(gla_a_bias[l]), gla_ng=row(gla_norm_g[l]),
        w_pa=w_pa[l].astype(BF16), w_pb=w_pb[l].astype(BF16), w_out=w_out[l].astype(BF16),
        rw_hi=rw_hi, rw_lo=(rw_t - rw_hi.astype(F32)).astype(BF16), router_b=router_b[l],
        exp_gate=exp_gate[l], exp_up=exp_up[l], exp_down=exp_down[l],
        sh_gate=sh_gate[l].astype(BF16), sh_up=sh_up[l].astype(BF16), sh_down=sh_down[l].astype(BF16),
    )


def _mixer_group(x, mod, s_rw, s_sh, s_gla, p, tn, first_tok, shared):
    qkv, xal, gg, mg, r, lw, k2, v, a_s, b_s, g, bonus, new_sh = _inproj_prep_call(x, mod, s_sh, p)
    y, rw_new = _rwscan_call(r, lw, k2, v, a_s, b_s, s_rw)
    o_b, gla_new = _gla_call(qkv, xal, gg, s_gla, p)
    x1, h2s, logits = _merge_call(y, g, bonus, o_b, mg, x, mod, p, tn, first_tok, shared)
    states = (rw_new, new_sh[:, 0, :], gla_new)
    return x1, h2s, logits, states


def _moe(h2s, logits, p):
    tn = h2s.shape[0] // PCH
    eidx, rank, wts, counts = _route_call(logits, p["router_b"])
    counts = counts[:, 0].astype(I32)
    padded = (counts + MOE_BLK - 1) // MOE_BLK * MOE_BLK
    pad_end = jnp.cumsum(padded)
    pad_start = (pad_end - padded).astype(I32)
    nb = (tn * TOP_K + N_EXPERTS * (MOE_BLK - 1)) // MOE_BLK + 1
    tables = _expert_tables(counts, pad_start, pad_end, nb)
    dest = _dest_call(eidx, rank, pad_start)
    xs = _dispatch_call(dest, h2s, nb * MOE_BLK)
    ob = _expert_call(tables, xs, p["exp_gate"], p["exp_up"], p["exp_down"])
    return ob, dest, wts


def kernel(x_prompt, x_sample, c_prompt, c_sample, state_rwkv, state_shift, state_gla, ada_w, ada_b, norm1_g,
           norm2_g, w_in, mu_shift, rw_w0, rw_w_up, rw_a0, rw_a_up, rw_g_up, rw_k_k, rw_k_a, rw_r_k, rw_gn_g,
           rw_gn_b, gla_a_up, gla_a_bias, gla_norm_g, w_pa, w_pb, w_out, router_w, router_b, exp_gate, exp_up,
           exp_down, sh_gate, sh_up, sh_down, final_g):
    depth = ada_w.shape[0]
    bp, bs = x_prompt.shape[0], x_sample.shape[0]
    tp = bp * x_prompt.shape[1]
    tn = tp + bs * x_sample.shape[1]
    xs_g = [x_prompt, x_sample]
    c_all = jnp.concatenate([c_prompt, c_sample], axis=0)
    zeros = lambda shape: jnp.zeros(shape, x_prompt.dtype)
    new_states = [[], []]
    fg = final_g.reshape(1, 1, D_MODEL)
    for l in range(depth):
        p = _layer_params(l, ada_w, ada_b, norm1_g, norm2_g, w_in, mu_shift, rw_w0, rw_w_up, rw_a0, rw_a_up,
                          rw_g_up, rw_k_k, rw_k_a, rw_r_k, rw_gn_g, rw_gn_b, gla_a_up, gla_a_bias, gla_norm_g,
                          w_pa, w_pb, w_out, router_w, router_b, exp_gate, exp_up, exp_down, sh_gate, sh_up,
                          sh_down)
        p["final_g"] = fg
        mod_all = _mod_call(c_all, p["ada_w"], p["ada_b"])
        mods = [mod_all[:bp], mod_all[bp:]]
        states_in = [
            (zeros((bp, RW_HEADS, RW_HEAD, RW_HEAD)), zeros((bp, RW_SHIFT_COLS)),
             zeros((bp, GLA_HEADS, GLA_DK, GLA_DV))),
            (state_rwkv[l], state_shift[l], state_gla[l]),
        ]
        x1s, shared = [], None
        firsts = [0, tp]
        for gi in range(2):
            x1, h2_all, lg_all, st = _mixer_group(xs_g[gi], mods[gi], *states_in[gi], p, tn, firsts[gi], shared)
            shared = (h2_all, lg_all)
            x1s.append(x1)
            new_states[gi].append(st)
        ob, dest, wts = _moe(*shared, p)
        assert depth == 1, "the fused final norm assumes a single layer"
        xs_g = [_combine_call(dest, wts, firsts[gi], ob, shared[0], x1s[gi], mods[gi], p) for gi in range(2)]
    stack = lambda gi, j: new_states[gi][0][j][None] if depth == 1 else jnp.stack([s[j] for s in new_states[gi]])
    return (xs_g[0], xs_g[1], stack(0, 0), stack(0, 1), stack(0, 2), stack(1, 0), stack(1, 1), stack(1, 2))
```

```python
import functools

import jax
import jax.numpy as jnp
from jax import lax
from jax.experimental import pallas as pl
from jax.experimental.pallas import tpu as pltpu

F32, BF16, I32 = jnp.float32, jnp.bfloat16, jnp.int32

D_MODEL = 1024
RW_HEADS, RW_HEAD = 8, 64
RW_WIDTH = RW_HEADS * RW_HEAD
RW_W_RANK, RW_A_RANK, RW_G_RANK = 64, 64, 128
RW_GN_EPS = 64e-5
GLA_HEADS, GLA_DK, GLA_DV = 4, 64, 128
GLA_KW, GLA_VW = GLA_HEADS * GLA_DK, GLA_HEADS * GLA_DV
GLA_GATE_RANK = 16
GLA_GATE_TAU = 16.0
GLA_CHUNK = 16
RW_SHIFT_COLS = 3 * RW_WIDTH + RW_W_RANK + RW_A_RANK + RW_G_RANK
N_EXPERTS, TOP_K, N_GROUPS, TOPK_GROUPS = 256, 8, 8, 4
GROUP_SIZE = N_EXPERTS // N_GROUPS
EXPERT_FF = 256
ROUTED_SCALE = 2.5
NORM_EPS = 1e-6

LANES = 128
CHUNKS = D_MODEL // LANES
PCH = CHUNKS // 2
UNIT = 64
RW_SCAN_PASSES = (1, 1, 1, 1, 1)
GLA_UNITS_PER_STEP = 4
RW_UNITS_PER_STEP = 4
VMEM_LIMIT = 56 * 1024 * 1024

PA_W, QKV_W, XAL_W, GG_W, MG_W = RW_SHIFT_COLS, 2 * GLA_KW + GLA_VW, LANES, GLA_VW, 2 * D_MODEL

TOK_TILE = 256
MOE_BLK = 512
EXPERT_PARTS = 1
WEIGHT_SLOTS = 3
ROW_SLOTS = 3
CMB_TILE = 256
CMB_GROUP = 16

_DN = {
    "nn": (((1,), (0,)), ((), ())),
    "nt": (((1,), (1,)), ((), ())),
    "tn": (((0,), (0,)), ((), ())),
}


def _split(x, pieces):
    out, rem = [], x
    for i in range(pieces):
        p = rem.astype(BF16)
        out.append(p)
        if i + 1 < pieces:
            rem = rem - p.astype(F32)
    return out


def _mm(a, b, form="nn", passes=1):
    dn = _DN[form]
    if passes == 6:
        return lax.dot_general(a.astype(F32), b.astype(F32), dn, precision=lax.Precision.HIGHEST,
                               preferred_element_type=F32)
    if passes == 1:
        return lax.dot_general(a.astype(BF16), b.astype(BF16), dn, preferred_element_type=F32)
    ah, al = _split(a, 2)
    bh, bl = _split(b, 2)
    out = lax.dot_general(ah, bh, dn, preferred_element_type=F32)
    out = out + lax.dot_general(ah, bl, dn, preferred_element_type=F32)
    return out + lax.dot_general(al, bh, dn, preferred_element_type=F32)


def _mm01(m01, x, pieces=3):
    m = m01.astype(BF16)
    out = None
    for p in _split(x, pieces):
        t = lax.dot_general(m, p, _DN["nn"], preferred_element_type=F32)
        out = t if out is None else out + t
    return out


def _xmm01(x, m01, pieces=2):
    m = m01.astype(BF16)
    out = None
    for p in _split(x, pieces):
        t = lax.dot_general(p, m, _DN["nn"], preferred_element_type=F32)
        out = t if out is None else out + t
    return out


HI16 = -65536


def _bf16_bits(x):
    return lax.bitcast_convert_type(x.astype(BF16).astype(F32), I32)


def _unpack_pair(w):
    return lax.bitcast_convert_type(w << 16, F32), lax.bitcast_convert_type(w & HI16, F32)


def _rows_to_packed(ref, x, first=0):
    for c in range(PCH):
        lo = _bf16_bits(x[:, c * LANES:(c + 1) * LANES])
        hi = _bf16_bits(x[:, (c + PCH) * LANES:(c + PCH + 1) * LANES])
        ref[pl.ds(first * PCH + c, x.shape[0], stride=PCH), :] = ((lo >> 16) & 0xFFFF) | (hi & HI16)


def _rows_from_packed(ref, n, live=None, first=0):
    lows, highs = [], []
    for c in range(PCH):
        w = ref[pl.ds(first * PCH + c, n, stride=PCH), :]
        if live is not None:
            w = jnp.where(live, w, 0)
        lo, hi = _unpack_pair(w)
        lows.append(lo.astype(BF16))
        highs.append(hi.astype(BF16))
    return jnp.concatenate(lows + highs, axis=1)


def _slab(ref, row):
    return ref.at[pl.ds(pl.multiple_of(row * PCH, PCH), PCH)]


def _sigmoid(x):
    return 1.0 / (1.0 + jnp.exp(-x))


def _softplus(x):
    return jnp.maximum(x, 0.0) + jnp.log(1.0 + jnp.exp(-jnp.abs(x)))


def _log2(n):
    assert n > 0 and n & (n - 1) == 0, n
    return n.bit_length() - 1


def _cparams(sem, vmem=None):
    return pltpu.CompilerParams(dimension_semantics=sem, vmem_limit_bytes=vmem)


def _mod_body(c_ref, w_ref, b_ref, o_ref):
    c = c_ref[...]
    o_ref[0] = _mm(c * _sigmoid(c), w_ref[...], passes=3) + b_ref[...]


def _mod_call(c_all, ada_w, ada_b):
    bt, d = c_all.shape
    out = pl.pallas_call(
        _mod_body,
        grid=(6,),
        in_specs=[pl.BlockSpec((bt, d), lambda k: (0, 0)),
                  pl.BlockSpec((d, d), lambda k: (0, k)),
                  pl.BlockSpec((1, d), lambda k: (0, k))],
        out_specs=pl.BlockSpec((1, bt, d), lambda k: (k, 0, 0)),
        out_shape=jax.ShapeDtypeStruct((6, bt, d), F32),
        compiler_params=_cparams(("arbitrary",)),
        name="adaln_mod",
    )(c_all, ada_w, ada_b.reshape(1, 6 * d))
    return jnp.transpose(out, (1, 0, 2))


def _inproj_body(x_ref, mod_ref, g_ref, wa_ref, wx_ref, wb_ref, pa_ref, qkv_ref, xal_ref, gg_ref, mg_ref):
    bb, ll, d = x_ref.shape
    x = x_ref[...]
    y = x * lax.rsqrt(jnp.mean(x * x, axis=-1, keepdims=True) + NORM_EPS) * g_ref[...]
    h = y * (1.0 + mod_ref[:, 1:2, :]) + mod_ref[:, 0:1, :]
    hb = h.reshape(bb * ll, d).astype(BF16)
    for w_ref, outs in ((wa_ref, (pa_ref, qkv_ref)), (wx_ref, (xal_ref,)), (wb_ref, (gg_ref, mg_ref))):
        off = 0
        for ref in outs:
            w = ref.shape[-1]
            ref[...] = jnp.dot(hb, w_ref[:, off:off + w], preferred_element_type=F32).reshape(bb, ll, w)
            off += w


def _tile(bn, seq, tile):
    if seq >= tile:
        assert seq % tile == 0
        return 1, tile
    assert tile % seq == 0 and bn % (tile // seq) == 0
    return tile // seq, seq


def _rwprep_body(pa_ref, sh_ref, mu_ref, w0_ref, wup_ref, a0_ref, aup_ref, gup_ref, kk_ref, ka_ref, rk_ref,
                 bd_ref, r_o, lw_o, k_o, v_o, a_o, b_o, g_o, bon_o, nsh_o, carry):
    bb, ll, wd = pa_ref.shape
    n = bb * ll
    hw = RW_WIDTH

    @pl.when(pl.program_id(1) == 0)
    def _():
        carry[...] = sh_ref[...]

    pa = pa_ref[...]
    rolled = pltpu.roll(pa.reshape(n, wd), 1, 0).reshape(bb, ll, wd)
    tok = lax.broadcasted_iota(I32, (bb, ll, wd), 1)
    prev = jnp.where(tok == 0, carry[...], rolled)
    last = pa_ref[:, ll - 1:ll, :]
    carry[...] = last
    nsh_o[...] = last
    xs = (pa + (prev - pa) * mu_ref[...]).reshape(n, wd)

    r, k, v = xs[:, 0:hw], xs[:, hw:2 * hw], xs[:, 2 * hw:3 * hw]
    xwa = xs[:, 3 * hw:3 * hw + LANES]
    xg = xs[:, 3 * hw + LANES:]
    w_log = -_softplus(-(w0_ref[...] + _mm(jnp.tanh(xwa), wup_ref[...], passes=3))) - 0.5
    lw = -jnp.exp(w_log)
    a = _sigmoid(a0_ref[...] + _mm(xwa, aup_ref[...], passes=3))
    g = _mm(_sigmoid(xg), gup_ref[...])
    bd = bd_ref[...]
    kkv = k * kk_ref[...]
    kkn = kkv * lax.rsqrt(jnp.maximum(_xmm01(kkv * kkv, bd, pieces=1), 1e-24))
    k2 = k * (1.0 + (a - 1.0) * ka_ref[...])
    bonus = _xmm01(r * k2 * rk_ref[...], bd, pieces=1) * v
    for ref, val in ((r_o, r), (lw_o, lw), (k_o, k2), (v_o, v), (a_o, -kkn), (b_o, kkn * a), (g_o, g),
                     (bon_o, bonus)):
        ref[...] = val.reshape(bb, ll, hw)


def _inproj_prep_body(x_ref, mod_ref, g_ref, wa_ref, wx_ref, wb_ref, sh_ref, mu_ref, w0_ref, wup_ref, a0_ref, aup_ref,
                      gup_ref, kk_ref, ka_ref, rk_ref, bd_ref, qkv_o, xal_o, gg_o, mg_o, r_o, lw_o, k_o, v_o, a_o,
                      b_o, g_o, bon_o, nsh_o, pa_s, carry):
    _inproj_body(x_ref, mod_ref, g_ref, wa_ref, wx_ref, wb_ref, pa_s, qkv_o, xal_o, gg_o, mg_o)
    _rwprep_body(pa_s, sh_ref, mu_ref, w0_ref, wup_ref, a0_ref, aup_ref, gup_ref, kk_ref, ka_ref, rk_ref, bd_ref,
                 r_o, lw_o, k_o, v_o, a_o, b_o, g_o, bon_o, nsh_o, carry)


def _inproj_prep_call(x, mod, s_sh, p):
    bn, seq, d = x.shape
    bb, ll = _tile(bn, seq, TOK_TILE)
    hw, wd = RW_WIDTH, PA_W
    tok = lambda w: pl.BlockSpec((bb, ll, w), lambda b, l: (b, l, 0))
    row = lambda w: pl.BlockSpec((bb, 1, w), lambda b, l: (b, 0, 0))
    full = lambda a: pl.BlockSpec(a.shape, lambda b, l: (0,) * a.ndim)
    consts = (p["mu"], p["w0"], p["wup"], p["a0"], p["aup"], p["gup"], p["kk"], p["ka"], p["rk"], p["bd64"])
    proj_w = (QKV_W, XAL_W, GG_W, MG_W)
    shapes = lambda ws: [jax.ShapeDtypeStruct((bn, seq, w), F32) for w in ws]
    return pl.pallas_call(
        _inproj_prep_body,
        grid=(bn // bb, seq // ll),
        in_specs=[tok(d), pl.BlockSpec((bb, 6, d), lambda b, l: (b, 0, 0)), full(p["norm1_g"])]
        + [full(w) for w in p["w_in"]] + [row(wd)] + [full(c) for c in consts],
        out_specs=[tok(w) for w in proj_w] + [tok(hw)] * 8 + [row(wd)],
        out_shape=shapes(proj_w) + shapes((hw,) * 8) + [jax.ShapeDtypeStruct((bn, 1, wd), F32)],
        scratch_shapes=[pltpu.VMEM((bb, ll, wd), F32), pltpu.VMEM((bb, 1, wd), F32)],
        compiler_params=_cparams(("arbitrary", "arbitrary"), VMEM_LIMIT),
        name="norm_inproj_prep",
    )(x, mod, p["norm1_g"], *p["w_in"], s_sh.reshape(bn, 1, wd), *consts)


def _unit_masks(n, tl):
    ri = lax.broadcasted_iota(I32, (n, n), 0)
    ci = lax.broadcasted_iota(I32, (n, n), 1)
    same = (ri >> _log2(tl)) == (ci >> _log2(tl))
    return same, same & (ri > ci), same & (ri >= ci)


def _rwscan_body(r_ref, lw_ref, k_ref, v_ref, a_ref, b_ref, s0_ref, y_ref, sn_ref, st, *, nu, nseq, tl, passes):
    n = nseq * tl
    n2 = 2 * n
    p_aa, p_inv, p_apply, p_state, p_y = passes

    hd = RW_HEAD

    @pl.when(pl.program_id(1) == 0)
    def _():
        zero = jnp.zeros((hd, hd), F32)
        for q in range(nu * nseq):
            for p in range(RW_HEADS // 2):
                st[q, p] = jnp.concatenate(
                    [jnp.concatenate([s0_ref[q, 2 * p], zero], axis=1),
                     jnp.concatenate([zero, s0_ref[q, 2 * p + 1]], axis=1)], axis=0)

    same, _, incl = _unit_masks(n, tl)
    m_cum = jnp.where(incl, 1.0, 0.0)
    m_seq = jnp.where(same, 1.0, 0.0)
    ri = lax.broadcasted_iota(I32, (n2, n2), 0)
    ci = lax.broadcasted_iota(I32, (n2, n2), 1)
    rt, ct = ri & (n - 1), ci & (n - 1)
    dsame = ((rt >> _log2(tl)) == (ct >> _log2(tl))) & ((ri >> _log2(n)) == (ci >> _log2(n)))
    strict_d = dsame & (rt > ct)
    incl_d = dsame & (rt >= ct)
    eye_d = jnp.where(ri == ci, 1.0, 0.0)
    lane = lax.broadcasted_iota(I32, (1, LANES), 1)
    m0 = jnp.where(lane < RW_HEAD, 1.0, 0.0)
    m1 = 1.0 - m0

    def dup(x):
        return jnp.concatenate([x * m0, x * m1], axis=0)

    def seq_rows(x, q):
        if nseq == 1:
            return x
        return jnp.concatenate([x[q * tl:(q + 1) * tl], x[n + q * tl:n + (q + 1) * tl]], axis=0)

    def unit_rows(parts):
        if nseq == 1:
            return parts[0]
        return jnp.concatenate([p[0:tl] for p in parts] + [p[tl:2 * tl] for p in parts], axis=0)

    chains = [(u, p) for u in range(nu) for p in range(RW_HEADS // 2)]
    ids = range(len(chains))
    cat0 = lambda *xs: jnp.concatenate(xs, axis=0)

    def ld(ref, c):
        u, p = chains[c]
        return ref[u * nseq:(u + 1) * nseq, :, p * LANES:(p + 1) * LANES].reshape(n, LANES)

    lw = [ld(lw_ref, c) for c in ids]
    cum = [_mm01(m_cum, x) for x in lw]
    tot = [_mm01(m_seq, x) for x in lw]
    e_c = [jnp.exp(x) for x in cum]
    e_n = [jnp.exp(-x) for x in cum]
    e_l = [jnp.exp(t - x) for t, x in zip(tot, cum)]
    at_d = [dup(ld(a_ref, c) * jnp.exp(cum[c] - lw[c])) for c in ids]
    rt_d = [dup(ld(r_ref, c) * e_c[c]) for c in ids]
    bt_d = [dup(ld(b_ref, c) * e_n[c]) for c in ids]
    kt_d = [dup(ld(k_ref, c) * e_n[c]) for c in ids]
    bh_d = [dup(ld(b_ref, c) * e_l[c]) for c in ids]
    kh_d = [dup(ld(k_ref, c) * e_l[c]) for c in ids]
    v_d = [dup(ld(v_ref, c)) for c in ids]
    aa = [_mm(cat0(at_d[c], rt_d[c]), cat0(bt_d[c], kt_d[c]), "nt", p_aa) for c in ids]
    a_ab = [jnp.where(strict_d, x[0:n2, 0:n2], 0.0) for x in aa]
    a_ak = [jnp.where(strict_d, x[0:n2, n2:], 0.0) for x in aa]
    a_rb = [jnp.where(incl_d, x[n2:, 0:n2], 0.0) for x in aa]
    a_rk = [jnp.where(incl_d, x[n2:, n2:], 0.0) for x in aa]
    zy = [_mm(cat0(a_ak[c], a_rk[c]), v_d[c], passes=p_apply) for c in ids]
    tinv = [eye_d + x for x in a_ab]
    nk = a_ab
    for _ in range(_log2(tl) - 1):
        nk = [_mm(x, x, passes=p_inv) for x in nk]
        tinv = [t + _mm(t, x, passes=p_inv) for t, x in zip(tinv, nk)]
    wu = [_mm(tinv[c], jnp.concatenate([at_d[c], zy[c][0:n2]], axis=1), passes=p_apply) for c in ids]
    seqs = range(nseq)
    srow = lambda c, q: (chains[c][0] * nseq + q, chains[c][1])
    s_old = [[st[srow(c, q)] for q in seqs] for c in ids]
    xs = [[_mm(cat0(seq_rows(wu[c][:, 0:LANES], q), seq_rows(rt_d[c], q)), s_old[c][q], "nt", p_state)
           for q in seqs] for c in ids]
    u_q = [[xs[c][q][0:2 * tl] + seq_rows(wu[c][:, LANES:], q) for q in seqs] for c in ids]
    for c in ids:
        for q in seqs:
            g_c = jnp.exp(tot[c][q * tl:q * tl + 1, :])
            st[srow(c, q)] = s_old[c][q] * g_c + _mm(cat0(u_q[c][q], seq_rows(v_d[c], q)),
                                                     cat0(seq_rows(bh_d[c], q), seq_rows(kh_d[c], q)), "tn", p_state)
    for c in ids:
        u, p = chains[c]
        y_d = (unit_rows([xs[c][q][2 * tl:] for q in seqs]) + _mm(a_rb[c], unit_rows(u_q[c]), passes=p_y)
               + zy[c][n2:])
        y_ref[u * nseq:(u + 1) * nseq, :, p * LANES:(p + 1) * LANES] = (y_d[0:n] + y_d[n:]).reshape(nseq, tl, LANES)

    @pl.when(pl.program_id(1) == pl.num_programs(1) - 1)
    def _():
        for q in range(nu * nseq):
            for p in range(RW_HEADS // 2):
                s = st[q, p]
                sn_ref[q, 2 * p] = s[0:hd, 0:hd]
                sn_ref[q, 2 * p + 1] = s[hd:, hd:]


def _unit_shape(bn, seq):
    if seq >= UNIT:
        assert seq % UNIT == 0
        return 1, UNIT
    assert UNIT % seq == 0 and bn % (UNIT // seq) == 0
    return UNIT // seq, seq


def _rwscan_call(r, lw, k2, v, a_s, b_s, s0, passes=RW_SCAN_PASSES):
    bn, seq, hw = r.shape
    nseq, tl = _unit_shape(bn, seq)
    nu = RW_UNITS_PER_STEP if bn % (RW_UNITS_PER_STEP * nseq) == 0 else 1
    rows = nu * nseq
    tok = pl.BlockSpec((rows, tl, hw), lambda b, c: (b, c, 0))
    stt = pl.BlockSpec((rows, RW_HEADS, RW_HEAD, RW_HEAD), lambda b, c: (b, 0, 0, 0))
    return pl.pallas_call(
        functools.partial(_rwscan_body, nu=nu, nseq=nseq, tl=tl, passes=passes),
        grid=(bn // rows, seq // tl),
        in_specs=[tok] * 6 + [stt],
        out_specs=[tok, stt],
        out_shape=[jax.ShapeDtypeStruct((bn, seq, hw), F32), jax.ShapeDtypeStruct(s0.shape, F32)],
        scratch_shapes=[pltpu.VMEM((rows, RW_HEADS // 2, LANES, LANES), F32)],
        compiler_params=_cparams(("arbitrary", "arbitrary"), VMEM_LIMIT),
        name="rwkv_scan",
    )(r, lw, k2, v, a_s, b_s, s0)


def _gla_body(qkv_ref, xal_ref, gate_ref, aup_ref, ab_ref, ng_ref, s0_ref, o_ref, sn_ref, st, *, nu, nseq, tl, cs):
    n = nseq * tl
    n2 = 2 * n
    nsub = tl // cs

    @pl.when(pl.program_id(1) == 0)
    def _():
        zero = jnp.zeros((GLA_DV, GLA_DK), F32)
        for q in range(nu * nseq):
            for p in range(GLA_HEADS // 2):
                st[q, p] = jnp.concatenate(
                    [jnp.concatenate([s0_ref[q, 2 * p].T, zero], axis=1),
                     jnp.concatenate([zero, s0_ref[q, 2 * p + 1].T], axis=1)], axis=0)

    same, _, incl = _unit_masks(n, cs)
    m_cum = jnp.where(incl, 1.0, 0.0)
    m_sub = jnp.where(same, 1.0, 0.0)
    ri = lax.broadcasted_iota(I32, (n2, n2), 0)
    ci = lax.broadcasted_iota(I32, (n2, n2), 1)
    rt, ct = ri & (n - 1), ci & (n - 1)
    causal_d = ((rt >> _log2(cs)) == (ct >> _log2(cs))) & ((ri >> _log2(n)) == (ci >> _log2(n))) & (rt >= ct)
    lane = lax.broadcasted_iota(I32, (1, LANES), 1)
    m0 = jnp.where(lane < GLA_DK, 1.0, 0.0)
    m1 = 1.0 - m0
    sr = lax.broadcasted_iota(I32, (2 * GLA_DV, LANES), 0)
    sc = lax.broadcasted_iota(I32, (2 * GLA_DV, LANES), 1)
    st_mask = jnp.where((sr >> _log2(GLA_DV)) == (sc >> _log2(GLA_DK)), 1.0, 0.0)

    def dup(x):
        return jnp.concatenate([x * m0, x * m1], axis=0)

    chains = [(u, p) for u in range(nu) for p in range(GLA_HEADS // 2)]
    ids = range(len(chains))
    urows = lambda u: slice(u * nseq, (u + 1) * nseq)
    ng = ng_ref[...]
    la_all = [-_softplus(-(_mm(xal_ref[urows(u), :, :].reshape(n, LANES), aup_ref[...], passes=3) + ab_ref[...]))
              * (1.0 / GLA_GATE_TAU) for u in range(nu)]

    def ld(ref, c, off, width):
        return ref[urows(chains[c][0]), :, off:off + width].reshape(n, width)

    q = [ld(qkv_ref, c, chains[c][1] * LANES, LANES) * (GLA_DK ** -0.5) for c in ids]
    k = [ld(qkv_ref, c, GLA_KW + chains[c][1] * LANES, LANES) for c in ids]
    vp = [ld(qkv_ref, c, 2 * GLA_KW + chains[c][1] * 2 * GLA_DV, 2 * GLA_DV) for c in ids]
    la = [la_all[u][:, p * LANES:(p + 1) * LANES] for u, p in chains]
    bc = [_mm01(m_cum, x) for x in la]
    bl = [_mm01(m_sub, x) for x in la]
    qe = [q[c] * jnp.exp(bc[c]) for c in ids]
    ke = [k[c] * jnp.exp(-bc[c]) for c in ids]
    kd = [k[c] * jnp.exp(bl[c] - bc[c]) for c in ids]
    att = [jnp.where(causal_d, _mm(dup(qe[c]), dup(ke[c]), "nt", passes=1), 0.0) for c in ids]
    v_st = [jnp.concatenate([x[:, 0:GLA_DV], x[:, GLA_DV:]], axis=0) for x in vp]
    o_st = [_mm(att[c], v_st[c], passes=1) for c in ids]
    upd = [[_mm(vp[c][r0:r0 + cs], kd[c][r0:r0 + cs], "tn", passes=1) for r0 in range(0, n, cs)] for c in ids]
    inter = [[None] * (n // cs) for _ in ids]
    for sq in range(nseq):
        s = [st[chains[c][0] * nseq + sq, chains[c][1]] for c in ids]
        for j in range(nsub):
            i = sq * nsub + j
            r0 = i * cs
            for c in ids:
                inter[c][i] = _mm(qe[c][r0:r0 + cs], s[c], "nt", passes=1)
                s[c] = s[c] * jnp.exp(bl[c][r0:r0 + 1, :]) + st_mask * upd[c][i]
        for c in ids:
            st[chains[c][0] * nseq + sq, chains[c][1]] = s[c]
    for c in ids:
        u, p = chains[c]
        o = o_st[c] + jnp.concatenate([x[:, 0:GLA_DV] for x in inter[c]] + [x[:, GLA_DV:] for x in inter[c]], axis=0)
        o = o * lax.rsqrt(jnp.mean(o * o, axis=-1, keepdims=True) + NORM_EPS) * ng
        goff = p * 2 * GLA_DV
        gp = ld(gate_ref, c, goff, 2 * GLA_DV)
        g_st = jnp.concatenate([gp[:, 0:GLA_DV], gp[:, GLA_DV:]], axis=0)
        ob = o * (g_st * _sigmoid(g_st))
        o_ref[urows(u), :, goff:goff + GLA_DV] = ob[0:n].reshape(nseq, tl, GLA_DV)
        o_ref[urows(u), :, goff + GLA_DV:goff + 2 * GLA_DV] = ob[n:].reshape(nseq, tl, GLA_DV)

    @pl.when(pl.program_id(1) == pl.num_programs(1) - 1)
    def _():
        for q in range(nu * nseq):
            for p in range(GLA_HEADS // 2):
                s = st[q, p]
                sn_ref[q, 2 * p] = s[0:GLA_DV, 0:GLA_DK].T
                sn_ref[q, 2 * p + 1] = s[GLA_DV:, GLA_DK:].T


def _gla_call(qkv, xal, gate, s0, p):
    bn, seq, _ = qkv.shape
    nseq, tl = _unit_shape(bn, seq)
    cs = min(GLA_CHUNK, seq)
    assert tl % cs == 0
    nu = GLA_UNITS_PER_STEP if bn % (GLA_UNITS_PER_STEP * nseq) == 0 else 1
    rows = nu * nseq
    tok = lambda w: pl.BlockSpec((rows, tl, w), lambda b, c: (b, c, 0))
    full = lambda a: pl.BlockSpec(a.shape, lambda b, c: (0,) * a.ndim)
    stt = pl.BlockSpec((rows, GLA_HEADS, GLA_DK, GLA_DV), lambda b, c: (b, 0, 0, 0))
    consts = (p["gla_aup"], p["gla_ab"], p["gla_ng"])
    return pl.pallas_call(
        functools.partial(_gla_body, nu=nu, nseq=nseq, tl=tl, cs=cs),
        grid=(bn // rows, seq // tl),
        in_specs=[tok(QKV_W), tok(XAL_W), tok(GG_W)] + [full(c) for c in consts] + [stt],
        out_specs=[tok(GLA_VW), stt],
        out_shape=[jax.ShapeDtypeStruct((bn, seq, GLA_VW), F32), jax.ShapeDtypeStruct(s0.shape, F32)],
        scratch_shapes=[pltpu.VMEM((rows, GLA_HEADS // 2, 2 * GLA_DV, LANES), F32)],
        compiler_params=_cparams(("arbitrary", "arbitrary"), VMEM_LIMIT),
        name="gla_chunked",
    )(qkv, xal, gate, *consts, s0)


def _merge_body(y_ref, g_ref, bon_ref, ob_ref, mg_ref, x_ref, mod_ref, gng_ref, gnb_ref, bd_ref, wpa_ref,
                wpb_ref, wout_ref, n2_ref, rwh_ref, rwl_ref, *rest):
    x1_o, h2_o, lg_o = rest[-3:]
    bb, ll, d = x_ref.shape
    n = bb * ll
    hw = RW_WIDTH
    bd = bd_ref[...]
    y = y_ref[...].reshape(n, hw)
    mu = _xmm01(y, bd, pieces=2) * (1.0 / RW_HEAD)
    dv = y - mu
    var = _xmm01(dv * dv, bd, pieces=1) * (1.0 / RW_HEAD)
    yn = dv * lax.rsqrt(var + RW_GN_EPS) * gng_ref[...] + gnb_ref[...]
    o_a = (yn + bon_ref[...].reshape(n, hw)) * g_ref[...].reshape(n, hw)
    o_b = ob_ref[...].reshape(n, GLA_VW)
    mg = mg_ref[...].reshape(n, 2 * d)
    merged = _sigmoid(mg[:, 0:d]) * _mm(o_a, wpa_ref[...]) + _sigmoid(mg[:, d:]) * _mm(o_b, wpb_ref[...])
    mix = _mm(merged, wout_ref[...]).reshape(bb, ll, d)
    x1 = x_ref[...] + mod_ref[:, 2:3, :] * mix
    x1_o[...] = x1
    yn2 = x1 * lax.rsqrt(jnp.mean(x1 * x1, axis=-1, keepdims=True) + NORM_EPS) * n2_ref[...]
    h2 = (yn2 * (1.0 + mod_ref[:, 4:5, :]) + mod_ref[:, 3:4, :]).reshape(n, d)
    hh, hl = _split(h2, 2)
    rwh, rwl = rwh_ref[...], rwl_ref[...]
    nt = lambda a, b: lax.dot_general(a, b, _DN["nt"], preferred_element_type=F32)
    lg_o[...] = nt(rwh, hh) + nt(rwl, hh) + nt(rwh, hl)
    _rows_to_packed(h2_o, h2)


def _merge_call(y, g, bonus, o_b, mg, x, mod, p, tn, first_tok, shared):
    bn, seq, d = x.shape
    bb, ll = _tile(bn, seq, TOK_TILE)
    nl = seq // ll
    assert first_tok % (bb * ll) == 0
    t0 = first_tok // (bb * ll)
    n_in = 7 + 9
    extra = [] if shared is None else list(shared)
    alias = {} if shared is None else {n_in: 1, n_in + 1: 2}
    tok = lambda w: pl.BlockSpec((bb, ll, w), lambda b, l: (b, l, 0))
    full = lambda a: pl.BlockSpec(a.shape, lambda b, l: (0,) * a.ndim)
    consts = (p["gn_g"], p["gn_b"], p["bd64"], p["w_pa"], p["w_pb"], p["w_out"], p["norm2_g"], p["rw_hi"],
              p["rw_lo"])
    return pl.pallas_call(
        _merge_body,
        grid=(bn // bb, nl),
        in_specs=[tok(RW_WIDTH)] * 3 + [tok(GLA_VW), tok(MG_W), tok(d),
                                        pl.BlockSpec((bb, 6, d), lambda b, l: (b, 0, 0))] + [full(c) for c in consts]
        + [pl.BlockSpec(memory_space=pl.ANY)] * len(extra),
        out_specs=[tok(d),
                   pl.BlockSpec((bb * ll * PCH, LANES), lambda b, l: (t0 + b * nl + l, 0)),
                   pl.BlockSpec((N_EXPERTS, bb * ll), lambda b, l: (0, t0 + b * nl + l))],
        out_shape=[jax.ShapeDtypeStruct((bn, seq, d), F32),
                   jax.ShapeDtypeStruct((tn * PCH, LANES), I32),
                   jax.ShapeDtypeStruct((N_EXPERTS, tn), F32)],
        input_output_aliases=alias,
        compiler_params=_cparams(("arbitrary", "arbitrary"), VMEM_LIMIT),
        name="merge_outproj_router",
    )(y, g, bonus, o_b, mg, x, mod, *consts, *extra)


def _route_body(lg_ref, rb_ref, e_o, rk_o, w_o, cnt_o, carry):
    ne, tm = lg_ref.shape

    @pl.when(pl.program_id(0) == 0)
    def _():
        carry[...] = jnp.zeros_like(carry)

    neg = -jnp.inf
    scores = _sigmoid(lg_ref[...])
    sel = scores + rb_ref[...]
    row_i = lax.broadcasted_iota(I32, (ne, tm), 0)
    row = row_i.astype(F32)
    grp = (row_i >> _log2(GROUP_SIZE)).astype(F32)

    def first_max(x, ids, none):
        m = jnp.max(x, axis=0, keepdims=True)
        return m, jnp.min(jnp.where(x == m, ids, none), axis=0, keepdims=True)

    gs = []
    gids = lax.broadcasted_iota(I32, (GROUP_SIZE, tm), 0)
    for gidx in range(N_GROUPS):
        rows = slice(gidx * GROUP_SIZE, (gidx + 1) * GROUP_SIZE)
        sg = _sigmoid(lg_ref[rows, :]) + rb_ref[rows, :]
        ids = (gids + gidx * GROUP_SIZE).astype(F32)
        m1, i1 = first_max(sg, ids, float(ne))
        gs.append(m1 + jnp.max(jnp.where(ids == i1, neg, sg), axis=0, keepdims=True))
    gs = jnp.concatenate(gs, axis=0)
    gid = lax.broadcasted_iota(I32, (N_GROUPS, tm), 0).astype(F32)
    cur = jnp.full((ne, tm), neg, F32)
    for _ in range(TOPK_GROUPS):
        _, gi = first_max(gs, gid, float(N_GROUPS))
        cur = jnp.where(grp == gi, sel, cur)
        gs = jnp.where(gid == gi, neg, gs)

    pm = jnp.zeros((ne, tm), F32)
    eidx, wts = [], []
    for _ in range(TOP_K):
        _, ei = first_max(cur, row, float(ne))
        hit = row == ei
        pm = jnp.where(hit, 1.0, pm)
        eidx.append(ei)
        wts.append(jnp.sum(jnp.where(hit, scores, 0.0), axis=0, keepdims=True))
        cur = jnp.where(hit, neg, cur)
    wsum = wts[0]
    for w in wts[1:]:
        wsum = wsum + w

    ri = lax.broadcasted_iota(I32, (tm, tm), 0)
    ci = lax.broadcasted_iota(I32, (tm, tm), 1)
    earlier = jnp.where(ri < ci, 1.0, 0.0)
    rank = _mm(pm, earlier, passes=1) + carry[...]
    carry[...] = carry[...] + jnp.sum(pm, axis=1, keepdims=True)
    cnt_o[...] = carry[...]

    rks = [jnp.sum(jnp.where(row == e, rank, 0.0), axis=0, keepdims=True) for e in eidx]
    e_o[0] = jnp.concatenate(eidx, axis=0).astype(I32)
    rk_o[0] = jnp.concatenate(rks, axis=0).astype(I32)
    w_o[0] = jnp.concatenate([w / wsum * ROUTED_SCALE for w in wts], axis=0)


def _route_call(logits_t, router_b):
    ne, tn = logits_t.shape
    tm = TOK_TILE
    assert tn % tm == 0
    col = pl.BlockSpec((ne, 1), lambda i: (0, 0))
    tab = pl.BlockSpec((1, TOP_K, tm), lambda i: (i, 0, 0))
    tab_shape = (tn // tm, TOP_K, tm)
    return pl.pallas_call(
        _route_body,
        grid=(tn // tm,),
        in_specs=[pl.BlockSpec((ne, tm), lambda i: (0, i)), col],
        out_specs=[tab, tab, tab, col],
        out_shape=[jax.ShapeDtypeStruct(tab_shape, I32), jax.ShapeDtypeStruct(tab_shape, I32),
                   jax.ShapeDtypeStruct(tab_shape, F32), jax.ShapeDtypeStruct((ne, 1), F32)],
        scratch_shapes=[pltpu.VMEM((ne, 1), F32)],
        compiler_params=_cparams(("arbitrary",)),
        name="moe_route",
    )(logits_t, router_b.reshape(ne, 1))


def _dest_body(e_ref, rk_ref, ps_ref, d_o):
    ne, tm = ps_ref.shape[0], e_ref.shape[2]
    ids = lax.broadcasted_iota(I32, (ne, tm), 0)
    ps = ps_ref[...]
    for t in range(e_ref.shape[0]):
        first = [jnp.sum(jnp.where(ids == e_ref[t, kk:kk + 1, :], ps, 0.0), axis=0, keepdims=True)
                 for kk in range(TOP_K)]
        d_o[t] = (jnp.concatenate(first, axis=0).astype(I32) + rk_ref[t]) * PCH


def _dest_call(eidx, rank, pad_start):
    nt, _, tm = eidx.shape
    ne = pad_start.shape[0]
    per = next(k for k in (4, 2, 1) if nt % k == 0)
    tab = pl.BlockSpec((per, TOP_K, tm), lambda i: (i, 0, 0))
    return pl.pallas_call(
        _dest_body,
        grid=(nt // per,),
        in_specs=[tab, tab, pl.BlockSpec((ne, 1), lambda i: (0, 0))],
        out_specs=tab,
        out_shape=jax.ShapeDtypeStruct(eidx.shape, I32),
        compiler_params=_cparams(("arbitrary",)),
        name="moe_dest",
    )(eidx, rank, pad_start.astype(F32).reshape(ne, 1))


def _pslab(ref, offset):
    return ref.at[pl.ds(pl.multiple_of(offset, PCH), PCH)]


def _dispatch_body(d_ref, h2_ref, xs_hbm, sem, *, tm):
    def issue(m, carry):
        for kk in range(TOP_K):
            pltpu.make_async_copy(_slab(h2_ref, m), _pslab(xs_hbm, d_ref[0, kk, m]), sem).start(priority=kk % 2)
        return carry

    lax.fori_loop(0, tm, issue, 0)
    all_rows = xs_hbm.at[pl.ds(0, tm * TOP_K * PCH)]
    pltpu.make_async_copy(all_rows, all_rows, sem).wait()


def _assign_spec(tm, index_map):
    return pl.BlockSpec((1, TOP_K, tm), index_map, memory_space=pltpu.SMEM)


def _dispatch_call(dest, h2s, n_rows):
    tn = h2s.shape[0] // PCH
    tm = TOK_TILE
    assert dest.shape == (tn // tm, TOP_K, tm)
    blk = _assign_spec(tm, lambda i: (i, 0, 0))
    return pl.pallas_call(
        functools.partial(_dispatch_body, tm=tm),
        grid=(tn // tm,),
        in_specs=[blk, pl.BlockSpec((tm * PCH, LANES), lambda i: (i, 0))],
        out_specs=pl.BlockSpec(memory_space=pl.ANY),
        out_shape=jax.ShapeDtypeStruct((n_rows * PCH, LANES), I32),
        scratch_shapes=[pltpu.SemaphoreType.DMA],
        compiler_params=_cparams(("arbitrary",)),
        name="moe_dispatch",
    )(dest, h2s)


def _expert_body(bi_ref, nr_ref, ld_ref, nx_ref, xs_hbm, wg_hbm, wu_hbm, wd_hbm, ob_ref, wg_buf, wu_buf, wd_buf,
                 wg_bf, wu_bf, wd_bf, xbuf, sem, xsem):
    i = pl.program_id(0)
    nsteps = pl.num_programs(0)
    nr = nr_ref[i]
    slot = ld_ref[i]
    blk_rows = MOE_BLK * PCH

    def row_block(j):
        s = lax.rem(j, ROW_SLOTS)
        src = xs_hbm.at[pl.ds(pl.multiple_of(bi_ref[j] * blk_rows, blk_rows), blk_rows)]
        return pltpu.make_async_copy(src, xbuf.at[s], xsem.at[s])

    @pl.when(i == 0)
    def _():
        for j in range(ROW_SLOTS - 1):
            row_block(j).start()

    @pl.when(i + ROW_SLOTS - 1 < nsteps)
    def _():
        row_block(i + ROW_SLOTS - 1).start()

    def fetch(e, s):
        return (pltpu.make_async_copy(wg_hbm.at[e], wg_buf.at[s], sem.at[s]),
                pltpu.make_async_copy(wu_hbm.at[e], wu_buf.at[s], sem.at[s]),
                pltpu.make_async_copy(wd_hbm.at[e], wd_buf.at[s], sem.at[s]))

    @pl.when(i == 0)
    def _():
        for s in range(WEIGHT_SLOTS - 1):
            e0 = nx_ref[nx_ref.shape[0] - (WEIGHT_SLOTS - 1) + s]

            @pl.when(e0 >= 0)
            def _():
                for k, cp in enumerate(fetch(e0, s)):
                    cp.start(priority=k % 2)

    @pl.when(slot >= 0)
    def _():
        for cp in fetch(0, slot):
            cp.wait()

        @pl.when(nx_ref[i] >= 0)
        def _():
            for k, cp in enumerate(fetch(nx_ref[i], lax.rem(slot + WEIGHT_SLOTS - 1, WEIGHT_SLOTS))):
                cp.start(priority=k % 2)

        wg_bf[...] = wg_buf[slot].astype(BF16)
        wu_bf[...] = wu_buf[slot].astype(BF16)
        wd_bf[...] = wd_buf[slot].astype(BF16)

    row_block(i).wait()

    @pl.when(nr > 0)
    def _():
        part = MOE_BLK // EXPERT_PARTS
        firsts = [q * part for q in range(EXPERT_PARTS)]
        rid = lax.broadcasted_iota(I32, (part, LANES), 0)
        xs_ref = xbuf.at[lax.rem(i, ROW_SLOTS)]
        x = [_rows_from_packed(xs_ref, part, rid < nr - f, f) for f in firsts]
        hg = [jnp.dot(v, wg_bf[...], preferred_element_type=F32) for v in x]
        hu = [jnp.dot(v, wu_bf[...], preferred_element_type=F32) for v in x]
        hh = [(g * _sigmoid(g) * u).astype(BF16) for g, u in zip(hg, hu)]
        out = [jnp.dot(v, wd_bf[...], preferred_element_type=F32) for v in hh]
        for f, v in zip(firsts, out):
            _rows_to_packed(ob_ref, v, f)


def _expert_tables(counts, pad_start, pad_end, nb):
    ne = counts.shape[0]
    experts = jnp.arange(ne, dtype=I32)
    first_row = jnp.arange(nb, dtype=I32) * MOE_BLK
    block_e = jnp.minimum(jnp.sum(pad_end[None, :] <= first_row[:, None], axis=1), ne - 1).astype(I32)
    mine = block_e[:, None] == experts[None, :]
    pick = lambda v: jnp.sum(jnp.where(mine, v[None, :], 0), axis=1)
    has = counts > 0
    ordinal = jnp.cumsum(has.astype(I32)) - 1
    start_b, count_b, ord_b = pick(pad_start), pick(counts), pick(ordinal)
    block_rows = jnp.clip(start_b + count_b - first_row, 0, MOE_BLK).astype(I32)
    block_i = jnp.minimum(jnp.arange(nb, dtype=I32), pad_end[-1] // MOE_BLK - 1).astype(I32)
    starts = (first_row == start_b) & (block_rows > 0)
    load_slot = jnp.where(starts, ord_b % WEIGHT_SLOTS, -1).astype(I32)
    nth = lambda want: jnp.max(jnp.where(has[None, :] & (ordinal[None, :] == want[:, None]), experts[None, :], -1),
                               axis=1)
    ahead = jnp.where(starts, nth(ord_b + WEIGHT_SLOTS - 1), -1)
    lead = nth(jnp.arange(WEIGHT_SLOTS - 1, dtype=I32))
    return block_i, block_rows, load_slot, jnp.concatenate([ahead, lead]).astype(I32)


def _expert_call(tables, xs, wg, wu, wd):
    nb = xs.shape[0] // (MOE_BLK * PCH)
    assert nb >= ROW_SLOTS
    d, ff = wg.shape[1], wg.shape[2]
    rows = pl.BlockSpec((MOE_BLK * PCH, LANES), lambda i, bi, nr, ld, nx: (bi[i], 0))
    hbm = pl.BlockSpec(memory_space=pl.ANY)
    grid_spec = pltpu.PrefetchScalarGridSpec(
        num_scalar_prefetch=4,
        grid=(nb,),
        in_specs=[hbm, hbm, hbm, hbm],
        out_specs=rows,
        scratch_shapes=[pltpu.VMEM((WEIGHT_SLOTS, d, ff), F32), pltpu.VMEM((WEIGHT_SLOTS, d, ff), F32),
                        pltpu.VMEM((WEIGHT_SLOTS, ff, d), F32),
                        pltpu.VMEM((d, ff), BF16), pltpu.VMEM((d, ff), BF16), pltpu.VMEM((ff, d), BF16),
                        pltpu.VMEM((ROW_SLOTS, MOE_BLK * PCH, LANES), I32),
                        pltpu.SemaphoreType.DMA((WEIGHT_SLOTS,)), pltpu.SemaphoreType.DMA((ROW_SLOTS,))],
    )
    return pl.pallas_call(
        _expert_body,
        grid_spec=grid_spec,
        out_shape=jax.ShapeDtypeStruct(xs.shape, I32),
        compiler_params=_cparams(("arbitrary",), VMEM_LIMIT),
        name="moe_experts",
    )(*tables, xs, wg, wu, wd)


def _combine_body(d_ref, dn_ref, wt_ref, ob_hbm, h2_ref, x1_ref, mod_ref, sg_ref, su_ref,
                  sd_ref, fg_ref, out_ref, gbuf, rbuf, wcol, sem, *, tm, nl):
    bb, ll, d = x1_ref.shape
    step = pl.program_id(0) * nl + pl.program_id(1)
    last = pl.num_programs(0) * nl - 1
    parity = lax.rem(step, 2)
    grp = CMB_GROUP

    def request(d_tab, g, s):
        for j in range(grp):
            m = g * grp + j
            for kk in range(TOP_K):
                pltpu.make_async_copy(_pslab(ob_hbm, d_tab[0, kk, m]), _slab(gbuf.at[s], kk * tm + m),
                                      sem.at[s]).start(priority=kk % 2)

    def mix(g, s):
        r0 = pl.multiple_of(g * grp, grp)
        w = wcol[pl.ds(r0, grp), :]
        wk = [w[:, kk:kk + 1] for kk in range(TOP_K)]
        for c in range(PCH):
            acc_lo = acc_hi = None
            for kk in range(TOP_K):
                words = gbuf[s, pl.ds((kk * tm + r0) * PCH + c, grp, stride=PCH), :]
                lo, hi = _unpack_pair(words)
                acc_lo = wk[kk] * lo if acc_lo is None else acc_lo + wk[kk] * lo
                acc_hi = wk[kk] * hi if acc_hi is None else acc_hi + wk[kk] * hi
            rbuf[pl.ds(r0, grp), c * LANES:(c + 1) * LANES] = acc_lo
            rbuf[pl.ds(r0, grp), (c + PCH) * LANES:(c + PCH + 1) * LANES] = acc_hi

    @pl.when(step == 0)
    def _():
        def first(g, carry):
            request(d_ref, g, 0)
            return carry
        lax.fori_loop(0, tm // grp, first, 0)

    ri = lax.broadcasted_iota(I32, (tm, tm), 0)
    ci = lax.broadcasted_iota(I32, (tm, tm), 1)
    eye = jnp.where(ri == ci, 1.0, 0.0).astype(BF16)
    wc = None
    for piece in _split(wt_ref[0], 3):
        t = lax.dot_general(eye, piece, _DN["nt"], preferred_element_type=F32)
        wc = t if wc is None else wc + t
    wcol[...] = wc

    def run(slot):
        pltpu.make_async_copy(ob_hbm.at[pl.ds(0, tm * TOP_K * PCH)], gbuf.at[slot], sem.at[slot]).wait()

        @pl.when(step < last)
        def _():
            def both(g, carry):
                request(dn_ref, g, 1 - slot)
                mix(g, slot)
                return carry
            lax.fori_loop(0, tm // grp, both, 0)

        @pl.when(step == last)
        def _():
            def only(g, carry):
                mix(g, slot)
                return carry
            lax.fori_loop(0, tm // grp, only, 0)

    for slot in range(2):
        pl.when(parity == slot)(functools.partial(run, slot))

    routed = rbuf[...]
    h2 = _rows_from_packed(h2_ref, tm)
    hg = jnp.dot(h2, sg_ref[...], preferred_element_type=F32)
    hu = jnp.dot(h2, su_ref[...], preferred_element_type=F32)
    shared = jnp.dot((hg * _sigmoid(hg) * hu).astype(BF16), sd_ref[...], preferred_element_type=F32)
    ff = (routed + shared).reshape(bb, ll, d)
    x2 = x1_ref[...] + mod_ref[:, 5:6, :] * ff
    out_ref[...] = x2 * lax.rsqrt(jnp.mean(x2 * x2, axis=-1, keepdims=True) + NORM_EPS) * fg_ref[...]


def _combine_call(dest, wts, first_tok, ob, h2s, x1, mod, p):
    bn, seq, d = x1.shape
    tm = CMB_TILE
    bb, ll = _tile(bn, seq, tm)
    nl = seq // ll
    tn = bn * seq
    nsteps = tn // tm
    per = dest.shape[2] // tm
    assert first_tok % tm == 0 and dest.shape[2] % tm == 0
    tile = lambda g: ((first_tok // tm + g) // per, 0, (first_tok // tm + g) % per)
    smem = _assign_spec(tm, lambda b, l: tile(b * nl + l))
    smem_next = _assign_spec(tm, lambda b, l: tile(jnp.minimum(b * nl + l + 1, nsteps - 1)))
    wblk = pl.BlockSpec((1, TOP_K, tm), lambda b, l: tile(b * nl + l))
    tok = pl.BlockSpec((bb, ll, d), lambda b, l: (b, l, 0))
    full = lambda a: pl.BlockSpec(a.shape, lambda b, l: (0,) * a.ndim)
    consts = (p["sh_gate"], p["sh_up"], p["sh_down"], p["final_g"])
    return pl.pallas_call(
        functools.partial(_combine_body, tm=tm, nl=nl),
        grid=(bn // bb, nl),
        in_specs=[smem, smem_next, wblk, pl.BlockSpec(memory_space=pl.ANY),
                  pl.BlockSpec((tm * PCH, LANES), lambda b, l: (first_tok // tm + b * nl + l, 0)),
                  tok, pl.BlockSpec((bb, 6, d), lambda b, l: (b, 0, 0))] + [full(c) for c in consts],
        out_specs=tok,
        out_shape=jax.ShapeDtypeStruct((bn, seq, d), F32),
        scratch_shapes=[pltpu.VMEM((2, tm * TOP_K * PCH, LANES), I32), pltpu.VMEM((tm, d), F32),
                        pltpu.VMEM((tm, TOP_K), F32), pltpu.SemaphoreType.DMA((2,))],
        compiler_params=_cparams(("arbitrary", "arbitrary"), VMEM_LIMIT),
        name="moe_combine_final",
    )(dest, dest, wts, ob, h2s, x1, mod, *consts)


def _layer_params(l, ada_w, ada_b, norm1_g, norm2_g, w_in, mu_shift, rw_w0, rw_w_up, rw_a0, rw_a_up, rw_g_up,
                  rw_k_k, rw_k_a, rw_r_k, rw_gn_g, rw_gn_b, gla_a_up, gla_a_bias, gla_norm_g, w_pa, w_pb, w_out,
                  router_w, router_b, exp_gate, exp_up, exp_down, sh_gate, sh_up, sh_down):
    d = D_MODEL
    wi = w_in[l]
    gla0 = RW_SHIFT_COLS
    xal0 = gla0 + QKV_W
    pad = jnp.zeros((d, XAL_W - GLA_GATE_RANK), BF16)
    w_pieces = (wi[:, :xal0].astype(BF16),
                jnp.concatenate([wi[:, xal0:xal0 + GLA_GATE_RANK].astype(BF16), pad], axis=1),
                wi[:, xal0 + GLA_GATE_RANK:].astype(BF16))
    zr = jnp.zeros((RW_W_RANK, RW_WIDTH), F32)
    hid = jnp.arange(RW_WIDTH) // RW_HEAD
    row = lambda a: a.reshape(1, -1)
    rw_t = router_w[l].T
    rw_hi = rw_t.astype(BF16)
    return dict(
        ada_w=ada_w[l], ada_b=ada_b[l], norm1_g=norm1_g[l].reshape(1, 1, d),
        norm2_g=norm2_g[l].reshape(1, 1, d), w_in=w_pieces,
        mu=mu_shift[l].reshape(1, 1, -1), w0=row(rw_w0[l]), wup=jnp.concatenate([rw_w_up[l], zr], axis=0),
        a0=row(rw_a0[l]), aup=jnp.concatenate([zr, rw_a_up[l]], axis=0), gup=rw_g_up[l].astype(BF16),
        kk=row(rw_k_k[l]), ka=row(rw_k_a[l]), rk=row(rw_r_k[l]),
        bd64=(hid[:, None] == hid[None, :]).astype(BF16),
        gn_g=row(rw_gn_g[l]), gn_b=row(rw_gn_b[l]),
        gla_aup=jnp.concatenate([gla_a_up[l], jnp.zeros((XAL_W - GLA_GATE_RANK, GLA_KW), F32)], axis=0),
        gla_ab=row(gla_a_bias[l]), gla_ng=row(gla_norm_g[l]),
        w_pa=w_pa[l].astype(BF16), w_pb=w_pb[l].astype(BF16), w_out=w_out[l].astype(BF16),
        rw_hi=rw_hi, rw_lo=(rw_t - rw_hi.astype(F32)).astype(BF16), router_b=router_b[l],
        exp_gate=exp_gate[l], exp_up=exp_up[l], exp_down=exp_down[l],
        sh_gate=sh_gate[l].astype(BF16), sh_up=sh_up[l].astype(BF16), sh_down=sh_down[l].astype(BF16),
    )


def _mixer_group(x, mod, s_rw, s_sh, s_gla, p, tn, first_tok, shared):
    qkv, xal, gg, mg, r, lw, k2, v, a_s, b_s, g, bonus, new_sh = _inproj_prep_call(x, mod, s_sh, p)
    y, rw_new = _rwscan_call(r, lw, k2, v, a_s, b_s, s_rw)
    o_b, gla_new = _gla_call(qkv, xal, gg, s_gla, p)
    x1, h2s, logits = _merge_call(y, g, bonus, o_b, mg, x, mod, p, tn, first_tok, shared)
    states = (rw_new, new_sh[:, 0, :], gla_new)
    return x1, h2s, logits, states


def _moe(h2s, logits, p):
    tn = h2s.shape[0] // PCH
    eidx, rank, wts, counts = _route_call(logits, p["router_b"])
    counts = counts[:, 0].astype(I32)
    padded = (counts + MOE_BLK - 1) // MOE_BLK * MOE_BLK
    pad_end = jnp.cumsum(padded)
    pad_start = (pad_end - padded).astype(I32)
    nb = (tn * TOP_K + N_EXPERTS * (MOE_BLK - 1)) // MOE_BLK + 1
    tables = _expert_tables(counts, pad_start, pad_end, nb)
    dest = _dest_call(eidx, rank, pad_start)
    xs = _dispatch_call(dest, h2s, nb * MOE_BLK)
    ob = _expert_call(tables, xs, p["exp_gate"], p["exp_up"], p["exp_down"])
    return ob, dest, wts


def kernel(x_prompt, x_sample, c_prompt, c_sample, state_rwkv, state_shift, state_gla, ada_w, ada_b, norm1_g,
           norm2_g, w_in, mu_shift, rw_w0, rw_w_up, rw_a0, rw_a_up, rw_g_up, rw_k_k, rw_k_a, rw_r_k, rw_gn_g,
           rw_gn_b, gla_a_up, gla_a_bias, gla_norm_g, w_pa, w_pb, w_out, router_w, router_b, exp_gate, exp_up,
           exp_down, sh_gate, sh_up, sh_down, final_g):
    depth = ada_w.shape[0]
    bp, bs = x_prompt.shape[0], x_sample.shape[0]
    tp = bp * x_prompt.shape[1]
    tn = tp + bs * x_sample.shape[1]
    xs_g = [x_prompt, x_sample]
    c_all = jnp.concatenate([c_prompt, c_sample], axis=0)
    zeros = lambda shape: jnp.zeros(shape, x_prompt.dtype)
    new_states = [[], []]
    fg = final_g.reshape(1, 1, D_MODEL)
    for l in range(depth):
        p = _layer_params(l, ada_w, ada_b, norm1_g, norm2_g, w_in, mu_shift, rw_w0, rw_w_up, rw_a0, rw_a_up,
                          rw_g_up, rw_k_k, rw_k_a, rw_r_k, rw_gn_g, rw_gn_b, gla_a_up, gla_a_bias, gla_norm_g,
                          w_pa, w_pb, w_out, router_w, router_b, exp_gate, exp_up, exp_down, sh_gate, sh_up,
                          sh_down)
        p["final_g"] = fg
        mod_all = _mod_call(c_all, p["ada_w"], p["ada_b"])
        mods = [mod_all[:bp], mod_all[bp:]]
        states_in = [
            (zeros((bp, RW_HEADS, RW_HEAD, RW_HEAD)), zeros((bp, RW_SHIFT_COLS)),
             zeros((bp, GLA_HEADS, GLA_DK, GLA_DV))),
            (state_rwkv[l], state_shift[l], state_gla[l]),
        ]
        x1s, shared = [], None
        firsts = [0, tp]
        for gi in range(2):
            x1, h2_all, lg_all, st = _mixer_group(xs_g[gi], mods[gi], *states_in[gi], p, tn, firsts[gi], shared)
            shared = (h2_all, lg_all)
            x1s.append(x1)
            new_states[gi].append(st)
        ob, dest, wts = _moe(*shared, p)
        assert depth == 1, "the fused final norm assumes a single layer"
        xs_g = [_combine_call(dest, wts, firsts[gi], ob, shared[0], x1s[gi], mods[gi], p) for gi in range(2)]
    stack = lambda gi, j: new_states[gi][0][j][None] if depth == 1 else jnp.stack([s[j] for s in new_states[gi]])
    return (xs_g[0], xs_g[1], stack(0, 0), stack(0, 1), stack(0, 2), stack(1, 0), stack(1, 1), stack(1, 2))
```

```python
import functools

import jax
import jax.numpy as jnp
from jax import lax
from jax.experimental import pallas as pl
from jax.experimental.pallas import tpu as pltpu

F32, BF16, I32 = jnp.float32, jnp.bfloat16, jnp.int32

D_MODEL = 1024
RW_HEADS, RW_HEAD = 8, 64
RW_WIDTH = RW_HEADS * RW_HEAD
RW_W_RANK, RW_A_RANK, RW_G_RANK = 64, 64, 128
RW_GN_EPS = 64e-5
GLA_HEADS, GLA_DK, GLA_DV = 4, 64, 128
GLA_KW, GLA_VW = GLA_HEADS * GLA_DK, GLA_HEADS * GLA_DV
GLA_GATE_RANK = 16
GLA_GATE_TAU = 16.0
GLA_CHUNK = 16
RW_SHIFT_COLS = 3 * RW_WIDTH + RW_W_RANK + RW_A_RANK + RW_G_RANK
N_EXPERTS, TOP_K, N_GROUPS, TOPK_GROUPS = 256, 8, 8, 4
GROUP_SIZE = N_EXPERTS // N_GROUPS
EXPERT_FF = 256
ROUTED_SCALE = 2.5
NORM_EPS = 1e-6

LANES = 128
CHUNKS = D_MODEL // LANES
PCH = CHUNKS // 2
UNIT = 64
RW_SCAN_PASSES = (1, 1, 1, 1, 1)
GLA_UNITS_PER_STEP = 4
RW_UNITS_PER_STEP = 4
VMEM_LIMIT = 56 * 1024 * 1024

PA_W, QKV_W, XAL_W, GG_W, MG_W = RW_SHIFT_COLS, 2 * GLA_KW + GLA_VW, LANES, GLA_VW, 2 * D_MODEL

TOK_TILE = 256
MOE_BLK = 304
EXPERT_PARTS = 1
WEIGHT_SLOTS = 3
ROW_SLOTS = 3
CMB_TILE = 256
CMB_GROUP = 16

_DN = {
    "nn": (((1,), (0,)), ((), ())),
    "nt": (((1,), (1,)), ((), ())),
    "tn": (((0,), (0,)), ((), ())),
}


def _split(x, pieces):
    out, rem = [], x
    for i in range(pieces):
        p = rem.astype(BF16)
        out.append(p)
        if i + 1 < pieces:
            rem = rem - p.astype(F32)
    return out


def _mm(a, b, form="nn", passes=1):
    dn = _DN[form]
    if passes == 6:
        return lax.dot_general(a.astype(F32), b.astype(F32), dn, precision=lax.Precision.HIGHEST,
                               preferred_element_type=F32)
    if passes == 1:
        return lax.dot_general(a.astype(BF16), b.astype(BF16), dn, preferred_element_type=F32)
    ah, al = _split(a, 2)
    bh, bl = _split(b, 2)
    out = lax.dot_general(ah, bh, dn, preferred_element_type=F32)
    out = out + lax.dot_general(ah, bl, dn, preferred_element_type=F32)
    return out + lax.dot_general(al, bh, dn, preferred_element_type=F32)


def _mm01(m01, x, pieces=3):
    m = m01.astype(BF16)
    out = None
    for p in _split(x, pieces):
        t = lax.dot_general(m, p, _DN["nn"], preferred_element_type=F32)
        out = t if out is None else out + t
    return out


def _xmm01(x, m01, pieces=2):
    m = m01.astype(BF16)
    out = None
    for p in _split(x, pieces):
        t = lax.dot_general(p, m, _DN["nn"], preferred_element_type=F32)
        out = t if out is None else out + t
    return out


HI16 = -65536


def _bf16_bits(x):
    return lax.bitcast_convert_type(x.astype(BF16).astype(F32), I32)


def _unpack_pair(w):
    return lax.bitcast_convert_type(w << 16, F32), lax.bitcast_convert_type(w & HI16, F32)


def _rows_to_packed(ref, x, first=0):
    for c in range(PCH):
        lo = _bf16_bits(x[:, c * LANES:(c + 1) * LANES])
        hi = _bf16_bits(x[:, (c + PCH) * LANES:(c + PCH + 1) * LANES])
        ref[pl.ds(first * PCH + c, x.shape[0], stride=PCH), :] = ((lo >> 16) & 0xFFFF) | (hi & HI16)


def _rows_from_packed(ref, n, live=None, first=0):
    lows, highs = [], []
    for c in range(PCH):
        w = ref[pl.ds(first * PCH + c, n, stride=PCH), :]
        if live is not None:
            w = jnp.where(live, w, 0)
        lo, hi = _unpack_pair(w)
        lows.append(lo.astype(BF16))
        highs.append(hi.astype(BF16))
    return jnp.concatenate(lows + highs, axis=1)


def _slab(ref, row):
    return ref.at[pl.ds(pl.multiple_of(row * PCH, PCH), PCH)]


def _sigmoid(x):
    return 1.0 / (1.0 + jnp.exp(-x))


def _softplus(x):
    return jnp.maximum(x, 0.0) + jnp.log(1.0 + jnp.exp(-jnp.abs(x)))


def _log2(n):
    assert n > 0 and n & (n - 1) == 0, n
    return n.bit_length() - 1


def _cparams(sem, vmem=None):
    return pltpu.CompilerParams(dimension_semantics=sem, vmem_limit_bytes=vmem)


def _mod_body(c_ref, w_ref, b_ref, o_ref):
    c = c_ref[...]
    o_ref[0] = _mm(c * _sigmoid(c), w_ref[...], passes=3) + b_ref[...]


def _mod_call(c_all, ada_w, ada_b):
    bt, d = c_all.shape
    out = pl.pallas_call(
        _mod_body,
        grid=(6,),
        in_specs=[pl.BlockSpec((bt, d), lambda k: (0, 0)),
                  pl.BlockSpec((d, d), lambda k: (0, k)),
                  pl.BlockSpec((1, d), lambda k: (0, k))],
        out_specs=pl.BlockSpec((1, bt, d), lambda k: (k, 0, 0)),
        out_shape=jax.ShapeDtypeStruct((6, bt, d), F32),
        compiler_params=_cparams(("arbitrary",)),
        name="adaln_mod",
    )(c_all, ada_w, ada_b.reshape(1, 6 * d))
    return jnp.transpose(out, (1, 0, 2))


def _inproj_body(x_ref, mod_ref, g_ref, wa_ref, wx_ref, wb_ref, pa_ref, qkv_ref, xal_ref, gg_ref, mg_ref):
    bb, ll, d = x_ref.shape
    x = x_ref[...]
    y = x * lax.rsqrt(jnp.mean(x * x, axis=-1, keepdims=True) + NORM_EPS) * g_ref[...]
    h = y * (1.0 + mod_ref[:, 1:2, :]) + mod_ref[:, 0:1, :]
    hb = h.reshape(bb * ll, d).astype(BF16)
    for w_ref, outs in ((wa_ref, (pa_ref, qkv_ref)), (wx_ref, (xal_ref,)), (wb_ref, (gg_ref, mg_ref))):
        off = 0
        for ref in outs:
            w = ref.shape[-1]
            ref[...] = jnp.dot(hb, w_ref[:, off:off + w], preferred_element_type=F32).reshape(bb, ll, w)
            off += w


def _tile(bn, seq, tile):
    if seq >= tile:
        assert seq % tile == 0
        return 1, tile
    assert tile % seq == 0 and bn % (tile // seq) == 0
    return tile // seq, seq


def _rwprep_body(pa_ref, sh_ref, mu_ref, w0_ref, wup_ref, a0_ref, aup_ref, gup_ref, kk_ref, ka_ref, rk_ref,
                 bd_ref, r_o, lw_o, k_o, v_o, a_o, b_o, g_o, bon_o, nsh_o, carry):
    bb, ll, wd = pa_ref.shape
    n = bb * ll
    hw = RW_WIDTH

    @pl.when(pl.program_id(1) == 0)
    def _():
        carry[...] = sh_ref[...]

    pa = pa_ref[...]
    rolled = pltpu.roll(pa.reshape(n, wd), 1, 0).reshape(bb, ll, wd)
    tok = lax.broadcasted_iota(I32, (bb, ll, wd), 1)
    prev = jnp.where(tok == 0, carry[...], rolled)
    last = pa_ref[:, ll - 1:ll, :]
    carry[...] = last
    nsh_o[...] = last
    xs = (pa + (prev - pa) * mu_ref[...]).reshape(n, wd)

    r, k, v = xs[:, 0:hw], xs[:, hw:2 * hw], xs[:, 2 * hw:3 * hw]
    xwa = xs[:, 3 * hw:3 * hw + LANES]
    xg = xs[:, 3 * hw + LANES:]
    w_log = -_softplus(-(w0_ref[...] + _mm(jnp.tanh(xwa), wup_ref[...], passes=3))) - 0.5
    lw = -jnp.exp(w_log)
    a = _sigmoid(a0_ref[...] + _mm(xwa, aup_ref[...], passes=3))
    g = _mm(_sigmoid(xg), gup_ref[...])
    bd = bd_ref[...]
    kkv = k * kk_ref[...]
    kkn = kkv * lax.rsqrt(jnp.maximum(_xmm01(kkv * kkv, bd, pieces=1), 1e-24))
    k2 = k * (1.0 + (a - 1.0) * ka_ref[...])
    bonus = _xmm01(r * k2 * rk_ref[...], bd, pieces=1) * v
    for ref, val in ((r_o, r), (lw_o, lw), (k_o, k2), (v_o, v), (a_o, -kkn), (b_o, kkn * a), (g_o, g),
                     (bon_o, bonus)):
        ref[...] = val.reshape(bb, ll, hw)


def _inproj_prep_body(x_ref, mod_ref, g_ref, wa_ref, wx_ref, wb_ref, sh_ref, mu_ref, w0_ref, wup_ref, a0_ref, aup_ref,
                      gup_ref, kk_ref, ka_ref, rk_ref, bd_ref, qkv_o, xal_o, gg_o, mg_o, r_o, lw_o, k_o, v_o, a_o,
                      b_o, g_o, bon_o, nsh_o, pa_s, carry):
    _inproj_body(x_ref, mod_ref, g_ref, wa_ref, wx_ref, wb_ref, pa_s, qkv_o, xal_o, gg_o, mg_o)
    _rwprep_body(pa_s, sh_ref, mu_ref, w0_ref, wup_ref, a0_ref, aup_ref, gup_ref, kk_ref, ka_ref, rk_ref, bd_ref,
                 r_o, lw_o, k_o, v_o, a_o, b_o, g_o, bon_o, nsh_o, carry)


def _inproj_prep_call(x, mod, s_sh, p):
    bn, seq, d = x.shape
    bb, ll = _tile(bn, seq, TOK_TILE)
    hw, wd = RW_WIDTH, PA_W
    tok = lambda w: pl.BlockSpec((bb, ll, w), lambda b, l: (b, l, 0))
    row = lambda w: pl.BlockSpec((bb, 1, w), lambda b, l: (b, 0, 0))
    full = lambda a: pl.BlockSpec(a.shape, lambda b, l: (0,) * a.ndim)
    consts = (p["mu"], p["w0"], p["wup"], p["a0"], p["aup"], p["gup"], p["kk"], p["ka"], p["rk"], p["bd64"])
    proj_w = (QKV_W, XAL_W, GG_W, MG_W)
    shapes = lambda ws: [jax.ShapeDtypeStruct((bn, seq, w), F32) for w in ws]
    return pl.pallas_call(
        _inproj_prep_body,
        grid=(bn // bb, seq // ll),
        in_specs=[tok(d), pl.BlockSpec((bb, 6, d), lambda b, l: (b, 0, 0)), full(p["norm1_g"])]
        + [full(w) for w in p["w_in"]] + [row(wd)] + [full(c) for c in consts],
        out_specs=[tok(w) for w in proj_w] + [tok(hw)] * 8 + [row(wd)],
        out_shape=shapes(proj_w) + shapes((hw,) * 8) + [jax.ShapeDtypeStruct((bn, 1, wd), F32)],
        scratch_shapes=[pltpu.VMEM((bb, ll, wd), F32), pltpu.VMEM((bb, 1, wd), F32)],
        compiler_params=_cparams(("arbitrary", "arbitrary"), VMEM_LIMIT),
        name="norm_inproj_prep",
    )(x, mod, p["norm1_g"], *p["w_in"], s_sh.reshape(bn, 1, wd), *consts)


def _unit_masks(n, tl):
    ri = lax.broadcasted_iota(I32, (n, n), 0)
    ci = lax.broadcasted_iota(I32, (n, n), 1)
    same = (ri >> _log2(tl)) == (ci >> _log2(tl))
    return same, same & (ri > ci), same & (ri >= ci)


def _rwscan_body(r_ref, lw_ref, k_ref, v_ref, a_ref, b_ref, s0_ref, y_ref, sn_ref, st, *, nu, nseq, tl, passes):
    n = nseq * tl
    n2 = 2 * n
    p_aa, p_inv, p_apply, p_state, p_y = passes

    hd = RW_HEAD

    @pl.when(pl.program_id(1) == 0)
    def _():
        zero = jnp.zeros((hd, hd), F32)
        for q in range(nu * nseq):
            for p in range(RW_HEADS // 2):
                st[q, p] = jnp.concatenate(
                    [jnp.concatenate([s0_ref[q, 2 * p], zero], axis=1),
                     jnp.concatenate([zero, s0_ref[q, 2 * p + 1]], axis=1)], axis=0)

    same, _, incl = _unit_masks(n, tl)
    m_cum = jnp.where(incl, 1.0, 0.0)
    m_seq = jnp.where(same, 1.0, 0.0)
    ri = lax.broadcasted_iota(I32, (n2, n2), 0)
    ci = lax.broadcasted_iota(I32, (n2, n2), 1)
    rt, ct = ri & (n - 1), ci & (n - 1)
    dsame = ((rt >> _log2(tl)) == (ct >> _log2(tl))) & ((ri >> _log2(n)) == (ci >> _log2(n)))
    strict_d = dsame & (rt > ct)
    incl_d = dsame & (rt >= ct)
    eye_d = jnp.where(ri == ci, 1.0, 0.0)
    lane = lax.broadcasted_iota(I32, (1, LANES), 1)
    m0 = jnp.where(lane < RW_HEAD, 1.0, 0.0)
    m1 = 1.0 - m0

    def dup(x):
        return jnp.concatenate([x * m0, x * m1], axis=0)

    def seq_rows(x, q):
        if nseq == 1:
            return x
        return jnp.concatenate([x[q * tl:(q + 1) * tl], x[n + q * tl:n + (q + 1) * tl]], axis=0)

    def unit_rows(parts):
        if nseq == 1:
            return parts[0]
        return jnp.concatenate([p[0:tl] for p in parts] + [p[tl:2 * tl] for p in parts], axis=0)

    chains = [(u, p) for u in range(nu) for p in range(RW_HEADS // 2)]
    ids = range(len(chains))
    cat0 = lambda *xs: jnp.concatenate(xs, axis=0)

    def ld(ref, c):
        u, p = chains[c]
        return ref[u * nseq:(u + 1) * nseq, :, p * LANES:(p + 1) * LANES].reshape(n, LANES)

    lw = [ld(lw_ref, c) for c in ids]
    cum = [_mm01(m_cum, x) for x in lw]
    tot = [_mm01(m_seq, x) for x in lw]
    e_c = [jnp.exp(x) for x in cum]
    e_n = [jnp.exp(-x) for x in cum]
    e_l = [jnp.exp(t - x) for t, x in zip(tot, cum)]
    at_d = [dup(ld(a_ref, c) * jnp.exp(cum[c] - lw[c])) for c in ids]
    rt_d = [dup(ld(r_ref, c) * e_c[c]) for c in ids]
    bt_d = [dup(ld(b_ref, c) * e_n[c]) for c in ids]
    kt_d = [dup(ld(k_ref, c) * e_n[c]) for c in ids]
    bh_d = [dup(ld(b_ref, c) * e_l[c]) for c in ids]
    kh_d = [dup(ld(k_ref, c) * e_l[c]) for c in ids]
    v_d = [dup(ld(v_ref, c)) for c in ids]
    aa = [_mm(cat0(at_d[c], rt_d[c]), cat0(bt_d[c], kt_d[c]), "nt", p_aa) for c in ids]
    a_ab = [jnp.where(strict_d, x[0:n2, 0:n2], 0.0) for x in aa]
    a_ak = [jnp.where(strict_d, x[0:n2, n2:], 0.0) for x in aa]
    a_rb = [jnp.where(incl_d, x[n2:, 0:n2], 0.0) for x in aa]
    a_rk = [jnp.where(incl_d, x[n2:, n2:], 0.0) for x in aa]
    zy = [_mm(cat0(a_ak[c], a_rk[c]), v_d[c], passes=p_apply) for c in ids]
    tinv = [eye_d + x for x in a_ab]
    nk = a_ab
    for _ in range(_log2(tl) - 1):
        nk = [_mm(x, x, passes=p_inv) for x in nk]
        tinv = [t + _mm(t, x, passes=p_inv) for t, x in zip(tinv, nk)]
    wu = [_mm(tinv[c], jnp.concatenate([at_d[c], zy[c][0:n2]], axis=1), passes=p_apply) for c in ids]
    seqs = range(nseq)
    srow = lambda c, q: (chains[c][0] * nseq + q, chains[c][1])
    s_old = [[st[srow(c, q)] for q in seqs] for c in ids]
    xs = [[_mm(cat0(seq_rows(wu[c][:, 0:LANES], q), seq_rows(rt_d[c], q)), s_old[c][q], "nt", p_state)
           for q in seqs] for c in ids]
    u_q = [[xs[c][q][0:2 * tl] + seq_rows(wu[c][:, LANES:], q) for q in seqs] for c in ids]
    for c in ids:
        for q in seqs:
            g_c = jnp.exp(tot[c][q * tl:q * tl + 1, :])
            st[srow(c, q)] = s_old[c][q] * g_c + _mm(cat0(u_q[c][q], seq_rows(v_d[c], q)),
                                                     cat0(seq_rows(bh_d[c], q), seq_rows(kh_d[c], q)), "tn", p_state)
    for c in ids:
        u, p = chains[c]
        y_d = (unit_rows([xs[c][q][2 * tl:] for q in seqs]) + _mm(a_rb[c], unit_rows(u_q[c]), passes=p_y)
               + zy[c][n2:])
        y_ref[u * nseq:(u + 1) * nseq, :, p * LANES:(p + 1) * LANES] = (y_d[0:n] + y_d[n:]).reshape(nseq, tl, LANES)

    @pl.when(pl.program_id(1) == pl.num_programs(1) - 1)
    def _():
        for q in range(nu * nseq):
            for p in range(RW_HEADS // 2):
                s = st[q, p]
                sn_ref[q, 2 * p] = s[0:hd, 0:hd]
                sn_ref[q, 2 * p + 1] = s[hd:, hd:]


def _unit_shape(bn, seq):
    if seq >= UNIT:
        assert seq % UNIT == 0
        return 1, UNIT
    assert UNIT % seq == 0 and bn % (UNIT // seq) == 0
    return UNIT // seq, seq


def _rwscan_call(r, lw, k2, v, a_s, b_s, s0, passes=RW_SCAN_PASSES):
    bn, seq, hw = r.shape
    nseq, tl = _unit_shape(bn, seq)
    nu = RW_UNITS_PER_STEP if bn % (RW_UNITS_PER_STEP * nseq) == 0 else 1
    rows = nu * nseq
    tok = pl.BlockSpec((rows, tl, hw), lambda b, c: (b, c, 0))
    stt = pl.BlockSpec((rows, RW_HEADS, RW_HEAD, RW_HEAD), lambda b, c: (b, 0, 0, 0))
    return pl.pallas_call(
        functools.partial(_rwscan_body, nu=nu, nseq=nseq, tl=tl, passes=passes),
        grid=(bn // rows, seq // tl),
        in_specs=[tok] * 6 + [stt],
        out_specs=[tok, stt],
        out_shape=[jax.ShapeDtypeStruct((bn, seq, hw), F32), jax.ShapeDtypeStruct(s0.shape, F32)],
        scratch_shapes=[pltpu.VMEM((rows, RW_HEADS // 2, LANES, LANES), F32)],
        compiler_params=_cparams(("arbitrary", "arbitrary"), VMEM_LIMIT),
        name="rwkv_scan",
    )(r, lw, k2, v, a_s, b_s, s0)


def _gla_body(qkv_ref, xal_ref, gate_ref, aup_ref, ab_ref, ng_ref, s0_ref, o_ref, sn_ref, st, *, nu, nseq, tl, cs):
    n = nseq * tl
    n2 = 2 * n
    nsub = tl // cs

    @pl.when(pl.program_id(1) == 0)
    def _():
        zero = jnp.zeros((GLA_DV, GLA_DK), F32)
        for q in range(nu * nseq):
            for p in range(GLA_HEADS // 2):
                st[q, p] = jnp.concatenate(
                    [jnp.concatenate([s0_ref[q, 2 * p].T, zero], axis=1),
                     jnp.concatenate([zero, s0_ref[q, 2 * p + 1].T], axis=1)], axis=0)

    same, _, incl = _unit_masks(n, cs)
    m_cum = jnp.where(incl, 1.0, 0.0)
    m_sub = jnp.where(same, 1.0, 0.0)
    ri = lax.broadcasted_iota(I32, (n2, n2), 0)
    ci = lax.broadcasted_iota(I32, (n2, n2), 1)
    rt, ct = ri & (n - 1), ci & (n - 1)
    causal_d = ((rt >> _log2(cs)) == (ct >> _log2(cs))) & ((ri >> _log2(n)) == (ci >> _log2(n))) & (rt >= ct)
    lane = lax.broadcasted_iota(I32, (1, LANES), 1)
    m0 = jnp.where(lane < GLA_DK, 1.0, 0.0)
    m1 = 1.0 - m0
    sr = lax.broadcasted_iota(I32, (2 * GLA_DV, LANES), 0)
    sc = lax.broadcasted_iota(I32, (2 * GLA_DV, LANES), 1)
    st_mask = jnp.where((sr >> _log2(GLA_DV)) == (sc >> _log2(GLA_DK)), 1.0, 0.0)

    def dup(x):
        return jnp.concatenate([x * m0, x * m1], axis=0)

    chains = [(u, p) for u in range(nu) for p in range(GLA_HEADS // 2)]
    ids = range(len(chains))
    urows = lambda u: slice(u * nseq, (u + 1) * nseq)
    ng = ng_ref[...]
    la_all = [-_softplus(-(_mm(xal_ref[urows(u), :, :].reshape(n, LANES), aup_ref[...], passes=3) + ab_ref[...]))
              * (1.0 / GLA_GATE_TAU) for u in range(nu)]

    def ld(ref, c, off, width):
        return ref[urows(chains[c][0]), :, off:off + width].reshape(n, width)

    q = [ld(qkv_ref, c, chains[c][1] * LANES, LANES) * (GLA_DK ** -0.5) for c in ids]
    k = [ld(qkv_ref, c, GLA_KW + chains[c][1] * LANES, LANES) for c in ids]
    vp = [ld(qkv_ref, c, 2 * GLA_KW + chains[c][1] * 2 * GLA_DV, 2 * GLA_DV) for c in ids]
    la = [la_all[u][:, p * LANES:(p + 1) * LANES] for u, p in chains]
    bc = [_mm01(m_cum, x) for x in la]
    bl = [_mm01(m_sub, x) for x in la]
    qe = [q[c] * jnp.exp(bc[c]) for c in ids]
    ke = [k[c] * jnp.exp(-bc[c]) for c in ids]
    kd = [k[c] * jnp.exp(bl[c] - bc[c]) for c in ids]
    att = [jnp.where(causal_d, _mm(dup(qe[c]), dup(ke[c]), "nt", passes=1), 0.0) for c in ids]
    v_st = [jnp.concatenate([x[:, 0:GLA_DV], x[:, GLA_DV:]], axis=0) for x in vp]
    o_st = [_mm(att[c], v_st[c], passes=1) for c in ids]
    upd = [[_mm(vp[c][r0:r0 + cs], kd[c][r0:r0 + cs], "tn", passes=1) for r0 in range(0, n, cs)] for c in ids]
    inter = [[None] * (n // cs) for _ in ids]
    for sq in range(nseq):
        s = [st[chains[c][0] * nseq + sq, chains[c][1]] for c in ids]
        for j in range(nsub):
            i = sq * nsub + j
            r0 = i * cs
            for c in ids:
                inter[c][i] = _mm(qe[c][r0:r0 + cs], s[c], "nt", passes=1)
                s[c] = s[c] * jnp.exp(bl[c][r0:r0 + 1, :]) + st_mask * upd[c][i]
        for c in ids:
            st[chains[c][0] * nseq + sq, chains[c][1]] = s[c]
    for c in ids:
        u, p = chains[c]
        o = o_st[c] + jnp.concatenate([x[:, 0:GLA_DV] for x in inter[c]] + [x[:, GLA_DV:] for x in inter[c]], axis=0)
        o = o * lax.rsqrt(jnp.mean(o * o, axis=-1, keepdims=True) + NORM_EPS) * ng
        goff = p * 2 * GLA_DV
        gp = ld(gate_ref, c, goff, 2 * GLA_DV)
        g_st = jnp.concatenate([gp[:, 0:GLA_DV], gp[:, GLA_DV:]], axis=0)
        ob = o * (g_st * _sigmoid(g_st))
        o_ref[urows(u), :, goff:goff + GLA_DV] = ob[0:n].reshape(nseq, tl, GLA_DV)
        o_ref[urows(u), :, goff + GLA_DV:goff + 2 * GLA_DV] = ob[n:].reshape(nseq, tl, GLA_DV)

    @pl.when(pl.program_id(1) == pl.num_programs(1) - 1)
    def _():
        for q in range(nu * nseq):
            for p in range(GLA_HEADS // 2):
                s = st[q, p]
                sn_ref[q, 2 * p] = s[0:GLA_DV, 0:GLA_DK].T
                sn_ref[q, 2 * p + 1] = s[GLA_DV:, GLA_DK:].T


def _gla_call(qkv, xal, gate, s0, p):
    bn, seq, _ = qkv.shape
    nseq, tl = _unit_shape(bn, seq)
    cs = min(GLA_CHUNK, seq)
    assert tl % cs == 0
    nu = GLA_UNITS_PER_STEP if bn % (GLA_UNITS_PER_STEP * nseq) == 0 else 1
    rows = nu * nseq
    tok = lambda w: pl.BlockSpec((rows, tl, w), lambda b, c: (b, c, 0))
    full = lambda a: pl.BlockSpec(a.shape, lambda b, c: (0,) * a.ndim)
    stt = pl.BlockSpec((rows, GLA_HEADS, GLA_DK, GLA_DV), lambda b, c: (b, 0, 0, 0))
    consts = (p["gla_aup"], p["gla_ab"], p["gla_ng"])
    return pl.pallas_call(
        functools.partial(_gla_body, nu=nu, nseq=nseq, tl=tl, cs=cs),
        grid=(bn // rows, seq // tl),
        in_specs=[tok(QKV_W), tok(XAL_W), tok(GG_W)] + [full(c) for c in consts] + [stt],
        out_specs=[tok(GLA_VW), stt],
        out_shape=[jax.ShapeDtypeStruct((bn, seq, GLA_VW), F32), jax.ShapeDtypeStruct(s0.shape, F32)],
        scratch_shapes=[pltpu.VMEM((rows, GLA_HEADS // 2, 2 * GLA_DV, LANES), F32)],
        compiler_params=_cparams(("arbitrary", "arbitrary"), VMEM_LIMIT),
        name="gla_chunked",
    )(qkv, xal, gate, *consts, s0)


def _merge_body(y_ref, g_ref, bon_ref, ob_ref, mg_ref, x_ref, mod_ref, gng_ref, gnb_ref, bd_ref, wpa_ref,
                wpb_ref, wout_ref, n2_ref, rwh_ref, rwl_ref, *rest):
    x1_o, h2_o, lg_o = rest[-3:]
    bb, ll, d = x_ref.shape
    n = bb * ll
    hw = RW_WIDTH
    bd = bd_ref[...]
    y = y_ref[...].reshape(n, hw)
    mu = _xmm01(y, bd, pieces=2) * (1.0 / RW_HEAD)
    dv = y - mu
    var = _xmm01(dv * dv, bd, pieces=1) * (1.0 / RW_HEAD)
    yn = dv * lax.rsqrt(var + RW_GN_EPS) * gng_ref[...] + gnb_ref[...]
    o_a = (yn + bon_ref[...].reshape(n, hw)) * g_ref[...].reshape(n, hw)
    o_b = ob_ref[...].reshape(n, GLA_VW)
    mg = mg_ref[...].reshape(n, 2 * d)
    merged = _sigmoid(mg[:, 0:d]) * _mm(o_a, wpa_ref[...]) + _sigmoid(mg[:, d:]) * _mm(o_b, wpb_ref[...])
    mix = _mm(merged, wout_ref[...]).reshape(bb, ll, d)
    x1 = x_ref[...] + mod_ref[:, 2:3, :] * mix
    x1_o[...] = x1
    yn2 = x1 * lax.rsqrt(jnp.mean(x1 * x1, axis=-1, keepdims=True) + NORM_EPS) * n2_ref[...]
    h2 = (yn2 * (1.0 + mod_ref[:, 4:5, :]) + mod_ref[:, 3:4, :]).reshape(n, d)
    hh, hl = _split(h2, 2)
    rwh, rwl = rwh_ref[...], rwl_ref[...]
    nt = lambda a, b: lax.dot_general(a, b, _DN["nt"], preferred_element_type=F32)
    lg_o[...] = nt(rwh, hh) + nt(rwl, hh) + nt(rwh, hl)
    _rows_to_packed(h2_o, h2)


def _merge_call(y, g, bonus, o_b, mg, x, mod, p, tn, first_tok, shared):
    bn, seq, d = x.shape
    bb, ll = _tile(bn, seq, TOK_TILE)
    nl = seq // ll
    assert first_tok % (bb * ll) == 0
    t0 = first_tok // (bb * ll)
    n_in = 7 + 9
    extra = [] if shared is None else list(shared)
    alias = {} if shared is None else {n_in: 1, n_in + 1: 2}
    tok = lambda w: pl.BlockSpec((bb, ll, w), lambda b, l: (b, l, 0))
    full = lambda a: pl.BlockSpec(a.shape, lambda b, l: (0,) * a.ndim)
    consts = (p["gn_g"], p["gn_b"], p["bd64"], p["w_pa"], p["w_pb"], p["w_out"], p["norm2_g"], p["rw_hi"],
              p["rw_lo"])
    return pl.pallas_call(
        _merge_body,
        grid=(bn // bb, nl),
        in_specs=[tok(RW_WIDTH)] * 3 + [tok(GLA_VW), tok(MG_W), tok(d),
                                        pl.BlockSpec((bb, 6, d), lambda b, l: (b, 0, 0))] + [full(c) for c in consts]
        + [pl.BlockSpec(memory_space=pl.ANY)] * len(extra),
        out_specs=[tok(d),
                   pl.BlockSpec((bb * ll * PCH, LANES), lambda b, l: (t0 + b * nl + l, 0)),
                   pl.BlockSpec((N_EXPERTS, bb * ll), lambda b, l: (0, t0 + b * nl + l))],
        out_shape=[jax.ShapeDtypeStruct((bn, seq, d), F32),
                   jax.ShapeDtypeStruct((tn * PCH, LANES), I32),
                   jax.ShapeDtypeStruct((N_EXPERTS, tn), F32)],
        input_output_aliases=alias,
        compiler_params=_cparams(("arbitrary", "arbitrary"), VMEM_LIMIT),
        name="merge_outproj_router",
    )(y, g, bonus, o_b, mg, x, mod, *consts, *extra)


def _route_body(lg_ref, rb_ref, e_o, rk_o, w_o, cnt_o, carry):
    ne, tm = lg_ref.shape

    @pl.when(pl.program_id(0) == 0)
    def _():
        carry[...] = jnp.zeros_like(carry)

    neg = -jnp.inf
    scores = _sigmoid(lg_ref[...])
    sel = scores + rb_ref[...]
    row_i = lax.broadcasted_iota(I32, (ne, tm), 0)
    row = row_i.astype(F32)
    grp = (row_i >> _log2(GROUP_SIZE)).astype(F32)

    def first_max(x, ids, none):
        m = jnp.max(x, axis=0, keepdims=True)
        return m, jnp.min(jnp.where(x == m, ids, none), axis=0, keepdims=True)

    gs = []
    gids = lax.broadcasted_iota(I32, (GROUP_SIZE, tm), 0)
    for gidx in range(N_GROUPS):
        rows = slice(gidx * GROUP_SIZE, (gidx + 1) * GROUP_SIZE)
        sg = _sigmoid(lg_ref[rows, :]) + rb_ref[rows, :]
        ids = (gids + gidx * GROUP_SIZE).astype(F32)
        m1, i1 = first_max(sg, ids, float(ne))
        gs.append(m1 + jnp.max(jnp.where(ids == i1, neg, sg), axis=0, keepdims=True))
    gs = jnp.concatenate(gs, axis=0)
    gid = lax.broadcasted_iota(I32, (N_GROUPS, tm), 0).astype(F32)
    cur = jnp.full((ne, tm), neg, F32)
    for _ in range(TOPK_GROUPS):
        _, gi = first_max(gs, gid, float(N_GROUPS))
        cur = jnp.where(grp == gi, sel, cur)
        gs = jnp.where(gid == gi, neg, gs)

    pm = jnp.zeros((ne, tm), F32)
    eidx, wts = [], []
    for _ in range(TOP_K):
        _, ei = first_max(cur, row, float(ne))
        hit = row == ei
        pm = jnp.where(hit, 1.0, pm)
        eidx.append(ei)
        wts.append(jnp.sum(jnp.where(hit, scores, 0.0), axis=0, keepdims=True))
        cur = jnp.where(hit, neg, cur)
    wsum = wts[0]
    for w in wts[1:]:
        wsum = wsum + w

    ri = lax.broadcasted_iota(I32, (tm, tm), 0)
    ci = lax.broadcasted_iota(I32, (tm, tm), 1)
    earlier = jnp.where(ri < ci, 1.0, 0.0)
    rank = _mm(pm, earlier, passes=1) + carry[...]
    carry[...] = carry[...] + jnp.sum(pm, axis=1, keepdims=True)
    cnt_o[...] = carry[...]

    rks = [jnp.sum(jnp.where(row == e, rank, 0.0), axis=0, keepdims=True) for e in eidx]
    e_o[0] = jnp.concatenate(eidx, axis=0).astype(I32)
    rk_o[0] = jnp.concatenate(rks, axis=0).astype(I32)
    w_o[0] = jnp.concatenate([w / wsum * ROUTED_SCALE for w in wts], axis=0)


def _route_call(logits_t, router_b):
    ne, tn = logits_t.shape
    tm = TOK_TILE
    assert tn % tm == 0
    col = pl.BlockSpec((ne, 1), lambda i: (0, 0))
    tab = pl.BlockSpec((1, TOP_K, tm), lambda i: (i, 0, 0))
    tab_shape = (tn // tm, TOP_K, tm)
    return pl.pallas_call(
        _route_body,
        grid=(tn // tm,),
        in_specs=[pl.BlockSpec((ne, tm), lambda i: (0, i)), col],
        out_specs=[tab, tab, tab, col],
        out_shape=[jax.ShapeDtypeStruct(tab_shape, I32), jax.ShapeDtypeStruct(tab_shape, I32),
                   jax.ShapeDtypeStruct(tab_shape, F32), jax.ShapeDtypeStruct((ne, 1), F32)],
        scratch_shapes=[pltpu.VMEM((ne, 1), F32)],
        compiler_params=_cparams(("arbitrary",)),
        name="moe_route",
    )(logits_t, router_b.reshape(ne, 1))


def _dest_body(e_ref, rk_ref, ps_ref, d_o):
    ne, tm = ps_ref.shape[0], e_ref.shape[2]
    ids = lax.broadcasted_iota(I32, (ne, tm), 0)
    ps = ps_ref[...]
    for t in range(e_ref.shape[0]):
        first = [jnp.sum(jnp.where(ids == e_ref[t, kk:kk + 1, :], ps, 0.0), axis=0, keepdims=True)
                 for kk in range(TOP_K)]
        d_o[t] = (jnp.concatenate(first, axis=0).astype(I32) + rk_ref[t]) * PCH


def _dest_call(eidx, rank, pad_start):
    nt, _, tm = eidx.shape
    ne = pad_start.shape[0]
    per = next(k for k in (4, 2, 1) if nt % k == 0)
    tab = pl.BlockSpec((per, TOP_K, tm), lambda i: (i, 0, 0))
    return pl.pallas_call(
        _dest_body,
        grid=(nt // per,),
        in_specs=[tab, tab, pl.BlockSpec((ne, 1), lambda i: (0, 0))],
        out_specs=tab,
        out_shape=jax.ShapeDtypeStruct(eidx.shape, I32),
        compiler_params=_cparams(("arbitrary",)),
        name="moe_dest",
    )(eidx, rank, pad_start.astype(F32).reshape(ne, 1))


def _pslab(ref, offset):
    return ref.at[pl.ds(pl.multiple_of(offset, PCH), PCH)]


def _dispatch_body(d_ref, h2_ref, xs_hbm, sem, *, tm):
    def issue(m, carry):
        for kk in range(TOP_K):
            pltpu.make_async_copy(_slab(h2_ref, m), _pslab(xs_hbm, d_ref[0, kk, m]), sem).start(priority=kk % 2)
        return carry

    lax.fori_loop(0, tm, issue, 0)
    all_rows = xs_hbm.at[pl.ds(0, tm * TOP_K * PCH)]
    pltpu.make_async_copy(all_rows, all_rows, sem).wait()


def _assign_spec(tm, index_map):
    return pl.BlockSpec((1, TOP_K, tm), index_map, memory_space=pltpu.SMEM)


def _dispatch_call(dest, h2s, n_rows):
    tn = h2s.shape[0] // PCH
    tm = TOK_TILE
    assert dest.shape == (tn // tm, TOP_K, tm)
    blk = _assign_spec(tm, lambda i: (i, 0, 0))
    return pl.pallas_call(
        functools.partial(_dispatch_body, tm=tm),
        grid=(tn // tm,),
        in_specs=[blk, pl.BlockSpec((tm * PCH, LANES), lambda i: (i, 0))],
        out_specs=pl.BlockSpec(memory_space=pl.ANY),
        out_shape=jax.ShapeDtypeStruct((n_rows * PCH, LANES), I32),
        scratch_shapes=[pltpu.SemaphoreType.DMA],
        compiler_params=_cparams(("arbitrary",)),
        name="moe_dispatch",
    )(dest, h2s)


def _expert_body(bi_ref, nr_ref, ld_ref, nx_ref, xs_hbm, wg_hbm, wu_hbm, wd_hbm, ob_ref, wg_buf, wu_buf, wd_buf,
                 wg_bf, wu_bf, wd_bf, xbuf, sem, xsem):
    i = pl.program_id(0)
    nsteps = pl.num_programs(0)
    nr = nr_ref[i]
    slot = ld_ref[i]
    blk_rows = MOE_BLK * PCH

    def row_block(j):
        s = lax.rem(j, ROW_SLOTS)
        src = xs_hbm.at[pl.ds(pl.multiple_of(bi_ref[j] * blk_rows, blk_rows), blk_rows)]
        return pltpu.make_async_copy(src, xbuf.at[s], xsem.at[s])

    @pl.when(i == 0)
    def _():
        for j in range(ROW_SLOTS - 1):
            row_block(j).start()

    @pl.when(i + ROW_SLOTS - 1 < nsteps)
    def _():
        row_block(i + ROW_SLOTS - 1).start()

    def fetch(e, s):
        return (pltpu.make_async_copy(wg_hbm.at[e], wg_buf.at[s], sem.at[s]),
                pltpu.make_async_copy(wu_hbm.at[e], wu_buf.at[s], sem.at[s]),
                pltpu.make_async_copy(wd_hbm.at[e], wd_buf.at[s], sem.at[s]))

    @pl.when(i == 0)
    def _():
        for s in range(WEIGHT_SLOTS - 1):
            e0 = nx_ref[nx_ref.shape[0] - (WEIGHT_SLOTS - 1) + s]

            @pl.when(e0 >= 0)
            def _():
                for k, cp in enumerate(fetch(e0, s)):
                    cp.start(priority=k % 2)

    @pl.when(slot >= 0)
    def _():
        for cp in fetch(0, slot):
            cp.wait()

        @pl.when(nx_ref[i] >= 0)
        def _():
            for k, cp in enumerate(fetch(nx_ref[i], lax.rem(slot + WEIGHT_SLOTS - 1, WEIGHT_SLOTS))):
                cp.start(priority=k % 2)

        wg_bf[...] = wg_buf[slot].astype(BF16)
        wu_bf[...] = wu_buf[slot].astype(BF16)
        wd_bf[...] = wd_buf[slot].astype(BF16)

    row_block(i).wait()

    @pl.when(nr > 0)
    def _():
        part = MOE_BLK // EXPERT_PARTS
        firsts = [q * part for q in range(EXPERT_PARTS)]
        rid = lax.broadcasted_iota(I32, (part, LANES), 0)
        xs_ref = xbuf.at[lax.rem(i, ROW_SLOTS)]
        x = [_rows_from_packed(xs_ref, part, rid < nr - f, f) for f in firsts]
        hg = [jnp.dot(v, wg_bf[...], preferred_element_type=F32) for v in x]
        hu = [jnp.dot(v, wu_bf[...], preferred_element_type=F32) for v in x]
        hh = [(g * _sigmoid(g) * u).astype(BF16) for g, u in zip(hg, hu)]
        out = [jnp.dot(v, wd_bf[...], preferred_element_type=F32) for v in hh]
        for f, v in zip(firsts, out):
            _rows_to_packed(ob_ref, v, f)


def _expert_tables(counts, pad_start, pad_end, nb):
    ne = counts.shape[0]
    experts = jnp.arange(ne, dtype=I32)
    first_row = jnp.arange(nb, dtype=I32) * MOE_BLK
    block_e = jnp.minimum(jnp.sum(pad_end[None, :] <= first_row[:, None], axis=1), ne - 1).astype(I32)
    mine = block_e[:, None] == experts[None, :]
    pick = lambda v: jnp.sum(jnp.where(mine, v[None, :], 0), axis=1)
    has = counts > 0
    ordinal = jnp.cumsum(has.astype(I32)) - 1
    start_b, count_b, ord_b = pick(pad_start), pick(counts), pick(ordinal)
    block_rows = jnp.clip(start_b + count_b - first_row, 0, MOE_BLK).astype(I32)
    block_i = jnp.minimum(jnp.arange(nb, dtype=I32), pad_end[-1] // MOE_BLK - 1).astype(I32)
    starts = (first_row == start_b) & (block_rows > 0)
    load_slot = jnp.where(starts, ord_b % WEIGHT_SLOTS, -1).astype(I32)
    nth = lambda want: jnp.max(jnp.where(has[None, :] & (ordinal[None, :] == want[:, None]), experts[None, :], -1),
                               axis=1)
    ahead = jnp.where(starts, nth(ord_b + WEIGHT_SLOTS - 1), -1)
    lead = nth(jnp.arange(WEIGHT_SLOTS - 1, dtype=I32))
    return block_i, block_rows, load_slot, jnp.concatenate([ahead, lead]).astype(I32)


def _expert_call(tables, xs, wg, wu, wd):
    nb = xs.shape[0] // (MOE_BLK * PCH)
    assert nb >= ROW_SLOTS
    d, ff = wg.shape[1], wg.shape[2]
    rows = pl.BlockSpec((MOE_BLK * PCH, LANES), lambda i, bi, nr, ld, nx: (bi[i], 0))
    hbm = pl.BlockSpec(memory_space=pl.ANY)
    grid_spec = pltpu.PrefetchScalarGridSpec(
        num_scalar_prefetch=4,
        grid=(nb,),
        in_specs=[hbm, hbm, hbm, hbm],
        out_specs=rows,
        scratch_shapes=[pltpu.VMEM((WEIGHT_SLOTS, d, ff), F32), pltpu.VMEM((WEIGHT_SLOTS, d, ff), F32),
                        pltpu.VMEM((WEIGHT_SLOTS, ff, d), F32),
                        pltpu.VMEM((d, ff), BF16), pltpu.VMEM((d, ff), BF16), pltpu.VMEM((ff, d), BF16),
                        pltpu.VMEM((ROW_SLOTS, MOE_BLK * PCH, LANES), I32),
                        pltpu.SemaphoreType.DMA((WEIGHT_SLOTS,)), pltpu.SemaphoreType.DMA((ROW_SLOTS,))],
    )
    return pl.pallas_call(
        _expert_body,
        grid_spec=grid_spec,
        out_shape=jax.ShapeDtypeStruct(xs.shape, I32),
        compiler_params=_cparams(("arbitrary",), VMEM_LIMIT),
        name="moe_experts",
    )(*tables, xs, wg, wu, wd)


def _combine_body(d_ref, dn_ref, wt_ref, ob_hbm, h2_ref, x1_ref, mod_ref, sg_ref, su_ref,
                  sd_ref, fg_ref, out_ref, gbuf, rbuf, wcol, sem, *, tm, nl):
    bb, ll, d = x1_ref.shape
    step = pl.program_id(0) * nl + pl.program_id(1)
    last = pl.num_programs(0) * nl - 1
    parity = lax.rem(step, 2)
    grp = CMB_GROUP

    def request(d_tab, g, s):
        for j in range(grp):
            m = g * grp + j
            for kk in range(TOP_K):
                pltpu.make_async_copy(_pslab(ob_hbm, d_tab[0, kk, m]), _slab(gbuf.at[s], kk * tm + m),
                                      sem.at[s]).start(priority=kk % 2)

    def mix(g, s):
        r0 = pl.multiple_of(g * grp, grp)
        w = wcol[pl.ds(r0, grp), :]
        wk = [w[:, kk:kk + 1] for kk in range(TOP_K)]
        for c in range(PCH):
            acc_lo = acc_hi = None
            for kk in range(TOP_K):
                words = gbuf[s, pl.ds((kk * tm + r0) * PCH + c, grp, stride=PCH), :]
                lo, hi = _unpack_pair(words)
                acc_lo = wk[kk] * lo if acc_lo is None else acc_lo + wk[kk] * lo
                acc_hi = wk[kk] * hi if acc_hi is None else acc_hi + wk[kk] * hi
            rbuf[pl.ds(r0, grp), c * LANES:(c + 1) * LANES] = acc_lo
            rbuf[pl.ds(r0, grp), (c + PCH) * LANES:(c + PCH + 1) * LANES] = acc_hi

    @pl.when(step == 0)
    def _():
        def first(g, carry):
            request(d_ref, g, 0)
            return carry
        lax.fori_loop(0, tm // grp, first, 0)

    ri = lax.broadcasted_iota(I32, (tm, tm), 0)
    ci = lax.broadcasted_iota(I32, (tm, tm), 1)
    eye = jnp.where(ri == ci, 1.0, 0.0).astype(BF16)
    wc = None
    for piece in _split(wt_ref[0], 3):
        t = lax.dot_general(eye, piece, _DN["nt"], preferred_element_type=F32)
        wc = t if wc is None else wc + t
    wcol[...] = wc

    def run(slot):
        pltpu.make_async_copy(ob_hbm.at[pl.ds(0, tm * TOP_K * PCH)], gbuf.at[slot], sem.at[slot]).wait()

        @pl.when(step < last)
        def _():
            def both(g, carry):
                request(dn_ref, g, 1 - slot)
                mix(g, slot)
                return carry
            lax.fori_loop(0, tm // grp, both, 0)

        @pl.when(step == last)
        def _():
            def only(g, carry):
                mix(g, slot)
                return carry
            lax.fori_loop(0, tm // grp, only, 0)

    for slot in range(2):
        pl.when(parity == slot)(functools.partial(run, slot))

    routed = rbuf[...]
    h2 = _rows_from_packed(h2_ref, tm)
    hg = jnp.dot(h2, sg_ref[...], preferred_element_type=F32)
    hu = jnp.dot(h2, su_ref[...], preferred_element_type=F32)
    shared = jnp.dot((hg * _sigmoid(hg) * hu).astype(BF16), sd_ref[...], preferred_element_type=F32)
    ff = (routed + shared).reshape(bb, ll, d)
    x2 = x1_ref[...] + mod_ref[:, 5:6, :] * ff
    out_ref[...] = x2 * lax.rsqrt(jnp.mean(x2 * x2, axis=-1, keepdims=True) + NORM_EPS) * fg_ref[...]


def _combine_call(dest, wts, first_tok, ob, h2s, x1, mod, p):
    bn, seq, d = x1.shape
    tm = CMB_TILE
    bb, ll = _tile(bn, seq, tm)
    nl = seq // ll
    tn = bn * seq
    nsteps = tn // tm
    per = dest.shape[2] // tm
    assert first_tok % tm == 0 and dest.shape[2] % tm == 0
    tile = lambda g: ((first_tok // tm + g) // per, 0, (first_tok // tm + g) % per)
    smem = _assign_spec(tm, lambda b, l: tile(b * nl + l))
    smem_next = _assign_spec(tm, lambda b, l: tile(jnp.minimum(b * nl + l + 1, nsteps - 1)))
    wblk = pl.BlockSpec((1, TOP_K, tm), lambda b, l: tile(b * nl + l))
    tok = pl.BlockSpec((bb, ll, d), lambda b, l: (b, l, 0))
    full = lambda a: pl.BlockSpec(a.shape, lambda b, l: (0,) * a.ndim)
    consts = (p["sh_gate"], p["sh_up"], p["sh_down"], p["final_g"])
    return pl.pallas_call(
        functools.partial(_combine_body, tm=tm, nl=nl),
        grid=(bn // bb, nl),
        in_specs=[smem, smem_next, wblk, pl.BlockSpec(memory_space=pl.ANY),
                  pl.BlockSpec((tm * PCH, LANES), lambda b, l: (first_tok // tm + b * nl + l, 0)),
                  tok, pl.BlockSpec((bb, 6, d), lambda b, l: (b, 0, 0))] + [full(c) for c in consts],
        out_specs=tok,
        out_shape=jax.ShapeDtypeStruct((bn, seq, d), F32),
        scratch_shapes=[pltpu.VMEM((2, tm * TOP_K * PCH, LANES), I32), pltpu.VMEM((tm, d), F32),
                        pltpu.VMEM((tm, TOP_K), F32), pltpu.SemaphoreType.DMA((2,))],
        compiler_params=_cparams(("arbitrary", "arbitrary"), VMEM_LIMIT),
        name="moe_combine_final",
    )(dest, dest, wts, ob, h2s, x1, mod, *consts)


def _layer_params(l, ada_w, ada_b, norm1_g, norm2_g, w_in, mu_shift, rw_w0, rw_w_up, rw_a0, rw_a_up, rw_g_up,
                  rw_k_k, rw_k_a, rw_r_k, rw_gn_g, rw_gn_b, gla_a_up, gla_a_bias, gla_norm_g, w_pa, w_pb, w_out,
                  router_w, router_b, exp_gate, exp_up, exp_down, sh_gate, sh_up, sh_down):
    d = D_MODEL
    wi = w_in[l]
    gla0 = RW_SHIFT_COLS
    xal0 = gla0 + QKV_W
    pad = jnp.zeros((d, XAL_W - GLA_GATE_RANK), BF16)
    w_pieces = (wi[:, :xal0].astype(BF16),
                jnp.concatenate([wi[:, xal0:xal0 + GLA_GATE_RANK].astype(BF16), pad], axis=1),
                wi[:, xal0 + GLA_GATE_RANK:].astype(BF16))
    zr = jnp.zeros((RW_W_RANK, RW_WIDTH), F32)
    hid = jnp.arange(RW_WIDTH) // RW_HEAD
    row = lambda a: a.reshape(1, -1)
    rw_t = router_w[l].T
    rw_hi = rw_t.astype(BF16)
    return dict(
        ada_w=ada_w[l], ada_b=ada_b[l], norm1_g=norm1_g[l].reshape(1, 1, d),
        norm2_g=norm2_g[l].reshape(1, 1, d), w_in=w_pieces,
        mu=mu_shift[l].reshape(1, 1, -1), w0=row(rw_w0[l]), wup=jnp.concatenate([rw_w_up[l], zr], axis=0),
        a0=row(rw_a0[l]), aup=jnp.concatenate([zr, rw_a_up[l]], axis=0), gup=rw_g_up[l].astype(BF16),
        kk=row(rw_k_k[l]), ka=row(rw_k_a[l]), rk=row(rw_r_k[l]),
        bd64=(hid[:, None] == hid[None, :]).astype(BF16),
        gn_g=row(rw_gn_g[l]), gn_b=row(rw_gn_b[l]),
        gla_aup=jnp.concatenate([gla_a_up[l], jnp.zeros((XAL_W - GLA_GATE_RANK, GLA_KW), F32)], axis=0),
        gla_ab=row(gla_a_bias[l]), gla_ng=row(gla_norm_g[l]),
        w_pa=w_pa[l].astype(BF16), w_pb=w_pb[l].astype(BF16), w_out=w_out[l].astype(BF16),
        rw_hi=rw_hi, rw_lo=(rw_t - rw_hi.astype(F32)).astype(BF16), router_b=router_b[l],
        exp_gate=exp_gate[l], exp_up=exp_up[l], exp_down=exp_down[l],
        sh_gate=sh_gate[l].astype(BF16), sh_up=sh_up[l].astype(BF16), sh_down=sh_down[l].astype(BF16),
    )


def _mixer_group(x, mod, s_rw, s_sh, s_gla, p, tn, first_tok, shared):
    qkv, xal, gg, mg, r, lw, k2, v, a_s, b_s, g, bonus, new_sh = _inproj_prep_call(x, mod, s_sh, p)
    y, rw_new = _rwscan_call(r, lw, k2, v, a_s, b_s, s_rw)
    o_b, gla_new = _gla_call(qkv, xal, gg, s_gla, p)
    x1, h2s, logits = _merge_call(y, g, bonus, o_b, mg, x, mod, p, tn, first_tok, shared)
    states = (rw_new, new_sh[:, 0, :], gla_new)
    return x1, h2s, logits, states


def _moe(h2s, logits, p):
    tn = h2s.shape[0] // PCH
    eidx, rank, wts, counts = _route_call(logits, p["router_b"])
    counts = counts[:, 0].astype(I32)
    padded = (counts + MOE_BLK - 1) // MOE_BLK * MOE_BLK
    pad_end = jnp.cumsum(padded)
    pad_start = (pad_end - padded).astype(I32)
    nb = (tn * TOP_K + N_EXPERTS * (MOE_BLK - 1)) // MOE_BLK + 1
    tables = _expert_tables(counts, pad_start, pad_end, nb)
    dest = _dest_call(eidx, rank, pad_start)
    xs = _dispatch_call(dest, h2s, nb * MOE_BLK)
    ob = _expert_call(tables, xs, p["exp_gate"], p["exp_up"], p["exp_down"])
    return ob, dest, wts


def kernel(x_prompt, x_sample, c_prompt, c_sample, state_rwkv, state_shift, state_gla, ada_w, ada_b, norm1_g,
           norm2_g, w_in, mu_shift, rw_w0, rw_w_up, rw_a0, rw_a_up, rw_g_up, rw_k_k, rw_k_a, rw_r_k, rw_gn_g,
           rw_gn_b, gla_a_up, gla_a_bias, gla_norm_g, w_pa, w_pb, w_out, router_w, router_b, exp_gate, exp_up,
           exp_down, sh_gate, sh_up, sh_down, final_g):
    depth = ada_w.shape[0]
    bp, bs = x_prompt.shape[0], x_sample.shape[0]
    tp = bp * x_prompt.shape[1]
    tn = tp + bs * x_sample.shape[1]
    xs_g = [x_prompt, x_sample]
    c_all = jnp.concatenate([c_prompt, c_sample], axis=0)
    zeros = lambda shape: jnp.zeros(shape, x_prompt.dtype)
    new_states = [[], []]
    fg = final_g.reshape(1, 1, D_MODEL)
    for l in range(depth):
        p = _layer_params(l, ada_w, ada_b, norm1_g, norm2_g, w_in, mu_shift, rw_w0, rw_w_up, rw_a0, rw_a_up,
                          rw_g_up, rw_k_k, rw_k_a, rw_r_k, rw_gn_g, rw_gn_b, gla_a_up, gla_a_bias, gla_norm_g,
                          w_pa, w_pb, w_out, router_w, router_b, exp_gate, exp_up, exp_down, sh_gate, sh_up,
                          sh_down)
        p["final_g"] = fg
        mod_all = _mod_call(c_all, p["ada_w"], p["ada_b"])
        mods = [mod_all[:bp], mod_all[bp:]]
        states_in = [
            (zeros((bp, RW_HEADS, RW_HEAD, RW_HEAD)), zeros((bp, RW_SHIFT_COLS)),
             zeros((bp, GLA_HEADS, GLA_DK, GLA_DV))),
            (state_rwkv[l], state_shift[l], state_gla[l]),
        ]
        x1s, shared = [], None
        firsts = [0, tp]
        for gi in range(2):
            x1, h2_all, lg_all, st = _mixer_group(xs_g[gi], mods[gi], *states_in[gi], p, tn, firsts[gi], shared)
            shared = (h2_all, lg_all)
            x1s.append(x1)
            new_states[gi].append(st)
        ob, dest, wts = _moe(*shared, p)
        assert depth == 1, "the fused final norm assumes a single layer"
        xs_g = [_combine_call(dest, wts, firsts[gi], ob, shared[0], x1s[gi], mods[gi], p) for gi in range(2)]
    stack = lambda gi, j: new_states[gi][0][j][None] if depth == 1 else jnp.stack([s[j] for s in new_states[gi]])
    return (xs_g[0], xs_g[1], stack(0, 0), stack(0, 1), stack(0, 2), stack(1, 0), stack(1, 1), stack(1, 2))
```

```python
import functools

import jax
import jax.numpy as jnp
from jax import lax
from jax.experimental import pallas as pl
from jax.experimental.pallas import tpu as pltpu

F32, BF16, I32 = jnp.float32, jnp.bfloat16, jnp.int32

D_MODEL = 1024
RW_HEADS, RW_HEAD = 8, 64
RW_WIDTH = RW_HEADS * RW_HEAD
RW_W_RANK, RW_A_RANK, RW_G_RANK = 64, 64, 128
RW_GN_EPS = 64e-5
GLA_HEADS, GLA_DK, GLA_DV = 4, 64, 128
GLA_KW, GLA_VW = GLA_HEADS * GLA_DK, GLA_HEADS * GLA_DV
GLA_GATE_RANK = 16
GLA_GATE_TAU = 16.0
GLA_CHUNK = 16
RW_SHIFT_COLS = 3 * RW_WIDTH + RW_W_RANK + RW_A_RANK + RW_G_RANK
N_EXPERTS, TOP_K, N_GROUPS, TOPK_GROUPS = 256, 8, 8, 4
GROUP_SIZE = N_EXPERTS // N_GROUPS
EXPERT_FF = 256
ROUTED_SCALE = 2.5
NORM_EPS = 1e-6

LANES = 128
CHUNKS = D_MODEL // LANES
PCH = CHUNKS // 2
UNIT = 64
RW_SCAN_PASSES = (1, 1, 1, 1, 1)
GLA_UNITS_PER_STEP = 4
RW_UNITS_PER_STEP = 4
VMEM_LIMIT = 56 * 1024 * 1024

PA_W, QKV_W, XAL_W, GG_W, MG_W = RW_SHIFT_COLS, 2 * GLA_KW + GLA_VW, LANES, GLA_VW, 2 * D_MODEL

TOK_TILE = 256
MOE_BLK = 640
EXPERT_PARTS = 1
WEIGHT_SLOTS = 3
ROW_SLOTS = 3
CMB_TILE = 256
CMB_GROUP = 16

_DN = {
    "nn": (((1,), (0,)), ((), ())),
    "nt": (((1,), (1,)), ((), ())),
    "tn": (((0,), (0,)), ((), ())),
}


def _split(x, pieces):
    out, rem = [], x
    for i in range(pieces):
        p = rem.astype(BF16)
        out.append(p)
        if i + 1 < pieces:
            rem = rem - p.astype(F32)
    return out


def _mm(a, b, form="nn", passes=1):
    dn = _DN[form]
    if passes == 6:
        return lax.dot_general(a.astype(F32), b.astype(F32), dn, precision=lax.Precision.HIGHEST,
                               preferred_element_type=F32)
    if passes == 1:
        return lax.dot_general(a.astype(BF16), b.astype(BF16), dn, preferred_element_type=F32)
    ah, al = _split(a, 2)
    bh, bl = _split(b, 2)
    out = lax.dot_general(ah, bh, dn, preferred_element_type=F32)
    out = out + lax.dot_general(ah, bl, dn, preferred_element_type=F32)
    return out + lax.dot_general(al, bh, dn, preferred_element_type=F32)


def _mm01(m01, x, pieces=3):
    m = m01.astype(BF16)
    out = None
    for p in _split(x, pieces):
        t = lax.dot_general(m, p, _DN["nn"], preferred_element_type=F32)
        out = t if out is None else out + t
    return out


def _xmm01(x, m01, pieces=2):
    m = m01.astype(BF16)
    out = None
    for p in _split(x, pieces):
        t = lax.dot_general(p, m, _DN["nn"], preferred_element_type=F32)
        out = t if out is None else out + t
    return out


HI16 = -65536


def _bf16_bits(x):
    return lax.bitcast_convert_type(x.astype(BF16).astype(F32), I32)


def _unpack_pair(w):
    return lax.bitcast_convert_type(w << 16, F32), lax.bitcast_convert_type(w & HI16, F32)


def _rows_to_packed(ref, x, first=0):
    for c in range(PCH):
        lo = _bf16_bits(x[:, c * LANES:(c + 1) * LANES])
        hi = _bf16_bits(x[:, (c + PCH) * LANES:(c + PCH + 1) * LANES])
        ref[pl.ds(first * PCH + c, x.shape[0], stride=PCH), :] = ((lo >> 16) & 0xFFFF) | (hi & HI16)


def _rows_from_packed(ref, n, live=None, first=0):
    lows, highs = [], []
    for c in range(PCH):
        w = ref[pl.ds(first * PCH + c, n, stride=PCH), :]
        if live is not None:
            w = jnp.where(live, w, 0)
        lo, hi = _unpack_pair(w)
        lows.append(lo.astype(BF16))
        highs.append(hi.astype(BF16))
    return jnp.concatenate(lows + highs, axis=1)


def _slab(ref, row):
    return ref.at[pl.ds(pl.multiple_of(row * PCH, PCH), PCH)]


def _sigmoid(x):
    return 1.0 / (1.0 + jnp.exp(-x))


def _softplus(x):
    return jnp.maximum(x, 0.0) + jnp.log(1.0 + jnp.exp(-jnp.abs(x)))


def _log2(n):
    assert n > 0 and n & (n - 1) == 0, n
    return n.bit_length() - 1


def _cparams(sem, vmem=None):
    return pltpu.CompilerParams(dimension_semantics=sem, vmem_limit_bytes=vmem)


def _mod_body(c_ref, w_ref, b_ref, o_ref):
    c = c_ref[...]
    o_ref[0] = _mm(c * _sigmoid(c), w_ref[...], passes=3) + b_ref[...]


def _mod_call(c_all, ada_w, ada_b):
    bt, d = c_all.shape
    out = pl.pallas_call(
        _mod_body,
        grid=(6,),
        in_specs=[pl.BlockSpec((bt, d), lambda k: (0, 0)),
                  pl.BlockSpec((d, d), lambda k: (0, k)),
                  pl.BlockSpec((1, d), lambda k: (0, k))],
        out_specs=pl.BlockSpec((1, bt, d), lambda k: (k, 0, 0)),
        out_shape=jax.ShapeDtypeStruct((6, bt, d), F32),
        compiler_params=_cparams(("arbitrary",)),
        name="adaln_mod",
    )(c_all, ada_w, ada_b.reshape(1, 6 * d))
    return jnp.transpose(out, (1, 0, 2))


def _inproj_body(x_ref, mod_ref, g_ref, wa_ref, wx_ref, wb_ref, pa_ref, qkv_ref, xal_ref, gg_ref, mg_ref):
    bb, ll, d = x_ref.shape
    x = x_ref[...]
    y = x * lax.rsqrt(jnp.mean(x * x, axis=-1, keepdims=True) + NORM_EPS) * g_ref[...]
    h = y * (1.0 + mod_ref[:, 1:2, :]) + mod_ref[:, 0:1, :]
    hb = h.reshape(bb * ll, d).astype(BF16)
    for w_ref, outs in ((wa_ref, (pa_ref, qkv_ref)), (wx_ref, (xal_ref,)), (wb_ref, (gg_ref, mg_ref))):
        off = 0
        for ref in outs:
            w = ref.shape[-1]
            ref[...] = jnp.dot(hb, w_ref[:, off:off + w], preferred_element_type=F32).reshape(bb, ll, w)
            off += w


def _tile(bn, seq, tile):
    if seq >= tile:
        assert seq % tile == 0
        return 1, tile
    assert tile % seq == 0 and bn % (tile // seq) == 0
    return tile // seq, seq


def _rwprep_body(pa_ref, sh_ref, mu_ref, w0_ref, wup_ref, a0_ref, aup_ref, gup_ref, kk_ref, ka_ref, rk_ref,
                 bd_ref, r_o, lw_o, k_o, v_o, a_o, b_o, g_o, bon_o, nsh_o, carry):
    bb, ll, wd = pa_ref.shape
    n = bb * ll
    hw = RW_WIDTH

    @pl.when(pl.program_id(1) == 0)
    def _():
        carry[...] = sh_ref[...]

    pa = pa_ref[...]
    rolled = pltpu.roll(pa.reshape(n, wd), 1, 0).reshape(bb, ll, wd)
    tok = lax.broadcasted_iota(I32, (bb, ll, wd), 1)
    prev = jnp.where(tok == 0, carry[...], rolled)
    last = pa_ref[:, ll - 1:ll, :]
    carry[...] = last
    nsh_o[...] = last
    xs = (pa + (prev - pa) * mu_ref[...]).reshape(n, wd)

    r, k, v = xs[:, 0:hw], xs[:, hw:2 * hw], xs[:, 2 * hw:3 * hw]
    xwa = xs[:, 3 * hw:3 * hw + LANES]
    xg = xs[:, 3 * hw + LANES:]
    w_log = -_softplus(-(w0_ref[...] + _mm(jnp.tanh(xwa), wup_ref[...], passes=3))) - 0.5
    lw = -jnp.exp(w_log)
    a = _sigmoid(a0_ref[...] + _mm(xwa, aup_ref[...], passes=3))
    g = _mm(_sigmoid(xg), gup_ref[...])
    bd = bd_ref[...]
    kkv = k * kk_ref[...]
    kkn = kkv * lax.rsqrt(jnp.maximum(_xmm01(kkv * kkv, bd, pieces=1), 1e-24))
    k2 = k * (1.0 + (a - 1.0) * ka_ref[...])
    bonus = _xmm01(r * k2 * rk_ref[...], bd, pieces=1) * v
    for ref, val in ((r_o, r), (lw_o, lw), (k_o, k2), (v_o, v), (a_o, -kkn), (b_o, kkn * a), (g_o, g),
                     (bon_o, bonus)):
        ref[...] = val.reshape(bb, ll, hw)


def _inproj_prep_body(x_ref, mod_ref, g_ref, wa_ref, wx_ref, wb_ref, sh_ref, mu_ref, w0_ref, wup_ref, a0_ref, aup_ref,
                      gup_ref, kk_ref, ka_ref, rk_ref, bd_ref, qkv_o, xal_o, gg_o, mg_o, r_o, lw_o, k_o, v_o, a_o,
                      b_o, g_o, bon_o, nsh_o, pa_s, carry):
    _inproj_body(x_ref, mod_ref, g_ref, wa_ref, wx_ref, wb_ref, pa_s, qkv_o, xal_o, gg_o, mg_o)
    _rwprep_body(pa_s, sh_ref, mu_ref, w0_ref, wup_ref, a0_ref, aup_ref, gup_ref, kk_ref, ka_ref, rk_ref, bd_ref,
                 r_o, lw_o, k_o, v_o, a_o, b_o, g_o, bon_o, nsh_o, carry)


def _inproj_prep_call(x, mod, s_sh, p):
    bn, seq, d = x.shape
    bb, ll = _tile(bn, seq, TOK_TILE)
    hw, wd = RW_WIDTH, PA_W
    tok = lambda w: pl.BlockSpec((bb, ll, w), lambda b, l: (b, l, 0))
    row = lambda w: pl.BlockSpec((bb, 1, w), lambda b, l: (b, 0, 0))
    full = lambda a: pl.BlockSpec(a.shape, lambda b, l: (0,) * a.ndim)
    consts = (p["mu"], p["w0"], p["wup"], p["a0"], p["aup"], p["gup"], p["kk"], p["ka"], p["rk"], p["bd64"])
    proj_w = (QKV_W, XAL_W, GG_W, MG_W)
    shapes = lambda ws: [jax.ShapeDtypeStruct((bn, seq, w), F32) for w in ws]
    return pl.pallas_call(
        _inproj_prep_body,
        grid=(bn // bb, seq // ll),
        in_specs=[tok(d), pl.BlockSpec((bb, 6, d), lambda b, l: (b, 0, 0)), full(p["norm1_g"])]
        + [full(w) for w in p["w_in"]] + [row(wd)] + [full(c) for c in consts],
        out_specs=[tok(w) for w in proj_w] + [tok(hw)] * 8 + [row(wd)],
        out_shape=shapes(proj_w) + shapes((hw,) * 8) + [jax.ShapeDtypeStruct((bn, 1, wd), F32)],
        scratch_shapes=[pltpu.VMEM((bb, ll, wd), F32), pltpu.VMEM((bb, 1, wd), F32)],
        compiler_params=_cparams(("arbitrary", "arbitrary"), VMEM_LIMIT),
        name="norm_inproj_prep",
    )(x, mod, p["norm1_g"], *p["w_in"], s_sh.reshape(bn, 1, wd), *consts)


def _unit_masks(n, tl):
    ri = lax.broadcasted_iota(I32, (n, n), 0)
    ci = lax.broadcasted_iota(I32, (n, n), 1)
    same = (ri >> _log2(tl)) == (ci >> _log2(tl))
    return same, same & (ri > ci), same & (ri >= ci)


def _rwscan_body(r_ref, lw_ref, k_ref, v_ref, a_ref, b_ref, s0_ref, y_ref, sn_ref, st, *, nu, nseq, tl, passes):
    n = nseq * tl
    n2 = 2 * n
    p_aa, p_inv, p_apply, p_state, p_y = passes

    hd = RW_HEAD

    @pl.when(pl.program_id(1) == 0)
    def _():
        zero = jnp.zeros((hd, hd), F32)
        for q in range(nu * nseq):
            for p in range(RW_HEADS // 2):
                st[q, p] = jnp.concatenate(
                    [jnp.concatenate([s0_ref[q, 2 * p], zero], axis=1),
                     jnp.concatenate([zero, s0_ref[q, 2 * p + 1]], axis=1)], axis=0)

    same, _, incl = _unit_masks(n, tl)
    m_cum = jnp.where(incl, 1.0, 0.0)
    m_seq = jnp.where(same, 1.0, 0.0)
    ri = lax.broadcasted_iota(I32, (n2, n2), 0)
    ci = lax.broadcasted_iota(I32, (n2, n2), 1)
    rt, ct = ri & (n - 1), ci & (n - 1)
    dsame = ((rt >> _log2(tl)) == (ct >> _log2(tl))) & ((ri >> _log2(n)) == (ci >> _log2(n)))
    strict_d = dsame & (rt > ct)
    incl_d = dsame & (rt >= ct)
    eye_d = jnp.where(ri == ci, 1.0, 0.0)
    lane = lax.broadcasted_iota(I32, (1, LANES), 1)
    m0 = jnp.where(lane < RW_HEAD, 1.0, 0.0)
    m1 = 1.0 - m0

    def dup(x):
        return jnp.concatenate([x * m0, x * m1], axis=0)

    def seq_rows(x, q):
        if nseq == 1:
            return x
        return jnp.concatenate([x[q * tl:(q + 1) * tl], x[n + q * tl:n + (q + 1) * tl]], axis=0)

    def unit_rows(parts):
        if nseq == 1:
            return parts[0]
        return jnp.concatenate([p[0:tl] for p in parts] + [p[tl:2 * tl] for p in parts], axis=0)

    chains = [(u, p) for u in range(nu) for p in range(RW_HEADS // 2)]
    ids = range(len(chains))
    cat0 = lambda *xs: jnp.concatenate(xs, axis=0)

    def ld(ref, c):
        u, p = chains[c]
        return ref[u * nseq:(u + 1) * nseq, :, p * LANES:(p + 1) * LANES].reshape(n, LANES)

    lw = [ld(lw_ref, c) for c in ids]
    cum = [_mm01(m_cum, x) for x in lw]
    tot = [_mm01(m_seq, x) for x in lw]
    e_c = [jnp.exp(x) for x in cum]
    e_n = [jnp.exp(-x) for x in cum]
    e_l = [jnp.exp(t - x) for t, x in zip(tot, cum)]
    at_d = [dup(ld(a_ref, c) * jnp.exp(cum[c] - lw[c])) for c in ids]
    rt_d = [dup(ld(r_ref, c) * e_c[c]) for c in ids]
    bt_d = [dup(ld(b_ref, c) * e_n[c]) for c in ids]
    kt_d = [dup(ld(k_ref, c) * e_n[c]) for c in ids]
    bh_d = [dup(ld(b_ref, c) * e_l[c]) for c in ids]
    kh_d = [dup(ld(k_ref, c) * e_l[c]) for c in ids]
    v_d = [dup(ld(v_ref, c)) for c in ids]
    aa = [_mm(cat0(at_d[c], rt_d[c]), cat0(bt_d[c], kt_d[c]), "nt", p_aa) for c in ids]
    a_ab = [jnp.where(strict_d, x[0:n2, 0:n2], 0.0) for x in aa]
    a_ak = [jnp.where(strict_d, x[0:n2, n2:], 0.0) for x in aa]
    a_rb = [jnp.where(incl_d, x[n2:, 0:n2], 0.0) for x in aa]
    a_rk = [jnp.where(incl_d, x[n2:, n2:], 0.0) for x in aa]
    zy = [_mm(cat0(a_ak[c], a_rk[c]), v_d[c], passes=p_apply) for c in ids]
    tinv = [eye_d + x for x in a_ab]
    nk = a_ab
    for _ in range(_log2(tl) - 1):
        nk = [_mm(x, x, passes=p_inv) for x in nk]
        tinv = [t + _mm(t, x, passes=p_inv) for t, x in zip(tinv, nk)]
    wu = [_mm(tinv[c], jnp.concatenate([at_d[c], zy[c][0:n2]], axis=1), passes=p_apply) for c in ids]
    seqs = range(nseq)
    srow = lambda c, q: (chains[c][0] * nseq + q, chains[c][1])
    s_old = [[st[srow(c, q)] for q in seqs] for c in ids]
    xs = [[_mm(cat0(seq_rows(wu[c][:, 0:LANES], q), seq_rows(rt_d[c], q)), s_old[c][q], "nt", p_state)
           for q in seqs] for c in ids]
    u_q = [[xs[c][q][0:2 * tl] + seq_rows(wu[c][:, LANES:], q) for q in seqs] for c in ids]
    for c in ids:
        for q in seqs:
            g_c = jnp.exp(tot[c][q * tl:q * tl + 1, :])
            st[srow(c, q)] = s_old[c][q] * g_c + _mm(cat0(u_q[c][q], seq_rows(v_d[c], q)),
                                                     cat0(seq_rows(bh_d[c], q), seq_rows(kh_d[c], q)), "tn", p_state)
    for c in ids:
        u, p = chains[c]
        y_d = (unit_rows([xs[c][q][2 * tl:] for q in seqs]) + _mm(a_rb[c], unit_rows(u_q[c]), passes=p_y)
               + zy[c][n2:])
        y_ref[u * nseq:(u + 1) * nseq, :, p * LANES:(p + 1) * LANES] = (y_d[0:n] + y_d[n:]).reshape(nseq, tl, LANES)

    @pl.when(pl.program_id(1) == pl.num_programs(1) - 1)
    def _():
        for q in range(nu * nseq):
            for p in range(RW_HEADS // 2):
                s = st[q, p]
                sn_ref[q, 2 * p] = s[0:hd, 0:hd]
                sn_ref[q, 2 * p + 1] = s[hd:, hd:]


def _unit_shape(bn, seq):
    if seq >= UNIT:
        assert seq % UNIT == 0
        return 1, UNIT
    assert UNIT % seq == 0 and bn % (UNIT // seq) == 0
    return UNIT // seq, seq


def _rwscan_call(r, lw, k2, v, a_s, b_s, s0, passes=RW_SCAN_PASSES):
    bn, seq, hw = r.shape
    nseq, tl = _unit_shape(bn, seq)
    nu = RW_UNITS_PER_STEP if bn % (RW_UNITS_PER_STEP * nseq) == 0 else 1
    rows = nu * nseq
    tok = pl.BlockSpec((rows, tl, hw), lambda b, c: (b, c, 0))
    stt = pl.BlockSpec((rows, RW_HEADS, RW_HEAD, RW_HEAD), lambda b, c: (b, 0, 0, 0))
    return pl.pallas_call(
        functools.partial(_rwscan_body, nu=nu, nseq=nseq, tl=tl, passes=passes),
        grid=(bn // rows, seq // tl),
        in_specs=[tok] * 6 + [stt],
        out_specs=[tok, stt],
        out_shape=[jax.ShapeDtypeStruct((bn, seq, hw), F32), jax.ShapeDtypeStruct(s0.shape, F32)],
        scratch_shapes=[pltpu.VMEM((rows, RW_HEADS // 2, LANES, LANES), F32)],
        compiler_params=_cparams(("arbitrary", "arbitrary"), VMEM_LIMIT),
        name="rwkv_scan",
    )(r, lw, k2, v, a_s, b_s, s0)


def _gla_body(qkv_ref, xal_ref, gate_ref, aup_ref, ab_ref, ng_ref, s0_ref, o_ref, sn_ref, st, *, nu, nseq, tl, cs):
    n = nseq * tl
    n2 = 2 * n
    nsub = tl // cs

    @pl.when(pl.program_id(1) == 0)
    def _():
        zero = jnp.zeros((GLA_DV, GLA_DK), F32)
        for q in range(nu * nseq):
            for p in range(GLA_HEADS // 2):
                st[q, p] = jnp.concatenate(
                    [jnp.concatenate([s0_ref[q, 2 * p].T, zero], axis=1),
                     jnp.concatenate([zero, s0_ref[q, 2 * p + 1].T], axis=1)], axis=0)

    same, _, incl = _unit_masks(n, cs)
    m_cum = jnp.where(incl, 1.0, 0.0)
    m_sub = jnp.where(same, 1.0, 0.0)
    ri = lax.broadcasted_iota(I32, (n2, n2), 0)
    ci = lax.broadcasted_iota(I32, (n2, n2), 1)
    rt, ct = ri & (n - 1), ci & (n - 1)
    causal_d = ((rt >> _log2(cs)) == (ct >> _log2(cs))) & ((ri >> _log2(n)) == (ci >> _log2(n))) & (rt >= ct)
    lane = lax.broadcasted_iota(I32, (1, LANES), 1)
    m0 = jnp.where(lane < GLA_DK, 1.0, 0.0)
    m1 = 1.0 - m0
    sr = lax.broadcasted_iota(I32, (2 * GLA_DV, LANES), 0)
    sc = lax.broadcasted_iota(I32, (2 * GLA_DV, LANES), 1)
    st_mask = jnp.where((sr >> _log2(GLA_DV)) == (sc >> _log2(GLA_DK)), 1.0, 0.0)

    def dup(x):
        return jnp.concatenate([x * m0, x * m1], axis=0)

    chains = [(u, p) for u in range(nu) for p in range(GLA_HEADS // 2)]
    ids = range(len(chains))
    urows = lambda u: slice(u * nseq, (u + 1) * nseq)
    ng = ng_ref[...]
    la_all = [-_softplus(-(_mm(xal_ref[urows(u), :, :].reshape(n, LANES), aup_ref[...], passes=3) + ab_ref[...]))
              * (1.0 / GLA_GATE_TAU) for u in range(nu)]

    def ld(ref, c, off, width):
        return ref[urows(chains[c][0]), :, off:off + width].reshape(n, width)

    q = [ld(qkv_ref, c, chains[c][1] * LANES, LANES) * (GLA_DK ** -0.5) for c in ids]
    k = [ld(qkv_ref, c, GLA_KW + chains[c][1] * LANES, LANES) for c in ids]
    vp = [ld(qkv_ref, c, 2 * GLA_KW + chains[c][1] * 2 * GLA_DV, 2 * GLA_DV) for c in ids]
    la = [la_all[u][:, p * LANES:(p + 1) * LANES] for u, p in chains]
    bc = [_mm01(m_cum, x) for x in la]
    bl = [_mm01(m_sub, x) for x in la]
    qe = [q[c] * jnp.exp(bc[c]) for c in ids]
    ke = [k[c] * jnp.exp(-bc[c]) for c in ids]
    kd = [k[c] * jnp.exp(bl[c] - bc[c]) for c in ids]
    att = [jnp.where(causal_d, _mm(dup(qe[c]), dup(ke[c]), "nt", passes=1), 0.0) for c in ids]
    v_st = [jnp.concatenate([x[:, 0:GLA_DV], x[:, GLA_DV:]], axis=0) for x in vp]
    o_st = [_mm(att[c], v_st[c], passes=1) for c in ids]
    upd = [[_mm(vp[c][r0:r0 + cs], kd[c][r0:r0 + cs], "tn", passes=1) for r0 in range(0, n, cs)] for c in ids]
    inter = [[None] * (n // cs) for _ in ids]
    for sq in range(nseq):
        s = [st[chains[c][0] * nseq + sq, chains[c][1]] for c in ids]
        for j in range(nsub):
            i = sq * nsub + j
            r0 = i * cs
            for c in ids:
                inter[c][i] = _mm(qe[c][r0:r0 + cs], s[c], "nt", passes=1)
                s[c] = s[c] * jnp.exp(bl[c][r0:r0 + 1, :]) + st_mask * upd[c][i]
        for c in ids:
            st[chains[c][0] * nseq + sq, chains[c][1]] = s[c]
    for c in ids:
        u, p = chains[c]
        o = o_st[c] + jnp.concatenate([x[:, 0:GLA_DV] for x in inter[c]] + [x[:, GLA_DV:] for x in inter[c]], axis=0)
        o = o * lax.rsqrt(jnp.mean(o * o, axis=-1, keepdims=True) + NORM_EPS) * ng
        goff = p * 2 * GLA_DV
        gp = ld(gate_ref, c, goff, 2 * GLA_DV)
        g_st = jnp.concatenate([gp[:, 0:GLA_DV], gp[:, GLA_DV:]], axis=0)
        ob = o * (g_st * _sigmoid(g_st))
        o_ref[urows(u), :, goff:goff + GLA_DV] = ob[0:n].reshape(nseq, tl, GLA_DV)
        o_ref[urows(u), :, goff + GLA_DV:goff + 2 * GLA_DV] = ob[n:].reshape(nseq, tl, GLA_DV)

    @pl.when(pl.program_id(1) == pl.num_programs(1) - 1)
    def _():
        for q in range(nu * nseq):
            for p in range(GLA_HEADS // 2):
                s = st[q, p]
                sn_ref[q, 2 * p] = s[0:GLA_DV, 0:GLA_DK].T
                sn_ref[q, 2 * p + 1] = s[GLA_DV:, GLA_DK:].T


def _gla_call(qkv, xal, gate, s0, p):
    bn, seq, _ = qkv.shape
    nseq, tl = _unit_shape(bn, seq)
    cs = min(GLA_CHUNK, seq)
    assert tl % cs == 0
    nu = GLA_UNITS_PER_STEP if bn % (GLA_UNITS_PER_STEP * nseq) == 0 else 1
    rows = nu * nseq
    tok = lambda w: pl.BlockSpec((rows, tl, w), lambda b, c: (b, c, 0))
    full = lambda a: pl.BlockSpec(a.shape, lambda b, c: (0,) * a.ndim)
    stt = pl.BlockSpec((rows, GLA_HEADS, GLA_DK, GLA_DV), lambda b, c: (b, 0, 0, 0))
    consts = (p["gla_aup"], p["gla_ab"], p["gla_ng"])
    return pl.pallas_call(
        functools.partial(_gla_body, nu=nu, nseq=nseq, tl=tl, cs=cs),
        grid=(bn // rows, seq // tl),
        in_specs=[tok(QKV_W), tok(XAL_W), tok(GG_W)] + [full(c) for c in consts] + [stt],
        out_specs=[tok(GLA_VW), stt],
        out_shape=[jax.ShapeDtypeStruct((bn, seq, GLA_VW), F32), jax.ShapeDtypeStruct(s0.shape, F32)],
        scratch_shapes=[pltpu.VMEM((rows, GLA_HEADS // 2, 2 * GLA_DV, LANES), F32)],
        compiler_params=_cparams(("arbitrary", "arbitrary"), VMEM_LIMIT),
        name="gla_chunked",
    )(qkv, xal, gate, *consts, s0)


def _merge_body(y_ref, g_ref, bon_ref, ob_ref, mg_ref, x_ref, mod_ref, gng_ref, gnb_ref, bd_ref, wpa_ref,
                wpb_ref, wout_ref, n2_ref, rwh_ref, rwl_ref, *rest):
    x1_o, h2_o, lg_o = rest[-3:]
    bb, ll, d = x_ref.shape
    n = bb * ll
    hw = RW_WIDTH
    bd = bd_ref[...]
    y = y_ref[...].reshape(n, hw)
    mu = _xmm01(y, bd, pieces=2) * (1.0 / RW_HEAD)
    dv = y - mu
    var = _xmm01(dv * dv, bd, pieces=1) * (1.0 / RW_HEAD)
    yn = dv * lax.rsqrt(var + RW_GN_EPS) * gng_ref[...] + gnb_ref[...]
    o_a = (yn + bon_ref[...].reshape(n, hw)) * g_ref[...].reshape(n, hw)
    o_b = ob_ref[...].reshape(n, GLA_VW)
    mg = mg_ref[...].reshape(n, 2 * d)
    merged = _sigmoid(mg[:, 0:d]) * _mm(o_a, wpa_ref[...]) + _sigmoid(mg[:, d:]) * _mm(o_b, wpb_ref[...])
    mix = _mm(merged, wout_ref[...]).reshape(bb, ll, d)
    x1 = x_ref[...] + mod_ref[:, 2:3, :] * mix
    x1_o[...] = x1
    yn2 = x1 * lax.rsqrt(jnp.mean(x1 * x1, axis=-1, keepdims=True) + NORM_EPS) * n2_ref[...]
    h2 = (yn2 * (1.0 + mod_ref[:, 4:5, :]) + mod_ref[:, 3:4, :]).reshape(n, d)
    hh, hl = _split(h2, 2)
    rwh, rwl = rwh_ref[...], rwl_ref[...]
    nt = lambda a, b: lax.dot_general(a, b, _DN["nt"], preferred_element_type=F32)
    lg_o[...] = nt(rwh, hh) + nt(rwl, hh) + nt(rwh, hl)
    _rows_to_packed(h2_o, h2)


def _merge_call(y, g, bonus, o_b, mg, x, mod, p, tn, first_tok, shared):
    bn, seq, d = x.shape
    bb, ll = _tile(bn, seq, TOK_TILE)
    nl = seq // ll
    assert first_tok % (bb * ll) == 0
    t0 = first_tok // (bb * ll)
    n_in = 7 + 9
    extra = [] if shared is None else list(shared)
    alias = {} if shared is None else {n_in: 1, n_in + 1: 2}
    tok = lambda w: pl.BlockSpec((bb, ll, w), lambda b, l: (b, l, 0))
    full = lambda a: pl.BlockSpec(a.shape, lambda b, l: (0,) * a.ndim)
    consts = (p["gn_g"], p["gn_b"], p["bd64"], p["w_pa"], p["w_pb"], p["w_out"], p["norm2_g"], p["rw_hi"],
              p["rw_lo"])
    return pl.pallas_call(
        _merge_body,
        grid=(bn // bb, nl),
        in_specs=[tok(RW_WIDTH)] * 3 + [tok(GLA_VW), tok(MG_W), tok(d),
                                        pl.BlockSpec((bb, 6, d), lambda b, l: (b, 0, 0))] + [full(c) for c in consts]
        + [pl.BlockSpec(memory_space=pl.ANY)] * len(extra),
        out_specs=[tok(d),
                   pl.BlockSpec((bb * ll * PCH, LANES), lambda b, l: (t0 + b * nl + l, 0)),
                   pl.BlockSpec((N_EXPERTS, bb * ll), lambda b, l: (0, t0 + b * nl + l))],
        out_shape=[jax.ShapeDtypeStruct((bn, seq, d), F32),
                   jax.ShapeDtypeStruct((tn * PCH, LANES), I32),
                   jax.ShapeDtypeStruct((N_EXPERTS, tn), F32)],
        input_output_aliases=alias,
        compiler_params=_cparams(("arbitrary", "arbitrary"), VMEM_LIMIT),
        name="merge_outproj_router",
    )(y, g, bonus, o_b, mg, x, mod, *consts, *extra)


def _route_body(lg_ref, rb_ref, e_o, rk_o, w_o, cnt_o, carry):
    ne, tm = lg_ref.shape

    @pl.when(pl.program_id(0) == 0)
    def _():
        carry[...] = jnp.zeros_like(carry)

    neg = -jnp.inf
    scores = _sigmoid(lg_ref[...])
    sel = scores + rb_ref[...]
    row_i = lax.broadcasted_iota(I32, (ne, tm), 0)
    row = row_i.astype(F32)
    grp = (row_i >> _log2(GROUP_SIZE)).astype(F32)

    def first_max(x, ids, none):
        m = jnp.max(x, axis=0, keepdims=True)
        return m, jnp.min(jnp.where(x == m, ids, none), axis=0, keepdims=True)

    gs = []
    gids = lax.broadcasted_iota(I32, (GROUP_SIZE, tm), 0)
    for gidx in range(N_GROUPS):
        rows = slice(gidx * GROUP_SIZE, (gidx + 1) * GROUP_SIZE)
        sg = _sigmoid(lg_ref[rows, :]) + rb_ref[rows, :]
        ids = (gids + gidx * GROUP_SIZE).astype(F32)
        m1, i1 = first_max(sg, ids, float(ne))
        gs.append(m1 + jnp.max(jnp.where(ids == i1, neg, sg), axis=0, keepdims=True))
    gs = jnp.concatenate(gs, axis=0)
    gid = lax.broadcasted_iota(I32, (N_GROUPS, tm), 0).astype(F32)
    cur = jnp.full((ne, tm), neg, F32)
    for _ in range(TOPK_GROUPS):
        _, gi = first_max(gs, gid, float(N_GROUPS))
        cur = jnp.where(grp == gi, sel, cur)
        gs = jnp.where(gid == gi, neg, gs)

    pm = jnp.zeros((ne, tm), F32)
    eidx, wts = [], []
    for _ in range(TOP_K):
        _, ei = first_max(cur, row, float(ne))
        hit = row == ei
        pm = jnp.where(hit, 1.0, pm)
        eidx.append(ei)
        wts.append(jnp.sum(jnp.where(hit, scores, 0.0), axis=0, keepdims=True))
        cur = jnp.where(hit, neg, cur)
    wsum = wts[0]
    for w in wts[1:]:
        wsum = wsum + w

    ri = lax.broadcasted_iota(I32, (tm, tm), 0)
    ci = lax.broadcasted_iota(I32, (tm, tm), 1)
    earlier = jnp.where(ri < ci, 1.0, 0.0)
    rank = _mm(pm, earlier, passes=1) + carry[...]
    carry[...] = carry[...] + jnp.sum(pm, axis=1, keepdims=True)
    cnt_o[...] = carry[...]

    rks = [jnp.sum(jnp.where(row == e, rank, 0.0), axis=0, keepdims=True) for e in eidx]
    e_o[0] = jnp.concatenate(eidx, axis=0).astype(I32)
    rk_o[0] = jnp.concatenate(rks, axis=0).astype(I32)
    w_o[0] = jnp.concatenate([w / wsum * ROUTED_SCALE for w in wts], axis=0)


def _route_call(logits_t, router_b):
    ne, tn = logits_t.shape
    tm = TOK_TILE
    assert tn % tm == 0
    col = pl.BlockSpec((ne, 1), lambda i: (0, 0))
    tab = pl.BlockSpec((1, TOP_K, tm), lambda i: (i, 0, 0))
    tab_shape = (tn // tm, TOP_K, tm)
    return pl.pallas_call(
        _route_body,
        grid=(tn // tm,),
        in_specs=[pl.BlockSpec((ne, tm), lambda i: (0, i)), col],
        out_specs=[tab, tab, tab, col],
        out_shape=[jax.ShapeDtypeStruct(tab_shape, I32), jax.ShapeDtypeStruct(tab_shape, I32),
                   jax.ShapeDtypeStruct(tab_shape, F32), jax.ShapeDtypeStruct((ne, 1), F32)],
        scratch_shapes=[pltpu.VMEM((ne, 1), F32)],
        compiler_params=_cparams(("arbitrary",)),
        name="moe_route",
    )(logits_t, router_b.reshape(ne, 1))


def _dest_body(e_ref, rk_ref, ps_ref, d_o):
    ne, tm = ps_ref.shape[0], e_ref.shape[2]
    ids = lax.broadcasted_iota(I32, (ne, tm), 0)
    ps = ps_ref[...]
    for t in range(e_ref.shape[0]):
        first = [jnp.sum(jnp.where(ids == e_ref[t, kk:kk + 1, :], ps, 0.0), axis=0, keepdims=True)
                 for kk in range(TOP_K)]
        d_o[t] = (jnp.concatenate(first, axis=0).astype(I32) + rk_ref[t]) * PCH


def _dest_call(eidx, rank, pad_start):
    nt, _, tm = eidx.shape
    ne = pad_start.shape[0]
    per = next(k for k in (4, 2, 1) if nt % k == 0)
    tab = pl.BlockSpec((per, TOP_K, tm), lambda i: (i, 0, 0))
    return pl.pallas_call(
        _dest_body,
        grid=(nt // per,),
        in_specs=[tab, tab, pl.BlockSpec((ne, 1), lambda i: (0, 0))],
        out_specs=tab,
        out_shape=jax.ShapeDtypeStruct(eidx.shape, I32),
        compiler_params=_cparams(("arbitrary",)),
        name="moe_dest",
    )(eidx, rank, pad_start.astype(F32).reshape(ne, 1))


def _pslab(ref, offset):
    return ref.at[pl.ds(pl.multiple_of(offset, PCH), PCH)]


def _dispatch_body(d_ref, h2_ref, xs_hbm, sem, *, tm):
    def issue(m, carry):
        for kk in range(TOP_K):
            pltpu.make_async_copy(_slab(h2_ref, m), _pslab(xs_hbm, d_ref[0, kk, m]), sem).start(priority=kk % 2)
        return carry

    lax.fori_loop(0, tm, issue, 0)
    all_rows = xs_hbm.at[pl.ds(0, tm * TOP_K * PCH)]
    pltpu.make_async_copy(all_rows, all_rows, sem).wait()


def _assign_spec(tm, index_map):
    return pl.BlockSpec((1, TOP_K, tm), index_map, memory_space=pltpu.SMEM)


def _dispatch_call(dest, h2s, n_rows):
    tn = h2s.shape[0] // PCH
    tm = TOK_TILE
    assert dest.shape == (tn // tm, TOP_K, tm)
    blk = _assign_spec(tm, lambda i: (i, 0, 0))
    return pl.pallas_call(
        functools.partial(_dispatch_body, tm=tm),
        grid=(tn // tm,),
        in_specs=[blk, pl.BlockSpec((tm * PCH, LANES), lambda i: (i, 0))],
        out_specs=pl.BlockSpec(memory_space=pl.ANY),
        out_shape=jax.ShapeDtypeStruct((n_rows * PCH, LANES), I32),
        scratch_shapes=[pltpu.SemaphoreType.DMA],
        compiler_params=_cparams(("arbitrary",)),
        name="moe_dispatch",
    )(dest, h2s)


def _expert_body(bi_ref, nr_ref, ld_ref, nx_ref, xs_hbm, wg_hbm, wu_hbm, wd_hbm, ob_ref, wg_buf, wu_buf, wd_buf,
                 wg_bf, wu_bf, wd_bf, xbuf, sem, xsem):
    i = pl.program_id(0)
    nsteps = pl.num_programs(0)
    nr = nr_ref[i]
    slot = ld_ref[i]
    blk_rows = MOE_BLK * PCH

    def row_block(j):
        s = lax.rem(j, ROW_SLOTS)
        src = xs_hbm.at[pl.ds(pl.multiple_of(bi_ref[j] * blk_rows, blk_rows), blk_rows)]
        return pltpu.make_async_copy(src, xbuf.at[s], xsem.at[s])

    @pl.when(i == 0)
    def _():
        for j in range(ROW_SLOTS - 1):
            row_block(j).start()

    @pl.when(i + ROW_SLOTS - 1 < nsteps)
    def _():
        row_block(i + ROW_SLOTS - 1).start()

    def fetch(e, s):
        return (pltpu.make_async_copy(wg_hbm.at[e], wg_buf.at[s], sem.at[s]),
                pltpu.make_async_copy(wu_hbm.at[e], wu_buf.at[s], sem.at[s]),
                pltpu.make_async_copy(wd_hbm.at[e], wd_buf.at[s], sem.at[s]))

    @pl.when(i == 0)
    def _():
        for s in range(WEIGHT_SLOTS - 1):
            e0 = nx_ref[nx_ref.shape[0] - (WEIGHT_SLOTS - 1) + s]

            @pl.when(e0 >= 0)
            def _():
                for k, cp in enumerate(fetch(e0, s)):
                    cp.start(priority=k % 2)

    @pl.when(slot >= 0)
    def _():
        for cp in fetch(0, slot):
            cp.wait()

        @pl.when(nx_ref[i] >= 0)
        def _():
            for k, cp in enumerate(fetch(nx_ref[i], lax.rem(slot + WEIGHT_SLOTS - 1, WEIGHT_SLOTS))):
                cp.start(priority=k % 2)

        wg_bf[...] = wg_buf[slot].astype(BF16)
        wu_bf[...] = wu_buf[slot].astype(BF16)
        wd_bf[...] = wd_buf[slot].astype(BF16)

    row_block(i).wait()

    @pl.when(nr > 0)
    def _():
        part = MOE_BLK // EXPERT_PARTS
        firsts = [q * part for q in range(EXPERT_PARTS)]
        rid = lax.broadcasted_iota(I32, (part, LANES), 0)
        xs_ref = xbuf.at[lax.rem(i, ROW_SLOTS)]
        x = [_rows_from_packed(xs_ref, part, rid < nr - f, f) for f in firsts]
        hg = [jnp.dot(v, wg_bf[...], preferred_element_type=F32) for v in x]
        hu = [jnp.dot(v, wu_bf[...], preferred_element_type=F32) for v in x]
        hh = [(g * _sigmoid(g) * u).astype(BF16) for g, u in zip(hg, hu)]
        out = [jnp.dot(v, wd_bf[...], preferred_element_type=F32) for v in hh]
        for f, v in zip(firsts, out):
            _rows_to_packed(ob_ref, v, f)


def _expert_tables(counts, pad_start, pad_end, nb):
    ne = counts.shape[0]
    experts = jnp.arange(ne, dtype=I32)
    first_row = jnp.arange(nb, dtype=I32) * MOE_BLK
    block_e = jnp.minimum(jnp.sum(pad_end[None, :] <= first_row[:, None], axis=1), ne - 1).astype(I32)
    mine = block_e[:, None] == experts[None, :]
    pick = lambda v: jnp.sum(jnp.where(mine, v[None, :], 0), axis=1)
    has = counts > 0
    ordinal = jnp.cumsum(has.astype(I32)) - 1
    start_b, count_b, ord_b = pick(pad_start), pick(counts), pick(ordinal)
    block_rows = jnp.clip(start_b + count_b - first_row, 0, MOE_BLK).astype(I32)
    block_i = jnp.minimum(jnp.arange(nb, dtype=I32), pad_end[-1] // MOE_BLK - 1).astype(I32)
    starts = (first_row == start_b) & (block_rows > 0)
    load_slot = jnp.where(starts, ord_b % WEIGHT_SLOTS, -1).astype(I32)
    nth = lambda want: jnp.max(jnp.where(has[None, :] & (ordinal[None, :] == want[:, None]), experts[None, :], -1),
                               axis=1)
    ahead = jnp.where(starts, nth(ord_b + WEIGHT_SLOTS - 1), -1)
    lead = nth(jnp.arange(WEIGHT_SLOTS - 1, dtype=I32))
    return block_i, block_rows, load_slot, jnp.concatenate([ahead, lead]).astype(I32)


def _expert_call(tables, xs, wg, wu, wd):
    nb = xs.shape[0] // (MOE_BLK * PCH)
    assert nb >= ROW_SLOTS
    d, ff = wg.shape[1], wg.shape[2]
    rows = pl.BlockSpec((MOE_BLK * PCH, LANES), lambda i, bi, nr, ld, nx: (bi[i], 0))
    hbm = pl.BlockSpec(memory_space=pl.ANY)
    grid_spec = pltpu.PrefetchScalarGridSpec(
        num_scalar_prefetch=4,
        grid=(nb,),
        in_specs=[hbm, hbm, hbm, hbm],
        out_specs=rows,
        scratch_shapes=[pltpu.VMEM((WEIGHT_SLOTS, d, ff), F32), pltpu.VMEM((WEIGHT_SLOTS, d, ff), F32),
                        pltpu.VMEM((WEIGHT_SLOTS, ff, d), F32),
                        pltpu.VMEM((d, ff), BF16), pltpu.VMEM((d, ff), BF16), pltpu.VMEM((ff, d), BF16),
                        pltpu.VMEM((ROW_SLOTS, MOE_BLK * PCH, LANES), I32),
                        pltpu.SemaphoreType.DMA((WEIGHT_SLOTS,)), pltpu.SemaphoreType.DMA((ROW_SLOTS,))],
    )
    return pl.pallas_call(
        _expert_body,
        grid_spec=grid_spec,
        out_shape=jax.ShapeDtypeStruct(xs.shape, I32),
        compiler_params=_cparams(("arbitrary",), VMEM_LIMIT),
        name="moe_experts",
    )(*tables, xs, wg, wu, wd)


def _combine_body(d_ref, dn_ref, wt_ref, ob_hbm, h2_ref, x1_ref, mod_ref, sg_ref, su_ref,
                  sd_ref, fg_ref, out_ref, gbuf, rbuf, wcol, sem, *, tm, nl):
    bb, ll, d = x1_ref.shape
    step = pl.program_id(0) * nl + pl.program_id(1)
    last = pl.num_programs(0) * nl - 1
    parity = lax.rem(step, 2)
    grp = CMB_GROUP

    def request(d_tab, g, s):
        for j in range(grp):
            m = g * grp + j
            for kk in range(TOP_K):
                pltpu.make_async_copy(_pslab(ob_hbm, d_tab[0, kk, m]), _slab(gbuf.at[s], kk * tm + m),
                                      sem.at[s]).start(priority=kk % 2)

    def mix(g, s):
        r0 = pl.multiple_of(g * grp, grp)
        w = wcol[pl.ds(r0, grp), :]
        wk = [w[:, kk:kk + 1] for kk in range(TOP_K)]
        for c in range(PCH):
            acc_lo = acc_hi = None
            for kk in range(TOP_K):
                words = gbuf[s, pl.ds((kk * tm + r0) * PCH + c, grp, stride=PCH), :]
                lo, hi = _unpack_pair(words)
                acc_lo = wk[kk] * lo if acc_lo is None else acc_lo + wk[kk] * lo
                acc_hi = wk[kk] * hi if acc_hi is None else acc_hi + wk[kk] * hi
            rbuf[pl.ds(r0, grp), c * LANES:(c + 1) * LANES] = acc_lo
            rbuf[pl.ds(r0, grp), (c + PCH) * LANES:(c + PCH + 1) * LANES] = acc_hi

    @pl.when(step == 0)
    def _():
        def first(g, carry):
            request(d_ref, g, 0)
            return carry
        lax.fori_loop(0, tm // grp, first, 0)

    ri = lax.broadcasted_iota(I32, (tm, tm), 0)
    ci = lax.broadcasted_iota(I32, (tm, tm), 1)
    eye = jnp.where(ri == ci, 1.0, 0.0).astype(BF16)
    wc = None
    for piece in _split(wt_ref[0], 3):
        t = lax.dot_general(eye, piece, _DN["nt"], preferred_element_type=F32)
        wc = t if wc is None else wc + t
    wcol[...] = wc

    def run(slot):
        pltpu.make_async_copy(ob_hbm.at[pl.ds(0, tm * TOP_K * PCH)], gbuf.at[slot], sem.at[slot]).wait()

        @pl.when(step < last)
        def _():
            def both(g, carry):
                request(dn_ref, g, 1 - slot)
                mix(g, slot)
                return carry
            lax.fori_loop(0, tm // grp, both, 0)

        @pl.when(step == last)
        def _():
            def only(g, carry):
                mix(g, slot)
                return carry
            lax.fori_loop(0, tm // grp, only, 0)

    for slot in range(2):
        pl.when(parity == slot)(functools.partial(run, slot))

    routed = rbuf[...]
    h2 = _rows_from_packed(h2_ref, tm)
    hg = jnp.dot(h2, sg_ref[...], preferred_element_type=F32)
    hu = jnp.dot(h2, su_ref[...], preferred_element_type=F32)
    shared = jnp.dot((hg * _sigmoid(hg) * hu).astype(BF16), sd_ref[...], preferred_element_type=F32)
    ff = (routed + shared).reshape(bb, ll, d)
    x2 = x1_ref[...] + mod_ref[:, 5:6, :] * ff
    out_ref[...] = x2 * lax.rsqrt(jnp.mean(x2 * x2, axis=-1, keepdims=True) + NORM_EPS) * fg_ref[...]


def _combine_call(dest, wts, first_tok, ob, h2s, x1, mod, p):
    bn, seq, d = x1.shape
    tm = CMB_TILE
    bb, ll = _tile(bn, seq, tm)
    nl = seq // ll
    tn = bn * seq
    nsteps = tn // tm
    per = dest.shape[2] // tm
    assert first_tok % tm == 0 and dest.shape[2] % tm == 0
    tile = lambda g: ((first_tok // tm + g) // per, 0, (first_tok // tm + g) % per)
    smem = _assign_spec(tm, lambda b, l: tile(b * nl + l))
    smem_next = _assign_spec(tm, lambda b, l: tile(jnp.minimum(b * nl + l + 1, nsteps - 1)))
    wblk = pl.BlockSpec((1, TOP_K, tm), lambda b, l: tile(b * nl + l))
    tok = pl.BlockSpec((bb, ll, d), lambda b, l: (b, l, 0))
    full = lambda a: pl.BlockSpec(a.shape, lambda b, l: (0,) * a.ndim)
    consts = (p["sh_gate"], p["sh_up"], p["sh_down"], p["final_g"])
    return pl.pallas_call(
        functools.partial(_combine_body, tm=tm, nl=nl),
        grid=(bn // bb, nl),
        in_specs=[smem, smem_next, wblk, pl.BlockSpec(memory_space=pl.ANY),
                  pl.BlockSpec((tm * PCH, LANES), lambda b, l: (first_tok // tm + b * nl + l, 0)),
                  tok, pl.BlockSpec((bb, 6, d), lambda b, l: (b, 0, 0))] + [full(c) for c in consts],
        out_specs=tok,
        out_shape=jax.ShapeDtypeStruct((bn, seq, d), F32),
        scratch_shapes=[pltpu.VMEM((2, tm * TOP_K * PCH, LANES), I32), pltpu.VMEM((tm, d), F32),
                        pltpu.VMEM((tm, TOP_K), F32), pltpu.SemaphoreType.DMA((2,))],
        compiler_params=_cparams(("arbitrary", "arbitrary"), VMEM_LIMIT),
        name="moe_combine_final",
    )(dest, dest, wts, ob, h2s, x1, mod, *consts)


def _layer_params(l, ada_w, ada_b, norm1_g, norm2_g, w_in, mu_shift, rw_w0, rw_w_up, rw_a0, rw_a_up, rw_g_up,
                  rw_k_k, rw_k_a, rw_r_k, rw_gn_g, rw_gn_b, gla_a_up, gla_a_bias, gla_norm_g, w_pa, w_pb, w_out,
                  router_w, router_b, exp_gate, exp_up, exp_down, sh_gate, sh_up, sh_down):
    d = D_MODEL
    wi = w_in[l]
    gla0 = RW_SHIFT_COLS
    xal0 = gla0 + QKV_W
    pad = jnp.zeros((d, XAL_W - GLA_GATE_RANK), BF16)
    w_pieces = (wi[:, :xal0].astype(BF16),
                jnp.concatenate([wi[:, xal0:xal0 + GLA_GATE_RANK].astype(BF16), pad], axis=1),
                wi[:, xal0 + GLA_GATE_RANK:].astype(BF16))
    zr = jnp.zeros((RW_W_RANK, RW_WIDTH), F32)
    hid = jnp.arange(RW_WIDTH) // RW_HEAD
    row = lambda a: a.reshape(1, -1)
    rw_t = router_w[l].T
    rw_hi = rw_t.astype(BF16)
    return dict(
        ada_w=ada_w[l], ada_b=ada_b[l], norm1_g=norm1_g[l].reshape(1, 1, d),
        norm2_g=norm2_g[l].reshape(1, 1, d), w_in=w_pieces,
        mu=mu_shift[l].reshape(1, 1, -1), w0=row(rw_w0[l]), wup=jnp.concatenate([rw_w_up[l], zr], axis=0),
        a0=row(rw_a0[l]), aup=jnp.concatenate([zr, rw_a_up[l]], axis=0), gup=rw_g_up[l].astype(BF16),
        kk=row(rw_k_k[l]), ka=row(rw_k_a[l]), rk=row(rw_r_k[l]),
        bd64=(hid[:, None] == hid[None, :]).astype(BF16),
        gn_g=row(rw_gn_g[l]), gn_b=row(rw_gn_b[l]),
        gla_aup=jnp.concatenate([gla_a_up[l], jnp.zeros((XAL_W - GLA_GATE_RANK, GLA_KW), F32)], axis=0),
        gla_ab=row(gla_a_bias[l]), gla_ng=row(gla_norm_g[l]),
        w_pa=w_pa[l].astype(BF16), w_pb=w_pb[l].astype(BF16), w_out=w_out[l].astype(BF16),
        rw_hi=rw_hi, rw_lo=(rw_t - rw_hi.astype(F32)).astype(BF16), router_b=router_b[l],
        exp_gate=exp_gate[l], exp_up=exp_up[l], exp_down=exp_down[l],
        sh_gate=sh_gate[l].astype(BF16), sh_up=sh_up[l].astype(BF16), sh_down=sh_down[l].astype(BF16),
    )


def _mixer_group(x, mod, s_rw, s_sh, s_gla, p, tn, first_tok, shared):
    qkv, xal, gg, mg, r, lw, k2, v, a_s, b_s, g, bonus, new_sh = _inproj_prep_call(x, mod, s_sh, p)
    y, rw_new = _rwscan_call(r, lw, k2, v, a_s, b_s, s_rw)
    o_b, gla_new = _gla_call(qkv, xal, gg, s_gla, p)
    x1, h2s, logits = _merge_call(y, g, bonus, o_b, mg, x, mod, p, tn, first_tok, shared)
    states = (rw_new, new_sh[:, 0, :], gla_new)
    return x1, h2s, logits, states


def _moe(h2s, logits, p):
    tn = h2s.shape[0] // PCH
    eidx, rank, wts, counts = _route_call(logits, p["router_b"])
    counts = counts[:, 0].astype(I32)
    padded = (counts + MOE_BLK - 1) // MOE_BLK * MOE_BLK
    pad_end = jnp.cumsum(padded)
    pad_start = (pad_end - padded).astype(I32)
    nb = (tn * TOP_K + N_EXPERTS * (MOE_BLK - 1)) // MOE_BLK + 1
    tables = _expert_tables(counts, pad_start, pad_end, nb)
    dest = _dest_call(eidx, rank, pad_start)
    xs = _dispatch_call(dest, h2s, nb * MOE_BLK)
    ob = _expert_call(tables, xs, p["exp_gate"], p["exp_up"], p["exp_down"])
    return ob, dest, wts


def kernel(x_prompt, x_sample, c_prompt, c_sample, state_rwkv, state_shift, state_gla, ada_w, ada_b, norm1_g,
           norm2_g, w_in, mu_shift, rw_w0, rw_w_up, rw_a0, rw_a_up, rw_g_up, rw_k_k, rw_k_a, rw_r_k, rw_gn_g,
           rw_gn_b, gla_a_up, gla_a_bias, gla_norm_g, w_pa, w_pb, w_out, router_w, router_b, exp_gate, exp_up,
           exp_down, sh_gate, sh_up, sh_down, final_g):
    depth = ada_w.shape[0]
    bp, bs = x_prompt.shape[0], x_sample.shape[0]
    tp = bp * x_prompt.shape[1]
    tn = tp + bs * x_sample.shape[1]
    xs_g = [x_prompt, x_sample]
    c_all = jnp.concatenate([c_prompt, c_sample], axis=0)
    zeros = lambda shape: jnp.zeros(shape, x_prompt.dtype)
    new_states = [[], []]
    fg = final_g.reshape(1, 1, D_MODEL)
    for l in range(depth):
        p = _layer_params(l, ada_w, ada_b, norm1_g, norm2_g, w_in, mu_shift, rw_w0, rw_w_up, rw_a0, rw_a_up,
                          rw_g_up, rw_k_k, rw_k_a, rw_r_k, rw_gn_g, rw_gn_b, gla_a_up, gla_a_bias, gla_norm_g,
                          w_pa, w_pb, w_out, router_w, router_b, exp_gate, exp_up, exp_down, sh_gate, sh_up,
                          sh_down)
        p["final_g"] = fg
        mod_all = _mod_call(c_all, p["ada_w"], p["ada_b"])
        mods = [mod_all[:bp], mod_all[bp:]]
        states_in = [
            (zeros((bp, RW_HEADS, RW_HEAD, RW_HEAD)), zeros((bp, RW_SHIFT_COLS)),
             zeros((bp, GLA_HEADS, GLA_DK, GLA_DV))),
            (state_rwkv[l], state_shift[l], state_gla[l]),
        ]
        x1s, shared = [], None
        firsts = [0, tp]
        for gi in range(2):
            x1, h2_all, lg_all, st = _mixer_group(xs_g[gi], mods[gi], *states_in[gi], p, tn, firsts[gi], shared)
            shared = (h2_all, lg_all)
            x1s.append(x1)
            new_states[gi].append(st)
        ob, dest, wts = _moe(*shared, p)
        assert depth == 1, "the fused final norm assumes a single layer"
        xs_g = [_combine_call(dest, wts, firsts[gi], ob, shared[0], x1s[gi], mods[gi], p) for gi in range(2)]
    stack = lambda gi, j: new_states[gi][0][j][None] if depth == 1 else jnp.stack([s[j] for s in new_states[gi]])
    return (xs_g[0], xs_g[1], stack(0, 0), stack(0, 1), stack(0, 2), stack(1, 0), stack(1, 1), stack(1, 2))
```

```python
import functools

import jax
import jax.numpy as jnp
from jax import lax
from jax.experimental import pallas as pl
from jax.experimental.pallas import tpu as pltpu

F32, BF16, I32 = jnp.float32, jnp.bfloat16, jnp.int32

D_MODEL = 1024
RW_HEADS, RW_HEAD = 8, 64
RW_WIDTH = RW_HEADS * RW_HEAD
RW_W_RANK, RW_A_RANK, RW_G_RANK = 64, 64, 128
RW_GN_EPS = 64e-5
GLA_HEADS, GLA_DK, GLA_DV = 4, 64, 128
GLA_KW, GLA_VW = GLA_HEADS * GLA_DK, GLA_HEADS * GLA_DV
GLA_GATE_RANK = 16
GLA_GATE_TAU = 16.0
GLA_CHUNK = 16
RW_SHIFT_COLS = 3 * RW_WIDTH + RW_W_RANK + RW_A_RANK + RW_G_RANK
N_EXPERTS, TOP_K, N_GROUPS, TOPK_GROUPS = 256, 8, 8, 4
GROUP_SIZE = N_EXPERTS // N_GROUPS
EXPERT_FF = 256
ROUTED_SCALE = 2.5
NORM_EPS = 1e-6

LANES = 128
CHUNKS = D_MODEL // LANES
PCH = CHUNKS // 2
UNIT = 64
RW_SCAN_PASSES = (1, 1, 1, 1, 1)
GLA_UNITS_PER_STEP = 4
RW_UNITS_PER_STEP = 4
VMEM_LIMIT = 56 * 1024 * 1024

PA_W, QKV_W, XAL_W, GG_W, MG_W = RW_SHIFT_COLS, 2 * GLA_KW + GLA_VW, LANES, GLA_VW, 2 * D_MODEL

TOK_TILE = 256
MOE_BLK = 640
EXPERT_PARTS = 1
WEIGHT_SLOTS = 5
ROW_SLOTS = 3
CMB_TILE = 256
CMB_GROUP = 16

_DN = {
    "nn": (((1,), (0,)), ((), ())),
    "nt": (((1,), (1,)), ((), ())),
    "tn": (((0,), (0,)), ((), ())),
}


def _split(x, pieces):
    out, rem = [], x
    for i in range(pieces):
        p = rem.astype(BF16)
        out.append(p)
        if i + 1 < pieces:
            rem = rem - p.astype(F32)
    return out


def _mm(a, b, form="nn", passes=1):
    dn = _DN[form]
    if passes == 6:
        return lax.dot_general(a.astype(F32), b.astype(F32), dn, precision=lax.Precision.HIGHEST,
                               preferred_element_type=F32)
    if passes == 1:
        return lax.dot_general(a.astype(BF16), b.astype(BF16), dn, preferred_element_type=F32)
    ah, al = _split(a, 2)
    bh, bl = _split(b, 2)
    out = lax.dot_general(ah, bh, dn, preferred_element_type=F32)
    out = out + lax.dot_general(ah, bl, dn, preferred_element_type=F32)
    return out + lax.dot_general(al, bh, dn, preferred_element_type=F32)


def _mm01(m01, x, pieces=3):
    m = m01.astype(BF16)
    out = None
    for p in _split(x, pieces):
        t = lax.dot_general(m, p, _DN["nn"], preferred_element_type=F32)
        out = t if out is None else out + t
    return out


def _xmm01(x, m01, pieces=2):
    m = m01.astype(BF16)
    out = None
    for p in _split(x, pieces):
        t = lax.dot_general(p, m, _DN["nn"], preferred_element_type=F32)
        out = t if out is None else out + t
    return out


HI16 = -65536


def _bf16_bits(x):
    return lax.bitcast_convert_type(x.astype(BF16).astype(F32), I32)


def _unpack_pair(w):
    return lax.bitcast_convert_type(w << 16, F32), lax.bitcast_convert_type(w & HI16, F32)


def _rows_to_packed(ref, x, first=0):
    for c in range(PCH):
        lo = _bf16_bits(x[:, c * LANES:(c + 1) * LANES])
        hi = _bf16_bits(x[:, (c + PCH) * LANES:(c + PCH + 1) * LANES])
        ref[pl.ds(first * PCH + c, x.shape[0], stride=PCH), :] = ((lo >> 16) & 0xFFFF) | (hi & HI16)


def _rows_from_packed(ref, n, live=None, first=0):
    lows, highs = [], []
    for c in range(PCH):
        w = ref[pl.ds(first * PCH + c, n, stride=PCH), :]
        if live is not None:
            w = jnp.where(live, w, 0)
        lo, hi = _unpack_pair(w)
        lows.append(lo.astype(BF16))
        highs.append(hi.astype(BF16))
    return jnp.concatenate(lows + highs, axis=1)


def _slab(ref, row):
    return ref.at[pl.ds(pl.multiple_of(row * PCH, PCH), PCH)]


def _sigmoid(x):
    return 1.0 / (1.0 + jnp.exp(-x))


def _softplus(x):
    return jnp.maximum(x, 0.0) + jnp.log(1.0 + jnp.exp(-jnp.abs(x)))


def _log2(n):
    assert n > 0 and n & (n - 1) == 0, n
    return n.bit_length() - 1


def _cparams(sem, vmem=None):
    return pltpu.CompilerParams(dimension_semantics=sem, vmem_limit_bytes=vmem)


def _mod_body(c_ref, w_ref, b_ref, o_ref):
    c = c_ref[...]
    o_ref[0] = _mm(c * _sigmoid(c), w_ref[...], passes=3) + b_ref[...]


def _mod_call(c_all, ada_w, ada_b):
    bt, d = c_all.shape
    out = pl.pallas_call(
        _mod_body,
        grid=(6,),
        in_specs=[pl.BlockSpec((bt, d), lambda k: (0, 0)),
                  pl.BlockSpec((d, d), lambda k: (0, k)),
                  pl.BlockSpec((1, d), lambda k: (0, k))],
        out_specs=pl.BlockSpec((1, bt, d), lambda k: (k, 0, 0)),
        out_shape=jax.ShapeDtypeStruct((6, bt, d), F32),
        compiler_params=_cparams(("arbitrary",)),
        name="adaln_mod",
    )(c_all, ada_w, ada_b.reshape(1, 6 * d))
    return jnp.transpose(out, (1, 0, 2))


def _inproj_body(x_ref, mod_ref, g_ref, wa_ref, wx_ref, wb_ref, pa_ref, qkv_ref, xal_ref, gg_ref, mg_ref):
    bb, ll, d = x_ref.shape
    x = x_ref[...]
    y = x * lax.rsqrt(jnp.mean(x * x, axis=-1, keepdims=True) + NORM_EPS) * g_ref[...]
    h = y * (1.0 + mod_ref[:, 1:2, :]) + mod_ref[:, 0:1, :]
    hb = h.reshape(bb * ll, d).astype(BF16)
    for w_ref, outs in ((wa_ref, (pa_ref, qkv_ref)), (wx_ref, (xal_ref,)), (wb_ref, (gg_ref, mg_ref))):
        off = 0
        for ref in outs:
            w = ref.shape[-1]
            ref[...] = jnp.dot(hb, w_ref[:, off:off + w], preferred_element_type=F32).reshape(bb, ll, w)
            off += w


def _tile(bn, seq, tile):
    if seq >= tile:
        assert seq % tile == 0
        return 1, tile
    assert tile % seq == 0 and bn % (tile // seq) == 0
    return tile // seq, seq


def _rwprep_body(pa_ref, sh_ref, mu_ref, w0_ref, wup_ref, a0_ref, aup_ref, gup_ref, kk_ref, ka_ref, rk_ref,
                 bd_ref, r_o, lw_o, k_o, v_o, a_o, b_o, g_o, bon_o, nsh_o, carry):
    bb, ll, wd = pa_ref.shape
    n = bb * ll
    hw = RW_WIDTH

    @pl.when(pl.program_id(1) == 0)
    def _():
        carry[...] = sh_ref[...]

    pa = pa_ref[...]
    rolled = pltpu.roll(pa.reshape(n, wd), 1, 0).reshape(bb, ll, wd)
    tok = lax.broadcasted_iota(I32, (bb, ll, wd), 1)
    prev = jnp.where(tok == 0, carry[...], rolled)
    last = pa_ref[:, ll - 1:ll, :]
    carry[...] = last
    nsh_o[...] = last
    xs = (pa + (prev - pa) * mu_ref[...]).reshape(n, wd)

    r, k, v = xs[:, 0:hw], xs[:, hw:2 * hw], xs[:, 2 * hw:3 * hw]
    xwa = xs[:, 3 * hw:3 * hw + LANES]
    xg = xs[:, 3 * hw + LANES:]
    w_log = -_softplus(-(w0_ref[...] + _mm(jnp.tanh(xwa), wup_ref[...], passes=3))) - 0.5
    lw = -jnp.exp(w_log)
    a = _sigmoid(a0_ref[...] + _mm(xwa, aup_ref[...], passes=3))
    g = _mm(_sigmoid(xg), gup_ref[...])
    bd = bd_ref[...]
    kkv = k * kk_ref[...]
    kkn = kkv * lax.rsqrt(jnp.maximum(_xmm01(kkv * kkv, bd, pieces=1), 1e-24))
    k2 = k * (1.0 + (a - 1.0) * ka_ref[...])
    bonus = _xmm01(r * k2 * rk_ref[...], bd, pieces=1) * v
    for ref, val in ((r_o, r), (lw_o, lw), (k_o, k2), (v_o, v), (a_o, -kkn), (b_o, kkn * a), (g_o, g),
                     (bon_o, bonus)):
        ref[...] = val.reshape(bb, ll, hw)


def _inproj_prep_body(x_ref, mod_ref, g_ref, wa_ref, wx_ref, wb_ref, sh_ref, mu_ref, w0_ref, wup_ref, a0_ref, aup_ref,
                      gup_ref, kk_ref, ka_ref, rk_ref, bd_ref, qkv_o, xal_o, gg_o, mg_o, r_o, lw_o, k_o, v_o, a_o,
                      b_o, g_o, bon_o, nsh_o, pa_s, carry):
    _inproj_body(x_ref, mod_ref, g_ref, wa_ref, wx_ref, wb_ref, pa_s, qkv_o, xal_o, gg_o, mg_o)
    _rwprep_body(pa_s, sh_ref, mu_ref, w0_ref, wup_ref, a0_ref, aup_ref, gup_ref, kk_ref, ka_ref, rk_ref, bd_ref,
                 r_o, lw_o, k_o, v_o, a_o, b_o, g_o, bon_o, nsh_o, carry)


def _inproj_prep_call(x, mod, s_sh, p):
    bn, seq, d = x.shape
    bb, ll = _tile(bn, seq, TOK_TILE)
    hw, wd = RW_WIDTH, PA_W
    tok = lambda w: pl.BlockSpec((bb, ll, w), lambda b, l: (b, l, 0))
    row = lambda w: pl.BlockSpec((bb, 1, w), lambda b, l: (b, 0, 0))
    full = lambda a: pl.BlockSpec(a.shape, lambda b, l: (0,) * a.ndim)
    consts = (p["mu"], p["w0"], p["wup"], p["a0"], p["aup"], p["gup"], p["kk"], p["ka"], p["rk"], p["bd64"])
    proj_w = (QKV_W, XAL_W, GG_W, MG_W)
    shapes = lambda ws: [jax.ShapeDtypeStruct((bn, seq, w), F32) for w in ws]
    return pl.pallas_call(
        _inproj_prep_body,
        grid=(bn // bb, seq // ll),
        in_specs=[tok(d), pl.BlockSpec((bb, 6, d), lambda b, l: (b, 0, 0)), full(p["norm1_g"])]
        + [full(w) for w in p["w_in"]] + [row(wd)] + [full(c) for c in consts],
        out_specs=[tok(w) for w in proj_w] + [tok(hw)] * 8 + [row(wd)],
        out_shape=shapes(proj_w) + shapes((hw,) * 8) + [jax.ShapeDtypeStruct((bn, 1, wd), F32)],
        scratch_shapes=[pltpu.VMEM((bb, ll, wd), F32), pltpu.VMEM((bb, 1, wd), F32)],
        compiler_params=_cparams(("arbitrary", "arbitrary"), VMEM_LIMIT),
        name="norm_inproj_prep",
    )(x, mod, p["norm1_g"], *p["w_in"], s_sh.reshape(bn, 1, wd), *consts)


def _unit_masks(n, tl):
    ri = lax.broadcasted_iota(I32, (n, n), 0)
    ci = lax.broadcasted_iota(I32, (n, n), 1)
    same = (ri >> _log2(tl)) == (ci >> _log2(tl))
    return same, same & (ri > ci), same & (ri >= ci)


def _rwscan_body(r_ref, lw_ref, k_ref, v_ref, a_ref, b_ref, s0_ref, y_ref, sn_ref, st, *, nu, nseq, tl, passes):
    n = nseq * tl
    n2 = 2 * n
    p_aa, p_inv, p_apply, p_state, p_y = passes

    hd = RW_HEAD

    @pl.when(pl.program_id(1) == 0)
    def _():
        zero = jnp.zeros((hd, hd), F32)
        for q in range(nu * nseq):
            for p in range(RW_HEADS // 2):
                st[q, p] = jnp.concatenate(
                    [jnp.concatenate([s0_ref[q, 2 * p], zero], axis=1),
                     jnp.concatenate([zero, s0_ref[q, 2 * p + 1]], axis=1)], axis=0)

    same, _, incl = _unit_masks(n, tl)
    m_cum = jnp.where(incl, 1.0, 0.0)
    m_seq = jnp.where(same, 1.0, 0.0)
    ri = lax.broadcasted_iota(I32, (n2, n2), 0)
    ci = lax.broadcasted_iota(I32, (n2, n2), 1)
    rt, ct = ri & (n - 1), ci & (n - 1)
    dsame = ((rt >> _log2(tl)) == (ct >> _log2(tl))) & ((ri >> _log2(n)) == (ci >> _log2(n)))
    strict_d = dsame & (rt > ct)
    incl_d = dsame & (rt >= ct)
    eye_d = jnp.where(ri == ci, 1.0, 0.0)
    lane = lax.broadcasted_iota(I32, (1, LANES), 1)
    m0 = jnp.where(lane < RW_HEAD, 1.0, 0.0)
    m1 = 1.0 - m0

    def dup(x):
        return jnp.concatenate([x * m0, x * m1], axis=0)

    def seq_rows(x, q):
        if nseq == 1:
            return x
        return jnp.concatenate([x[q * tl:(q + 1) * tl], x[n + q * tl:n + (q + 1) * tl]], axis=0)

    def unit_rows(parts):
        if nseq == 1:
            return parts[0]
        return jnp.concatenate([p[0:tl] for p in parts] + [p[tl:2 * tl] for p in parts], axis=0)

    chains = [(u, p) for u in range(nu) for p in range(RW_HEADS // 2)]
    ids = range(len(chains))
    cat0 = lambda *xs: jnp.concatenate(xs, axis=0)

    def ld(ref, c):
        u, p = chains[c]
        return ref[u * nseq:(u + 1) * nseq, :, p * LANES:(p + 1) * LANES].reshape(n, LANES)

    lw = [ld(lw_ref, c) for c in ids]
    cum = [_mm01(m_cum, x) for x in lw]
    tot = [_mm01(m_seq, x) for x in lw]
    e_c = [jnp.exp(x) for x in cum]
    e_n = [jnp.exp(-x) for x in cum]
    e_l = [jnp.exp(t - x) for t, x in zip(tot, cum)]
    at_d = [dup(ld(a_ref, c) * jnp.exp(cum[c] - lw[c])) for c in ids]
    rt_d = [dup(ld(r_ref, c) * e_c[c]) for c in ids]
    bt_d = [dup(ld(b_ref, c) * e_n[c]) for c in ids]
    kt_d = [dup(ld(k_ref, c) * e_n[c]) for c in ids]
    bh_d = [dup(ld(b_ref, c) * e_l[c]) for c in ids]
    kh_d = [dup(ld(k_ref, c) * e_l[c]) for c in ids]
    v_d = [dup(ld(v_ref, c)) for c in ids]
    aa = [_mm(cat0(at_d[c], rt_d[c]), cat0(bt_d[c], kt_d[c]), "nt", p_aa) for c in ids]
    a_ab = [jnp.where(strict_d, x[0:n2, 0:n2], 0.0) for x in aa]
    a_ak = [jnp.where(strict_d, x[0:n2, n2:], 0.0) for x in aa]
    a_rb = [jnp.where(incl_d, x[n2:, 0:n2], 0.0) for x in aa]
    a_rk = [jnp.where(incl_d, x[n2:, n2:], 0.0) for x in aa]
    zy = [_mm(cat0(a_ak[c], a_rk[c]), v_d[c], passes=p_apply) for c in ids]
    tinv = [eye_d + x for x in a_ab]
    nk = a_ab
    for _ in range(_log2(tl) - 1):
        nk = [_mm(x, x, passes=p_inv) for x in nk]
        tinv = [t + _mm(t, x, passes=p_inv) for t, x in zip(tinv, nk)]
    wu = [_mm(tinv[c], jnp.concatenate([at_d[c], zy[c][0:n2]], axis=1), passes=p_apply) for c in ids]
    seqs = range(nseq)
    srow = lambda c, q: (chains[c][0] * nseq + q, chains[c][1])
    s_old = [[st[srow(c, q)] for q in seqs] for c in ids]
    xs = [[_mm(cat0(seq_rows(wu[c][:, 0:LANES], q), seq_rows(rt_d[c], q)), s_old[c][q], "nt", p_state)
           for q in seqs] for c in ids]
    u_q = [[xs[c][q][0:2 * tl] + seq_rows(wu[c][:, LANES:], q) for q in seqs] for c in ids]
    for c in ids:
        for q in seqs:
            g_c = jnp.exp(tot[c][q * tl:q * tl + 1, :])
            st[srow(c, q)] = s_old[c][q] * g_c + _mm(cat0(u_q[c][q], seq_rows(v_d[c], q)),
                                                     cat0(seq_rows(bh_d[c], q), seq_rows(kh_d[c], q)), "tn", p_state)
    for c in ids:
        u, p = chains[c]
        y_d = (unit_rows([xs[c][q][2 * tl:] for q in seqs]) + _mm(a_rb[c], unit_rows(u_q[c]), passes=p_y)
               + zy[c][n2:])
        y_ref[u * nseq:(u + 1) * nseq, :, p * LANES:(p + 1) * LANES] = (y_d[0:n] + y_d[n:]).reshape(nseq, tl, LANES)

    @pl.when(pl.program_id(1) == pl.num_programs(1) - 1)
    def _():
        for q in range(nu * nseq):
            for p in range(RW_HEADS // 2):
                s = st[q, p]
                sn_ref[q, 2 * p] = s[0:hd, 0:hd]
                sn_ref[q, 2 * p + 1] = s[hd:, hd:]


def _unit_shape(bn, seq):
    if seq >= UNIT:
        assert seq % UNIT == 0
        return 1, UNIT
    assert UNIT % seq == 0 and bn % (UNIT // seq) == 0
    return UNIT // seq, seq


def _rwscan_call(r, lw, k2, v, a_s, b_s, s0, passes=RW_SCAN_PASSES):
    bn, seq, hw = r.shape
    nseq, tl = _unit_shape(bn, seq)
    nu = RW_UNITS_PER_STEP if bn % (RW_UNITS_PER_STEP * nseq) == 0 else 1
    rows = nu * nseq
    tok = pl.BlockSpec((rows, tl, hw), lambda b, c: (b, c, 0))
    stt = pl.BlockSpec((rows, RW_HEADS, RW_HEAD, RW_HEAD), lambda b, c: (b, 0, 0, 0))
    return pl.pallas_call(
        functools.partial(_rwscan_body, nu=nu, nseq=nseq, tl=tl, passes=passes),
        grid=(bn // rows, seq // tl),
        in_specs=[tok] * 6 + [stt],
        out_specs=[tok, stt],
        out_shape=[jax.ShapeDtypeStruct((bn, seq, hw), F32), jax.ShapeDtypeStruct(s0.shape, F32)],
        scratch_shapes=[pltpu.VMEM((rows, RW_HEADS // 2, LANES, LANES), F32)],
        compiler_params=_cparams(("arbitrary", "arbitrary"), VMEM_LIMIT),
        name="rwkv_scan",
    )(r, lw, k2, v, a_s, b_s, s0)


def _gla_body(qkv_ref, xal_ref, gate_ref, aup_ref, ab_ref, ng_ref, s0_ref, o_ref, sn_ref, st, *, nu, nseq, tl, cs):
    n = nseq * tl
    n2 = 2 * n
    nsub = tl // cs

    @pl.when(pl.program_id(1) == 0)
    def _():
        zero = jnp.zeros((GLA_DV, GLA_DK), F32)
        for q in range(nu * nseq):
            for p in range(GLA_HEADS // 2):
                st[q, p] = jnp.concatenate(
                    [jnp.concatenate([s0_ref[q, 2 * p].T, zero], axis=1),
                     jnp.concatenate([zero, s0_ref[q, 2 * p + 1].T], axis=1)], axis=0)

    same, _, incl = _unit_masks(n, cs)
    m_cum = jnp.where(incl, 1.0, 0.0)
    m_sub = jnp.where(same, 1.0, 0.0)
    ri = lax.broadcasted_iota(I32, (n2, n2), 0)
    ci = lax.broadcasted_iota(I32, (n2, n2), 1)
    rt, ct = ri & (n - 1), ci & (n - 1)
    causal_d = ((rt >> _log2(cs)) == (ct >> _log2(cs))) & ((ri >> _log2(n)) == (ci >> _log2(n))) & (rt >= ct)
    lane = lax.broadcasted_iota(I32, (1, LANES), 1)
    m0 = jnp.where(lane < GLA_DK, 1.0, 0.0)
    m1 = 1.0 - m0
    sr = lax.broadcasted_iota(I32, (2 * GLA_DV, LANES), 0)
    sc = lax.broadcasted_iota(I32, (2 * GLA_DV, LANES), 1)
    st_mask = jnp.where((sr >> _log2(GLA_DV)) == (sc >> _log2(GLA_DK)), 1.0, 0.0)

    def dup(x):
        return jnp.concatenate([x * m0, x * m1], axis=0)

    chains = [(u, p) for u in range(nu) for p in range(GLA_HEADS // 2)]
    ids = range(len(chains))
    urows = lambda u: slice(u * nseq, (u + 1) * nseq)
    ng = ng_ref[...]
    la_all = [-_softplus(-(_mm(xal_ref[urows(u), :, :].reshape(n, LANES), aup_ref[...], passes=3) + ab_ref[...]))
              * (1.0 / GLA_GATE_TAU) for u in range(nu)]

    def ld(ref, c, off, width):
        return ref[urows(chains[c][0]), :, off:off + width].reshape(n, width)

    q = [ld(qkv_ref, c, chains[c][1] * LANES, LANES) * (GLA_DK ** -0.5) for c in ids]
    k = [ld(qkv_ref, c, GLA_KW + chains[c][1] * LANES, LANES) for c in ids]
    vp = [ld(qkv_ref, c, 2 * GLA_KW + chains[c][1] * 2 * GLA_DV, 2 * GLA_DV) for c in ids]
    la = [la_all[u][:, p * LANES:(p + 1) * LANES] for u, p in chains]
    bc = [_mm01(m_cum, x) for x in la]
    bl = [_mm01(m_sub, x) for x in la]
    qe = [q[c] * jnp.exp(bc[c]) for c in ids]
    ke = [k[c] * jnp.exp(-bc[c]) for c in ids]
    kd = [k[c] * jnp.exp(bl[c] - bc[c]) for c in ids]
    att = [jnp.where(causal_d, _mm(dup(qe[c]), dup(ke[c]), "nt", passes=1), 0.0) for c in ids]
    v_st = [jnp.concatenate([x[:, 0:GLA_DV], x[:, GLA_DV:]], axis=0) for x in vp]
    o_st = [_mm(att[c], v_st[c], passes=1) for c in ids]
    upd = [[_mm(vp[c][r0:r0 + cs], kd[c][r0:r0 + cs], "tn", passes=1) for r0 in range(0, n, cs)] for c in ids]
    inter = [[None] * (n // cs) for _ in ids]
    for sq in range(nseq):
        s = [st[chains[c][0] * nseq + sq, chains[c][1]] for c in ids]
        for j in range(nsub):
            i = sq * nsub + j
            r0 = i * cs
            for c in ids:
                inter[c][i] = _mm(qe[c][r0:r0 + cs], s[c], "nt", passes=1)
                s[c] = s[c] * jnp.exp(bl[c][r0:r0 + 1, :]) + st_mask * upd[c][i]
        for c in ids:
            st[chains[c][0] * nseq + sq, chains[c][1]] = s[c]
    for c in ids:
        u, p = chains[c]
        o = o_st[c] + jnp.concatenate([x[:, 0:GLA_DV] for x in inter[c]] + [x[:, GLA_DV:] for x in inter[c]], axis=0)
        o = o * lax.rsqrt(jnp.mean(o * o, axis=-1, keepdims=True) + NORM_EPS) * ng
        goff = p * 2 * GLA_DV
        gp = ld(gate_ref, c, goff, 2 * GLA_DV)
        g_st = jnp.concatenate([gp[:, 0:GLA_DV], gp[:, GLA_DV:]], axis=0)
        ob = o * (g_st * _sigmoid(g_st))
        o_ref[urows(u), :, goff:goff + GLA_DV] = ob[0:n].reshape(nseq, tl, GLA_DV)
        o_ref[urows(u), :, goff + GLA_DV:goff + 2 * GLA_DV] = ob[n:].reshape(nseq, tl, GLA_DV)

    @pl.when(pl.program_id(1) == pl.num_programs(1) - 1)
    def _():
        for q in range(nu * nseq):
            for p in range(GLA_HEADS // 2):
                s = st[q, p]
                sn_ref[q, 2 * p] = s[0:GLA_DV, 0:GLA_DK].T
                sn_ref[q, 2 * p + 1] = s[GLA_DV:, GLA_DK:].T


def _gla_call(qkv, xal, gate, s0, p):
    bn, seq, _ = qkv.shape
    nseq, tl = _unit_shape(bn, seq)
    cs = min(GLA_CHUNK, seq)
    assert tl % cs == 0
    nu = GLA_UNITS_PER_STEP if bn % (GLA_UNITS_PER_STEP * nseq) == 0 else 1
    rows = nu * nseq
    tok = lambda w: pl.BlockSpec((rows, tl, w), lambda b, c: (b, c, 0))
    full = lambda a: pl.BlockSpec(a.shape, lambda b, c: (0,) * a.ndim)
    stt = pl.BlockSpec((rows, GLA_HEADS, GLA_DK, GLA_DV), lambda b, c: (b, 0, 0, 0))
    consts = (p["gla_aup"], p["gla_ab"], p["gla_ng"])
    return pl.pallas_call(
        functools.partial(_gla_body, nu=nu, nseq=nseq, tl=tl, cs=cs),
        grid=(bn // rows, seq // tl),
        in_specs=[tok(QKV_W), tok(XAL_W), tok(GG_W)] + [full(c) for c in consts] + [stt],
        out_specs=[tok(GLA_VW), stt],
        out_shape=[jax.ShapeDtypeStruct((bn, seq, GLA_VW), F32), jax.ShapeDtypeStruct(s0.shape, F32)],
        scratch_shapes=[pltpu.VMEM((rows, GLA_HEADS // 2, 2 * GLA_DV, LANES), F32)],
        compiler_params=_cparams(("arbitrary", "arbitrary"), VMEM_LIMIT),
        name="gla_chunked",
    )(qkv, xal, gate, *consts, s0)


def _merge_body(y_ref, g_ref, bon_ref, ob_ref, mg_ref, x_ref, mod_ref, gng_ref, gnb_ref, bd_ref, wpa_ref,
                wpb_ref, wout_ref, n2_ref, rwh_ref, rwl_ref, *rest):
    x1_o, h2_o, lg_o = rest[-3:]
    bb, ll, d = x_ref.shape
    n = bb * ll
    hw = RW_WIDTH
    bd = bd_ref[...]
    y = y_ref[...].reshape(n, hw)
    mu = _xmm01(y, bd, pieces=2) * (1.0 / RW_HEAD)
    dv = y - mu
    var = _xmm01(dv * dv, bd, pieces=1) * (1.0 / RW_HEAD)
    yn = dv * lax.rsqrt(var + RW_GN_EPS) * gng_ref[...] + gnb_ref[...]
    o_a = (yn + bon_ref[...].reshape(n, hw)) * g_ref[...].reshape(n, hw)
    o_b = ob_ref[...].reshape(n, GLA_VW)
    mg = mg_ref[...].reshape(n, 2 * d)
    merged = _sigmoid(mg[:, 0:d]) * _mm(o_a, wpa_ref[...]) + _sigmoid(mg[:, d:]) * _mm(o_b, wpb_ref[...])
    mix = _mm(merged, wout_ref[...]).reshape(bb, ll, d)
    x1 = x_ref[...] + mod_ref[:, 2:3, :] * mix
    x1_o[...] = x1
    yn2 = x1 * lax.rsqrt(jnp.mean(x1 * x1, axis=-1, keepdims=True) + NORM_EPS) * n2_ref[...]
    h2 = (yn2 * (1.0 + mod_ref[:, 4:5, :]) + mod_ref[:, 3:4, :]).reshape(n, d)
    hh, hl = _split(h2, 2)
    rwh, rwl = rwh_ref[...], rwl_ref[...]
    nt = lambda a, b: lax.dot_general(a, b, _DN["nt"], preferred_element_type=F32)
    lg_o[...] = nt(rwh, hh) + nt(rwl, hh) + nt(rwh, hl)
    _rows_to_packed(h2_o, h2)


def _merge_call(y, g, bonus, o_b, mg, x, mod, p, tn, first_tok, shared):
    bn, seq, d = x.shape
    bb, ll = _tile(bn, seq, TOK_TILE)
    nl = seq // ll
    assert first_tok % (bb * ll) == 0
    t0 = first_tok // (bb * ll)
    n_in = 7 + 9
    extra = [] if shared is None else list(shared)
    alias = {} if shared is None else {n_in: 1, n_in + 1: 2}
    tok = lambda w: pl.BlockSpec((bb, ll, w), lambda b, l: (b, l, 0))
    full = lambda a: pl.BlockSpec(a.shape, lambda b, l: (0,) * a.ndim)
    consts = (p["gn_g"], p["gn_b"], p["bd64"], p["w_pa"], p["w_pb"], p["w_out"], p["norm2_g"], p["rw_hi"],
              p["rw_lo"])
    return pl.pallas_call(
        _merge_body,
        grid=(bn // bb, nl),
        in_specs=[tok(RW_WIDTH)] * 3 + [tok(GLA_VW), tok(MG_W), tok(d),
                                        pl.BlockSpec((bb, 6, d), lambda b, l: (b, 0, 0))] + [full(c) for c in consts]
        + [pl.BlockSpec(memory_space=pl.ANY)] * len(extra),
        out_specs=[tok(d),
                   pl.BlockSpec((bb * ll * PCH, LANES), lambda b, l: (t0 + b * nl + l, 0)),
                   pl.BlockSpec((N_EXPERTS, bb * ll), lambda b, l: (0, t0 + b * nl + l))],
        out_shape=[jax.ShapeDtypeStruct((bn, seq, d), F32),
                   jax.ShapeDtypeStruct((tn * PCH, LANES), I32),
                   jax.ShapeDtypeStruct((N_EXPERTS, tn), F32)],
        input_output_aliases=alias,
        compiler_params=_cparams(("arbitrary", "arbitrary"), VMEM_LIMIT),
        name="merge_outproj_router",
    )(y, g, bonus, o_b, mg, x, mod, *consts, *extra)


def _route_body(lg_ref, rb_ref, e_o, rk_o, w_o, cnt_o, carry):
    ne, tm = lg_ref.shape

    @pl.when(pl.program_id(0) == 0)
    def _():
        carry[...] = jnp.zeros_like(carry)

    neg = -jnp.inf
    scores = _sigmoid(lg_ref[...])
    sel = scores + rb_ref[...]
    row_i = lax.broadcasted_iota(I32, (ne, tm), 0)
    row = row_i.astype(F32)
    grp = (row_i >> _log2(GROUP_SIZE)).astype(F32)

    def first_max(x, ids, none):
        m = jnp.max(x, axis=0, keepdims=True)
        return m, jnp.min(jnp.where(x == m, ids, none), axis=0, keepdims=True)

    gs = []
    gids = lax.broadcasted_iota(I32, (GROUP_SIZE, tm), 0)
    for gidx in range(N_GROUPS):
        rows = slice(gidx * GROUP_SIZE, (gidx + 1) * GROUP_SIZE)
        sg = _sigmoid(lg_ref[rows, :]) + rb_ref[rows, :]
        ids = (gids + gidx * GROUP_SIZE).astype(F32)
        m1, i1 = first_max(sg, ids, float(ne))
        gs.append(m1 + jnp.max(jnp.where(ids == i1, neg, sg), axis=0, keepdims=True))
    gs = jnp.concatenate(gs, axis=0)
    gid = lax.broadcasted_iota(I32, (N_GROUPS, tm), 0).astype(F32)
    cur = jnp.full((ne, tm), neg, F32)
    for _ in range(TOPK_GROUPS):
        _, gi = first_max(gs, gid, float(N_GROUPS))
        cur = jnp.where(grp == gi, sel, cur)
        gs = jnp.where(gid == gi, neg, gs)

    pm = jnp.zeros((ne, tm), F32)
    eidx, wts = [], []
    for _ in range(TOP_K):
        _, ei = first_max(cur, row, float(ne))
        hit = row == ei
        pm = jnp.where(hit, 1.0, pm)
        eidx.append(ei)
        wts.append(jnp.sum(jnp.where(hit, scores, 0.0), axis=0, keepdims=True))
        cur = jnp.where(hit, neg, cur)
    wsum = wts[0]
    for w in wts[1:]:
        wsum = wsum + w

    ri = lax.broadcasted_iota(I32, (tm, tm), 0)
    ci = lax.broadcasted_iota(I32, (tm, tm), 1)
    earlier = jnp.where(ri < ci, 1.0, 0.0)
    rank = _mm(pm, earlier, passes=1) + carry[...]
    carry[...] = carry[...] + jnp.sum(pm, axis=1, keepdims=True)
    cnt_o[...] = carry[...]

    rks = [jnp.sum(jnp.where(row == e, rank, 0.0), axis=0, keepdims=True) for e in eidx]
    e_o[0] = jnp.concatenate(eidx, axis=0).astype(I32)
    rk_o[0] = jnp.concatenate(rks, axis=0).astype(I32)
    w_o[0] = jnp.concatenate([w / wsum * ROUTED_SCALE for w in wts], axis=0)


def _route_call(logits_t, router_b):
    ne, tn = logits_t.shape
    tm = TOK_TILE
    assert tn % tm == 0
    col = pl.BlockSpec((ne, 1), lambda i: (0, 0))
    tab = pl.BlockSpec((1, TOP_K, tm), lambda i: (i, 0, 0))
    tab_shape = (tn // tm, TOP_K, tm)
    return pl.pallas_call(
        _route_body,
        grid=(tn // tm,),
        in_specs=[pl.BlockSpec((ne, tm), lambda i: (0, i)), col],
        out_specs=[tab, tab, tab, col],
        out_shape=[jax.ShapeDtypeStruct(tab_shape, I32), jax.ShapeDtypeStruct(tab_shape, I32),
                   jax.ShapeDtypeStruct(tab_shape, F32), jax.ShapeDtypeStruct((ne, 1), F32)],
        scratch_shapes=[pltpu.VMEM((ne, 1), F32)],
        compiler_params=_cparams(("arbitrary",)),
        name="moe_route",
    )(logits_t, router_b.reshape(ne, 1))


def _dest_body(e_ref, rk_ref, ps_ref, d_o):
    ne, tm = ps_ref.shape[0], e_ref.shape[2]
    ids = lax.broadcasted_iota(I32, (ne, tm), 0)
    ps = ps_ref[...]
    for t in range(e_ref.shape[0]):
        first = [jnp.sum(jnp.where(ids == e_ref[t, kk:kk + 1, :], ps, 0.0), axis=0, keepdims=True)
                 for kk in range(TOP_K)]
        d_o[t] = (jnp.concatenate(first, axis=0).astype(I32) + rk_ref[t]) * PCH


def _dest_call(eidx, rank, pad_start):
    nt, _, tm = eidx.shape
    ne = pad_start.shape[0]
    per = next(k for k in (4, 2, 1) if nt % k == 0)
    tab = pl.BlockSpec((per, TOP_K, tm), lambda i: (i, 0, 0))
    return pl.pallas_call(
        _dest_body,
        grid=(nt // per,),
        in_specs=[tab, tab, pl.BlockSpec((ne, 1), lambda i: (0, 0))],
        out_specs=tab,
        out_shape=jax.ShapeDtypeStruct(eidx.shape, I32),
        compiler_params=_cparams(("arbitrary",)),
        name="moe_dest",
    )(eidx, rank, pad_start.astype(F32).reshape(ne, 1))


def _pslab(ref, offset):
    return ref.at[pl.ds(pl.multiple_of(offset, PCH), PCH)]


def _dispatch_body(d_ref, h2_ref, xs_hbm, sem, *, tm):
    def issue(m, carry):
        for kk in range(TOP_K):
            pltpu.make_async_copy(_slab(h2_ref, m), _pslab(xs_hbm, d_ref[0, kk, m]), sem).start(priority=kk % 2)
        return carry

    lax.fori_loop(0, tm, issue, 0)
    all_rows = xs_hbm.at[pl.ds(0, tm * TOP_K * PCH)]
    pltpu.make_async_copy(all_rows, all_rows, sem).wait()


def _assign_spec(tm, index_map):
    return pl.BlockSpec((1, TOP_K, tm), index_map, memory_space=pltpu.SMEM)


def _dispatch_call(dest, h2s, n_rows):
    tn = h2s.shape[0] // PCH
    tm = TOK_TILE
    assert dest.shape == (tn // tm, TOP_K, tm)
    blk = _assign_spec(tm, lambda i: (i, 0, 0))
    return pl.pallas_call(
        functools.partial(_dispatch_body, tm=tm),
        grid=(tn // tm,),
        in_specs=[blk, pl.BlockSpec((tm * PCH, LANES), lambda i: (i, 0))],
        out_specs=pl.BlockSpec(memory_space=pl.ANY),
        out_shape=jax.ShapeDtypeStruct((n_rows * PCH, LANES), I32),
        scratch_shapes=[pltpu.SemaphoreType.DMA],
        compiler_params=_cparams(("arbitrary",)),
        name="moe_dispatch",
    )(dest, h2s)


def _expert_body(bi_ref, nr_ref, ld_ref, nx_ref, xs_hbm, wg_hbm, wu_hbm, wd_hbm, ob_ref, wg_buf, wu_buf, wd_buf,
                 wg_bf, wu_bf, wd_bf, xbuf, sem, xsem):
    i = pl.program_id(0)
    nsteps = pl.num_programs(0)
    nr = nr_ref[i]
    slot = ld_ref[i]
    blk_rows = MOE_BLK * PCH

    def row_block(j):
        s = lax.rem(j, ROW_SLOTS)
        src = xs_hbm.at[pl.ds(pl.multiple_of(bi_ref[j] * blk_rows, blk_rows), blk_rows)]
        return pltpu.make_async_copy(src, xbuf.at[s], xsem.at[s])

    @pl.when(i == 0)
    def _():
        for j in range(ROW_SLOTS - 1):
            row_block(j).start()

    @pl.when(i + ROW_SLOTS - 1 < nsteps)
    def _():
        row_block(i + ROW_SLOTS - 1).start()

    def fetch(e, s):
        return (pltpu.make_async_copy(wg_hbm.at[e], wg_buf.at[s], sem.at[s]),
                pltpu.make_async_copy(wu_hbm.at[e], wu_buf.at[s], sem.at[s]),
                pltpu.make_async_copy(wd_hbm.at[e], wd_buf.at[s], sem.at[s]))

    @pl.when(i == 0)
    def _():
        for s in range(WEIGHT_SLOTS - 1):
            e0 = nx_ref[nx_ref.shape[0] - (WEIGHT_SLOTS - 1) + s]

            @pl.when(e0 >= 0)
            def _():
                for k, cp in enumerate(fetch(e0, s)):
                    cp.start(priority=k % 2)

    @pl.when(slot >= 0)
    def _():
        for cp in fetch(0, slot):
            cp.wait()

        @pl.when(nx_ref[i] >= 0)
        def _():
            for k, cp in enumerate(fetch(nx_ref[i], lax.rem(slot + WEIGHT_SLOTS - 1, WEIGHT_SLOTS))):
                cp.start(priority=k % 2)

        wg_bf[...] = wg_buf[slot].astype(BF16)
        wu_bf[...] = wu_buf[slot].astype(BF16)
        wd_bf[...] = wd_buf[slot].astype(BF16)

    row_block(i).wait()

    @pl.when(nr > 0)
    def _():
        part = MOE_BLK // EXPERT_PARTS
        firsts = [q * part for q in range(EXPERT_PARTS)]
        rid = lax.broadcasted_iota(I32, (part, LANES), 0)
        xs_ref = xbuf.at[lax.rem(i, ROW_SLOTS)]
        x = [_rows_from_packed(xs_ref, part, rid < nr - f, f) for f in firsts]
        hg = [jnp.dot(v, wg_bf[...], preferred_element_type=F32) for v in x]
        hu = [jnp.dot(v, wu_bf[...], preferred_element_type=F32) for v in x]
        hh = [(g * _sigmoid(g) * u).astype(BF16) for g, u in zip(hg, hu)]
        out = [jnp.dot(v, wd_bf[...], preferred_element_type=F32) for v in hh]
        for f, v in zip(firsts, out):
            _rows_to_packed(ob_ref, v, f)


def _expert_tables(counts, pad_start, pad_end, nb):
    ne = counts.shape[0]
    experts = jnp.arange(ne, dtype=I32)
    first_row = jnp.arange(nb, dtype=I32) * MOE_BLK
    block_e = jnp.minimum(jnp.sum(pad_end[None, :] <= first_row[:, None], axis=1), ne - 1).astype(I32)
    mine = block_e[:, None] == experts[None, :]
    pick = lambda v: jnp.sum(jnp.where(mine, v[None, :], 0), axis=1)
    has = counts > 0
    ordinal = jnp.cumsum(has.astype(I32)) - 1
    start_b, count_b, ord_b = pick(pad_start), pick(counts), pick(ordinal)
    block_rows = jnp.clip(start_b + count_b - first_row, 0, MOE_BLK).astype(I32)
    block_i = jnp.minimum(jnp.arange(nb, dtype=I32), pad_end[-1] // MOE_BLK - 1).astype(I32)
    starts = (first_row == start_b) & (block_rows > 0)
    load_slot = jnp.where(starts, ord_b % WEIGHT_SLOTS, -1).astype(I32)
    nth = lambda want: jnp.max(jnp.where(has[None, :] & (ordinal[None, :] == want[:, None]), experts[None, :], -1),
                               axis=1)
    ahead = jnp.where(starts, nth(ord_b + WEIGHT_SLOTS - 1), -1)
    lead = nth(jnp.arange(WEIGHT_SLOTS - 1, dtype=I32))
    return block_i, block_rows, load_slot, jnp.concatenate([ahead, lead]).astype(I32)


def _expert_call(tables, xs, wg, wu, wd):
    nb = xs.shape[0] // (MOE_BLK * PCH)
    assert nb >= ROW_SLOTS
    d, ff = wg.shape[1], wg.shape[2]
    rows = pl.BlockSpec((MOE_BLK * PCH, LANES), lambda i, bi, nr, ld, nx: (bi[i], 0))
    hbm = pl.BlockSpec(memory_space=pl.ANY)
    grid_spec = pltpu.PrefetchScalarGridSpec(
        num_scalar_prefetch=4,
        grid=(nb,),
        in_specs=[hbm, hbm, hbm, hbm],
        out_specs=rows,
        scratch_shapes=[pltpu.VMEM((WEIGHT_SLOTS, d, ff), F32), pltpu.VMEM((WEIGHT_SLOTS, d, ff), F32),
                        pltpu.VMEM((WEIGHT_SLOTS, ff, d), F32),
                        pltpu.VMEM((d, ff), BF16), pltpu.VMEM((d, ff), BF16), pltpu.VMEM((ff, d), BF16),
                        pltpu.VMEM((ROW_SLOTS, MOE_BLK * PCH, LANES), I32),
                        pltpu.SemaphoreType.DMA((WEIGHT_SLOTS,)), pltpu.SemaphoreType.DMA((ROW_SLOTS,))],
    )
    return pl.pallas_call(
        _expert_body,
        grid_spec=grid_spec,
        out_shape=jax.ShapeDtypeStruct(xs.shape, I32),
        compiler_params=_cparams(("arbitrary",), VMEM_LIMIT),
        name="moe_experts",
    )(*tables, xs, wg, wu, wd)


def _combine_body(d_ref, dn_ref, wt_ref, ob_hbm, h2_ref, x1_ref, mod_ref, sg_ref, su_ref,
                  sd_ref, fg_ref, out_ref, gbuf, rbuf, wcol, sem, *, tm, nl):
    bb, ll, d = x1_ref.shape
    step = pl.program_id(0) * nl + pl.program_id(1)
    last = pl.num_programs(0) * nl - 1
    parity = lax.rem(step, 2)
    grp = CMB_GROUP

    def request(d_tab, g, s):
        for j in range(grp):
            m = g * grp + j
            for kk in range(TOP_K):
                pltpu.make_async_copy(_pslab(ob_hbm, d_tab[0, kk, m]), _slab(gbuf.at[s], kk * tm + m),
                                      sem.at[s]).start(priority=kk % 2)

    def mix(g, s):
        r0 = pl.multiple_of(g * grp, grp)
        w = wcol[pl.ds(r0, grp), :]
        wk = [w[:, kk:kk + 1] for kk in range(TOP_K)]
        for c in range(PCH):
            acc_lo = acc_hi = None
            for kk in range(TOP_K):
                words = gbuf[s, pl.ds((kk * tm + r0) * PCH + c, grp, stride=PCH), :]
                lo, hi = _unpack_pair(words)
                acc_lo = wk[kk] * lo if acc_lo is None else acc_lo + wk[kk] * lo
                acc_hi = wk[kk] * hi if acc_hi is None else acc_hi + wk[kk] * hi
            rbuf[pl.ds(r0, grp), c * LANES:(c + 1) * LANES] = acc_lo
            rbuf[pl.ds(r0, grp), (c + PCH) * LANES:(c + PCH + 1) * LANES] = acc_hi

    @pl.when(step == 0)
    def _():
        def first(g, carry):
            request(d_ref, g, 0)
            return carry
        lax.fori_loop(0, tm // grp, first, 0)

    ri = lax.broadcasted_iota(I32, (tm, tm), 0)
    ci = lax.broadcasted_iota(I32, (tm, tm), 1)
    eye = jnp.where(ri == ci, 1.0, 0.0).astype(BF16)
    wc = None
    for piece in _split(wt_ref[0], 3):
        t = lax.dot_general(eye, piece, _DN["nt"], preferred_element_type=F32)
        wc = t if wc is None else wc + t
    wcol[...] = wc

    def run(slot):
        pltpu.make_async_copy(ob_hbm.at[pl.ds(0, tm * TOP_K * PCH)], gbuf.at[slot], sem.at[slot]).wait()

        @pl.when(step < last)
        def _():
            def both(g, carry):
                request(dn_ref, g, 1 - slot)
                mix(g, slot)
                return carry
            lax.fori_loop(0, tm // grp, both, 0)

        @pl.when(step == last)
        def _():
            def only(g, carry):
                mix(g, slot)
                return carry
            lax.fori_loop(0, tm // grp, only, 0)

    for slot in range(2):
        pl.when(parity == slot)(functools.partial(run, slot))

    routed = rbuf[...]
    h2 = _rows_from_packed(h2_ref, tm)
    hg = jnp.dot(h2, sg_ref[...], preferred_element_type=F32)
    hu = jnp.dot(h2, su_ref[...], preferred_element_type=F32)
    shared = jnp.dot((hg * _sigmoid(hg) * hu).astype(BF16), sd_ref[...], preferred_element_type=F32)
    ff = (routed + shared).reshape(bb, ll, d)
    x2 = x1_ref[...] + mod_ref[:, 5:6, :] * ff
    out_ref[...] = x2 * lax.rsqrt(jnp.mean(x2 * x2, axis=-1, keepdims=True) + NORM_EPS) * fg_ref[...]


def _combine_call(dest, wts, first_tok, ob, h2s, x1, mod, p):
    bn, seq, d = x1.shape
    tm = CMB_TILE
    bb, ll = _tile(bn, seq, tm)
    nl = seq // ll
    tn = bn * seq
    nsteps = tn // tm
    per = dest.shape[2] // tm
    assert first_tok % tm == 0 and dest.shape[2] % tm == 0
    tile = lambda g: ((first_tok // tm + g) // per, 0, (first_tok // tm + g) % per)
    smem = _assign_spec(tm, lambda b, l: tile(b * nl + l))
    smem_next = _assign_spec(tm, lambda b, l: tile(jnp.minimum(b * nl + l + 1, nsteps - 1)))
    wblk = pl.BlockSpec((1, TOP_K, tm), lambda b, l: tile(b * nl + l))
    tok = pl.BlockSpec((bb, ll, d), lambda b, l: (b, l, 0))
    full = lambda a: pl.BlockSpec(a.shape, lambda b, l: (0,) * a.ndim)
    consts = (p["sh_gate"], p["sh_up"], p["sh_down"], p["final_g"])
    return pl.pallas_call(
        functools.partial(_combine_body, tm=tm, nl=nl),
        grid=(bn // bb, nl),
        in_specs=[smem, smem_next, wblk, pl.BlockSpec(memory_space=pl.ANY),
                  pl.BlockSpec((tm * PCH, LANES), lambda b, l: (first_tok // tm + b * nl + l, 0)),
                  tok, pl.BlockSpec((bb, 6, d), lambda b, l: (b, 0, 0))] + [full(c) for c in consts],
        out_specs=tok,
        out_shape=jax.ShapeDtypeStruct((bn, seq, d), F32),
        scratch_shapes=[pltpu.VMEM((2, tm * TOP_K * PCH, LANES), I32), pltpu.VMEM((tm, d), F32),
                        pltpu.VMEM((tm, TOP_K), F32), pltpu.SemaphoreType.DMA((2,))],
        compiler_params=_cparams(("arbitrary", "arbitrary"), VMEM_LIMIT),
        name="moe_combine_final",
    )(dest, dest, wts, ob, h2s, x1, mod, *consts)


def _layer_params(l, ada_w, ada_b, norm1_g, norm2_g, w_in, mu_shift, rw_w0, rw_w_up, rw_a0, rw_a_up, rw_g_up,
                  rw_k_k, rw_k_a, rw_r_k, rw_gn_g, rw_gn_b, gla_a_up, gla_a_bias, gla_norm_g, w_pa, w_pb, w_out,
                  router_w, router_b, exp_gate, exp_up, exp_down, sh_gate, sh_up, sh_down):
    d = D_MODEL
    wi = w_in[l]
    gla0 = RW_SHIFT_COLS
    xal0 = gla0 + QKV_W
    pad = jnp.zeros((d, XAL_W - GLA_GATE_RANK), BF16)
    w_pieces = (wi[:, :xal0].astype(BF16),
                jnp.concatenate([wi[:, xal0:xal0 + GLA_GATE_RANK].astype(BF16), pad], axis=1),
                wi[:, xal0 + GLA_GATE_RANK:].astype(BF16))
    zr = jnp.zeros((RW_W_RANK, RW_WIDTH), F32)
    hid = jnp.arange(RW_WIDTH) // RW_HEAD
    row = lambda a: a.reshape(1, -1)
    rw_t = router_w[l].T
    rw_hi = rw_t.astype(BF16)
    return dict(
        ada_w=ada_w[l], ada_b=ada_b[l], norm1_g=norm1_g[l].reshape(1, 1, d),
        norm2_g=norm2_g[l].reshape(1, 1, d), w_in=w_pieces,
        mu=mu_shift[l].reshape(1, 1, -1), w0=row(rw_w0[l]), wup=jnp.concatenate([rw_w_up[l], zr], axis=0),
        a0=row(rw_a0[l]), aup=jnp.concatenate([zr, rw_a_up[l]], axis=0), gup=rw_g_up[l].astype(BF16),
        kk=row(rw_k_k[l]), ka=row(rw_k_a[l]), rk=row(rw_r_k[l]),
        bd64=(hid[:, None] == hid[None, :]).astype(BF16),
        gn_g=row(rw_gn_g[l]), gn_b=row(rw_gn_b[l]),
        gla_aup=jnp.concatenate([gla_a_up[l], jnp.zeros((XAL_W - GLA_GATE_RANK, GLA_KW), F32)], axis=0),
        gla_ab=row(gla_a_bias[l]), gla_ng=row(gla_norm_g[l]),
        w_pa=w_pa[l].astype(BF16), w_pb=w_pb[l].astype(BF16), w_out=w_out[l].astype(BF16),
        rw_hi=rw_hi, rw_lo=(rw_t - rw_hi.astype(F32)).astype(BF16), router_b=router_b[l],
        exp_gate=exp_gate[l], exp_up=exp_up[l], exp_down=exp_down[l],
        sh_gate=sh_gate[l].astype(BF16), sh_up=sh_up[l].astype(BF16), sh_down=sh_down[l].astype(BF16),
    )


def _mixer_group(x, mod, s_rw, s_sh, s_gla, p, tn, first_tok, shared):
    qkv, xal, gg, mg, r, lw, k2, v, a_s, b_s, g, bonus, new_sh = _inproj_prep_call(x, mod, s_sh, p)
    y, rw_new = _rwscan_call(r, lw, k2, v, a_s, b_s, s_rw)
    o_b, gla_new = _gla_call(qkv, xal, gg, s_gla, p)
    x1, h2s, logits = _merge_call(y, g, bonus, o_b, mg, x, mod, p, tn, first_tok, shared)
    states = (rw_new, new_sh[:, 0, :], gla_new)
    return x1, h2s, logits, states


def _moe(h2s, logits, p):
    tn = h2s.shape[0] // PCH
    eidx, rank, wts, counts = _route_call(logits, p["router_b"])
    counts = counts[:, 0].astype(I32)
    padded = (counts + MOE_BLK - 1) // MOE_BLK * MOE_BLK
    pad_end = jnp.cumsum(padded)
    pad_start = (pad_end - padded).astype(I32)
    nb = (tn * TOP_K + N_EXPERTS * (MOE_BLK - 1)) // MOE_BLK + 1
    tables = _expert_tables(counts, pad_start, pad_end, nb)
    dest = _dest_call(eidx, rank, pad_start)
    xs = _dispatch_call(dest, h2s, nb * MOE_BLK)
    ob = _expert_call(tables, xs, p["exp_gate"], p["exp_up"], p["exp_down"])
    return ob, dest, wts


def kernel(x_prompt, x_sample, c_prompt, c_sample, state_rwkv, state_shift, state_gla, ada_w, ada_b, norm1_g,
           norm2_g, w_in, mu_shift, rw_w0, rw_w_up, rw_a0, rw_a_up, rw_g_up, rw_k_k, rw_k_a, rw_r_k, rw_gn_g,
           rw_gn_b, gla_a_up, gla_a_bias, gla_norm_g, w_pa, w_pb, w_out, router_w, router_b, exp_gate, exp_up,
           exp_down, sh_gate, sh_up, sh_down, final_g):
    depth = ada_w.shape[0]
    bp, bs = x_prompt.shape[0], x_sample.shape[0]
    tp = bp * x_prompt.shape[1]
    tn = tp + bs * x_sample.shape[1]
    xs_g = [x_prompt, x_sample]
    c_all = jnp.concatenate([c_prompt, c_sample], axis=0)
    zeros = lambda shape: jnp.zeros(shape, x_prompt.dtype)
    new_states = [[], []]
    fg = final_g.reshape(1, 1, D_MODEL)
    for l in range(depth):
        p = _layer_params(l, ada_w, ada_b, norm1_g, norm2_g, w_in, mu_shift, rw_w0, rw_w_up, rw_a0, rw_a_up,
                          rw_g_up, rw_k_k, rw_k_a, rw_r_k, rw_gn_g, rw_gn_b, gla_a_up, gla_a_bias, gla_norm_g,
                          w_pa, w_pb, w_out, router_w, router_b, exp_gate, exp_up, exp_down, sh_gate, sh_up,
                          sh_down)
        p["final_g"] = fg
        mod_all = _mod_call(c_all, p["ada_w"], p["ada_b"])
        mods = [mod_all[:bp], mod_all[bp:]]
        states_in = [
            (zeros((bp, RW_HEADS, RW_HEAD, RW_HEAD)), zeros((bp, RW_SHIFT_COLS)),
             zeros((bp, GLA_HEADS, GLA_DK, GLA_DV))),
            (state_rwkv[l], state_shift[l], state_gla[l]),
        ]
        x1s, shared = [], None
        firsts = [0, tp]
        for gi in range(2):
            x1, h2_all, lg_all, st = _mixer_group(xs_g[gi], mods[gi], *states_in[gi], p, tn, firsts[gi], shared)
            shared = (h2_all, lg_all)
            x1s.append(x1)
            new_states[gi].append(st)
        ob, dest, wts = _moe(*shared, p)
        assert depth == 1, "the fused final norm assumes a single layer"
        xs_g = [_combine_call(dest, wts, firsts[gi], ob, shared[0], x1s[gi], mods[gi], p) for gi in range(2)]
    stack = lambda gi, j: new_states[gi][0][j][None] if depth == 1 else jnp.stack([s[j] for s in new_states[gi]])
    return (xs_g[0], xs_g[1], stack(0, 0), stack(0, 1), stack(0, 2), stack(1, 0), stack(1, 1), stack(1, 2))
```

```python
import functools

import jax
import jax.numpy as jnp
from jax import lax
from jax.experimental import pallas as pl
from jax.experimental.pallas import tpu as pltpu

F32, BF16, I32 = jnp.float32, jnp.bfloat16, jnp.int32

D_MODEL = 1024
RW_HEADS, RW_HEAD = 8, 64
RW_WIDTH = RW_HEADS * RW_HEAD
RW_W_RANK, RW_A_RANK, RW_G_RANK = 64, 64, 128
RW_GN_EPS = 64e-5
GLA_HEADS, GLA_DK, GLA_DV = 4, 64, 128
GLA_KW, GLA_VW = GLA_HEADS * GLA_DK, GLA_HEADS * GLA_DV
GLA_GATE_RANK = 16
GLA_GATE_TAU = 16.0
GLA_CHUNK = 16
RW_SHIFT_COLS = 3 * RW_WIDTH + RW_W_RANK + RW_A_RANK + RW_G_RANK
N_EXPERTS, TOP_K, N_GROUPS, TOPK_GROUPS = 256, 8, 8, 4
GROUP_SIZE = N_EXPERTS // N_GROUPS
EXPERT_FF = 256
ROUTED_SCALE = 2.5
NORM_EPS = 1e-6

LANES = 128
CHUNKS = D_MODEL // LANES
PCH = CHUNKS // 2
UNIT = 64
RW_SCAN_PASSES = (1, 1, 1, 1, 1)
GLA_UNITS_PER_STEP = 4
RW_UNITS_PER_STEP = 4
VMEM_LIMIT = 56 * 1024 * 1024

PA_W, QKV_W, XAL_W, GG_W, MG_W = RW_SHIFT_COLS, 2 * GLA_KW + GLA_VW, LANES, GLA_VW, 2 * D_MODEL

TOK_TILE = 256
MOE_BLK = 512
EXPERT_PARTS = 1
WEIGHT_SLOTS = 3
ROW_SLOTS = 3
CMB_TILE = 256
CMB_GROUP = 16

_DN = {
    "nn": (((1,), (0,)), ((), ())),
    "nt": (((1,), (1,)), ((), ())),
    "tn": (((0,), (0,)), ((), ())),
}


def _split(x, pieces):
    out, rem = [], x
    for i in range(pieces):
        p = rem.astype(BF16)
        out.append(p)
        if i + 1 < pieces:
            rem = rem - p.astype(F32)
    return out


def _mm(a, b, form="nn", passes=1):
    dn = _DN[form]
    if passes == 6:
        return lax.dot_general(a.astype(F32), b.astype(F32), dn, precision=lax.Precision.HIGHEST,
                               preferred_element_type=F32)
    if passes == 1:
        return lax.dot_general(a.astype(BF16), b.astype(BF16), dn, preferred_element_type=F32)
    ah, al = _split(a, 2)
    bh, bl = _split(b, 2)
    out = lax.dot_general(ah, bh, dn, preferred_element_type=F32)
    out = out + lax.dot_general(ah, bl, dn, preferred_element_type=F32)
    return out + lax.dot_general(al, bh, dn, preferred_element_type=F32)


def _mm01(m01, x, pieces=3):
    m = m01.astype(BF16)
    out = None
    for p in _split(x, pieces):
        t = lax.dot_general(m, p, _DN["nn"], preferred_element_type=F32)
        out = t if out is None else out + t
    return out


def _xmm01(x, m01, pieces=2):
    m = m01.astype(BF16)
    out = None
    for p in _split(x, pieces):
        t = lax.dot_general(p, m, _DN["nn"], preferred_element_type=F32)
        out = t if out is None else out + t
    return out


HI16 = -65536


def _bf16_bits(x):
    return lax.bitcast_convert_type(x.astype(BF16).astype(F32), I32)


def _unpack_pair(w):
    return lax.bitcast_convert_type(w << 16, F32), lax.bitcast_convert_type(w & HI16, F32)


def _rows_to_packed(ref, x, first=0):
    for c in range(PCH):
        lo = _bf16_bits(x[:, c * LANES:(c + 1) * LANES])
        hi = _bf16_bits(x[:, (c + PCH) * LANES:(c + PCH + 1) * LANES])
        ref[pl.ds(first * PCH + c, x.shape[0], stride=PCH), :] = ((lo >> 16) & 0xFFFF) | (hi & HI16)


def _rows_from_packed(ref, n, live=None, first=0):
    lows, highs = [], []
    for c in range(PCH):
        w = ref[pl.ds(first * PCH + c, n, stride=PCH), :]
        if live is not None:
            w = jnp.where(live, w, 0)
        lo, hi = _unpack_pair(w)
        lows.append(lo.astype(BF16))
        highs.append(hi.astype(BF16))
    return jnp.concatenate(lows + highs, axis=1)


def _slab(ref, row):
    return ref.at[pl.ds(pl.multiple_of(row * PCH, PCH), PCH)]


def _sigmoid(x):
    return 1.0 / (1.0 + jnp.exp(-x))


def _softplus(x):
    return jnp.maximum(x, 0.0) + jnp.log(1.0 + jnp.exp(-jnp.abs(x)))


def _log2(n):
    assert n > 0 and n & (n - 1) == 0, n
    return n.bit_length() - 1


def _cparams(sem, vmem=None):
    return pltpu.CompilerParams(dimension_semantics=sem, vmem_limit_bytes=vmem)


def _mod_body(c_ref, w_ref, b_ref, o_ref):
    c = c_ref[...]
    o_ref[0] = _mm(c * _sigmoid(c), w_ref[...], passes=3) + b_ref[...]


def _mod_call(c_all, ada_w, ada_b):
    bt, d = c_all.shape
    out = pl.pallas_call(
        _mod_body,
        grid=(6,),
        in_specs=[pl.BlockSpec((bt, d), lambda k: (0, 0)),
                  pl.BlockSpec((d, d), lambda k: (0, k)),
                  pl.BlockSpec((1, d), lambda k: (0, k))],
        out_specs=pl.BlockSpec((1, bt, d), lambda k: (k, 0, 0)),
        out_shape=jax.ShapeDtypeStruct((6, bt, d), F32),
        compiler_params=_cparams(("arbitrary",)),
        name="adaln_mod",
    )(c_all, ada_w, ada_b.reshape(1, 6 * d))
    return jnp.transpose(out, (1, 0, 2))


def _inproj_body(x_ref, mod_ref, g_ref, wa_ref, wx_ref, wb_ref, pa_ref, qkv_ref, xal_ref, gg_ref, mg_ref):
    bb, ll, d = x_ref.shape
    x = x_ref[...]
    y = x * lax.rsqrt(jnp.mean(x * x, axis=-1, keepdims=True) + NORM_EPS) * g_ref[...]
    h = y * (1.0 + mod_ref[:, 1:2, :]) + mod_ref[:, 0:1, :]
    hb = h.reshape(bb * ll, d).astype(BF16)
    for w_ref, outs in ((wa_ref, (pa_ref, qkv_ref)), (wx_ref, (xal_ref,)), (wb_ref, (gg_ref, mg_ref))):
        off = 0
        for ref in outs:
            w = ref.shape[-1]
            ref[...] = jnp.dot(hb, w_ref[:, off:off + w], preferred_element_type=F32).reshape(bb, ll, w)
            off += w


def _tile(bn, seq, tile):
    if seq >= tile:
        assert seq % tile == 0
        return 1, tile
    assert tile % seq == 0 and bn % (tile // seq) == 0
    return tile // seq, seq


def _rwprep_body(pa_ref, sh_ref, mu_ref, w0_ref, wup_ref, a0_ref, aup_ref, gup_ref, kk_ref, ka_ref, rk_ref,
                 bd_ref, r_o, lw_o, k_o, v_o, a_o, b_o, g_o, bon_o, nsh_o, carry):
    bb, ll, wd = pa_ref.shape
    n = bb * ll
    hw = RW_WIDTH

    @pl.when(pl.program_id(1) == 0)
    def _():
        carry[...] = sh_ref[...]

    pa = pa_ref[...]
    rolled = pltpu.roll(pa.reshape(n, wd), 1, 0).reshape(bb, ll, wd)
    tok = lax.broadcasted_iota(I32, (bb, ll, wd), 1)
    prev = jnp.where(tok == 0, carry[...], rolled)
    last = pa_ref[:, ll - 1:ll, :]
    carry[...] = last
    nsh_o[...] = last
    xs = (pa + (prev - pa) * mu_ref[...]).reshape(n, wd)

    r, k, v = xs[:, 0:hw], xs[:, hw:2 * hw], xs[:, 2 * hw:3 * hw]
    xwa = xs[:, 3 * hw:3 * hw + LANES]
    xg = xs[:, 3 * hw + LANES:]
    w_log = -_softplus(-(w0_ref[...] + _mm(jnp.tanh(xwa), wup_ref[...], passes=3))) - 0.5
    lw = -jnp.exp(w_log)
    a = _sigmoid(a0_ref[...] + _mm(xwa, aup_ref[...], passes=3))
    g = _mm(_sigmoid(xg), gup_ref[...])
    bd = bd_ref[...]
    kkv = k * kk_ref[...]
    kkn = kkv * lax.rsqrt(jnp.maximum(_xmm01(kkv * kkv, bd, pieces=1), 1e-24))
    k2 = k * (1.0 + (a - 1.0) * ka_ref[...])
    bonus = _xmm01(r * k2 * rk_ref[...], bd, pieces=1) * v
    for ref, val in ((r_o, r), (lw_o, lw), (k_o, k2), (v_o, v), (a_o, -kkn), (b_o, kkn * a), (g_o, g),
                     (bon_o, bonus)):
        ref[...] = val.reshape(bb, ll, hw)


def _inproj_prep_body(x_ref, mod_ref, g_ref, wa_ref, wx_ref, wb_ref, sh_ref, mu_ref, w0_ref, wup_ref, a0_ref, aup_ref,
                      gup_ref, kk_ref, ka_ref, rk_ref, bd_ref, qkv_o, xal_o, gg_o, mg_o, r_o, lw_o, k_o, v_o, a_o,
                      b_o, g_o, bon_o, nsh_o, pa_s, carry):
    _inproj_body(x_ref, mod_ref, g_ref, wa_ref, wx_ref, wb_ref, pa_s, qkv_o, xal_o, gg_o, mg_o)
    _rwprep_body(pa_s, sh_ref, mu_ref, w0_ref, wup_ref, a0_ref, aup_ref, gup_ref, kk_ref, ka_ref, rk_ref, bd_ref,
                 r_o, lw_o, k_o, v_o, a_o, b_o, g_o, bon_o, nsh_o, carry)


def _inproj_prep_call(x, mod, s_sh, p):
    bn, seq, d = x.shape
    bb, ll = _tile(bn, seq, TOK_TILE)
    hw, wd = RW_WIDTH, PA_W
    tok = lambda w: pl.BlockSpec((bb, ll, w), lambda b, l: (b, l, 0))
    row = lambda w: pl.BlockSpec((bb, 1, w), lambda b, l: (b, 0, 0))
    full = lambda a: pl.BlockSpec(a.shape, lambda b, l: (0,) * a.ndim)
    consts = (p["mu"], p["w0"], p["wup"], p["a0"], p["aup"], p["gup"], p["kk"], p["ka"], p["rk"], p["bd64"])
    proj_w = (QKV_W, XAL_W, GG_W, MG_W)
    shapes = lambda ws: [jax.ShapeDtypeStruct((bn, seq, w), F32) for w in ws]
    return pl.pallas_call(
        _inproj_prep_body,
        grid=(bn // bb, seq // ll),
        in_specs=[tok(d), pl.BlockSpec((bb, 6, d), lambda b, l: (b, 0, 0)), full(p["norm1_g"])]
        + [full(w) for w in p["w_in"]] + [row(wd)] + [full(c) for c in consts],
        out_specs=[tok(w) for w in proj_w] + [tok(hw)] * 8 + [row(wd)],
        out_shape=shapes(proj_w) + shapes((hw,) * 8) + [jax.ShapeDtypeStruct((bn, 1, wd), F32)],
        scratch_shapes=[pltpu.VMEM((bb, ll, wd), F32), pltpu.VMEM((bb, 1, wd), F32)],
        compiler_params=_cparams(("arbitrary", "arbitrary"), VMEM_LIMIT),
        name="norm_inproj_prep",
    )(x, mod, p["norm1_g"], *p["w_in"], s_sh.reshape(bn, 1, wd), *consts)


def _unit_masks(n, tl):
    ri = lax.broadcasted_iota(I32, (n, n), 0)
    ci = lax.broadcasted_iota(I32, (n, n), 1)
    same = (ri >> _log2(tl)) == (ci >> _log2(tl))
    return same, same & (ri > ci), same & (ri >= ci)


def _rwscan_body(r_ref, lw_ref, k_ref, v_ref, a_ref, b_ref, s0_ref, y_ref, sn_ref, st, *, nu, nseq, tl, passes):
    n = nseq * tl
    n2 = 2 * n
    p_aa, p_inv, p_apply, p_state, p_y = passes

    hd = RW_HEAD

    @pl.when(pl.program_id(1) == 0)
    def _():
        zero = jnp.zeros((hd, hd), F32)
        for q in range(nu * nseq):
            for p in range(RW_HEADS // 2):
                st[q, p] = jnp.concatenate(
                    [jnp.concatenate([s0_ref[q, 2 * p], zero], axis=1),
                     jnp.concatenate([zero, s0_ref[q, 2 * p + 1]], axis=1)], axis=0)

    same, _, incl = _unit_masks(n, tl)
    m_cum = jnp.where(incl, 1.0, 0.0)
    m_seq = jnp.where(same, 1.0, 0.0)
    ri = lax.broadcasted_iota(I32, (n2, n2), 0)
    ci = lax.broadcasted_iota(I32, (n2, n2), 1)
    rt, ct = ri & (n - 1), ci & (n - 1)
    dsame = ((rt >> _log2(tl)) == (ct >> _log2(tl))) & ((ri >> _log2(n)) == (ci >> _log2(n)))
    strict_d = dsame & (rt > ct)
    incl_d = dsame & (rt >= ct)
    eye_d = jnp.where(ri == ci, 1.0, 0.0)
    lane = lax.broadcasted_iota(I32, (1, LANES), 1)
    m0 = jnp.where(lane < RW_HEAD, 1.0, 0.0)
    m1 = 1.0 - m0

    def dup(x):
        return jnp.concatenate([x * m0, x * m1], axis=0)

    def seq_rows(x, q):
        if nseq == 1:
            return x
        return jnp.concatenate([x[q * tl:(q + 1) * tl], x[n + q * tl:n + (q + 1) * tl]], axis=0)

    def unit_rows(parts):
        if nseq == 1:
            return parts[0]
        return jnp.concatenate([p[0:tl] for p in parts] + [p[tl:2 * tl] for p in parts], axis=0)

    chains = [(u, p) for u in range(nu) for p in range(RW_HEADS // 2)]
    ids = range(len(chains))
    cat0 = lambda *xs: jnp.concatenate(xs, axis=0)

    def ld(ref, c):
        u, p = chains[c]
        return ref[u * nseq:(u + 1) * nseq, :, p * LANES:(p + 1) * LANES].reshape(n, LANES)

    lw = [ld(lw_ref, c) for c in ids]
    cum = [_mm01(m_cum, x) for x in lw]
    tot = [_mm01(m_seq, x) for x in lw]
    e_c = [jnp.exp(x) for x in cum]
    e_n = [jnp.exp(-x) for x in cum]
    e_l = [jnp.exp(t - x) for t, x in zip(tot, cum)]
    at_d = [dup(ld(a_ref, c) * jnp.exp(cum[c] - lw[c])) for c in ids]
    rt_d = [dup(ld(r_ref, c) * e_c[c]) for c in ids]
    bt_d = [dup(ld(b_ref, c) * e_n[c]) for c in ids]
    kt_d = [dup(ld(k_ref, c) * e_n[c]) for c in ids]
    bh_d = [dup(ld(b_ref, c) * e_l[c]) for c in ids]
    kh_d = [dup(ld(k_ref, c) * e_l[c]) for c in ids]
    v_d = [dup(ld(v_ref, c)) for c in ids]
    aa = [_mm(cat0(at_d[c], rt_d[c]), cat0(bt_d[c], kt_d[c]), "nt", p_aa) for c in ids]
    a_ab = [jnp.where(strict_d, x[0:n2, 0:n2], 0.0) for x in aa]
    a_ak = [jnp.where(strict_d, x[0:n2, n2:], 0.0) for x in aa]
    a_rb = [jnp.where(incl_d, x[n2:, 0:n2], 0.0) for x in aa]
    a_rk = [jnp.where(incl_d, x[n2:, n2:], 0.0) for x in aa]
    zy = [_mm(cat0(a_ak[c], a_rk[c]), v_d[c], passes=p_apply) for c in ids]
    tinv = [eye_d + x for x in a_ab]
    nk = a_ab
    for _ in range(_log2(tl) - 1):
        nk = [_mm(x, x, passes=p_inv) for x in nk]
        tinv = [t + _mm(t, x, passes=p_inv) for t, x in zip(tinv, nk)]
    wu = [_mm(tinv[c], jnp.concatenate([at_d[c], zy[c][0:n2]], axis=1), passes=p_apply) for c in ids]
    seqs = range(nseq)
    srow = lambda c, q: (chains[c][0] * nseq + q, chains[c][1])
    s_old = [[st[srow(c, q)] for q in seqs] for c in ids]
    xs = [[_mm(cat0(seq_rows(wu[c][:, 0:LANES], q), seq_rows(rt_d[c], q)), s_old[c][q], "nt", p_state)
           for q in seqs] for c in ids]
    u_q = [[xs[c][q][0:2 * tl] + seq_rows(wu[c][:, LANES:], q) for q in seqs] for c in ids]
    for c in ids:
        for q in seqs:
            g_c = jnp.exp(tot[c][q * tl:q * tl + 1, :])
            st[srow(c, q)] = s_old[c][q] * g_c + _mm(cat0(u_q[c][q], seq_rows(v_d[c], q)),
                                                     cat0(seq_rows(bh_d[c], q), seq_rows(kh_d[c], q)), "tn", p_state)
    for c in ids:
        u, p = chains[c]
        y_d = (unit_rows([xs[c][q][2 * tl:] for q in seqs]) + _mm(a_rb[c], unit_rows(u_q[c]), passes=p_y)
               + zy[c][n2:])
        y_ref[u * nseq:(u + 1) * nseq, :, p * LANES:(p + 1) * LANES] = (y_d[0:n] + y_d[n:]).reshape(nseq, tl, LANES)

    @pl.when(pl.program_id(1) == pl.num_programs(1) - 1)
    def _():
        for q in range(nu * nseq):
            for p in range(RW_HEADS // 2):
                s = st[q, p]
                sn_ref[q, 2 * p] = s[0:hd, 0:hd]
                sn_ref[q, 2 * p + 1] = s[hd:, hd:]


def _unit_shape(bn, seq):
    if seq >= UNIT:
        assert seq % UNIT == 0
        return 1, UNIT
    assert UNIT % seq == 0 and bn % (UNIT // seq) == 0
    return UNIT // seq, seq


def _rwscan_call(r, lw, k2, v, a_s, b_s, s0, passes=RW_SCAN_PASSES):
    bn, seq, hw = r.shape
    nseq, tl = _unit_shape(bn, seq)
    nu = RW_UNITS_PER_STEP if bn % (RW_UNITS_PER_STEP * nseq) == 0 else 1
    rows = nu * nseq
    tok = pl.BlockSpec((rows, tl, hw), lambda b, c: (b, c, 0))
    stt = pl.BlockSpec((rows, RW_HEADS, RW_HEAD, RW_HEAD), lambda b, c: (b, 0, 0, 0))
    return pl.pallas_call(
        functools.partial(_rwscan_body, nu=nu, nseq=nseq, tl=tl, passes=passes),
        grid=(bn // rows, seq // tl),
        in_specs=[tok] * 6 + [stt],
        out_specs=[tok, stt],
        out_shape=[jax.ShapeDtypeStruct((bn, seq, hw), F32), jax.ShapeDtypeStruct(s0.shape, F32)],
        scratch_shapes=[pltpu.VMEM((rows, RW_HEADS // 2, LANES, LANES), F32)],
        compiler_params=_cparams(("arbitrary", "arbitrary"), VMEM_LIMIT),
        name="rwkv_scan",
    )(r, lw, k2, v, a_s, b_s, s0)


def _gla_body(qkv_ref, xal_ref, gate_ref, aup_ref, ab_ref, ng_ref, s0_ref, o_ref, sn_ref, st, *, nu, nseq, tl, cs):
    n = nseq * tl
    n2 = 2 * n
    nsub = tl // cs

    @pl.when(pl.program_id(1) == 0)
    def _():
        zero = jnp.zeros((GLA_DV, GLA_DK), F32)
        for q in range(nu * nseq):
            for p in range(GLA_HEADS // 2):
                st[q, p] = jnp.concatenate(
                    [jnp.concatenate([s0_ref[q, 2 * p].T, zero], axis=1),
                     jnp.concatenate([zero, s0_ref[q, 2 * p + 1].T], axis=1)], axis=0)

    same, _, incl = _unit_masks(n, cs)
    m_cum = jnp.where(incl, 1.0, 0.0)
    m_sub = jnp.where(same, 1.0, 0.0)
    ri = lax.broadcasted_iota(I32, (n2, n2), 0)
    ci = lax.broadcasted_iota(I32, (n2, n2), 1)
    rt, ct = ri & (n - 1), ci & (n - 1)
    causal_d = ((rt >> _log2(cs)) == (ct >> _log2(cs))) & ((ri >> _log2(n)) == (ci >> _log2(n))) & (rt >= ct)
    lane = lax.broadcasted_iota(I32, (1, LANES), 1)
    m0 = jnp.where(lane < GLA_DK, 1.0, 0.0)
    m1 = 1.0 - m0
    sr = lax.broadcasted_iota(I32, (2 * GLA_DV, LANES), 0)
    sc = lax.broadcasted_iota(I32, (2 * GLA_DV, LANES), 1)
    st_mask = jnp.where((sr >> _log2(GLA_DV)) == (sc >> _log2(GLA_DK)), 1.0, 0.0)

    def dup(x):
        return jnp.concatenate([x * m0, x * m1], axis=0)

    chains = [(u, p) for u in range(nu) for p in range(GLA_HEADS // 2)]
    ids = range(len(chains))
    urows = lambda u: slice(u * nseq, (u + 1) * nseq)
    ng = ng_ref[...]
    la_all = [-_softplus(-(_mm(xal_ref[urows(u), :, :].reshape(n, LANES), aup_ref[...], passes=3) + ab_ref[...]))
              * (1.0 / GLA_GATE_TAU) for u in range(nu)]

    def ld(ref, c, off, width):
        return ref[urows(chains[c][0]), :, off:off + width].reshape(n, width)

    q = [ld(qkv_ref, c, chains[c][1] * LANES, LANES) * (GLA_DK ** -0.5) for c in ids]
    k = [ld(qkv_ref, c, GLA_KW + chains[c][1] * LANES, LANES) for c in ids]
    vp = [ld(qkv_ref, c, 2 * GLA_KW + chains[c][1] * 2 * GLA_DV, 2 * GLA_DV) for c in ids]
    la = [la_all[u][:, p * LANES:(p + 1) * LANES] for u, p in chains]
    bc = [_mm01(m_cum, x) for x in la]
    bl = [_mm01(m_sub, x) for x in la]
    qe = [q[c] * jnp.exp(bc[c]) for c in ids]
    ke = [k[c] * jnp.exp(-bc[c]) for c in ids]
    kd = [k[c] * jnp.exp(bl[c] - bc[c]) for c in ids]
    att = [jnp.where(causal_d, _mm(dup(qe[c]), dup(ke[c]), "nt", passes=1), 0.0) for c in ids]
    v_st = [jnp.concatenate([x[:, 0:GLA_DV], x[:, GLA_DV:]], axis=0) for x in vp]
    o_st = [_mm(att[c], v_st[c], passes=1) for c in ids]
    upd = [[_mm(vp[c][r0:r0 + cs], kd[c][r0:r0 + cs], "tn", passes=1) for r0 in range(0, n, cs)] for c in ids]
    inter = [[None] * (n // cs) for _ in ids]
    for sq in range(nseq):
        s = [st[chains[c][0] * nseq + sq, chains[c][1]] for c in ids]
        for j in range(nsub):
            i = sq * nsub + j
            r0 = i * cs
            for c in ids:
                inter[c][i] = _mm(qe[c][r0:r0 + cs], s[c], "nt", passes=1)
                s[c] = s[c] * jnp.exp(bl[c][r0:r0 + 1, :]) + st_mask * upd[c][i]
        for c in ids:
            st[chains[c][0] * nseq + sq, chains[c][1]] = s[c]
    for c in ids:
        u, p = chains[c]
        o = o_st[c] + jnp.concatenate([x[:, 0:GLA_DV] for x in inter[c]] + [x[:, GLA_DV:] for x in inter[c]], axis=0)
        o = o * lax.rsqrt(jnp.mean(o * o, axis=-1, keepdims=True) + NORM_EPS) * ng
        goff = p * 2 * GLA_DV
        gp = ld(gate_ref, c, goff, 2 * GLA_DV)
        g_st = jnp.concatenate([gp[:, 0:GLA_DV], gp[:, GLA_DV:]], axis=0)
        ob = o * (g_st * _sigmoid(g_st))
        o_ref[urows(u), :, goff:goff + GLA_DV] = ob[0:n].reshape(nseq, tl, GLA_DV)
        o_ref[urows(u), :, goff + GLA_DV:goff + 2 * GLA_DV] = ob[n:].reshape(nseq, tl, GLA_DV)

    @pl.when(pl.program_id(1) == pl.num_programs(1) - 1)
    def _():
        for q in range(nu * nseq):
            for p in range(GLA_HEADS // 2):
                s = st[q, p]
                sn_ref[q, 2 * p] = s[0:GLA_DV, 0:GLA_DK].T
                sn_ref[q, 2 * p + 1] = s[GLA_DV:, GLA_DK:].T


def _gla_call(qkv, xal, gate, s0, p):
    bn, seq, _ = qkv.shape
    nseq, tl = _unit_shape(bn, seq)
    cs = min(GLA_CHUNK, seq)
    assert tl % cs == 0
    nu = GLA_UNITS_PER_STEP if bn % (GLA_UNITS_PER_STEP * nseq) == 0 else 1
    rows = nu * nseq
    tok = lambda w: pl.BlockSpec((rows, tl, w), lambda b, c: (b, c, 0))
    full = lambda a: pl.BlockSpec(a.shape, lambda b, c: (0,) * a.ndim)
    stt = pl.BlockSpec((rows, GLA_HEADS, GLA_DK, GLA_DV), lambda b, c: (b, 0, 0, 0))
    consts = (p["gla_aup"], p["gla_ab"], p["gla_ng"])
    return pl.pallas_call(
        functools.partial(_gla_body, nu=nu, nseq=nseq, tl=tl, cs=cs),
        grid=(bn // rows, seq // tl),
        in_specs=[tok(QKV_W), tok(XAL_W), tok(GG_W)] + [full(c) for c in consts] + [stt],
        out_specs=[tok(GLA_VW), stt],
        out_shape=[jax.ShapeDtypeStruct((bn, seq, GLA_VW), F32), jax.ShapeDtypeStruct(s0.shape, F32)],
        scratch_shapes=[pltpu.VMEM((rows, GLA_HEADS // 2, 2 * GLA_DV, LANES), F32)],
        compiler_params=_cparams(("arbitrary", "arbitrary"), VMEM_LIMIT),
        name="gla_chunked",
    )(qkv, xal, gate, *consts, s0)


def _merge_body(y_ref, g_ref, bon_ref, ob_ref, mg_ref, x_ref, mod_ref, gng_ref, gnb_ref, bd_ref, wpa_ref,
                wpb_ref, wout_ref, n2_ref, rwh_ref, rwl_ref, *rest):
    x1_o, h2_o, lg_o = rest[-3:]
    bb, ll, d = x_ref.shape
    n = bb * ll
    hw = RW_WIDTH
    bd = bd_ref[...]
    y = y_ref[...].reshape(n, hw)
    mu = _xmm01(y, bd, pieces=2) * (1.0 / RW_HEAD)
    dv = y - mu
    var = _xmm01(dv * dv, bd, pieces=1) * (1.0 / RW_HEAD)
    yn = dv * lax.rsqrt(var + RW_GN_EPS) * gng_ref[...] + gnb_ref[...]
    o_a = (yn + bon_ref[...].reshape(n, hw)) * g_ref[...].reshape(n, hw)
    o_b = ob_ref[...].reshape(n, GLA_VW)
    mg = mg_ref[...].reshape(n, 2 * d)
    merged = _sigmoid(mg[:, 0:d]) * _mm(o_a, wpa_ref[...]) + _sigmoid(mg[:, d:]) * _mm(o_b, wpb_ref[...])
    mix = _mm(merged, wout_ref[...]).reshape(bb, ll, d)
    x1 = x_ref[...] + mod_ref[:, 2:3, :] * mix
    x1_o[...] = x1
    yn2 = x1 * lax.rsqrt(jnp.mean(x1 * x1, axis=-1, keepdims=True) + NORM_EPS) * n2_ref[...]
    h2 = (yn2 * (1.0 + mod_ref[:, 4:5, :]) + mod_ref[:, 3:4, :]).reshape(n, d)
    hh, hl = _split(h2, 2)
    rwh, rwl = rwh_ref[...], rwl_ref[...]
    nt = lambda a, b: lax.dot_general(a, b, _DN["nt"], preferred_element_type=F32)
    lg_o[...] = nt(rwh, hh) + nt(rwl, hh) + nt(rwh, hl)
    _rows_to_packed(h2_o, h2)


def _merge_call(y, g, bonus, o_b, mg, x, mod, p, tn, first_tok, shared):
    bn, seq, d = x.shape
    bb, ll = _tile(bn, seq, TOK_TILE)
    nl = seq // ll
    assert first_tok % (bb * ll) == 0
    t0 = first_tok // (bb * ll)
    n_in = 7 + 9
    extra = [] if shared is None else list(shared)
    alias = {} if shared is None else {n_in: 1, n_in + 1: 2}
    tok = lambda w: pl.BlockSpec((bb, ll, w), lambda b, l: (b, l, 0))
    full = lambda a: pl.BlockSpec(a.shape, lambda b, l: (0,) * a.ndim)
    consts = (p["gn_g"], p["gn_b"], p["bd64"], p["w_pa"], p["w_pb"], p["w_out"], p["norm2_g"], p["rw_hi"],
              p["rw_lo"])
    return pl.pallas_call(
        _merge_body,
        grid=(bn // bb, nl),
        in_specs=[tok(RW_WIDTH)] * 3 + [tok(GLA_VW), tok(MG_W), tok(d),
                                        pl.BlockSpec((bb, 6, d), lambda b, l: (b, 0, 0))] + [full(c) for c in consts]
        + [pl.BlockSpec(memory_space=pl.ANY)] * len(extra),
        out_specs=[tok(d),
                   pl.BlockSpec((bb * ll * PCH, LANES), lambda b, l: (t0 + b * nl + l, 0)),
                   pl.BlockSpec((N_EXPERTS, bb * ll), lambda b, l: (0, t0 + b * nl + l))],
        out_shape=[jax.ShapeDtypeStruct((bn, seq, d), F32),
                   jax.ShapeDtypeStruct((tn * PCH, LANES), I32),
                   jax.ShapeDtypeStruct((N_EXPERTS, tn), F32)],
        input_output_aliases=alias,
        compiler_params=_cparams(("arbitrary", "arbitrary"), VMEM_LIMIT),
        name="merge_outproj_router",
    )(y, g, bonus, o_b, mg, x, mod, *consts, *extra)


def _route_body(lg_ref, rb_ref, e_o, rk_o, w_o, cnt_o, carry):
    ne, tm = lg_ref.shape

    @pl.when(pl.program_id(0) == 0)
    def _():
        carry[...] = jnp.zeros_like(carry)

    neg = -jnp.inf
    scores = _sigmoid(lg_ref[...])
    sel = scores + rb_ref[...]
    row_i = lax.broadcasted_iota(I32, (ne, tm), 0)
    row = row_i.astype(F32)
    grp = (row_i >> _log2(GROUP_SIZE)).astype(F32)

    def first_max(x, ids, none):
        m = jnp.max(x, axis=0, keepdims=True)
        return m, jnp.min(jnp.where(x == m, ids, none), axis=0, keepdims=True)

    gs = []
    gids = lax.broadcasted_iota(I32, (GROUP_SIZE, tm), 0)
    for gidx in range(N_GROUPS):
        rows = slice(gidx * GROUP_SIZE, (gidx + 1) * GROUP_SIZE)
        sg = _sigmoid(lg_ref[rows, :]) + rb_ref[rows, :]
        ids = (gids + gidx * GROUP_SIZE).astype(F32)
        m1, i1 = first_max(sg, ids, float(ne))
        gs.append(m1 + jnp.max(jnp.where(ids == i1, neg, sg), axis=0, keepdims=True))
    gs = jnp.concatenate(gs, axis=0)
    gid = lax.broadcasted_iota(I32, (N_GROUPS, tm), 0).astype(F32)
    cur = jnp.full((ne, tm), neg, F32)
    for _ in range(TOPK_GROUPS):
        _, gi = first_max(gs, gid, float(N_GROUPS))
        cur = jnp.where(grp == gi, sel, cur)
        gs = jnp.where(gid == gi, neg, gs)

    pm = jnp.zeros((ne, tm), F32)
    eidx, wts = [], []
    for _ in range(TOP_K):
        _, ei = first_max(cur, row, float(ne))
        hit = row == ei
        pm = jnp.where(hit, 1.0, pm)
        eidx.append(ei)
        wts.append(jnp.sum(jnp.where(hit, scores, 0.0), axis=0, keepdims=True))
        cur = jnp.where(hit, neg, cur)
    wsum = wts[0]
    for w in wts[1:]:
        wsum = wsum + w

    ri = lax.broadcasted_iota(I32, (tm, tm), 0)
    ci = lax.broadcasted_iota(I32, (tm, tm), 1)
    earlier = jnp.where(ri < ci, 1.0, 0.0)
    rank = _mm(pm, earlier, passes=1) + carry[...]
    carry[...] = carry[...] + jnp.sum(pm, axis=1, keepdims=True)
    cnt_o[...] = carry[...]

    rks = [jnp.sum(jnp.where(row == e, rank, 0.0), axis=0, keepdims=True) for e in eidx]
    e_o[0] = jnp.concatenate(eidx, axis=0).astype(I32)
    rk_o[0] = jnp.concatenate(rks, axis=0).astype(I32)
    w_o[0] = jnp.concatenate([w / wsum * ROUTED_SCALE for w in wts], axis=0)


def _route_call(logits_t, router_b):
    ne, tn = logits_t.shape
    tm = TOK_TILE
    assert tn % tm == 0
    col = pl.BlockSpec((ne, 1), lambda i: (0, 0))
    tab = pl.BlockSpec((1, TOP_K, tm), lambda i: (i, 0, 0))
    tab_shape = (tn // tm, TOP_K, tm)
    return pl.pallas_call(
        _route_body,
        grid=(tn // tm,),
        in_specs=[pl.BlockSpec((ne, tm), lambda i: (0, i)), col],
        out_specs=[tab, tab, tab, col],
        out_shape=[jax.ShapeDtypeStruct(tab_shape, I32), jax.ShapeDtypeStruct(tab_shape, I32),
                   jax.ShapeDtypeStruct(tab_shape, F32), jax.ShapeDtypeStruct((ne, 1), F32)],
        scratch_shapes=[pltpu.VMEM((ne, 1), F32)],
        compiler_params=_cparams(("arbitrary",)),
        name="moe_route",
    )(logits_t, router_b.reshape(ne, 1))


def _dest_body(e_ref, rk_ref, ps_ref, d_o):
    ne, tm = ps_ref.shape[0], e_ref.shape[2]
    ids = lax.broadcasted_iota(I32, (ne, tm), 0)
    ps = ps_ref[...]
    for t in range(e_ref.shape[0]):
        first = [jnp.sum(jnp.where(ids == e_ref[t, kk:kk + 1, :], ps, 0.0), axis=0, keepdims=True)
                 for kk in range(TOP_K)]
        d_o[t] = (jnp.concatenate(first, axis=0).astype(I32) + rk_ref[t]) * PCH


def _dest_call(eidx, rank, pad_start):
    nt, _, tm = eidx.shape
    ne = pad_start.shape[0]
    per = next(k for k in (4, 2, 1) if nt % k == 0)
    tab = pl.BlockSpec((per, TOP_K, tm), lambda i: (i, 0, 0))
    return pl.pallas_call(
        _dest_body,
        grid=(nt // per,),
        in_specs=[tab, tab, pl.BlockSpec((ne, 1), lambda i: (0, 0))],
        out_specs=tab,
        out_shape=jax.ShapeDtypeStruct(eidx.shape, I32),
        compiler_params=_cparams(("arbitrary",)),
        name="moe_dest",
    )(eidx, rank, pad_start.astype(F32).reshape(ne, 1))


def _pslab(ref, offset):
    return ref.at[pl.ds(pl.multiple_of(offset, PCH), PCH)]


def _dispatch_body(d_ref, h2_ref, xs_hbm, sem, *, tm):
    def issue(m, carry):
        for kk in range(TOP_K):
            pltpu.make_async_copy(_slab(h2_ref, m), _pslab(xs_hbm, d_ref[0, 0, m * TOP_K + kk]), sem).start(priority=kk % 2)
        return carry

    lax.fori_loop(0, tm, issue, 0)
    all_rows = xs_hbm.at[pl.ds(0, tm * TOP_K * PCH)]
    pltpu.make_async_copy(all_rows, all_rows, sem).wait()


def _assign_spec(tm, index_map):
    return pl.BlockSpec((1, 1, tm * TOP_K), index_map, memory_space=pltpu.SMEM)


def _dispatch_call(dest, h2s, n_rows):
    tn = h2s.shape[0] // PCH
    tm = TOK_TILE
    assert dest.shape == (tn // tm, 1, tm * TOP_K)
    blk = _assign_spec(tm, lambda i: (i, 0, 0))
    return pl.pallas_call(
        functools.partial(_dispatch_body, tm=tm),
        grid=(tn // tm,),
        in_specs=[blk, pl.BlockSpec((tm * PCH, LANES), lambda i: (i, 0))],
        out_specs=pl.BlockSpec(memory_space=pl.ANY),
        out_shape=jax.ShapeDtypeStruct((n_rows * PCH, LANES), I32),
        scratch_shapes=[pltpu.SemaphoreType.DMA],
        compiler_params=_cparams(("arbitrary",)),
        name="moe_dispatch",
    )(dest, h2s)


def _expert_body(bi_ref, nr_ref, ld_ref, nx_ref, xs_hbm, wg_hbm, wu_hbm, wd_hbm, ob_ref, wg_buf, wu_buf, wd_buf,
                 wg_bf, wu_bf, wd_bf, xbuf, sem, xsem):
    i = pl.program_id(0)
    nsteps = pl.num_programs(0)
    nr = nr_ref[i]
    slot = ld_ref[i]
    blk_rows = MOE_BLK * PCH

    def row_block(j):
        s = lax.rem(j, ROW_SLOTS)
        src = xs_hbm.at[pl.ds(pl.multiple_of(bi_ref[j] * blk_rows, blk_rows), blk_rows)]
        return pltpu.make_async_copy(src, xbuf.at[s], xsem.at[s])

    @pl.when(i == 0)
    def _():
        for j in range(ROW_SLOTS - 1):
            row_block(j).start()

    @pl.when(i + ROW_SLOTS - 1 < nsteps)
    def _():
        row_block(i + ROW_SLOTS - 1).start()

    def fetch(e, s):
        return (pltpu.make_async_copy(wg_hbm.at[e], wg_buf.at[s], sem.at[s]),
                pltpu.make_async_copy(wu_hbm.at[e], wu_buf.at[s], sem.at[s]),
                pltpu.make_async_copy(wd_hbm.at[e], wd_buf.at[s], sem.at[s]))

    @pl.when(i == 0)
    def _():
        for s in range(WEIGHT_SLOTS - 1):
            e0 = nx_ref[nx_ref.shape[0] - (WEIGHT_SLOTS - 1) + s]

            @pl.when(e0 >= 0)
            def _():
                for k, cp in enumerate(fetch(e0, s)):
                    cp.start(priority=k % 2)

    @pl.when(slot >= 0)
    def _():
        for cp in fetch(0, slot):
            cp.wait()

        @pl.when(nx_ref[i] >= 0)
        def _():
            for k, cp in enumerate(fetch(nx_ref[i], lax.rem(slot + WEIGHT_SLOTS - 1, WEIGHT_SLOTS))):
                cp.start(priority=k % 2)

        wg_bf[...] = wg_buf[slot].astype(BF16)
        wu_bf[...] = wu_buf[slot].astype(BF16)
        wd_bf[...] = wd_buf[slot].astype(BF16)

    row_block(i).wait()

    @pl.when(nr > 0)
    def _():
        part = MOE_BLK // EXPERT_PARTS
        firsts = [q * part for q in range(EXPERT_PARTS)]
        rid = lax.broadcasted_iota(I32, (part, LANES), 0)
        xs_ref = xbuf.at[lax.rem(i, ROW_SLOTS)]
        x = [_rows_from_packed(xs_ref, part, rid < nr - f, f) for f in firsts]
        hg = [jnp.dot(v, wg_bf[...], preferred_element_type=F32) for v in x]
        hu = [jnp.dot(v, wu_bf[...], preferred_element_type=F32) for v in x]
        hh = [(g * _sigmoid(g) * u).astype(BF16) for g, u in zip(hg, hu)]
        out = [jnp.dot(v, wd_bf[...], preferred_element_type=F32) for v in hh]
        for f, v in zip(firsts, out):
            _rows_to_packed(ob_ref, v, f)


def _expert_tables(counts, pad_start, pad_end, nb):
    ne = counts.shape[0]
    experts = jnp.arange(ne, dtype=I32)
    first_row = jnp.arange(nb, dtype=I32) * MOE_BLK
    block_e = jnp.minimum(jnp.sum(pad_end[None, :] <= first_row[:, None], axis=1), ne - 1).astype(I32)
    mine = block_e[:, None] == experts[None, :]
    pick = lambda v: jnp.sum(jnp.where(mine, v[None, :], 0), axis=1)
    has = counts > 0
    ordinal = jnp.cumsum(has.astype(I32)) - 1
    start_b, count_b, ord_b = pick(pad_start), pick(counts), pick(ordinal)
    block_rows = jnp.clip(start_b + count_b - first_row, 0, MOE_BLK).astype(I32)
    block_i = jnp.minimum(jnp.arange(nb, dtype=I32), pad_end[-1] // MOE_BLK - 1).astype(I32)
    starts = (first_row == start_b) & (block_rows > 0)
    load_slot = jnp.where(starts, ord_b % WEIGHT_SLOTS, -1).astype(I32)
    nth = lambda want: jnp.max(jnp.where(has[None, :] & (ordinal[None, :] == want[:, None]), experts[None, :], -1),
                               axis=1)
    ahead = jnp.where(starts, nth(ord_b + WEIGHT_SLOTS - 1), -1)
    lead = nth(jnp.arange(WEIGHT_SLOTS - 1, dtype=I32))
    return block_i, block_rows, load_slot, jnp.concatenate([ahead, lead]).astype(I32)


def _expert_call(tables, xs, wg, wu, wd):
    nb = xs.shape[0] // (MOE_BLK * PCH)
    assert nb >= ROW_SLOTS
    d, ff = wg.shape[1], wg.shape[2]
    rows = pl.BlockSpec((MOE_BLK * PCH, LANES), lambda i, bi, nr, ld, nx: (bi[i], 0))
    hbm = pl.BlockSpec(memory_space=pl.ANY)
    grid_spec = pltpu.PrefetchScalarGridSpec(
        num_scalar_prefetch=4,
        grid=(nb,),
        in_specs=[hbm, hbm, hbm, hbm],
        out_specs=rows,
        scratch_shapes=[pltpu.VMEM((WEIGHT_SLOTS, d, ff), F32), pltpu.VMEM((WEIGHT_SLOTS, d, ff), F32),
                        pltpu.VMEM((WEIGHT_SLOTS, ff, d), F32),
                        pltpu.VMEM((d, ff), BF16), pltpu.VMEM((d, ff), BF16), pltpu.VMEM((ff, d), BF16),
                        pltpu.VMEM((ROW_SLOTS, MOE_BLK * PCH, LANES), I32),
                        pltpu.SemaphoreType.DMA((WEIGHT_SLOTS,)), pltpu.SemaphoreType.DMA((ROW_SLOTS,))],
    )
    return pl.pallas_call(
        _expert_body,
        grid_spec=grid_spec,
        out_shape=jax.ShapeDtypeStruct(xs.shape, I32),
        compiler_params=_cparams(("arbitrary",), VMEM_LIMIT),
        name="moe_experts",
    )(*tables, xs, wg, wu, wd)


def _combine_body(d_ref, dn_ref, wt_ref, ob_hbm, h2_ref, x1_ref, mod_ref, sg_ref, su_ref,
                  sd_ref, fg_ref, out_ref, gbuf, rbuf, wcol, sem, *, tm, nl):
    bb, ll, d = x1_ref.shape
    step = pl.program_id(0) * nl + pl.program_id(1)
    last = pl.num_programs(0) * nl - 1
    parity = lax.rem(step, 2)
    grp = CMB_GROUP

    def request(d_tab, g, s):
        for j in range(grp):
            m = g * grp + j
            for kk in range(TOP_K):
                pltpu.make_async_copy(_pslab(ob_hbm, d_tab[0, 0, m * TOP_K + kk]), _slab(gbuf.at[s], kk * tm + m),
                                      sem.at[s]).start(priority=kk % 2)

    def mix(g, s):
        r0 = pl.multiple_of(g * grp, grp)
        w = wcol[pl.ds(r0, grp), :]
        wk = [w[:, kk:kk + 1] for kk in range(TOP_K)]
        for c in range(PCH):
            acc_lo = acc_hi = None
            for kk in range(TOP_K):
                words = gbuf[s, pl.ds((kk * tm + r0) * PCH + c, grp, stride=PCH), :]
                lo, hi = _unpack_pair(words)
                acc_lo = wk[kk] * lo if acc_lo is None else acc_lo + wk[kk] * lo
                acc_hi = wk[kk] * hi if acc_hi is None else acc_hi + wk[kk] * hi
            rbuf[pl.ds(r0, grp), c * LANES:(c + 1) * LANES] = acc_lo
            rbuf[pl.ds(r0, grp), (c + PCH) * LANES:(c + PCH + 1) * LANES] = acc_hi

    @pl.when(step == 0)
    def _():
        def first(g, carry):
            request(d_ref, g, 0)
            return carry
        lax.fori_loop(0, tm // grp, first, 0)

    ri = lax.broadcasted_iota(I32, (tm, tm), 0)
    ci = lax.broadcasted_iota(I32, (tm, tm), 1)
    eye = jnp.where(ri == ci, 1.0, 0.0).astype(BF16)
    wc = None
    for piece in _split(wt_ref[0], 3):
        t = lax.dot_general(eye, piece, _DN["nt"], preferred_element_type=F32)
        wc = t if wc is None else wc + t
    wcol[...] = wc

    def run(slot):
        pltpu.make_async_copy(ob_hbm.at[pl.ds(0, tm * TOP_K * PCH)], gbuf.at[slot], sem.at[slot]).wait()

        @pl.when(step < last)
        def _():
            def both(g, carry):
                request(dn_ref, g, 1 - slot)
                mix(g, slot)
                return carry
            lax.fori_loop(0, tm // grp, both, 0)

        @pl.when(step == last)
        def _():
            def only(g, carry):
                mix(g, slot)
                return carry
            lax.fori_loop(0, tm // grp, only, 0)

    for slot in range(2):
        pl.when(parity == slot)(functools.partial(run, slot))

    routed = rbuf[...]
    h2 = _rows_from_packed(h2_ref, tm)
    hg = jnp.dot(h2, sg_ref[...], preferred_element_type=F32)
    hu = jnp.dot(h2, su_ref[...], preferred_element_type=F32)
    shared = jnp.dot((hg * _sigmoid(hg) * hu).astype(BF16), sd_ref[...], preferred_element_type=F32)
    ff = (routed + shared).reshape(bb, ll, d)
    x2 = x1_ref[...] + mod_ref[:, 5:6, :] * ff
    out_ref[...] = x2 * lax.rsqrt(jnp.mean(x2 * x2, axis=-1, keepdims=True) + NORM_EPS) * fg_ref[...]


def _combine_call(dest, wts, first_tok, ob, h2s, x1, mod, p):
    bn, seq, d = x1.shape
    tm = CMB_TILE
    bb, ll = _tile(bn, seq, tm)
    nl = seq // ll
    tn = bn * seq
    nsteps = tn // tm
    assert first_tok % tm == 0 and dest.shape[2] == tm * TOP_K and wts.shape[2] == tm
    tile = lambda g: (first_tok // tm + g, 0, 0)
    smem = _assign_spec(tm, lambda b, l: tile(b * nl + l))
    smem_next = _assign_spec(tm, lambda b, l: tile(jnp.minimum(b * nl + l + 1, nsteps - 1)))
    wblk = pl.BlockSpec((1, TOP_K, tm), lambda b, l: tile(b * nl + l))
    tok = pl.BlockSpec((bb, ll, d), lambda b, l: (b, l, 0))
    full = lambda a: pl.BlockSpec(a.shape, lambda b, l: (0,) * a.ndim)
    consts = (p["sh_gate"], p["sh_up"], p["sh_down"], p["final_g"])
    return pl.pallas_call(
        functools.partial(_combine_body, tm=tm, nl=nl),
        grid=(bn // bb, nl),
        in_specs=[smem, smem_next, wblk, pl.BlockSpec(memory_space=pl.ANY),
                  pl.BlockSpec((tm * PCH, LANES), lambda b, l: (first_tok // tm + b * nl + l, 0)),
                  tok, pl.BlockSpec((bb, 6, d), lambda b, l: (b, 0, 0))] + [full(c) for c in consts],
        out_specs=tok,
        out_shape=jax.ShapeDtypeStruct((bn, seq, d), F32),
        scratch_shapes=[pltpu.VMEM((2, tm * TOP_K * PCH, LANES), I32), pltpu.VMEM((tm, d), F32),
                        pltpu.VMEM((tm, TOP_K), F32), pltpu.SemaphoreType.DMA((2,))],
        compiler_params=_cparams(("arbitrary", "arbitrary"), VMEM_LIMIT),
        name="moe_combine_final",
    )(dest, dest, wts, ob, h2s, x1, mod, *consts)


def _layer_params(l, ada_w, ada_b, norm1_g, norm2_g, w_in, mu_shift, rw_w0, rw_w_up, rw_a0, rw_a_up, rw_g_up,
                  rw_k_k, rw_k_a, rw_r_k, rw_gn_g, rw_gn_b, gla_a_up, gla_a_bias, gla_norm_g, w_pa, w_pb, w_out,
                  router_w, router_b, exp_gate, exp_up, exp_down, sh_gate, sh_up, sh_down):
    d = D_MODEL
    wi = w_in[l]
    gla0 = RW_SHIFT_COLS
    xal0 = gla0 + QKV_W
    pad = jnp.zeros((d, XAL_W - GLA_GATE_RANK), BF16)
    w_pieces = (wi[:, :xal0].astype(BF16),
                jnp.concatenate([wi[:, xal0:xal0 + GLA_GATE_RANK].astype(BF16), pad], axis=1),
                wi[:, xal0 + GLA_GATE_RANK:].astype(BF16))
    zr = jnp.zeros((RW_W_RANK, RW_WIDTH), F32)
    hid = jnp.arange(RW_WIDTH) // RW_HEAD
    row = lambda a: a.reshape(1, -1)
    rw_t = router_w[l].T
    rw_hi = rw_t.astype(BF16)
    return dict(
        ada_w=ada_w[l], ada_b=ada_b[l], norm1_g=norm1_g[l].reshape(1, 1, d),
        norm2_g=norm2_g[l].reshape(1, 1, d), w_in=w_pieces,
        mu=mu_shift[l].reshape(1, 1, -1), w0=row(rw_w0[l]), wup=jnp.concatenate([rw_w_up[l], zr], axis=0),
        a0=row(rw_a0[l]), aup=jnp.concatenate([zr, rw_a_up[l]], axis=0), gup=rw_g_up[l].astype(BF16),
        kk=row(rw_k_k[l]), ka=row(rw_k_a[l]), rk=row(rw_r_k[l]),
        bd64=(hid[:, None] == hid[None, :]).astype(BF16),
        gn_g=row(rw_gn_g[l]), gn_b=row(rw_gn_b[l]),
        gla_aup=jnp.concatenate([gla_a_up[l], jnp.zeros((XAL_W - GLA_GATE_RANK, GLA_KW), F32)], axis=0),
        gla_ab=row(gla_a_bias[l]), gla_ng=row(gla_norm_g[l]),
        w_pa=w_pa[l].astype(BF16), w_pb=w_pb[l].astype(BF16), w_out=w_out[l].astype(BF16),
        rw_hi=rw_hi, rw_lo=(rw_t - rw_hi.astype(F32)).astype(BF16), router_b=router_b[l],
        exp_gate=exp_gate[l], exp_up=exp_up[l], exp_down=exp_down[l],
        sh_gate=sh_gate[l].astype(BF16), sh_up=sh_up[l].astype(BF16), sh_down=sh_down[l].astype(BF16),
    )


def _mixer_group(x, mod, s_rw, s_sh, s_gla, p, tn, first_tok, shared):
    qkv, xal, gg, mg, r, lw, k2, v, a_s, b_s, g, bonus, new_sh = _inproj_prep_call(x, mod, s_sh, p)
    y, rw_new = _rwscan_call(r, lw, k2, v, a_s, b_s, s_rw)
    o_b, gla_new = _gla_call(qkv, xal, gg, s_gla, p)
    x1, h2s, logits = _merge_call(y, g, bonus, o_b, mg, x, mod, p, tn, first_tok, shared)
    states = (rw_new, new_sh[:, 0, :], gla_new)
    return x1, h2s, logits, states


def _moe(h2s, logits, p):
    tn = h2s.shape[0] // PCH
    eidx, rank, wts, counts = _route_call(logits, p["router_b"])
    counts = counts[:, 0].astype(I32)
    padded = (counts + MOE_BLK - 1) // MOE_BLK * MOE_BLK
    pad_end = jnp.cumsum(padded)
    pad_start = (pad_end - padded).astype(I32)
    nb = (tn * TOP_K + N_EXPERTS * (MOE_BLK - 1)) // MOE_BLK + 1
    tables = _expert_tables(counts, pad_start, pad_end, nb)
    dest = _dest_call(eidx, rank, pad_start)
    dest = jnp.swapaxes(dest, 1, 2).reshape(dest.shape[0], 1, -1)
    xs = _dispatch_call(dest, h2s, nb * MOE_BLK)
    ob = _expert_call(tables, xs, p["exp_gate"], p["exp_up"], p["exp_down"])
    return ob, dest, wts


def kernel(x_prompt, x_sample, c_prompt, c_sample, state_rwkv, state_shift, state_gla, ada_w, ada_b, norm1_g,
           norm2_g, w_in, mu_shift, rw_w0, rw_w_up, rw_a0, rw_a_up, rw_g_up, rw_k_k, rw_k_a, rw_r_k, rw_gn_g,
           rw_gn_b, gla_a_up, gla_a_bias, gla_norm_g, w_pa, w_pb, w_out, router_w, router_b, exp_gate, exp_up,
           exp_down, sh_gate, sh_up, sh_down, final_g):
    depth = ada_w.shape[0]
    bp, bs = x_prompt.shape[0], x_sample.shape[0]
    tp = bp * x_prompt.shape[1]
    tn = tp + bs * x_sample.shape[1]
    xs_g = [x_prompt, x_sample]
    c_all = jnp.concatenate([c_prompt, c_sample], axis=0)
    zeros = lambda shape: jnp.zeros(shape, x_prompt.dtype)
    new_states = [[], []]
    fg = final_g.reshape(1, 1, D_MODEL)
    for l in range(depth):
        p = _layer_params(l, ada_w, ada_b, norm1_g, norm2_g, w_in, mu_shift, rw_w0, rw_w_up, rw_a0, rw_a_up,
                          rw_g_up, rw_k_k, rw_k_a, rw_r_k, rw_gn_g, rw_gn_b, gla_a_up, gla_a_bias, gla_norm_g,
                          w_pa, w_pb, w_out, router_w, router_b, exp_gate, exp_up, exp_down, sh_gate, sh_up,
                          sh_down)
        p["final_g"] = fg
        mod_all = _mod_call(c_all, p["ada_w"], p["ada_b"])
        mods = [mod_all[:bp], mod_all[bp:]]
        states_in = [
            (zeros((bp, RW_HEADS, RW_HEAD, RW_HEAD)), zeros((bp, RW_SHIFT_COLS)),
             zeros((bp, GLA_HEADS, GLA_DK, GLA_DV))),
            (state_rwkv[l], state_shift[l], state_gla[l]),
        ]
        x1s, shared = [], None
        firsts = [0, tp]
        for gi in range(2):
            x1, h2_all, lg_all, st = _mixer_group(xs_g[gi], mods[gi], *states_in[gi], p, tn, firsts[gi], shared)
            shared = (h2_all, lg_all)
            x1s.append(x1)
            new_states[gi].append(st)
        ob, dest, wts = _moe(*shared, p)
        assert depth == 1, "the fused final norm assumes a single layer"
        xs_g = [_combine_call(dest, wts, firsts[gi], ob, shared[0], x1s[gi], mods[gi], p) for gi in range(2)]
    stack = lambda gi, j: new_states[gi][0][j][None] if depth == 1 else jnp.stack([s[j] for s in new_states[gi]])
    return (xs_g[0], xs_g[1], stack(0, 0), stack(0, 1), stack(0, 2), stack(1, 0), stack(1, 1), stack(1, 2))
```

```python
import functools

import jax
import jax.numpy as jnp
from jax import lax
from jax.experimental import pallas as pl
from jax.experimental.pallas import tpu as pltpu

F32, BF16, I32 = jnp.float32, jnp.bfloat16, jnp.int32

D_MODEL = 1024
RW_HEADS, RW_HEAD = 8, 64
RW_WIDTH = RW_HEADS * RW_HEAD
RW_W_RANK, RW_A_RANK, RW_G_RANK = 64, 64, 128
RW_GN_EPS = 64e-5
GLA_HEADS, GLA_DK, GLA_DV = 4, 64, 128
GLA_KW, GLA_VW = GLA_HEADS * GLA_DK, GLA_HEADS * GLA_DV
GLA_GATE_RANK = 16
GLA_GATE_TAU = 16.0
GLA_CHUNK = 16
RW_SHIFT_COLS = 3 * RW_WIDTH + RW_W_RANK + RW_A_RANK + RW_G_RANK
N_EXPERTS, TOP_K, N_GROUPS, TOPK_GROUPS = 256, 8, 8, 4
GROUP_SIZE = N_EXPERTS // N_GROUPS
EXPERT_FF = 256
ROUTED_SCALE = 2.5
NORM_EPS = 1e-6

LANES = 128
CHUNKS = D_MODEL // LANES
PCH = CHUNKS // 2
UNIT = 64
RW_SCAN_PASSES = (1, 1, 1, 1, 1)
GLA_UNITS_PER_STEP = 4
RW_UNITS_PER_STEP = 4
VMEM_LIMIT = 56 * 1024 * 1024

PA_W, QKV_W, XAL_W, GG_W, MG_W = RW_SHIFT_COLS, 2 * GLA_KW + GLA_VW, LANES, GLA_VW, 2 * D_MODEL

TOK_TILE = 256
MOE_BLK = 512
EXPERT_PARTS = 1
WEIGHT_SLOTS = 3
ROW_SLOTS = 3
CMB_TILE = 256
CMB_GROUP = 16

_DN = {
    "nn": (((1,), (0,)), ((), ())),
    "nt": (((1,), (1,)), ((), ())),
    "tn": (((0,), (0,)), ((), ())),
}


def _split(x, pieces):
    out, rem = [], x
    for i in range(pieces):
        p = rem.astype(BF16)
        out.append(p)
        if i + 1 < pieces:
            rem = rem - p.astype(F32)
    return out


def _mm(a, b, form="nn", passes=1):
    dn = _DN[form]
    if passes == 6:
        return lax.dot_general(a.astype(F32), b.astype(F32), dn, precision=lax.Precision.HIGHEST,
                               preferred_element_type=F32)
    if passes == 1:
        return lax.dot_general(a.astype(BF16), b.astype(BF16), dn, preferred_element_type=F32)
    ah, al = _split(a, 2)
    bh, bl = _split(b, 2)
    out = lax.dot_general(ah, bh, dn, preferred_element_type=F32)
    out = out + lax.dot_general(ah, bl, dn, preferred_element_type=F32)
    return out + lax.dot_general(al, bh, dn, preferred_element_type=F32)


def _mm01(m01, x, pieces=3):
    m = m01.astype(BF16)
    out = None
    for p in _split(x, pieces):
        t = lax.dot_general(m, p, _DN["nn"], preferred_element_type=F32)
        out = t if out is None else out + t
    return out


def _xmm01(x, m01, pieces=2):
    m = m01.astype(BF16)
    out = None
    for p in _split(x, pieces):
        t = lax.dot_general(p, m, _DN["nn"], preferred_element_type=F32)
        out = t if out is None else out + t
    return out


HI16 = -65536


def _bf16_bits(x):
    return lax.bitcast_convert_type(x.astype(BF16).astype(F32), I32)


def _unpack_pair(w):
    return lax.bitcast_convert_type(w << 16, F32), lax.bitcast_convert_type(w & HI16, F32)


def _rows_to_packed(ref, x, first=0):
    for c in range(PCH):
        lo = _bf16_bits(x[:, c * LANES:(c + 1) * LANES])
        hi = _bf16_bits(x[:, (c + PCH) * LANES:(c + PCH + 1) * LANES])
        ref[pl.ds(first * PCH + c, x.shape[0], stride=PCH), :] = ((lo >> 16) & 0xFFFF) | (hi & HI16)


def _rows_from_packed(ref, n, live=None, first=0):
    lows, highs = [], []
    for c in range(PCH):
        w = ref[pl.ds(first * PCH + c, n, stride=PCH), :]
        if live is not None:
            w = jnp.where(live, w, 0)
        lo, hi = _unpack_pair(w)
        lows.append(lo.astype(BF16))
        highs.append(hi.astype(BF16))
    return jnp.concatenate(lows + highs, axis=1)


def _slab(ref, row):
    return ref.at[pl.ds(pl.multiple_of(row * PCH, PCH), PCH)]


def _sigmoid(x):
    return 1.0 / (1.0 + jnp.exp(-x))


def _softplus(x):
    return jnp.maximum(x, 0.0) + jnp.log(1.0 + jnp.exp(-jnp.abs(x)))


def _log2(n):
    assert n > 0 and n & (n - 1) == 0, n
    return n.bit_length() - 1


def _cparams(sem, vmem=None):
    return pltpu.CompilerParams(dimension_semantics=sem, vmem_limit_bytes=vmem)


def _mod_body(c_ref, w_ref, b_ref, o_ref):
    c = c_ref[...]
    o_ref[0] = _mm(c * _sigmoid(c), w_ref[...], passes=3) + b_ref[...]


def _mod_call(c_all, ada_w, ada_b):
    bt, d = c_all.shape
    out = pl.pallas_call(
        _mod_body,
        grid=(6,),
        in_specs=[pl.BlockSpec((bt, d), lambda k: (0, 0)),
                  pl.BlockSpec((d, d), lambda k: (0, k)),
                  pl.BlockSpec((1, d), lambda k: (0, k))],
        out_specs=pl.BlockSpec((1, bt, d), lambda k: (k, 0, 0)),
        out_shape=jax.ShapeDtypeStruct((6, bt, d), F32),
        compiler_params=_cparams(("arbitrary",)),
        name="adaln_mod",
    )(c_all, ada_w, ada_b.reshape(1, 6 * d))
    return jnp.transpose(out, (1, 0, 2))


def _inproj_body(x_ref, mod_ref, g_ref, wa_ref, wx_ref, wb_ref, pa_ref, qkv_ref, xal_ref, gg_ref, mg_ref):
    bb, ll, d = x_ref.shape
    x = x_ref[...]
    y = x * lax.rsqrt(jnp.mean(x * x, axis=-1, keepdims=True) + NORM_EPS) * g_ref[...]
    h = y * (1.0 + mod_ref[:, 1:2, :]) + mod_ref[:, 0:1, :]
    hb = h.reshape(bb * ll, d).astype(BF16)
    for w_ref, outs in ((wa_ref, (pa_ref, qkv_ref)), (wx_ref, (xal_ref,)), (wb_ref, (gg_ref, mg_ref))):
        off = 0
        for ref in outs:
            w = ref.shape[-1]
            ref[...] = jnp.dot(hb, w_ref[:, off:off + w], preferred_element_type=F32).reshape(bb, ll, w)
            off += w


def _tile(bn, seq, tile):
    if seq >= tile:
        assert seq % tile == 0
        return 1, tile
    assert tile % seq == 0 and bn % (tile // seq) == 0
    return tile // seq, seq


def _rwprep_body(pa_ref, sh_ref, mu_ref, w0_ref, wup_ref, a0_ref, aup_ref, gup_ref, kk_ref, ka_ref, rk_ref,
                 bd_ref, r_o, lw_o, k_o, v_o, a_o, b_o, g_o, bon_o, nsh_o, carry):
    bb, ll, wd = pa_ref.shape
    n = bb * ll
    hw = RW_WIDTH

    @pl.when(pl.program_id(1) == 0)
    def _():
        carry[...] = sh_ref[...]

    pa = pa_ref[...]
    rolled = pltpu.roll(pa.reshape(n, wd), 1, 0).reshape(bb, ll, wd)
    tok = lax.broadcasted_iota(I32, (bb, ll, wd), 1)
    prev = jnp.where(tok == 0, carry[...], rolled)
    last = pa_ref[:, ll - 1:ll, :]
    carry[...] = last
    nsh_o[...] = last
    xs = (pa + (prev - pa) * mu_ref[...]).reshape(n, wd)

    r, k, v = xs[:, 0:hw], xs[:, hw:2 * hw], xs[:, 2 * hw:3 * hw]
    xwa = xs[:, 3 * hw:3 * hw + LANES]
    xg = xs[:, 3 * hw + LANES:]
    w_log = -_softplus(-(w0_ref[...] + _mm(jnp.tanh(xwa), wup_ref[...], passes=3))) - 0.5
    lw = -jnp.exp(w_log)
    a = _sigmoid(a0_ref[...] + _mm(xwa, aup_ref[...], passes=3))
    g = _mm(_sigmoid(xg), gup_ref[...])
    bd = bd_ref[...]
    kkv = k * kk_ref[...]
    kkn = kkv * lax.rsqrt(jnp.maximum(_xmm01(kkv * kkv, bd, pieces=1), 1e-24))
    k2 = k * (1.0 + (a - 1.0) * ka_ref[...])
    bonus = _xmm01(r * k2 * rk_ref[...], bd, pieces=1) * v
    for ref, val in ((r_o, r), (lw_o, lw), (k_o, k2), (v_o, v), (a_o, -kkn), (b_o, kkn * a), (g_o, g),
                     (bon_o, bonus)):
        ref[...] = val.reshape(bb, ll, hw)


def _inproj_prep_body(x_ref, mod_ref, g_ref, wa_ref, wx_ref, wb_ref, sh_ref, mu_ref, w0_ref, wup_ref, a0_ref, aup_ref,
                      gup_ref, kk_ref, ka_ref, rk_ref, bd_ref, qkv_o, xal_o, gg_o, mg_o, r_o, lw_o, k_o, v_o, a_o,
                      b_o, g_o, bon_o, nsh_o, pa_s, carry):
    _inproj_body(x_ref, mod_ref, g_ref, wa_ref, wx_ref, wb_ref, pa_s, qkv_o, xal_o, gg_o, mg_o)
    _rwprep_body(pa_s, sh_ref, mu_ref, w0_ref, wup_ref, a0_ref, aup_ref, gup_ref, kk_ref, ka_ref, rk_ref, bd_ref,
                 r_o, lw_o, k_o, v_o, a_o, b_o, g_o, bon_o, nsh_o, carry)


def _inproj_prep_call(x, mod, s_sh, p):
    bn, seq, d = x.shape
    bb, ll = _tile(bn, seq, TOK_TILE)
    hw, wd = RW_WIDTH, PA_W
    tok = lambda w: pl.BlockSpec((bb, ll, w), lambda b, l: (b, l, 0))
    row = lambda w: pl.BlockSpec((bb, 1, w), lambda b, l: (b, 0, 0))
    full = lambda a: pl.BlockSpec(a.shape, lambda b, l: (0,) * a.ndim)
    consts = (p["mu"], p["w0"], p["wup"], p["a0"], p["aup"], p["gup"], p["kk"], p["ka"], p["rk"], p["bd64"])
    proj_w = (QKV_W, XAL_W, GG_W, MG_W)
    shapes = lambda ws: [jax.ShapeDtypeStruct((bn, seq, w), F32) for w in ws]
    return pl.pallas_call(
        _inproj_prep_body,
        grid=(bn // bb, seq // ll),
        in_specs=[tok(d), pl.BlockSpec((bb, 6, d), lambda b, l: (b, 0, 0)), full(p["norm1_g"])]
        + [full(w) for w in p["w_in"]] + [row(wd)] + [full(c) for c in consts],
        out_specs=[tok(w) for w in proj_w] + [tok(hw)] * 8 + [row(wd)],
        out_shape=shapes(proj_w) + shapes((hw,) * 8) + [jax.ShapeDtypeStruct((bn, 1, wd), F32)],
        scratch_shapes=[pltpu.VMEM((bb, ll, wd), F32), pltpu.VMEM((bb, 1, wd), F32)],
        compiler_params=_cparams(("arbitrary", "arbitrary"), VMEM_LIMIT),
        name="norm_inproj_prep",
    )(x, mod, p["norm1_g"], *p["w_in"], s_sh.reshape(bn, 1, wd), *consts)


def _unit_masks(n, tl):
    ri = lax.broadcasted_iota(I32, (n, n), 0)
    ci = lax.broadcasted_iota(I32, (n, n), 1)
    same = (ri >> _log2(tl)) == (ci >> _log2(tl))
    return same, same & (ri > ci), same & (ri >= ci)


def _rwscan_body(r_ref, lw_ref, k_ref, v_ref, a_ref, b_ref, s0_ref, y_ref, sn_ref, st, *, nu, nseq, tl, passes):
    n = nseq * tl
    n2 = 2 * n
    p_aa, p_inv, p_apply, p_state, p_y = passes

    hd = RW_HEAD

    @pl.when(pl.program_id(1) == 0)
    def _():
        zero = jnp.zeros((hd, hd), F32)
        for q in range(nu * nseq):
            for p in range(RW_HEADS // 2):
                st[q, p] = jnp.concatenate(
                    [jnp.concatenate([s0_ref[q, 2 * p], zero], axis=1),
                     jnp.concatenate([zero, s0_ref[q, 2 * p + 1]], axis=1)], axis=0)

    same, _, incl = _unit_masks(n, tl)
    m_cum = jnp.where(incl, 1.0, 0.0)
    m_seq = jnp.where(same, 1.0, 0.0)
    ri = lax.broadcasted_iota(I32, (n2, n2), 0)
    ci = lax.broadcasted_iota(I32, (n2, n2), 1)
    rt, ct = ri & (n - 1), ci & (n - 1)
    dsame = ((rt >> _log2(tl)) == (ct >> _log2(tl))) & ((ri >> _log2(n)) == (ci >> _log2(n)))
    strict_d = dsame & (rt > ct)
    incl_d = dsame & (rt >= ct)
    eye_d = jnp.where(ri == ci, 1.0, 0.0)
    lane = lax.broadcasted_iota(I32, (1, LANES), 1)
    m0 = jnp.where(lane < RW_HEAD, 1.0, 0.0)
    m1 = 1.0 - m0

    def dup(x):
        return jnp.concatenate([x * m0, x * m1], axis=0)

    def seq_rows(x, q):
        if nseq == 1:
            return x
        return jnp.concatenate([x[q * tl:(q + 1) * tl], x[n + q * tl:n + (q + 1) * tl]], axis=0)

    def unit_rows(parts):
        if nseq == 1:
            return parts[0]
        return jnp.concatenate([p[0:tl] for p in parts] + [p[tl:2 * tl] for p in parts], axis=0)

    chains = [(u, p) for u in range(nu) for p in range(RW_HEADS // 2)]
    ids = range(len(chains))
    cat0 = lambda *xs: jnp.concatenate(xs, axis=0)

    def ld(ref, c):
        u, p = chains[c]
        return ref[u * nseq:(u + 1) * nseq, :, p * LANES:(p + 1) * LANES].reshape(n, LANES)

    lw = [ld(lw_ref, c) for c in ids]
    cum = [_mm01(m_cum, x) for x in lw]
    tot = [_mm01(m_seq, x) for x in lw]
    e_c = [jnp.exp(x) for x in cum]
    e_n = [jnp.exp(-x) for x in cum]
    e_l = [jnp.exp(t - x) for t, x in zip(tot, cum)]
    at_d = [dup(ld(a_ref, c) * jnp.exp(cum[c] - lw[c])) for c in ids]
    rt_d = [dup(ld(r_ref, c) * e_c[c]) for c in ids]
    bt_d = [dup(ld(b_ref, c) * e_n[c]) for c in ids]
    kt_d = [dup(ld(k_ref, c) * e_n[c]) for c in ids]
    bh_d = [dup(ld(b_ref, c) * e_l[c]) for c in ids]
    kh_d = [dup(ld(k_ref, c) * e_l[c]) for c in ids]
    v_d = [dup(ld(v_ref, c)) for c in ids]
    aa = [_mm(cat0(at_d[c], rt_d[c]), cat0(bt_d[c], kt_d[c]), "nt", p_aa) for c in ids]
    a_ab = [jnp.where(strict_d, x[0:n2, 0:n2], 0.0) for x in aa]
    a_ak = [jnp.where(strict_d, x[0:n2, n2:], 0.0) for x in aa]
    a_rb = [jnp.where(incl_d, x[n2:, 0:n2], 0.0) for x in aa]
    a_rk = [jnp.where(incl_d, x[n2:, n2:], 0.0) for x in aa]
    zy = [_mm(cat0(a_ak[c], a_rk[c]), v_d[c], passes=p_apply) for c in ids]
    tinv = [eye_d + x for x in a_ab]
    nk = a_ab
    for _ in range(_log2(tl) - 1):
        nk = [_mm(x, x, passes=p_inv) for x in nk]
        tinv = [t + _mm(t, x, passes=p_inv) for t, x in zip(tinv, nk)]
    wu = [_mm(tinv[c], jnp.concatenate([at_d[c], zy[c][0:n2]], axis=1), passes=p_apply) for c in ids]
    seqs = range(nseq)
    srow = lambda c, q: (chains[c][0] * nseq + q, chains[c][1])
    s_old = [[st[srow(c, q)] for q in seqs] for c in ids]
    xs = [[_mm(cat0(seq_rows(wu[c][:, 0:LANES], q), seq_rows(rt_d[c], q)), s_old[c][q], "nt", p_state)
           for q in seqs] for c in ids]
    u_q = [[xs[c][q][0:2 * tl] + seq_rows(wu[c][:, LANES:], q) for q in seqs] for c in ids]
    for c in ids:
        for q in seqs:
            g_c = jnp.exp(tot[c][q * tl:q * tl + 1, :])
            st[srow(c, q)] = s_old[c][q] * g_c + _mm(cat0(u_q[c][q], seq_rows(v_d[c], q)),
                                                     cat0(seq_rows(bh_d[c], q), seq_rows(kh_d[c], q)), "tn", p_state)
    for c in ids:
        u, p = chains[c]
        y_d = (unit_rows([xs[c][q][2 * tl:] for q in seqs]) + _mm(a_rb[c], unit_rows(u_q[c]), passes=p_y)
               + zy[c][n2:])
        y_ref[u * nseq:(u + 1) * nseq, :, p * LANES:(p + 1) * LANES] = (y_d[0:n] + y_d[n:]).reshape(nseq, tl, LANES)

    @pl.when(pl.program_id(1) == pl.num_programs(1) - 1)
    def _():
        for q in range(nu * nseq):
            for p in range(RW_HEADS // 2):
                s = st[q, p]
                sn_ref[q, 2 * p] = s[0:hd, 0:hd]
                sn_ref[q, 2 * p + 1] = s[hd:, hd:]


def _unit_shape(bn, seq):
    if seq >= UNIT:
        assert seq % UNIT == 0
        return 1, UNIT
    assert UNIT % seq == 0 and bn % (UNIT // seq) == 0
    return UNIT // seq, seq


def _rwscan_call(r, lw, k2, v, a_s, b_s, s0, passes=RW_SCAN_PASSES):
    bn, seq, hw = r.shape
    nseq, tl = _unit_shape(bn, seq)
    nu = RW_UNITS_PER_STEP if bn % (RW_UNITS_PER_STEP * nseq) == 0 else 1
    rows = nu * nseq
    tok = pl.BlockSpec((rows, tl, hw), lambda b, c: (b, c, 0))
    stt = pl.BlockSpec((rows, RW_HEADS, RW_HEAD, RW_HEAD), lambda b, c: (b, 0, 0, 0))
    return pl.pallas_call(
        functools.partial(_rwscan_body, nu=nu, nseq=nseq, tl=tl, passes=passes),
        grid=(bn // rows, seq // tl),
        in_specs=[tok] * 6 + [stt],
        out_specs=[tok, stt],
        out_shape=[jax.ShapeDtypeStruct((bn, seq, hw), F32), jax.ShapeDtypeStruct(s0.shape, F32)],
        scratch_shapes=[pltpu.VMEM((rows, RW_HEADS // 2, LANES, LANES), F32)],
        compiler_params=_cparams(("arbitrary", "arbitrary"), VMEM_LIMIT),
        name="rwkv_scan",
    )(r, lw, k2, v, a_s, b_s, s0)


def _gla_body(qkv_ref, xal_ref, gate_ref, aup_ref, ab_ref, ng_ref, s0_ref, o_ref, sn_ref, st, *, nu, nseq, tl, cs):
    n = nseq * tl
    n2 = 2 * n
    nsub = tl // cs

    @pl.when(pl.program_id(1) == 0)
    def _():
        zero = jnp.zeros((GLA_DV, GLA_DK), F32)
        for q in range(nu * nseq):
            for p in range(GLA_HEADS // 2):
                st[q, p] = jnp.concatenate(
                    [jnp.concatenate([s0_ref[q, 2 * p].T, zero], axis=1),
                     jnp.concatenate([zero, s0_ref[q, 2 * p + 1].T], axis=1)], axis=0)

    same, _, incl = _unit_masks(n, cs)
    m_cum = jnp.where(incl, 1.0, 0.0)
    m_sub = jnp.where(same, 1.0, 0.0)
    ri = lax.broadcasted_iota(I32, (n2, n2), 0)
    ci = lax.broadcasted_iota(I32, (n2, n2), 1)
    rt, ct = ri & (n - 1), ci & (n - 1)
    causal_d = ((rt >> _log2(cs)) == (ct >> _log2(cs))) & ((ri >> _log2(n)) == (ci >> _log2(n))) & (rt >= ct)
    lane = lax.broadcasted_iota(I32, (1, LANES), 1)
    m0 = jnp.where(lane < GLA_DK, 1.0, 0.0)
    m1 = 1.0 - m0
    sr = lax.broadcasted_iota(I32, (2 * GLA_DV, LANES), 0)
    sc = lax.broadcasted_iota(I32, (2 * GLA_DV, LANES), 1)
    st_mask = jnp.where((sr >> _log2(GLA_DV)) == (sc >> _log2(GLA_DK)), 1.0, 0.0)

    def dup(x):
        return jnp.concatenate([x * m0, x * m1], axis=0)

    chains = [(u, p) for u in range(nu) for p in range(GLA_HEADS // 2)]
    ids = range(len(chains))
    urows = lambda u: slice(u * nseq, (u + 1) * nseq)
    ng = ng_ref[...]
    la_all = [-_softplus(-(_mm(xal_ref[urows(u), :, :].reshape(n, LANES), aup_ref[...], passes=3) + ab_ref[...]))
              * (1.0 / GLA_GATE_TAU) for u in range(nu)]

    def ld(ref, c, off, width):
        return ref[urows(chains[c][0]), :, off:off + width].reshape(n, width)

    q = [ld(qkv_ref, c, chains[c][1] * LANES, LANES) * (GLA_DK ** -0.5) for c in ids]
    k = [ld(qkv_ref, c, GLA_KW + chains[c][1] * LANES, LANES) for c in ids]
    vp = [ld(qkv_ref, c, 2 * GLA_KW + chains[c][1] * 2 * GLA_DV, 2 * GLA_DV) for c in ids]
    la = [la_all[u][:, p * LANES:(p + 1) * LANES] for u, p in chains]
    bc = [_mm01(m_cum, x) for x in la]
    bl = [_mm01(m_sub, x) for x in la]
    qe = [q[c] * jnp.exp(bc[c]) for c in ids]
    ke = [k[c] * jnp.exp(-bc[c]) for c in ids]
    kd = [k[c] * jnp.exp(bl[c] - bc[c]) for c in ids]
    att = [jnp.where(causal_d, _mm(dup(qe[c]), dup(ke[c]), "nt", passes=1), 0.0) for c in ids]
    v_st = [jnp.concatenate([x[:, 0:GLA_DV], x[:, GLA_DV:]], axis=0) for x in vp]
    o_st = [_mm(att[c], v_st[c], passes=1) for c in ids]
    upd = [[_mm(vp[c][r0:r0 + cs], kd[c][r0:r0 + cs], "tn", passes=1) for r0 in range(0, n, cs)] for c in ids]
    inter = [[None] * (n // cs) for _ in ids]
    for sq in range(nseq):
        s = [st[chains[c][0] * nseq + sq, chains[c][1]] for c in ids]
        for j in range(nsub):
            i = sq * nsub + j
            r0 = i * cs
            for c in ids:
                inter[c][i] = _mm(qe[c][r0:r0 + cs], s[c], "nt", passes=1)
                s[c] = s[c] * jnp.exp(bl[c][r0:r0 + 1, :]) + st_mask * upd[c][i]
        for c in ids:
            st[chains[c][0] * nseq + sq, chains[c][1]] = s[c]
    for c in ids:
        u, p = chains[c]
        o = o_st[c] + jnp.concatenate([x[:, 0:GLA_DV] for x in inter[c]] + [x[:, GLA_DV:] for x in inter[c]], axis=0)
        o = o * lax.rsqrt(jnp.mean(o * o, axis=-1, keepdims=True) + NORM_EPS) * ng
        goff = p * 2 * GLA_DV
        gp = ld(gate_ref, c, goff, 2 * GLA_DV)
        g_st = jnp.concatenate([gp[:, 0:GLA_DV], gp[:, GLA_DV:]], axis=0)
        ob = o * (g_st * _sigmoid(g_st))
        o_ref[urows(u), :, goff:goff + GLA_DV] = ob[0:n].reshape(nseq, tl, GLA_DV)
        o_ref[urows(u), :, goff + GLA_DV:goff + 2 * GLA_DV] = ob[n:].reshape(nseq, tl, GLA_DV)

    @pl.when(pl.program_id(1) == pl.num_programs(1) - 1)
    def _():
        for q in range(nu * nseq):
            for p in range(GLA_HEADS // 2):
                s = st[q, p]
                sn_ref[q, 2 * p] = s[0:GLA_DV, 0:GLA_DK].T
                sn_ref[q, 2 * p + 1] = s[GLA_DV:, GLA_DK:].T


def _gla_call(qkv, xal, gate, s0, p):
    bn, seq, _ = qkv.shape
    nseq, tl = _unit_shape(bn, seq)
    cs = min(GLA_CHUNK, seq)
    assert tl % cs == 0
    nu = GLA_UNITS_PER_STEP if bn % (GLA_UNITS_PER_STEP * nseq) == 0 else 1
    rows = nu * nseq
    tok = lambda w: pl.BlockSpec((rows, tl, w), lambda b, c: (b, c, 0))
    full = lambda a: pl.BlockSpec(a.shape, lambda b, c: (0,) * a.ndim)
    stt = pl.BlockSpec((rows, GLA_HEADS, GLA_DK, GLA_DV), lambda b, c: (b, 0, 0, 0))
    consts = (p["gla_aup"], p["gla_ab"], p["gla_ng"])
    return pl.pallas_call(
        functools.partial(_gla_body, nu=nu, nseq=nseq, tl=tl, cs=cs),
        grid=(bn // rows, seq // tl),
        in_specs=[tok(QKV_W), tok(XAL_W), tok(GG_W)] + [full(c) for c in consts] + [stt],
        out_specs=[tok(GLA_VW), stt],
        out_shape=[jax.ShapeDtypeStruct((bn, seq, GLA_VW), F32), jax.ShapeDtypeStruct(s0.shape, F32)],
        scratch_shapes=[pltpu.VMEM((rows, GLA_HEADS // 2, 2 * GLA_DV, LANES), F32)],
        compiler_params=_cparams(("arbitrary", "arbitrary"), VMEM_LIMIT),
        name="gla_chunked",
    )(qkv, xal, gate, *consts, s0)


def _merge_body(y_ref, g_ref, bon_ref, ob_ref, mg_ref, x_ref, mod_ref, gng_ref, gnb_ref, bd_ref, wpa_ref,
                wpb_ref, wout_ref, n2_ref, rwh_ref, rwl_ref, *rest):
    x1_o, h2_o, lg_o = rest[-3:]
    bb, ll, d = x_ref.shape
    n = bb * ll
    hw = RW_WIDTH
    bd = bd_ref[...]
    y = y_ref[...].reshape(n, hw)
    mu = _xmm01(y, bd, pieces=2) * (1.0 / RW_HEAD)
    dv = y - mu
    var = _xmm01(dv * dv, bd, pieces=1) * (1.0 / RW_HEAD)
    yn = dv * lax.rsqrt(var + RW_GN_EPS) * gng_ref[...] + gnb_ref[...]
    o_a = (yn + bon_ref[...].reshape(n, hw)) * g_ref[...].reshape(n, hw)
    o_b = ob_ref[...].reshape(n, GLA_VW)
    mg = mg_ref[...].reshape(n, 2 * d)
    merged = _sigmoid(mg[:, 0:d]) * _mm(o_a, wpa_ref[...]) + _sigmoid(mg[:, d:]) * _mm(o_b, wpb_ref[...])
    mix = _mm(merged, wout_ref[...]).reshape(bb, ll, d)
    x1 = x_ref[...] + mod_ref[:, 2:3, :] * mix
    x1_o[...] = x1
    yn2 = x1 * lax.rsqrt(jnp.mean(x1 * x1, axis=-1, keepdims=True) + NORM_EPS) * n2_ref[...]
    h2 = (yn2 * (1.0 + mod_ref[:, 4:5, :]) + mod_ref[:, 3:4, :]).reshape(n, d)
    hh, hl = _split(h2, 2)
    rwh, rwl = rwh_ref[...], rwl_ref[...]
    nt = lambda a, b: lax.dot_general(a, b, _DN["nt"], preferred_element_type=F32)
    lg_o[...] = nt(rwh, hh) + nt(rwl, hh) + nt(rwh, hl)
    _rows_to_packed(h2_o, h2)


def _merge_call(y, g, bonus, o_b, mg, x, mod, p, tn, first_tok, shared):
    bn, seq, d = x.shape
    bb, ll = _tile(bn, seq, TOK_TILE)
    nl = seq // ll
    assert first_tok % (bb * ll) == 0
    t0 = first_tok // (bb * ll)
    n_in = 7 + 9
    extra = [] if shared is None else list(shared)
    alias = {} if shared is None else {n_in: 1, n_in + 1: 2}
    tok = lambda w: pl.BlockSpec((bb, ll, w), lambda b, l: (b, l, 0))
    full = lambda a: pl.BlockSpec(a.shape, lambda b, l: (0,) * a.ndim)
    consts = (p["gn_g"], p["gn_b"], p["bd64"], p["w_pa"], p["w_pb"], p["w_out"], p["norm2_g"], p["rw_hi"],
              p["rw_lo"])
    return pl.pallas_call(
        _merge_body,
        grid=(bn // bb, nl),
        in_specs=[tok(RW_WIDTH)] * 3 + [tok(GLA_VW), tok(MG_W), tok(d),
                                        pl.BlockSpec((bb, 6, d), lambda b, l: (b, 0, 0))] + [full(c) for c in consts]
        + [pl.BlockSpec(memory_space=pl.ANY)] * len(extra),
        out_specs=[tok(d),
                   pl.BlockSpec((bb * ll * PCH, LANES), lambda b, l: (t0 + b * nl + l, 0)),
                   pl.BlockSpec((N_EXPERTS, bb * ll), lambda b, l: (0, t0 + b * nl + l))],
        out_shape=[jax.ShapeDtypeStruct((bn, seq, d), F32),
                   jax.ShapeDtypeStruct((tn * PCH, LANES), I32),
                   jax.ShapeDtypeStruct((N_EXPERTS, tn), F32)],
        input_output_aliases=alias,
        compiler_params=_cparams(("arbitrary", "arbitrary"), VMEM_LIMIT),
        name="merge_outproj_router",
    )(y, g, bonus, o_b, mg, x, mod, *consts, *extra)


def _route_body(lg_ref, rb_ref, e_o, rk_o, w_o, cnt_o, carry):
    ne, tm = lg_ref.shape

    @pl.when(pl.program_id(0) == 0)
    def _():
        carry[...] = jnp.zeros_like(carry)

    neg = -jnp.inf
    scores = _sigmoid(lg_ref[...])
    sel = scores + rb_ref[...]
    row_i = lax.broadcasted_iota(I32, (ne, tm), 0)
    row = row_i.astype(F32)
    grp = (row_i >> _log2(GROUP_SIZE)).astype(F32)

    def first_max(x, ids, none):
        m = jnp.max(x, axis=0, keepdims=True)
        return m, jnp.min(jnp.where(x == m, ids, none), axis=0, keepdims=True)

    gs = []
    gids = lax.broadcasted_iota(I32, (GROUP_SIZE, tm), 0)
    for gidx in range(N_GROUPS):
        rows = slice(gidx * GROUP_SIZE, (gidx + 1) * GROUP_SIZE)
        sg = _sigmoid(lg_ref[rows, :]) + rb_ref[rows, :]
        ids = (gids + gidx * GROUP_SIZE).astype(F32)
        m1, i1 = first_max(sg, ids, float(ne))
        gs.append(m1 + jnp.max(jnp.where(ids == i1, neg, sg), axis=0, keepdims=True))
    gs = jnp.concatenate(gs, axis=0)
    gid = lax.broadcasted_iota(I32, (N_GROUPS, tm), 0).astype(F32)
    cur = jnp.full((ne, tm), neg, F32)
    for _ in range(TOPK_GROUPS):
        _, gi = first_max(gs, gid, float(N_GROUPS))
        cur = jnp.where(grp == gi, sel, cur)
        gs = jnp.where(gid == gi, neg, gs)

    pm = jnp.zeros((ne, tm), F32)
    eidx, wts = [], []
    for _ in range(TOP_K):
        _, ei = first_max(cur, row, float(ne))
        hit = row == ei
        pm = jnp.where(hit, 1.0, pm)
        eidx.append(ei)
        wts.append(jnp.sum(jnp.where(hit, scores, 0.0), axis=0, keepdims=True))
        cur = jnp.where(hit, neg, cur)
    wsum = wts[0]
    for w in wts[1:]:
        wsum = wsum + w

    ri = lax.broadcasted_iota(I32, (tm, tm), 0)
    ci = lax.broadcasted_iota(I32, (tm, tm), 1)
    earlier = jnp.where(ri < ci, 1.0, 0.0)
    rank = _mm(pm, earlier, passes=1) + carry[...]
    carry[...] = carry[...] + jnp.sum(pm, axis=1, keepdims=True)
    cnt_o[...] = carry[...]

    rks = [jnp.sum(jnp.where(row == e, rank, 0.0), axis=0, keepdims=True) for e in eidx]
    e_o[0] = jnp.concatenate(eidx, axis=0).astype(I32)
    rk_o[0] = jnp.concatenate(rks, axis=0).astype(I32)
    w_o[0] = jnp.concatenate([w / wsum * ROUTED_SCALE for w in wts], axis=0)


def _route_call(logits_t, router_b):
    ne, tn = logits_t.shape
    tm = TOK_TILE
    assert tn % tm == 0
    col = pl.BlockSpec((ne, 1), lambda i: (0, 0))
    tab = pl.BlockSpec((1, TOP_K, tm), lambda i: (i, 0, 0))
    tab_shape = (tn // tm, TOP_K, tm)
    return pl.pallas_call(
        _route_body,
        grid=(tn // tm,),
        in_specs=[pl.BlockSpec((ne, tm), lambda i: (0, i)), col],
        out_specs=[tab, tab, tab, col],
        out_shape=[jax.ShapeDtypeStruct(tab_shape, I32), jax.ShapeDtypeStruct(tab_shape, I32),
                   jax.ShapeDtypeStruct(tab_shape, F32), jax.ShapeDtypeStruct((ne, 1), F32)],
        scratch_shapes=[pltpu.VMEM((ne, 1), F32)],
        compiler_params=_cparams(("arbitrary",)),
        name="moe_route",
    )(logits_t, router_b.reshape(ne, 1))


def _dest_body(e_ref, rk_ref, ps_ref, d_o):
    ne, tm = ps_ref.shape[0], e_ref.shape[2]
    ids = lax.broadcasted_iota(I32, (ne, tm), 0)
    ps = ps_ref[...]
    for t in range(e_ref.shape[0]):
        first = [jnp.sum(jnp.where(ids == e_ref[t, kk:kk + 1, :], ps, 0.0), axis=0, keepdims=True)
                 for kk in range(TOP_K)]
        d_o[t] = (jnp.concatenate(first, axis=0).astype(I32) + rk_ref[t]) * PCH


def _dest_call(eidx, rank, pad_start):
    nt, _, tm = eidx.shape
    ne = pad_start.shape[0]
    per = next(k for k in (4, 2, 1) if nt % k == 0)
    tab = pl.BlockSpec((per, TOP_K, tm), lambda i: (i, 0, 0))
    return pl.pallas_call(
        _dest_body,
        grid=(nt // per,),
        in_specs=[tab, tab, pl.BlockSpec((ne, 1), lambda i: (0, 0))],
        out_specs=tab,
        out_shape=jax.ShapeDtypeStruct(eidx.shape, I32),
        compiler_params=_cparams(("arbitrary",)),
        name="moe_dest",
    )(eidx, rank, pad_start.astype(F32).reshape(ne, 1))


def _pslab(ref, offset):
    return ref.at[pl.ds(pl.multiple_of(offset, PCH), PCH)]


def _dispatch_body(d_ref, h2_ref, xs_hbm, sem, *, tm):
    def issue(m, carry):
        for kk in range(TOP_K):
            pltpu.make_async_copy(_slab(h2_ref, m), _pslab(xs_hbm, d_ref[0, 0, m * TOP_K + kk]), sem).start(priority=kk % 2)
        return carry

    lax.fori_loop(0, tm, issue, 0)
    all_rows = xs_hbm.at[pl.ds(0, tm * TOP_K * PCH)]
    pltpu.make_async_copy(all_rows, all_rows, sem).wait()


def _assign_spec(tm, index_map):
    return pl.BlockSpec((1, 1, tm * TOP_K), index_map, memory_space=pltpu.SMEM)


def _dispatch_call(dest, h2s, n_rows):
    tn = h2s.shape[0] // PCH
    tm = TOK_TILE
    assert dest.shape == (tn // tm, 1, tm * TOP_K)
    blk = _assign_spec(tm, lambda i: (i, 0, 0))
    return pl.pallas_call(
        functools.partial(_dispatch_body, tm=tm),
        grid=(tn // tm,),
        in_specs=[blk, pl.BlockSpec((tm * PCH, LANES), lambda i: (i, 0))],
        out_specs=pl.BlockSpec(memory_space=pl.ANY),
        out_shape=jax.ShapeDtypeStruct((n_rows * PCH, LANES), I32),
        scratch_shapes=[pltpu.SemaphoreType.DMA],
        compiler_params=_cparams(("arbitrary",)),
        name="moe_dispatch",
    )(dest, h2s)


def _expert_body(bi_ref, nr_ref, ld_ref, nx_ref, xs_hbm, wg_hbm, wu_hbm, wd_hbm, ob_ref, wg_buf, wu_buf, wd_buf,
                 wg_bf, wu_bf, wd_bf, xbuf, sem, xsem):
    i = pl.program_id(0)
    nsteps = pl.num_programs(0)
    nr = nr_ref[i]
    slot = ld_ref[i]
    blk_rows = MOE_BLK * PCH

    def row_block(j):
        s = lax.rem(j, ROW_SLOTS)
        src = xs_hbm.at[pl.ds(pl.multiple_of(bi_ref[j] * blk_rows, blk_rows), blk_rows)]
        return pltpu.make_async_copy(src, xbuf.at[s], xsem.at[s])

    @pl.when(i == 0)
    def _():
        for j in range(ROW_SLOTS - 1):
            row_block(j).start()

    @pl.when(i + ROW_SLOTS - 1 < nsteps)
    def _():
        row_block(i + ROW_SLOTS - 1).start()

    def fetch(e, s):
        return (pltpu.make_async_copy(wg_hbm.at[e], wg_buf.at[s], sem.at[s]),
                pltpu.make_async_copy(wu_hbm.at[e], wu_buf.at[s], sem.at[s]),
                pltpu.make_async_copy(wd_hbm.at[e], wd_buf.at[s], sem.at[s]))

    @pl.when(i == 0)
    def _():
        for s in range(WEIGHT_SLOTS - 1):
            e0 = nx_ref[nx_ref.shape[0] - (WEIGHT_SLOTS - 1) + s]

            @pl.when(e0 >= 0)
            def _():
                for k, cp in enumerate(fetch(e0, s)):
                    cp.start(priority=k % 2)

    @pl.when(slot >= 0)
    def _():
        for cp in fetch(0, slot):
            cp.wait()

        @pl.when(nx_ref[i] >= 0)
        def _():
            for k, cp in enumerate(fetch(nx_ref[i], lax.rem(slot + WEIGHT_SLOTS - 1, WEIGHT_SLOTS))):
                cp.start(priority=k % 2)

        wg_bf[...] = wg_buf[slot].astype(BF16)
        wu_bf[...] = wu_buf[slot].astype(BF16)
        wd_bf[...] = wd_buf[slot].astype(BF16)

    row_block(i).wait()

    @pl.when(nr > 0)
    def _():
        part = MOE_BLK // EXPERT_PARTS
        firsts = [q * part for q in range(EXPERT_PARTS)]
        rid = lax.broadcasted_iota(I32, (part, LANES), 0)
        xs_ref = xbuf.at[lax.rem(i, ROW_SLOTS)]
        x = [_rows_from_packed(xs_ref, part, rid < nr - f, f) for f in firsts]
        hg = [jnp.dot(v, wg_bf[...], preferred_element_type=F32) for v in x]
        hu = [jnp.dot(v, wu_bf[...], preferred_element_type=F32) for v in x]
        hh = [(g * _sigmoid(g) * u).astype(BF16) for g, u in zip(hg, hu)]
        out = [jnp.dot(v, wd_bf[...], preferred_element_type=F32) for v in hh]
        for f, v in zip(firsts, out):
            _rows_to_packed(ob_ref, v, f)


def _expert_tables(counts, pad_start, pad_end, nb):
    ne = counts.shape[0]
    experts = jnp.arange(ne, dtype=I32)
    first_row = jnp.arange(nb, dtype=I32) * MOE_BLK
    block_e = jnp.minimum(jnp.sum(pad_end[None, :] <= first_row[:, None], axis=1), ne - 1).astype(I32)
    mine = block_e[:, None] == experts[None, :]
    pick = lambda v: jnp.sum(jnp.where(mine, v[None, :], 0), axis=1)
    has = counts > 0
    ordinal = jnp.cumsum(has.astype(I32)) - 1
    start_b, count_b, ord_b = pick(pad_start), pick(counts), pick(ordinal)
    block_rows = jnp.clip(start_b + count_b - first_row, 0, MOE_BLK).astype(I32)
    block_i = jnp.minimum(jnp.arange(nb, dtype=I32), pad_end[-1] // MOE_BLK - 1).astype(I32)
    starts = (first_row == start_b) & (block_rows > 0)
    load_slot = jnp.where(starts, ord_b % WEIGHT_SLOTS, -1).astype(I32)
    nth = lambda want: jnp.max(jnp.where(has[None, :] & (ordinal[None, :] == want[:, None]), experts[None, :], -1),
                               axis=1)
    ahead = jnp.where(starts, nth(ord_b + WEIGHT_SLOTS - 1), -1)
    lead = nth(jnp.arange(WEIGHT_SLOTS - 1, dtype=I32))
    return block_i, block_rows, load_slot, jnp.concatenate([ahead, lead]).astype(I32)


def _expert_call(tables, xs, wg, wu, wd):
    nb = xs.shape[0] // (MOE_BLK * PCH)
    assert nb >= ROW_SLOTS
    d, ff = wg.shape[1], wg.shape[2]
    rows = pl.BlockSpec((MOE_BLK * PCH, LANES), lambda i, bi, nr, ld, nx: (bi[i], 0))
    hbm = pl.BlockSpec(memory_space=pl.ANY)
    grid_spec = pltpu.PrefetchScalarGridSpec(
        num_scalar_prefetch=4,
        grid=(nb,),
        in_specs=[hbm, hbm, hbm, hbm],
        out_specs=rows,
        scratch_shapes=[pltpu.VMEM((WEIGHT_SLOTS, d, ff), F32), pltpu.VMEM((WEIGHT_SLOTS, d, ff), F32),
                        pltpu.VMEM((WEIGHT_SLOTS, ff, d), F32),
                        pltpu.VMEM((d, ff), BF16), pltpu.VMEM((d, ff), BF16), pltpu.VMEM((ff, d), BF16),
                        pltpu.VMEM((ROW_SLOTS, MOE_BLK * PCH, LANES), I32),
                        pltpu.SemaphoreType.DMA((WEIGHT_SLOTS,)), pltpu.SemaphoreType.DMA((ROW_SLOTS,))],
    )
    return pl.pallas_call(
        _expert_body,
        grid_spec=grid_spec,
        out_shape=jax.ShapeDtypeStruct(xs.shape, I32),
        compiler_params=_cparams(("arbitrary",), VMEM_LIMIT),
        name="moe_experts",
    )(*tables, xs, wg, wu, wd)


def _combine_body(d_ref, dn_ref, wt_ref, ob_hbm, h2_ref, x1_ref, mod_ref, sg_ref, su_ref,
                  sd_ref, fg_ref, out_ref, gbuf, rbuf, sem, *, tm, nl):
    bb, ll, d = x1_ref.shape
    step = pl.program_id(0) * nl + pl.program_id(1)
    last = pl.num_programs(0) * nl - 1
    parity = lax.rem(step, 2)
    grp = CMB_GROUP

    def request(d_tab, g, s):
        for j in range(grp):
            m = g * grp + j
            for kk in range(TOP_K):
                pltpu.make_async_copy(_pslab(ob_hbm, d_tab[0, 0, m * TOP_K + kk]), _slab(gbuf.at[s], kk * tm + m),
                                      sem.at[s]).start(priority=kk % 2)

    def mix(g, s):
        r0 = pl.multiple_of(g * grp, grp)
        w = wt_ref[0, pl.ds(r0, grp), :]
        wk = [w[:, kk:kk + 1] for kk in range(TOP_K)]
        for c in range(PCH):
            acc_lo = acc_hi = None
            for kk in range(TOP_K):
                words = gbuf[s, pl.ds((kk * tm + r0) * PCH + c, grp, stride=PCH), :]
                lo, hi = _unpack_pair(words)
                acc_lo = wk[kk] * lo if acc_lo is None else acc_lo + wk[kk] * lo
                acc_hi = wk[kk] * hi if acc_hi is None else acc_hi + wk[kk] * hi
            rbuf[pl.ds(r0, grp), c * LANES:(c + 1) * LANES] = acc_lo
            rbuf[pl.ds(r0, grp), (c + PCH) * LANES:(c + PCH + 1) * LANES] = acc_hi

    @pl.when(step == 0)
    def _():
        def first(g, carry):
            request(d_ref, g, 0)
            return carry
        lax.fori_loop(0, tm // grp, first, 0)

    def run(slot):
        pltpu.make_async_copy(ob_hbm.at[pl.ds(0, tm * TOP_K * PCH)], gbuf.at[slot], sem.at[slot]).wait()

        @pl.when(step < last)
        def _():
            def both(g, carry):
                request(dn_ref, g, 1 - slot)
                mix(g, slot)
                return carry
            lax.fori_loop(0, tm // grp, both, 0)

        @pl.when(step == last)
        def _():
            def only(g, carry):
                mix(g, slot)
                return carry
            lax.fori_loop(0, tm // grp, only, 0)

    for slot in range(2):
        pl.when(parity == slot)(functools.partial(run, slot))

    routed = rbuf[...]
    h2 = _rows_from_packed(h2_ref, tm)
    hg = jnp.dot(h2, sg_ref[...], preferred_element_type=F32)
    hu = jnp.dot(h2, su_ref[...], preferred_element_type=F32)
    shared = jnp.dot((hg * _sigmoid(hg) * hu).astype(BF16), sd_ref[...], preferred_element_type=F32)
    ff = (routed + shared).reshape(bb, ll, d)
    x2 = x1_ref[...] + mod_ref[:, 5:6, :] * ff
    out_ref[...] = x2 * lax.rsqrt(jnp.mean(x2 * x2, axis=-1, keepdims=True) + NORM_EPS) * fg_ref[...]


def _combine_call(dest, wts, first_tok, ob, h2s, x1, mod, p):
    bn, seq, d = x1.shape
    tm = CMB_TILE
    bb, ll = _tile(bn, seq, tm)
    nl = seq // ll
    tn = bn * seq
    nsteps = tn // tm
    assert first_tok % tm == 0 and dest.shape[2] == tm * TOP_K and wts.shape[1:] == (tm, TOP_K)
    tile = lambda g: (first_tok // tm + g, 0, 0)
    smem = _assign_spec(tm, lambda b, l: tile(b * nl + l))
    smem_next = _assign_spec(tm, lambda b, l: tile(jnp.minimum(b * nl + l + 1, nsteps - 1)))
    wblk = pl.BlockSpec((1, tm, TOP_K), lambda b, l: tile(b * nl + l))
    tok = pl.BlockSpec((bb, ll, d), lambda b, l: (b, l, 0))
    full = lambda a: pl.BlockSpec(a.shape, lambda b, l: (0,) * a.ndim)
    consts = (p["sh_gate"], p["sh_up"], p["sh_down"], p["final_g"])
    return pl.pallas_call(
        functools.partial(_combine_body, tm=tm, nl=nl),
        grid=(bn // bb, nl),
        in_specs=[smem, smem_next, wblk, pl.BlockSpec(memory_space=pl.ANY),
                  pl.BlockSpec((tm * PCH, LANES), lambda b, l: (first_tok // tm + b * nl + l, 0)),
                  tok, pl.BlockSpec((bb, 6, d), lambda b, l: (b, 0, 0))] + [full(c) for c in consts],
        out_specs=tok,
        out_shape=jax.ShapeDtypeStruct((bn, seq, d), F32),
        scratch_shapes=[pltpu.VMEM((2, tm * TOP_K * PCH, LANES), I32), pltpu.VMEM((tm, d), F32),
                        pltpu.SemaphoreType.DMA((2,))],
        compiler_params=_cparams(("arbitrary", "arbitrary"), VMEM_LIMIT),
        name="moe_combine_final",
    )(dest, dest, wts, ob, h2s, x1, mod, *consts)


def _layer_params(l, ada_w, ada_b, norm1_g, norm2_g, w_in, mu_shift, rw_w0, rw_w_up, rw_a0, rw_a_up, rw_g_up,
                  rw_k_k, rw_k_a, rw_r_k, rw_gn_g, rw_gn_b, gla_a_up, gla_a_bias, gla_norm_g, w_pa, w_pb, w_out,
                  router_w, router_b, exp_gate, exp_up, exp_down, sh_gate, sh_up, sh_down):
    d = D_MODEL
    wi = w_in[l]
    gla0 = RW_SHIFT_COLS
    xal0 = gla0 + QKV_W
    pad = jnp.zeros((d, XAL_W - GLA_GATE_RANK), BF16)
    w_pieces = (wi[:, :xal0].astype(BF16),
                jnp.concatenate([wi[:, xal0:xal0 + GLA_GATE_RANK].astype(BF16), pad], axis=1),
                wi[:, xal0 + GLA_GATE_RANK:].astype(BF16))
    zr = jnp.zeros((RW_W_RANK, RW_WIDTH), F32)
    hid = jnp.arange(RW_WIDTH) // RW_HEAD
    row = lambda a: a.reshape(1, -1)
    rw_t = router_w[l].T
    rw_hi = rw_t.astype(BF16)
    return dict(
        ada_w=ada_w[l], ada_b=ada_b[l], norm1_g=norm1_g[l].reshape(1, 1, d),
        norm2_g=norm2_g[l].reshape(1, 1, d), w_in=w_pieces,
        mu=mu_shift[l].reshape(1, 1, -1), w0=row(rw_w0[l]), wup=jnp.concatenate([rw_w_up[l], zr], axis=0),
        a0=row(rw_a0[l]), aup=jnp.concatenate([zr, rw_a_up[l]], axis=0), gup=rw_g_up[l].astype(BF16),
        kk=row(rw_k_k[l]), ka=row(rw_k_a[l]), rk=row(rw_r_k[l]),
        bd64=(hid[:, None] == hid[None, :]).astype(BF16),
        gn_g=row(rw_gn_g[l]), gn_b=row(rw_gn_b[l]),
        gla_aup=jnp.concatenate([gla_a_up[l], jnp.zeros((XAL_W - GLA_GATE_RANK, GLA_KW), F32)], axis=0),
        gla_ab=row(gla_a_bias[l]), gla_ng=row(gla_norm_g[l]),
        w_pa=w_pa[l].astype(BF16), w_pb=w_pb[l].astype(BF16), w_out=w_out[l].astype(BF16),
        rw_hi=rw_hi, rw_lo=(rw_t - rw_hi.astype(F32)).astype(BF16), router_b=router_b[l],
        exp_gate=exp_gate[l], exp_up=exp_up[l], exp_down=exp_down[l],
        sh_gate=sh_gate[l].astype(BF16), sh_up=sh_up[l].astype(BF16), sh_down=sh_down[l].astype(BF16),
    )


def _mixer_group(x, mod, s_rw, s_sh, s_gla, p, tn, first_tok, shared):
    qkv, xal, gg, mg, r, lw, k2, v, a_s, b_s, g, bonus, new_sh = _inproj_prep_call(x, mod, s_sh, p)
    y, rw_new = _rwscan_call(r, lw, k2, v, a_s, b_s, s_rw)
    o_b, gla_new = _gla_call(qkv, xal, gg, s_gla, p)
    x1, h2s, logits = _merge_call(y, g, bonus, o_b, mg, x, mod, p, tn, first_tok, shared)
    states = (rw_new, new_sh[:, 0, :], gla_new)
    return x1, h2s, logits, states


def _moe(h2s, logits, p):
    tn = h2s.shape[0] // PCH
    eidx, rank, wts, counts = _route_call(logits, p["router_b"])
    counts = counts[:, 0].astype(I32)
    padded = (counts + MOE_BLK - 1) // MOE_BLK * MOE_BLK
    pad_end = jnp.cumsum(padded)
    pad_start = (pad_end - padded).astype(I32)
    nb = (tn * TOP_K + N_EXPERTS * (MOE_BLK - 1)) // MOE_BLK + 1
    tables = _expert_tables(counts, pad_start, pad_end, nb)
    dest = _dest_call(eidx, rank, pad_start)
    dest = jnp.swapaxes(dest, 1, 2).reshape(dest.shape[0], 1, -1)
    wts = jnp.swapaxes(wts, 1, 2)
    xs = _dispatch_call(dest, h2s, nb * MOE_BLK)
    ob = _expert_call(tables, xs, p["exp_gate"], p["exp_up"], p["exp_down"])
    return ob, dest, wts


def kernel(x_prompt, x_sample, c_prompt, c_sample, state_rwkv, state_shift, state_gla, ada_w, ada_b, norm1_g,
           norm2_g, w_in, mu_shift, rw_w0, rw_w_up, rw_a0, rw_a_up, rw_g_up, rw_k_k, rw_k_a, rw_r_k, rw_gn_g,
           rw_gn_b, gla_a_up, gla_a_bias, gla_norm_g, w_pa, w_pb, w_out, router_w, router_b, exp_gate, exp_up,
           exp_down, sh_gate, sh_up, sh_down, final_g):
    depth = ada_w.shape[0]
    bp, bs = x_prompt.shape[0], x_sample.shape[0]
    tp = bp * x_prompt.shape[1]
    tn = tp + bs * x_sample.shape[1]
    xs_g = [x_prompt, x_sample]
    c_all = jnp.concatenate([c_prompt, c_sample], axis=0)
    zeros = lambda shape: jnp.zeros(shape, x_prompt.dtype)
    new_states = [[], []]
    fg = final_g.reshape(1, 1, D_MODEL)
    for l in range(depth):
        p = _layer_params(l, ada_w, ada_b, norm1_g, norm2_g, w_in, mu_shift, rw_w0, rw_w_up, rw_a0, rw_a_up,
                          rw_g_up, rw_k_k, rw_k_a, rw_r_k, rw_gn_g, rw_gn_b, gla_a_up, gla_a_bias, gla_norm_g,
                          w_pa, w_pb, w_out, router_w, router_b, exp_gate, exp_up, exp_down, sh_gate, sh_up,
                          sh_down)
        p["final_g"] = fg
        mod_all = _mod_call(c_all, p["ada_w"], p["ada_b"])
        mods = [mod_all[:bp], mod_all[bp:]]
        states_in = [
            (zeros((bp, RW_HEADS, RW_HEAD, RW_HEAD)), zeros((bp, RW_SHIFT_COLS)),
             zeros((bp, GLA_HEADS, GLA_DK, GLA_DV))),
            (state_rwkv[l], state_shift[l], state_gla[l]),
        ]
        x1s, shared = [], None
        firsts = [0, tp]
        for gi in range(2):
            x1, h2_all, lg_all, st = _mixer_group(xs_g[gi], mods[gi], *states_in[gi], p, tn, firsts[gi], shared)
            shared = (h2_all, lg_all)
            x1s.append(x1)
            new_states[gi].append(st)
        ob, dest, wts = _moe(*shared, p)
        assert depth == 1, "the fused final norm assumes a single layer"
        xs_g = [_combine_call(dest, wts, firsts[gi], ob, shared[0], x1s[gi], mods[gi], p) for gi in range(2)]
    stack = lambda gi, j: new_states[gi][0][j][None] if depth == 1 else jnp.stack([s[j] for s in new_states[gi]])
    return (xs_g[0], xs_g[1], stack(0, 0), stack(0, 1), stack(0, 2), stack(1, 0), stack(1, 1), stack(1, 2))
```

```python
import functools

import jax
import jax.numpy as jnp
from jax import lax
from jax.experimental import pallas as pl
from jax.experimental.pallas import tpu as pltpu

F32, BF16, I32 = jnp.float32, jnp.bfloat16, jnp.int32

D_MODEL = 1024
RW_HEADS, RW_HEAD = 8, 64
RW_WIDTH = RW_HEADS * RW_HEAD
RW_W_RANK, RW_A_RANK, RW_G_RANK = 64, 64, 128
RW_GN_EPS = 64e-5
GLA_HEADS, GLA_DK, GLA_DV = 4, 64, 128
GLA_KW, GLA_VW = GLA_HEADS * GLA_DK, GLA_HEADS * GLA_DV
GLA_GATE_RANK = 16
GLA_GATE_TAU = 16.0
GLA_CHUNK = 16
RW_SHIFT_COLS = 3 * RW_WIDTH + RW_W_RANK + RW_A_RANK + RW_G_RANK
N_EXPERTS, TOP_K, N_GROUPS, TOPK_GROUPS = 256, 8, 8, 4
GROUP_SIZE = N_EXPERTS // N_GROUPS
EXPERT_FF = 256
ROUTED_SCALE = 2.5
NORM_EPS = 1e-6

LANES = 128
CHUNKS = D_MODEL // LANES
PCH = CHUNKS // 2
UNIT = 64
RW_SCAN_PASSES = (1, 1, 1, 1, 1)
GLA_UNITS_PER_STEP = 4
RW_UNITS_PER_STEP = 4
VMEM_LIMIT = 56 * 1024 * 1024

PA_W, QKV_W, XAL_W, GG_W, MG_W = RW_SHIFT_COLS, 2 * GLA_KW + GLA_VW, LANES, GLA_VW, 2 * D_MODEL

TOK_TILE = 256
MOE_BLK = 512
MOE_SMALL = 128
WEIGHT_SLOTS = 3
ROW_SLOTS = 3
CMB_TILE = 256
CMB_GROUP = 16

_DN = {
    "nn": (((1,), (0,)), ((), ())),
    "nt": (((1,), (1,)), ((), ())),
    "tn": (((0,), (0,)), ((), ())),
}


def _split(x, pieces):
    out, rem = [], x
    for i in range(pieces):
        p = rem.astype(BF16)
        out.append(p)
        if i + 1 < pieces:
            rem = rem - p.astype(F32)
    return out


def _mm(a, b, form="nn", passes=1):
    dn = _DN[form]
    if passes == 6:
        return lax.dot_general(a.astype(F32), b.astype(F32), dn, precision=lax.Precision.HIGHEST,
                               preferred_element_type=F32)
    if passes == 1:
        return lax.dot_general(a.astype(BF16), b.astype(BF16), dn, preferred_element_type=F32)
    ah, al = _split(a, 2)
    bh, bl = _split(b, 2)
    out = lax.dot_general(ah, bh, dn, preferred_element_type=F32)
    out = out + lax.dot_general(ah, bl, dn, preferred_element_type=F32)
    return out + lax.dot_general(al, bh, dn, preferred_element_type=F32)


def _mm01(m01, x, pieces=3):
    m = m01.astype(BF16)
    out = None
    for p in _split(x, pieces):
        t = lax.dot_general(m, p, _DN["nn"], preferred_element_type=F32)
        out = t if out is None else out + t
    return out


def _xmm01(x, m01, pieces=2):
    m = m01.astype(BF16)
    out = None
    for p in _split(x, pieces):
        t = lax.dot_general(p, m, _DN["nn"], preferred_element_type=F32)
        out = t if out is None else out + t
    return out


HI16 = -65536


def _bf16_bits(x):
    return lax.bitcast_convert_type(x.astype(BF16).astype(F32), I32)


def _unpack_pair(w):
    return lax.bitcast_convert_type(w << 16, F32), lax.bitcast_convert_type(w & HI16, F32)


def _rows_to_packed(ref, x, first=0):
    for c in range(PCH):
        lo = _bf16_bits(x[:, c * LANES:(c + 1) * LANES])
        hi = _bf16_bits(x[:, (c + PCH) * LANES:(c + PCH + 1) * LANES])
        ref[pl.ds(first * PCH + c, x.shape[0], stride=PCH), :] = ((lo >> 16) & 0xFFFF) | (hi & HI16)


def _rows_from_packed(ref, n, live=None, first=0):
    lows, highs = [], []
    for c in range(PCH):
        w = ref[pl.ds(first * PCH + c, n, stride=PCH), :]
        if live is not None:
            w = jnp.where(live, w, 0)
        lo, hi = _unpack_pair(w)
        lows.append(lo.astype(BF16))
        highs.append(hi.astype(BF16))
    return jnp.concatenate(lows + highs, axis=1)


def _slab(ref, row):
    return ref.at[pl.ds(pl.multiple_of(row * PCH, PCH), PCH)]


def _sigmoid(x):
    return 1.0 / (1.0 + jnp.exp(-x))


def _softplus(x):
    return jnp.maximum(x, 0.0) + jnp.log(1.0 + jnp.exp(-jnp.abs(x)))


def _log2(n):
    assert n > 0 and n & (n - 1) == 0, n
    return n.bit_length() - 1


def _cparams(sem, vmem=None):
    return pltpu.CompilerParams(dimension_semantics=sem, vmem_limit_bytes=vmem)


def _mod_body(c_ref, w_ref, b_ref, o_ref):
    c = c_ref[...]
    o_ref[0] = _mm(c * _sigmoid(c), w_ref[...], passes=3) + b_ref[...]


def _mod_call(c_all, ada_w, ada_b):
    bt, d = c_all.shape
    out = pl.pallas_call(
        _mod_body,
        grid=(6,),
        in_specs=[pl.BlockSpec((bt, d), lambda k: (0, 0)),
                  pl.BlockSpec((d, d), lambda k: (0, k)),
                  pl.BlockSpec((1, d), lambda k: (0, k))],
        out_specs=pl.BlockSpec((1, bt, d), lambda k: (k, 0, 0)),
        out_shape=jax.ShapeDtypeStruct((6, bt, d), F32),
        compiler_params=_cparams(("arbitrary",)),
        name="adaln_mod",
    )(c_all, ada_w, ada_b.reshape(1, 6 * d))
    return jnp.transpose(out, (1, 0, 2))


def _inproj_body(x_ref, mod_ref, g_ref, wa_ref, wx_ref, wb_ref, pa_ref, qkv_ref, xal_ref, gg_ref, mg_ref):
    bb, ll, d = x_ref.shape
    x = x_ref[...]
    y = x * lax.rsqrt(jnp.mean(x * x, axis=-1, keepdims=True) + NORM_EPS) * g_ref[...]
    h = y * (1.0 + mod_ref[:, 1:2, :]) + mod_ref[:, 0:1, :]
    hb = h.reshape(bb * ll, d).astype(BF16)
    for w_ref, outs in ((wa_ref, (pa_ref, qkv_ref)), (wx_ref, (xal_ref,)), (wb_ref, (gg_ref, mg_ref))):
        off = 0
        for ref in outs:
            w = ref.shape[-1]
            ref[...] = jnp.dot(hb, w_ref[:, off:off + w], preferred_element_type=F32).reshape(bb, ll, w)
            off += w


def _tile(bn, seq, tile):
    if seq >= tile:
        assert seq % tile == 0
        return 1, tile
    assert tile % seq == 0 and bn % (tile // seq) == 0
    return tile // seq, seq


def _rwprep_body(pa_ref, sh_ref, mu_ref, w0_ref, wup_ref, a0_ref, aup_ref, gup_ref, kk_ref, ka_ref, rk_ref,
                 bd_ref, r_o, lw_o, k_o, v_o, a_o, b_o, g_o, bon_o, nsh_o, carry):
    bb, ll, wd = pa_ref.shape
    n = bb * ll
    hw = RW_WIDTH

    @pl.when(pl.program_id(1) == 0)
    def _():
        carry[...] = sh_ref[...]

    pa = pa_ref[...]
    rolled = pltpu.roll(pa.reshape(n, wd), 1, 0).reshape(bb, ll, wd)
    tok = lax.broadcasted_iota(I32, (bb, ll, wd), 1)
    prev = jnp.where(tok == 0, carry[...], rolled)
    last = pa_ref[:, ll - 1:ll, :]
    carry[...] = last
    nsh_o[...] = last
    xs = (pa + (prev - pa) * mu_ref[...]).reshape(n, wd)

    r, k, v = xs[:, 0:hw], xs[:, hw:2 * hw], xs[:, 2 * hw:3 * hw]
    xwa = xs[:, 3 * hw:3 * hw + LANES]
    xg = xs[:, 3 * hw + LANES:]
    w_log = -_softplus(-(w0_ref[...] + _mm(jnp.tanh(xwa), wup_ref[...], passes=3))) - 0.5
    lw = -jnp.exp(w_log)
    a = _sigmoid(a0_ref[...] + _mm(xwa, aup_ref[...], passes=3))
    g = _mm(_sigmoid(xg), gup_ref[...])
    bd = bd_ref[...]
    kkv = k * kk_ref[...]
    kkn = kkv * lax.rsqrt(jnp.maximum(_xmm01(kkv * kkv, bd, pieces=1), 1e-24))
    k2 = k * (1.0 + (a - 1.0) * ka_ref[...])
    bonus = _xmm01(r * k2 * rk_ref[...], bd, pieces=1) * v
    for ref, val in ((r_o, r), (lw_o, lw), (k_o, k2), (v_o, v), (a_o, -kkn), (b_o, kkn * a), (g_o, g),
                     (bon_o, bonus)):
        ref[...] = val.reshape(bb, ll, hw)


def _inproj_prep_body(x_ref, mod_ref, g_ref, wa_ref, wx_ref, wb_ref, sh_ref, mu_ref, w0_ref, wup_ref, a0_ref, aup_ref,
                      gup_ref, kk_ref, ka_ref, rk_ref, bd_ref, qkv_o, xal_o, gg_o, mg_o, r_o, lw_o, k_o, v_o, a_o,
                      b_o, g_o, bon_o, nsh_o, pa_s, carry):
    _inproj_body(x_ref, mod_ref, g_ref, wa_ref, wx_ref, wb_ref, pa_s, qkv_o, xal_o, gg_o, mg_o)
    _rwprep_body(pa_s, sh_ref, mu_ref, w0_ref, wup_ref, a0_ref, aup_ref, gup_ref, kk_ref, ka_ref, rk_ref, bd_ref,
                 r_o, lw_o, k_o, v_o, a_o, b_o, g_o, bon_o, nsh_o, carry)


def _inproj_prep_call(x, mod, s_sh, p):
    bn, seq, d = x.shape
    bb, ll = _tile(bn, seq, TOK_TILE)
    hw, wd = RW_WIDTH, PA_W
    tok = lambda w: pl.BlockSpec((bb, ll, w), lambda b, l: (b, l, 0))
    row = lambda w: pl.BlockSpec((bb, 1, w), lambda b, l: (b, 0, 0))
    full = lambda a: pl.BlockSpec(a.shape, lambda b, l: (0,) * a.ndim)
    consts = (p["mu"], p["w0"], p["wup"], p["a0"], p["aup"], p["gup"], p["kk"], p["ka"], p["rk"], p["bd64"])
    proj_w = (QKV_W, XAL_W, GG_W, MG_W)
    shapes = lambda ws: [jax.ShapeDtypeStruct((bn, seq, w), F32) for w in ws]
    return pl.pallas_call(
        _inproj_prep_body,
        grid=(bn // bb, seq // ll),
        in_specs=[tok(d), pl.BlockSpec((bb, 6, d), lambda b, l: (b, 0, 0)), full(p["norm1_g"])]
        + [full(w) for w in p["w_in"]] + [row(wd)] + [full(c) for c in consts],
        out_specs=[tok(w) for w in proj_w] + [tok(hw)] * 8 + [row(wd)],
        out_shape=shapes(proj_w) + shapes((hw,) * 8) + [jax.ShapeDtypeStruct((bn, 1, wd), F32)],
        scratch_shapes=[pltpu.VMEM((bb, ll, wd), F32), pltpu.VMEM((bb, 1, wd), F32)],
        compiler_params=_cparams(("arbitrary", "arbitrary"), VMEM_LIMIT),
        name="norm_inproj_prep",
    )(x, mod, p["norm1_g"], *p["w_in"], s_sh.reshape(bn, 1, wd), *consts)


def _unit_masks(n, tl):
    ri = lax.broadcasted_iota(I32, (n, n), 0)
    ci = lax.broadcasted_iota(I32, (n, n), 1)
    same = (ri >> _log2(tl)) == (ci >> _log2(tl))
    return same, same & (ri > ci), same & (ri >= ci)


def _rwscan_body(r_ref, lw_ref, k_ref, v_ref, a_ref, b_ref, s0_ref, y_ref, sn_ref, st, *, nu, nseq, tl, passes):
    n = nseq * tl
    n2 = 2 * n
    p_aa, p_inv, p_apply, p_state, p_y = passes

    hd = RW_HEAD

    @pl.when(pl.program_id(1) == 0)
    def _():
        zero = jnp.zeros((hd, hd), F32)
        for q in range(nu * nseq):
            for p in range(RW_HEADS // 2):
                st[q, p] = jnp.concatenate(
                    [jnp.concatenate([s0_ref[q, 2 * p], zero], axis=1),
                     jnp.concatenate([zero, s0_ref[q, 2 * p + 1]], axis=1)], axis=0)

    same, _, incl = _unit_masks(n, tl)
    m_cum = jnp.where(incl, 1.0, 0.0)
    m_seq = jnp.where(same, 1.0, 0.0)
    ri = lax.broadcasted_iota(I32, (n2, n2), 0)
    ci = lax.broadcasted_iota(I32, (n2, n2), 1)
    rt, ct = ri & (n - 1), ci & (n - 1)
    dsame = ((rt >> _log2(tl)) == (ct >> _log2(tl))) & ((ri >> _log2(n)) == (ci >> _log2(n)))
    strict_d = dsame & (rt > ct)
    incl_d = dsame & (rt >= ct)
    eye_d = jnp.where(ri == ci, 1.0, 0.0)
    lane = lax.broadcasted_iota(I32, (1, LANES), 1)
    m0 = jnp.where(lane < RW_HEAD, 1.0, 0.0)
    m1 = 1.0 - m0

    def dup(x):
        return jnp.concatenate([x * m0, x * m1], axis=0)

    def seq_rows(x, q):
        if nseq == 1:
            return x
        return jnp.concatenate([x[q * tl:(q + 1) * tl], x[n + q * tl:n + (q + 1) * tl]], axis=0)

    def unit_rows(parts):
        if nseq == 1:
            return parts[0]
        return jnp.concatenate([p[0:tl] for p in parts] + [p[tl:2 * tl] for p in parts], axis=0)

    chains = [(u, p) for u in range(nu) for p in range(RW_HEADS // 2)]
    ids = range(len(chains))
    cat0 = lambda *xs: jnp.concatenate(xs, axis=0)

    def ld(ref, c):
        u, p = chains[c]
        return ref[u * nseq:(u + 1) * nseq, :, p * LANES:(p + 1) * LANES].reshape(n, LANES)

    lw = [ld(lw_ref, c) for c in ids]
    cum = [_mm01(m_cum, x) for x in lw]
    tot = [_mm01(m_seq, x) for x in lw]
    e_c = [jnp.exp(x) for x in cum]
    e_n = [jnp.exp(-x) for x in cum]
    e_l = [jnp.exp(t - x) for t, x in zip(tot, cum)]
    at_d = [dup(ld(a_ref, c) * jnp.exp(cum[c] - lw[c])) for c in ids]
    rt_d = [dup(ld(r_ref, c) * e_c[c]) for c in ids]
    bt_d = [dup(ld(b_ref, c) * e_n[c]) for c in ids]
    kt_d = [dup(ld(k_ref, c) * e_n[c]) for c in ids]
    bh_d = [dup(ld(b_ref, c) * e_l[c]) for c in ids]
    kh_d = [dup(ld(k_ref, c) * e_l[c]) for c in ids]
    v_d = [dup(ld(v_ref, c)) for c in ids]
    aa = [_mm(cat0(at_d[c], rt_d[c]), cat0(bt_d[c], kt_d[c]), "nt", p_aa) for c in ids]
    a_ab = [jnp.where(strict_d, x[0:n2, 0:n2], 0.0) for x in aa]
    a_ak = [jnp.where(strict_d, x[0:n2, n2:], 0.0) for x in aa]
    a_rb = [jnp.where(incl_d, x[n2:, 0:n2], 0.0) for x in aa]
    a_rk = [jnp.where(incl_d, x[n2:, n2:], 0.0) for x in aa]
    zy = [_mm(cat0(a_ak[c], a_rk[c]), v_d[c], passes=p_apply) for c in ids]
    tinv = [eye_d + x for x in a_ab]
    nk = a_ab
    for _ in range(_log2(tl) - 1):
        nk = [_mm(x, x, passes=p_inv) for x in nk]
        tinv = [t + _mm(t, x, passes=p_inv) for t, x in zip(tinv, nk)]
    wu = [_mm(tinv[c], jnp.concatenate([at_d[c], zy[c][0:n2]], axis=1), passes=p_apply) for c in ids]
    seqs = range(nseq)
    srow = lambda c, q: (chains[c][0] * nseq + q, chains[c][1])
    s_old = [[st[srow(c, q)] for q in seqs] for c in ids]
    xs = [[_mm(cat0(seq_rows(wu[c][:, 0:LANES], q), seq_rows(rt_d[c], q)), s_old[c][q], "nt", p_state)
           for q in seqs] for c in ids]
    u_q = [[xs[c][q][0:2 * tl] + seq_rows(wu[c][:, LANES:], q) for q in seqs] for c in ids]
    for c in ids:
        for q in seqs:
            g_c = jnp.exp(tot[c][q * tl:q * tl + 1, :])
            st[srow(c, q)] = s_old[c][q] * g_c + _mm(cat0(u_q[c][q], seq_rows(v_d[c], q)),
                                                     cat0(seq_rows(bh_d[c], q), seq_rows(kh_d[c], q)), "tn", p_state)
    for c in ids:
        u, p = chains[c]
        y_d = (unit_rows([xs[c][q][2 * tl:] for q in seqs]) + _mm(a_rb[c], unit_rows(u_q[c]), passes=p_y)
               + zy[c][n2:])
        y_ref[u * nseq:(u + 1) * nseq, :, p * LANES:(p + 1) * LANES] = (y_d[0:n] + y_d[n:]).reshape(nseq, tl, LANES)

    @pl.when(pl.program_id(1) == pl.num_programs(1) - 1)
    def _():
        for q in range(nu * nseq):
            for p in range(RW_HEADS // 2):
                s = st[q, p]
                sn_ref[q, 2 * p] = s[0:hd, 0:hd]
                sn_ref[q, 2 * p + 1] = s[hd:, hd:]


def _unit_shape(bn, seq):
    if seq >= UNIT:
        assert seq % UNIT == 0
        return 1, UNIT
    assert UNIT % seq == 0 and bn % (UNIT // seq) == 0
    return UNIT // seq, seq


def _rwscan_call(r, lw, k2, v, a_s, b_s, s0, passes=RW_SCAN_PASSES):
    bn, seq, hw = r.shape
    nseq, tl = _unit_shape(bn, seq)
    nu = RW_UNITS_PER_STEP if bn % (RW_UNITS_PER_STEP * nseq) == 0 else 1
    rows = nu * nseq
    tok = pl.BlockSpec((rows, tl, hw), lambda b, c: (b, c, 0))
    stt = pl.BlockSpec((rows, RW_HEADS, RW_HEAD, RW_HEAD), lambda b, c: (b, 0, 0, 0))
    return pl.pallas_call(
        functools.partial(_rwscan_body, nu=nu, nseq=nseq, tl=tl, passes=passes),
        grid=(bn // rows, seq // tl),
        in_specs=[tok] * 6 + [stt],
        out_specs=[tok, stt],
        out_shape=[jax.ShapeDtypeStruct((bn, seq, hw), F32), jax.ShapeDtypeStruct(s0.shape, F32)],
        scratch_shapes=[pltpu.VMEM((rows, RW_HEADS // 2, LANES, LANES), F32)],
        compiler_params=_cparams(("arbitrary", "arbitrary"), VMEM_LIMIT),
        name="rwkv_scan",
    )(r, lw, k2, v, a_s, b_s, s0)


def _gla_body(qkv_ref, xal_ref, gate_ref, aup_ref, ab_ref, ng_ref, s0_ref, o_ref, sn_ref, st, *, nu, nseq, tl, cs):
    n = nseq * tl
    n2 = 2 * n
    nsub = tl // cs

    @pl.when(pl.program_id(1) == 0)
    def _():
        zero = jnp.zeros((GLA_DV, GLA_DK), F32)
        for q in range(nu * nseq):
            for p in range(GLA_HEADS // 2):
                st[q, p] = jnp.concatenate(
                    [jnp.concatenate([s0_ref[q, 2 * p].T, zero], axis=1),
                     jnp.concatenate([zero, s0_ref[q, 2 * p + 1].T], axis=1)], axis=0)

    same, _, incl = _unit_masks(n, cs)
    m_cum = jnp.where(incl, 1.0, 0.0)
    m_sub = jnp.where(same, 1.0, 0.0)
    ri = lax.broadcasted_iota(I32, (n2, n2), 0)
    ci = lax.broadcasted_iota(I32, (n2, n2), 1)
    rt, ct = ri & (n - 1), ci & (n - 1)
    causal_d = ((rt >> _log2(cs)) == (ct >> _log2(cs))) & ((ri >> _log2(n)) == (ci >> _log2(n))) & (rt >= ct)
    lane = lax.broadcasted_iota(I32, (1, LANES), 1)
    m0 = jnp.where(lane < GLA_DK, 1.0, 0.0)
    m1 = 1.0 - m0
    sr = lax.broadcasted_iota(I32, (2 * GLA_DV, LANES), 0)
    sc = lax.broadcasted_iota(I32, (2 * GLA_DV, LANES), 1)
    st_mask = jnp.where((sr >> _log2(GLA_DV)) == (sc >> _log2(GLA_DK)), 1.0, 0.0)

    def dup(x):
        return jnp.concatenate([x * m0, x * m1], axis=0)

    chains = [(u, p) for u in range(nu) for p in range(GLA_HEADS // 2)]
    ids = range(len(chains))
    urows = lambda u: slice(u * nseq, (u + 1) * nseq)
    ng = ng_ref[...]
    la_all = [-_softplus(-(_mm(xal_ref[urows(u), :, :].reshape(n, LANES), aup_ref[...], passes=3) + ab_ref[...]))
              * (1.0 / GLA_GATE_TAU) for u in range(nu)]

    def ld(ref, c, off, width):
        return ref[urows(chains[c][0]), :, off:off + width].reshape(n, width)

    q = [ld(qkv_ref, c, chains[c][1] * LANES, LANES) * (GLA_DK ** -0.5) for c in ids]
    k = [ld(qkv_ref, c, GLA_KW + chains[c][1] * LANES, LANES) for c in ids]
    vp = [ld(qkv_ref, c, 2 * GLA_KW + chains[c][1] * 2 * GLA_DV, 2 * GLA_DV) for c in ids]
    la = [la_all[u][:, p * LANES:(p + 1) * LANES] for u, p in chains]
    bc = [_mm01(m_cum, x) for x in la]
    bl = [_mm01(m_sub, x) for x in la]
    qe = [q[c] * jnp.exp(bc[c]) for c in ids]
    ke = [k[c] * jnp.exp(-bc[c]) for c in ids]
    kd = [k[c] * jnp.exp(bl[c] - bc[c]) for c in ids]
    att = [jnp.where(causal_d, _mm(dup(qe[c]), dup(ke[c]), "nt", passes=1), 0.0) for c in ids]
    v_st = [jnp.concatenate([x[:, 0:GLA_DV], x[:, GLA_DV:]], axis=0) for x in vp]
    o_st = [_mm(att[c], v_st[c], passes=1) for c in ids]
    upd = [[_mm(vp[c][r0:r0 + cs], kd[c][r0:r0 + cs], "tn", passes=1) for r0 in range(0, n, cs)] for c in ids]
    inter = [[None] * (n // cs) for _ in ids]
    for sq in range(nseq):
        s = [st[chains[c][0] * nseq + sq, chains[c][1]] for c in ids]
        for j in range(nsub):
            i = sq * nsub + j
            r0 = i * cs
            for c in ids:
                inter[c][i] = _mm(qe[c][r0:r0 + cs], s[c], "nt", passes=1)
                s[c] = s[c] * jnp.exp(bl[c][r0:r0 + 1, :]) + st_mask * upd[c][i]
        for c in ids:
            st[chains[c][0] * nseq + sq, chains[c][1]] = s[c]
    for c in ids:
        u, p = chains[c]
        o = o_st[c] + jnp.concatenate([x[:, 0:GLA_DV] for x in inter[c]] + [x[:, GLA_DV:] for x in inter[c]], axis=0)
        o = o * lax.rsqrt(jnp.mean(o * o, axis=-1, keepdims=True) + NORM_EPS) * ng
        goff = p * 2 * GLA_DV
        gp = ld(gate_ref, c, goff, 2 * GLA_DV)
        g_st = jnp.concatenate([gp[:, 0:GLA_DV], gp[:, GLA_DV:]], axis=0)
        ob = o * (g_st * _sigmoid(g_st))
        o_ref[urows(u), :, goff:goff + GLA_DV] = ob[0:n].reshape(nseq, tl, GLA_DV)
        o_ref[urows(u), :, goff + GLA_DV:goff + 2 * GLA_DV] = ob[n:].reshape(nseq, tl, GLA_DV)

    @pl.when(pl.program_id(1) == pl.num_programs(1) - 1)
    def _():
        for q in range(nu * nseq):
            for p in range(GLA_HEADS // 2):
                s = st[q, p]
                sn_ref[q, 2 * p] = s[0:GLA_DV, 0:GLA_DK].T
                sn_ref[q, 2 * p + 1] = s[GLA_DV:, GLA_DK:].T


def _gla_call(qkv, xal, gate, s0, p):
    bn, seq, _ = qkv.shape
    nseq, tl = _unit_shape(bn, seq)
    cs = min(GLA_CHUNK, seq)
    assert tl % cs == 0
    nu = GLA_UNITS_PER_STEP if bn % (GLA_UNITS_PER_STEP * nseq) == 0 else 1
    rows = nu * nseq
    tok = lambda w: pl.BlockSpec((rows, tl, w), lambda b, c: (b, c, 0))
    full = lambda a: pl.BlockSpec(a.shape, lambda b, c: (0,) * a.ndim)
    stt = pl.BlockSpec((rows, GLA_HEADS, GLA_DK, GLA_DV), lambda b, c: (b, 0, 0, 0))
    consts = (p["gla_aup"], p["gla_ab"], p["gla_ng"])
    return pl.pallas_call(
        functools.partial(_gla_body, nu=nu, nseq=nseq, tl=tl, cs=cs),
        grid=(bn // rows, seq // tl),
        in_specs=[tok(QKV_W), tok(XAL_W), tok(GG_W)] + [full(c) for c in consts] + [stt],
        out_specs=[tok(GLA_VW), stt],
        out_shape=[jax.ShapeDtypeStruct((bn, seq, GLA_VW), F32), jax.ShapeDtypeStruct(s0.shape, F32)],
        scratch_shapes=[pltpu.VMEM((rows, GLA_HEADS // 2, 2 * GLA_DV, LANES), F32)],
        compiler_params=_cparams(("arbitrary", "arbitrary"), VMEM_LIMIT),
        name="gla_chunked",
    )(qkv, xal, gate, *consts, s0)


def _merge_body(y_ref, g_ref, bon_ref, ob_ref, mg_ref, x_ref, mod_ref, gng_ref, gnb_ref, bd_ref, wpa_ref,
                wpb_ref, wout_ref, n2_ref, rwh_ref, rwl_ref, *rest):
    x1_o, h2_o, lg_o = rest[-3:]
    bb, ll, d = x_ref.shape
    n = bb * ll
    hw = RW_WIDTH
    bd = bd_ref[...]
    y = y_ref[...].reshape(n, hw)
    mu = _xmm01(y, bd, pieces=2) * (1.0 / RW_HEAD)
    dv = y - mu
    var = _xmm01(dv * dv, bd, pieces=1) * (1.0 / RW_HEAD)
    yn = dv * lax.rsqrt(var + RW_GN_EPS) * gng_ref[...] + gnb_ref[...]
    o_a = (yn + bon_ref[...].reshape(n, hw)) * g_ref[...].reshape(n, hw)
    o_b = ob_ref[...].reshape(n, GLA_VW)
    mg = mg_ref[...].reshape(n, 2 * d)
    merged = _sigmoid(mg[:, 0:d]) * _mm(o_a, wpa_ref[...]) + _sigmoid(mg[:, d:]) * _mm(o_b, wpb_ref[...])
    mix = _mm(merged, wout_ref[...]).reshape(bb, ll, d)
    x1 = x_ref[...] + mod_ref[:, 2:3, :] * mix
    x1_o[...] = x1
    yn2 = x1 * lax.rsqrt(jnp.mean(x1 * x1, axis=-1, keepdims=True) + NORM_EPS) * n2_ref[...]
    h2 = (yn2 * (1.0 + mod_ref[:, 4:5, :]) + mod_ref[:, 3:4, :]).reshape(n, d)
    hh, hl = _split(h2, 2)
    rwh, rwl = rwh_ref[...], rwl_ref[...]
    nt = lambda a, b: lax.dot_general(a, b, _DN["nt"], preferred_element_type=F32)
    lg_o[...] = nt(rwh, hh) + nt(rwl, hh) + nt(rwh, hl)
    _rows_to_packed(h2_o, h2)


def _merge_call(y, g, bonus, o_b, mg, x, mod, p, tn, first_tok, shared):
    bn, seq, d = x.shape
    bb, ll = _tile(bn, seq, TOK_TILE)
    nl = seq // ll
    assert first_tok % (bb * ll) == 0
    t0 = first_tok // (bb * ll)
    n_in = 7 + 9
    extra = [] if shared is None else list(shared)
    alias = {} if shared is None else {n_in: 1, n_in + 1: 2}
    tok = lambda w: pl.BlockSpec((bb, ll, w), lambda b, l: (b, l, 0))
    full = lambda a: pl.BlockSpec(a.shape, lambda b, l: (0,) * a.ndim)
    consts = (p["gn_g"], p["gn_b"], p["bd64"], p["w_pa"], p["w_pb"], p["w_out"], p["norm2_g"], p["rw_hi"],
              p["rw_lo"])
    return pl.pallas_call(
        _merge_body,
        grid=(bn // bb, nl),
        in_specs=[tok(RW_WIDTH)] * 3 + [tok(GLA_VW), tok(MG_W), tok(d),
                                        pl.BlockSpec((bb, 6, d), lambda b, l: (b, 0, 0))] + [full(c) for c in consts]
        + [pl.BlockSpec(memory_space=pl.ANY)] * len(extra),
        out_specs=[tok(d),
                   pl.BlockSpec((bb * ll * PCH, LANES), lambda b, l: (t0 + b * nl + l, 0)),
                   pl.BlockSpec((N_EXPERTS, bb * ll), lambda b, l: (0, t0 + b * nl + l))],
        out_shape=[jax.ShapeDtypeStruct((bn, seq, d), F32),
                   jax.ShapeDtypeStruct((tn * PCH, LANES), I32),
                   jax.ShapeDtypeStruct((N_EXPERTS, tn), F32)],
        input_output_aliases=alias,
        compiler_params=_cparams(("arbitrary", "arbitrary"), VMEM_LIMIT),
        name="merge_outproj_router",
    )(y, g, bonus, o_b, mg, x, mod, *consts, *extra)


def _route_body(lg_ref, rb_ref, e_o, rk_o, w_o, cnt_o, carry):
    ne, tm = lg_ref.shape

    @pl.when(pl.program_id(0) == 0)
    def _():
        carry[...] = jnp.zeros_like(carry)

    neg = -jnp.inf
    scores = _sigmoid(lg_ref[...])
    sel = scores + rb_ref[...]
    row_i = lax.broadcasted_iota(I32, (ne, tm), 0)
    row = row_i.astype(F32)
    grp = (row_i >> _log2(GROUP_SIZE)).astype(F32)

    def first_max(x, ids, none):
        m = jnp.max(x, axis=0, keepdims=True)
        return m, jnp.min(jnp.where(x == m, ids, none), axis=0, keepdims=True)

    gs = []
    gids = lax.broadcasted_iota(I32, (GROUP_SIZE, tm), 0)
    for gidx in range(N_GROUPS):
        rows = slice(gidx * GROUP_SIZE, (gidx + 1) * GROUP_SIZE)
        sg = _sigmoid(lg_ref[rows, :]) + rb_ref[rows, :]
        ids = (gids + gidx * GROUP_SIZE).astype(F32)
        m1, i1 = first_max(sg, ids, float(ne))
        gs.append(m1 + jnp.max(jnp.where(ids == i1, neg, sg), axis=0, keepdims=True))
    gs = jnp.concatenate(gs, axis=0)
    gid = lax.broadcasted_iota(I32, (N_GROUPS, tm), 0).astype(F32)
    cur = jnp.full((ne, tm), neg, F32)
    for _ in range(TOPK_GROUPS):
        _, gi = first_max(gs, gid, float(N_GROUPS))
        cur = jnp.where(grp == gi, sel, cur)
        gs = jnp.where(gid == gi, neg, gs)

    pm = jnp.zeros((ne, tm), F32)
    eidx, wts = [], []
    for _ in range(TOP_K):
        _, ei = first_max(cur, row, float(ne))
        hit = row == ei
        pm = jnp.where(hit, 1.0, pm)
        eidx.append(ei)
        wts.append(jnp.sum(jnp.where(hit, scores, 0.0), axis=0, keepdims=True))
        cur = jnp.where(hit, neg, cur)
    wsum = wts[0]
    for w in wts[1:]:
        wsum = wsum + w

    ri = lax.broadcasted_iota(I32, (tm, tm), 0)
    ci = lax.broadcasted_iota(I32, (tm, tm), 1)
    earlier = jnp.where(ri < ci, 1.0, 0.0)
    rank = _mm(pm, earlier, passes=1) + carry[...]
    carry[...] = carry[...] + jnp.sum(pm, axis=1, keepdims=True)
    cnt_o[...] = carry[...]

    rks = [jnp.sum(jnp.where(row == e, rank, 0.0), axis=0, keepdims=True) for e in eidx]
    e_o[0] = jnp.concatenate(eidx, axis=0).astype(I32)
    rk_o[0] = jnp.concatenate(rks, axis=0).astype(I32)
    w_o[0] = jnp.concatenate([w / wsum * ROUTED_SCALE for w in wts], axis=0)


def _route_call(logits_t, router_b):
    ne, tn = logits_t.shape
    tm = TOK_TILE
    assert tn % tm == 0
    col = pl.BlockSpec((ne, 1), lambda i: (0, 0))
    tab = pl.BlockSpec((1, TOP_K, tm), lambda i: (i, 0, 0))
    tab_shape = (tn // tm, TOP_K, tm)
    return pl.pallas_call(
        _route_body,
        grid=(tn // tm,),
        in_specs=[pl.BlockSpec((ne, tm), lambda i: (0, i)), col],
        out_specs=[tab, tab, tab, col],
        out_shape=[jax.ShapeDtypeStruct(tab_shape, I32), jax.ShapeDtypeStruct(tab_shape, I32),
                   jax.ShapeDtypeStruct(tab_shape, F32), jax.ShapeDtypeStruct((ne, 1), F32)],
        scratch_shapes=[pltpu.VMEM((ne, 1), F32)],
        compiler_params=_cparams(("arbitrary",)),
        name="moe_route",
    )(logits_t, router_b.reshape(ne, 1))


def _dest_body(e_ref, rk_ref, ps_ref, d_o):
    ne, tm = ps_ref.shape[0], e_ref.shape[2]
    ids = lax.broadcasted_iota(I32, (ne, tm), 0)
    ps = ps_ref[...]
    for t in range(e_ref.shape[0]):
        first = [jnp.sum(jnp.where(ids == e_ref[t, kk:kk + 1, :], ps, 0.0), axis=0, keepdims=True)
                 for kk in range(TOP_K)]
        d_o[t] = (jnp.concatenate(first, axis=0).astype(I32) + rk_ref[t]) * PCH


def _dest_call(eidx, rank, pad_start):
    nt, _, tm = eidx.shape
    ne = pad_start.shape[0]
    per = next(k for k in (4, 2, 1) if nt % k == 0)
    tab = pl.BlockSpec((per, TOP_K, tm), lambda i: (i, 0, 0))
    return pl.pallas_call(
        _dest_body,
        grid=(nt // per,),
        in_specs=[tab, tab, pl.BlockSpec((ne, 1), lambda i: (0, 0))],
        out_specs=tab,
        out_shape=jax.ShapeDtypeStruct(eidx.shape, I32),
        compiler_params=_cparams(("arbitrary",)),
        name="moe_dest",
    )(eidx, rank, pad_start.astype(F32).reshape(ne, 1))


def _pslab(ref, offset):
    return ref.at[pl.ds(pl.multiple_of(offset, PCH), PCH)]


def _dispatch_body(d_ref, h2_ref, xs_hbm, sem, *, tm):
    def issue(m, carry):
        for kk in range(TOP_K):
            pltpu.make_async_copy(_slab(h2_ref, m), _pslab(xs_hbm, d_ref[0, 0, m * TOP_K + kk]), sem).start(priority=kk % 2)
        return carry

    lax.fori_loop(0, tm, issue, 0)
    all_rows = xs_hbm.at[pl.ds(0, tm * TOP_K * PCH)]
    pltpu.make_async_copy(all_rows, all_rows, sem).wait()


def _assign_spec(tm, index_map):
    return pl.BlockSpec((1, 1, tm * TOP_K), index_map, memory_space=pltpu.SMEM)


def _dispatch_call(dest, h2s, n_rows):
    tn = h2s.shape[0] // PCH
    tm = TOK_TILE
    assert dest.shape == (tn // tm, 1, tm * TOP_K)
    blk = _assign_spec(tm, lambda i: (i, 0, 0))
    return pl.pallas_call(
        functools.partial(_dispatch_body, tm=tm),
        grid=(tn // tm,),
        in_specs=[blk, pl.BlockSpec((tm * PCH, LANES), lambda i: (i, 0))],
        out_specs=pl.BlockSpec(memory_space=pl.ANY),
        out_shape=jax.ShapeDtypeStruct((n_rows * PCH, LANES), I32),
        scratch_shapes=[pltpu.SemaphoreType.DMA],
        compiler_params=_cparams(("arbitrary",)),
        name="moe_dispatch",
    )(dest, h2s)


def _expert_body(bi_ref, nr_ref, ld_ref, nx_ref, xs_hbm, wg_hbm, wu_hbm, wd_hbm, ob_ref, wg_buf, wu_buf, wd_buf,
                 wg_bf, wu_bf, wd_bf, xbuf, sem, xsem):
    i = pl.program_id(0)
    nsteps = pl.num_programs(0)
    nr = nr_ref[i]
    slot = ld_ref[i]
    blk_rows = MOE_BLK * PCH
    small_rows = MOE_SMALL * PCH

    def row_block(j, go):
        s = lax.rem(j, ROW_SLOTS)
        base = pl.multiple_of(bi_ref[j] * blk_rows, blk_rows)
        head = pltpu.make_async_copy(xs_hbm.at[pl.ds(base, small_rows)], xbuf.at[s].at[pl.ds(0, small_rows)],
                                     xsem.at[s])
        rest = pltpu.make_async_copy(xs_hbm.at[pl.ds(base + small_rows, blk_rows - small_rows)],
                                     xbuf.at[s].at[pl.ds(small_rows, blk_rows - small_rows)], xsem.at[s])
        go(head)
        pl.when(nr_ref[j] > MOE_SMALL)(lambda: go(rest))

    start = lambda cp: cp.start()
    wait = lambda cp: cp.wait()

    @pl.when(i == 0)
    def _():
        for j in range(ROW_SLOTS - 1):
            row_block(j, start)

    @pl.when(i + ROW_SLOTS - 1 < nsteps)
    def _():
        row_block(i + ROW_SLOTS - 1, start)

    def fetch(e, s):
        return (pltpu.make_async_copy(wg_hbm.at[e], wg_buf.at[s], sem.at[s]),
                pltpu.make_async_copy(wu_hbm.at[e], wu_buf.at[s], sem.at[s]),
                pltpu.make_async_copy(wd_hbm.at[e], wd_buf.at[s], sem.at[s]))

    @pl.when(i == 0)
    def _():
        for s in range(WEIGHT_SLOTS - 1):
            e0 = nx_ref[nx_ref.shape[0] - (WEIGHT_SLOTS - 1) + s]

            @pl.when(e0 >= 0)
            def _():
                for k, cp in enumerate(fetch(e0, s)):
                    cp.start(priority=k % 2)

    @pl.when(slot >= 0)
    def _():
        for cp in fetch(0, slot):
            cp.wait()

        @pl.when(nx_ref[i] >= 0)
        def _():
            for k, cp in enumerate(fetch(nx_ref[i], lax.rem(slot + WEIGHT_SLOTS - 1, WEIGHT_SLOTS))):
                cp.start(priority=k % 2)

        wg_bf[...] = wg_buf[slot].astype(BF16)
        wu_bf[...] = wu_buf[slot].astype(BF16)
        wd_bf[...] = wd_buf[slot].astype(BF16)

    row_block(i, wait)

    def work(n):
        rid = lax.broadcasted_iota(I32, (n, LANES), 0)
        x = _rows_from_packed(xbuf.at[lax.rem(i, ROW_SLOTS)], n, rid < nr)
        hg = jnp.dot(x, wg_bf[...], preferred_element_type=F32)
        hu = jnp.dot(x, wu_bf[...], preferred_element_type=F32)
        hh = (hg * _sigmoid(hg) * hu).astype(BF16)
        _rows_to_packed(ob_ref, jnp.dot(hh, wd_bf[...], preferred_element_type=F32), 0)

    pl.when(nr > MOE_SMALL)(lambda: work(MOE_BLK))
    pl.when((nr > 0) & (nr <= MOE_SMALL))(lambda: work(MOE_SMALL))


def _expert_tables(counts, pad_start, pad_end, nb):
    ne = counts.shape[0]
    experts = jnp.arange(ne, dtype=I32)
    first_row = jnp.arange(nb, dtype=I32) * MOE_BLK
    block_e = jnp.minimum(jnp.sum(pad_end[None, :] <= first_row[:, None], axis=1), ne - 1).astype(I32)
    mine = block_e[:, None] == experts[None, :]
    pick = lambda v: jnp.sum(jnp.where(mine, v[None, :], 0), axis=1)
    has = counts > 0
    ordinal = jnp.cumsum(has.astype(I32)) - 1
    start_b, count_b, ord_b = pick(pad_start), pick(counts), pick(ordinal)
    block_rows = jnp.clip(start_b + count_b - first_row, 0, MOE_BLK).astype(I32)
    block_i = jnp.minimum(jnp.arange(nb, dtype=I32), pad_end[-1] // MOE_BLK - 1).astype(I32)
    starts = (first_row == start_b) & (block_rows > 0)
    load_slot = jnp.where(starts, ord_b % WEIGHT_SLOTS, -1).astype(I32)
    nth = lambda want: jnp.max(jnp.where(has[None, :] & (ordinal[None, :] == want[:, None]), experts[None, :], -1),
                               axis=1)
    ahead = jnp.where(starts, nth(ord_b + WEIGHT_SLOTS - 1), -1)
    lead = nth(jnp.arange(WEIGHT_SLOTS - 1, dtype=I32))
    return block_i, block_rows, load_slot, jnp.concatenate([ahead, lead]).astype(I32)


def _expert_call(tables, xs, wg, wu, wd):
    nb = xs.shape[0] // (MOE_BLK * PCH)
    assert nb >= ROW_SLOTS
    d, ff = wg.shape[1], wg.shape[2]
    rows = pl.BlockSpec((MOE_BLK * PCH, LANES), lambda i, bi, nr, ld, nx: (bi[i], 0))
    hbm = pl.BlockSpec(memory_space=pl.ANY)
    grid_spec = pltpu.PrefetchScalarGridSpec(
        num_scalar_prefetch=4,
        grid=(nb,),
        in_specs=[hbm, hbm, hbm, hbm],
        out_specs=rows,
        scratch_shapes=[pltpu.VMEM((WEIGHT_SLOTS, d, ff), F32), pltpu.VMEM((WEIGHT_SLOTS, d, ff), F32),
                        pltpu.VMEM((WEIGHT_SLOTS, ff, d), F32),
                        pltpu.VMEM((d, ff), BF16), pltpu.VMEM((d, ff), BF16), pltpu.VMEM((ff, d), BF16),
                        pltpu.VMEM((ROW_SLOTS, MOE_BLK * PCH, LANES), I32),
                        pltpu.SemaphoreType.DMA((WEIGHT_SLOTS,)), pltpu.SemaphoreType.DMA((ROW_SLOTS,))],
    )
    return pl.pallas_call(
        _expert_body,
        grid_spec=grid_spec,
        out_shape=jax.ShapeDtypeStruct(xs.shape, I32),
        compiler_params=_cparams(("arbitrary",), VMEM_LIMIT),
        name="moe_experts",
    )(*tables, xs, wg, wu, wd)


def _combine_body(d_ref, dn_ref, wt_ref, ob_hbm, h2_ref, x1_ref, mod_ref, sg_ref, su_ref,
                  sd_ref, fg_ref, out_ref, gbuf, rbuf, sem, *, tm, nl):
    bb, ll, d = x1_ref.shape
    step = pl.program_id(0) * nl + pl.program_id(1)
    last = pl.num_programs(0) * nl - 1
    parity = lax.rem(step, 2)
    grp = CMB_GROUP

    def request(d_tab, g, s):
        for j in range(grp):
            m = g * grp + j
            for kk in range(TOP_K):
                pltpu.make_async_copy(_pslab(ob_hbm, d_tab[0, 0, m * TOP_K + kk]), _slab(gbuf.at[s], kk * tm + m),
                                      sem.at[s]).start(priority=kk % 2)

    def mix(g, s):
        r0 = pl.multiple_of(g * grp, grp)
        w = wt_ref[0, pl.ds(r0, grp), :]
        wk = [w[:, kk:kk + 1] for kk in range(TOP_K)]
        for c in range(PCH):
            acc_lo = acc_hi = None
            for kk in range(TOP_K):
                words = gbuf[s, pl.ds((kk * tm + r0) * PCH + c, grp, stride=PCH), :]
                lo, hi = _unpack_pair(words)
                acc_lo = wk[kk] * lo if acc_lo is None else acc_lo + wk[kk] * lo
                acc_hi = wk[kk] * hi if acc_hi is None else acc_hi + wk[kk] * hi
            rbuf[pl.ds(r0, grp), c * LANES:(c + 1) * LANES] = acc_lo
            rbuf[pl.ds(r0, grp), (c + PCH) * LANES:(c + PCH + 1) * LANES] = acc_hi

    @pl.when(step == 0)
    def _():
        def first(g, carry):
            request(d_ref, g, 0)
            return carry
        lax.fori_loop(0, tm // grp, first, 0)

    def run(slot):
        pltpu.make_async_copy(ob_hbm.at[pl.ds(0, tm * TOP_K * PCH)], gbuf.at[slot], sem.at[slot]).wait()

        @pl.when(step < last)
        def _():
            def both(g, carry):
                request(dn_ref, g, 1 - slot)
                mix(g, slot)
                return carry
            lax.fori_loop(0, tm // grp, both, 0)

        @pl.when(step == last)
        def _():
            def only(g, carry):
                mix(g, slot)
                return carry
            lax.fori_loop(0, tm // grp, only, 0)

    for slot in range(2):
        pl.when(parity == slot)(functools.partial(run, slot))

    routed = rbuf[...]
    h2 = _rows_from_packed(h2_ref, tm)
    hg = jnp.dot(h2, sg_ref[...], preferred_element_type=F32)
    hu = jnp.dot(h2, su_ref[...], preferred_element_type=F32)
    shared = jnp.dot((hg * _sigmoid(hg) * hu).astype(BF16), sd_ref[...], preferred_element_type=F32)
    ff = (routed + shared).reshape(bb, ll, d)
    x2 = x1_ref[...] + mod_ref[:, 5:6, :] * ff
    out_ref[...] = x2 * lax.rsqrt(jnp.mean(x2 * x2, axis=-1, keepdims=True) + NORM_EPS) * fg_ref[...]


def _combine_call(dest, wts, first_tok, ob, h2s, x1, mod, p):
    bn, seq, d = x1.shape
    tm = CMB_TILE
    bb, ll = _tile(bn, seq, tm)
    nl = seq // ll
    tn = bn * seq
    nsteps = tn // tm
    assert first_tok % tm == 0 and dest.shape[2] == tm * TOP_K and wts.shape[1:] == (tm, TOP_K)
    tile = lambda g: (first_tok // tm + g, 0, 0)
    smem = _assign_spec(tm, lambda b, l: tile(b * nl + l))
    smem_next = _assign_spec(tm, lambda b, l: tile(jnp.minimum(b * nl + l + 1, nsteps - 1)))
    wblk = pl.BlockSpec((1, tm, TOP_K), lambda b, l: tile(b * nl + l))
    tok = pl.BlockSpec((bb, ll, d), lambda b, l: (b, l, 0))
    full = lambda a: pl.BlockSpec(a.shape, lambda b, l: (0,) * a.ndim)
    consts = (p["sh_gate"], p["sh_up"], p["sh_down"], p["final_g"])
    return pl.pallas_call(
        functools.partial(_combine_body, tm=tm, nl=nl),
        grid=(bn // bb, nl),
        in_specs=[smem, smem_next, wblk, pl.BlockSpec(memory_space=pl.ANY),
                  pl.BlockSpec((tm * PCH, LANES), lambda b, l: (first_tok // tm + b * nl + l, 0)),
                  tok, pl.BlockSpec((bb, 6, d), lambda b, l: (b, 0, 0))] + [full(c) for c in consts],
        out_specs=tok,
        out_shape=jax.ShapeDtypeStruct((bn, seq, d), F32),
        scratch_shapes=[pltpu.VMEM((2, tm * TOP_K * PCH, LANES), I32), pltpu.VMEM((tm, d), F32),
                        pltpu.SemaphoreType.DMA((2,))],
        compiler_params=_cparams(("arbitrary", "arbitrary"), VMEM_LIMIT),
        name="moe_combine_final",
    )(dest, dest, wts, ob, h2s, x1, mod, *consts)


def _layer_params(l, ada_w, ada_b, norm1_g, norm2_g, w_in, mu_shift, rw_w0, rw_w_up, rw_a0, rw_a_up, rw_g_up,
                  rw_k_k, rw_k_a, rw_r_k, rw_gn_g, rw_gn_b, gla_a_up, gla_a_bias, gla_norm_g, w_pa, w_pb, w_out,
                  router_w, router_b, exp_gate, exp_up, exp_down, sh_gate, sh_up, sh_down):
    d = D_MODEL
    wi = w_in[l]
    gla0 = RW_SHIFT_COLS
    xal0 = gla0 + QKV_W
    pad = jnp.zeros((d, XAL_W - GLA_GATE_RANK), BF16)
    w_pieces = (wi[:, :xal0].astype(BF16),
                jnp.concatenate([wi[:, xal0:xal0 + GLA_GATE_RANK].astype(BF16), pad], axis=1),
                wi[:, xal0 + GLA_GATE_RANK:].astype(BF16))
    zr = jnp.zeros((RW_W_RANK, RW_WIDTH), F32)
    hid = jnp.arange(RW_WIDTH) // RW_HEAD
    row = lambda a: a.reshape(1, -1)
    rw_t = router_w[l].T
    rw_hi = rw_t.astype(BF16)
    return dict(
        ada_w=ada_w[l], ada_b=ada_b[l], norm1_g=norm1_g[l].reshape(1, 1, d),
        norm2_g=norm2_g[l].reshape(1, 1, d), w_in=w_pieces,
        mu=mu_shift[l].reshape(1, 1, -1), w0=row(rw_w0[l]), wup=jnp.concatenate([rw_w_up[l], zr], axis=0),
        a0=row(rw_a0[l]), aup=jnp.concatenate([zr, rw_a_up[l]], axis=0), gup=rw_g_up[l].astype(BF16),
        kk=row(rw_k_k[l]), ka=row(rw_k_a[l]), rk=row(rw_r_k[l]),
        bd64=(hid[:, None] == hid[None, :]).astype(BF16),
        gn_g=row(rw_gn_g[l]), gn_b=row(rw_gn_b[l]),
        gla_aup=jnp.concatenate([gla_a_up[l], jnp.zeros((XAL_W - GLA_GATE_RANK, GLA_KW), F32)], axis=0),
        gla_ab=row(gla_a_bias[l]), gla_ng=row(gla_norm_g[l]),
        w_pa=w_pa[l].astype(BF16), w_pb=w_pb[l].astype(BF16), w_out=w_out[l].astype(BF16),
        rw_hi=rw_hi, rw_lo=(rw_t - rw_hi.astype(F32)).astype(BF16), router_b=router_b[l],
        exp_gate=exp_gate[l], exp_up=exp_up[l], exp_down=exp_down[l],
        sh_gate=sh_gate[l].astype(BF16), sh_up=sh_up[l].astype(BF16), sh_down=sh_down[l].astype(BF16),
    )


def _mixer_group(x, mod, s_rw, s_sh, s_gla, p, tn, first_tok, shared):
    qkv, xal, gg, mg, r, lw, k2, v, a_s, b_s, g, bonus, new_sh = _inproj_prep_call(x, mod, s_sh, p)
    y, rw_new = _rwscan_call(r, lw, k2, v, a_s, b_s, s_rw)
    o_b, gla_new = _gla_call(qkv, xal, gg, s_gla, p)
    x1, h2s, logits = _merge_call(y, g, bonus, o_b, mg, x, mod, p, tn, first_tok, shared)
    states = (rw_new, new_sh[:, 0, :], gla_new)
    return x1, h2s, logits, states


def _moe(h2s, logits, p):
    tn = h2s.shape[0] // PCH
    eidx, rank, wts, counts = _route_call(logits, p["router_b"])
    counts = counts[:, 0].astype(I32)
    padded = (counts + MOE_BLK - 1) // MOE_BLK * MOE_BLK
    pad_end = jnp.cumsum(padded)
    pad_start = (pad_end - padded).astype(I32)
    nb = (tn * TOP_K + N_EXPERTS * (MOE_BLK - 1)) // MOE_BLK + 1
    tables = _expert_tables(counts, pad_start, pad_end, nb)
    dest = _dest_call(eidx, rank, pad_start)
    dest = jnp.swapaxes(dest, 1, 2).reshape(dest.shape[0], 1, -1)
    wts = jnp.swapaxes(wts, 1, 2)
    xs = _dispatch_call(dest, h2s, nb * MOE_BLK)
    ob = _expert_call(tables, xs, p["exp_gate"], p["exp_up"], p["exp_down"])
    return ob, dest, wts


def kernel(x_prompt, x_sample, c_prompt, c_sample, state_rwkv, state_shift, state_gla, ada_w, ada_b, norm1_g,
           norm2_g, w_in, mu_shift, rw_w0, rw_w_up, rw_a0, rw_a_up, rw_g_up, rw_k_k, rw_k_a, rw_r_k, rw_gn_g,
           rw_gn_b, gla_a_up, gla_a_bias, gla_norm_g, w_pa, w_pb, w_out, router_w, router_b, exp_gate, exp_up,
           exp_down, sh_gate, sh_up, sh_down, final_g):
    depth = ada_w.shape[0]
    bp, bs = x_prompt.shape[0], x_sample.shape[0]
    tp = bp * x_prompt.shape[1]
    tn = tp + bs * x_sample.shape[1]
    xs_g = [x_prompt, x_sample]
    c_all = jnp.concatenate([c_prompt, c_sample], axis=0)
    zeros = lambda shape: jnp.zeros(shape, x_prompt.dtype)
    new_states = [[], []]
    fg = final_g.reshape(1, 1, D_MODEL)
    for l in range(depth):
        p = _layer_params(l, ada_w, ada_b, norm1_g, norm2_g, w_in, mu_shift, rw_w0, rw_w_up, rw_a0, rw_a_up,
                          rw_g_up, rw_k_k, rw_k_a, rw_r_k, rw_gn_g, rw_gn_b, gla_a_up, gla_a_bias, gla_norm_g,
                          w_pa, w_pb, w_out, router_w, router_b, exp_gate, exp_up, exp_down, sh_gate, sh_up,
                          sh_down)
        p["final_g"] = fg
        mod_all = _mod_call(c_all, p["ada_w"], p["ada_b"])
        mods = [mod_all[:bp], mod_all[bp:]]
        states_in = [
            (zeros((bp, RW_HEADS, RW_HEAD, RW_HEAD)), zeros((bp, RW_SHIFT_COLS)),
             zeros((bp, GLA_HEADS, GLA_DK, GLA_DV))),
            (state_rwkv[l], state_shift[l], state_gla[l]),
        ]
        x1s, shared = [], None
        firsts = [0, tp]
        for gi in range(2):
            x1, h2_all, lg_all, st = _mixer_group(xs_g[gi], mods[gi], *states_in[gi], p, tn, firsts[gi], shared)
            shared = (h2_all, lg_all)
            x1s.append(x1)
            new_states[gi].append(st)
        ob, dest, wts = _moe(*shared, p)
        assert depth == 1, "the fused final norm assumes a single layer"
        xs_g = [_combine_call(dest, wts, firsts[gi], ob, shared[0], x1s[gi], mods[gi], p) for gi in range(2)]
    stack = lambda gi, j: new_states[gi][0][j][None] if depth == 1 else jnp.stack([s[j] for s in new_states[gi]])
    return (xs_g[0], xs_g[1], stack(0, 0), stack(0, 1), stack(0, 2), stack(1, 0), stack(1, 1), stack(1, 2))
```

```python
import functools

import jax
import jax.numpy as jnp
from jax import lax
from jax.experimental import pallas as pl
from jax.experimental.pallas import tpu as pltpu

F32, BF16, I32 = jnp.float32, jnp.bfloat16, jnp.int32

D_MODEL = 1024
RW_HEADS, RW_HEAD = 8, 64
RW_WIDTH = RW_HEADS * RW_HEAD
RW_W_RANK, RW_A_RANK, RW_G_RANK = 64, 64, 128
RW_GN_EPS = 64e-5
GLA_HEADS, GLA_DK, GLA_DV = 4, 64, 128
GLA_KW, GLA_VW = GLA_HEADS * GLA_DK, GLA_HEADS * GLA_DV
GLA_GATE_RANK = 16
GLA_GATE_TAU = 16.0
GLA_CHUNK = 16
RW_SHIFT_COLS = 3 * RW_WIDTH + RW_W_RANK + RW_A_RANK + RW_G_RANK
N_EXPERTS, TOP_K, N_GROUPS, TOPK_GROUPS = 256, 8, 8, 4
GROUP_SIZE = N_EXPERTS // N_GROUPS
EXPERT_FF = 256
ROUTED_SCALE = 2.5
NORM_EPS = 1e-6

LANES = 128
CHUNKS = D_MODEL // LANES
PCH = CHUNKS // 2
UNIT = 64
RW_SCAN_PASSES = (1, 1, 1, 1, 1)
GLA_UNITS_PER_STEP = 4
RW_UNITS_PER_STEP = 4
VMEM_LIMIT = 56 * 1024 * 1024

PA_W, QKV_W, XAL_W, GG_W, MG_W = RW_SHIFT_COLS, 2 * GLA_KW + GLA_VW, LANES, GLA_VW, 2 * D_MODEL

TOK_TILE = 256
MOE_BLK = 512
MOE_SMALL = 128
WEIGHT_SLOTS = 3
ROW_SLOTS = 3
CMB_TILE = 256
CMB_GROUP = 16

_DN = {
    "nn": (((1,), (0,)), ((), ())),
    "nt": (((1,), (1,)), ((), ())),
    "tn": (((0,), (0,)), ((), ())),
}


def _split(x, pieces):
    out, rem = [], x
    for i in range(pieces):
        p = rem.astype(BF16)
        out.append(p)
        if i + 1 < pieces:
            rem = rem - p.astype(F32)
    return out


def _mm(a, b, form="nn", passes=1):
    dn = _DN[form]
    if passes == 6:
        return lax.dot_general(a.astype(F32), b.astype(F32), dn, precision=lax.Precision.HIGHEST,
                               preferred_element_type=F32)
    if passes == 1:
        return lax.dot_general(a.astype(BF16), b.astype(BF16), dn, preferred_element_type=F32)
    ah, al = _split(a, 2)
    bh, bl = _split(b, 2)
    out = lax.dot_general(ah, bh, dn, preferred_element_type=F32)
    out = out + lax.dot_general(ah, bl, dn, preferred_element_type=F32)
    return out + lax.dot_general(al, bh, dn, preferred_element_type=F32)


def _mm01(m01, x, pieces=3):
    m = m01.astype(BF16)
    out = None
    for p in _split(x, pieces):
        t = lax.dot_general(m, p, _DN["nn"], preferred_element_type=F32)
        out = t if out is None else out + t
    return out


def _xmm01(x, m01, pieces=2):
    m = m01.astype(BF16)
    out = None
    for p in _split(x, pieces):
        t = lax.dot_general(p, m, _DN["nn"], preferred_element_type=F32)
        out = t if out is None else out + t
    return out


HI16 = -65536


def _bf16_bits(x):
    return lax.bitcast_convert_type(x.astype(BF16).astype(F32), I32)


def _unpack_pair(w):
    return lax.bitcast_convert_type(w << 16, F32), lax.bitcast_convert_type(w & HI16, F32)


def _rows_to_packed(ref, x, first=0):
    for c in range(PCH):
        lo = _bf16_bits(x[:, c * LANES:(c + 1) * LANES])
        hi = _bf16_bits(x[:, (c + PCH) * LANES:(c + PCH + 1) * LANES])
        ref[pl.ds(first * PCH + c, x.shape[0], stride=PCH), :] = ((lo >> 16) & 0xFFFF) | (hi & HI16)


def _rows_from_packed(ref, n, live=None, first=0):
    lows, highs = [], []
    for c in range(PCH):
        w = ref[pl.ds(first * PCH + c, n, stride=PCH), :]
        if live is not None:
            w = jnp.where(live, w, 0)
        lo, hi = _unpack_pair(w)
        lows.append(lo.astype(BF16))
        highs.append(hi.astype(BF16))
    return jnp.concatenate(lows + highs, axis=1)


def _slab(ref, row):
    return ref.at[pl.ds(pl.multiple_of(row * PCH, PCH), PCH)]


def _sigmoid(x):
    return 1.0 / (1.0 + jnp.exp(-x))


def _softplus(x):
    return jnp.maximum(x, 0.0) + jnp.log(1.0 + jnp.exp(-jnp.abs(x)))


def _log2(n):
    assert n > 0 and n & (n - 1) == 0, n
    return n.bit_length() - 1


def _cparams(sem, vmem=None):
    return pltpu.CompilerParams(dimension_semantics=sem, vmem_limit_bytes=vmem)


def _mod_body(c_ref, w_ref, b_ref, o_ref):
    c = c_ref[...]
    o_ref[0] = _mm(c * _sigmoid(c), w_ref[...], passes=3) + b_ref[...]


def _mod_call(c_all, ada_w, ada_b):
    bt, d = c_all.shape
    out = pl.pallas_call(
        _mod_body,
        grid=(6,),
        in_specs=[pl.BlockSpec((bt, d), lambda k: (0, 0)),
                  pl.BlockSpec((d, d), lambda k: (0, k)),
                  pl.BlockSpec((1, d), lambda k: (0, k))],
        out_specs=pl.BlockSpec((1, bt, d), lambda k: (k, 0, 0)),
        out_shape=jax.ShapeDtypeStruct((6, bt, d), F32),
        compiler_params=_cparams(("arbitrary",)),
        name="adaln_mod",
    )(c_all, ada_w, ada_b.reshape(1, 6 * d))
    return jnp.transpose(out, (1, 0, 2))


def _inproj_body(x_ref, mod_ref, g_ref, wa_ref, wx_ref, wb_ref, pa_ref, qkv_ref, xal_ref, gg_ref, mg_ref):
    bb, ll, d = x_ref.shape
    x = x_ref[...]
    y = x * lax.rsqrt(jnp.mean(x * x, axis=-1, keepdims=True) + NORM_EPS) * g_ref[...]
    h = y * (1.0 + mod_ref[:, 1:2, :]) + mod_ref[:, 0:1, :]
    hb = h.reshape(bb * ll, d).astype(BF16)
    for w_ref, outs in ((wa_ref, (pa_ref, qkv_ref)), (wx_ref, (xal_ref,)), (wb_ref, (gg_ref, mg_ref))):
        off = 0
        for ref in outs:
            w = ref.shape[-1]
            ref[...] = jnp.dot(hb, w_ref[:, off:off + w], preferred_element_type=F32).reshape(bb, ll, w)
            off += w


def _tile(bn, seq, tile):
    if seq >= tile:
        assert seq % tile == 0
        return 1, tile
    assert tile % seq == 0 and bn % (tile // seq) == 0
    return tile // seq, seq


def _rwprep_body(pa_ref, sh_ref, mu_ref, w0_ref, wup_ref, a0_ref, aup_ref, gup_ref, kk_ref, ka_ref, rk_ref,
                 bd_ref, r_o, lw_o, k_o, v_o, a_o, b_o, g_o, bon_o, nsh_o, carry):
    bb, ll, wd = pa_ref.shape
    n = bb * ll
    hw = RW_WIDTH

    @pl.when(pl.program_id(1) == 0)
    def _():
        carry[...] = sh_ref[...]

    pa = pa_ref[...]
    rolled = pltpu.roll(pa.reshape(n, wd), 1, 0).reshape(bb, ll, wd)
    tok = lax.broadcasted_iota(I32, (bb, ll, wd), 1)
    prev = jnp.where(tok == 0, carry[...], rolled)
    last = pa_ref[:, ll - 1:ll, :]
    carry[...] = last
    nsh_o[...] = last
    xs = (pa + (prev - pa) * mu_ref[...]).reshape(n, wd)

    r, k, v = xs[:, 0:hw], xs[:, hw:2 * hw], xs[:, 2 * hw:3 * hw]
    xwa = xs[:, 3 * hw:3 * hw + LANES]
    xg = xs[:, 3 * hw + LANES:]
    w_log = -_softplus(-(w0_ref[...] + _mm(jnp.tanh(xwa), wup_ref[...], passes=3))) - 0.5
    lw = -jnp.exp(w_log)
    a = _sigmoid(a0_ref[...] + _mm(xwa, aup_ref[...], passes=3))
    g = _mm(_sigmoid(xg), gup_ref[...])
    bd = bd_ref[...]
    kkv = k * kk_ref[...]
    kkn = kkv * lax.rsqrt(jnp.maximum(_xmm01(kkv * kkv, bd, pieces=1), 1e-24))
    k2 = k * (1.0 + (a - 1.0) * ka_ref[...])
    bonus = _xmm01(r * k2 * rk_ref[...], bd, pieces=1) * v
    for ref, val in ((r_o, r), (lw_o, lw), (k_o, k2), (v_o, v), (a_o, -kkn), (b_o, kkn * a), (g_o, g),
                     (bon_o, bonus)):
        ref[...] = val.reshape(bb, ll, hw)


def _inproj_prep_body(x_ref, mod_ref, g_ref, wa_ref, wx_ref, wb_ref, sh_ref, mu_ref, w0_ref, wup_ref, a0_ref, aup_ref,
                      gup_ref, kk_ref, ka_ref, rk_ref, bd_ref, qkv_o, xal_o, gg_o, mg_o, r_o, lw_o, k_o, v_o, a_o,
                      b_o, g_o, bon_o, nsh_o, pa_s, carry):
    _inproj_body(x_ref, mod_ref, g_ref, wa_ref, wx_ref, wb_ref, pa_s, qkv_o, xal_o, gg_o, mg_o)
    _rwprep_body(pa_s, sh_ref, mu_ref, w0_ref, wup_ref, a0_ref, aup_ref, gup_ref, kk_ref, ka_ref, rk_ref, bd_ref,
                 r_o, lw_o, k_o, v_o, a_o, b_o, g_o, bon_o, nsh_o, carry)


def _inproj_prep_call(x, mod, s_sh, p):
    bn, seq, d = x.shape
    bb, ll = _tile(bn, seq, TOK_TILE)
    hw, wd = RW_WIDTH, PA_W
    tok = lambda w: pl.BlockSpec((bb, ll, w), lambda b, l: (b, l, 0))
    row = lambda w: pl.BlockSpec((bb, 1, w), lambda b, l: (b, 0, 0))
    full = lambda a: pl.BlockSpec(a.shape, lambda b, l: (0,) * a.ndim)
    consts = (p["mu"], p["w0"], p["wup"], p["a0"], p["aup"], p["gup"], p["kk"], p["ka"], p["rk"], p["bd64"])
    proj_w = (QKV_W, XAL_W, GG_W, MG_W)
    shapes = lambda ws: [jax.ShapeDtypeStruct((bn, seq, w), F32) for w in ws]
    return pl.pallas_call(
        _inproj_prep_body,
        grid=(bn // bb, seq // ll),
        in_specs=[tok(d), pl.BlockSpec((bb, 6, d), lambda b, l: (b, 0, 0)), full(p["norm1_g"])]
        + [full(w) for w in p["w_in"]] + [row(wd)] + [full(c) for c in consts],
        out_specs=[tok(w) for w in proj_w] + [tok(hw)] * 8 + [row(wd)],
        out_shape=shapes(proj_w) + shapes((hw,) * 8) + [jax.ShapeDtypeStruct((bn, 1, wd), F32)],
        scratch_shapes=[pltpu.VMEM((bb, ll, wd), F32), pltpu.VMEM((bb, 1, wd), F32)],
        compiler_params=_cparams(("arbitrary", "arbitrary"), VMEM_LIMIT),
        name="norm_inproj_prep",
    )(x, mod, p["norm1_g"], *p["w_in"], s_sh.reshape(bn, 1, wd), *consts)


def _unit_masks(n, tl):
    ri = lax.broadcasted_iota(I32, (n, n), 0)
    ci = lax.broadcasted_iota(I32, (n, n), 1)
    same = (ri >> _log2(tl)) == (ci >> _log2(tl))
    return same, same & (ri > ci), same & (ri >= ci)


def _rwscan_body(r_ref, lw_ref, k_ref, v_ref, a_ref, b_ref, s0_ref, y_ref, sn_ref, st, *, nu, nseq, tl, passes):
    n = nseq * tl
    n2 = 2 * n
    p_aa, p_inv, p_apply, p_state, p_y = passes

    hd = RW_HEAD

    @pl.when(pl.program_id(1) == 0)
    def _():
        zero = jnp.zeros((hd, hd), F32)
        for q in range(nu * nseq):
            for p in range(RW_HEADS // 2):
                st[q, p] = jnp.concatenate(
                    [jnp.concatenate([s0_ref[q, 2 * p], zero], axis=1),
                     jnp.concatenate([zero, s0_ref[q, 2 * p + 1]], axis=1)], axis=0)

    same, _, incl = _unit_masks(n, tl)
    m_cum = jnp.where(incl, 1.0, 0.0)
    m_seq = jnp.where(same, 1.0, 0.0)
    ri = lax.broadcasted_iota(I32, (n2, n2), 0)
    ci = lax.broadcasted_iota(I32, (n2, n2), 1)
    rt, ct = ri & (n - 1), ci & (n - 1)
    dsame = ((rt >> _log2(tl)) == (ct >> _log2(tl))) & ((ri >> _log2(n)) == (ci >> _log2(n)))
    strict_d = dsame & (rt > ct)
    incl_d = dsame & (rt >= ct)
    eye_d = jnp.where(ri == ci, 1.0, 0.0)
    lane = lax.broadcasted_iota(I32, (1, LANES), 1)
    m0 = jnp.where(lane < RW_HEAD, 1.0, 0.0)
    m1 = 1.0 - m0

    def dup(x):
        return jnp.concatenate([x * m0, x * m1], axis=0)

    def seq_rows(x, q):
        if nseq == 1:
            return x
        return jnp.concatenate([x[q * tl:(q + 1) * tl], x[n + q * tl:n + (q + 1) * tl]], axis=0)

    def unit_rows(parts):
        if nseq == 1:
            return parts[0]
        return jnp.concatenate([p[0:tl] for p in parts] + [p[tl:2 * tl] for p in parts], axis=0)

    chains = [(u, p) for u in range(nu) for p in range(RW_HEADS // 2)]
    ids = range(len(chains))
    cat0 = lambda *xs: jnp.concatenate(xs, axis=0)

    def ld(ref, c):
        u, p = chains[c]
        return ref[u * nseq:(u + 1) * nseq, :, p * LANES:(p + 1) * LANES].reshape(n, LANES)

    lw = [ld(lw_ref, c) for c in ids]
    cum = [_mm01(m_cum, x) for x in lw]
    tot = [_mm01(m_seq, x) for x in lw]
    e_c = [jnp.exp(x) for x in cum]
    e_n = [jnp.exp(-x) for x in cum]
    e_l = [jnp.exp(t - x) for t, x in zip(tot, cum)]
    at_d = [dup(ld(a_ref, c) * jnp.exp(cum[c] - lw[c])) for c in ids]
    rt_d = [dup(ld(r_ref, c) * e_c[c]) for c in ids]
    bt_d = [dup(ld(b_ref, c) * e_n[c]) for c in ids]
    kt_d = [dup(ld(k_ref, c) * e_n[c]) for c in ids]
    bh_d = [dup(ld(b_ref, c) * e_l[c]) for c in ids]
    kh_d = [dup(ld(k_ref, c) * e_l[c]) for c in ids]
    v_d = [dup(ld(v_ref, c)) for c in ids]
    aa = [_mm(cat0(at_d[c], rt_d[c]), cat0(bt_d[c], kt_d[c]), "nt", p_aa) for c in ids]
    a_ab = [jnp.where(strict_d, x[0:n2, 0:n2], 0.0) for x in aa]
    a_ak = [jnp.where(strict_d, x[0:n2, n2:], 0.0) for x in aa]
    a_rb = [jnp.where(incl_d, x[n2:, 0:n2], 0.0) for x in aa]
    a_rk = [jnp.where(incl_d, x[n2:, n2:], 0.0) for x in aa]
    zy = [_mm(cat0(a_ak[c], a_rk[c]), v_d[c], passes=p_apply) for c in ids]
    tinv = [eye_d + x for x in a_ab]
    nk = a_ab
    for _ in range(_log2(tl) - 1):
        nk = [_mm(x, x, passes=p_inv) for x in nk]
        tinv = [t + _mm(t, x, passes=p_inv) for t, x in zip(tinv, nk)]
    wu = [_mm(tinv[c], jnp.concatenate([at_d[c], zy[c][0:n2]], axis=1), passes=p_apply) for c in ids]
    seqs = range(nseq)
    srow = lambda c, q: (chains[c][0] * nseq + q, chains[c][1])
    s_old = [[st[srow(c, q)] for q in seqs] for c in ids]
    xs = [[_mm(cat0(seq_rows(wu[c][:, 0:LANES], q), seq_rows(rt_d[c], q)), s_old[c][q], "nt", p_state)
           for q in seqs] for c in ids]
    u_q = [[xs[c][q][0:2 * tl] + seq_rows(wu[c][:, LANES:], q) for q in seqs] for c in ids]
    for c in ids:
        for q in seqs:
            g_c = jnp.exp(tot[c][q * tl:q * tl + 1, :])
            st[srow(c, q)] = s_old[c][q] * g_c + _mm(cat0(u_q[c][q], seq_rows(v_d[c], q)),
                                                     cat0(seq_rows(bh_d[c], q), seq_rows(kh_d[c], q)), "tn", p_state)
    for c in ids:
        u, p = chains[c]
        y_d = (unit_rows([xs[c][q][2 * tl:] for q in seqs]) + _mm(a_rb[c], unit_rows(u_q[c]), passes=p_y)
               + zy[c][n2:])
        y_ref[u * nseq:(u + 1) * nseq, :, p * LANES:(p + 1) * LANES] = (y_d[0:n] + y_d[n:]).reshape(nseq, tl, LANES)

    @pl.when(pl.program_id(1) == pl.num_programs(1) - 1)
    def _():
        for q in range(nu * nseq):
            for p in range(RW_HEADS // 2):
                s = st[q, p]
                sn_ref[q, 2 * p] = s[0:hd, 0:hd]
                sn_ref[q, 2 * p + 1] = s[hd:, hd:]


def _unit_shape(bn, seq):
    if seq >= UNIT:
        assert seq % UNIT == 0
        return 1, UNIT
    assert UNIT % seq == 0 and bn % (UNIT // seq) == 0
    return UNIT // seq, seq


def _rwscan_call(r, lw, k2, v, a_s, b_s, s0, passes=RW_SCAN_PASSES):
    bn, seq, hw = r.shape
    nseq, tl = _unit_shape(bn, seq)
    nu = RW_UNITS_PER_STEP if bn % (RW_UNITS_PER_STEP * nseq) == 0 else 1
    rows = nu * nseq
    tok = pl.BlockSpec((rows, tl, hw), lambda b, c: (b, c, 0))
    stt = pl.BlockSpec((rows, RW_HEADS, RW_HEAD, RW_HEAD), lambda b, c: (b, 0, 0, 0))
    return pl.pallas_call(
        functools.partial(_rwscan_body, nu=nu, nseq=nseq, tl=tl, passes=passes),
        grid=(bn // rows, seq // tl),
        in_specs=[tok] * 6 + [stt],
        out_specs=[tok, stt],
        out_shape=[jax.ShapeDtypeStruct((bn, seq, hw), F32), jax.ShapeDtypeStruct(s0.shape, F32)],
        scratch_shapes=[pltpu.VMEM((rows, RW_HEADS // 2, LANES, LANES), F32)],
        compiler_params=_cparams(("arbitrary", "arbitrary"), VMEM_LIMIT),
        name="rwkv_scan",
    )(r, lw, k2, v, a_s, b_s, s0)


def _gla_body(qkv_ref, xal_ref, gate_ref, aup_ref, ab_ref, ng_ref, s0_ref, o_ref, sn_ref, st, *, nu, nseq, tl, cs):
    n = nseq * tl
    n2 = 2 * n
    nsub = tl // cs

    @pl.when(pl.program_id(1) == 0)
    def _():
        zero = jnp.zeros((GLA_DV, GLA_DK), F32)
        for q in range(nu * nseq):
            for p in range(GLA_HEADS // 2):
                st[q, p] = jnp.concatenate(
                    [jnp.concatenate([s0_ref[q, 2 * p].T, zero], axis=1),
                     jnp.concatenate([zero, s0_ref[q, 2 * p + 1].T], axis=1)], axis=0)

    same, _, incl = _unit_masks(n, cs)
    m_cum = jnp.where(incl, 1.0, 0.0)
    m_sub = jnp.where(same, 1.0, 0.0)
    ri = lax.broadcasted_iota(I32, (n2, n2), 0)
    ci = lax.broadcasted_iota(I32, (n2, n2), 1)
    rt, ct = ri & (n - 1), ci & (n - 1)
    causal_d = ((rt >> _log2(cs)) == (ct >> _log2(cs))) & ((ri >> _log2(n)) == (ci >> _log2(n))) & (rt >= ct)
    lane = lax.broadcasted_iota(I32, (1, LANES), 1)
    m0 = jnp.where(lane < GLA_DK, 1.0, 0.0)
    m1 = 1.0 - m0
    sr = lax.broadcasted_iota(I32, (2 * GLA_DV, LANES), 0)
    sc = lax.broadcasted_iota(I32, (2 * GLA_DV, LANES), 1)
    st_mask = jnp.where((sr >> _log2(GLA_DV)) == (sc >> _log2(GLA_DK)), 1.0, 0.0)

    def dup(x):
        return jnp.concatenate([x * m0, x * m1], axis=0)

    chains = [(u, p) for u in range(nu) for p in range(GLA_HEADS // 2)]
    ids = range(len(chains))
    urows = lambda u: slice(u * nseq, (u + 1) * nseq)
    ng = ng_ref[...]
    la_all = [-_softplus(-(_mm(xal_ref[urows(u), :, :].reshape(n, LANES), aup_ref[...], passes=3) + ab_ref[...]))
              * (1.0 / GLA_GATE_TAU) for u in range(nu)]

    def ld(ref, c, off, width):
        return ref[urows(chains[c][0]), :, off:off + width].reshape(n, width)

    q = [ld(qkv_ref, c, chains[c][1] * LANES, LANES) * (GLA_DK ** -0.5) for c in ids]
    k = [ld(qkv_ref, c, GLA_KW + chains[c][1] * LANES, LANES) for c in ids]
    vp = [ld(qkv_ref, c, 2 * GLA_KW + chains[c][1] * 2 * GLA_DV, 2 * GLA_DV) for c in ids]
    la = [la_all[u][:, p * LANES:(p + 1) * LANES] for u, p in chains]
    bc = [_mm01(m_cum, x) for x in la]
    bl = [_mm01(m_sub, x) for x in la]
    qe = [q[c] * jnp.exp(bc[c]) for c in ids]
    ke = [k[c] * jnp.exp(-bc[c]) for c in ids]
    kd = [k[c] * jnp.exp(bl[c] - bc[c]) for c in ids]
    att = [jnp.where(causal_d, _mm(dup(qe[c]), dup(ke[c]), "nt", passes=1), 0.0) for c in ids]
    v_st = [jnp.concatenate([x[:, 0:GLA_DV], x[:, GLA_DV:]], axis=0) for x in vp]
    o_st = [_mm(att[c], v_st[c], passes=1) for c in ids]
    upd = [[_mm(vp[c][r0:r0 + cs], kd[c][r0:r0 + cs], "tn", passes=1) for r0 in range(0, n, cs)] for c in ids]
    inter = [[None] * (n // cs) for _ in ids]
    for sq in range(nseq):
        s = [st[chains[c][0] * nseq + sq, chains[c][1]] for c in ids]
        for j in range(nsub):
            i = sq * nsub + j
            r0 = i * cs
            for c in ids:
                inter[c][i] = _mm(qe[c][r0:r0 + cs], s[c], "nt", passes=1)
                s[c] = s[c] * jnp.exp(bl[c][r0:r0 + 1, :]) + st_mask * upd[c][i]
        for c in ids:
            st[chains[c][0] * nseq + sq, chains[c][1]] = s[c]
    for c in ids:
        u, p = chains[c]
        o = o_st[c] + jnp.concatenate([x[:, 0:GLA_DV] for x in inter[c]] + [x[:, GLA_DV:] for x in inter[c]], axis=0)
        o = o * lax.rsqrt(jnp.mean(o * o, axis=-1, keepdims=True) + NORM_EPS) * ng
        goff = p * 2 * GLA_DV
        gp = ld(gate_ref, c, goff, 2 * GLA_DV)
        g_st = jnp.concatenate([gp[:, 0:GLA_DV], gp[:, GLA_DV:]], axis=0)
        ob = o * (g_st * _sigmoid(g_st))
        o_ref[urows(u), :, goff:goff + GLA_DV] = ob[0:n].reshape(nseq, tl, GLA_DV)
        o_ref[urows(u), :, goff + GLA_DV:goff + 2 * GLA_DV] = ob[n:].reshape(nseq, tl, GLA_DV)

    @pl.when(pl.program_id(1) == pl.num_programs(1) - 1)
    def _():
        for q in range(nu * nseq):
            for p in range(GLA_HEADS // 2):
                s = st[q, p]
                sn_ref[q, 2 * p] = s[0:GLA_DV, 0:GLA_DK].T
                sn_ref[q, 2 * p + 1] = s[GLA_DV:, GLA_DK:].T


def _gla_call(qkv, xal, gate, s0, p):
    bn, seq, _ = qkv.shape
    nseq, tl = _unit_shape(bn, seq)
    cs = min(GLA_CHUNK, seq)
    assert tl % cs == 0
    nu = GLA_UNITS_PER_STEP if bn % (GLA_UNITS_PER_STEP * nseq) == 0 else 1
    rows = nu * nseq
    tok = lambda w: pl.BlockSpec((rows, tl, w), lambda b, c: (b, c, 0))
    full = lambda a: pl.BlockSpec(a.shape, lambda b, c: (0,) * a.ndim)
    stt = pl.BlockSpec((rows, GLA_HEADS, GLA_DK, GLA_DV), lambda b, c: (b, 0, 0, 0))
    consts = (p["gla_aup"], p["gla_ab"], p["gla_ng"])
    return pl.pallas_call(
        functools.partial(_gla_body, nu=nu, nseq=nseq, tl=tl, cs=cs),
        grid=(bn // rows, seq // tl),
        in_specs=[tok(QKV_W), tok(XAL_W), tok(GG_W)] + [full(c) for c in consts] + [stt],
        out_specs=[tok(GLA_VW), stt],
        out_shape=[jax.ShapeDtypeStruct((bn, seq, GLA_VW), F32), jax.ShapeDtypeStruct(s0.shape, F32)],
        scratch_shapes=[pltpu.VMEM((rows, GLA_HEADS // 2, 2 * GLA_DV, LANES), F32)],
        compiler_params=_cparams(("arbitrary", "arbitrary"), VMEM_LIMIT),
        name="gla_chunked",
    )(qkv, xal, gate, *consts, s0)


def _merge_body(y_ref, g_ref, bon_ref, ob_ref, mg_ref, x_ref, mod_ref, gng_ref, gnb_ref, bd_ref, wpa_ref,
                wpb_ref, wout_ref, n2_ref, rwh_ref, rwl_ref, *rest):
    x1_o, h2_o, lg_o = rest[-3:]
    bb, ll, d = x_ref.shape
    n = bb * ll
    hw = RW_WIDTH
    bd = bd_ref[...]
    y = y_ref[...].reshape(n, hw)
    mu = _xmm01(y, bd, pieces=2) * (1.0 / RW_HEAD)
    dv = y - mu
    var = _xmm01(dv * dv, bd, pieces=1) * (1.0 / RW_HEAD)
    yn = dv * lax.rsqrt(var + RW_GN_EPS) * gng_ref[...] + gnb_ref[...]
    o_a = (yn + bon_ref[...].reshape(n, hw)) * g_ref[...].reshape(n, hw)
    o_b = ob_ref[...].reshape(n, GLA_VW)
    mg = mg_ref[...].reshape(n, 2 * d)
    merged = _sigmoid(mg[:, 0:d]) * _mm(o_a, wpa_ref[...]) + _sigmoid(mg[:, d:]) * _mm(o_b, wpb_ref[...])
    mix = _mm(merged, wout_ref[...]).reshape(bb, ll, d)
    x1 = x_ref[...] + mod_ref[:, 2:3, :] * mix
    x1_o[...] = x1
    yn2 = x1 * lax.rsqrt(jnp.mean(x1 * x1, axis=-1, keepdims=True) + NORM_EPS) * n2_ref[...]
    h2 = (yn2 * (1.0 + mod_ref[:, 4:5, :]) + mod_ref[:, 3:4, :]).reshape(n, d)
    hh, hl = _split(h2, 2)
    rwh, rwl = rwh_ref[...], rwl_ref[...]
    nt = lambda a, b: lax.dot_general(a, b, _DN["nt"], preferred_element_type=F32)
    lg_o[...] = nt(rwh, hh) + nt(rwl, hh) + nt(rwh, hl)
    _rows_to_packed(h2_o, h2)


def _merge_call(y, g, bonus, o_b, mg, x, mod, p, tn, first_tok, shared):
    bn, seq, d = x.shape
    bb, ll = _tile(bn, seq, TOK_TILE)
    nl = seq // ll
    assert first_tok % (bb * ll) == 0
    t0 = first_tok // (bb * ll)
    n_in = 7 + 9
    extra = [] if shared is None else list(shared)
    alias = {} if shared is None else {n_in: 1, n_in + 1: 2}
    tok = lambda w: pl.BlockSpec((bb, ll, w), lambda b, l: (b, l, 0))
    full = lambda a: pl.BlockSpec(a.shape, lambda b, l: (0,) * a.ndim)
    consts = (p["gn_g"], p["gn_b"], p["bd64"], p["w_pa"], p["w_pb"], p["w_out"], p["norm2_g"], p["rw_hi"],
              p["rw_lo"])
    return pl.pallas_call(
        _merge_body,
        grid=(bn // bb, nl),
        in_specs=[tok(RW_WIDTH)] * 3 + [tok(GLA_VW), tok(MG_W), tok(d),
                                        pl.BlockSpec((bb, 6, d), lambda b, l: (b, 0, 0))] + [full(c) for c in consts]
        + [pl.BlockSpec(memory_space=pl.ANY)] * len(extra),
        out_specs=[tok(d),
                   pl.BlockSpec((bb * ll * PCH, LANES), lambda b, l: (t0 + b * nl + l, 0)),
                   pl.BlockSpec((N_EXPERTS, bb * ll), lambda b, l: (0, t0 + b * nl + l))],
        out_shape=[jax.ShapeDtypeStruct((bn, seq, d), F32),
                   jax.ShapeDtypeStruct((tn * PCH, LANES), I32),
                   jax.ShapeDtypeStruct((N_EXPERTS, tn), F32)],
        input_output_aliases=alias,
        compiler_params=_cparams(("arbitrary", "arbitrary"), VMEM_LIMIT),
        name="merge_outproj_router",
    )(y, g, bonus, o_b, mg, x, mod, *consts, *extra)


def _route_body(lg_ref, rb_ref, e_o, rk_o, w_o, cnt_o, carry):
    ne, tm = lg_ref.shape

    @pl.when(pl.program_id(0) == 0)
    def _():
        carry[...] = jnp.zeros_like(carry)

    neg = -jnp.inf
    scores = _sigmoid(lg_ref[...])
    sel = scores + rb_ref[...]
    row_i = lax.broadcasted_iota(I32, (ne, tm), 0)
    row = row_i.astype(F32)
    grp = (row_i >> _log2(GROUP_SIZE)).astype(F32)

    def first_max(x, ids, none):
        m = jnp.max(x, axis=0, keepdims=True)
        return m, jnp.min(jnp.where(x == m, ids, none), axis=0, keepdims=True)

    gs = []
    gids = lax.broadcasted_iota(I32, (GROUP_SIZE, tm), 0)
    for gidx in range(N_GROUPS):
        rows = slice(gidx * GROUP_SIZE, (gidx + 1) * GROUP_SIZE)
        sg = _sigmoid(lg_ref[rows, :]) + rb_ref[rows, :]
        ids = (gids + gidx * GROUP_SIZE).astype(F32)
        m1, i1 = first_max(sg, ids, float(ne))
        gs.append(m1 + jnp.max(jnp.where(ids == i1, neg, sg), axis=0, keepdims=True))
    gs = jnp.concatenate(gs, axis=0)
    gid = lax.broadcasted_iota(I32, (N_GROUPS, tm), 0).astype(F32)
    cur = jnp.full((ne, tm), neg, F32)
    for _ in range(TOPK_GROUPS):
        _, gi = first_max(gs, gid, float(N_GROUPS))
        cur = jnp.where(grp == gi, sel, cur)
        gs = jnp.where(gid == gi, neg, gs)

    pm = jnp.zeros((ne, tm), F32)
    eidx, wts = [], []
    for _ in range(TOP_K):
        _, ei = first_max(cur, row, float(ne))
        hit = row == ei
        pm = jnp.where(hit, 1.0, pm)
        eidx.append(ei)
        wts.append(jnp.sum(jnp.where(hit, scores, 0.0), axis=0, keepdims=True))
        cur = jnp.where(hit, neg, cur)
    wsum = wts[0]
    for w in wts[1:]:
        wsum = wsum + w

    ri = lax.broadcasted_iota(I32, (tm, tm), 0)
    ci = lax.broadcasted_iota(I32, (tm, tm), 1)
    earlier = jnp.where(ri < ci, 1.0, 0.0)
    rank = _mm(pm, earlier, passes=1) + carry[...]
    carry[...] = carry[...] + jnp.sum(pm, axis=1, keepdims=True)
    cnt_o[...] = carry[...]

    rks = [jnp.sum(jnp.where(row == e, rank, 0.0), axis=0, keepdims=True) for e in eidx]
    e_o[0] = jnp.concatenate(eidx, axis=0).astype(I32)
    rk_o[0] = jnp.concatenate(rks, axis=0).astype(I32)
    w_o[0] = jnp.concatenate([w / wsum * ROUTED_SCALE for w in wts], axis=0)


def _route_call(logits_t, router_b):
    ne, tn = logits_t.shape
    tm = TOK_TILE
    assert tn % tm == 0
    col = pl.BlockSpec((ne, 1), lambda i: (0, 0))
    tab = pl.BlockSpec((1, TOP_K, tm), lambda i: (i, 0, 0))
    tab_shape = (tn // tm, TOP_K, tm)
    return pl.pallas_call(
        _route_body,
        grid=(tn // tm,),
        in_specs=[pl.BlockSpec((ne, tm), lambda i: (0, i)), col],
        out_specs=[tab, tab, tab, col],
        out_shape=[jax.ShapeDtypeStruct(tab_shape, I32), jax.ShapeDtypeStruct(tab_shape, I32),
                   jax.ShapeDtypeStruct(tab_shape, F32), jax.ShapeDtypeStruct((ne, 1), F32)],
        scratch_shapes=[pltpu.VMEM((ne, 1), F32)],
        compiler_params=_cparams(("arbitrary",)),
        name="moe_route",
    )(logits_t, router_b.reshape(ne, 1))


def _dest_body(e_ref, rk_ref, ps_ref, d_o):
    ne, tm = ps_ref.shape[0], e_ref.shape[2]
    ids = lax.broadcasted_iota(I32, (ne, tm), 0)
    ps = ps_ref[...]
    for t in range(e_ref.shape[0]):
        first = [jnp.sum(jnp.where(ids == e_ref[t, kk:kk + 1, :], ps, 0.0), axis=0, keepdims=True)
                 for kk in range(TOP_K)]
        d_o[t] = (jnp.concatenate(first, axis=0).astype(I32) + rk_ref[t]) * PCH


def _dest_call(eidx, rank, pad_start):
    nt, _, tm = eidx.shape
    ne = pad_start.shape[0]
    per = next(k for k in (4, 2, 1) if nt % k == 0)
    tab = pl.BlockSpec((per, TOP_K, tm), lambda i: (i, 0, 0))
    return pl.pallas_call(
        _dest_body,
        grid=(nt // per,),
        in_specs=[tab, tab, pl.BlockSpec((ne, 1), lambda i: (0, 0))],
        out_specs=tab,
        out_shape=jax.ShapeDtypeStruct(eidx.shape, I32),
        compiler_params=_cparams(("arbitrary",)),
        name="moe_dest",
    )(eidx, rank, pad_start.astype(F32).reshape(ne, 1))


def _pslab(ref, offset):
    return ref.at[pl.ds(pl.multiple_of(offset, PCH), PCH)]


def _dispatch_body(d_ref, h2_ref, xs_hbm, sem, *, tm):
    def issue(m, carry):
        for kk in range(TOP_K):
            pltpu.make_async_copy(_slab(h2_ref, m), _pslab(xs_hbm, d_ref[0, 0, m * TOP_K + kk]), sem).start(priority=kk % 2)
        return carry

    lax.fori_loop(0, tm, issue, 0)
    all_rows = xs_hbm.at[pl.ds(0, tm * TOP_K * PCH)]
    pltpu.make_async_copy(all_rows, all_rows, sem).wait()


def _assign_spec(tm, index_map):
    return pl.BlockSpec((1, 1, tm * TOP_K), index_map, memory_space=pltpu.SMEM)


def _dispatch_call(dest, h2s, n_rows):
    tn = h2s.shape[0] // PCH
    tm = TOK_TILE
    assert dest.shape == (tn // tm, 1, tm * TOP_K)
    blk = _assign_spec(tm, lambda i: (i, 0, 0))
    return pl.pallas_call(
        functools.partial(_dispatch_body, tm=tm),
        grid=(tn // tm,),
        in_specs=[blk, pl.BlockSpec((tm * PCH, LANES), lambda i: (i, 0))],
        out_specs=pl.BlockSpec(memory_space=pl.ANY),
        out_shape=jax.ShapeDtypeStruct((n_rows * PCH, LANES), I32),
        scratch_shapes=[pltpu.SemaphoreType.DMA],
        compiler_params=_cparams(("arbitrary",)),
        name="moe_dispatch",
    )(dest, h2s)


def _expert_body(bi_ref, nr_ref, ld_ref, nx_ref, xs_hbm, wg_hbm, wu_hbm, wd_hbm, ob_hbm, wg_buf, wu_buf, wd_buf,
                 wg_bf, wu_bf, wd_bf, xbuf, obuf, sem, xsem, osem):
    i = pl.program_id(0)
    nsteps = pl.num_programs(0)
    nr = nr_ref[i]
    slot = ld_ref[i]
    blk_rows = MOE_BLK * PCH
    small_rows = MOE_SMALL * PCH

    def row_block(j, go):
        s = lax.rem(j, ROW_SLOTS)
        base = pl.multiple_of(bi_ref[j] * blk_rows, blk_rows)
        head = pltpu.make_async_copy(xs_hbm.at[pl.ds(base, small_rows)], xbuf.at[s].at[pl.ds(0, small_rows)],
                                     xsem.at[s])
        rest = pltpu.make_async_copy(xs_hbm.at[pl.ds(base + small_rows, blk_rows - small_rows)],
                                     xbuf.at[s].at[pl.ds(small_rows, blk_rows - small_rows)], xsem.at[s])
        go(head)
        pl.when(nr_ref[j] > MOE_SMALL)(lambda: go(rest))

    start = lambda cp: cp.start()
    wait = lambda cp: cp.wait()

    @pl.when(i == 0)
    def _():
        for j in range(ROW_SLOTS - 1):
            row_block(j, start)

    @pl.when(i + ROW_SLOTS - 1 < nsteps)
    def _():
        row_block(i + ROW_SLOTS - 1, start)

    def fetch(e, s):
        return (pltpu.make_async_copy(wg_hbm.at[e], wg_buf.at[s], sem.at[s]),
                pltpu.make_async_copy(wu_hbm.at[e], wu_buf.at[s], sem.at[s]),
                pltpu.make_async_copy(wd_hbm.at[e], wd_buf.at[s], sem.at[s]))

    @pl.when(i == 0)
    def _():
        for s in range(WEIGHT_SLOTS - 1):
            e0 = nx_ref[nx_ref.shape[0] - (WEIGHT_SLOTS - 1) + s]

            @pl.when(e0 >= 0)
            def _():
                for k, cp in enumerate(fetch(e0, s)):
                    cp.start(priority=k % 2)

    @pl.when(slot >= 0)
    def _():
        for cp in fetch(0, slot):
            cp.wait()

        @pl.when(nx_ref[i] >= 0)
        def _():
            for k, cp in enumerate(fetch(nx_ref[i], lax.rem(slot + WEIGHT_SLOTS - 1, WEIGHT_SLOTS))):
                cp.start(priority=k % 2)

        wg_bf[...] = wg_buf[slot].astype(BF16)
        wu_bf[...] = wu_buf[slot].astype(BF16)
        wd_bf[...] = wd_buf[slot].astype(BF16)

    def out_block(j, go):
        s = lax.rem(j, 2)
        base = pl.multiple_of(bi_ref[j] * blk_rows, blk_rows)
        head = pltpu.make_async_copy(obuf.at[s].at[pl.ds(0, small_rows)], ob_hbm.at[pl.ds(base, small_rows)],
                                     osem.at[s])
        rest = pltpu.make_async_copy(obuf.at[s].at[pl.ds(small_rows, blk_rows - small_rows)],
                                     ob_hbm.at[pl.ds(base + small_rows, blk_rows - small_rows)], osem.at[s])
        pl.when(nr_ref[j] > 0)(lambda: go(head))
        pl.when(nr_ref[j] > MOE_SMALL)(lambda: go(rest))

    row_block(i, wait)
    pl.when(i >= 2)(lambda: out_block(i - 2, wait))
    ob_ref = obuf.at[lax.rem(i, 2)]

    def work(n):
        rid = lax.broadcasted_iota(I32, (n, LANES), 0)
        x = _rows_from_packed(xbuf.at[lax.rem(i, ROW_SLOTS)], n, rid < nr)
        hg = jnp.dot(x, wg_bf[...], preferred_element_type=F32)
        hu = jnp.dot(x, wu_bf[...], preferred_element_type=F32)
        hh = (hg * _sigmoid(hg) * hu).astype(BF16)
        _rows_to_packed(ob_ref, jnp.dot(hh, wd_bf[...], preferred_element_type=F32), 0)

    pl.when(nr > MOE_SMALL)(lambda: work(MOE_BLK))
    pl.when((nr > 0) & (nr <= MOE_SMALL))(lambda: work(MOE_SMALL))
    out_block(i, start)

    @pl.when(i == nsteps - 1)
    def _():
        out_block(i - 1, wait)
        out_block(i, wait)


def _expert_tables(counts, pad_start, pad_end, nb):
    ne = counts.shape[0]
    experts = jnp.arange(ne, dtype=I32)
    first_row = jnp.arange(nb, dtype=I32) * MOE_BLK
    block_e = jnp.minimum(jnp.sum(pad_end[None, :] <= first_row[:, None], axis=1), ne - 1).astype(I32)
    mine = block_e[:, None] == experts[None, :]
    pick = lambda v: jnp.sum(jnp.where(mine, v[None, :], 0), axis=1)
    has = counts > 0
    ordinal = jnp.cumsum(has.astype(I32)) - 1
    start_b, count_b, ord_b = pick(pad_start), pick(counts), pick(ordinal)
    block_rows = jnp.clip(start_b + count_b - first_row, 0, MOE_BLK).astype(I32)
    block_i = jnp.minimum(jnp.arange(nb, dtype=I32), pad_end[-1] // MOE_BLK - 1).astype(I32)
    starts = (first_row == start_b) & (block_rows > 0)
    load_slot = jnp.where(starts, ord_b % WEIGHT_SLOTS, -1).astype(I32)
    nth = lambda want: jnp.max(jnp.where(has[None, :] & (ordinal[None, :] == want[:, None]), experts[None, :], -1),
                               axis=1)
    ahead = jnp.where(starts, nth(ord_b + WEIGHT_SLOTS - 1), -1)
    lead = nth(jnp.arange(WEIGHT_SLOTS - 1, dtype=I32))
    return block_i, block_rows, load_slot, jnp.concatenate([ahead, lead]).astype(I32)


def _expert_call(tables, xs, wg, wu, wd):
    nb = xs.shape[0] // (MOE_BLK * PCH)
    assert nb >= ROW_SLOTS
    d, ff = wg.shape[1], wg.shape[2]
    hbm = pl.BlockSpec(memory_space=pl.ANY)
    grid_spec = pltpu.PrefetchScalarGridSpec(
        num_scalar_prefetch=4,
        grid=(nb,),
        in_specs=[hbm, hbm, hbm, hbm],
        out_specs=hbm,
        scratch_shapes=[pltpu.VMEM((WEIGHT_SLOTS, d, ff), F32), pltpu.VMEM((WEIGHT_SLOTS, d, ff), F32),
                        pltpu.VMEM((WEIGHT_SLOTS, ff, d), F32),
                        pltpu.VMEM((d, ff), BF16), pltpu.VMEM((d, ff), BF16), pltpu.VMEM((ff, d), BF16),
                        pltpu.VMEM((ROW_SLOTS, MOE_BLK * PCH, LANES), I32), pltpu.VMEM((2, MOE_BLK * PCH, LANES), I32),
                        pltpu.SemaphoreType.DMA((WEIGHT_SLOTS,)), pltpu.SemaphoreType.DMA((ROW_SLOTS,)),
                        pltpu.SemaphoreType.DMA((2,))],
    )
    return pl.pallas_call(
        _expert_body,
        grid_spec=grid_spec,
        out_shape=jax.ShapeDtypeStruct(xs.shape, I32),
        compiler_params=_cparams(("arbitrary",), VMEM_LIMIT),
        name="moe_experts",
    )(*tables, xs, wg, wu, wd)


def _combine_body(d_ref, dn_ref, wt_ref, ob_hbm, h2_ref, x1_ref, mod_ref, sg_ref, su_ref,
                  sd_ref, fg_ref, out_ref, gbuf, rbuf, sem, *, tm, nl):
    bb, ll, d = x1_ref.shape
    step = pl.program_id(0) * nl + pl.program_id(1)
    last = pl.num_programs(0) * nl - 1
    parity = lax.rem(step, 2)
    grp = CMB_GROUP

    def request(d_tab, g, s):
        for j in range(grp):
            m = g * grp + j
            for kk in range(TOP_K):
                pltpu.make_async_copy(_pslab(ob_hbm, d_tab[0, 0, m * TOP_K + kk]), _slab(gbuf.at[s], kk * tm + m),
                                      sem.at[s]).start(priority=kk % 2)

    def mix(g, s):
        r0 = pl.multiple_of(g * grp, grp)
        w = wt_ref[0, pl.ds(r0, grp), :]
        wk = [w[:, kk:kk + 1] for kk in range(TOP_K)]
        for c in range(PCH):
            acc_lo = acc_hi = None
            for kk in range(TOP_K):
                words = gbuf[s, pl.ds((kk * tm + r0) * PCH + c, grp, stride=PCH), :]
                lo, hi = _unpack_pair(words)
                acc_lo = wk[kk] * lo if acc_lo is None else acc_lo + wk[kk] * lo
                acc_hi = wk[kk] * hi if acc_hi is None else acc_hi + wk[kk] * hi
            rbuf[pl.ds(r0, grp), c * LANES:(c + 1) * LANES] = acc_lo
            rbuf[pl.ds(r0, grp), (c + PCH) * LANES:(c + PCH + 1) * LANES] = acc_hi

    @pl.when(step == 0)
    def _():
        def first(g, carry):
            request(d_ref, g, 0)
            return carry
        lax.fori_loop(0, tm // grp, first, 0)

    def run(slot):
        pltpu.make_async_copy(ob_hbm.at[pl.ds(0, tm * TOP_K * PCH)], gbuf.at[slot], sem.at[slot]).wait()

        @pl.when(step < last)
        def _():
            def both(g, carry):
                request(dn_ref, g, 1 - slot)
                mix(g, slot)
                return carry
            lax.fori_loop(0, tm // grp, both, 0)

        @pl.when(step == last)
        def _():
            def only(g, carry):
                mix(g, slot)
                return carry
            lax.fori_loop(0, tm // grp, only, 0)

    for slot in range(2):
        pl.when(parity == slot)(functools.partial(run, slot))

    routed = rbuf[...]
    h2 = _rows_from_packed(h2_ref, tm)
    hg = jnp.dot(h2, sg_ref[...], preferred_element_type=F32)
    hu = jnp.dot(h2, su_ref[...], preferred_element_type=F32)
    shared = jnp.dot((hg * _sigmoid(hg) * hu).astype(BF16), sd_ref[...], preferred_element_type=F32)
    ff = (routed + shared).reshape(bb, ll, d)
    x2 = x1_ref[...] + mod_ref[:, 5:6, :] * ff
    out_ref[...] = x2 * lax.rsqrt(jnp.mean(x2 * x2, axis=-1, keepdims=True) + NORM_EPS) * fg_ref[...]


def _combine_call(dest, wts, first_tok, ob, h2s, x1, mod, p):
    bn, seq, d = x1.shape
    tm = CMB_TILE
    bb, ll = _tile(bn, seq, tm)
    nl = seq // ll
    tn = bn * seq
    nsteps = tn // tm
    assert first_tok % tm == 0 and dest.shape[2] == tm * TOP_K and wts.shape[1:] == (tm, TOP_K)
    tile = lambda g: (first_tok // tm + g, 0, 0)
    smem = _assign_spec(tm, lambda b, l: tile(b * nl + l))
    smem_next = _assign_spec(tm, lambda b, l: tile(jnp.minimum(b * nl + l + 1, nsteps - 1)))
    wblk = pl.BlockSpec((1, tm, TOP_K), lambda b, l: tile(b * nl + l))
    tok = pl.BlockSpec((bb, ll, d), lambda b, l: (b, l, 0))
    full = lambda a: pl.BlockSpec(a.shape, lambda b, l: (0,) * a.ndim)
    consts = (p["sh_gate"], p["sh_up"], p["sh_down"], p["final_g"])
    return pl.pallas_call(
        functools.partial(_combine_body, tm=tm, nl=nl),
        grid=(bn // bb, nl),
        in_specs=[smem, smem_next, wblk, pl.BlockSpec(memory_space=pl.ANY),
                  pl.BlockSpec((tm * PCH, LANES), lambda b, l: (first_tok // tm + b * nl + l, 0)),
                  tok, pl.BlockSpec((bb, 6, d), lambda b, l: (b, 0, 0))] + [full(c) for c in consts],
        out_specs=tok,
        out_shape=jax.ShapeDtypeStruct((bn, seq, d), F32),
        scratch_shapes=[pltpu.VMEM((2, tm * TOP_K * PCH, LANES), I32), pltpu.VMEM((tm, d), F32),
                        pltpu.SemaphoreType.DMA((2,))],
        compiler_params=_cparams(("arbitrary", "arbitrary"), VMEM_LIMIT),
        name="moe_combine_final",
    )(dest, dest, wts, ob, h2s, x1, mod, *consts)


def _layer_params(l, ada_w, ada_b, norm1_g, norm2_g, w_in, mu_shift, rw_w0, rw_w_up, rw_a0, rw_a_up, rw_g_up,
                  rw_k_k, rw_k_a, rw_r_k, rw_gn_g, rw_gn_b, gla_a_up, gla_a_bias, gla_norm_g, w_pa, w_pb, w_out,
                  router_w, router_b, exp_gate, exp_up, exp_down, sh_gate, sh_up, sh_down):
    d = D_MODEL
    wi = w_in[l]
    gla0 = RW_SHIFT_COLS
    xal0 = gla0 + QKV_W
    pad = jnp.zeros((d, XAL_W - GLA_GATE_RANK), BF16)
    w_pieces = (wi[:, :xal0].astype(BF16),
                jnp.concatenate([wi[:, xal0:xal0 + GLA_GATE_RANK].astype(BF16), pad], axis=1),
                wi[:, xal0 + GLA_GATE_RANK:].astype(BF16))
    zr = jnp.zeros((RW_W_RANK, RW_WIDTH), F32)
    hid = jnp.arange(RW_WIDTH) // RW_HEAD
    row = lambda a: a.reshape(1, -1)
    rw_t = router_w[l].T
    rw_hi = rw_t.astype(BF16)
    return dict(
        ada_w=ada_w[l], ada_b=ada_b[l], norm1_g=norm1_g[l].reshape(1, 1, d),
        norm2_g=norm2_g[l].reshape(1, 1, d), w_in=w_pieces,
        mu=mu_shift[l].reshape(1, 1, -1), w0=row(rw_w0[l]), wup=jnp.concatenate([rw_w_up[l], zr], axis=0),
        a0=row(rw_a0[l]), aup=jnp.concatenate([zr, rw_a_up[l]], axis=0), gup=rw_g_up[l].astype(BF16),
        kk=row(rw_k_k[l]), ka=row(rw_k_a[l]), rk=row(rw_r_k[l]),
        bd64=(hid[:, None] == hid[None, :]).astype(BF16),
        gn_g=row(rw_gn_g[l]), gn_b=row(rw_gn_b[l]),
        gla_aup=jnp.concatenate([gla_a_up[l], jnp.zeros((XAL_W - GLA_GATE_RANK, GLA_KW), F32)], axis=0),
        gla_ab=row(gla_a_bias[l]), gla_ng=row(gla_norm_g[l]),
        w_pa=w_pa[l].astype(BF16), w_pb=w_pb[l].astype(BF16), w_out=w_out[l].astype(BF16),
        rw_hi=rw_hi, rw_lo=(rw_t - rw_hi.astype(F32)).astype(BF16), router_b=router_b[l],
        exp_gate=exp_gate[l], exp_up=exp_up[l], exp_down=exp_down[l],
        sh_gate=sh_gate[l].astype(BF16), sh_up=sh_up[l].astype(BF16), sh_down=sh_down[l].astype(BF16),
    )


def _mixer_group(x, mod, s_rw, s_sh, s_gla, p, tn, first_tok, shared):
    qkv, xal, gg, mg, r, lw, k2, v, a_s, b_s, g, bonus, new_sh = _inproj_prep_call(x, mod, s_sh, p)
    y, rw_new = _rwscan_call(r, lw, k2, v, a_s, b_s, s_rw)
    o_b, gla_new = _gla_call(qkv, xal, gg, s_gla, p)
    x1, h2s, logits = _merge_call(y, g, bonus, o_b, mg, x, mod, p, tn, first_tok, shared)
    states = (rw_new, new_sh[:, 0, :], gla_new)
    return x1, h2s, logits, states


def _moe(h2s, logits, p):
    tn = h2s.shape[0] // PCH
    eidx, rank, wts, counts = _route_call(logits, p["router_b"])
    counts = counts[:, 0].astype(I32)
    padded = (counts + MOE_BLK - 1) // MOE_BLK * MOE_BLK
    pad_end = jnp.cumsum(padded)
    pad_start = (pad_end - padded).astype(I32)
    nb = (tn * TOP_K + N_EXPERTS * (MOE_BLK - 1)) // MOE_BLK + 1
    tables = _expert_tables(counts, pad_start, pad_end, nb)
    dest = _dest_call(eidx, rank, pad_start)
    dest = jnp.swapaxes(dest, 1, 2).reshape(dest.shape[0], 1, -1)
    wts = jnp.swapaxes(wts, 1, 2)
    xs = _dispatch_call(dest, h2s, nb * MOE_BLK)
    ob = _expert_call(tables, xs, p["exp_gate"], p["exp_up"], p["exp_down"])
    return ob, dest, wts


def kernel(x_prompt, x_sample, c_prompt, c_sample, state_rwkv, state_shift, state_gla, ada_w, ada_b, norm1_g,
           norm2_g, w_in, mu_shift, rw_w0, rw_w_up, rw_a0, rw_a_up, rw_g_up, rw_k_k, rw_k_a, rw_r_k, rw_gn_g,
           rw_gn_b, gla_a_up, gla_a_bias, gla_norm_g, w_pa, w_pb, w_out, router_w, router_b, exp_gate, exp_up,
           exp_down, sh_gate, sh_up, sh_down, final_g):
    depth = ada_w.shape[0]
    bp, bs = x_prompt.shape[0], x_sample.shape[0]
    tp = bp * x_prompt.shape[1]
    tn = tp + bs * x_sample.shape[1]
    xs_g = [x_prompt, x_sample]
    c_all = jnp.concatenate([c_prompt, c_sample], axis=0)
    zeros = lambda shape: jnp.zeros(shape, x_prompt.dtype)
    new_states = [[], []]
    fg = final_g.reshape(1, 1, D_MODEL)
    for l in range(depth):
        p = _layer_params(l, ada_w, ada_b, norm1_g, norm2_g, w_in, mu_shift, rw_w0, rw_w_up, rw_a0, rw_a_up,
                          rw_g_up, rw_k_k, rw_k_a, rw_r_k, rw_gn_g, rw_gn_b, gla_a_up, gla_a_bias, gla_norm_g,
                          w_pa, w_pb, w_out, router_w, router_b, exp_gate, exp_up, exp_down, sh_gate, sh_up,
                          sh_down)
        p["final_g"] = fg
        mod_all = _mod_call(c_all, p["ada_w"], p["ada_b"])
        mods = [mod_all[:bp], mod_all[bp:]]
        states_in = [
            (zeros((bp, RW_HEADS, RW_HEAD, RW_HEAD)), zeros((bp, RW_SHIFT_COLS)),
             zeros((bp, GLA_HEADS, GLA_DK, GLA_DV))),
            (state_rwkv[l], state_shift[l], state_gla[l]),
        ]
        x1s, shared = [], None
        firsts = [0, tp]
        for gi in range(2):
            x1, h2_all, lg_all, st = _mixer_group(xs_g[gi], mods[gi], *states_in[gi], p, tn, firsts[gi], shared)
            shared = (h2_all, lg_all)
            x1s.append(x1)
            new_states[gi].append(st)
        ob, dest, wts = _moe(*shared, p)
        assert depth == 1, "the fused final norm assumes a single layer"
        xs_g = [_combine_call(dest, wts, firsts[gi], ob, shared[0], x1s[gi], mods[gi], p) for gi in range(2)]
    stack = lambda gi, j: new_states[gi][0][j][None] if depth == 1 else jnp.stack([s[j] for s in new_states[gi]])
    return (xs_g[0], xs_g[1], stack(0, 0), stack(0, 1), stack(0, 2), stack(1, 0), stack(1, 1), stack(1, 2))
```

```python
import functools

import jax
import jax.numpy as jnp
from jax import lax
from jax.experimental import pallas as pl
from jax.experimental.pallas import tpu as pltpu

F32, BF16, I32 = jnp.float32, jnp.bfloat16, jnp.int32

D_MODEL = 1024
RW_HEADS, RW_HEAD = 8, 64
RW_WIDTH = RW_HEADS * RW_HEAD
RW_W_RANK, RW_A_RANK, RW_G_RANK = 64, 64, 128
RW_GN_EPS = 64e-5
GLA_HEADS, GLA_DK, GLA_DV = 4, 64, 128
GLA_KW, GLA_VW = GLA_HEADS * GLA_DK, GLA_HEADS * GLA_DV
GLA_GATE_RANK = 16
GLA_GATE_TAU = 16.0
GLA_CHUNK = 16
RW_SHIFT_COLS = 3 * RW_WIDTH + RW_W_RANK + RW_A_RANK + RW_G_RANK
N_EXPERTS, TOP_K, N_GROUPS, TOPK_GROUPS = 256, 8, 8, 4
GROUP_SIZE = N_EXPERTS // N_GROUPS
EXPERT_FF = 256
ROUTED_SCALE = 2.5
NORM_EPS = 1e-6

LANES = 128
CHUNKS = D_MODEL // LANES
PCH = CHUNKS // 2
UNIT = 64
RW_SCAN_PASSES = (1, 1, 1, 1, 1)
GLA_UNITS_PER_STEP = 4
RW_UNITS_PER_STEP = 4
VMEM_LIMIT = 56 * 1024 * 1024

PA_W, QKV_W, XAL_W, GG_W, MG_W = RW_SHIFT_COLS, 2 * GLA_KW + GLA_VW, LANES, GLA_VW, 2 * D_MODEL

TOK_TILE = 256
MOE_BLK = 640
MOE_SMALL = 128
WEIGHT_SLOTS = 3
ROW_SLOTS = 3
CMB_TILE = 256
CMB_GROUP = 16

_DN = {
    "nn": (((1,), (0,)), ((), ())),
    "nt": (((1,), (1,)), ((), ())),
    "tn": (((0,), (0,)), ((), ())),
}


def _split(x, pieces):
    out, rem = [], x
    for i in range(pieces):
        p = rem.astype(BF16)
        out.append(p)
        if i + 1 < pieces:
            rem = rem - p.astype(F32)
    return out


def _mm(a, b, form="nn", passes=1):
    dn = _DN[form]
    if passes == 6:
        return lax.dot_general(a.astype(F32), b.astype(F32), dn, precision=lax.Precision.HIGHEST,
                               preferred_element_type=F32)
    if passes == 1:
        return lax.dot_general(a.astype(BF16), b.astype(BF16), dn, preferred_element_type=F32)
    ah, al = _split(a, 2)
    bh, bl = _split(b, 2)
    out = lax.dot_general(ah, bh, dn, preferred_element_type=F32)
    out = out + lax.dot_general(ah, bl, dn, preferred_element_type=F32)
    return out + lax.dot_general(al, bh, dn, preferred_element_type=F32)


def _mm01(m01, x, pieces=3):
    m = m01.astype(BF16)
    out = None
    for p in _split(x, pieces):
        t = lax.dot_general(m, p, _DN["nn"], preferred_element_type=F32)
        out = t if out is None else out + t
    return out


def _xmm01(x, m01, pieces=2):
    m = m01.astype(BF16)
    out = None
    for p in _split(x, pieces):
        t = lax.dot_general(p, m, _DN["nn"], preferred_element_type=F32)
        out = t if out is None else out + t
    return out


HI16 = -65536


def _bf16_bits(x):
    return lax.bitcast_convert_type(x.astype(BF16).astype(F32), I32)


def _unpack_pair(w):
    return lax.bitcast_convert_type(w << 16, F32), lax.bitcast_convert_type(w & HI16, F32)


def _rows_to_packed(ref, x, first=0):
    for c in range(PCH):
        lo = _bf16_bits(x[:, c * LANES:(c + 1) * LANES])
        hi = _bf16_bits(x[:, (c + PCH) * LANES:(c + PCH + 1) * LANES])
        ref[pl.ds(first * PCH + c, x.shape[0], stride=PCH), :] = ((lo >> 16) & 0xFFFF) | (hi & HI16)


def _rows_from_packed(ref, n, live=None, first=0):
    lows, highs = [], []
    for c in range(PCH):
        w = ref[pl.ds(first * PCH + c, n, stride=PCH), :]
        if live is not None:
            w = jnp.where(live, w, 0)
        lo, hi = _unpack_pair(w)
        lows.append(lo.astype(BF16))
        highs.append(hi.astype(BF16))
    return jnp.concatenate(lows + highs, axis=1)


def _slab(ref, row):
    return ref.at[pl.ds(pl.multiple_of(row * PCH, PCH), PCH)]


def _sigmoid(x):
    return 1.0 / (1.0 + jnp.exp(-x))


def _softplus(x):
    return jnp.maximum(x, 0.0) + jnp.log(1.0 + jnp.exp(-jnp.abs(x)))


def _log2(n):
    assert n > 0 and n & (n - 1) == 0, n
    return n.bit_length() - 1


def _cparams(sem, vmem=None):
    return pltpu.CompilerParams(dimension_semantics=sem, vmem_limit_bytes=vmem)


def _mod_body(c_ref, w_ref, b_ref, o_ref):
    c = c_ref[...]
    o_ref[0] = _mm(c * _sigmoid(c), w_ref[...], passes=3) + b_ref[...]


def _mod_call(c_all, ada_w, ada_b):
    bt, d = c_all.shape
    out = pl.pallas_call(
        _mod_body,
        grid=(6,),
        in_specs=[pl.BlockSpec((bt, d), lambda k: (0, 0)),
                  pl.BlockSpec((d, d), lambda k: (0, k)),
                  pl.BlockSpec((1, d), lambda k: (0, k))],
        out_specs=pl.BlockSpec((1, bt, d), lambda k: (k, 0, 0)),
        out_shape=jax.ShapeDtypeStruct((6, bt, d), F32),
        compiler_params=_cparams(("arbitrary",)),
        name="adaln_mod",
    )(c_all, ada_w, ada_b.reshape(1, 6 * d))
    return jnp.transpose(out, (1, 0, 2))


def _inproj_body(x_ref, mod_ref, g_ref, wa_ref, wx_ref, wb_ref, pa_ref, qkv_ref, xal_ref, gg_ref, mg_ref):
    bb, ll, d = x_ref.shape
    x = x_ref[...]
    y = x * lax.rsqrt(jnp.mean(x * x, axis=-1, keepdims=True) + NORM_EPS) * g_ref[...]
    h = y * (1.0 + mod_ref[:, 1:2, :]) + mod_ref[:, 0:1, :]
    hb = h.reshape(bb * ll, d).astype(BF16)
    for w_ref, outs in ((wa_ref, (pa_ref, qkv_ref)), (wx_ref, (xal_ref,)), (wb_ref, (gg_ref, mg_ref))):
        off = 0
        for ref in outs:
            w = ref.shape[-1]
            ref[...] = jnp.dot(hb, w_ref[:, off:off + w], preferred_element_type=F32).reshape(bb, ll, w)
            off += w


def _tile(bn, seq, tile):
    if seq >= tile:
        assert seq % tile == 0
        return 1, tile
    assert tile % seq == 0 and bn % (tile // seq) == 0
    return tile // seq, seq


def _rwprep_body(pa_ref, sh_ref, mu_ref, w0_ref, wup_ref, a0_ref, aup_ref, gup_ref, kk_ref, ka_ref, rk_ref,
                 bd_ref, r_o, lw_o, k_o, v_o, a_o, b_o, g_o, bon_o, nsh_o, carry):
    bb, ll, wd = pa_ref.shape
    n = bb * ll
    hw = RW_WIDTH

    @pl.when(pl.program_id(1) == 0)
    def _():
        carry[...] = sh_ref[...]

    pa = pa_ref[...]
    rolled = pltpu.roll(pa.reshape(n, wd), 1, 0).reshape(bb, ll, wd)
    tok = lax.broadcasted_iota(I32, (bb, ll, wd), 1)
    prev = jnp.where(tok == 0, carry[...], rolled)
    last = pa_ref[:, ll - 1:ll, :]
    carry[...] = last
    nsh_o[...] = last
    xs = (pa + (prev - pa) * mu_ref[...]).reshape(n, wd)

    r, k, v = xs[:, 0:hw], xs[:, hw:2 * hw], xs[:, 2 * hw:3 * hw]
    xwa = xs[:, 3 * hw:3 * hw + LANES]
    xg = xs[:, 3 * hw + LANES:]
    w_log = -_softplus(-(w0_ref[...] + _mm(jnp.tanh(xwa), wup_ref[...], passes=3))) - 0.5
    lw = -jnp.exp(w_log)
    a = _sigmoid(a0_ref[...] + _mm(xwa, aup_ref[...], passes=3))
    g = _mm(_sigmoid(xg), gup_ref[...])
    bd = bd_ref[...]
    kkv = k * kk_ref[...]
    kkn = kkv * lax.rsqrt(jnp.maximum(_xmm01(kkv * kkv, bd, pieces=1), 1e-24))
    k2 = k * (1.0 + (a - 1.0) * ka_ref[...])
    bonus = _xmm01(r * k2 * rk_ref[...], bd, pieces=1) * v
    for ref, val in ((r_o, r), (lw_o, lw), (k_o, k2), (v_o, v), (a_o, -kkn), (b_o, kkn * a), (g_o, g),
                     (bon_o, bonus)):
        ref[...] = val.reshape(bb, ll, hw)


def _inproj_prep_body(x_ref, mod_ref, g_ref, wa_ref, wx_ref, wb_ref, sh_ref, mu_ref, w0_ref, wup_ref, a0_ref, aup_ref,
                      gup_ref, kk_ref, ka_ref, rk_ref, bd_ref, qkv_o, xal_o, gg_o, mg_o, r_o, lw_o, k_o, v_o, a_o,
                      b_o, g_o, bon_o, nsh_o, pa_s, carry):
    _inproj_body(x_ref, mod_ref, g_ref, wa_ref, wx_ref, wb_ref, pa_s, qkv_o, xal_o, gg_o, mg_o)
    _rwprep_body(pa_s, sh_ref, mu_ref, w0_ref, wup_ref, a0_ref, aup_ref, gup_ref, kk_ref, ka_ref, rk_ref, bd_ref,
                 r_o, lw_o, k_o, v_o, a_o, b_o, g_o, bon_o, nsh_o, carry)


def _inproj_prep_call(x, mod, s_sh, p):
    bn, seq, d = x.shape
    bb, ll = _tile(bn, seq, TOK_TILE)
    hw, wd = RW_WIDTH, PA_W
    tok = lambda w: pl.BlockSpec((bb, ll, w), lambda b, l: (b, l, 0))
    row = lambda w: pl.BlockSpec((bb, 1, w), lambda b, l: (b, 0, 0))
    full = lambda a: pl.BlockSpec(a.shape, lambda b, l: (0,) * a.ndim)
    consts = (p["mu"], p["w0"], p["wup"], p["a0"], p["aup"], p["gup"], p["kk"], p["ka"], p["rk"], p["bd64"])
    proj_w = (QKV_W, XAL_W, GG_W, MG_W)
    shapes = lambda ws: [jax.ShapeDtypeStruct((bn, seq, w), F32) for w in ws]
    return pl.pallas_call(
        _inproj_prep_body,
        grid=(bn // bb, seq // ll),
        in_specs=[tok(d), pl.BlockSpec((bb, 6, d), lambda b, l: (b, 0, 0)), full(p["norm1_g"])]
        + [full(w) for w in p["w_in"]] + [row(wd)] + [full(c) for c in consts],
        out_specs=[tok(w) for w in proj_w] + [tok(hw)] * 8 + [row(wd)],
        out_shape=shapes(proj_w) + shapes((hw,) * 8) + [jax.ShapeDtypeStruct((bn, 1, wd), F32)],
        scratch_shapes=[pltpu.VMEM((bb, ll, wd), F32), pltpu.VMEM((bb, 1, wd), F32)],
        compiler_params=_cparams(("arbitrary", "arbitrary"), VMEM_LIMIT),
        name="norm_inproj_prep",
    )(x, mod, p["norm1_g"], *p["w_in"], s_sh.reshape(bn, 1, wd), *consts)


def _unit_masks(n, tl):
    ri = lax.broadcasted_iota(I32, (n, n), 0)
    ci = lax.broadcasted_iota(I32, (n, n), 1)
    same = (ri >> _log2(tl)) == (ci >> _log2(tl))
    return same, same & (ri > ci), same & (ri >= ci)


def _rwscan_body(r_ref, lw_ref, k_ref, v_ref, a_ref, b_ref, s0_ref, y_ref, sn_ref, st, *, nu, nseq, tl, passes):
    n = nseq * tl
    n2 = 2 * n
    p_aa, p_inv, p_apply, p_state, p_y = passes

    hd = RW_HEAD

    @pl.when(pl.program_id(1) == 0)
    def _():
        zero = jnp.zeros((hd, hd), F32)
        for q in range(nu * nseq):
            for p in range(RW_HEADS // 2):
                st[q, p] = jnp.concatenate(
                    [jnp.concatenate([s0_ref[q, 2 * p], zero], axis=1),
                     jnp.concatenate([zero, s0_ref[q, 2 * p + 1]], axis=1)], axis=0)

    same, _, incl = _unit_masks(n, tl)
    m_cum = jnp.where(incl, 1.0, 0.0)
    m_seq = jnp.where(same, 1.0, 0.0)
    ri = lax.broadcasted_iota(I32, (n2, n2), 0)
    ci = lax.broadcasted_iota(I32, (n2, n2), 1)
    rt, ct = ri & (n - 1), ci & (n - 1)
    dsame = ((rt >> _log2(tl)) == (ct >> _log2(tl))) & ((ri >> _log2(n)) == (ci >> _log2(n)))
    strict_d = dsame & (rt > ct)
    incl_d = dsame & (rt >= ct)
    eye_d = jnp.where(ri == ci, 1.0, 0.0)
    lane = lax.broadcasted_iota(I32, (1, LANES), 1)
    m0 = jnp.where(lane < RW_HEAD, 1.0, 0.0)
    m1 = 1.0 - m0

    def dup(x):
        return jnp.concatenate([x * m0, x * m1], axis=0)

    def seq_rows(x, q):
        if nseq == 1:
            return x
        return jnp.concatenate([x[q * tl:(q + 1) * tl], x[n + q * tl:n + (q + 1) * tl]], axis=0)

    def unit_rows(parts):
        if nseq == 1:
            return parts[0]
        return jnp.concatenate([p[0:tl] for p in parts] + [p[tl:2 * tl] for p in parts], axis=0)

    chains = [(u, p) for u in range(nu) for p in range(RW_HEADS // 2)]
    ids = range(len(chains))
    cat0 = lambda *xs: jnp.concatenate(xs, axis=0)

    def ld(ref, c):
        u, p = chains[c]
        return ref[u * nseq:(u + 1) * nseq, :, p * LANES:(p + 1) * LANES].reshape(n, LANES)

    lw = [ld(lw_ref, c) for c in ids]
    cum = [_mm01(m_cum, x) for x in lw]
    tot = [_mm01(m_seq, x) for x in lw]
    e_c = [jnp.exp(x) for x in cum]
    e_n = [jnp.exp(-x) for x in cum]
    e_l = [jnp.exp(t - x) for t, x in zip(tot, cum)]
    at_d = [dup(ld(a_ref, c) * jnp.exp(cum[c] - lw[c])) for c in ids]
    rt_d = [dup(ld(r_ref, c) * e_c[c]) for c in ids]
    bt_d = [dup(ld(b_ref, c) * e_n[c]) for c in ids]
    kt_d = [dup(ld(k_ref, c) * e_n[c]) for c in ids]
    bh_d = [dup(ld(b_ref, c) * e_l[c]) for c in ids]
    kh_d = [dup(ld(k_ref, c) * e_l[c]) for c in ids]
    v_d = [dup(ld(v_ref, c)) for c in ids]
    aa = [_mm(cat0(at_d[c], rt_d[c]), cat0(bt_d[c], kt_d[c]), "nt", p_aa) for c in ids]
    a_ab = [jnp.where(strict_d, x[0:n2, 0:n2], 0.0) for x in aa]
    a_ak = [jnp.where(strict_d, x[0:n2, n2:], 0.0) for x in aa]
    a_rb = [jnp.where(incl_d, x[n2:, 0:n2], 0.0) for x in aa]
    a_rk = [jnp.where(incl_d, x[n2:, n2:], 0.0) for x in aa]
    zy = [_mm(cat0(a_ak[c], a_rk[c]), v_d[c], passes=p_apply) for c in ids]
    tinv = [eye_d + x for x in a_ab]
    nk = a_ab
    for _ in range(_log2(tl) - 1):
        nk = [_mm(x, x, passes=p_inv) for x in nk]
        tinv = [t + _mm(t, x, passes=p_inv) for t, x in zip(tinv, nk)]
    wu = [_mm(tinv[c], jnp.concatenate([at_d[c], zy[c][0:n2]], axis=1), passes=p_apply) for c in ids]
    seqs = range(nseq)
    srow = lambda c, q: (chains[c][0] * nseq + q, chains[c][1])
    s_old = [[st[srow(c, q)] for q in seqs] for c in ids]
    xs = [[_mm(cat0(seq_rows(wu[c][:, 0:LANES], q), seq_rows(rt_d[c], q)), s_old[c][q], "nt", p_state)
           for q in seqs] for c in ids]
    u_q = [[xs[c][q][0:2 * tl] + seq_rows(wu[c][:, LANES:], q) for q in seqs] for c in ids]
    for c in ids:
        for q in seqs:
            g_c = jnp.exp(tot[c][q * tl:q * tl + 1, :])
            st[srow(c, q)] = s_old[c][q] * g_c + _mm(cat0(u_q[c][q], seq_rows(v_d[c], q)),
                                                     cat0(seq_rows(bh_d[c], q), seq_rows(kh_d[c], q)), "tn", p_state)
    for c in ids:
        u, p = chains[c]
        y_d = (unit_rows([xs[c][q][2 * tl:] for q in seqs]) + _mm(a_rb[c], unit_rows(u_q[c]), passes=p_y)
               + zy[c][n2:])
        y_ref[u * nseq:(u + 1) * nseq, :, p * LANES:(p + 1) * LANES] = (y_d[0:n] + y_d[n:]).reshape(nseq, tl, LANES)

    @pl.when(pl.program_id(1) == pl.num_programs(1) - 1)
    def _():
        for q in range(nu * nseq):
            for p in range(RW_HEADS // 2):
                s = st[q, p]
                sn_ref[q, 2 * p] = s[0:hd, 0:hd]
                sn_ref[q, 2 * p + 1] = s[hd:, hd:]


def _unit_shape(bn, seq):
    if seq >= UNIT:
        assert seq % UNIT == 0
        return 1, UNIT
    assert UNIT % seq == 0 and bn % (UNIT // seq) == 0
    return UNIT // seq, seq


def _rwscan_call(r, lw, k2, v, a_s, b_s, s0, passes=RW_SCAN_PASSES):
    bn, seq, hw = r.shape
    nseq, tl = _unit_shape(bn, seq)
    nu = RW_UNITS_PER_STEP if bn % (RW_UNITS_PER_STEP * nseq) == 0 else 1
    rows = nu * nseq
    tok = pl.BlockSpec((rows, tl, hw), lambda b, c: (b, c, 0))
    stt = pl.BlockSpec((rows, RW_HEADS, RW_HEAD, RW_HEAD), lambda b, c: (b, 0, 0, 0))
    return pl.pallas_call(
        functools.partial(_rwscan_body, nu=nu, nseq=nseq, tl=tl, passes=passes),
        grid=(bn // rows, seq // tl),
        in_specs=[tok] * 6 + [stt],
        out_specs=[tok, stt],
        out_shape=[jax.ShapeDtypeStruct((bn, seq, hw), F32), jax.ShapeDtypeStruct(s0.shape, F32)],
        scratch_shapes=[pltpu.VMEM((rows, RW_HEADS // 2, LANES, LANES), F32)],
        compiler_params=_cparams(("arbitrary", "arbitrary"), VMEM_LIMIT),
        name="rwkv_scan",
    )(r, lw, k2, v, a_s, b_s, s0)


def _gla_body(qkv_ref, xal_ref, gate_ref, aup_ref, ab_ref, ng_ref, s0_ref, o_ref, sn_ref, st, *, nu, nseq, tl, cs):
    n = nseq * tl
    n2 = 2 * n
    nsub = tl // cs

    @pl.when(pl.program_id(1) == 0)
    def _():
        zero = jnp.zeros((GLA_DV, GLA_DK), F32)
        for q in range(nu * nseq):
            for p in range(GLA_HEADS // 2):
                st[q, p] = jnp.concatenate(
                    [jnp.concatenate([s0_ref[q, 2 * p].T, zero], axis=1),
                     jnp.concatenate([zero, s0_ref[q, 2 * p + 1].T], axis=1)], axis=0)

    same, _, incl = _unit_masks(n, cs)
    m_cum = jnp.where(incl, 1.0, 0.0)
    m_sub = jnp.where(same, 1.0, 0.0)
    ri = lax.broadcasted_iota(I32, (n2, n2), 0)
    ci = lax.broadcasted_iota(I32, (n2, n2), 1)
    rt, ct = ri & (n - 1), ci & (n - 1)
    causal_d = ((rt >> _log2(cs)) == (ct >> _log2(cs))) & ((ri >> _log2(n)) == (ci >> _log2(n))) & (rt >= ct)
    lane = lax.broadcasted_iota(I32, (1, LANES), 1)
    m0 = jnp.where(lane < GLA_DK, 1.0, 0.0)
    m1 = 1.0 - m0
    sr = lax.broadcasted_iota(I32, (2 * GLA_DV, LANES), 0)
    sc = lax.broadcasted_iota(I32, (2 * GLA_DV, LANES), 1)
    st_mask = jnp.where((sr >> _log2(GLA_DV)) == (sc >> _log2(GLA_DK)), 1.0, 0.0)

    def dup(x):
        return jnp.concatenate([x * m0, x * m1], axis=0)

    chains = [(u, p) for u in range(nu) for p in range(GLA_HEADS // 2)]
    ids = range(len(chains))
    urows = lambda u: slice(u * nseq, (u + 1) * nseq)
    ng = ng_ref[...]
    la_all = [-_softplus(-(_mm(xal_ref[urows(u), :, :].reshape(n, LANES), aup_ref[...], passes=3) + ab_ref[...]))
              * (1.0 / GLA_GATE_TAU) for u in range(nu)]

    def ld(ref, c, off, width):
        return ref[urows(chains[c][0]), :, off:off + width].reshape(n, width)

    q = [ld(qkv_ref, c, chains[c][1] * LANES, LANES) * (GLA_DK ** -0.5) for c in ids]
    k = [ld(qkv_ref, c, GLA_KW + chains[c][1] * LANES, LANES) for c in ids]
    vp = [ld(qkv_ref, c, 2 * GLA_KW + chains[c][1] * 2 * GLA_DV, 2 * GLA_DV) for c in ids]
    la = [la_all[u][:, p * LANES:(p + 1) * LANES] for u, p in chains]
    bc = [_mm01(m_cum, x) for x in la]
    bl = [_mm01(m_sub, x) for x in la]
    qe = [q[c] * jnp.exp(bc[c]) for c in ids]
    ke = [k[c] * jnp.exp(-bc[c]) for c in ids]
    kd = [k[c] * jnp.exp(bl[c] - bc[c]) for c in ids]
    att = [jnp.where(causal_d, _mm(dup(qe[c]), dup(ke[c]), "nt", passes=1), 0.0) for c in ids]
    v_st = [jnp.concatenate([x[:, 0:GLA_DV], x[:, GLA_DV:]], axis=0) for x in vp]
    o_st = [_mm(att[c], v_st[c], passes=1) for c in ids]
    upd = [[_mm(vp[c][r0:r0 + cs], kd[c][r0:r0 + cs], "tn", passes=1) for r0 in range(0, n, cs)] for c in ids]
    inter = [[None] * (n // cs) for _ in ids]
    for sq in range(nseq):
        s = [st[chains[c][0] * nseq + sq, chains[c][1]] for c in ids]
        for j in range(nsub):
            i = sq * nsub + j
            r0 = i * cs
            for c in ids:
                inter[c][i] = _mm(qe[c][r0:r0 + cs], s[c], "nt", passes=1)
                s[c] = s[c] * jnp.exp(bl[c][r0:r0 + 1, :]) + st_mask * upd[c][i]
        for c in ids:
            st[chains[c][0] * nseq + sq, chains[c][1]] = s[c]
    for c in ids:
        u, p = chains[c]
        o = o_st[c] + jnp.concatenate([x[:, 0:GLA_DV] for x in inter[c]] + [x[:, GLA_DV:] for x in inter[c]], axis=0)
        o = o * lax.rsqrt(jnp.mean(o * o, axis=-1, keepdims=True) + NORM_EPS) * ng
        goff = p * 2 * GLA_DV
        gp = ld(gate_ref, c, goff, 2 * GLA_DV)
        g_st = jnp.concatenate([gp[:, 0:GLA_DV], gp[:, GLA_DV:]], axis=0)
        ob = o * (g_st * _sigmoid(g_st))
        o_ref[urows(u), :, goff:goff + GLA_DV] = ob[0:n].reshape(nseq, tl, GLA_DV)
        o_ref[urows(u), :, goff + GLA_DV:goff + 2 * GLA_DV] = ob[n:].reshape(nseq, tl, GLA_DV)

    @pl.when(pl.program_id(1) == pl.num_programs(1) - 1)
    def _():
        for q in range(nu * nseq):
            for p in range(GLA_HEADS // 2):
                s = st[q, p]
                sn_ref[q, 2 * p] = s[0:GLA_DV, 0:GLA_DK].T
                sn_ref[q, 2 * p + 1] = s[GLA_DV:, GLA_DK:].T


def _gla_call(qkv, xal, gate, s0, p):
    bn, seq, _ = qkv.shape
    nseq, tl = _unit_shape(bn, seq)
    cs = min(GLA_CHUNK, seq)
    assert tl % cs == 0
    nu = GLA_UNITS_PER_STEP if bn % (GLA_UNITS_PER_STEP * nseq) == 0 else 1
    rows = nu * nseq
    tok = lambda w: pl.BlockSpec((rows, tl, w), lambda b, c: (b, c, 0))
    full = lambda a: pl.BlockSpec(a.shape, lambda b, c: (0,) * a.ndim)
    stt = pl.BlockSpec((rows, GLA_HEADS, GLA_DK, GLA_DV), lambda b, c: (b, 0, 0, 0))
    consts = (p["gla_aup"], p["gla_ab"], p["gla_ng"])
    return pl.pallas_call(
        functools.partial(_gla_body, nu=nu, nseq=nseq, tl=tl, cs=cs),
        grid=(bn // rows, seq // tl),
        in_specs=[tok(QKV_W), tok(XAL_W), tok(GG_W)] + [full(c) for c in consts] + [stt],
        out_specs=[tok(GLA_VW), stt],
        out_shape=[jax.ShapeDtypeStruct((bn, seq, GLA_VW), F32), jax.ShapeDtypeStruct(s0.shape, F32)],
        scratch_shapes=[pltpu.VMEM((rows, GLA_HEADS // 2, 2 * GLA_DV, LANES), F32)],
        compiler_params=_cparams(("arbitrary", "arbitrary"), VMEM_LIMIT),
        name="gla_chunked",
    )(qkv, xal, gate, *consts, s0)


def _merge_body(y_ref, g_ref, bon_ref, ob_ref, mg_ref, x_ref, mod_ref, gng_ref, gnb_ref, bd_ref, wpa_ref,
                wpb_ref, wout_ref, n2_ref, rwh_ref, rwl_ref, *rest):
    x1_o, h2_o, lg_o = rest[-3:]
    bb, ll, d = x_ref.shape
    n = bb * ll
    hw = RW_WIDTH
    bd = bd_ref[...]
    y = y_ref[...].reshape(n, hw)
    mu = _xmm01(y, bd, pieces=2) * (1.0 / RW_HEAD)
    dv = y - mu
    var = _xmm01(dv * dv, bd, pieces=1) * (1.0 / RW_HEAD)
    yn = dv * lax.rsqrt(var + RW_GN_EPS) * gng_ref[...] + gnb_ref[...]
    o_a = (yn + bon_ref[...].reshape(n, hw)) * g_ref[...].reshape(n, hw)
    o_b = ob_ref[...].reshape(n, GLA_VW)
    mg = mg_ref[...].reshape(n, 2 * d)
    merged = _sigmoid(mg[:, 0:d]) * _mm(o_a, wpa_ref[...]) + _sigmoid(mg[:, d:]) * _mm(o_b, wpb_ref[...])
    mix = _mm(merged, wout_ref[...]).reshape(bb, ll, d)
    x1 = x_ref[...] + mod_ref[:, 2:3, :] * mix
    x1_o[...] = x1
    yn2 = x1 * lax.rsqrt(jnp.mean(x1 * x1, axis=-1, keepdims=True) + NORM_EPS) * n2_ref[...]
    h2 = (yn2 * (1.0 + mod_ref[:, 4:5, :]) + mod_ref[:, 3:4, :]).reshape(n, d)
    hh, hl = _split(h2, 2)
    rwh, rwl = rwh_ref[...], rwl_ref[...]
    nt = lambda a, b: lax.dot_general(a, b, _DN["nt"], preferred_element_type=F32)
    lg_o[...] = nt(rwh, hh) + nt(rwl, hh) + nt(rwh, hl)
    _rows_to_packed(h2_o, h2)


def _merge_call(y, g, bonus, o_b, mg, x, mod, p, tn, first_tok, shared):
    bn, seq, d = x.shape
    bb, ll = _tile(bn, seq, TOK_TILE)
    nl = seq // ll
    assert first_tok % (bb * ll) == 0
    t0 = first_tok // (bb * ll)
    n_in = 7 + 9
    extra = [] if shared is None else list(shared)
    alias = {} if shared is None else {n_in: 1, n_in + 1: 2}
    tok = lambda w: pl.BlockSpec((bb, ll, w), lambda b, l: (b, l, 0))
    full = lambda a: pl.BlockSpec(a.shape, lambda b, l: (0,) * a.ndim)
    consts = (p["gn_g"], p["gn_b"], p["bd64"], p["w_pa"], p["w_pb"], p["w_out"], p["norm2_g"], p["rw_hi"],
              p["rw_lo"])
    return pl.pallas_call(
        _merge_body,
        grid=(bn // bb, nl),
        in_specs=[tok(RW_WIDTH)] * 3 + [tok(GLA_VW), tok(MG_W), tok(d),
                                        pl.BlockSpec((bb, 6, d), lambda b, l: (b, 0, 0))] + [full(c) for c in consts]
        + [pl.BlockSpec(memory_space=pl.ANY)] * len(extra),
        out_specs=[tok(d),
                   pl.BlockSpec((bb * ll * PCH, LANES), lambda b, l: (t0 + b * nl + l, 0)),
                   pl.BlockSpec((N_EXPERTS, bb * ll), lambda b, l: (0, t0 + b * nl + l))],
        out_shape=[jax.ShapeDtypeStruct((bn, seq, d), F32),
                   jax.ShapeDtypeStruct((tn * PCH, LANES), I32),
                   jax.ShapeDtypeStruct((N_EXPERTS, tn), F32)],
        input_output_aliases=alias,
        compiler_params=_cparams(("arbitrary", "arbitrary"), VMEM_LIMIT),
        name="merge_outproj_router",
    )(y, g, bonus, o_b, mg, x, mod, *consts, *extra)


def _route_body(lg_ref, rb_ref, e_o, rk_o, w_o, cnt_o, carry):
    ne, tm = lg_ref.shape

    @pl.when(pl.program_id(0) == 0)
    def _():
        carry[...] = jnp.zeros_like(carry)

    neg = -jnp.inf
    scores = _sigmoid(lg_ref[...])
    sel = scores + rb_ref[...]
    row_i = lax.broadcasted_iota(I32, (ne, tm), 0)
    row = row_i.astype(F32)
    grp = (row_i >> _log2(GROUP_SIZE)).astype(F32)

    def first_max(x, ids, none):
        m = jnp.max(x, axis=0, keepdims=True)
        return m, jnp.min(jnp.where(x == m, ids, none), axis=0, keepdims=True)

    gs = []
    gids = lax.broadcasted_iota(I32, (GROUP_SIZE, tm), 0)
    for gidx in range(N_GROUPS):
        rows = slice(gidx * GROUP_SIZE, (gidx + 1) * GROUP_SIZE)
        sg = _sigmoid(lg_ref[rows, :]) + rb_ref[rows, :]
        ids = (gids + gidx * GROUP_SIZE).astype(F32)
        m1, i1 = first_max(sg, ids, float(ne))
        gs.append(m1 + jnp.max(jnp.where(ids == i1, neg, sg), axis=0, keepdims=True))
    gs = jnp.concatenate(gs, axis=0)
    gid = lax.broadcasted_iota(I32, (N_GROUPS, tm), 0).astype(F32)
    cur = jnp.full((ne, tm), neg, F32)
    for _ in range(TOPK_GROUPS):
        _, gi = first_max(gs, gid, float(N_GROUPS))
        cur = jnp.where(grp == gi, sel, cur)
        gs = jnp.where(gid == gi, neg, gs)

    pm = jnp.zeros((ne, tm), F32)
    eidx, wts = [], []
    for _ in range(TOP_K):
        _, ei = first_max(cur, row, float(ne))
        hit = row == ei
        pm = jnp.where(hit, 1.0, pm)
        eidx.append(ei)
        wts.append(jnp.sum(jnp.where(hit, scores, 0.0), axis=0, keepdims=True))
        cur = jnp.where(hit, neg, cur)
    wsum = wts[0]
    for w in wts[1:]:
        wsum = wsum + w

    ri = lax.broadcasted_iota(I32, (tm, tm), 0)
    ci = lax.broadcasted_iota(I32, (tm, tm), 1)
    earlier = jnp.where(ri < ci, 1.0, 0.0)
    rank = _mm(pm, earlier, passes=1) + carry[...]
    carry[...] = carry[...] + jnp.sum(pm, axis=1, keepdims=True)
    cnt_o[...] = carry[...]

    rks = [jnp.sum(jnp.where(row == e, rank, 0.0), axis=0, keepdims=True) for e in eidx]
    e_o[0] = jnp.concatenate(eidx, axis=0).astype(I32)
    rk_o[0] = jnp.concatenate(rks, axis=0).astype(I32)
    w_o[0] = jnp.concatenate([w / wsum * ROUTED_SCALE for w in wts], axis=0)


def _route_call(logits_t, router_b):
    ne, tn = logits_t.shape
    tm = TOK_TILE
    assert tn % tm == 0
    col = pl.BlockSpec((ne, 1), lambda i: (0, 0))
    tab = pl.BlockSpec((1, TOP_K, tm), lambda i: (i, 0, 0))
    tab_shape = (tn // tm, TOP_K, tm)
    return pl.pallas_call(
        _route_body,
        grid=(tn // tm,),
        in_specs=[pl.BlockSpec((ne, tm), lambda i: (0, i)), col],
        out_specs=[tab, tab, tab, col],
        out_shape=[jax.ShapeDtypeStruct(tab_shape, I32), jax.ShapeDtypeStruct(tab_shape, I32),
                   jax.ShapeDtypeStruct(tab_shape, F32), jax.ShapeDtypeStruct((ne, 1), F32)],
        scratch_shapes=[pltpu.VMEM((ne, 1), F32)],
        compiler_params=_cparams(("arbitrary",)),
        name="moe_route",
    )(logits_t, router_b.reshape(ne, 1))


def _dest_body(e_ref, rk_ref, ps_ref, d_o):
    ne, tm = ps_ref.shape[0], e_ref.shape[2]
    ids = lax.broadcasted_iota(I32, (ne, tm), 0)
    ps = ps_ref[...]
    for t in range(e_ref.shape[0]):
        first = [jnp.sum(jnp.where(ids == e_ref[t, kk:kk + 1, :], ps, 0.0), axis=0, keepdims=True)
                 for kk in range(TOP_K)]
        d_o[t] = (jnp.concatenate(first, axis=0).astype(I32) + rk_ref[t]) * PCH


def _dest_call(eidx, rank, pad_start):
    nt, _, tm = eidx.shape
    ne = pad_start.shape[0]
    per = next(k for k in (4, 2, 1) if nt % k == 0)
    tab = pl.BlockSpec((per, TOP_K, tm), lambda i: (i, 0, 0))
    return pl.pallas_call(
        _dest_body,
        grid=(nt // per,),
        in_specs=[tab, tab, pl.BlockSpec((ne, 1), lambda i: (0, 0))],
        out_specs=tab,
        out_shape=jax.ShapeDtypeStruct(eidx.shape, I32),
        compiler_params=_cparams(("arbitrary",)),
        name="moe_dest",
    )(eidx, rank, pad_start.astype(F32).reshape(ne, 1))


def _pslab(ref, offset):
    return ref.at[pl.ds(pl.multiple_of(offset, PCH), PCH)]


def _dispatch_body(d_ref, h2_ref, xs_hbm, sem, *, tm):
    def issue(m, carry):
        for kk in range(TOP_K):
            pltpu.make_async_copy(_slab(h2_ref, m), _pslab(xs_hbm, d_ref[0, 0, m * TOP_K + kk]), sem).start(priority=kk % 2)
        return carry

    lax.fori_loop(0, tm, issue, 0)
    all_rows = xs_hbm.at[pl.ds(0, tm * TOP_K * PCH)]
    pltpu.make_async_copy(all_rows, all_rows, sem).wait()


def _assign_spec(tm, index_map):
    return pl.BlockSpec((1, 1, tm * TOP_K), index_map, memory_space=pltpu.SMEM)


def _dispatch_call(dest, h2s, n_rows):
    tn = h2s.shape[0] // PCH
    tm = TOK_TILE
    assert dest.shape == (tn // tm, 1, tm * TOP_K)
    blk = _assign_spec(tm, lambda i: (i, 0, 0))
    return pl.pallas_call(
        functools.partial(_dispatch_body, tm=tm),
        grid=(tn // tm,),
        in_specs=[blk, pl.BlockSpec((tm * PCH, LANES), lambda i: (i, 0))],
        out_specs=pl.BlockSpec(memory_space=pl.ANY),
        out_shape=jax.ShapeDtypeStruct((n_rows * PCH, LANES), I32),
        scratch_shapes=[pltpu.SemaphoreType.DMA],
        compiler_params=_cparams(("arbitrary",)),
        name="moe_dispatch",
    )(dest, h2s)


def _expert_body(bi_ref, nr_ref, ld_ref, nx_ref, xs_hbm, wg_hbm, wu_hbm, wd_hbm, ob_hbm, wg_buf, wu_buf, wd_buf,
                 wg_bf, wu_bf, wd_bf, xbuf, obuf, sem, xsem, osem):
    i = pl.program_id(0)
    nsteps = pl.num_programs(0)
    nr = nr_ref[i]
    slot = ld_ref[i]
    blk_rows = MOE_BLK * PCH
    small_rows = MOE_SMALL * PCH

    def row_block(j, go):
        s = lax.rem(j, ROW_SLOTS)
        base = pl.multiple_of(bi_ref[j] * blk_rows, blk_rows)
        head = pltpu.make_async_copy(xs_hbm.at[pl.ds(base, small_rows)], xbuf.at[s].at[pl.ds(0, small_rows)],
                                     xsem.at[s])
        rest = pltpu.make_async_copy(xs_hbm.at[pl.ds(base + small_rows, blk_rows - small_rows)],
                                     xbuf.at[s].at[pl.ds(small_rows, blk_rows - small_rows)], xsem.at[s])
        go(head)
        pl.when(nr_ref[j] > MOE_SMALL)(lambda: go(rest))

    start = lambda cp: cp.start()
    wait = lambda cp: cp.wait()

    @pl.when(i == 0)
    def _():
        for j in range(ROW_SLOTS - 1):
            row_block(j, start)

    @pl.when(i + ROW_SLOTS - 1 < nsteps)
    def _():
        row_block(i + ROW_SLOTS - 1, start)

    def fetch(e, s):
        return (pltpu.make_async_copy(wg_hbm.at[e], wg_buf.at[s], sem.at[s]),
                pltpu.make_async_copy(wu_hbm.at[e], wu_buf.at[s], sem.at[s]),
                pltpu.make_async_copy(wd_hbm.at[e], wd_buf.at[s], sem.at[s]))

    @pl.when(i == 0)
    def _():
        for s in range(WEIGHT_SLOTS - 1):
            e0 = nx_ref[nx_ref.shape[0] - (WEIGHT_SLOTS - 1) + s]

            @pl.when(e0 >= 0)
            def _():
                for k, cp in enumerate(fetch(e0, s)):
                    cp.start(priority=k % 2)

    @pl.when(slot >= 0)
    def _():
        for cp in fetch(0, slot):
            cp.wait()

        @pl.when(nx_ref[i] >= 0)
        def _():
            for k, cp in enumerate(fetch(nx_ref[i], lax.rem(slot + WEIGHT_SLOTS - 1, WEIGHT_SLOTS))):
                cp.start(priority=k % 2)

        wg_bf[...] = wg_buf[slot].astype(BF16)
        wu_bf[...] = wu_buf[slot].astype(BF16)
        wd_bf[...] = wd_buf[slot].astype(BF16)

    def out_block(j, go):
        s = lax.rem(j, 2)
        base = pl.multiple_of(bi_ref[j] * blk_rows, blk_rows)
        head = pltpu.make_async_copy(obuf.at[s].at[pl.ds(0, small_rows)], ob_hbm.at[pl.ds(base, small_rows)],
                                     osem.at[s])
        rest = pltpu.make_async_copy(obuf.at[s].at[pl.ds(small_rows, blk_rows - small_rows)],
                                     ob_hbm.at[pl.ds(base + small_rows, blk_rows - small_rows)], osem.at[s])
        pl.when(nr_ref[j] > 0)(lambda: go(head))
        pl.when(nr_ref[j] > MOE_SMALL)(lambda: go(rest))

    row_block(i, wait)
    pl.when(i >= 2)(lambda: out_block(i - 2, wait))
    ob_ref = obuf.at[lax.rem(i, 2)]

    def work(n):
        rid = lax.broadcasted_iota(I32, (n, LANES), 0)
        x = _rows_from_packed(xbuf.at[lax.rem(i, ROW_SLOTS)], n, rid < nr)
        hg = jnp.dot(x, wg_bf[...], preferred_element_type=F32)
        hu = jnp.dot(x, wu_bf[...], preferred_element_type=F32)
        hh = (hg * _sigmoid(hg) * hu).astype(BF16)
        _rows_to_packed(ob_ref, jnp.dot(hh, wd_bf[...], preferred_element_type=F32), 0)

    pl.when(nr > MOE_SMALL)(lambda: work(MOE_BLK))
    pl.when((nr > 0) & (nr <= MOE_SMALL))(lambda: work(MOE_SMALL))
    out_block(i, start)

    @pl.when(i == nsteps - 1)
    def _():
        out_block(i - 1, wait)
        out_block(i, wait)


def _expert_tables(counts, pad_start, pad_end, nb):
    ne = counts.shape[0]
    experts = jnp.arange(ne, dtype=I32)
    first_row = jnp.arange(nb, dtype=I32) * MOE_BLK
    block_e = jnp.minimum(jnp.sum(pad_end[None, :] <= first_row[:, None], axis=1), ne - 1).astype(I32)
    mine = block_e[:, None] == experts[None, :]
    pick = lambda v: jnp.sum(jnp.where(mine, v[None, :], 0), axis=1)
    has = counts > 0
    ordinal = jnp.cumsum(has.astype(I32)) - 1
    start_b, count_b, ord_b = pick(pad_start), pick(counts), pick(ordinal)
    block_rows = jnp.clip(start_b + count_b - first_row, 0, MOE_BLK).astype(I32)
    block_i = jnp.minimum(jnp.arange(nb, dtype=I32), pad_end[-1] // MOE_BLK - 1).astype(I32)
    starts = (first_row == start_b) & (block_rows > 0)
    load_slot = jnp.where(starts, ord_b % WEIGHT_SLOTS, -1).astype(I32)
    nth = lambda want: jnp.max(jnp.where(has[None, :] & (ordinal[None, :] == want[:, None]), experts[None, :], -1),
                               axis=1)
    ahead = jnp.where(starts, nth(ord_b + WEIGHT_SLOTS - 1), -1)
    lead = nth(jnp.arange(WEIGHT_SLOTS - 1, dtype=I32))
    return block_i, block_rows, load_slot, jnp.concatenate([ahead, lead]).astype(I32)


def _expert_call(tables, xs, wg, wu, wd):
    nb = xs.shape[0] // (MOE_BLK * PCH)
    assert nb >= ROW_SLOTS
    d, ff = wg.shape[1], wg.shape[2]
    hbm = pl.BlockSpec(memory_space=pl.ANY)
    grid_spec = pltpu.PrefetchScalarGridSpec(
        num_scalar_prefetch=4,
        grid=(nb,),
        in_specs=[hbm, hbm, hbm, hbm],
        out_specs=hbm,
        scratch_shapes=[pltpu.VMEM((WEIGHT_SLOTS, d, ff), F32), pltpu.VMEM((WEIGHT_SLOTS, d, ff), F32),
                        pltpu.VMEM((WEIGHT_SLOTS, ff, d), F32),
                        pltpu.VMEM((d, ff), BF16), pltpu.VMEM((d, ff), BF16), pltpu.VMEM((ff, d), BF16),
                        pltpu.VMEM((ROW_SLOTS, MOE_BLK * PCH, LANES), I32), pltpu.VMEM((2, MOE_BLK * PCH, LANES), I32),
                        pltpu.SemaphoreType.DMA((WEIGHT_SLOTS,)), pltpu.SemaphoreType.DMA((ROW_SLOTS,)),
                        pltpu.SemaphoreType.DMA((2,))],
    )
    return pl.pallas_call(
        _expert_body,
        grid_spec=grid_spec,
        out_shape=jax.ShapeDtypeStruct(xs.shape, I32),
        compiler_params=_cparams(("arbitrary",), VMEM_LIMIT),
        name="moe_experts",
    )(*tables, xs, wg, wu, wd)


def _combine_body(d_ref, dn_ref, wt_ref, ob_hbm, h2_ref, x1_ref, mod_ref, sg_ref, su_ref,
                  sd_ref, fg_ref, out_ref, gbuf, rbuf, sem, *, tm, nl):
    bb, ll, d = x1_ref.shape
    step = pl.program_id(0) * nl + pl.program_id(1)
    last = pl.num_programs(0) * nl - 1
    parity = lax.rem(step, 2)
    grp = CMB_GROUP

    def request(d_tab, g, s):
        for j in range(grp):
            m = g * grp + j
            for kk in range(TOP_K):
                pltpu.make_async_copy(_pslab(ob_hbm, d_tab[0, 0, m * TOP_K + kk]), _slab(gbuf.at[s], kk * tm + m),
                                      sem.at[s]).start(priority=kk % 2)

    def mix(g, s):
        r0 = pl.multiple_of(g * grp, grp)
        w = wt_ref[0, pl.ds(r0, grp), :]
        wk = [w[:, kk:kk + 1] for kk in range(TOP_K)]
        for c in range(PCH):
            acc_lo = acc_hi = None
            for kk in range(TOP_K):
                words = gbuf[s, pl.ds((kk * tm + r0) * PCH + c, grp, stride=PCH), :]
                lo, hi = _unpack_pair(words)
                acc_lo = wk[kk] * lo if acc_lo is None else acc_lo + wk[kk] * lo
                acc_hi = wk[kk] * hi if acc_hi is None else acc_hi + wk[kk] * hi
            rbuf[pl.ds(r0, grp), c * LANES:(c + 1) * LANES] = acc_lo
            rbuf[pl.ds(r0, grp), (c + PCH) * LANES:(c + PCH + 1) * LANES] = acc_hi

    @pl.when(step == 0)
    def _():
        def first(g, carry):
            request(d_ref, g, 0)
            return carry
        lax.fori_loop(0, tm // grp, first, 0)

    def run(slot):
        pltpu.make_async_copy(ob_hbm.at[pl.ds(0, tm * TOP_K * PCH)], gbuf.at[slot], sem.at[slot]).wait()

        @pl.when(step < last)
        def _():
            def both(g, carry):
                request(dn_ref, g, 1 - slot)
                mix(g, slot)
                return carry
            lax.fori_loop(0, tm // grp, both, 0)

        @pl.when(step == last)
        def _():
            def only(g, carry):
                mix(g, slot)
                return carry
            lax.fori_loop(0, tm // grp, only, 0)

    for slot in range(2):
        pl.when(parity == slot)(functools.partial(run, slot))

    routed = rbuf[...]
    h2 = _rows_from_packed(h2_ref, tm)
    hg = jnp.dot(h2, sg_ref[...], preferred_element_type=F32)
    hu = jnp.dot(h2, su_ref[...], preferred_element_type=F32)
    shared = jnp.dot((hg * _sigmoid(hg) * hu).astype(BF16), sd_ref[...], preferred_element_type=F32)
    ff = (routed + shared).reshape(bb, ll, d)
    x2 = x1_ref[...] + mod_ref[:, 5:6, :] * ff
    out_ref[...] = x2 * lax.rsqrt(jnp.mean(x2 * x2, axis=-1, keepdims=True) + NORM_EPS) * fg_ref[...]


def _combine_call(dest, wts, first_tok, ob, h2s, x1, mod, p):
    bn, seq, d = x1.shape
    tm = CMB_TILE
    bb, ll = _tile(bn, seq, tm)
    nl = seq // ll
    tn = bn * seq
    nsteps = tn // tm
    assert first_tok % tm == 0 and dest.shape[2] == tm * TOP_K and wts.shape[1:] == (tm, TOP_K)
    tile = lambda g: (first_tok // tm + g, 0, 0)
    smem = _assign_spec(tm, lambda b, l: tile(b * nl + l))
    smem_next = _assign_spec(tm, lambda b, l: tile(jnp.minimum(b * nl + l + 1, nsteps - 1)))
    wblk = pl.BlockSpec((1, tm, TOP_K), lambda b, l: tile(b * nl + l))
    tok = pl.BlockSpec((bb, ll, d), lambda b, l: (b, l, 0))
    full = lambda a: pl.BlockSpec(a.shape, lambda b, l: (0,) * a.ndim)
    consts = (p["sh_gate"], p["sh_up"], p["sh_down"], p["final_g"])
    return pl.pallas_call(
        functools.partial(_combine_body, tm=tm, nl=nl),
        grid=(bn // bb, nl),
        in_specs=[smem, smem_next, wblk, pl.BlockSpec(memory_space=pl.ANY),
                  pl.BlockSpec((tm * PCH, LANES), lambda b, l: (first_tok // tm + b * nl + l, 0)),
                  tok, pl.BlockSpec((bb, 6, d), lambda b, l: (b, 0, 0))] + [full(c) for c in consts],
        out_specs=tok,
        out_shape=jax.ShapeDtypeStruct((bn, seq, d), F32),
        scratch_shapes=[pltpu.VMEM((2, tm * TOP_K * PCH, LANES), I32), pltpu.VMEM((tm, d), F32),
                        pltpu.SemaphoreType.DMA((2,))],
        compiler_params=_cparams(("arbitrary", "arbitrary"), VMEM_LIMIT),
        name="moe_combine_final",
    )(dest, dest, wts, ob, h2s, x1, mod, *consts)


def _layer_params(l, ada_w, ada_b, norm1_g, norm2_g, w_in, mu_shift, rw_w0, rw_w_up, rw_a0, rw_a_up, rw_g_up,
                  rw_k_k, rw_k_a, rw_r_k, rw_gn_g, rw_gn_b, gla_a_up, gla_a_bias, gla_norm_g, w_pa, w_pb, w_out,
                  router_w, router_b, exp_gate, exp_up, exp_down, sh_gate, sh_up, sh_down):
    d = D_MODEL
    wi = w_in[l]
    gla0 = RW_SHIFT_COLS
    xal0 = gla0 + QKV_W
    pad = jnp.zeros((d, XAL_W - GLA_GATE_RANK), BF16)
    w_pieces = (wi[:, :xal0].astype(BF16),
                jnp.concatenate([wi[:, xal0:xal0 + GLA_GATE_RANK].astype(BF16), pad], axis=1),
                wi[:, xal0 + GLA_GATE_RANK:].astype(BF16))
    zr = jnp.zeros((RW_W_RANK, RW_WIDTH), F32)
    hid = jnp.arange(RW_WIDTH) // RW_HEAD
    row = lambda a: a.reshape(1, -1)
    rw_t = router_w[l].T
    rw_hi = rw_t.astype(BF16)
    return dict(
        ada_w=ada_w[l], ada_b=ada_b[l], norm1_g=norm1_g[l].reshape(1, 1, d),
        norm2_g=norm2_g[l].reshape(1, 1, d), w_in=w_pieces,
        mu=mu_shift[l].reshape(1, 1, -1), w0=row(rw_w0[l]), wup=jnp.concatenate([rw_w_up[l], zr], axis=0),
        a0=row(rw_a0[l]), aup=jnp.concatenate([zr, rw_a_up[l]], axis=0), gup=rw_g_up[l].astype(BF16),
        kk=row(rw_k_k[l]), ka=row(rw_k_a[l]), rk=row(rw_r_k[l]),
        bd64=(hid[:, None] == hid[None, :]).astype(BF16),
        gn_g=row(rw_gn_g[l]), gn_b=row(rw_gn_b[l]),
        gla_aup=jnp.concatenate([gla_a_up[l], jnp.zeros((XAL_W - GLA_GATE_RANK, GLA_KW), F32)], axis=0),
        gla_ab=row(gla_a_bias[l]), gla_ng=row(gla_norm_g[l]),
        w_pa=w_pa[l].astype(BF16), w_pb=w_pb[l].astype(BF16), w_out=w_out[l].astype(BF16),
        rw_hi=rw_hi, rw_lo=(rw_t - rw_hi.astype(F32)).astype(BF16), router_b=router_b[l],
        exp_gate=exp_gate[l], exp_up=exp_up[l], exp_down=exp_down[l],
        sh_gate=sh_gate[l].astype(BF16), sh_up=sh_up[l].astype(BF16), sh_down=sh_down[l].astype(BF16),
    )


def _mixer_group(x, mod, s_rw, s_sh, s_gla, p, tn, first_tok, shared):
    qkv, xal, gg, mg, r, lw, k2, v, a_s, b_s, g, bonus, new_sh = _inproj_prep_call(x, mod, s_sh, p)
    y, rw_new = _rwscan_call(r, lw, k2, v, a_s, b_s, s_rw)
    o_b, gla_new = _gla_call(qkv, xal, gg, s_gla, p)
    x1, h2s, logits = _merge_call(y, g, bonus, o_b, mg, x, mod, p, tn, first_tok, shared)
    states = (rw_new, new_sh[:, 0, :], gla_new)
    return x1, h2s, logits, states


def _moe(h2s, logits, p):
    tn = h2s.shape[0] // PCH
    eidx, rank, wts, counts = _route_call(logits, p["router_b"])
    counts = counts[:, 0].astype(I32)
    padded = (counts + MOE_BLK - 1) // MOE_BLK * MOE_BLK
    pad_end = jnp.cumsum(padded)
    pad_start = (pad_end - padded).astype(I32)
    nb = (tn * TOP_K + N_EXPERTS * (MOE_BLK - 1)) // MOE_BLK + 1
    tables = _expert_tables(counts, pad_start, pad_end, nb)
    dest = _dest_call(eidx, rank, pad_start)
    dest = jnp.swapaxes(dest, 1, 2).reshape(dest.shape[0], 1, -1)
    wts = jnp.swapaxes(wts, 1, 2)
    xs = _dispatch_call(dest, h2s, nb * MOE_BLK)
    ob = _expert_call(tables, xs, p["exp_gate"], p["exp_up"], p["exp_down"])
    return ob, dest, wts


def kernel(x_prompt, x_sample, c_prompt, c_sample, state_rwkv, state_shift, state_gla, ada_w, ada_b, norm1_g,
           norm2_g, w_in, mu_shift, rw_w0, rw_w_up, rw_a0, rw_a_up, rw_g_up, rw_k_k, rw_k_a, rw_r_k, rw_gn_g,
           rw_gn_b, gla_a_up, gla_a_bias, gla_norm_g, w_pa, w_pb, w_out, router_w, router_b, exp_gate, exp_up,
           exp_down, sh_gate, sh_up, sh_down, final_g):
    depth = ada_w.shape[0]
    bp, bs = x_prompt.shape[0], x_sample.shape[0]
    tp = bp * x_prompt.shape[1]
    tn = tp + bs * x_sample.shape[1]
    xs_g = [x_prompt, x_sample]
    c_all = jnp.concatenate([c_prompt, c_sample], axis=0)
    zeros = lambda shape: jnp.zeros(shape, x_prompt.dtype)
    new_states = [[], []]
    fg = final_g.reshape(1, 1, D_MODEL)
    for l in range(depth):
        p = _layer_params(l, ada_w, ada_b, norm1_g, norm2_g, w_in, mu_shift, rw_w0, rw_w_up, rw_a0, rw_a_up,
                          rw_g_up, rw_k_k, rw_k_a, rw_r_k, rw_gn_g, rw_gn_b, gla_a_up, gla_a_bias, gla_norm_g,
                          w_pa, w_pb, w_out, router_w, router_b, exp_gate, exp_up, exp_down, sh_gate, sh_up,
                          sh_down)
        p["final_g"] = fg
        mod_all = _mod_call(c_all, p["ada_w"], p["ada_b"])
        mods = [mod_all[:bp], mod_all[bp:]]
        states_in = [
            (zeros((bp, RW_HEADS, RW_HEAD, RW_HEAD)), zeros((bp, RW_SHIFT_COLS)),
             zeros((bp, GLA_HEADS, GLA_DK, GLA_DV))),
            (state_rwkv[l], state_shift[l], state_gla[l]),
        ]
        x1s, shared = [], None
        firsts = [0, tp]
        for gi in range(2):
            x1, h2_all, lg_all, st = _mixer_group(xs_g[gi], mods[gi], *states_in[gi], p, tn, firsts[gi], shared)
            shared = (h2_all, lg_all)
            x1s.append(x1)
            new_states[gi].append(st)
        ob, dest, wts = _moe(*shared, p)
        assert depth == 1, "the fused final norm assumes a single layer"
        xs_g = [_combine_call(dest, wts, firsts[gi], ob, shared[0], x1s[gi], mods[gi], p) for gi in range(2)]
    stack = lambda gi, j: new_states[gi][0][j][None] if depth == 1 else jnp.stack([s[j] for s in new_states[gi]])
    return (xs_g[0], xs_g[1], stack(0, 0), stack(0, 1), stack(0, 2), stack(1, 0), stack(1, 1), stack(1, 2))
```
